```python
import math
import jax, jax.numpy as jnp
from jax import lax
import numpy as np

D_MODEL = 1024
BATCH = 8
SEQ = 8192
DEPTH = 4

N_META = 16
BLOCK = 128
WINDOW = 128
ATT_HEADS = 8
ATT_KV_HEADS = 2
HEAD_DIM = 64
SSM_HEADS = 16
SSM_HEAD_DIM = 64
SSM_INNER = SSM_HEADS * SSM_HEAD_DIM
SSM_GROUPS = 2
SSM_STATE = 64
SSM_CONV = 4
CONF_DIM = D_MODEL
CONF_KERNEL = 31
D_FF = 4 * D_MODEL
EPS = 1e-6
LN_EPS = 1e-5

Q_W = ATT_HEADS * HEAD_DIM
KV_W = ATT_KV_HEADS * HEAD_DIM
BC_W = SSM_GROUPS * SSM_STATE
XBC_W = SSM_INNER + 2 * BC_W
S_Q = Q_W
S_K = S_Q + KV_W
S_V = S_K + KV_W
S_Z = S_V + SSM_INNER
S_XBC = S_Z + XBC_W
IN_W = S_XBC + SSM_HEADS
MIX_W = Q_W + SSM_INNER
N_EVEN = (DEPTH + 1) // 2
N_ODD = DEPTH // 2
FRONT_PAD = BLOCK - N_META

kernel_name = "hybrid_swa_ssd_conformer_trunk"


def rms_norm(x, g):
    xf = x.astype(jnp.float32)
    y = xf * lax.rsqrt(jnp.mean(xf * xf, -1, keepdims=True) + EPS)
    return (y * g.astype(jnp.float32)).astype(x.dtype)


def layer_norm(x, g, b):
    xf = x.astype(jnp.float32)
    mu = jnp.mean(xf, -1, keepdims=True)
    var = jnp.mean(jnp.square(xf - mu), -1, keepdims=True)
    y = (xf - mu) * lax.rsqrt(var + LN_EPS)
    return (y * g.astype(jnp.float32) + b.astype(jnp.float32)).astype(x.dtype)


def causal_depthwise_conv(x, w, b):
    k = w.shape[0]
    y = lax.conv_general_dilated(x, w[:, None, :].astype(x.dtype), window_strides=(1,),
                                 padding=((k - 1, 0),), dimension_numbers=('NWC', 'WIO', 'NWC'),
                                 feature_group_count=x.shape[-1])
    return y + b.astype(x.dtype)


def front_pad(a, pad):
    return jnp.pad(a, [(0, 0), (pad, 0)] + [(0, 0)] * (a.ndim - 2))


def alibi_slopes(n):
    return jnp.exp2(-8.0 * jnp.arange(1, n + 1, dtype=jnp.float32) / n)


def sliding_window_attention(q, k, v, sinks):
    bsz = q.shape[0]
    grp = ATT_HEADS // ATT_KV_HEADS
    qb = front_pad(q, FRONT_PAD).reshape(bsz, -1, BLOCK, ATT_KV_HEADS, grp, HEAD_DIM)
    kb = front_pad(k, FRONT_PAD).reshape(bsz, -1, BLOCK, ATT_KV_HEADS, HEAD_DIM)
    vb = front_pad(v, FRONT_PAD).reshape(bsz, -1, BLOCK, ATT_KV_HEADS, HEAD_DIM)
    nb = kb.shape[1]
    shift = ((0, 0), (1, 0), (0, 0), (0, 0), (0, 0))
    kprev = jnp.pad(kb, shift)[:, :-1]
    vprev = jnp.pad(vb, shift)[:, :-1]
    kmeta = jnp.broadcast_to(k[:, None, :N_META], (bsz, nb, N_META, ATT_KV_HEADS, HEAD_DIM))
    vmeta = jnp.broadcast_to(v[:, None, :N_META], (bsz, nb, N_META, ATT_KV_HEADS, HEAD_DIM))
    keys = jnp.concatenate([kmeta, kprev, kb], axis=2)
    vals = jnp.concatenate([vmeta, vprev, vb], axis=2)
    nk = N_META + 2 * BLOCK
    q_pos = jnp.arange(nb)[:, None] * BLOCK + jnp.arange(BLOCK)[None, :] - FRONT_PAD
    meta_pos = jnp.broadcast_to(jnp.arange(N_META)[None, :], (nb, N_META))
    k_pos = jnp.concatenate([meta_pos, q_pos - BLOCK, q_pos], axis=1)
    is_meta = (jnp.arange(nk) < N_META)[None, None, :]
    dist = q_pos[:, :, None] - k_pos[:, None, :]
    valid = jnp.where(is_meta, dist >= 0,
                      (k_pos[:, None, :] >= N_META) & (dist >= 0) & (dist < WINDOW))
    pen_dist = jnp.where(is_meta, jnp.minimum(jnp.abs(dist), WINDOW), jnp.abs(dist)).astype(jnp.float32)
    slopes = alibi_slopes(ATT_HEADS).reshape(ATT_KV_HEADS, grp)
    bias = -slopes[None, :, :, None, None] * pen_dist[:, None, None]
    s = jnp.einsum('bnqkgd,bnskd->bnkgqs', qb, keys).astype(jnp.float32) * (HEAD_DIM ** -0.5)
    s = jnp.where(valid[:, None, None], s + bias, -1e30)
    sink = jnp.broadcast_to(sinks.astype(jnp.float32).reshape(ATT_KV_HEADS, grp)[None, None, :, :, None, None],
                            s.shape[:-1] + (1,))
    p = jax.nn.softmax(jnp.concatenate([s, sink], axis=-1), axis=-1)[..., :-1]
    o = jnp.einsum('bnkgqs,bnskd->bnqkgd', p.astype(vals.dtype), vals)
    return o.reshape(bsz, nb * BLOCK, Q_W)[:, FRONT_PAD:]


def segsum(a):
    cs = jnp.cumsum(a, axis=-1)
    d = cs[..., :, None] - cs[..., None, :]
    t = a.shape[-1]
    return jnp.where(jnp.tril(jnp.ones((t, t), bool)), d, -jnp.inf)


def ssd_scan(x, dt, a, b_mat, c_mat):
    out_dtype = x.dtype
    f32 = jnp.float32
    bsz, lp = x.shape[:2]
    nc = lp // BLOCK
    hpg = SSM_HEADS // SSM_GROUPS
    xc = x.astype(f32).reshape(bsz, nc, BLOCK, SSM_GROUPS, hpg, SSM_HEAD_DIM)
    dtc = dt.astype(f32).reshape(bsz, nc, BLOCK, SSM_GROUPS, hpg)
    bc = b_mat.astype(f32).reshape(bsz, nc, BLOCK, SSM_GROUPS, SSM_STATE)
    cc = c_mat.astype(f32).reshape(bsz, nc, BLOCK, SSM_GROUPS, SSM_STATE)
    dt_t = jnp.moveaxis(dtc, 2, -1)
    adt = dt_t * a.astype(f32).reshape(SSM_GROUPS, hpg)[None, None, :, :, None]
    a_cs = jnp.cumsum(adt, axis=-1)
    cb = jnp.einsum('bclgn,bcsgn->bcgls', cc, bc)
    w = cb[:, :, :, None] * jnp.exp(segsum(adt)) * dt_t[:, :, :, :, None, :]
    y_diag = jnp.einsum('bcghls,bcsghp->bclghp', w, xc)
    wx = jnp.exp(a_cs[..., -1:] - a_cs) * dt_t
    states = jnp.einsum('bclgn,bcghl,bclghp->bcghpn', bc, wx, xc)
    chunk_decay = jnp.exp(a_cs[..., -1])

    def step(h, inp):
        st, dec = inp
        return h * dec[..., None, None] + st, h

    h0 = jnp.zeros((bsz, SSM_GROUPS, hpg, SSM_HEAD_DIM, SSM_STATE), f32)
    _, prev = lax.scan(step, h0, (jnp.moveaxis(states, 1, 0), jnp.moveaxis(chunk_decay, 1, 0)))
    prev = jnp.moveaxis(prev, 0, 1)
    y_off = jnp.einsum('bclgn,bcghpn,bcghl->bclghp', cc, prev, jnp.exp(a_cs))
    y = (y_diag + y_off).reshape(bsz, lp, SSM_HEADS, SSM_HEAD_DIM)
    return y.astype(out_dtype)


def mamba2_branch(z, xbc, dt_raw, conv_w, conv_b, dt_bias, a_log, d_skip, norm_w):
    bsz, seqlen = z.shape[:2]
    xbc = jax.nn.silu(causal_depthwise_conv(xbc, conv_w, conv_b))
    xs, bm, cm = jnp.split(xbc, [SSM_INNER, SSM_INNER + BC_W], axis=-1)
    xs = xs.reshape(bsz, seqlen, SSM_HEADS, SSM_HEAD_DIM)
    bm = bm.reshape(bsz, seqlen, SSM_GROUPS, SSM_STATE)
    cm = cm.reshape(bsz, seqlen, SSM_GROUPS, SSM_STATE)
    dt = jax.nn.softplus(dt_raw + dt_bias.astype(dt_raw.dtype))
    a = -jnp.exp(a_log.astype(jnp.float32))
    y = ssd_scan(front_pad(xs, FRONT_PAD), front_pad(dt, FRONT_PAD), a,
                 front_pad(bm, FRONT_PAD), front_pad(cm, FRONT_PAD))[:, FRONT_PAD:]
    y = y + xs * d_skip.astype(xs.dtype)[:, None]
    y = y.reshape(bsz, seqlen, SSM_INNER) * jax.nn.silu(z)
    yg = y.astype(jnp.float32).reshape(bsz, seqlen, SSM_GROUPS, SSM_INNER // SSM_GROUPS)
    yg = yg * lax.rsqrt(jnp.mean(yg * yg, -1, keepdims=True) + EPS)
    return (yg.reshape(bsz, seqlen, SSM_INNER) * norm_w.astype(jnp.float32)).astype(z.dtype)


def even_mixer(h, norm_g, w_in, conv_w, conv_b, dt_bias, a_log, d_skip, ssm_norm_w, q_norm, k_norm, sinks, w_out):
    bsz, seqlen = h.shape[:2]
    u = rms_norm(h, norm_g)
    proj = u @ w_in
    q, k, v, z, xbc, dt_raw = jnp.split(proj, [S_Q, S_K, S_V, S_Z, S_XBC], axis=-1)
    q = rms_norm(q.reshape(bsz, seqlen, ATT_HEADS, HEAD_DIM), q_norm)
    k = rms_norm(k.reshape(bsz, seqlen, ATT_KV_HEADS, HEAD_DIM), k_norm)
    v = v.reshape(bsz, seqlen, ATT_KV_HEADS, HEAD_DIM)
    att = sliding_window_attention(q, k, v, sinks)
    ssm = mamba2_branch(z, xbc, dt_raw, conv_w, conv_b, dt_bias, a_log, d_skip, ssm_norm_w)
    return jnp.concatenate([att, ssm], axis=-1) @ w_out


def conformer_conv_module(h, norm_g, pw1_w, pw1_b, dw_w, dw_b, ln_g, ln_b, pw2_w, pw2_b):
    u = rms_norm(h, norm_g) @ pw1_w + pw1_b
    u = u[..., :CONF_DIM] * jax.nn.sigmoid(u[..., CONF_DIM:])
    u = causal_depthwise_conv(u, dw_w, dw_b)
    u = jax.nn.silu(layer_norm(u, ln_g, ln_b))
    return u @ pw2_w + pw2_b


def sq_relu_mlp(h, norm_g, w_up, w_down):
    return jnp.square(jax.nn.relu(rms_norm(h, norm_g) @ w_up)) @ w_down


def _fwd_setup_inputs(seed: int = 0) -> dict:
    key = jax.random.key(seed)
    ks = iter(jax.random.split(key, 40))

    def nrm(shape, scale):
        return scale * jax.random.normal(next(ks), shape, jnp.float32)

    def gain(shape):
        return 1.0 + nrm(shape, 0.02)

    x = nrm((BATCH, SEQ, D_MODEL), 1.0)
    meta_tokens = nrm((N_META, D_MODEL), 1.0)
    mix_norm_even = gain((N_EVEN, D_MODEL))
    w_in = nrm((N_EVEN, D_MODEL, IN_W), D_MODEL ** -0.5)
    ssm_conv_w = nrm((N_EVEN, SSM_CONV, XBC_W), SSM_CONV ** -0.5)
    ssm_conv_b = nrm((N_EVEN, XBC_W), 0.02)
    dt0 = jnp.exp(jax.random.uniform(next(ks), (N_EVEN, SSM_HEADS), jnp.float32,
                                     minval=math.log(1e-3), maxval=math.log(1e-1)))
    dt_bias = dt0 + jnp.log(-jnp.expm1(-dt0))
    a_log = jnp.log(jax.random.uniform(next(ks), (N_EVEN, SSM_HEADS), jnp.float32, minval=1.0, maxval=16.0))
    d_skip = gain((N_EVEN, SSM_HEADS))
    ssm_norm_w = gain((N_EVEN, SSM_INNER))
    q_norm = gain((N_EVEN, HEAD_DIM))
    k_norm = gain((N_EVEN, HEAD_DIM))
    sinks = nrm((N_EVEN, ATT_HEADS), 0.5)
    w_out = nrm((N_EVEN, MIX_W, D_MODEL), MIX_W ** -0.5)
    mix_norm_odd = gain((N_ODD, D_MODEL))
    pw1_w = nrm((N_ODD, D_MODEL, 2 * CONF_DIM), D_MODEL ** -0.5)
    pw1_b = nrm((N_ODD, 2 * CONF_DIM), 0.02)
    dw_w = nrm((N_ODD, CONF_KERNEL, CONF_DIM), CONF_KERNEL ** -0.5)
    dw_b = nrm((N_ODD, CONF_DIM), 0.02)
    ln_g = gain((N_ODD, CONF_DIM))
    ln_b = nrm((N_ODD, CONF_DIM), 0.02)
    pw2_w = nrm((N_ODD, CONF_DIM, D_MODEL), CONF_DIM ** -0.5)
    pw2_b = nrm((N_ODD, D_MODEL), 0.02)
    mlp_norm = gain((DEPTH, D_MODEL))
    w_up = nrm((DEPTH, D_MODEL, D_FF), D_MODEL ** -0.5)
    w_down = nrm((DEPTH, D_FF, D_MODEL), D_FF ** -0.5)
    return {"x": x, "meta_tokens": meta_tokens, "mix_norm_even": mix_norm_even, "w_in": w_in,
            "ssm_conv_w": ssm_conv_w, "ssm_conv_b": ssm_conv_b, "dt_bias": dt_bias, "a_log": a_log,
            "d_skip": d_skip, "ssm_norm_w": ssm_norm_w, "q_norm": q_norm, "k_norm": k_norm,
            "sinks": sinks, "w_out": w_out, "mix_norm_odd": mix_norm_odd, "pw1_w": pw1_w,
            "pw1_b": pw1_b, "dw_w": dw_w, "dw_b": dw_b, "ln_g": ln_g, "ln_b": ln_b,
            "pw2_w": pw2_w, "pw2_b": pw2_b, "mlp_norm": mlp_norm, "w_up": w_up, "w_down": w_down}


def _fwd_reference(x, meta_tokens, mix_norm_even, w_in, ssm_conv_w, ssm_conv_b, dt_bias, a_log, d_skip,
              ssm_norm_w, q_norm, k_norm, sinks, w_out, mix_norm_odd, pw1_w, pw1_b, dw_w, dw_b,
              ln_g, ln_b, pw2_w, pw2_b, mlp_norm, w_up, w_down):
    bsz = x.shape[0]
    meta = jnp.broadcast_to(meta_tokens[None].astype(x.dtype), (bsz, N_META, D_MODEL))
    h = jnp.concatenate([meta, x], axis=1)
    for layer in range(DEPTH):
        i = layer // 2
        if layer % 2 == 0:
            h = h + even_mixer(h, mix_norm_even[i], w_in[i], ssm_conv_w[i], ssm_conv_b[i], dt_bias[i],
                               a_log[i], d_skip[i], ssm_norm_w[i], q_norm[i], k_norm[i], sinks[i], w_out[i])
        else:
            h = h + conformer_conv_module(h, mix_norm_odd[i], pw1_w[i], pw1_b[i], dw_w[i], dw_b[i],
                                          ln_g[i], ln_b[i], pw2_w[i], pw2_b[i])
        h = h + sq_relu_mlp(h, mlp_norm[layer], w_up[layer], w_down[layer])
    return h[:, N_META:]


import jax as _jax
import jax.numpy as _jnp

TWIN_FORMAT = 'train_step'
FWD_PARAMS = ['x', 'meta_tokens', 'mix_norm_even', 'w_in', 'ssm_conv_w', 'ssm_conv_b', 'dt_bias', 'a_log', 'd_skip', 'ssm_norm_w', 'q_norm', 'k_norm', 'sinks', 'w_out', 'mix_norm_odd', 'pw1_w', 'pw1_b', 'dw_w', 'dw_b', 'ln_g', 'ln_b', 'pw2_w', 'pw2_b', 'mlp_norm', 'w_up', 'w_down']
TWIN_WEIGHTS = ['meta_tokens', 'mix_norm_even', 'w_in', 'ssm_conv_w', 'ssm_conv_b', 'dt_bias', 'a_log', 'd_skip', 'ssm_norm_w', 'q_norm', 'k_norm', 'sinks', 'w_out', 'mix_norm_odd', 'pw1_w', 'pw1_b', 'dw_w', 'dw_b', 'ln_g', 'ln_b', 'pw2_w', 'pw2_b', 'mlp_norm', 'w_up', 'w_down']
TWIN_DIFF_INPUT = 'x'
TWIN_INPUTS = ['x', 'meta_tokens', 'mix_norm_even', 'w_in', 'ssm_conv_w', 'ssm_conv_b', 'dt_bias', 'a_log', 'd_skip', 'ssm_norm_w', 'q_norm', 'k_norm', 'sinks', 'w_out', 'mix_norm_odd', 'pw1_w', 'pw1_b', 'dw_w', 'dw_b', 'ln_g', 'ln_b', 'pw2_w', 'pw2_b', 'mlp_norm', 'w_up', 'w_down', 'loss_target', 'm_meta_tokens', 'm_mix_norm_even', 'm_w_in', 'm_ssm_conv_w', 'm_ssm_conv_b', 'm_dt_bias', 'm_a_log', 'm_d_skip', 'm_ssm_norm_w', 'm_q_norm', 'm_k_norm', 'm_sinks', 'm_w_out', 'm_mix_norm_odd', 'm_pw1_w', 'm_pw1_b', 'm_dw_w', 'm_dw_b', 'm_ln_g', 'm_ln_b', 'm_pw2_w', 'm_pw2_b', 'm_mlp_norm', 'm_w_up', 'm_w_down', 'v_meta_tokens', 'v_mix_norm_even', 'v_w_in', 'v_ssm_conv_w', 'v_ssm_conv_b', 'v_dt_bias', 'v_a_log', 'v_d_skip', 'v_ssm_norm_w', 'v_q_norm', 'v_k_norm', 'v_sinks', 'v_w_out', 'v_mix_norm_odd', 'v_pw1_w', 'v_pw1_b', 'v_dw_w', 'v_dw_b', 'v_ln_g', 'v_ln_b', 'v_pw2_w', 'v_pw2_b', 'v_mlp_norm', 'v_w_up', 'v_w_down']
TWIN_OUTPUTS = ['loss', 'grad_x', 'grad_meta_tokens', 'grad_mix_norm_even', 'grad_w_in', 'grad_ssm_conv_w', 'grad_ssm_conv_b', 'grad_dt_bias', 'grad_a_log', 'grad_d_skip', 'grad_ssm_norm_w', 'grad_q_norm', 'grad_k_norm', 'grad_sinks', 'grad_w_out', 'grad_mix_norm_odd', 'grad_pw1_w', 'grad_pw1_b', 'grad_dw_w', 'grad_dw_b', 'grad_ln_g', 'grad_ln_b', 'grad_pw2_w', 'grad_pw2_b', 'grad_mlp_norm', 'grad_w_up', 'grad_w_down', 'delta_meta_tokens', 'delta_mix_norm_even', 'delta_w_in', 'delta_ssm_conv_w', 'delta_ssm_conv_b', 'delta_dt_bias', 'delta_a_log', 'delta_d_skip', 'delta_ssm_norm_w', 'delta_q_norm', 'delta_k_norm', 'delta_sinks', 'delta_w_out', 'delta_mix_norm_odd', 'delta_pw1_w', 'delta_pw1_b', 'delta_dw_w', 'delta_dw_b', 'delta_ln_g', 'delta_ln_b', 'delta_pw2_w', 'delta_pw2_b', 'delta_mlp_norm', 'delta_w_up', 'delta_w_down', 'new_m_meta_tokens', 'new_m_mix_norm_even', 'new_m_w_in', 'new_m_ssm_conv_w', 'new_m_ssm_conv_b', 'new_m_dt_bias', 'new_m_a_log', 'new_m_d_skip', 'new_m_ssm_norm_w', 'new_m_q_norm', 'new_m_k_norm', 'new_m_sinks', 'new_m_w_out', 'new_m_mix_norm_odd', 'new_m_pw1_w', 'new_m_pw1_b', 'new_m_dw_w', 'new_m_dw_b', 'new_m_ln_g', 'new_m_ln_b', 'new_m_pw2_w', 'new_m_pw2_b', 'new_m_mlp_norm', 'new_m_w_up', 'new_m_w_down', 'new_v_meta_tokens', 'new_v_mix_norm_even', 'new_v_w_in', 'new_v_ssm_conv_w', 'new_v_ssm_conv_b', 'new_v_dt_bias', 'new_v_a_log', 'new_v_d_skip', 'new_v_ssm_norm_w', 'new_v_q_norm', 'new_v_k_norm', 'new_v_sinks', 'new_v_w_out', 'new_v_mix_norm_odd', 'new_v_pw1_w', 'new_v_pw1_b', 'new_v_dw_w', 'new_v_dw_b', 'new_v_ln_g', 'new_v_ln_b', 'new_v_pw2_w', 'new_v_pw2_b', 'new_v_mlp_norm', 'new_v_w_up', 'new_v_w_down']
TWIN_LEAF_KINDS = {'loss': 'loss', 'grad_x': 'grad_x', 'grad_meta_tokens': 'grad_w', 'grad_mix_norm_even': 'grad_w', 'grad_w_in': 'grad_w', 'grad_ssm_conv_w': 'grad_w', 'grad_ssm_conv_b': 'grad_w', 'grad_dt_bias': 'grad_w', 'grad_a_log': 'grad_w', 'grad_d_skip': 'grad_w', 'grad_ssm_norm_w': 'grad_w', 'grad_q_norm': 'grad_w', 'grad_k_norm': 'grad_w', 'grad_sinks': 'grad_w', 'grad_w_out': 'grad_w', 'grad_mix_norm_odd': 'grad_w', 'grad_pw1_w': 'grad_w', 'grad_pw1_b': 'grad_w', 'grad_dw_w': 'grad_w', 'grad_dw_b': 'grad_w', 'grad_ln_g': 'grad_w', 'grad_ln_b': 'grad_w', 'grad_pw2_w': 'grad_w', 'grad_pw2_b': 'grad_w', 'grad_mlp_norm': 'grad_w', 'grad_w_up': 'grad_w', 'grad_w_down': 'grad_w', 'delta_meta_tokens': 'delta_w', 'delta_mix_norm_even': 'delta_w', 'delta_w_in': 'delta_w', 'delta_ssm_conv_w': 'delta_w', 'delta_ssm_conv_b': 'delta_w', 'delta_dt_bias': 'delta_w', 'delta_a_log': 'delta_w', 'delta_d_skip': 'delta_w', 'delta_ssm_norm_w': 'delta_w', 'delta_q_norm': 'delta_w', 'delta_k_norm': 'delta_w', 'delta_sinks': 'delta_w', 'delta_w_out': 'delta_w', 'delta_mix_norm_odd': 'delta_w', 'delta_pw1_w': 'delta_w', 'delta_pw1_b': 'delta_w', 'delta_dw_w': 'delta_w', 'delta_dw_b': 'delta_w', 'delta_ln_g': 'delta_w', 'delta_ln_b': 'delta_w', 'delta_pw2_w': 'delta_w', 'delta_pw2_b': 'delta_w', 'delta_mlp_norm': 'delta_w', 'delta_w_up': 'delta_w', 'delta_w_down': 'delta_w', 'new_m_meta_tokens': 'new_m', 'new_m_mix_norm_even': 'new_m', 'new_m_w_in': 'new_m', 'new_m_ssm_conv_w': 'new_m', 'new_m_ssm_conv_b': 'new_m', 'new_m_dt_bias': 'new_m', 'new_m_a_log': 'new_m', 'new_m_d_skip': 'new_m', 'new_m_ssm_norm_w': 'new_m', 'new_m_q_norm': 'new_m', 'new_m_k_norm': 'new_m', 'new_m_sinks': 'new_m', 'new_m_w_out': 'new_m', 'new_m_mix_norm_odd': 'new_m', 'new_m_pw1_w': 'new_m', 'new_m_pw1_b': 'new_m', 'new_m_dw_w': 'new_m', 'new_m_dw_b': 'new_m', 'new_m_ln_g': 'new_m', 'new_m_ln_b': 'new_m', 'new_m_pw2_w': 'new_m', 'new_m_pw2_b': 'new_m', 'new_m_mlp_norm': 'new_m', 'new_m_w_up': 'new_m', 'new_m_w_down': 'new_m', 'new_v_meta_tokens': 'new_v', 'new_v_mix_norm_even': 'new_v', 'new_v_w_in': 'new_v', 'new_v_ssm_conv_w': 'new_v', 'new_v_ssm_conv_b': 'new_v', 'new_v_dt_bias': 'new_v', 'new_v_a_log': 'new_v', 'new_v_d_skip': 'new_v', 'new_v_ssm_norm_w': 'new_v', 'new_v_q_norm': 'new_v', 'new_v_k_norm': 'new_v', 'new_v_sinks': 'new_v', 'new_v_w_out': 'new_v', 'new_v_mix_norm_odd': 'new_v', 'new_v_pw1_w': 'new_v', 'new_v_pw1_b': 'new_v', 'new_v_dw_w': 'new_v', 'new_v_dw_b': 'new_v', 'new_v_ln_g': 'new_v', 'new_v_ln_b': 'new_v', 'new_v_pw2_w': 'new_v', 'new_v_pw2_b': 'new_v', 'new_v_mlp_norm': 'new_v', 'new_v_w_up': 'new_v', 'new_v_w_down': 'new_v'}


def _forward(args):
    return _fwd_reference(*[args[k] for k in FWD_PARAMS])


def _output_shape():
    def fwd():
        inp = _fwd_setup_inputs(0)
        return _fwd_reference(*[inp[k] for k in FWD_PARAMS])
    out = _jax.eval_shape(fwd)
    return out.shape, out.dtype

N_MICROBATCH = 1
ADAM_LR = 0.001
ADAM_B1 = 0.9
ADAM_B2 = 0.999
ADAM_EPS = 1e-08
ADAM_WD = 0.01
ADAM_STEP = 10
PER_EXAMPLE_BATCH_AXIS = {'x': 0, 'loss_target': 0}
SHARED_INPUTS = []
_WEIGHT_DTYPES = {'meta_tokens': _jnp.float32, 'mix_norm_even': _jnp.float32, 'w_in': _jnp.float32, 'ssm_conv_w': _jnp.float32, 'ssm_conv_b': _jnp.float32, 'dt_bias': _jnp.float32, 'a_log': _jnp.float32, 'd_skip': _jnp.float32, 'ssm_norm_w': _jnp.float32, 'q_norm': _jnp.float32, 'k_norm': _jnp.float32, 'sinks': _jnp.float32, 'w_out': _jnp.float32, 'mix_norm_odd': _jnp.float32, 'pw1_w': _jnp.float32, 'pw1_b': _jnp.float32, 'dw_w': _jnp.float32, 'dw_b': _jnp.float32, 'ln_g': _jnp.float32, 'ln_b': _jnp.float32, 'pw2_w': _jnp.float32, 'pw2_b': _jnp.float32, 'mlp_norm': _jnp.float32, 'w_up': _jnp.float32, 'w_down': _jnp.float32}
MOMENT_SCALE = {'meta_tokens': 2.263409e-01, 'mix_norm_even': 1.824998e+01, 'w_in': 1.009547e+01, 'ssm_conv_w': 1.306751e+01, 'ssm_conv_b': 3.798821e+01, 'dt_bias': 2.773873e+00, 'a_log': 4.376191e+01, 'd_skip': 9.993505e+01, 'ssm_norm_w': 7.171650e+01, 'q_norm': 1.013463e+01, 'k_norm': 1.015752e+01, 'sinks': 5.004161e+01, 'w_out': 2.449480e+01, 'mix_norm_odd': 1.872491e+01, 'pw1_w': 1.311436e+01, 'pw1_b': 4.042048e+01, 'dw_w': 1.905005e+01, 'dw_b': 1.027362e+02, 'ln_g': 5.505193e+01, 'ln_b': 5.859153e+01, 'pw2_w': 2.832742e+01, 'pw2_b': 1.072465e+02, 'mlp_norm': 1.998221e+02, 'w_up': 1.599106e+01, 'w_down': 5.677160e+01}


def _to_microbatches(a, axis):
    t = _jnp.moveaxis(a, axis, 0)
    t = t.reshape((N_MICROBATCH, t.shape[0] // N_MICROBATCH) + t.shape[1:])
    return _jnp.moveaxis(t, 1, axis + 1)


def setup_inputs(seed: int = 0) -> dict:
    inp = _fwd_setup_inputs(seed)
    key = _jax.random.fold_in(_jax.random.key(seed), 7919)
    shape, _ = _output_shape()
    out = dict(inp)
    out["loss_target"] = _jax.random.normal(_jax.random.fold_in(key, 0), shape, _jnp.float32)
    for i, name in enumerate(TWIN_WEIGHTS):
        w = inp[name].astype(_jnp.float32)
        if MOMENT_SCALE is None:
            s = _jnp.sqrt(_jnp.mean(_jnp.square(w)) + 1e-30)
        else:
            s = MOMENT_SCALE[name]
        km, kv = _jax.random.split(_jax.random.fold_in(key, i + 1))
        out[name] = w
        out["m_" + name] = s * _jax.random.normal(km, w.shape, _jnp.float32)
        out["v_" + name] = (s * s) * _jax.random.uniform(kv, w.shape, _jnp.float32, 0.5, 1.5)
    if N_MICROBATCH > 1:
        for name, axis in PER_EXAMPLE_BATCH_AXIS.items():
            out[name] = _to_microbatches(out[name], axis)
    return {'x': out['x'], 'meta_tokens': out['meta_tokens'], 'mix_norm_even': out['mix_norm_even'], 'w_in': out['w_in'], 'ssm_conv_w': out['ssm_conv_w'], 'ssm_conv_b': out['ssm_conv_b'], 'dt_bias': out['dt_bias'], 'a_log': out['a_log'], 'd_skip': out['d_skip'], 'ssm_norm_w': out['ssm_norm_w'], 'q_norm': out['q_norm'], 'k_norm': out['k_norm'], 'sinks': out['sinks'], 'w_out': out['w_out'], 'mix_norm_odd': out['mix_norm_odd'], 'pw1_w': out['pw1_w'], 'pw1_b': out['pw1_b'], 'dw_w': out['dw_w'], 'dw_b': out['dw_b'], 'ln_g': out['ln_g'], 'ln_b': out['ln_b'], 'pw2_w': out['pw2_w'], 'pw2_b': out['pw2_b'], 'mlp_norm': out['mlp_norm'], 'w_up': out['w_up'], 'w_down': out['w_down'], 'loss_target': out['loss_target'], 'm_meta_tokens': out['m_meta_tokens'], 'm_mix_norm_even': out['m_mix_norm_even'], 'm_w_in': out['m_w_in'], 'm_ssm_conv_w': out['m_ssm_conv_w'], 'm_ssm_conv_b': out['m_ssm_conv_b'], 'm_dt_bias': out['m_dt_bias'], 'm_a_log': out['m_a_log'], 'm_d_skip': out['m_d_skip'], 'm_ssm_norm_w': out['m_ssm_norm_w'], 'm_q_norm': out['m_q_norm'], 'm_k_norm': out['m_k_norm'], 'm_sinks': out['m_sinks'], 'm_w_out': out['m_w_out'], 'm_mix_norm_odd': out['m_mix_norm_odd'], 'm_pw1_w': out['m_pw1_w'], 'm_pw1_b': out['m_pw1_b'], 'm_dw_w': out['m_dw_w'], 'm_dw_b': out['m_dw_b'], 'm_ln_g': out['m_ln_g'], 'm_ln_b': out['m_ln_b'], 'm_pw2_w': out['m_pw2_w'], 'm_pw2_b': out['m_pw2_b'], 'm_mlp_norm': out['m_mlp_norm'], 'm_w_up': out['m_w_up'], 'm_w_down': out['m_w_down'], 'v_meta_tokens': out['v_meta_tokens'], 'v_mix_norm_even': out['v_mix_norm_even'], 'v_w_in': out['v_w_in'], 'v_ssm_conv_w': out['v_ssm_conv_w'], 'v_ssm_conv_b': out['v_ssm_conv_b'], 'v_dt_bias': out['v_dt_bias'], 'v_a_log': out['v_a_log'], 'v_d_skip': out['v_d_skip'], 'v_ssm_norm_w': out['v_ssm_norm_w'], 'v_q_norm': out['v_q_norm'], 'v_k_norm': out['v_k_norm'], 'v_sinks': out['v_sinks'], 'v_w_out': out['v_w_out'], 'v_mix_norm_odd': out['v_mix_norm_odd'], 'v_pw1_w': out['v_pw1_w'], 'v_pw1_b': out['v_pw1_b'], 'v_dw_w': out['v_dw_w'], 'v_dw_b': out['v_dw_b'], 'v_ln_g': out['v_ln_g'], 'v_ln_b': out['v_ln_b'], 'v_pw2_w': out['v_pw2_w'], 'v_pw2_b': out['v_pw2_b'], 'v_mlp_norm': out['v_mlp_norm'], 'v_w_up': out['v_w_up'], 'v_w_down': out['v_w_down']}


def _loss(weights, diff, rest, loss_target):
    with _jax.named_scope("forward"):
        args = {**rest, TWIN_DIFF_INPUT: diff, **{k: w.astype(_WEIGHT_DTYPES[k]) for k, w in weights.items()}}
        y = _forward(args)
    with _jax.named_scope("loss_head"):
        err = _jnp.square(y.astype(_jnp.float32) - loss_target)
        return 0.5 * _jnp.sum(_jnp.mean(err, axis=-1)) if err.ndim else 0.5 * err


def _adamw(w, g, m, v):
    m = ADAM_B1 * m + (1.0 - ADAM_B1) * g
    v = ADAM_B2 * v + (1.0 - ADAM_B2) * _jnp.square(g)
    m_hat = m / (1.0 - ADAM_B1 ** ADAM_STEP)
    v_hat = v / (1.0 - ADAM_B2 ** ADAM_STEP)
    delta = -ADAM_LR * (m_hat / (_jnp.sqrt(v_hat) + ADAM_EPS) + ADAM_WD * w)
    return delta, m, v


def reference(x, meta_tokens, mix_norm_even, w_in, ssm_conv_w, ssm_conv_b, dt_bias, a_log, d_skip, ssm_norm_w, q_norm, k_norm, sinks, w_out, mix_norm_odd, pw1_w, pw1_b, dw_w, dw_b, ln_g, ln_b, pw2_w, pw2_b, mlp_norm, w_up, w_down, loss_target, m_meta_tokens, m_mix_norm_even, m_w_in, m_ssm_conv_w, m_ssm_conv_b, m_dt_bias, m_a_log, m_d_skip, m_ssm_norm_w, m_q_norm, m_k_norm, m_sinks, m_w_out, m_mix_norm_odd, m_pw1_w, m_pw1_b, m_dw_w, m_dw_b, m_ln_g, m_ln_b, m_pw2_w, m_pw2_b, m_mlp_norm, m_w_up, m_w_down, v_meta_tokens, v_mix_norm_even, v_w_in, v_ssm_conv_w, v_ssm_conv_b, v_dt_bias, v_a_log, v_d_skip, v_ssm_norm_w, v_q_norm, v_k_norm, v_sinks, v_w_out, v_mix_norm_odd, v_pw1_w, v_pw1_b, v_dw_w, v_dw_b, v_ln_g, v_ln_b, v_pw2_w, v_pw2_b, v_mlp_norm, v_w_up, v_w_down):
    given = dict(x=x, meta_tokens=meta_tokens, mix_norm_even=mix_norm_even, w_in=w_in, ssm_conv_w=ssm_conv_w, ssm_conv_b=ssm_conv_b, dt_bias=dt_bias, a_log=a_log, d_skip=d_skip, ssm_norm_w=ssm_norm_w, q_norm=q_norm, k_norm=k_norm, sinks=sinks, w_out=w_out, mix_norm_odd=mix_norm_odd, pw1_w=pw1_w, pw1_b=pw1_b, dw_w=dw_w, dw_b=dw_b, ln_g=ln_g, ln_b=ln_b, pw2_w=pw2_w, pw2_b=pw2_b, mlp_norm=mlp_norm, w_up=w_up, w_down=w_down, loss_target=loss_target, m_meta_tokens=m_meta_tokens, m_mix_norm_even=m_mix_norm_even, m_w_in=m_w_in, m_ssm_conv_w=m_ssm_conv_w, m_ssm_conv_b=m_ssm_conv_b, m_dt_bias=m_dt_bias, m_a_log=m_a_log, m_d_skip=m_d_skip, m_ssm_norm_w=m_ssm_norm_w, m_q_norm=m_q_norm, m_k_norm=m_k_norm, m_sinks=m_sinks, m_w_out=m_w_out, m_mix_norm_odd=m_mix_norm_odd, m_pw1_w=m_pw1_w, m_pw1_b=m_pw1_b, m_dw_w=m_dw_w, m_dw_b=m_dw_b, m_ln_g=m_ln_g, m_ln_b=m_ln_b, m_pw2_w=m_pw2_w, m_pw2_b=m_pw2_b, m_mlp_norm=m_mlp_norm, m_w_up=m_w_up, m_w_down=m_w_down, v_meta_tokens=v_meta_tokens, v_mix_norm_even=v_mix_norm_even, v_w_in=v_w_in, v_ssm_conv_w=v_ssm_conv_w, v_ssm_conv_b=v_ssm_conv_b, v_dt_bias=v_dt_bias, v_a_log=v_a_log, v_d_skip=v_d_skip, v_ssm_norm_w=v_ssm_norm_w, v_q_norm=v_q_norm, v_k_norm=v_k_norm, v_sinks=v_sinks, v_w_out=v_w_out, v_mix_norm_odd=v_mix_norm_odd, v_pw1_w=v_pw1_w, v_pw1_b=v_pw1_b, v_dw_w=v_dw_w, v_dw_b=v_dw_b, v_ln_g=v_ln_g, v_ln_b=v_ln_b, v_pw2_w=v_pw2_w, v_pw2_b=v_pw2_b, v_mlp_norm=v_mlp_norm, v_w_up=v_w_up, v_w_down=v_w_down)
    weights = {n: given[n] for n in TWIN_WEIGHTS}
    shared = {n: given[n] for n in SHARED_INPUTS}
    per_example = {n: given[n] for n in ['x']}
    grad_fn = _jax.value_and_grad(_loss, argnums=(0, 1))

    def one_microbatch(ex, loss_target):
        ex = dict(ex)
        diff = ex.pop(TWIN_DIFF_INPUT)
        return grad_fn(weights, diff, {**shared, **ex}, loss_target)

    if N_MICROBATCH == 1:
        loss, (grad_w, grad_x) = one_microbatch(per_example, given["loss_target"])
    else:
        def body(carry, xs):
            loss_sum, grad_sum = carry
            l_k, (gw_k, gx_k) = one_microbatch(xs[0], xs[1])
            with _jax.named_scope("update"):
                return (loss_sum + l_k, _jax.tree.map(_jnp.add, grad_sum, gw_k)), gx_k

        init = (_jnp.zeros((), _jnp.float32), _jax.tree.map(_jnp.zeros_like, weights))
        (loss, grad_w), grad_x = _jax.lax.scan(body, init, (per_example, given["loss_target"]))
    with _jax.named_scope("update"):
        delta_w, new_m, new_v = {}, {}, {}
        for n in TWIN_WEIGHTS:
            delta_w[n], new_m[n], new_v[n] = _adamw(weights[n], grad_w[n], given["m_" + n], given["v_" + n])
    return (loss, grad_x, *[grad_w[n] for n in TWIN_WEIGHTS], *[delta_w[n] for n in TWIN_WEIGHTS],
            *[new_m[n] for n in TWIN_WEIGHTS], *[new_v[n] for n in TWIN_WEIGHTS])
```

```python
import functools
import math

import jax
import jax.numpy as jnp
from jax import lax
from jax.experimental import pallas as pl
from jax.experimental.pallas import tpu as pltpu

F32 = jnp.float32
MXU_DTYPE = jnp.bfloat16
HIGHEST = lax.Precision.HIGHEST

D_MODEL = 1024
N_META = 16
BLOCK = 128
FRONT_PAD = BLOCK - N_META
ATT_HEADS = 8
ATT_KV_HEADS = 2
HEAD_DIM = 64
SSM_HEADS = 16
SSM_HEAD_DIM = 64
SSM_INNER = 1024
SSM_GROUPS = 2
SSM_STATE = 64
SSM_CONV = 4
CONF_KERNEL = 31
D_FF = 4096
EPS = 1e-6
LN_EPS = 1e-5
Q_W = 512
KV_W = 128
IN_W = 3088
MIX_W = 1536
DEPTH = 4
N_DEV = 8

ADAM_LR = 0.001
ADAM_B1 = 0.9
ADAM_B2 = 0.999
ADAM_EPS = 1e-08
ADAM_WD = 0.01
ADAM_STEP = 10

PROJ_W = 3200
COL_Z, COL_XS, COL_Q, COL_BC, COL_K, COL_V, COL_DT = 0, 1024, 2048, 2560, 2816, 2944, 3072

ROW_TILE = 640
VMEM_LIMIT = 56 * 1024 * 1024
LANE = 128
PACK_W = 1024


def _params(*sem):
    return pltpu.CompilerParams(dimension_semantics=sem, vmem_limit_bytes=VMEM_LIMIT)


def _mx(x):
    return x.astype(MXU_DTYPE)


def _dot(a, b):
    return jnp.dot(_mx(a), _mx(b), preferred_element_type=F32)


def _dot_nt(a, b):
    return lax.dot_general(_mx(a), _mx(b), (((1,), (1,)), ((), ())), preferred_element_type=F32)


def _dot_tn(a, b):
    return lax.dot_general(_mx(a), _mx(b), (((0,), (0,)), ((), ())), preferred_element_type=F32)


def _dot_hi(a, b):
    return jnp.dot(a, b, precision=HIGHEST, preferred_element_type=F32)


def _dot_tn_hi(a, b):
    return lax.dot_general(a, b, (((0,), (0,)), ((), ())), precision=HIGHEST, preferred_element_type=F32)


def _sigmoid(x):
    return 1.0 / (1.0 + jnp.exp(-x))


def _row_ids(start, rows, cols):
    return start + lax.broadcasted_iota(jnp.int32, (rows, cols), 0)


def rms_fwd(h, g, name="rms_fwd"):
    m, d = h.shape
    tm = ROW_TILE

    def body(h_ref, g_ref, u_ref):
        x = h_ref[...]
        r = lax.rsqrt(jnp.mean(x * x, -1, keepdims=True) + EPS)
        u_ref[...] = ((x * r) * g_ref[...]).astype(u_ref.dtype)

    return pl.pallas_call(
        body, grid=(m // tm,),
        in_specs=[pl.BlockSpec((tm, d), lambda i: (i, 0)), pl.BlockSpec((1, d), lambda i: (0, 0))],
        out_specs=pl.BlockSpec((tm, d), lambda i: (i, 0)),
        out_shape=jax.ShapeDtypeStruct((m, d), MXU_DTYPE), name=name, compiler_params=_params("arbitrary"),
    )(h, g)


def rms_bwd(h, g, du, dh_out, name="rms_bwd"):
    m, d = h.shape
    tm = ROW_TILE

    def body(h_ref, g_ref, du_ref, dho_ref, dh_ref, dg_ref):
        @pl.when(pl.program_id(0) == 0)
        def _():
            dg_ref[...] = jnp.zeros_like(dg_ref)

        x = h_ref[...]
        du_ = du_ref[...]
        r = lax.rsqrt(jnp.mean(x * x, -1, keepdims=True) + EPS)
        gy = du_ * g_ref[...]
        dx = r * gy - x * ((r * r * r) * jnp.mean(x * gy, -1, keepdims=True))
        dh_ref[...] = dho_ref[...] + dx
        dg_ref[...] += jnp.sum(du_ * (x * r), axis=0, keepdims=True)

    row = pl.BlockSpec((tm, d), lambda i: (i, 0))
    vec = pl.BlockSpec((1, d), lambda i: (0, 0))
    return pl.pallas_call(
        body, grid=(m // tm,), in_specs=[row, vec, row, row], out_specs=[row, vec],
        out_shape=[jax.ShapeDtypeStruct((m, d), F32), jax.ShapeDtypeStruct((1, d), F32)],
        name=name, compiler_params=_params("arbitrary"),
    )(h, g, du, dh_out)


def loss_fwd_bwd(h, target, name="loss"):
    m, d = h.shape
    nb = m // BLOCK

    def body(h_ref, t_ref, dh_ref, l_ref):
        i = pl.program_id(0)

        @pl.when(i == 0)
        def _():
            l_ref[...] = jnp.zeros_like(l_ref)
            dh_ref[...] = jnp.zeros_like(dh_ref)

        @pl.when(i > 0)
        def _():
            e = h_ref[...] - t_ref[...]
            dh_ref[...] = e * (1.0 / d)
            s = jnp.sum(jnp.sum(e * e, axis=-1, keepdims=True), axis=0, keepdims=True)
            l_ref[...] += jnp.broadcast_to(s * (0.5 / d), l_ref.shape)

    return pl.pallas_call(
        body, grid=(nb,),
        in_specs=[pl.BlockSpec((BLOCK, d), lambda i: (i, 0)),
                  pl.BlockSpec((BLOCK, d), lambda i: (jnp.maximum(i - 1, 0), 0))],
        out_specs=[pl.BlockSpec((BLOCK, d), lambda i: (i, 0)), pl.BlockSpec((1, LANE), lambda i: (0, 0))],
        out_shape=[jax.ShapeDtypeStruct((m, d), F32), jax.ShapeDtypeStruct((1, LANE), F32)],
        name=name, compiler_params=_params("arbitrary"),
    )(h, target)


def col_sum(x, name="col_sum"):
    m, n = x.shape
    tm = ROW_TILE

    def body(x_ref, o_ref):
        @pl.when(pl.program_id(0) == 0)
        def _():
            o_ref[...] = jnp.zeros_like(o_ref)

        o_ref[...] += jnp.sum(x_ref[...].astype(F32), axis=0, keepdims=True)

    return pl.pallas_call(
        body, grid=(m // tm,), in_specs=[pl.BlockSpec((tm, n), lambda i: (i, 0))],
        out_specs=pl.BlockSpec((1, n), lambda i: (0, 0)), out_shape=jax.ShapeDtypeStruct((1, n), F32),
        name=name, compiler_params=_params("arbitrary"),
    )(x)


def matmul(a, b, *, trans_b=False, tn=None, epilogue=None, bias=None, extra=None, out_dtype=F32, name="matmul"):
    m, k = a.shape
    n = b.shape[0] if trans_b else b.shape[1]
    tm = ROW_TILE
    tn = n if tn is None else tn
    has_bias = bias is not None
    has_extra = extra is not None

    def body(*refs):
        a_ref, b_ref = refs[0], refs[1]
        pos = 2
        bias_ref = extra_ref = None
        if has_bias:
            bias_ref = refs[pos]
            pos += 1
        if has_extra:
            extra_ref = refs[pos]
            pos += 1
        outs = refs[pos:]
        if trans_b:
            acc = _dot_nt(a_ref[...], b_ref[...])
        else:
            acc = _dot(a_ref[...], b_ref[...])
        if has_bias:
            acc = acc + bias_ref[...]
        if epilogue is None:
            outs[0][...] = acc.astype(outs[0].dtype)
        elif epilogue == "relu2":
            outs[0][...] = acc
            r = jnp.maximum(acc, 0.0)
            outs[1][...] = (r * r).astype(outs[1].dtype)
        elif epilogue == "drelu2":
            outs[0][...] = (acc * (2.0 * jnp.maximum(extra_ref[...], 0.0))).astype(outs[0].dtype)
        elif epilogue == "resid":
            rows = _row_ids(pl.program_id(0) * tm, tm, tn)
            outs[0][...] = extra_ref[...] + jnp.where(rows >= FRONT_PAD, acc, 0.0)
        else:
            raise ValueError(epilogue)

    in_specs = [pl.BlockSpec((tm, k), lambda i, j: (i, 0)),
                pl.BlockSpec((tn, k), lambda i, j: (j, 0)) if trans_b else pl.BlockSpec((k, tn), lambda i, j: (0, j))]
    args = [a, b]
    if has_bias:
        in_specs.append(pl.BlockSpec((1, tn), lambda i, j: (0, j)))
        args.append(bias)
    if has_extra:
        in_specs.append(pl.BlockSpec((tm, tn), lambda i, j: (i, j)))
        args.append(extra)
    tile = pl.BlockSpec((tm, tn), lambda i, j: (i, j))
    if epilogue == "relu2":
        out_specs = [tile, tile]
        out_shape = [jax.ShapeDtypeStruct((m, n), F32), jax.ShapeDtypeStruct((m, n), MXU_DTYPE)]
    else:
        out_specs = tile
        out_shape = jax.ShapeDtypeStruct((m, n), out_dtype)
    return pl.pallas_call(
        body, grid=(m // tm, n // tn), in_specs=in_specs, out_specs=out_specs, out_shape=out_shape,
        name=name, compiler_params=_params("arbitrary", "arbitrary"),
    )(*args)


def matmul_tn(x, dy, *, ti, tn, name="matmul_tn"):
    m, k1 = x.shape
    n = dy.shape[1]
    tm = ROW_TILE

    def body(x_ref, dy_ref, o_ref):
        @pl.when(pl.program_id(2) == 0)
        def _():
            o_ref[...] = jnp.zeros_like(o_ref)

        o_ref[...] += _dot_tn(x_ref[...], dy_ref[...])

    return pl.pallas_call(
        body, grid=(k1 // ti, n // tn, m // tm),
        in_specs=[pl.BlockSpec((tm, ti), lambda i, j, r: (r, i)), pl.BlockSpec((tm, tn), lambda i, j, r: (r, j))],
        out_specs=pl.BlockSpec((ti, tn), lambda i, j, r: (i, j)),
        out_shape=jax.ShapeDtypeStruct((k1, n), F32), name=name,
        compiler_params=_params("arbitrary", "arbitrary", "arbitrary"),
    )(x, dy)


_ATT_SCALE = HEAD_DIM ** -0.5


def _alibi_slope(h):
    return 2.0 ** (-8.0 * (h + 1) / ATT_HEADS)


def _head_norm(x, w, nheads):
    outs, rs = [], []
    for h in range(nheads):
        xh = x[:, HEAD_DIM * h:HEAD_DIM * (h + 1)]
        r = lax.rsqrt(jnp.mean(xh * xh, -1, keepdims=True) + EPS)
        outs.append((xh * r) * w)
        rs.append(r)
    return outs, rs


def _head_norm_bwd(x, r, w, dy):
    gy = dy * w
    dx = r * gy - x * ((r * r * r) * jnp.mean(x * gy, -1, keepdims=True))
    dw = jnp.sum(dy * (x * r), axis=0, keepdims=True)
    return dx, dw


def _att_masks(b):
    row = lax.broadcasted_iota(jnp.int32, (BLOCK, BLOCK), 0)
    col = lax.broadcasted_iota(jnp.int32, (BLOCK, BLOCK), 1)
    valid_c = jnp.logical_and(col <= row, b >= 1)
    dist_c = (row - col).astype(F32)
    valid_p = jnp.logical_and(col > row, b >= 2)
    dist_p = (row - col + BLOCK).astype(F32)
    q_pos = b * BLOCK + row - FRONT_PAD
    meta_j = col - FRONT_PAD
    valid_m = jnp.logical_and(col >= FRONT_PAD, q_pos >= meta_j)
    dist_m = jnp.minimum(q_pos - meta_j, BLOCK).astype(F32)
    return (valid_c, dist_c), (valid_p, dist_p), (valid_m, dist_m)


def _att_probs(qh, keys, masks, slope, sink):
    ss = []
    for kh, (valid, dist) in zip(keys, masks):
        s = _dot_nt(qh, kh) * _ATT_SCALE
        ss.append(jnp.where(valid, s - slope * dist, -1e30))
    mx = sink
    for s in ss:
        mx = jnp.maximum(mx, jnp.max(s, axis=-1, keepdims=True))
    ps = [jnp.exp(s - mx) for s in ss]
    p_sink = jnp.exp(sink - mx)
    den = p_sink
    for p in ps:
        den = den + jnp.sum(p, axis=-1, keepdims=True)
    inv = 1.0 / den
    return [p * inv for p in ps], p_sink * inv


def attention_fwd(proj, q_w, k_w, sinks, name="att_fwd"):
    m = proj.shape[0]
    nb = m // BLOCK
    cq, ck, cv = COL_Q // Q_W, COL_K // KV_W, COL_V // KV_W

    def body(q_ref, kc_ref, vc_ref, kp_ref, vp_ref, km_ref, vm_ref, qw_ref, kw_ref, sk_ref, o_ref):
        b = pl.program_id(0)
        qn, _ = _head_norm(q_ref[...], qw_ref[...], ATT_HEADS)
        kw = kw_ref[...]
        kcn, _ = _head_norm(kc_ref[...], kw, ATT_KV_HEADS)
        kpn, _ = _head_norm(kp_ref[...], kw, ATT_KV_HEADS)
        kmn, _ = _head_norm(km_ref[...], kw, ATT_KV_HEADS)
        vc, vp, vm = vc_ref[...], vp_ref[...], vm_ref[...]
        masks = _att_masks(b)
        outs = []
        for h in range(ATT_HEADS):
            g = h // (ATT_HEADS // ATT_KV_HEADS)
            sl = slice(HEAD_DIM * g, HEAD_DIM * (g + 1))
            probs, _ = _att_probs(qn[h], (kcn[g], kpn[g], kmn[g]), masks, _alibi_slope(h), sk_ref[0:1, h:h + 1])
            o = _dot(probs[0], vc[:, sl]) + _dot(probs[1], vp[:, sl]) + _dot(probs[2], vm[:, sl])
            outs.append(o)
        o_ref[...] = jnp.concatenate(outs, axis=1).astype(o_ref.dtype)

    prev = lambda i: jnp.maximum(i - 1, 0)
    vec = lambda w: pl.BlockSpec((1, w), lambda i: (0, 0))
    return pl.pallas_call(
        body, grid=(nb,),
        in_specs=[pl.BlockSpec((BLOCK, Q_W), lambda i: (i, cq)),
                  pl.BlockSpec((BLOCK, KV_W), lambda i: (i, ck)), pl.BlockSpec((BLOCK, KV_W), lambda i: (i, cv)),
                  pl.BlockSpec((BLOCK, KV_W), lambda i: (prev(i), ck)), pl.BlockSpec((BLOCK, KV_W), lambda i: (prev(i), cv)),
                  pl.BlockSpec((BLOCK, KV_W), lambda i: (0, ck)), pl.BlockSpec((BLOCK, KV_W), lambda i: (0, cv)),
                  vec(HEAD_DIM), vec(HEAD_DIM), vec(LANE)],
        out_specs=pl.BlockSpec((BLOCK, Q_W), lambda i: (i, 0)),
        out_shape=jax.ShapeDtypeStruct((m, Q_W), MXU_DTYPE), name=name, compiler_params=_params("arbitrary"),
    )(proj, proj, proj, proj, proj, proj, proj, q_w, k_w, sinks)


def attention_bwd(proj, dmix, q_w, k_w, sinks, name="att_bwd"):
    m = proj.shape[0]
    nb = m // BLOCK
    cq, ck, cv = COL_Q // Q_W, COL_K // KV_W, COL_V // KV_W
    c_datt = SSM_INNER // Q_W
    grp = ATT_HEADS // ATT_KV_HEADS

    def body(do_ref, q_ref, kc_ref, vc_ref, kp_ref, vp_ref, km_ref, vm_ref, qw_ref, kw_ref, sk_ref,
             dq_ref, dk_ref, dv_ref, dqw_ref, dkw_ref, dsk_ref, car_k, car_v, met_k, met_v):
        i = pl.program_id(0)
        b = nb - 1 - i

        @pl.when(i == 0)
        def _():
            car_k[...] = jnp.zeros_like(car_k)
            car_v[...] = jnp.zeros_like(car_v)
            met_k[...] = jnp.zeros_like(met_k)
            met_v[...] = jnp.zeros_like(met_v)
            dqw_ref[...] = jnp.zeros_like(dqw_ref)
            dkw_ref[...] = jnp.zeros_like(dkw_ref)
            dsk_ref[...] = jnp.zeros_like(dsk_ref)

        q = q_ref[...]
        kc = kc_ref[...]
        qw, kw = qw_ref[...], kw_ref[...]
        qn, qr = _head_norm(q, qw, ATT_HEADS)
        kcn, kcr = _head_norm(kc, kw, ATT_KV_HEADS)
        kpn, _ = _head_norm(kp_ref[...], kw, ATT_KV_HEADS)
        kmn, _ = _head_norm(km_ref[...], kw, ATT_KV_HEADS)
        vc, vp, vm = vc_ref[...], vp_ref[...], vm_ref[...]
        do = do_ref[...]
        masks = _att_masks(b)
        zero = jnp.zeros((BLOCK, HEAD_DIM), F32)
        dkn = [[zero, zero] for _ in range(3)]
        dvv = [[zero, zero] for _ in range(3)]
        dqs = []
        dqw = jnp.zeros((1, HEAD_DIM), F32)
        lane = lax.broadcasted_iota(jnp.int32, (1, LANE), 1)
        dsk = jnp.zeros((1, LANE), F32)
        for h in range(ATT_HEADS):
            g = h // grp
            sl = slice(HEAD_DIM * g, HEAD_DIM * (g + 1))
            keys = (kcn[g], kpn[g], kmn[g])
            vals = (vc[:, sl], vp[:, sl], vm[:, sl])
            probs, p_sink = _att_probs(qn[h], keys, masks, _alibi_slope(h), sk_ref[0:1, h:h + 1])
            do_h = do[:, HEAD_DIM * h:HEAD_DIM * (h + 1)]
            o = _dot(probs[0], vals[0]) + _dot(probs[1], vals[1]) + _dot(probs[2], vals[2])
            delta = jnp.sum(do_h * o, axis=-1, keepdims=True)
            dqn = jnp.zeros((BLOCK, HEAD_DIM), F32)
            for t in range(3):
                dp = _dot_nt(do_h, vals[t])
                ds = probs[t] * (dp - delta)
                dqn = dqn + _dot(ds, keys[t])
                dkn[t][g] = dkn[t][g] + _dot_tn(ds, qn[h]) * _ATT_SCALE
                dvv[t][g] = dvv[t][g] + _dot_tn(probs[t], do_h)
            dqn = dqn * _ATT_SCALE
            dq_h, dw = _head_norm_bwd(q[:, HEAD_DIM * h:HEAD_DIM * (h + 1)], qr[h], qw, dqn)
            dqs.append(dq_h)
            dqw = dqw + dw
            dsk = dsk + jnp.where(lane == h, -jnp.sum(p_sink * delta, axis=0, keepdims=True), 0.0)
        dq_ref[...] = jnp.concatenate(dqs, axis=1)
        dqw_ref[...] += dqw
        dsk_ref[...] += dsk

        met_k[...] += jnp.concatenate(dkn[2], axis=1)
        met_v[...] += jnp.concatenate(dvv[2], axis=1)
        dkn_tot = jnp.concatenate(dkn[0], axis=1) + car_k[...]
        dv_tot = jnp.concatenate(dvv[0], axis=1) + car_v[...]
        first = (b == 0).astype(F32)
        dkn_tot = dkn_tot + first * met_k[...]
        dv_tot = dv_tot + first * met_v[...]
        car_k[...] = jnp.concatenate(dkn[1], axis=1)
        car_v[...] = jnp.concatenate(dvv[1], axis=1)
        dks = []
        dkw = jnp.zeros((1, HEAD_DIM), F32)
        for g in range(ATT_KV_HEADS):
            sl = slice(HEAD_DIM * g, HEAD_DIM * (g + 1))
            dk_g, dw = _head_norm_bwd(kc[:, sl], kcr[g], kw, dkn_tot[:, sl])
            dks.append(dk_g)
            dkw = dkw + dw
        dk_ref[...] = jnp.concatenate(dks, axis=1)
        dv_ref[...] = dv_tot
        dkw_ref[...] += dkw

    blk = lambda i: nb - 1 - i
    prev = lambda i: jnp.maximum(nb - 2 - i, 0)
    vec = lambda w: pl.BlockSpec((1, w), lambda i: (0, 0))
    kv_scratch = pltpu.VMEM((BLOCK, KV_W), F32)
    return pl.pallas_call(
        body, grid=(nb,),
        in_specs=[pl.BlockSpec((BLOCK, Q_W), lambda i: (blk(i), c_datt)),
                  pl.BlockSpec((BLOCK, Q_W), lambda i: (blk(i), cq)),
                  pl.BlockSpec((BLOCK, KV_W), lambda i: (blk(i), ck)), pl.BlockSpec((BLOCK, KV_W), lambda i: (blk(i), cv)),
                  pl.BlockSpec((BLOCK, KV_W), lambda i: (prev(i), ck)), pl.BlockSpec((BLOCK, KV_W), lambda i: (prev(i), cv)),
                  pl.BlockSpec((BLOCK, KV_W), lambda i: (0, ck)), pl.BlockSpec((BLOCK, KV_W), lambda i: (0, cv)),
                  vec(HEAD_DIM), vec(HEAD_DIM), vec(LANE)],
        out_specs=[pl.BlockSpec((BLOCK, Q_W), lambda i: (blk(i), 0)),
                   pl.BlockSpec((BLOCK, KV_W), lambda i: (blk(i), 0)), pl.BlockSpec((BLOCK, KV_W), lambda i: (blk(i), 0)),
                   vec(HEAD_DIM), vec(HEAD_DIM), vec(LANE)],
        out_shape=[jax.ShapeDtypeStruct((m, Q_W), F32), jax.ShapeDtypeStruct((m, KV_W), F32),
                   jax.ShapeDtypeStruct((m, KV_W), F32), jax.ShapeDtypeStruct((1, HEAD_DIM), F32),
                   jax.ShapeDtypeStruct((1, HEAD_DIM), F32), jax.ShapeDtypeStruct((1, LANE), F32)],
        scratch_shapes=[kv_scratch, kv_scratch, kv_scratch, kv_scratch],
        name=name, compiler_params=_params("arbitrary"),
    )(dmix, proj, proj, proj, proj, proj, proj, proj, q_w, k_w, sinks)


HALO = 8
GROUP_W = SSM_INNER // SSM_GROUPS
HEADS_PER_GROUP = SSM_HEADS // SSM_GROUPS


def _head_expand():
    h = jnp.arange(LANE)[:, None]
    c = jnp.arange(SSM_INNER)[None, :]
    return (c // SSM_HEAD_DIM == h).astype(F32)


def _softplus(x):
    return jnp.maximum(x, 0.0) + jnp.log1p(jnp.exp(-jnp.abs(x)))


def _ssd_decays(dt, a_log_row):
    row = lax.broadcasted_iota(jnp.int32, (BLOCK, BLOCK), 0)
    col = lax.broadcasted_iota(jnp.int32, (BLOCK, BLOCK), 1)
    lower = row >= col
    a = -jnp.exp(a_log_row)
    a_cs = _dot_hi(lower.astype(F32), dt * a)
    return a, a_cs, lower


def _decay_matrix(a_cs, a_cs_t, h, lower):
    diff = a_cs[:, h:h + 1] - a_cs_t[h:h + 1, :]
    return jnp.where(lower, jnp.exp(jnp.where(lower, diff, 0.0)), 0.0)


def _conv_taps(s_ref, w_ref, first, rows):
    acc = w_ref[0:1, :] * s_ref[pl.ds(first, rows), :]
    for j in range(1, SSM_CONV):
        acc = acc + w_ref[j:j + 1, :] * s_ref[pl.ds(first + j, rows), :]
    return acc


def ssd_fwd(proj, cw_x, cw_bc, cb_x, cb_bc, dt_bias, a_log, d_exp, norm_w, name="ssd_fwd"):
    m = proj.shape[0]
    nb = m // BLOCK
    expand = _head_expand()
    expand_t = expand.T

    def body(z_ref, xs_ref, bc_ref, dtr_ref, cwx_ref, cwbc_ref, cbx_ref, cbbc_ref, dtb_ref, alog_ref, dexp_ref,
             nw_ref, e_ref, et_ref, out_ref, prex_ref, prebc_ref, dt_ref, ypre_ref, st_ref, sx, sbc, state):
        c = pl.program_id(0)

        @pl.when(c == 0)
        def _():
            sx[0:HALO, :] = jnp.zeros((HALO, SSM_INNER), F32)
            sbc[0:HALO, :] = jnp.zeros((HALO, 2 * LANE), F32)
            state[...] = jnp.zeros_like(state)

        sx[HALO:HALO + BLOCK, :] = xs_ref[...]
        sbc[HALO:HALO + BLOCK, :] = bc_ref[...]
        first = HALO - (SSM_CONV - 1)
        pre_x = _conv_taps(sx, cwx_ref, first, BLOCK) + cbx_ref[...]
        pre_bc = _conv_taps(sbc, cwbc_ref, first, BLOCK) + cbbc_ref[...]
        sx[0:HALO, :] = xs_ref[BLOCK - HALO:BLOCK, :]
        sbc[0:HALO, :] = bc_ref[BLOCK - HALO:BLOCK, :]
        prex_ref[...] = pre_x
        prebc_ref[...] = pre_bc
        xc = pre_x * _sigmoid(pre_x)
        bcv = pre_bc * _sigmoid(pre_bc)

        rows = _row_ids(c * BLOCK, BLOCK, LANE)
        lanes = lax.broadcasted_iota(jnp.int32, (BLOCK, LANE), 1)
        live = jnp.logical_and(rows >= FRONT_PAD, lanes < SSM_HEADS)
        dt = jnp.where(live, _softplus(dtr_ref[...] + dtb_ref[...]), 0.0)
        dt_ref[...] = dt
        a, a_cs, lower = _ssd_decays(dt, alog_ref[...])
        a_cs_t = a_cs.T
        dt_t = dt.T
        e = e_ref[...]
        es_full = _dot_hi(jnp.exp(a_cs), e)
        wx_full = _dot_hi(jnp.exp(a_cs[BLOCK - 1:BLOCK, :] - a_cs) * dt, e)
        end_col = jnp.exp(a_cs_t[:, BLOCK - 1:BLOCK])
        dec_full = _dot_hi(et_ref[...], jnp.broadcast_to(end_col, (LANE, SSM_STATE)))

        st_ref[0] = state[...]
        ys = []
        for g in range(SSM_GROUPS):
            b_g = bcv[:, SSM_STATE * g:SSM_STATE * (g + 1)]
            c_g = bcv[:, LANE + SSM_STATE * g:LANE + SSM_STATE * (g + 1)]
            gs = slice(GROUP_W * g, GROUP_W * (g + 1))
            cb = _dot_nt(c_g, b_g)
            yd = []
            for hh in range(HEADS_PER_GROUP):
                h = g * HEADS_PER_GROUP + hh
                w = cb * _decay_matrix(a_cs, a_cs_t, h, lower) * dt_t[h:h + 1, :]
                yd.append(_dot(w, xc[:, SSM_HEAD_DIM * h:SSM_HEAD_DIM * (h + 1)]))
            h_g = state[gs, :]
            y_off = _dot_nt(c_g, h_g) * es_full[:, gs]
            ys.append(jnp.concatenate(yd, axis=1) + y_off)
            new_state = _dot_tn(xc[:, gs] * wx_full[:, gs], b_g)
            state[gs, :] = h_g * dec_full[gs, :] + new_state
        y_pre = jnp.concatenate(ys, axis=1) + xc * dexp_ref[...]
        ypre_ref[...] = y_pre
        z = z_ref[...]
        gt = y_pre * (z * _sigmoid(z))
        outs = []
        for g in range(SSM_GROUPS):
            gg = gt[:, GROUP_W * g:GROUP_W * (g + 1)]
            r = lax.rsqrt(jnp.mean(gg * gg, -1, keepdims=True) + EPS)
            outs.append(gg * r)
        out_ref[...] = (jnp.concatenate(outs, axis=1) * nw_ref[...]).astype(out_ref.dtype)

    full = lambda s: pl.BlockSpec(s, lambda i: (0,) * len(s))
    rowblk = lambda w, cidx: pl.BlockSpec((BLOCK, w), lambda i: (i, cidx))
    return pl.pallas_call(
        body, grid=(nb,),
        in_specs=[rowblk(SSM_INNER, COL_Z // SSM_INNER), rowblk(SSM_INNER, COL_XS // SSM_INNER),
                  rowblk(2 * LANE, COL_BC // (2 * LANE)), rowblk(LANE, COL_DT // LANE),
                  full((SSM_CONV, SSM_INNER)), full((SSM_CONV, 2 * LANE)), full((1, SSM_INNER)), full((1, 2 * LANE)),
                  full((1, LANE)), full((1, LANE)), full((1, SSM_INNER)), full((1, SSM_INNER)),
                  full((LANE, SSM_INNER)), full((SSM_INNER, LANE))],
        out_specs=[rowblk(SSM_INNER, 0), rowblk(SSM_INNER, 0), rowblk(2 * LANE, 0), rowblk(LANE, 0),
                   rowblk(SSM_INNER, 0), pl.BlockSpec((1, SSM_INNER, SSM_STATE), lambda i: (i, 0, 0))],
        out_shape=[jax.ShapeDtypeStruct((m, SSM_INNER), MXU_DTYPE), jax.ShapeDtypeStruct((m, SSM_INNER), F32),
                   jax.ShapeDtypeStruct((m, 2 * LANE), F32), jax.ShapeDtypeStruct((m, LANE), F32),
                   jax.ShapeDtypeStruct((m, SSM_INNER), F32), jax.ShapeDtypeStruct((nb, SSM_INNER, SSM_STATE), F32)],
        scratch_shapes=[pltpu.VMEM((HALO + BLOCK, SSM_INNER), F32), pltpu.VMEM((HALO + BLOCK, 2 * LANE), F32),
                        pltpu.VMEM((SSM_INNER, SSM_STATE), F32)],
        name=name, compiler_params=_params("arbitrary"),
    )(proj, proj, proj, proj, cw_x, cw_bc, cb_x, cb_bc, dt_bias, a_log, d_exp, norm_w, expand, expand_t)


def ssd_bwd(proj, dmix, pre_x, pre_bc, dt, y_pre, states, cw_x, cw_bc, dt_bias, a_log, d_exp, norm_w,
            name="ssd_bwd"):
    m = proj.shape[0]
    nb = m // BLOCK
    expand = _head_expand()
    expand_t = expand.T

    def body(do_ref, z_ref, xs_ref, xsp_ref, bc_ref, bcp_ref, dtr_ref, prex_ref, prebc_ref, dt_ref, ypre_ref, st_ref,
             cwx_ref, cwbc_ref, dtb_ref, alog_ref, dexp_ref, nw_ref, e_ref, et_ref,
             dz_ref, dxs_ref, dbc_ref, ddt_ref, dcwx_ref, dcwbc_ref, dcbx_ref, dcbbc_ref, ddtb_ref, dalog_ref,
             dd_ref, dnw_ref,
             dstate, hnext, tx, tbc, sx, sbc, dlane):
        i = pl.program_id(0)
        c = nb - 1 - i

        @pl.when(i == 0)
        def _():
            dstate[...] = jnp.zeros_like(dstate)
            hnext[...] = jnp.zeros_like(hnext)
            tx[BLOCK:BLOCK + HALO, :] = jnp.zeros((HALO, SSM_INNER), F32)
            tbc[BLOCK:BLOCK + HALO, :] = jnp.zeros((HALO, 2 * LANE), F32)
            dlane[...] = jnp.zeros_like(dlane)
            for r in (dcwx_ref, dcwbc_ref, dcbx_ref, dcbbc_ref, ddtb_ref, dalog_ref, dd_ref, dnw_ref):
                r[...] = jnp.zeros_like(r)

        e = e_ref[...]
        et = et_ref[...]
        pre_x = prex_ref[...]
        pre_bc = prebc_ref[...]
        sig_x = _sigmoid(pre_x)
        sig_bc = _sigmoid(pre_bc)
        xc = pre_x * sig_x
        bcv = pre_bc * sig_bc
        dt = dt_ref[...]
        a, a_cs, lower = _ssd_decays(dt, alog_ref[...])
        a_cs_t = a_cs.T
        es_full = _dot_hi(jnp.exp(a_cs), e)
        ed_full = _dot_hi(jnp.exp(a_cs[BLOCK - 1:BLOCK, :] - a_cs), e)
        dt_full = _dot_hi(dt, e)
        end_col = jnp.exp(a_cs_t[:, BLOCK - 1:BLOCK])
        dec_full = _dot_hi(et, jnp.broadcast_to(end_col, (LANE, SSM_STATE)))
        dexp = dexp_ref[...]

        z = z_ref[...]
        zs = _sigmoid(z)
        sz = z * zs
        y_pre = ypre_ref[...]
        gt = y_pre * sz
        do = do_ref[...]
        nw = nw_ref[...]
        dgt = []
        dnw = []
        for g in range(SSM_GROUPS):
            gs = slice(GROUP_W * g, GROUP_W * (g + 1))
            gg = gt[:, gs]
            r = lax.rsqrt(jnp.mean(gg * gg, -1, keepdims=True) + EPS)
            gn = do[:, gs] * nw[:, gs]
            dgt.append(r * gn - gg * ((r * r * r) * jnp.mean(gg * gn, -1, keepdims=True)))
            dnw.append(jnp.sum(do[:, gs] * (gg * r), axis=0, keepdims=True))
        dgt = jnp.concatenate(dgt, axis=1)
        dnw_ref[...] += jnp.concatenate(dnw, axis=1)
        dy = dgt * sz
        dz_ref[...] = dgt * y_pre * (zs * (1.0 + z * (1.0 - zs)))
        dlane[...] += jnp.sum(dy * xc, axis=0, keepdims=True)
        xd = xc * dt_full

        lane_id = lax.broadcasted_iota(jnp.int32, (BLOCK, LANE), 1)
        sub_id = lax.broadcasted_iota(jnp.int32, (LANE, BLOCK), 0)
        ds_to = jnp.zeros((BLOCK, LANE), F32)
        ds_from_t = jnp.zeros((LANE, BLOCK), F32)
        dxd_parts, inter_parts = [], []
        dbs, dcs = [], []
        for g in range(SSM_GROUPS):
            gs = slice(GROUP_W * g, GROUP_W * (g + 1))
            b_g = bcv[:, SSM_STATE * g:SSM_STATE * (g + 1)]
            c_g = bcv[:, LANE + SSM_STATE * g:LANE + SSM_STATE * (g + 1)]
            cb = _dot_nt(c_g, b_g)
            dcb = jnp.zeros((BLOCK, BLOCK), F32)
            dxd_h = []
            for hh in range(HEADS_PER_GROUP):
                h = g * HEADS_PER_GROUP + hh
                hs = slice(SSM_HEAD_DIM * h, SSM_HEAD_DIM * (h + 1))
                lm = _decay_matrix(a_cs, a_cs_t, h, lower)
                dy_h = dy[:, hs]
                gl = _dot_nt(dy_h, xd[:, hs]) * lm
                dcb = dcb + gl
                e_h = gl * cb
                ds_to = ds_to + jnp.where(lane_id == h, jnp.sum(e_h, axis=-1, keepdims=True), 0.0)
                ds_from_t = ds_from_t + jnp.where(sub_id == h, jnp.sum(e_h, axis=0, keepdims=True), 0.0)
                dxd_h.append(_dot_tn(cb * lm, dy_h))
            h_g = st_ref[0, gs, :]
            dh_g = dstate[gs, :]
            dys_g = dy[:, gs] * es_full[:, gs]
            xde_g = xd[:, gs] * ed_full[:, gs]
            dcs.append(_dot(dcb, b_g) + _dot(dys_g, h_g))
            dbs.append(_dot_tn(dcb, c_g) + _dot(xde_g, dh_g))
            y_off = _dot_nt(c_g, h_g) * es_full[:, gs]
            dxd_state = _dot_nt(b_g, dh_g) * ed_full[:, gs]
            inter_parts.append(dy[:, gs] * y_off - xd[:, gs] * dxd_state)
            dxd_parts.append(jnp.concatenate(dxd_h, axis=1) + dxd_state)
            dstate[gs, :] = dh_g * dec_full[gs, :] + _dot_tn(dys_g, c_g)
            if g == 0:
                end_dot = hnext[gs, :] * dh_g
            else:
                end_dot = jnp.concatenate([end_dot, hnext[gs, :] * dh_g], axis=0)
        dxd = jnp.concatenate(dxd_parts, axis=1)
        hnext[...] = st_ref[0]

        ds = ds_to - ds_from_t.T + _dot_hi(jnp.concatenate(inter_parts, axis=1), et)
        ds_end = jnp.sum(_dot_tn_hi(end_dot, et), axis=0, keepdims=True)
        rows_l = lax.broadcasted_iota(jnp.int32, (BLOCK, LANE), 0)
        ds = ds + jnp.where(rows_l == BLOCK - 1, ds_end, 0.0)
        row = lax.broadcasted_iota(jnp.int32, (BLOCK, BLOCK), 0)
        col = lax.broadcasted_iota(jnp.int32, (BLOCK, BLOCK), 1)
        dadt = _dot_hi((col >= row).astype(F32), ds)
        ddt = dadt * a + _dot_hi(dxd * xc, et)
        dalog_ref[...] += jnp.sum(dadt * dt, axis=0, keepdims=True) * a
        rows = _row_ids(c * BLOCK, BLOCK, LANE)
        lanes = lax.broadcasted_iota(jnp.int32, (BLOCK, LANE), 1)
        live = jnp.logical_and(rows >= FRONT_PAD, lanes < SSM_HEADS)
        ddt_raw = jnp.where(live, ddt * _sigmoid(dtr_ref[...] + dtb_ref[...]), 0.0)
        ddt_ref[...] = ddt_raw
        ddtb_ref[...] += jnp.sum(ddt_raw, axis=0, keepdims=True)

        dxc = dxd * dt_full + dy * dexp
        dpre_x = dxc * (sig_x * (1.0 + pre_x * (1.0 - sig_x)))
        dpre_bc = jnp.concatenate(dbs + dcs, axis=1) * (sig_bc * (1.0 + pre_bc * (1.0 - sig_bc)))
        dcbx_ref[...] += jnp.sum(dpre_x, axis=0, keepdims=True)
        dcbbc_ref[...] += jnp.sum(dpre_bc, axis=0, keepdims=True)
        keep_x = _row_ids(c * BLOCK, BLOCK, SSM_INNER) >= FRONT_PAD
        keep_bc = _row_ids(c * BLOCK, BLOCK, 2 * LANE) >= FRONT_PAD
        prev_live = (c > 0).astype(F32)
        for (dpre, t_ref, s_ref, cur_ref, prv_ref, w_ref, dw_ref, dx_ref, keep) in (
                (dpre_x, tx, sx, xs_ref, xsp_ref, cwx_ref, dcwx_ref, dxs_ref, keep_x),
                (dpre_bc, tbc, sbc, bc_ref, bcp_ref, cwbc_ref, dcwbc_ref, dbc_ref, keep_bc)):
            t_ref[0:BLOCK, :] = dpre
            acc = w_ref[0:1, :] * t_ref[pl.ds(SSM_CONV - 1, BLOCK), :]
            for j in range(1, SSM_CONV):
                acc = acc + w_ref[j:j + 1, :] * t_ref[pl.ds(SSM_CONV - 1 - j, BLOCK), :]
            dx_ref[...] = jnp.where(keep, acc, 0.0)
            t_ref[BLOCK:BLOCK + HALO, :] = dpre[0:HALO, :]
            s_ref[0:HALO, :] = prv_ref[BLOCK - HALO:BLOCK, :] * prev_live
            s_ref[HALO:HALO + BLOCK, :] = cur_ref[...]
            first = HALO - (SSM_CONV - 1)
            for j in range(SSM_CONV):
                dw_ref[j:j + 1, :] += jnp.sum(dpre * s_ref[pl.ds(first + j, BLOCK), :], axis=0, keepdims=True)

        @pl.when(i == nb - 1)
        def _():
            dd_ref[...] = _dot_hi(jnp.broadcast_to(dlane[...], (HALO, SSM_INNER)), et)[0:1, :]

    blk = lambda i: nb - 1 - i
    prv = lambda i: jnp.maximum(nb - 2 - i, 0)
    full = lambda s: pl.BlockSpec(s, lambda i: (0,) * len(s))
    rowblk = lambda w, cidx: pl.BlockSpec((BLOCK, w), lambda i: (blk(i), cidx))
    prvblk = lambda w, cidx: pl.BlockSpec((BLOCK, w), lambda i: (prv(i), cidx))
    return pl.pallas_call(
        body, grid=(nb,),
        in_specs=[rowblk(SSM_INNER, 0),
                  rowblk(SSM_INNER, COL_Z // SSM_INNER),
                  rowblk(SSM_INNER, COL_XS // SSM_INNER), prvblk(SSM_INNER, COL_XS // SSM_INNER),
                  rowblk(2 * LANE, COL_BC // (2 * LANE)), prvblk(2 * LANE, COL_BC // (2 * LANE)),
                  rowblk(LANE, COL_DT // LANE),
                  rowblk(SSM_INNER, 0), rowblk(2 * LANE, 0), rowblk(LANE, 0), rowblk(SSM_INNER, 0),
                  pl.BlockSpec((1, SSM_INNER, SSM_STATE), lambda i: (blk(i), 0, 0)),
                  full((SSM_CONV, SSM_INNER)), full((SSM_CONV, 2 * LANE)), full((1, LANE)), full((1, LANE)),
                  full((1, SSM_INNER)), full((1, SSM_INNER)), full((LANE, SSM_INNER)), full((SSM_INNER, LANE))],
        out_specs=[rowblk(SSM_INNER, 0), rowblk(SSM_INNER, 0), rowblk(2 * LANE, 0), rowblk(LANE, 0),
                   full((SSM_CONV, SSM_INNER)), full((SSM_CONV, 2 * LANE)), full((1, SSM_INNER)), full((1, 2 * LANE)),
                   full((1, LANE)), full((1, LANE)), full((1, LANE)), full((1, SSM_INNER))],
        out_shape=[jax.ShapeDtypeStruct((m, SSM_INNER), F32), jax.ShapeDtypeStruct((m, SSM_INNER), F32),
                   jax.ShapeDtypeStruct((m, 2 * LANE), F32), jax.ShapeDtypeStruct((m, LANE), F32),
                   jax.ShapeDtypeStruct((SSM_CONV, SSM_INNER), F32), jax.ShapeDtypeStruct((SSM_CONV, 2 * LANE), F32),
                   jax.ShapeDtypeStruct((1, SSM_INNER), F32), jax.ShapeDtypeStruct((1, 2 * LANE), F32),
                   jax.ShapeDtypeStruct((1, LANE), F32), jax.ShapeDtypeStruct((1, LANE), F32),
                   jax.ShapeDtypeStruct((1, LANE), F32), jax.ShapeDtypeStruct((1, SSM_INNER), F32)],
        scratch_shapes=[pltpu.VMEM((SSM_INNER, SSM_STATE), F32), pltpu.VMEM((SSM_INNER, SSM_STATE), F32),
                        pltpu.VMEM((BLOCK + HALO, SSM_INNER), F32), pltpu.VMEM((BLOCK + HALO, 2 * LANE), F32),
                        pltpu.VMEM((HALO + BLOCK, SSM_INNER), F32), pltpu.VMEM((HALO + BLOCK, 2 * LANE), F32),
                        pltpu.VMEM((1, SSM_INNER), F32)],
        name=name, compiler_params=_params("arbitrary"),
    )(dmix, proj, proj, proj, proj, proj, proj, pre_x, pre_bc, dt, y_pre, states,
      cw_x, cw_bc, dt_bias, a_log, d_exp, norm_w, expand, expand_t)


CONF_HALO = 32


def _glu_masked(v, first_row):
    a = v[:, :D_MODEL]
    s = _sigmoid(v[:, D_MODEL:])
    rows = _row_ids(first_row, v.shape[0], D_MODEL)
    return jnp.where(rows >= FRONT_PAD, a * s, 0.0), a, s


def _layer_norm_stats(c):
    mu = jnp.mean(c, -1, keepdims=True)
    xc = c - mu
    rstd = lax.rsqrt(jnp.mean(xc * xc, -1, keepdims=True) + LN_EPS)
    return xc * rstd, rstd


def conformer_mid_fwd(v, dw_w, dw_b, ln_g, ln_b, name="conf_mid_fwd"):
    m = v.shape[0]
    nb = m // BLOCK
    kpad = dw_w.shape[0]

    def body(vc_ref, vp_ref, w_ref, b_ref, g_ref, beta_ref, c_ref, s_ref, sg):
        i = pl.program_id(0)
        g_prev, _, _ = _glu_masked(vp_ref[BLOCK - CONF_HALO:BLOCK, :], (i - 1) * BLOCK + BLOCK - CONF_HALO)
        sg[0:CONF_HALO, :] = g_prev * (i > 0).astype(F32)
        g_cur, _, _ = _glu_masked(vc_ref[...], i * BLOCK)
        sg[CONF_HALO:CONF_HALO + BLOCK, :] = g_cur
        first = CONF_HALO - (CONF_KERNEL - 1)
        acc = b_ref[...] + w_ref[0:1, :] * sg[pl.ds(first, BLOCK), :]
        for j in range(1, CONF_KERNEL):
            acc = acc + w_ref[j:j + 1, :] * sg[pl.ds(first + j, BLOCK), :]
        c_ref[...] = acc
        xhat, _ = _layer_norm_stats(acc)
        nrm = xhat * g_ref[...] + beta_ref[...]
        s_ref[...] = (nrm * _sigmoid(nrm)).astype(s_ref.dtype)

    full = lambda s: pl.BlockSpec(s, lambda i: (0,) * len(s))
    return pl.pallas_call(
        body, grid=(nb,),
        in_specs=[pl.BlockSpec((BLOCK, 2 * D_MODEL), lambda i: (i, 0)),
                  pl.BlockSpec((BLOCK, 2 * D_MODEL), lambda i: (jnp.maximum(i - 1, 0), 0)),
                  full((kpad, D_MODEL)), full((1, D_MODEL)), full((1, D_MODEL)), full((1, D_MODEL))],
        out_specs=[pl.BlockSpec((BLOCK, D_MODEL), lambda i: (i, 0)), pl.BlockSpec((BLOCK, D_MODEL), lambda i: (i, 0))],
        out_shape=[jax.ShapeDtypeStruct((m, D_MODEL), F32), jax.ShapeDtypeStruct((m, D_MODEL), MXU_DTYPE)],
        scratch_shapes=[pltpu.VMEM((CONF_HALO + BLOCK, D_MODEL), F32)],
        name=name, compiler_params=_params("arbitrary"),
    )(v, v, dw_w, dw_b, ln_g, ln_b)


def conformer_ln_bwd(ds, c, ln_g, ln_b, name="conf_ln_bwd"):
    m, d = c.shape
    tm = ROW_TILE

    def body(ds_ref, c_ref, g_ref, beta_ref, dc_ref, dg_ref, db_ref):
        @pl.when(pl.program_id(0) == 0)
        def _():
            dg_ref[...] = jnp.zeros_like(dg_ref)
            db_ref[...] = jnp.zeros_like(db_ref)

        xhat, rstd = _layer_norm_stats(c_ref[...])
        g = g_ref[...]
        nrm = xhat * g + beta_ref[...]
        sg = _sigmoid(nrm)
        dn = ds_ref[...] * (sg * (1.0 + nrm * (1.0 - sg)))
        db_ref[...] += jnp.sum(dn, axis=0, keepdims=True)
        dg_ref[...] += jnp.sum(dn * xhat, axis=0, keepdims=True)
        dx = dn * g
        dc_ref[...] = rstd * (dx - jnp.mean(dx, -1, keepdims=True) - xhat * jnp.mean(dx * xhat, -1, keepdims=True))

    row = pl.BlockSpec((tm, d), lambda i: (i, 0))
    vec = pl.BlockSpec((1, d), lambda i: (0, 0))
    return pl.pallas_call(
        body, grid=(m // tm,), in_specs=[row, row, vec, vec], out_specs=[row, vec, vec],
        out_shape=[jax.ShapeDtypeStruct((m, d), F32), jax.ShapeDtypeStruct((1, d), F32), jax.ShapeDtypeStruct((1, d), F32)],
        name=name, compiler_params=_params("arbitrary"),
    )(ds, c, ln_g, ln_b)


def conformer_conv_bwd(dc, v, dw_w, name="conf_conv_bwd"):
    m = v.shape[0]
    nb = m // BLOCK
    kpad = dw_w.shape[0]

    def body(dcc_ref, dcn_ref, vc_ref, vp_ref, w_ref, dv_ref, dw_ref, db_ref, dvb_ref, tg, sg):
        i = pl.program_id(0)

        @pl.when(i == 0)
        def _():
            dw_ref[...] = jnp.zeros_like(dw_ref)
            db_ref[...] = jnp.zeros_like(db_ref)
            dvb_ref[...] = jnp.zeros_like(dvb_ref)

        dc_cur = dcc_ref[...]
        tg[0:BLOCK, :] = dc_cur
        tg[BLOCK:BLOCK + CONF_HALO, :] = dcn_ref[0:CONF_HALO, :] * (i < nb - 1).astype(F32)
        g_prev, _, _ = _glu_masked(vp_ref[BLOCK - CONF_HALO:BLOCK, :], (i - 1) * BLOCK + BLOCK - CONF_HALO)
        sg[0:CONF_HALO, :] = g_prev * (i > 0).astype(F32)
        g_cur, a, s = _glu_masked(vc_ref[...], i * BLOCK)
        sg[CONF_HALO:CONF_HALO + BLOCK, :] = g_cur
        db_ref[...] += jnp.sum(dc_cur, axis=0, keepdims=True)
        first = CONF_HALO - (CONF_KERNEL - 1)
        dg = w_ref[0:1, :] * tg[pl.ds(CONF_KERNEL - 1, BLOCK), :]
        for j in range(1, CONF_KERNEL):
            dg = dg + w_ref[j:j + 1, :] * tg[pl.ds(CONF_KERNEL - 1 - j, BLOCK), :]
        for j in range(CONF_KERNEL):
            dw_ref[j:j + 1, :] += jnp.sum(dc_cur * sg[pl.ds(first + j, BLOCK), :], axis=0, keepdims=True)
        rows = _row_ids(i * BLOCK, BLOCK, D_MODEL)
        dg = jnp.where(rows >= FRONT_PAD, dg, 0.0)
        da = dg * s
        dbv = dg * a * (s * (1.0 - s))
        dv = jnp.concatenate([da, dbv], axis=1)
        dv_ref[...] = dv.astype(dv_ref.dtype)
        dvb_ref[...] += jnp.sum(dv, axis=0, keepdims=True)

    full = lambda s: pl.BlockSpec(s, lambda i: (0,) * len(s))
    return pl.pallas_call(
        body, grid=(nb,),
        in_specs=[pl.BlockSpec((BLOCK, D_MODEL), lambda i: (i, 0)),
                  pl.BlockSpec((BLOCK, D_MODEL), lambda i: (jnp.minimum(i + 1, nb - 1), 0)),
                  pl.BlockSpec((BLOCK, 2 * D_MODEL), lambda i: (i, 0)),
                  pl.BlockSpec((BLOCK, 2 * D_MODEL), lambda i: (jnp.maximum(i - 1, 0), 0)),
                  full((kpad, D_MODEL))],
        out_specs=[pl.BlockSpec((BLOCK, 2 * D_MODEL), lambda i: (i, 0)), full((kpad, D_MODEL)),
                   full((1, D_MODEL)), full((1, 2 * D_MODEL))],
        out_shape=[jax.ShapeDtypeStruct((m, 2 * D_MODEL), MXU_DTYPE), jax.ShapeDtypeStruct((kpad, D_MODEL), F32),
                   jax.ShapeDtypeStruct((1, D_MODEL), F32), jax.ShapeDtypeStruct((1, 2 * D_MODEL), F32)],
        scratch_shapes=[pltpu.VMEM((BLOCK + CONF_HALO, D_MODEL), F32), pltpu.VMEM((CONF_HALO + BLOCK, D_MODEL), F32)],
        name=name, compiler_params=_params("arbitrary"),
    )(dc, dc, v, v, dw_w)


def _row(v, width=None):
    v = v.reshape(1, -1).astype(F32)
    if width is not None and v.shape[1] < width:
        v = jnp.pad(v, ((0, 0), (0, width - v.shape[1])))
    return v


def _w_in_to_kernel(w):
    pad = jnp.zeros((w.shape[0], PROJ_W - COL_DT - SSM_HEADS), w.dtype)
    return jnp.concatenate([w[:, 768:1792], w[:, 1792:2816], w[:, 0:512], w[:, 2816:3072], w[:, 512:640],
                            w[:, 640:768], w[:, 3072:3088], pad], axis=1)


def _w_in_from_kernel(g):
    return jnp.concatenate([g[:, COL_Q:COL_Q + Q_W], g[:, COL_K:COL_K + KV_W], g[:, COL_V:COL_V + KV_W],
                            g[:, COL_Z:COL_Z + SSM_INNER], g[:, COL_XS:COL_XS + SSM_INNER],
                            g[:, COL_BC:COL_BC + 2 * LANE], g[:, COL_DT:COL_DT + SSM_HEADS]], axis=1)


def even_fwd(h, p):
    u = rms_fwd(h, p["norm"])
    proj = matmul(u, p["w_in"], name="mm_proj")
    att = attention_fwd(proj, p["q_norm"], p["k_norm"], p["sinks"])
    ssm, pre_x, pre_bc, dt, y_pre, states = ssd_fwd(proj, p["cw_x"], p["cw_bc"], p["cb_x"], p["cb_bc"], p["dt_bias"],
                                                    p["a_log"], p["d_exp"], p["ssm_norm"])
    mix = jnp.concatenate([ssm, att], axis=1)
    out = matmul(mix, p["w_out"], epilogue="resid", extra=h, name="mm_mix_out")
    return out, (h, u, proj, mix, pre_x, pre_bc, dt, y_pre, states)


def even_bwd(dh, p, saved):
    h, u, proj, mix, pre_x, pre_bc, dt, y_pre, states = saved
    dmix = matmul(dh, p["w_out"], trans_b=True, name="mm_dmix")
    dw_out = matmul_tn(mix, dh, ti=512, tn=D_MODEL, name="mm_dw_out")
    dq, dk, dv, dqw, dkw, dsk = attention_bwd(proj, dmix, p["q_norm"], p["k_norm"], p["sinks"])
    (dz, dxs, dbc, ddt, dcwx, dcwbc, dcbx, dcbbc, ddtb, dalog, dd, dnw) = ssd_bwd(
        proj, dmix, pre_x, pre_bc, dt, y_pre, states, p["cw_x"], p["cw_bc"], p["dt_bias"], p["a_log"], p["d_exp"],
        p["ssm_norm"])
    dproj = jnp.concatenate([dz, dxs, dq, dbc, dk, dv, ddt], axis=1).astype(MXU_DTYPE)
    du = matmul(dproj, p["w_in"], trans_b=True, name="mm_du_in")
    dw_in = matmul_tn(u, dproj, ti=512, tn=PROJ_W, name="mm_dw_in")
    dh_in, dg = rms_bwd(h, p["norm"], du, dh)
    grads = dict(norm=dg, w_in=dw_in, w_out=dw_out, cw_x=dcwx, cw_bc=dcwbc, cb_x=dcbx, cb_bc=dcbbc, dt_bias=ddtb,
                 a_log=dalog, d_skip=dd, ssm_norm=dnw, q_norm=dqw, k_norm=dkw, sinks=dsk)
    return dh_in, grads


def conf_fwd(h, p):
    u = rms_fwd(h, p["norm"])
    v = matmul(u, p["pw1_w"], bias=p["pw1_b"], tn=D_MODEL, name="mm_pw1")
    c, s = conformer_mid_fwd(v, p["dw_w"], p["dw_b"], p["ln_g"], p["ln_b"])
    out = matmul(s, p["pw2_w"], bias=p["pw2_b"], epilogue="resid", extra=h, name="mm_pw2")
    return out, (h, u, v, c, s)


def conf_bwd(dh, p, saved):
    h, u, v, c, s = saved
    dpw2_b = col_sum(dh)
    ds = matmul(dh, p["pw2_w"], trans_b=True, name="mm_ds")
    dpw2_w = matmul_tn(s, dh, ti=D_MODEL, tn=D_MODEL, name="mm_dpw2")
    dc, dln_g, dln_b = conformer_ln_bwd(ds, c, p["ln_g"], p["ln_b"])
    dv, ddw_w, ddw_b, dpw1_b = conformer_conv_bwd(dc, v, p["dw_w"])
    du = matmul(dv, p["pw1_w"], trans_b=True, name="mm_du_pw1")
    dpw1_w = matmul_tn(u, dv, ti=D_MODEL, tn=D_MODEL, name="mm_dpw1")
    dh_in, dg = rms_bwd(h, p["norm"], du, dh)
    grads = dict(norm=dg, pw1_w=dpw1_w, pw1_b=dpw1_b, dw_w=ddw_w, dw_b=ddw_b, ln_g=dln_g, ln_b=dln_b, pw2_w=dpw2_w,
                 pw2_b=dpw2_b)
    return dh_in, grads


def mlp_fwd(h, p):
    u = rms_fwd(h, p["norm"])
    a, act = matmul(u, p["w_up"], tn=D_MODEL, epilogue="relu2", name="mm_up")
    out = matmul(act, p["w_down"], epilogue="resid", extra=h, name="mm_down")
    return out, (h, u, a, act)


def mlp_bwd(dh, p, saved):
    h, u, a, act = saved
    da = matmul(dh, p["w_down"], trans_b=True, tn=D_MODEL, epilogue="drelu2", extra=a, out_dtype=MXU_DTYPE, name="mm_da")
    dw_down = matmul_tn(act, dh, ti=D_MODEL, tn=D_MODEL, name="mm_dw_down")
    dw_up = matmul_tn(u, da, ti=D_MODEL, tn=D_MODEL, name="mm_dw_up")
    du = matmul(da, p["w_up"], trans_b=True, name="mm_du_up")
    dh_in, dg = rms_bwd(h, p["norm"], du, dh)
    return dh_in, dict(norm=dg, w_up=dw_up, w_down=dw_down)


def local_step(x, target, w):
    n_even, n_odd = (DEPTH + 1) // 2, DEPTH // 2
    h = jnp.concatenate([jnp.zeros((FRONT_PAD, D_MODEL), F32), w["meta_tokens"].astype(F32), x], axis=0)
    even_p, odd_p, mlp_p = [], [], []
    for i in range(n_even):
        cw = w["ssm_conv_w"][i]
        even_p.append(dict(
            norm=_row(w["mix_norm_even"][i]), w_in=_w_in_to_kernel(w["w_in"][i]),
            w_out=jnp.concatenate([w["w_out"][i][Q_W:], w["w_out"][i][:Q_W]], axis=0),
            cw_x=cw[:, :SSM_INNER], cw_bc=cw[:, SSM_INNER:], cb_x=_row(w["ssm_conv_b"][i][:SSM_INNER]),
            cb_bc=_row(w["ssm_conv_b"][i][SSM_INNER:]), dt_bias=_row(w["dt_bias"][i], LANE),
            a_log=_row(w["a_log"][i], LANE), d_exp=_row(jnp.repeat(w["d_skip"][i], SSM_HEAD_DIM)),
            ssm_norm=_row(w["ssm_norm_w"][i]), q_norm=_row(w["q_norm"][i]), k_norm=_row(w["k_norm"][i]),
            sinks=_row(w["sinks"][i], LANE)))
    for i in range(n_odd):
        odd_p.append(dict(
            norm=_row(w["mix_norm_odd"][i]), pw1_w=w["pw1_w"][i], pw1_b=_row(w["pw1_b"][i]),
            dw_w=jnp.pad(w["dw_w"][i], ((0, CONF_HALO - CONF_KERNEL), (0, 0))), dw_b=_row(w["dw_b"][i]),
            ln_g=_row(w["ln_g"][i]), ln_b=_row(w["ln_b"][i]), pw2_w=w["pw2_w"][i], pw2_b=_row(w["pw2_b"][i])))
    for layer in range(DEPTH):
        mlp_p.append(dict(norm=_row(w["mlp_norm"][layer]), w_up=w["w_up"][layer], w_down=w["w_down"][layer]))

    tape = []
    for layer in range(DEPTH):
        if layer % 2 == 0:
            h, saved = even_fwd(h, even_p[layer // 2])
        else:
            h, saved = conf_fwd(h, odd_p[layer // 2])
        tape.append(saved)
        h, saved = mlp_fwd(h, mlp_p[layer])
        tape.append(saved)
    dh, loss_row = loss_fwd_bwd(h, target)

    ge = [None] * n_even
    go = [None] * n_odd
    gm = [None] * DEPTH
    for layer in reversed(range(DEPTH)):
        dh, gm[layer] = mlp_bwd(dh, mlp_p[layer], tape.pop())
        if layer % 2 == 0:
            dh, ge[layer // 2] = even_bwd(dh, even_p[layer // 2], tape.pop())
        else:
            dh, go[layer // 2] = conf_bwd(dh, odd_p[layer // 2], tape.pop())

    stack = lambda gs, f: jnp.stack([f(g) for g in gs])
    grads = dict(
        meta_tokens=dh[FRONT_PAD:BLOCK],
        mix_norm_even=stack(ge, lambda g: g["norm"][0]),
        w_in=stack(ge, lambda g: _w_in_from_kernel(g["w_in"])),
        ssm_conv_w=stack(ge, lambda g: jnp.concatenate([g["cw_x"], g["cw_bc"]], axis=1)),
        ssm_conv_b=stack(ge, lambda g: jnp.concatenate([g["cb_x"][0], g["cb_bc"][0]])),
        dt_bias=stack(ge, lambda g: g["dt_bias"][0, :SSM_HEADS]),
        a_log=stack(ge, lambda g: g["a_log"][0, :SSM_HEADS]),
        d_skip=stack(ge, lambda g: g["d_skip"][0, :SSM_HEADS]),
        ssm_norm_w=stack(ge, lambda g: g["ssm_norm"][0]),
        q_norm=stack(ge, lambda g: g["q_norm"][0]),
        k_norm=stack(ge, lambda g: g["k_norm"][0]),
        sinks=stack(ge, lambda g: g["sinks"][0, :ATT_HEADS]),
        w_out=stack(ge, lambda g: jnp.concatenate([g["w_out"][SSM_INNER:], g["w_out"][:SSM_INNER]], axis=0)),
        mix_norm_odd=stack(go, lambda g: g["norm"][0]),
        pw1_w=stack(go, lambda g: g["pw1_w"]),
        pw1_b=stack(go, lambda g: g["pw1_b"][0]),
        dw_w=stack(go, lambda g: g["dw_w"][:CONF_KERNEL]),
        dw_b=stack(go, lambda g: g["dw_b"][0]),
        ln_g=stack(go, lambda g: g["ln_g"][0]),
        ln_b=stack(go, lambda g: g["ln_b"][0]),
        pw2_w=stack(go, lambda g: g["pw2_w"]),
        pw2_b=stack(go, lambda g: g["pw2_b"][0]),
        mlp_norm=stack(gm, lambda g: g["norm"][0]),
        w_up=stack(gm, lambda g: g["w_up"]),
        w_down=stack(gm, lambda g: g["w_down"]),
    )
    return loss_row[0, 0], dh[BLOCK:], grads


PARAMS = (
    ("meta_tokens", (16, 1024), 1), ("mix_norm_even", (2, 1024), None), ("w_in", (2, 1024, 3088), 2),
    ("ssm_conv_w", (2, 4, 1280), 2), ("ssm_conv_b", (2, 1280), None), ("dt_bias", (2, 16), None),
    ("a_log", (2, 16), None), ("d_skip", (2, 16), None), ("ssm_norm_w", (2, 1024), None), ("q_norm", (2, 64), None),
    ("k_norm", (2, 64), None), ("sinks", (2, 8), None), ("w_out", (2, 1536, 1024), 1), ("mix_norm_odd", (2, 1024), 1),
    ("pw1_w", (2, 1024, 2048), 2), ("pw1_b", (2, 2048), 1), ("dw_w", (2, 31, 1024), 2), ("dw_b", (2, 1024), 1),
    ("ln_g", (2, 1024), 1), ("ln_b", (2, 1024), 1), ("pw2_w", (2, 1024, 1024), 1), ("pw2_b", (2, 1024), 1),
    ("mlp_norm", (4, 1024), None), ("w_up", (4, 1024, 4096), 2), ("w_down", (4, 4096, 1024), 1),
)
MATMUL_WEIGHTS = ("w_in", "w_out", "pw1_w", "pw2_w", "w_up", "w_down")
PACK_ROW_ALIGN = 16 * PACK_W


def _block_shape(shape, axis):
    if axis is None:
        return tuple(shape)
    return tuple(s // N_DEV if a == axis else s for a, s in enumerate(shape))


def _numel(shape):
    return math.prod(shape)


def _pack(arrays, dtype):
    flat = jnp.concatenate([a.reshape(-1).astype(dtype) for a in arrays])
    n = flat.shape[0]
    padded = -(-n // PACK_ROW_ALIGN) * PACK_ROW_ALIGN
    return jnp.pad(flat, (0, padded - n)).reshape(-1, PACK_W)


def _pack_rows(arrays_by_dev, dtype):
    flat = jnp.concatenate([a.reshape(N_DEV, -1).astype(dtype) for a in arrays_by_dev], axis=1)
    n = flat.shape[1]
    padded = -(-n // PACK_ROW_ALIGN) * PACK_ROW_ALIGN
    return jnp.pad(flat, ((0, 0), (0, padded - n))).reshape(N_DEV, -1, PACK_W)


def _to_shards(full, axis):
    shape = full.shape
    split = full.reshape(shape[:axis] + (N_DEV, shape[axis] // N_DEV) + shape[axis + 1:])
    return jnp.moveaxis(split, axis, 0)


def _from_shards(blocks, axis):
    moved = jnp.moveaxis(blocks, 0, axis)
    shape = moved.shape
    return moved.reshape(shape[:axis] + (shape[axis] * shape[axis + 1],) + shape[axis + 2:])


_MESH = pl.DeviceIdType.MESH
_ANY = pl.BlockSpec(memory_space=pl.ANY)


def _mesh_place():
    x, y, c = lax.axis_index("x"), lax.axis_index("y"), lax.axis_index("c")
    return x, y, c


def _peer(x, y, c, rel):
    dx, dy, dc = (rel >> 2) & 1, (rel >> 1) & 1, rel & 1
    return (x ^ dx if dx else x, y ^ dy if dy else y, c ^ dc if dc else c)


def _dev_index(x, y, c):
    return 4 * x + 2 * y + c


def all_gather_weights(big, small):
    rb = big.shape[0]
    rs = small.shape[0]

    def body(big_ref, small_ref, big_out, small_out, send_sems, recv_sems, small_send, small_recv, local_sems):
        x, y, c = _mesh_place()
        me = (x, y, c)
        sibling = (x, y, 1 - c)
        chips = [(1 - x, y), (x, 1 - y), (1 - x, 1 - y)]

        def big_copy(k, block, to, src=None):
            dst = big_out.at[_dev_index(*block)]
            return pltpu.make_async_remote_copy(src_ref=dst if src is None else src, dst_ref=dst,
                                                send_sem=send_sems.at[k], recv_sem=recv_sems.at[k],
                                                device_id=to, device_id_type=_MESH)

        def small_copy(rel, block, to):
            return pltpu.make_async_remote_copy(src_ref=small_ref, dst_ref=small_out.at[_dev_index(*block)],
                                                send_sem=small_send.at[rel - 1], recv_sem=small_recv.at[rel - 1],
                                                device_id=to, device_id_type=_MESH)

        mine_big = pltpu.make_async_copy(big_ref, big_out.at[_dev_index(*me)], local_sems.at[0])
        mine_small = pltpu.make_async_copy(small_ref, small_out.at[_dev_index(*me)], local_sems.at[1])
        mine_big.start()
        mine_small.start()
        first = [big_copy(0, me, sibling, src=big_ref)]
        first += [big_copy(1 + j, me, (*chip, c), src=big_ref) for j, chip in enumerate(chips)]
        for cp in first:
            cp.start()
        smalls = [small_copy(rel, me, _peer(x, y, c, rel)) for rel in range(1, N_DEV)]
        for cp in smalls:
            cp.start()
        passed = [big_copy(4 + j, (*chip, c), sibling) for j, chip in enumerate(chips)]
        for j, chip in enumerate(chips):
            big_copy(1 + j, (*chip, c), me).wait_recv()
            passed[j].start()
        big_copy(0, sibling, me).wait_recv()
        for j, chip in enumerate(chips):
            big_copy(4 + j, (*chip, 1 - c), me).wait_recv()
        for rel in range(1, N_DEV):
            small_copy(rel, _peer(x, y, c, rel), me).wait_recv()
        for cp in first + passed + smalls:
            cp.wait_send()
        mine_big.wait()
        mine_small.wait()

    return pl.pallas_call(
        body, in_specs=[_ANY, _ANY], out_specs=[_ANY, _ANY],
        out_shape=[jax.ShapeDtypeStruct((N_DEV, rb, PACK_W), big.dtype),
                   jax.ShapeDtypeStruct((N_DEV, rs, PACK_W), small.dtype)],
        scratch_shapes=[pltpu.SemaphoreType.DMA((N_DEV - 1,)), pltpu.SemaphoreType.DMA((N_DEV - 1,)),
                        pltpu.SemaphoreType.DMA((N_DEV - 1,)), pltpu.SemaphoreType.DMA((N_DEV - 1,)),
                        pltpu.SemaphoreType.DMA((2,))],
        name="all_gather_weights",
    )(big, small)


def exchange_gradients(gpack):
    r = gpack.shape[1]

    def body(g_ref, out_ref, send_sems, recv_sems, local_sem):
        x, y, c = _mesh_place()
        me = _dev_index(x, y, c)

        def copy(rel):
            peer = _peer(x, y, c, rel)
            return pltpu.make_async_remote_copy(src_ref=g_ref.at[_dev_index(*peer)], dst_ref=out_ref.at[me],
                                                send_sem=send_sems.at[rel - 1], recv_sem=recv_sems.at[rel - 1],
                                                device_id=peer, device_id_type=_MESH)

        mine = pltpu.make_async_copy(g_ref.at[me], out_ref.at[me], local_sem)
        mine.start()
        copies = [copy(rel) for rel in range(1, N_DEV)]
        for cp in copies:
            cp.start()
        for rel in range(1, N_DEV):
            peer = _peer(x, y, c, rel)
            pltpu.make_async_remote_copy(src_ref=g_ref.at[me], dst_ref=out_ref.at[_dev_index(*peer)],
                                         send_sem=send_sems.at[rel - 1], recv_sem=recv_sems.at[rel - 1],
                                         device_id=peer, device_id_type=_MESH).wait_recv()
        for cp in copies:
            cp.wait_send()
        mine.wait()

    return pl.pallas_call(
        body, in_specs=[_ANY], out_specs=_ANY,
        out_shape=jax.ShapeDtypeStruct((N_DEV, r, PACK_W), gpack.dtype),
        scratch_shapes=[pltpu.SemaphoreType.DMA((N_DEV - 1,)), pltpu.SemaphoreType.DMA((N_DEV - 1,)),
                        pltpu.SemaphoreType.DMA(())],
        name="exchange_gradients",
    )(gpack)


def reduce_adamw(parts, w, m, v, tr):
    r = w.shape[0]

    def body(p_ref, w_ref, m_ref, v_ref, g_ref, d_ref, nm_ref, nv_ref):
        g = p_ref[0]
        for d in range(1, N_DEV):
            g = g + p_ref[d]
        g_ref[...] = g
        nm = ADAM_B1 * m_ref[...] + (1.0 - ADAM_B1) * g
        nv = ADAM_B2 * v_ref[...] + (1.0 - ADAM_B2) * (g * g)
        nm_ref[...] = nm
        nv_ref[...] = nv
        m_hat = nm / (1.0 - ADAM_B1 ** ADAM_STEP)
        v_hat = nv / (1.0 - ADAM_B2 ** ADAM_STEP)
        d_ref[...] = -ADAM_LR * (m_hat / (jnp.sqrt(v_hat) + ADAM_EPS) + ADAM_WD * w_ref[...])

    row = pl.BlockSpec((tr, PACK_W), lambda i: (i, 0))
    return pl.pallas_call(
        body, grid=(r // tr,),
        in_specs=[pl.BlockSpec((N_DEV, tr, PACK_W), lambda i: (0, i, 0)), row, row, row],
        out_specs=[row, row, row, row], out_shape=[jax.ShapeDtypeStruct((r, PACK_W), F32)] * 4,
        name="reduce_adamw", compiler_params=_params("arbitrary"),
    )(parts, w, m, v)


def _adamw_tile(rows):
    for tr in (224, 256, 192, 128, 112, 64, 48, 32, 16):
        if rows % tr == 0:
            return tr
    raise ValueError(rows)


def kernel(x, meta_tokens, mix_norm_even, w_in, ssm_conv_w, ssm_conv_b, dt_bias, a_log, d_skip, ssm_norm_w, q_norm, k_norm, sinks, w_out, mix_norm_odd, pw1_w, pw1_b, dw_w, dw_b, ln_g, ln_b, pw2_w, pw2_b, mlp_norm, w_up, w_down, loss_target, m_meta_tokens, m_mix_norm_even, m_w_in, m_ssm_conv_w, m_ssm_conv_b, m_dt_bias, m_a_log, m_d_skip, m_ssm_norm_w, m_q_norm, m_k_norm, m_sinks, m_w_out, m_mix_norm_odd, m_pw1_w, m_pw1_b, m_dw_w, m_dw_b, m_ln_g, m_ln_b, m_pw2_w, m_pw2_b, m_mlp_norm, m_w_up, m_w_down, v_meta_tokens, v_mix_norm_even, v_w_in, v_ssm_conv_w, v_ssm_conv_b, v_dt_bias, v_a_log, v_d_skip, v_ssm_norm_w, v_q_norm, v_k_norm, v_sinks, v_w_out, v_mix_norm_odd, v_pw1_w, v_pw1_b, v_dw_w, v_dw_b, v_ln_g, v_ln_b, v_pw2_w, v_pw2_b, v_mlp_norm, v_w_up, v_w_down):
    names = [p[0] for p in PARAMS]
    w_loc = dict(zip(names, (meta_tokens, mix_norm_even, w_in, ssm_conv_w, ssm_conv_b, dt_bias, a_log, d_skip, ssm_norm_w, q_norm, k_norm, sinks, w_out, mix_norm_odd, pw1_w, pw1_b, dw_w, dw_b, ln_g, ln_b, pw2_w, pw2_b, mlp_norm, w_up, w_down)))
    m_loc = dict(zip(names, (m_meta_tokens, m_mix_norm_even, m_w_in, m_ssm_conv_w, m_ssm_conv_b, m_dt_bias, m_a_log, m_d_skip, m_ssm_norm_w, m_q_norm, m_k_norm, m_sinks, m_w_out, m_mix_norm_odd, m_pw1_w, m_pw1_b, m_dw_w, m_dw_b, m_ln_g, m_ln_b, m_pw2_w, m_pw2_b, m_mlp_norm, m_w_up, m_w_down)))
    v_loc = dict(zip(names, (v_meta_tokens, v_mix_norm_even, v_w_in, v_ssm_conv_w, v_ssm_conv_b, v_dt_bias, v_a_log, v_d_skip, v_ssm_norm_w, v_q_norm, v_k_norm, v_sinks, v_w_out, v_mix_norm_odd, v_pw1_w, v_pw1_b, v_dw_w, v_dw_b, v_ln_g, v_ln_b, v_pw2_w, v_pw2_b, v_mlp_norm, v_w_up, v_w_down)))
    big = [p for p in PARAMS if p[0] in MATMUL_WEIGHTS]
    small = [p for p in PARAMS if p[2] is not None and p[0] not in MATMUL_WEIGHTS]
    sharded = [p for p in PARAMS if p[2] is not None]
    replicated = [p for p in PARAMS if p[2] is None]

    big_all, small_all = all_gather_weights(_pack([w_loc[n] for n, _, _ in big], MXU_DTYPE),
                                            _pack([w_loc[n] for n, _, _ in small], F32))

    def unpack_gathered(buf, plist):
        flat = buf.reshape(N_DEV, -1)
        out, off = {}, 0
        for n, shape, axis in plist:
            blk = _block_shape(shape, axis)
            size = _numel(blk)
            out[n] = _from_shards(flat[:, off:off + size].reshape((N_DEV,) + blk), axis)
            off += size
        return out

    w_full = {n: w_loc[n] for n, _, _ in replicated}
    w_full.update(unpack_gathered(big_all, big))
    w_full.update(unpack_gathered(small_all, small))

    loss_local, grad_x, g_full = local_step(x[0], loss_target[0], w_full)
    loss = lax.psum(loss_local, ("x", "y", "c"))

    pack_list = sharded + replicated
    by_dev = [_to_shards(g_full[n], axis) for n, _, axis in sharded]
    by_dev += [jnp.broadcast_to(g_full[n][None], (N_DEV,) + tuple(shape)) for n, shape, _ in replicated]
    parts = exchange_gradients(_pack_rows(by_dev, F32))

    pk = lambda d: _pack([d[n] for n, _, _ in pack_list], F32)
    rows = parts.shape[1]
    g_p, d_p, m_p, v_p = reduce_adamw(parts, pk(w_loc), pk(m_loc), pk(v_loc), _adamw_tile(rows))

    def unpack_local(buf):
        flat = buf.reshape(-1)
        out, off = {}, 0
        for n, shape, axis in pack_list:
            blk = _block_shape(shape, axis)
            size = _numel(blk)
            out[n] = flat[off:off + size].reshape(blk)
            off += size
        return out

    g_o, d_o, m_o, v_o = unpack_local(g_p), unpack_local(d_p), unpack_local(m_p), unpack_local(v_p)
    return (loss, grad_x[None], *[g_o[n] for n in names], *[d_o[n] for n in names], *[m_o[n] for n in names],
            *[v_o[n] for n in names])
```

```python
import functools
import math

import jax
import jax.numpy as jnp
from jax import lax
from jax.experimental import pallas as pl
from jax.experimental.pallas import tpu as pltpu

F32 = jnp.float32
MXU_DTYPE = jnp.bfloat16
GRAD_WIRE_DTYPE = jnp.bfloat16
HIGHEST = lax.Precision.HIGHEST

D_MODEL = 1024
N_META = 16
BLOCK = 128
FRONT_PAD = BLOCK - N_META
ATT_HEADS = 8
ATT_KV_HEADS = 2
HEAD_DIM = 64
SSM_HEADS = 16
SSM_HEAD_DIM = 64
SSM_INNER = 1024
SSM_GROUPS = 2
SSM_STATE = 64
SSM_CONV = 4
CONF_KERNEL = 31
D_FF = 4096
EPS = 1e-6
LN_EPS = 1e-5
Q_W = 512
KV_W = 128
IN_W = 3088
MIX_W = 1536
DEPTH = 4
N_DEV = 8

ADAM_LR = 0.001
ADAM_B1 = 0.9
ADAM_B2 = 0.999
ADAM_EPS = 1e-08
ADAM_WD = 0.01
ADAM_STEP = 10

PROJ_W = 3200
COL_Z, COL_XS, COL_Q, COL_BC, COL_K, COL_V, COL_DT = 0, 1024, 2048, 2560, 2816, 2944, 3072

ROW_TILE = 640
VMEM_LIMIT = 56 * 1024 * 1024
LANE = 128
PACK_W = 1024


def _params(*sem):
    return pltpu.CompilerParams(dimension_semantics=sem, vmem_limit_bytes=VMEM_LIMIT)


def _mx(x):
    return x.astype(MXU_DTYPE)


def _dot(a, b):
    return jnp.dot(_mx(a), _mx(b), preferred_element_type=F32)


def _dot_nt(a, b):
    return lax.dot_general(_mx(a), _mx(b), (((1,), (1,)), ((), ())), preferred_element_type=F32)


def _dot_tn(a, b):
    return lax.dot_general(_mx(a), _mx(b), (((0,), (0,)), ((), ())), preferred_element_type=F32)


def _dot_hi(a, b):
    return jnp.dot(a, b, precision=HIGHEST, preferred_element_type=F32)


def _dot_tn_hi(a, b):
    return lax.dot_general(a, b, (((0,), (0,)), ((), ())), precision=HIGHEST, preferred_element_type=F32)


def _sigmoid(x):
    return 1.0 / (1.0 + jnp.exp(-x))


def _row_ids(start, rows, cols):
    return start + lax.broadcasted_iota(jnp.int32, (rows, cols), 0)


def rms_fwd(h, g, name="rms_fwd"):
    m, d = h.shape
    tm = ROW_TILE

    def body(h_ref, g_ref, u_ref):
        x = h_ref[...]
        r = lax.rsqrt(jnp.mean(x * x, -1, keepdims=True) + EPS)
        u_ref[...] = ((x * r) * g_ref[...]).astype(u_ref.dtype)

    return pl.pallas_call(
        body, grid=(m // tm,),
        in_specs=[pl.BlockSpec((tm, d), lambda i: (i, 0)), pl.BlockSpec((1, d), lambda i: (0, 0))],
        out_specs=pl.BlockSpec((tm, d), lambda i: (i, 0)),
        out_shape=jax.ShapeDtypeStruct((m, d), MXU_DTYPE), name=name, compiler_params=_params("arbitrary"),
    )(h, g)


def rms_bwd(h, g, du, dh_out, name="rms_bwd"):
    m, d = h.shape
    tm = ROW_TILE

    def body(h_ref, g_ref, du_ref, dho_ref, dh_ref, dg_ref):
        @pl.when(pl.program_id(0) == 0)
        def _():
            dg_ref[...] = jnp.zeros_like(dg_ref)

        x = h_ref[...]
        du_ = du_ref[...]
        r = lax.rsqrt(jnp.mean(x * x, -1, keepdims=True) + EPS)
        gy = du_ * g_ref[...]
        dx = r * gy - x * ((r * r * r) * jnp.mean(x * gy, -1, keepdims=True))
        dh_ref[...] = dho_ref[...] + dx
        dg_ref[...] += jnp.sum(du_ * (x * r), axis=0, keepdims=True)

    row = pl.BlockSpec((tm, d), lambda i: (i, 0))
    vec = pl.BlockSpec((1, d), lambda i: (0, 0))
    return pl.pallas_call(
        body, grid=(m // tm,), in_specs=[row, vec, row, row], out_specs=[row, vec],
        out_shape=[jax.ShapeDtypeStruct((m, d), F32), jax.ShapeDtypeStruct((1, d), F32)],
        name=name, compiler_params=_params("arbitrary"),
    )(h, g, du, dh_out)


def loss_fwd_bwd(h, target, name="loss"):
    m, d = h.shape
    nb = m // BLOCK

    def body(h_ref, t_ref, dh_ref, l_ref):
        i = pl.program_id(0)

        @pl.when(i == 0)
        def _():
            l_ref[...] = jnp.zeros_like(l_ref)
            dh_ref[...] = jnp.zeros_like(dh_ref)

        @pl.when(i > 0)
        def _():
            e = h_ref[...] - t_ref[...]
            dh_ref[...] = e * (1.0 / d)
            s = jnp.sum(jnp.sum(e * e, axis=-1, keepdims=True), axis=0, keepdims=True)
            l_ref[...] += jnp.broadcast_to(s * (0.5 / d), l_ref.shape)

    return pl.pallas_call(
        body, grid=(nb,),
        in_specs=[pl.BlockSpec((BLOCK, d), lambda i: (i, 0)),
                  pl.BlockSpec((BLOCK, d), lambda i: (jnp.maximum(i - 1, 0), 0))],
        out_specs=[pl.BlockSpec((BLOCK, d), lambda i: (i, 0)), pl.BlockSpec((1, LANE), lambda i: (0, 0))],
        out_shape=[jax.ShapeDtypeStruct((m, d), F32), jax.ShapeDtypeStruct((1, LANE), F32)],
        name=name, compiler_params=_params("arbitrary"),
    )(h, target)


def col_sum(x, name="col_sum"):
    m, n = x.shape
    tm = ROW_TILE

    def body(x_ref, o_ref):
        @pl.when(pl.program_id(0) == 0)
        def _():
            o_ref[...] = jnp.zeros_like(o_ref)

        o_ref[...] += jnp.sum(x_ref[...].astype(F32), axis=0, keepdims=True)

    return pl.pallas_call(
        body, grid=(m // tm,), in_specs=[pl.BlockSpec((tm, n), lambda i: (i, 0))],
        out_specs=pl.BlockSpec((1, n), lambda i: (0, 0)), out_shape=jax.ShapeDtypeStruct((1, n), F32),
        name=name, compiler_params=_params("arbitrary"),
    )(x)


def matmul(a, b, *, b_kind="full", layer=0, trans_b=False, tn=None, epilogue=None, bias=None, extra=None,
           out_dtype=F32, name="matmul"):
    m, k = a.shape
    tm = ROW_TILE
    merge = False
    if b_kind == "full":
        n = b.shape[0] if trans_b else b.shape[1]
        tn = n if tn is None else tn
        b_spec = pl.BlockSpec((tn, k), lambda i, j: (j, 0)) if trans_b else pl.BlockSpec((k, tn), lambda i, j: (0, j))
    elif b_kind == "colshard":
        assert not trans_b and b.shape[2] == k
        tn = b.shape[3]
        n = N_DEV * tn
        b_spec = pl.BlockSpec((None, None, k, tn), lambda i, j: (j, layer, 0, 0))
    elif b_kind == "rowshard":
        ks, wn = b.shape[2], b.shape[3]
        if trans_b and tn == ks:
            assert wn == k
            n = N_DEV * ks
            b_spec = pl.BlockSpec((None, None, ks, wn), lambda i, j: (j, layer, 0, 0))
        else:
            assert tn is None
            merge = True
            n = N_DEV * ks if trans_b else wn
            assert (wn if trans_b else N_DEV * ks) == k
            tn = n
            b_spec = pl.BlockSpec((N_DEV, None, ks, wn), lambda i, j: (0, layer, 0, 0))
    else:
        raise ValueError(b_kind)
    has_bias = bias is not None
    has_extra = extra is not None

    def body(*refs):
        a_ref, b_ref = refs[0], refs[1]
        pos = 2
        bias_ref = extra_ref = None
        if has_bias:
            bias_ref = refs[pos]
            pos += 1
        if has_extra:
            extra_ref = refs[pos]
            pos += 1
        outs = refs[pos:]
        w = b_ref[...]
        if merge:
            w = w.reshape(N_DEV * w.shape[1], w.shape[2])
        if trans_b:
            acc = _dot_nt(a_ref[...], w)
        else:
            acc = _dot(a_ref[...], w)
        if has_bias:
            acc = acc + bias_ref[...]
        if epilogue is None:
            outs[0][...] = acc.astype(outs[0].dtype)
        elif epilogue == "relu2":
            outs[0][...] = acc
            r = jnp.maximum(acc, 0.0)
            outs[1][...] = (r * r).astype(outs[1].dtype)
        elif epilogue == "drelu2":
            outs[0][...] = (acc * (2.0 * jnp.maximum(extra_ref[...], 0.0))).astype(outs[0].dtype)
        elif epilogue == "resid":
            rows = _row_ids(pl.program_id(0) * tm, tm, tn)
            outs[0][...] = extra_ref[...] + jnp.where(rows >= FRONT_PAD, acc, 0.0)
        else:
            raise ValueError(epilogue)

    in_specs = [pl.BlockSpec((tm, k), lambda i, j: (i, 0)), b_spec]
    args = [a, b]
    if has_bias:
        in_specs.append(pl.BlockSpec((1, tn), lambda i, j: (0, j)))
        args.append(bias)
    if has_extra:
        in_specs.append(pl.BlockSpec((tm, tn), lambda i, j: (i, j)))
        args.append(extra)
    tile = pl.BlockSpec((tm, tn), lambda i, j: (i, j))
    if epilogue == "relu2":
        out_specs = [tile, tile]
        out_shape = [jax.ShapeDtypeStruct((m, n), F32), jax.ShapeDtypeStruct((m, n), MXU_DTYPE)]
    else:
        out_specs = tile
        out_shape = jax.ShapeDtypeStruct((m, n), out_dtype)
    return pl.pallas_call(
        body, grid=(m // tm, n // tn), in_specs=in_specs, out_specs=out_specs, out_shape=out_shape,
        name=name, compiler_params=_params("arbitrary", "arbitrary"),
    )(*args)


def matmul_nt_acc(a, b, *, layer=0, name="matmul_nt_acc"):
    m = a.shape[0]
    k, ns = b.shape[2], b.shape[3]
    tm = ROW_TILE

    def body(a_ref, b_ref, o_ref):
        @pl.when(pl.program_id(1) == 0)
        def _():
            o_ref[...] = jnp.zeros_like(o_ref)

        o_ref[...] += _dot_nt(a_ref[...], b_ref[...])

    return pl.pallas_call(
        body, grid=(m // tm, N_DEV),
        in_specs=[pl.BlockSpec((tm, ns), lambda i, d: (i, d)),
                  pl.BlockSpec((None, None, k, ns), lambda i, d: (d, layer, 0, 0))],
        out_specs=pl.BlockSpec((tm, k), lambda i, d: (i, 0)),
        out_shape=jax.ShapeDtypeStruct((m, k), F32), name=name, compiler_params=_params("arbitrary", "arbitrary"),
    )(a, b)


def matmul_tn(x, dy, *, ti, tn, colshard=False, out_dtype=F32, name="matmul_tn"):
    m, k1 = x.shape
    n = dy.shape[1]
    tm = ROW_TILE
    last = m // tm - 1

    def body(x_ref, dy_ref, o_ref, acc_ref):
        r = pl.program_id(2)

        @pl.when(r == 0)
        def _():
            acc_ref[...] = jnp.zeros_like(acc_ref)

        acc_ref[...] += _dot_tn(x_ref[...], dy_ref[...])

        @pl.when(r == last)
        def _():
            o_ref[...] = acc_ref[...].astype(o_ref.dtype)

    if colshard:
        assert n == N_DEV * tn
        out_specs = pl.BlockSpec((None, ti, tn), lambda i, j, r: (j, i, 0))
        out_shape = jax.ShapeDtypeStruct((N_DEV, k1, tn), out_dtype)
    else:
        out_specs = pl.BlockSpec((ti, tn), lambda i, j, r: (i, j))
        out_shape = jax.ShapeDtypeStruct((k1, n), out_dtype)
    return pl.pallas_call(
        body, grid=(k1 // ti, n // tn, m // tm),
        in_specs=[pl.BlockSpec((tm, ti), lambda i, j, r: (r, i)), pl.BlockSpec((tm, tn), lambda i, j, r: (r, j))],
        out_specs=out_specs, out_shape=out_shape, scratch_shapes=[pltpu.VMEM((ti, tn), F32)], name=name,
        compiler_params=_params("arbitrary", "arbitrary", "arbitrary"),
    )(x, dy)


_ATT_SCALE = HEAD_DIM ** -0.5


def _alibi_slope(h):
    return 2.0 ** (-8.0 * (h + 1) / ATT_HEADS)


def _head_norm(x, w, nheads):
    outs, rs = [], []
    for h in range(nheads):
        xh = x[:, HEAD_DIM * h:HEAD_DIM * (h + 1)]
        r = lax.rsqrt(jnp.mean(xh * xh, -1, keepdims=True) + EPS)
        outs.append((xh * r) * w)
        rs.append(r)
    return outs, rs


def _head_norm_bwd(x, r, w, dy):
    gy = dy * w
    dx = r * gy - x * ((r * r * r) * jnp.mean(x * gy, -1, keepdims=True))
    dw = jnp.sum(dy * (x * r), axis=0, keepdims=True)
    return dx, dw


def _att_masks(b):
    row = lax.broadcasted_iota(jnp.int32, (BLOCK, BLOCK), 0)
    col = lax.broadcasted_iota(jnp.int32, (BLOCK, BLOCK), 1)
    valid_c = jnp.logical_and(col <= row, b >= 1)
    dist_c = (row - col).astype(F32)
    valid_p = jnp.logical_and(col > row, b >= 2)
    dist_p = (row - col + BLOCK).astype(F32)
    q_pos = b * BLOCK + row - FRONT_PAD
    meta_j = col - FRONT_PAD
    valid_m = jnp.logical_and(col >= FRONT_PAD, q_pos >= meta_j)
    dist_m = jnp.minimum(q_pos - meta_j, BLOCK).astype(F32)
    return (valid_c, dist_c), (valid_p, dist_p), (valid_m, dist_m)


def _att_probs(qh, keys, masks, slope, sink):
    ss = []
    for kh, (valid, dist) in zip(keys, masks):
        s = _dot_nt(qh, kh) * _ATT_SCALE
        ss.append(jnp.where(valid, s - slope * dist, -1e30))
    mx = sink
    for s in ss:
        mx = jnp.maximum(mx, jnp.max(s, axis=-1, keepdims=True))
    ps = [jnp.exp(s - mx) for s in ss]
    p_sink = jnp.exp(sink - mx)
    den = p_sink
    for p in ps:
        den = den + jnp.sum(p, axis=-1, keepdims=True)
    inv = 1.0 / den
    return [p * inv for p in ps], p_sink * inv


def attention_fwd(proj, q_w, k_w, sinks, name="att_fwd"):
    m = proj.shape[0]
    nb = m // BLOCK
    cq, ck, cv = COL_Q // Q_W, COL_K // KV_W, COL_V // KV_W

    def body(q_ref, kc_ref, vc_ref, kp_ref, vp_ref, km_ref, vm_ref, qw_ref, kw_ref, sk_ref, o_ref):
        b = pl.program_id(0)
        qn, _ = _head_norm(q_ref[...], qw_ref[...], ATT_HEADS)
        kw = kw_ref[...]
        kcn, _ = _head_norm(kc_ref[...], kw, ATT_KV_HEADS)
        kpn, _ = _head_norm(kp_ref[...], kw, ATT_KV_HEADS)
        kmn, _ = _head_norm(km_ref[...], kw, ATT_KV_HEADS)
        vc, vp, vm = vc_ref[...], vp_ref[...], vm_ref[...]
        masks = _att_masks(b)
        outs = []
        for h in range(ATT_HEADS):
            g = h // (ATT_HEADS // ATT_KV_HEADS)
            sl = slice(HEAD_DIM * g, HEAD_DIM * (g + 1))
            probs, _ = _att_probs(qn[h], (kcn[g], kpn[g], kmn[g]), masks, _alibi_slope(h), sk_ref[0:1, h:h + 1])
            o = _dot(probs[0], vc[:, sl]) + _dot(probs[1], vp[:, sl]) + _dot(probs[2], vm[:, sl])
            outs.append(o)
        o_ref[...] = jnp.concatenate(outs, axis=1).astype(o_ref.dtype)

    prev = lambda i: jnp.maximum(i - 1, 0)
    vec = lambda w: pl.BlockSpec((1, w), lambda i: (0, 0))
    return pl.pallas_call(
        body, grid=(nb,),
        in_specs=[pl.BlockSpec((BLOCK, Q_W), lambda i: (i, cq)),
                  pl.BlockSpec((BLOCK, KV_W), lambda i: (i, ck)), pl.BlockSpec((BLOCK, KV_W), lambda i: (i, cv)),
                  pl.BlockSpec((BLOCK, KV_W), lambda i: (prev(i), ck)), pl.BlockSpec((BLOCK, KV_W), lambda i: (prev(i), cv)),
                  pl.BlockSpec((BLOCK, KV_W), lambda i: (0, ck)), pl.BlockSpec((BLOCK, KV_W), lambda i: (0, cv)),
                  vec(HEAD_DIM), vec(HEAD_DIM), vec(LANE)],
        out_specs=pl.BlockSpec((BLOCK, Q_W), lambda i: (i, 0)),
        out_shape=jax.ShapeDtypeStruct((m, Q_W), MXU_DTYPE), name=name, compiler_params=_params("arbitrary"),
    )(proj, proj, proj, proj, proj, proj, proj, q_w, k_w, sinks)


def attention_bwd(proj, dmix, q_w, k_w, sinks, name="att_bwd"):
    m = proj.shape[0]
    nb = m // BLOCK
    cq, ck, cv = COL_Q // Q_W, COL_K // KV_W, COL_V // KV_W
    c_datt = 0
    grp = ATT_HEADS // ATT_KV_HEADS

    def body(do_ref, q_ref, kc_ref, vc_ref, kp_ref, vp_ref, km_ref, vm_ref, qw_ref, kw_ref, sk_ref,
             dq_ref, dk_ref, dv_ref, dqw_ref, dkw_ref, dsk_ref, car_k, car_v, met_k, met_v):
        i = pl.program_id(0)
        b = nb - 1 - i

        @pl.when(i == 0)
        def _():
            car_k[...] = jnp.zeros_like(car_k)
            car_v[...] = jnp.zeros_like(car_v)
            met_k[...] = jnp.zeros_like(met_k)
            met_v[...] = jnp.zeros_like(met_v)
            dqw_ref[...] = jnp.zeros_like(dqw_ref)
            dkw_ref[...] = jnp.zeros_like(dkw_ref)
            dsk_ref[...] = jnp.zeros_like(dsk_ref)

        q = q_ref[...]
        kc = kc_ref[...]
        qw, kw = qw_ref[...], kw_ref[...]
        qn, qr = _head_norm(q, qw, ATT_HEADS)
        kcn, kcr = _head_norm(kc, kw, ATT_KV_HEADS)
        kpn, _ = _head_norm(kp_ref[...], kw, ATT_KV_HEADS)
        kmn, _ = _head_norm(km_ref[...], kw, ATT_KV_HEADS)
        vc, vp, vm = vc_ref[...], vp_ref[...], vm_ref[...]
        do = do_ref[...]
        masks = _att_masks(b)
        zero = jnp.zeros((BLOCK, HEAD_DIM), F32)
        dkn = [[zero, zero] for _ in range(3)]
        dvv = [[zero, zero] for _ in range(3)]
        dqs = []
        dqw = jnp.zeros((1, HEAD_DIM), F32)
        lane = lax.broadcasted_iota(jnp.int32, (1, LANE), 1)
        dsk = jnp.zeros((1, LANE), F32)
        for h in range(ATT_HEADS):
            g = h // grp
            sl = slice(HEAD_DIM * g, HEAD_DIM * (g + 1))
            keys = (kcn[g], kpn[g], kmn[g])
            vals = (vc[:, sl], vp[:, sl], vm[:, sl])
            probs, p_sink = _att_probs(qn[h], keys, masks, _alibi_slope(h), sk_ref[0:1, h:h + 1])
            do_h = do[:, HEAD_DIM * h:HEAD_DIM * (h + 1)]
            o = _dot(probs[0], vals[0]) + _dot(probs[1], vals[1]) + _dot(probs[2], vals[2])
            delta = jnp.sum(do_h * o, axis=-1, keepdims=True)
            dqn = jnp.zeros((BLOCK, HEAD_DIM), F32)
            for t in range(3):
                dp = _dot_nt(do_h, vals[t])
                ds = probs[t] * (dp - delta)
                dqn = dqn + _dot(ds, keys[t])
                dkn[t][g] = dkn[t][g] + _dot_tn(ds, qn[h]) * _ATT_SCALE
                dvv[t][g] = dvv[t][g] + _dot_tn(probs[t], do_h)
            dqn = dqn * _ATT_SCALE
            dq_h, dw = _head_norm_bwd(q[:, HEAD_DIM * h:HEAD_DIM * (h + 1)], qr[h], qw, dqn)
            dqs.append(dq_h)
            dqw = dqw + dw
            dsk = dsk + jnp.where(lane == h, -jnp.sum(p_sink * delta, axis=0, keepdims=True), 0.0)
        dq_ref[...] = jnp.concatenate(dqs, axis=1)
        dqw_ref[...] += dqw
        dsk_ref[...] += dsk

        met_k[...] += jnp.concatenate(dkn[2], axis=1)
        met_v[...] += jnp.concatenate(dvv[2], axis=1)
        dkn_tot = jnp.concatenate(dkn[0], axis=1) + car_k[...]
        dv_tot = jnp.concatenate(dvv[0], axis=1) + car_v[...]
        first = (b == 0).astype(F32)
        dkn_tot = dkn_tot + first * met_k[...]
        dv_tot = dv_tot + first * met_v[...]
        car_k[...] = jnp.concatenate(dkn[1], axis=1)
        car_v[...] = jnp.concatenate(dvv[1], axis=1)
        dks = []
        dkw = jnp.zeros((1, HEAD_DIM), F32)
        for g in range(ATT_KV_HEADS):
            sl = slice(HEAD_DIM * g, HEAD_DIM * (g + 1))
            dk_g, dw = _head_norm_bwd(kc[:, sl], kcr[g], kw, dkn_tot[:, sl])
            dks.append(dk_g)
            dkw = dkw + dw
        dk_ref[...] = jnp.concatenate(dks, axis=1)
        dv_ref[...] = dv_tot
        dkw_ref[...] += dkw

    blk = lambda i: nb - 1 - i
    prev = lambda i: jnp.maximum(nb - 2 - i, 0)
    vec = lambda w: pl.BlockSpec((1, w), lambda i: (0, 0))
    kv_scratch = pltpu.VMEM((BLOCK, KV_W), F32)
    return pl.pallas_call(
        body, grid=(nb,),
        in_specs=[pl.BlockSpec((BLOCK, Q_W), lambda i: (blk(i), c_datt)),
                  pl.BlockSpec((BLOCK, Q_W), lambda i: (blk(i), cq)),
                  pl.BlockSpec((BLOCK, KV_W), lambda i: (blk(i), ck)), pl.BlockSpec((BLOCK, KV_W), lambda i: (blk(i), cv)),
                  pl.BlockSpec((BLOCK, KV_W), lambda i: (prev(i), ck)), pl.BlockSpec((BLOCK, KV_W), lambda i: (prev(i), cv)),
                  pl.BlockSpec((BLOCK, KV_W), lambda i: (0, ck)), pl.BlockSpec((BLOCK, KV_W), lambda i: (0, cv)),
                  vec(HEAD_DIM), vec(HEAD_DIM), vec(LANE)],
        out_specs=[pl.BlockSpec((BLOCK, Q_W), lambda i: (blk(i), 0)),
                   pl.BlockSpec((BLOCK, KV_W), lambda i: (blk(i), 0)), pl.BlockSpec((BLOCK, KV_W), lambda i: (blk(i), 0)),
                   vec(HEAD_DIM), vec(HEAD_DIM), vec(LANE)],
        out_shape=[jax.ShapeDtypeStruct((m, Q_W), F32), jax.ShapeDtypeStruct((m, KV_W), F32),
                   jax.ShapeDtypeStruct((m, KV_W), F32), jax.ShapeDtypeStruct((1, HEAD_DIM), F32),
                   jax.ShapeDtypeStruct((1, HEAD_DIM), F32), jax.ShapeDtypeStruct((1, LANE), F32)],
        scratch_shapes=[kv_scratch, kv_scratch, kv_scratch, kv_scratch],
        name=name, compiler_params=_params("arbitrary"),
    )(dmix, proj, proj, proj, proj, proj, proj, proj, q_w, k_w, sinks)


HALO = 8
GROUP_W = SSM_INNER // SSM_GROUPS
HEADS_PER_GROUP = SSM_HEADS // SSM_GROUPS


def _head_expand():
    h = jnp.arange(LANE)[:, None]
    c = jnp.arange(SSM_INNER)[None, :]
    return (c // SSM_HEAD_DIM == h).astype(F32)


def _softplus(x):
    return jnp.maximum(x, 0.0) + jnp.log1p(jnp.exp(-jnp.abs(x)))


def _ssd_decays(dt, a_log_row):
    row = lax.broadcasted_iota(jnp.int32, (BLOCK, BLOCK), 0)
    col = lax.broadcasted_iota(jnp.int32, (BLOCK, BLOCK), 1)
    lower = row >= col
    a = -jnp.exp(a_log_row)
    a_cs = _dot_hi(lower.astype(F32), dt * a)
    return a, a_cs, lower


def _decay_matrix(a_cs, a_cs_t, h, lower):
    diff = a_cs[:, h:h + 1] - a_cs_t[h:h + 1, :]
    return jnp.where(lower, jnp.exp(jnp.where(lower, diff, 0.0)), 0.0)


def _conv_taps(s_ref, w_ref, first, rows):
    acc = w_ref[0:1, :] * s_ref[pl.ds(first, rows), :]
    for j in range(1, SSM_CONV):
        acc = acc + w_ref[j:j + 1, :] * s_ref[pl.ds(first + j, rows), :]
    return acc


def ssd_fwd(proj, cw_x, cw_bc, cb_x, cb_bc, dt_bias, a_log, d_exp, norm_w, name="ssd_fwd"):
    m = proj.shape[0]
    nb = m // BLOCK
    expand = _head_expand()
    expand_t = expand.T

    def body(z_ref, xs_ref, bc_ref, dtr_ref, cwx_ref, cwbc_ref, cbx_ref, cbbc_ref, dtb_ref, alog_ref, dexp_ref,
             nw_ref, e_ref, et_ref, out_ref, prex_ref, prebc_ref, dt_ref, ypre_ref, st_ref, sx, sbc, state):
        c = pl.program_id(0)

        @pl.when(c == 0)
        def _():
            sx[0:HALO, :] = jnp.zeros((HALO, SSM_INNER), F32)
            sbc[0:HALO, :] = jnp.zeros((HALO, 2 * LANE), F32)
            state[...] = jnp.zeros_like(state)

        sx[HALO:HALO + BLOCK, :] = xs_ref[...]
        sbc[HALO:HALO + BLOCK, :] = bc_ref[...]
        first = HALO - (SSM_CONV - 1)
        pre_x = _conv_taps(sx, cwx_ref, first, BLOCK) + cbx_ref[...]
        pre_bc = _conv_taps(sbc, cwbc_ref, first, BLOCK) + cbbc_ref[...]
        sx[0:HALO, :] = xs_ref[BLOCK - HALO:BLOCK, :]
        sbc[0:HALO, :] = bc_ref[BLOCK - HALO:BLOCK, :]
        prex_ref[...] = pre_x
        prebc_ref[...] = pre_bc
        xc = pre_x * _sigmoid(pre_x)
        bcv = pre_bc * _sigmoid(pre_bc)

        rows = _row_ids(c * BLOCK, BLOCK, LANE)
        lanes = lax.broadcasted_iota(jnp.int32, (BLOCK, LANE), 1)
        live = jnp.logical_and(rows >= FRONT_PAD, lanes < SSM_HEADS)
        dt = jnp.where(live, _softplus(dtr_ref[...] + dtb_ref[...]), 0.0)
        dt_ref[...] = dt
        a, a_cs, lower = _ssd_decays(dt, alog_ref[...])
        a_cs_t = a_cs.T
        dt_t = dt.T
        e = e_ref[...]
        es_full = _dot_hi(jnp.exp(a_cs), e)
        wx_full = _dot_hi(jnp.exp(a_cs[BLOCK - 1:BLOCK, :] - a_cs) * dt, e)
        end_col = jnp.exp(a_cs_t[:, BLOCK - 1:BLOCK])
        dec_full = _dot_hi(et_ref[...], jnp.broadcast_to(end_col, (LANE, SSM_STATE)))

        st_ref[0] = state[...]
        ys = []
        for g in range(SSM_GROUPS):
            b_g = bcv[:, SSM_STATE * g:SSM_STATE * (g + 1)]
            c_g = bcv[:, LANE + SSM_STATE * g:LANE + SSM_STATE * (g + 1)]
            gs = slice(GROUP_W * g, GROUP_W * (g + 1))
            cb = _dot_nt(c_g, b_g)
            yd = []
            for hh in range(HEADS_PER_GROUP):
                h = g * HEADS_PER_GROUP + hh
                w = cb * _decay_matrix(a_cs, a_cs_t, h, lower) * dt_t[h:h + 1, :]
                yd.append(_dot(w, xc[:, SSM_HEAD_DIM * h:SSM_HEAD_DIM * (h + 1)]))
            h_g = state[gs, :]
            y_off = _dot_nt(c_g, h_g) * es_full[:, gs]
            ys.append(jnp.concatenate(yd, axis=1) + y_off)
            new_state = _dot_tn(xc[:, gs] * wx_full[:, gs], b_g)
            state[gs, :] = h_g * dec_full[gs, :] + new_state
        y_pre = jnp.concatenate(ys, axis=1) + xc * dexp_ref[...]
        ypre_ref[...] = y_pre
        z = z_ref[...]
        gt = y_pre * (z * _sigmoid(z))
        outs = []
        for g in range(SSM_GROUPS):
            gg = gt[:, GROUP_W * g:GROUP_W * (g + 1)]
            r = lax.rsqrt(jnp.mean(gg * gg, -1, keepdims=True) + EPS)
            outs.append(gg * r)
        out_ref[...] = (jnp.concatenate(outs, axis=1) * nw_ref[...]).astype(out_ref.dtype)

    full = lambda s: pl.BlockSpec(s, lambda i: (0,) * len(s))
    rowblk = lambda w, cidx: pl.BlockSpec((BLOCK, w), lambda i: (i, cidx))
    return pl.pallas_call(
        body, grid=(nb,),
        in_specs=[rowblk(SSM_INNER, COL_Z // SSM_INNER), rowblk(SSM_INNER, COL_XS // SSM_INNER),
                  rowblk(2 * LANE, COL_BC // (2 * LANE)), rowblk(LANE, COL_DT // LANE),
                  full((SSM_CONV, SSM_INNER)), full((SSM_CONV, 2 * LANE)), full((1, SSM_INNER)), full((1, 2 * LANE)),
                  full((1, LANE)), full((1, LANE)), full((1, SSM_INNER)), full((1, SSM_INNER)),
                  full((LANE, SSM_INNER)), full((SSM_INNER, LANE))],
        out_specs=[rowblk(SSM_INNER, 0), rowblk(SSM_INNER, 0), rowblk(2 * LANE, 0), rowblk(LANE, 0),
                   rowblk(SSM_INNER, 0), pl.BlockSpec((1, SSM_INNER, SSM_STATE), lambda i: (i, 0, 0))],
        out_shape=[jax.ShapeDtypeStruct((m, SSM_INNER), MXU_DTYPE), jax.ShapeDtypeStruct((m, SSM_INNER), F32),
                   jax.ShapeDtypeStruct((m, 2 * LANE), F32), jax.ShapeDtypeStruct((m, LANE), F32),
                   jax.ShapeDtypeStruct((m, SSM_INNER), F32), jax.ShapeDtypeStruct((nb, SSM_INNER, SSM_STATE), F32)],
        scratch_shapes=[pltpu.VMEM((HALO + BLOCK, SSM_INNER), F32), pltpu.VMEM((HALO + BLOCK, 2 * LANE), F32),
                        pltpu.VMEM((SSM_INNER, SSM_STATE), F32)],
        name=name, compiler_params=_params("arbitrary"),
    )(proj, proj, proj, proj, cw_x, cw_bc, cb_x, cb_bc, dt_bias, a_log, d_exp, norm_w, expand, expand_t)


def ssd_bwd(proj, dmix, pre_x, pre_bc, dt, y_pre, states, cw_x, cw_bc, dt_bias, a_log, d_exp, norm_w,
            name="ssd_bwd"):
    m = proj.shape[0]
    nb = m // BLOCK
    expand = _head_expand()
    expand_t = expand.T

    def body(do0_ref, do1_ref, z_ref, xs_ref, xsp_ref, bc_ref, bcp_ref, dtr_ref, prex_ref, prebc_ref, dt_ref, ypre_ref,
             st_ref,
             cwx_ref, cwbc_ref, dtb_ref, alog_ref, dexp_ref, nw_ref, e_ref, et_ref,
             dz_ref, dxs_ref, dbc_ref, ddt_ref, dcwx_ref, dcwbc_ref, dcbx_ref, dcbbc_ref, ddtb_ref, dalog_ref,
             dd_ref, dnw_ref,
             dstate, hnext, tx, tbc, sx, sbc, dlane):
        i = pl.program_id(0)
        c = nb - 1 - i

        @pl.when(i == 0)
        def _():
            dstate[...] = jnp.zeros_like(dstate)
            hnext[...] = jnp.zeros_like(hnext)
            tx[BLOCK:BLOCK + HALO, :] = jnp.zeros((HALO, SSM_INNER), F32)
            tbc[BLOCK:BLOCK + HALO, :] = jnp.zeros((HALO, 2 * LANE), F32)
            dlane[...] = jnp.zeros_like(dlane)
            for r in (dcwx_ref, dcwbc_ref, dcbx_ref, dcbbc_ref, ddtb_ref, dalog_ref, dd_ref, dnw_ref):
                r[...] = jnp.zeros_like(r)

        e = e_ref[...]
        et = et_ref[...]
        pre_x = prex_ref[...]
        pre_bc = prebc_ref[...]
        sig_x = _sigmoid(pre_x)
        sig_bc = _sigmoid(pre_bc)
        xc = pre_x * sig_x
        bcv = pre_bc * sig_bc
        dt = dt_ref[...]
        a, a_cs, lower = _ssd_decays(dt, alog_ref[...])
        a_cs_t = a_cs.T
        es_full = _dot_hi(jnp.exp(a_cs), e)
        ed_full = _dot_hi(jnp.exp(a_cs[BLOCK - 1:BLOCK, :] - a_cs), e)
        dt_full = _dot_hi(dt, e)
        end_col = jnp.exp(a_cs_t[:, BLOCK - 1:BLOCK])
        dec_full = _dot_hi(et, jnp.broadcast_to(end_col, (LANE, SSM_STATE)))
        dexp = dexp_ref[...]

        z = z_ref[...]
        zs = _sigmoid(z)
        sz = z * zs
        y_pre = ypre_ref[...]
        gt = y_pre * sz
        do = jnp.concatenate([do0_ref[...], do1_ref[...]], axis=1)
        nw = nw_ref[...]
        dgt = []
        dnw = []
        for g in range(SSM_GROUPS):
            gs = slice(GROUP_W * g, GROUP_W * (g + 1))
            gg = gt[:, gs]
            r = lax.rsqrt(jnp.mean(gg * gg, -1, keepdims=True) + EPS)
            gn = do[:, gs] * nw[:, gs]
            dgt.append(r * gn - gg * ((r * r * r) * jnp.mean(gg * gn, -1, keepdims=True)))
            dnw.append(jnp.sum(do[:, gs] * (gg * r), axis=0, keepdims=True))
        dgt = jnp.concatenate(dgt, axis=1)
        dnw_ref[...] += jnp.concatenate(dnw, axis=1)
        dy = dgt * sz
        dz_ref[...] = dgt * y_pre * (zs * (1.0 + z * (1.0 - zs)))
        dlane[...] += jnp.sum(dy * xc, axis=0, keepdims=True)
        xd = xc * dt_full

        lane_id = lax.broadcasted_iota(jnp.int32, (BLOCK, LANE), 1)
        sub_id = lax.broadcasted_iota(jnp.int32, (LANE, BLOCK), 0)
        ds_to = jnp.zeros((BLOCK, LANE), F32)
        ds_from_t = jnp.zeros((LANE, BLOCK), F32)
        dxd_parts, inter_parts = [], []
        dbs, dcs = [], []
        for g in range(SSM_GROUPS):
            gs = slice(GROUP_W * g, GROUP_W * (g + 1))
            b_g = bcv[:, SSM_STATE * g:SSM_STATE * (g + 1)]
            c_g = bcv[:, LANE + SSM_STATE * g:LANE + SSM_STATE * (g + 1)]
            cb = _dot_nt(c_g, b_g)
            dcb = jnp.zeros((BLOCK, BLOCK), F32)
            dxd_h = []
            for hh in range(HEADS_PER_GROUP):
                h = g * HEADS_PER_GROUP + hh
                hs = slice(SSM_HEAD_DIM * h, SSM_HEAD_DIM * (h + 1))
                lm = _decay_matrix(a_cs, a_cs_t, h, lower)
                dy_h = dy[:, hs]
                gl = _dot_nt(dy_h, xd[:, hs]) * lm
                dcb = dcb + gl
                e_h = gl * cb
                ds_to = ds_to + jnp.where(lane_id == h, jnp.sum(e_h, axis=-1, keepdims=True), 0.0)
                ds_from_t = ds_from_t + jnp.where(sub_id == h, jnp.sum(e_h, axis=0, keepdims=True), 0.0)
                dxd_h.append(_dot_tn(cb * lm, dy_h))
            h_g = st_ref[0, gs, :]
            dh_g = dstate[gs, :]
            dys_g = dy[:, gs] * es_full[:, gs]
            xde_g = xd[:, gs] * ed_full[:, gs]
            dcs.append(_dot(dcb, b_g) + _dot(dys_g, h_g))
            dbs.append(_dot_tn(dcb, c_g) + _dot(xde_g, dh_g))
            y_off = _dot_nt(c_g, h_g) * es_full[:, gs]
            dxd_state = _dot_nt(b_g, dh_g) * ed_full[:, gs]
            inter_parts.append(dy[:, gs] * y_off - xd[:, gs] * dxd_state)
            dxd_parts.append(jnp.concatenate(dxd_h, axis=1) + dxd_state)
            dstate[gs, :] = dh_g * dec_full[gs, :] + _dot_tn(dys_g, c_g)
            if g == 0:
                end_dot = hnext[gs, :] * dh_g
            else:
                end_dot = jnp.concatenate([end_dot, hnext[gs, :] * dh_g], axis=0)
        dxd = jnp.concatenate(dxd_parts, axis=1)
        hnext[...] = st_ref[0]

        ds = ds_to - ds_from_t.T + _dot_hi(jnp.concatenate(inter_parts, axis=1), et)
        ds_end = jnp.sum(_dot_tn_hi(end_dot, et), axis=0, keepdims=True)
        rows_l = lax.broadcasted_iota(jnp.int32, (BLOCK, LANE), 0)
        ds = ds + jnp.where(rows_l == BLOCK - 1, ds_end, 0.0)
        row = lax.broadcasted_iota(jnp.int32, (BLOCK, BLOCK), 0)
        col = lax.broadcasted_iota(jnp.int32, (BLOCK, BLOCK), 1)
        dadt = _dot_hi((col >= row).astype(F32), ds)
        ddt = dadt * a + _dot_hi(dxd * xc, et)
        dalog_ref[...] += jnp.sum(dadt * dt, axis=0, keepdims=True) * a
        rows = _row_ids(c * BLOCK, BLOCK, LANE)
        lanes = lax.broadcasted_iota(jnp.int32, (BLOCK, LANE), 1)
        live = jnp.logical_and(rows >= FRONT_PAD, lanes < SSM_HEADS)
        ddt_raw = jnp.where(live, ddt * _sigmoid(dtr_ref[...] + dtb_ref[...]), 0.0)
        ddt_ref[...] = ddt_raw
        ddtb_ref[...] += jnp.sum(ddt_raw, axis=0, keepdims=True)

        dxc = dxd * dt_full + dy * dexp
        dpre_x = dxc * (sig_x * (1.0 + pre_x * (1.0 - sig_x)))
        dpre_bc = jnp.concatenate(dbs + dcs, axis=1) * (sig_bc * (1.0 + pre_bc * (1.0 - sig_bc)))
        dcbx_ref[...] += jnp.sum(dpre_x, axis=0, keepdims=True)
        dcbbc_ref[...] += jnp.sum(dpre_bc, axis=0, keepdims=True)
        keep_x = _row_ids(c * BLOCK, BLOCK, SSM_INNER) >= FRONT_PAD
        keep_bc = _row_ids(c * BLOCK, BLOCK, 2 * LANE) >= FRONT_PAD
        prev_live = (c > 0).astype(F32)
        for (dpre, t_ref, s_ref, cur_ref, prv_ref, w_ref, dw_ref, dx_ref, keep) in (
                (dpre_x, tx, sx, xs_ref, xsp_ref, cwx_ref, dcwx_ref, dxs_ref, keep_x),
                (dpre_bc, tbc, sbc, bc_ref, bcp_ref, cwbc_ref, dcwbc_ref, dbc_ref, keep_bc)):
            t_ref[0:BLOCK, :] = dpre
            acc = w_ref[0:1, :] * t_ref[pl.ds(SSM_CONV - 1, BLOCK), :]
            for j in range(1, SSM_CONV):
                acc = acc + w_ref[j:j + 1, :] * t_ref[pl.ds(SSM_CONV - 1 - j, BLOCK), :]
            dx_ref[...] = jnp.where(keep, acc, 0.0)
            t_ref[BLOCK:BLOCK + HALO, :] = dpre[0:HALO, :]
            s_ref[0:HALO, :] = prv_ref[BLOCK - HALO:BLOCK, :] * prev_live
            s_ref[HALO:HALO + BLOCK, :] = cur_ref[...]
            first = HALO - (SSM_CONV - 1)
            for j in range(SSM_CONV):
                dw_ref[j:j + 1, :] += jnp.sum(dpre * s_ref[pl.ds(first + j, BLOCK), :], axis=0, keepdims=True)

        @pl.when(i == nb - 1)
        def _():
            dd_ref[...] = _dot_hi(jnp.broadcast_to(dlane[...], (HALO, SSM_INNER)), et)[0:1, :]

    blk = lambda i: nb - 1 - i
    prv = lambda i: jnp.maximum(nb - 2 - i, 0)
    full = lambda s: pl.BlockSpec(s, lambda i: (0,) * len(s))
    rowblk = lambda w, cidx: pl.BlockSpec((BLOCK, w), lambda i: (blk(i), cidx))
    prvblk = lambda w, cidx: pl.BlockSpec((BLOCK, w), lambda i: (prv(i), cidx))
    return pl.pallas_call(
        body, grid=(nb,),
        in_specs=[rowblk(GROUP_W, Q_W // GROUP_W), rowblk(GROUP_W, Q_W // GROUP_W + 1),
                  rowblk(SSM_INNER, COL_Z // SSM_INNER),
                  rowblk(SSM_INNER, COL_XS // SSM_INNER), prvblk(SSM_INNER, COL_XS // SSM_INNER),
                  rowblk(2 * LANE, COL_BC // (2 * LANE)), prvblk(2 * LANE, COL_BC // (2 * LANE)),
                  rowblk(LANE, COL_DT // LANE),
                  rowblk(SSM_INNER, 0), rowblk(2 * LANE, 0), rowblk(LANE, 0), rowblk(SSM_INNER, 0),
                  pl.BlockSpec((1, SSM_INNER, SSM_STATE), lambda i: (blk(i), 0, 0)),
                  full((SSM_CONV, SSM_INNER)), full((SSM_CONV, 2 * LANE)), full((1, LANE)), full((1, LANE)),
                  full((1, SSM_INNER)), full((1, SSM_INNER)), full((LANE, SSM_INNER)), full((SSM_INNER, LANE))],
        out_specs=[rowblk(SSM_INNER, 0), rowblk(SSM_INNER, 0), rowblk(2 * LANE, 0), rowblk(LANE, 0),
                   full((SSM_CONV, SSM_INNER)), full((SSM_CONV, 2 * LANE)), full((1, SSM_INNER)), full((1, 2 * LANE)),
                   full((1, LANE)), full((1, LANE)), full((1, LANE)), full((1, SSM_INNER))],
        out_shape=[jax.ShapeDtypeStruct((m, SSM_INNER), F32), jax.ShapeDtypeStruct((m, SSM_INNER), F32),
                   jax.ShapeDtypeStruct((m, 2 * LANE), F32), jax.ShapeDtypeStruct((m, LANE), F32),
                   jax.ShapeDtypeStruct((SSM_CONV, SSM_INNER), F32), jax.ShapeDtypeStruct((SSM_CONV, 2 * LANE), F32),
                   jax.ShapeDtypeStruct((1, SSM_INNER), F32), jax.ShapeDtypeStruct((1, 2 * LANE), F32),
                   jax.ShapeDtypeStruct((1, LANE), F32), jax.ShapeDtypeStruct((1, LANE), F32),
                   jax.ShapeDtypeStruct((1, LANE), F32), jax.ShapeDtypeStruct((1, SSM_INNER), F32)],
        scratch_shapes=[pltpu.VMEM((SSM_INNER, SSM_STATE), F32), pltpu.VMEM((SSM_INNER, SSM_STATE), F32),
                        pltpu.VMEM((BLOCK + HALO, SSM_INNER), F32), pltpu.VMEM((BLOCK + HALO, 2 * LANE), F32),
                        pltpu.VMEM((HALO + BLOCK, SSM_INNER), F32), pltpu.VMEM((HALO + BLOCK, 2 * LANE), F32),
                        pltpu.VMEM((1, SSM_INNER), F32)],
        name=name, compiler_params=_params("arbitrary"),
    )(dmix, dmix, proj, proj, proj, proj, proj, proj, pre_x, pre_bc, dt, y_pre, states,
      cw_x, cw_bc, dt_bias, a_log, d_exp, norm_w, expand, expand_t)


CONF_HALO = 32


def _glu_masked(v, first_row):
    a = v[:, :D_MODEL]
    s = _sigmoid(v[:, D_MODEL:])
    rows = _row_ids(first_row, v.shape[0], D_MODEL)
    return jnp.where(rows >= FRONT_PAD, a * s, 0.0), a, s


def _layer_norm_stats(c):
    mu = jnp.mean(c, -1, keepdims=True)
    xc = c - mu
    rstd = lax.rsqrt(jnp.mean(xc * xc, -1, keepdims=True) + LN_EPS)
    return xc * rstd, rstd


def conformer_mid_fwd(v, dw_w, dw_b, ln_g, ln_b, name="conf_mid_fwd"):
    m = v.shape[0]
    nb = m // BLOCK
    kpad = dw_w.shape[0]

    def body(vc_ref, vp_ref, w_ref, b_ref, g_ref, beta_ref, c_ref, s_ref, sg):
        i = pl.program_id(0)
        g_prev, _, _ = _glu_masked(vp_ref[BLOCK - CONF_HALO:BLOCK, :], (i - 1) * BLOCK + BLOCK - CONF_HALO)
        sg[0:CONF_HALO, :] = g_prev * (i > 0).astype(F32)
        g_cur, _, _ = _glu_masked(vc_ref[...], i * BLOCK)
        sg[CONF_HALO:CONF_HALO + BLOCK, :] = g_cur
        first = CONF_HALO - (CONF_KERNEL - 1)
        acc = b_ref[...] + w_ref[0:1, :] * sg[pl.ds(first, BLOCK), :]
        for j in range(1, CONF_KERNEL):
            acc = acc + w_ref[j:j + 1, :] * sg[pl.ds(first + j, BLOCK), :]
        c_ref[...] = acc
        xhat, _ = _layer_norm_stats(acc)
        nrm = xhat * g_ref[...] + beta_ref[...]
        s_ref[...] = (nrm * _sigmoid(nrm)).astype(s_ref.dtype)

    full = lambda s: pl.BlockSpec(s, lambda i: (0,) * len(s))
    return pl.pallas_call(
        body, grid=(nb,),
        in_specs=[pl.BlockSpec((BLOCK, 2 * D_MODEL), lambda i: (i, 0)),
                  pl.BlockSpec((BLOCK, 2 * D_MODEL), lambda i: (jnp.maximum(i - 1, 0), 0)),
                  full((kpad, D_MODEL)), full((1, D_MODEL)), full((1, D_MODEL)), full((1, D_MODEL))],
        out_specs=[pl.BlockSpec((BLOCK, D_MODEL), lambda i: (i, 0)), pl.BlockSpec((BLOCK, D_MODEL), lambda i: (i, 0))],
        out_shape=[jax.ShapeDtypeStruct((m, D_MODEL), F32), jax.ShapeDtypeStruct((m, D_MODEL), MXU_DTYPE)],
        scratch_shapes=[pltpu.VMEM((CONF_HALO + BLOCK, D_MODEL), F32)],
        name=name, compiler_params=_params("arbitrary"),
    )(v, v, dw_w, dw_b, ln_g, ln_b)


def conformer_ln_bwd(ds, c, ln_g, ln_b, name="conf_ln_bwd"):
    m, d = c.shape
    tm = ROW_TILE

    def body(ds_ref, c_ref, g_ref, beta_ref, dc_ref, dg_ref, db_ref):
        @pl.when(pl.program_id(0) == 0)
        def _():
            dg_ref[...] = jnp.zeros_like(dg_ref)
            db_ref[...] = jnp.zeros_like(db_ref)

        xhat, rstd = _layer_norm_stats(c_ref[...])
        g = g_ref[...]
        nrm = xhat * g + beta_ref[...]
        sg = _sigmoid(nrm)
        dn = ds_ref[...] * (sg * (1.0 + nrm * (1.0 - sg)))
        db_ref[...] += jnp.sum(dn, axis=0, keepdims=True)
        dg_ref[...] += jnp.sum(dn * xhat, axis=0, keepdims=True)
        dx = dn * g
        dc_ref[...] = rstd * (dx - jnp.mean(dx, -1, keepdims=True) - xhat * jnp.mean(dx * xhat, -1, keepdims=True))

    row = pl.BlockSpec((tm, d), lambda i: (i, 0))
    vec = pl.BlockSpec((1, d), lambda i: (0, 0))
    return pl.pallas_call(
        body, grid=(m // tm,), in_specs=[row, row, vec, vec], out_specs=[row, vec, vec],
        out_shape=[jax.ShapeDtypeStruct((m, d), F32), jax.ShapeDtypeStruct((1, d), F32), jax.ShapeDtypeStruct((1, d), F32)],
        name=name, compiler_params=_params("arbitrary"),
    )(ds, c, ln_g, ln_b)


def conformer_conv_bwd(dc, v, dw_w, name="conf_conv_bwd"):
    m = v.shape[0]
    nb = m // BLOCK
    kpad = dw_w.shape[0]

    def body(dcc_ref, dcn_ref, vc_ref, vp_ref, w_ref, dv_ref, dw_ref, db_ref, dvb_ref, tg, sg):
        i = pl.program_id(0)

        @pl.when(i == 0)
        def _():
            dw_ref[...] = jnp.zeros_like(dw_ref)
            db_ref[...] = jnp.zeros_like(db_ref)
            dvb_ref[...] = jnp.zeros_like(dvb_ref)

        dc_cur = dcc_ref[...]
        tg[0:BLOCK, :] = dc_cur
        tg[BLOCK:BLOCK + CONF_HALO, :] = dcn_ref[0:CONF_HALO, :] * (i < nb - 1).astype(F32)
        g_prev, _, _ = _glu_masked(vp_ref[BLOCK - CONF_HALO:BLOCK, :], (i - 1) * BLOCK + BLOCK - CONF_HALO)
        sg[0:CONF_HALO, :] = g_prev * (i > 0).astype(F32)
        g_cur, a, s = _glu_masked(vc_ref[...], i * BLOCK)
        sg[CONF_HALO:CONF_HALO + BLOCK, :] = g_cur
        db_ref[...] += jnp.sum(dc_cur, axis=0, keepdims=True)
        first = CONF_HALO - (CONF_KERNEL - 1)
        dg = w_ref[0:1, :] * tg[pl.ds(CONF_KERNEL - 1, BLOCK), :]
        for j in range(1, CONF_KERNEL):
            dg = dg + w_ref[j:j + 1, :] * tg[pl.ds(CONF_KERNEL - 1 - j, BLOCK), :]
        for j in range(CONF_KERNEL):
            dw_ref[j:j + 1, :] += jnp.sum(dc_cur * sg[pl.ds(first + j, BLOCK), :], axis=0, keepdims=True)
        rows = _row_ids(i * BLOCK, BLOCK, D_MODEL)
        dg = jnp.where(rows >= FRONT_PAD, dg, 0.0)
        da = dg * s
        dbv = dg * a * (s * (1.0 - s))
        dv = jnp.concatenate([da, dbv], axis=1)
        dv_ref[...] = dv.astype(dv_ref.dtype)
        dvb_ref[...] += jnp.sum(dv, axis=0, keepdims=True)

    full = lambda s: pl.BlockSpec(s, lambda i: (0,) * len(s))
    return pl.pallas_call(
        body, grid=(nb,),
        in_specs=[pl.BlockSpec((BLOCK, D_MODEL), lambda i: (i, 0)),
                  pl.BlockSpec((BLOCK, D_MODEL), lambda i: (jnp.minimum(i + 1, nb - 1), 0)),
                  pl.BlockSpec((BLOCK, 2 * D_MODEL), lambda i: (i, 0)),
                  pl.BlockSpec((BLOCK, 2 * D_MODEL), lambda i: (jnp.maximum(i - 1, 0), 0)),
                  full((kpad, D_MODEL))],
        out_specs=[pl.BlockSpec((BLOCK, 2 * D_MODEL), lambda i: (i, 0)), full((kpad, D_MODEL)),
                   full((1, D_MODEL)), full((1, 2 * D_MODEL))],
        out_shape=[jax.ShapeDtypeStruct((m, 2 * D_MODEL), MXU_DTYPE), jax.ShapeDtypeStruct((kpad, D_MODEL), F32),
                   jax.ShapeDtypeStruct((1, D_MODEL), F32), jax.ShapeDtypeStruct((1, 2 * D_MODEL), F32)],
        scratch_shapes=[pltpu.VMEM((BLOCK + CONF_HALO, D_MODEL), F32), pltpu.VMEM((CONF_HALO + BLOCK, D_MODEL), F32)],
        name=name, compiler_params=_params("arbitrary"),
    )(dc, dc, v, v, dw_w)


def _row(v, width=None):
    v = v.reshape(1, -1).astype(F32)
    if width is not None and v.shape[1] < width:
        v = jnp.pad(v, ((0, 0), (0, width - v.shape[1])))
    return v


def _w_in_to_kernel(w):
    pad = jnp.zeros((w.shape[0], PROJ_W - COL_DT - SSM_HEADS), w.dtype)
    return jnp.concatenate([w[:, 768:1792], w[:, 1792:2816], w[:, 0:512], w[:, 2816:3072], w[:, 512:640],
                            w[:, 640:768], w[:, 3072:3088], pad], axis=1)


def _w_in_from_kernel(g):
    return jnp.concatenate([g[:, COL_Q:COL_Q + Q_W], g[:, COL_K:COL_K + KV_W], g[:, COL_V:COL_V + KV_W],
                            g[:, COL_Z:COL_Z + SSM_INNER], g[:, COL_XS:COL_XS + SSM_INNER],
                            g[:, COL_BC:COL_BC + 2 * LANE], g[:, COL_DT:COL_DT + SSM_HEADS]], axis=1)


def even_fwd(h, p):
    u = rms_fwd(h, p["norm"])
    proj = matmul(u, p["w_in"], name="mm_proj")
    att = attention_fwd(proj, p["q_norm"], p["k_norm"], p["sinks"])
    ssm, pre_x, pre_bc, dt, y_pre, states = ssd_fwd(proj, p["cw_x"], p["cw_bc"], p["cb_x"], p["cb_bc"], p["dt_bias"],
                                                    p["a_log"], p["d_exp"], p["ssm_norm"])
    mix = jnp.concatenate([att, ssm], axis=1)
    out = matmul(mix, p["w_out"], b_kind="rowshard", layer=p["layer"], epilogue="resid", extra=h, name="mm_mix_out")
    return out, (h, u, proj, mix, pre_x, pre_bc, dt, y_pre, states)


def even_bwd(dh, p, saved):
    h, u, proj, mix, pre_x, pre_bc, dt, y_pre, states = saved
    dmix = matmul(dh, p["w_out"], b_kind="rowshard", layer=p["layer"], trans_b=True, name="mm_dmix")
    dw_out = matmul_tn(mix, dh, ti=512, tn=D_MODEL, out_dtype=GRAD_WIRE_DTYPE, name="mm_dw_out")
    dw_out = dw_out.reshape(N_DEV, MIX_W // N_DEV, D_MODEL)
    dq, dk, dv, dqw, dkw, dsk = attention_bwd(proj, dmix, p["q_norm"], p["k_norm"], p["sinks"])
    (dz, dxs, dbc, ddt, dcwx, dcwbc, dcbx, dcbbc, ddtb, dalog, dd, dnw) = ssd_bwd(
        proj, dmix, pre_x, pre_bc, dt, y_pre, states, p["cw_x"], p["cw_bc"], p["dt_bias"], p["a_log"], p["d_exp"],
        p["ssm_norm"])
    dproj = jnp.concatenate([dz, dxs, dq, dbc, dk, dv, ddt], axis=1).astype(MXU_DTYPE)
    du = matmul(dproj, p["w_in"], trans_b=True, name="mm_du_in")
    dw_in = matmul_tn(u, dproj, ti=512, tn=PROJ_W, name="mm_dw_in")
    dw_in = _to_shards(_w_in_from_kernel(dw_in), 1).astype(GRAD_WIRE_DTYPE)
    dh_in, dg = rms_bwd(h, p["norm"], du, dh)
    grads = dict(norm=dg, w_in=dw_in, w_out=dw_out, cw_x=dcwx, cw_bc=dcwbc, cb_x=dcbx, cb_bc=dcbbc, dt_bias=ddtb,
                 a_log=dalog, d_skip=dd, ssm_norm=dnw, q_norm=dqw, k_norm=dkw, sinks=dsk)
    return dh_in, grads


def conf_fwd(h, p):
    u = rms_fwd(h, p["norm"])
    v = matmul(u, p["pw1_w"], b_kind="colshard", layer=p["layer"], bias=p["pw1_b"], name="mm_pw1")
    c, s = conformer_mid_fwd(v, p["dw_w"], p["dw_b"], p["ln_g"], p["ln_b"])
    out = matmul(s, p["pw2_w"], b_kind="rowshard", layer=p["layer"], bias=p["pw2_b"], epilogue="resid", extra=h,
                 name="mm_pw2")
    return out, (h, u, v, c, s)


def conf_bwd(dh, p, saved):
    h, u, v, c, s = saved
    dpw2_b = col_sum(dh)
    ds = matmul(dh, p["pw2_w"], b_kind="rowshard", layer=p["layer"], trans_b=True, name="mm_ds")
    dpw2_w = matmul_tn(s, dh, ti=D_MODEL, tn=D_MODEL, out_dtype=GRAD_WIRE_DTYPE, name="mm_dpw2")
    dpw2_w = dpw2_w.reshape(N_DEV, D_MODEL // N_DEV, D_MODEL)
    dc, dln_g, dln_b = conformer_ln_bwd(ds, c, p["ln_g"], p["ln_b"])
    dv, ddw_w, ddw_b, dpw1_b = conformer_conv_bwd(dc, v, p["dw_w"])
    du = matmul_nt_acc(dv, p["pw1_w"], layer=p["layer"], name="mm_du_pw1")
    dpw1_w = matmul_tn(u, dv, ti=D_MODEL, tn=2 * D_MODEL // N_DEV, colshard=True, out_dtype=GRAD_WIRE_DTYPE,
                       name="mm_dpw1")
    dh_in, dg = rms_bwd(h, p["norm"], du, dh)
    grads = dict(norm=dg, pw1_w=dpw1_w, pw1_b=dpw1_b, dw_w=ddw_w, dw_b=ddw_b, ln_g=dln_g, ln_b=dln_b, pw2_w=dpw2_w,
                 pw2_b=dpw2_b)
    return dh_in, grads


def mlp_fwd(h, p):
    u = rms_fwd(h, p["norm"])
    a, act = matmul(u, p["w_up"], b_kind="colshard", layer=p["layer"], epilogue="relu2", name="mm_up")
    out = matmul(act, p["w_down"], b_kind="rowshard", layer=p["layer"], epilogue="resid", extra=h, name="mm_down")
    return out, (h, u, a, act)


def mlp_bwd(dh, p, saved):
    h, u, a, act = saved
    da = matmul(dh, p["w_down"], b_kind="rowshard", layer=p["layer"], trans_b=True, tn=D_FF // N_DEV, epilogue="drelu2",
                extra=a, out_dtype=MXU_DTYPE, name="mm_da")
    dw_down = matmul_tn(act, dh, ti=D_MODEL, tn=D_MODEL, out_dtype=GRAD_WIRE_DTYPE, name="mm_dw_down")
    dw_down = dw_down.reshape(N_DEV, D_FF // N_DEV, D_MODEL)
    dw_up = matmul_tn(u, da, ti=D_MODEL, tn=D_FF // N_DEV, colshard=True, out_dtype=GRAD_WIRE_DTYPE, name="mm_dw_up")
    du = matmul_nt_acc(da, p["w_up"], layer=p["layer"], name="mm_du_up")
    dh_in, dg = rms_bwd(h, p["norm"], du, dh)
    return dh_in, dict(norm=dg, w_up=dw_up, w_down=dw_down)


def local_step(x, target, w):
    n_even, n_odd = (DEPTH + 1) // 2, DEPTH // 2
    h = jnp.concatenate([jnp.zeros((FRONT_PAD, D_MODEL), F32), w["meta_tokens"].astype(F32), x], axis=0)
    even_p, odd_p, mlp_p = [], [], []
    for i in range(n_even):
        cw = w["ssm_conv_w"][i]
        even_p.append(dict(
            layer=i, norm=_row(w["mix_norm_even"][i]), w_in=_w_in_to_kernel(_from_shards(w["w_in"][:, i], 1)),
            w_out=w["w_out"],
            cw_x=cw[:, :SSM_INNER], cw_bc=cw[:, SSM_INNER:], cb_x=_row(w["ssm_conv_b"][i][:SSM_INNER]),
            cb_bc=_row(w["ssm_conv_b"][i][SSM_INNER:]), dt_bias=_row(w["dt_bias"][i], LANE),
            a_log=_row(w["a_log"][i], LANE), d_exp=_row(jnp.repeat(w["d_skip"][i], SSM_HEAD_DIM)),
            ssm_norm=_row(w["ssm_norm_w"][i]), q_norm=_row(w["q_norm"][i]), k_norm=_row(w["k_norm"][i]),
            sinks=_row(w["sinks"][i], LANE)))
    for i in range(n_odd):
        odd_p.append(dict(
            layer=i, norm=_row(w["mix_norm_odd"][i]), pw1_w=w["pw1_w"], pw1_b=_row(w["pw1_b"][i]),
            dw_w=jnp.pad(w["dw_w"][i], ((0, CONF_HALO - CONF_KERNEL), (0, 0))), dw_b=_row(w["dw_b"][i]),
            ln_g=_row(w["ln_g"][i]), ln_b=_row(w["ln_b"][i]), pw2_w=w["pw2_w"], pw2_b=_row(w["pw2_b"][i])))
    for layer in range(DEPTH):
        mlp_p.append(dict(layer=layer, norm=_row(w["mlp_norm"][layer]), w_up=w["w_up"], w_down=w["w_down"]))

    tape = []
    for layer in range(DEPTH):
        if layer % 2 == 0:
            h, saved = even_fwd(h, even_p[layer // 2])
        else:
            h, saved = conf_fwd(h, odd_p[layer // 2])
        tape.append(saved)
        h, saved = mlp_fwd(h, mlp_p[layer])
        tape.append(saved)
    dh, loss_row = loss_fwd_bwd(h, target)

    ge = [None] * n_even
    go = [None] * n_odd
    gm = [None] * DEPTH
    for layer in reversed(range(DEPTH)):
        dh, gm[layer] = mlp_bwd(dh, mlp_p[layer], tape.pop())
        if layer % 2 == 0:
            dh, ge[layer // 2] = even_bwd(dh, even_p[layer // 2], tape.pop())
        else:
            dh, go[layer // 2] = conf_bwd(dh, odd_p[layer // 2], tape.pop())

    stack = lambda gs, f: jnp.stack([f(g) for g in gs])
    grads = dict(
        meta_tokens=dh[FRONT_PAD:BLOCK],
        mix_norm_even=stack(ge, lambda g: g["norm"][0]),
        w_in=[g["w_in"] for g in ge],
        ssm_conv_w=stack(ge, lambda g: jnp.concatenate([g["cw_x"], g["cw_bc"]], axis=1)),
        ssm_conv_b=stack(ge, lambda g: jnp.concatenate([g["cb_x"][0], g["cb_bc"][0]])),
        dt_bias=stack(ge, lambda g: g["dt_bias"][0, :SSM_HEADS]),
        a_log=stack(ge, lambda g: g["a_log"][0, :SSM_HEADS]),
        d_skip=stack(ge, lambda g: g["d_skip"][0, :SSM_HEADS]),
        ssm_norm_w=stack(ge, lambda g: g["ssm_norm"][0]),
        q_norm=stack(ge, lambda g: g["q_norm"][0]),
        k_norm=stack(ge, lambda g: g["k_norm"][0]),
        sinks=stack(ge, lambda g: g["sinks"][0, :ATT_HEADS]),
        w_out=[g["w_out"] for g in ge],
        mix_norm_odd=stack(go, lambda g: g["norm"][0]),
        pw1_w=[g["pw1_w"] for g in go],
        pw1_b=stack(go, lambda g: g["pw1_b"][0]),
        dw_w=stack(go, lambda g: g["dw_w"][:CONF_KERNEL]),
        dw_b=stack(go, lambda g: g["dw_b"][0]),
        ln_g=stack(go, lambda g: g["ln_g"][0]),
        ln_b=stack(go, lambda g: g["ln_b"][0]),
        pw2_w=[g["pw2_w"] for g in go],
        pw2_b=stack(go, lambda g: g["pw2_b"][0]),
        mlp_norm=stack(gm, lambda g: g["norm"][0]),
        w_up=[g["w_up"] for g in gm],
        w_down=[g["w_down"] for g in gm],
    )
    return loss_row[0, 0], dh[BLOCK:], grads


PARAMS = (
    ("meta_tokens", (16, 1024), 1), ("mix_norm_even", (2, 1024), None), ("w_in", (2, 1024, 3088), 2),
    ("ssm_conv_w", (2, 4, 1280), 2), ("ssm_conv_b", (2, 1280), None), ("dt_bias", (2, 16), None),
    ("a_log", (2, 16), None), ("d_skip", (2, 16), None), ("ssm_norm_w", (2, 1024), None), ("q_norm", (2, 64), None),
    ("k_norm", (2, 64), None), ("sinks", (2, 8), None), ("w_out", (2, 1536, 1024), 1), ("mix_norm_odd", (2, 1024), 1),
    ("pw1_w", (2, 1024, 2048), 2), ("pw1_b", (2, 2048), 1), ("dw_w", (2, 31, 1024), 2), ("dw_b", (2, 1024), 1),
    ("ln_g", (2, 1024), 1), ("ln_b", (2, 1024), 1), ("pw2_w", (2, 1024, 1024), 1), ("pw2_b", (2, 1024), 1),
    ("mlp_norm", (4, 1024), None), ("w_up", (4, 1024, 4096), 2), ("w_down", (4, 4096, 1024), 1),
)
MATMUL_WEIGHTS = ("w_in", "w_out", "pw1_w", "pw2_w", "w_up", "w_down")
PACK_ROW_ALIGN = 16 * PACK_W


def _block_shape(shape, axis):
    if axis is None:
        return tuple(shape)
    return tuple(s // N_DEV if a == axis else s for a, s in enumerate(shape))


def _numel(shape):
    return math.prod(shape)


def _pack(arrays, dtype):
    flat = jnp.concatenate([a.reshape(-1).astype(dtype) for a in arrays])
    n = flat.shape[0]
    padded = -(-n // PACK_ROW_ALIGN) * PACK_ROW_ALIGN
    return jnp.pad(flat, (0, padded - n)).reshape(-1, PACK_W)


def _pack_rows(arrays_by_dev, dtype):
    flat = jnp.concatenate([a.reshape(N_DEV, -1).astype(dtype) for a in arrays_by_dev], axis=1)
    n = flat.shape[1]
    padded = -(-n // PACK_ROW_ALIGN) * PACK_ROW_ALIGN
    return jnp.pad(flat, ((0, 0), (0, padded - n))).reshape(N_DEV, -1, PACK_W)


def _to_shards(full, axis):
    shape = full.shape
    split = full.reshape(shape[:axis] + (N_DEV, shape[axis] // N_DEV) + shape[axis + 1:])
    return jnp.moveaxis(split, axis, 0)


def _from_shards(blocks, axis):
    moved = jnp.moveaxis(blocks, 0, axis)
    shape = moved.shape
    return moved.reshape(shape[:axis] + (shape[axis] * shape[axis + 1],) + shape[axis + 2:])


_MESH = pl.DeviceIdType.MESH
_ANY = pl.BlockSpec(memory_space=pl.ANY)


def _mesh_place():
    x, y, c = lax.axis_index("x"), lax.axis_index("y"), lax.axis_index("c")
    return x, y, c


def _peer(x, y, c, rel):
    dx, dy, dc = (rel >> 2) & 1, (rel >> 1) & 1, rel & 1
    return (x ^ dx if dx else x, y ^ dy if dy else y, c ^ dc if dc else c)


def _dev_index(x, y, c):
    return 4 * x + 2 * y + c


def all_gather_weights(bigs, small):
    nt = len(bigs)

    def body(*refs):
        big_refs, small_ref = refs[:nt], refs[nt]
        big_outs, small_out = refs[nt + 1:2 * nt + 1], refs[2 * nt + 1]
        send_sems, recv_sems, small_send, small_recv, local_sems = refs[2 * nt + 2:]
        x, y, c = _mesh_place()
        me = (x, y, c)
        sibling = (x, y, 1 - c)
        chips = [(1 - x, y), (x, 1 - y), (1 - x, 1 - y)]

        def big_copy(t, k, block, to, from_input=False):
            dst = big_outs[t].at[_dev_index(*block)]
            return pltpu.make_async_remote_copy(src_ref=big_refs[t] if from_input else dst, dst_ref=dst,
                                                send_sem=send_sems.at[t, k], recv_sem=recv_sems.at[t, k],
                                                device_id=to, device_id_type=_MESH)

        def small_copy(rel, block, to):
            return pltpu.make_async_remote_copy(src_ref=small_ref, dst_ref=small_out.at[_dev_index(*block)],
                                                send_sem=small_send.at[rel - 1], recv_sem=small_recv.at[rel - 1],
                                                device_id=to, device_id_type=_MESH)

        mine = [pltpu.make_async_copy(big_refs[t], big_outs[t].at[_dev_index(*me)], local_sems.at[t]) for t in range(nt)]
        mine.append(pltpu.make_async_copy(small_ref, small_out.at[_dev_index(*me)], local_sems.at[nt]))
        for cp in mine:
            cp.start()
        first = []
        for t in range(nt):
            first.append(big_copy(t, 0, me, sibling, from_input=True))
            first += [big_copy(t, 1 + j, me, (*chip, c), from_input=True) for j, chip in enumerate(chips)]
        for cp in first:
            cp.start()
        smalls = [small_copy(rel, me, _peer(x, y, c, rel)) for rel in range(1, N_DEV)]
        for cp in smalls:
            cp.start()
        passed = []
        for j, chip in enumerate(chips):
            for t in range(nt):
                big_copy(t, 1 + j, (*chip, c), me).wait_recv()
                fwd = big_copy(t, 4 + j, (*chip, c), sibling)
                fwd.start()
                passed.append(fwd)
        for t in range(nt):
            big_copy(t, 0, sibling, me).wait_recv()
            for j, chip in enumerate(chips):
                big_copy(t, 4 + j, (*chip, 1 - c), me).wait_recv()
        for rel in range(1, N_DEV):
            small_copy(rel, _peer(x, y, c, rel), me).wait_recv()
        for cp in first + passed + smalls:
            cp.wait_send()
        for cp in mine:
            cp.wait()

    return pl.pallas_call(
        body, in_specs=[_ANY] * (nt + 1), out_specs=[_ANY] * (nt + 1),
        out_shape=[jax.ShapeDtypeStruct((N_DEV,) + b.shape, b.dtype) for b in bigs]
        + [jax.ShapeDtypeStruct((N_DEV,) + small.shape, small.dtype)],
        scratch_shapes=[pltpu.SemaphoreType.DMA((nt, N_DEV - 1)), pltpu.SemaphoreType.DMA((nt, N_DEV - 1)),
                        pltpu.SemaphoreType.DMA((N_DEV - 1,)), pltpu.SemaphoreType.DMA((N_DEV - 1,)),
                        pltpu.SemaphoreType.DMA((nt + 1,))],
        name="all_gather_weights",
    )(*bigs, small)


def exchange_gradients(groups):
    flat = [(gi, li, a) for gi, group in enumerate(groups) for li, a in enumerate(group)]
    n_in = len(flat)
    n_out = len(groups)

    def body(*refs):
        in_refs = refs[:n_in]
        out_refs = refs[n_in:n_in + n_out]
        send_sems, recv_sems, local_sems = refs[n_in + n_out:]
        x, y, c = _mesh_place()
        me = _dev_index(x, y, c)

        def copy(rel, p):
            gi, li, _ = flat[p]
            peer = _peer(x, y, c, rel)
            return pltpu.make_async_remote_copy(src_ref=in_refs[p].at[_dev_index(*peer)], dst_ref=out_refs[gi].at[me, li],
                                                send_sem=send_sems.at[rel - 1, p], recv_sem=recv_sems.at[rel - 1, p],
                                                device_id=peer, device_id_type=_MESH)

        def arrival(rel, p):
            gi, li, _ = flat[p]
            peer = _peer(x, y, c, rel)
            return pltpu.make_async_remote_copy(src_ref=in_refs[p].at[me], dst_ref=out_refs[gi].at[_dev_index(*peer), li],
                                                send_sem=send_sems.at[rel - 1, p], recv_sem=recv_sems.at[rel - 1, p],
                                                device_id=peer, device_id_type=_MESH)

        mine = [pltpu.make_async_copy(in_refs[p].at[me], out_refs[flat[p][0]].at[me, flat[p][1]], local_sems.at[p])
                for p in range(n_in)]
        for cp in mine:
            cp.start()
        copies = [copy(rel, p) for p in range(n_in) for rel in range(1, N_DEV)]
        for cp in copies:
            cp.start()
        for p in range(n_in):
            for rel in range(1, N_DEV):
                arrival(rel, p).wait_recv()
        for cp in copies:
            cp.wait_send()
        for cp in mine:
            cp.wait()

    return pl.pallas_call(
        body, in_specs=[_ANY] * n_in, out_specs=[_ANY] * n_out,
        out_shape=[jax.ShapeDtypeStruct((N_DEV, len(group)) + group[0].shape[1:], group[0].dtype) for group in groups],
        scratch_shapes=[pltpu.SemaphoreType.DMA((N_DEV - 1, n_in)), pltpu.SemaphoreType.DMA((N_DEV - 1, n_in)),
                        pltpu.SemaphoreType.DMA((n_in,))],
        name="exchange_gradients",
    )(*[a for _, _, a in flat])


def reduce_adamw(parts, w, m, v, tr):
    nl, r, cols = w.shape

    def body(p_ref, w_ref, m_ref, v_ref, g_ref, d_ref, nm_ref, nv_ref):
        g = p_ref[0].astype(F32)
        for d in range(1, N_DEV):
            g = g + p_ref[d].astype(F32)
        g_ref[...] = g
        nm = ADAM_B1 * m_ref[...] + (1.0 - ADAM_B1) * g
        nv = ADAM_B2 * v_ref[...] + (1.0 - ADAM_B2) * (g * g)
        nm_ref[...] = nm
        nv_ref[...] = nv
        m_hat = nm / (1.0 - ADAM_B1 ** ADAM_STEP)
        v_hat = nv / (1.0 - ADAM_B2 ** ADAM_STEP)
        d_ref[...] = -ADAM_LR * (m_hat / (jnp.sqrt(v_hat) + ADAM_EPS) + ADAM_WD * w_ref[...])

    row = pl.BlockSpec((None, tr, cols), lambda l, i: (l, i, 0))
    return pl.pallas_call(
        body, grid=(nl, r // tr),
        in_specs=[pl.BlockSpec((N_DEV, None, tr, cols), lambda l, i: (0, l, i, 0)), row, row, row],
        out_specs=[row, row, row, row], out_shape=[jax.ShapeDtypeStruct((nl, r, cols), F32)] * 4,
        name="reduce_adamw", compiler_params=_params("arbitrary", "arbitrary"),
    )(parts, w, m, v)


ADAMW_TILE_BYTES = 1 << 20


def _adamw_tile(rows, cols):
    lanes = -(-cols // LANE) * LANE
    best = None
    for tr in range(16, rows + 1, 16):
        if rows % tr == 0 and tr * lanes * 4 <= ADAMW_TILE_BYTES:
            best = tr
    if best is None:
        raise ValueError((rows, cols))
    return best


def kernel(x, meta_tokens, mix_norm_even, w_in, ssm_conv_w, ssm_conv_b, dt_bias, a_log, d_skip, ssm_norm_w, q_norm, k_norm, sinks, w_out, mix_norm_odd, pw1_w, pw1_b, dw_w, dw_b, ln_g, ln_b, pw2_w, pw2_b, mlp_norm, w_up, w_down, loss_target, m_meta_tokens, m_mix_norm_even, m_w_in, m_ssm_conv_w, m_ssm_conv_b, m_dt_bias, m_a_log, m_d_skip, m_ssm_norm_w, m_q_norm, m_k_norm, m_sinks, m_w_out, m_mix_norm_odd, m_pw1_w, m_pw1_b, m_dw_w, m_dw_b, m_ln_g, m_ln_b, m_pw2_w, m_pw2_b, m_mlp_norm, m_w_up, m_w_down, v_meta_tokens, v_mix_norm_even, v_w_in, v_ssm_conv_w, v_ssm_conv_b, v_dt_bias, v_a_log, v_d_skip, v_ssm_norm_w, v_q_norm, v_k_norm, v_sinks, v_w_out, v_mix_norm_odd, v_pw1_w, v_pw1_b, v_dw_w, v_dw_b, v_ln_g, v_ln_b, v_pw2_w, v_pw2_b, v_mlp_norm, v_w_up, v_w_down):
    names = [p[0] for p in PARAMS]
    w_loc = dict(zip(names, (meta_tokens, mix_norm_even, w_in, ssm_conv_w, ssm_conv_b, dt_bias, a_log, d_skip, ssm_norm_w, q_norm, k_norm, sinks, w_out, mix_norm_odd, pw1_w, pw1_b, dw_w, dw_b, ln_g, ln_b, pw2_w, pw2_b, mlp_norm, w_up, w_down)))
    m_loc = dict(zip(names, (m_meta_tokens, m_mix_norm_even, m_w_in, m_ssm_conv_w, m_ssm_conv_b, m_dt_bias, m_a_log, m_d_skip, m_ssm_norm_w, m_q_norm, m_k_norm, m_sinks, m_w_out, m_mix_norm_odd, m_pw1_w, m_pw1_b, m_dw_w, m_dw_b, m_ln_g, m_ln_b, m_pw2_w, m_pw2_b, m_mlp_norm, m_w_up, m_w_down)))
    v_loc = dict(zip(names, (v_meta_tokens, v_mix_norm_even, v_w_in, v_ssm_conv_w, v_ssm_conv_b, v_dt_bias, v_a_log, v_d_skip, v_ssm_norm_w, v_q_norm, v_k_norm, v_sinks, v_w_out, v_mix_norm_odd, v_pw1_w, v_pw1_b, v_dw_w, v_dw_b, v_ln_g, v_ln_b, v_pw2_w, v_pw2_b, v_mlp_norm, v_w_up, v_w_down)))
    small_sharded = [p for p in PARAMS if p[2] is not None and p[0] not in MATMUL_WEIGHTS]
    replicated = [p for p in PARAMS if p[2] is None]
    small_list = small_sharded + replicated

    gathered = all_gather_weights([w_loc[n].astype(MXU_DTYPE) for n in MATMUL_WEIGHTS],
                                  _pack([w_loc[n] for n, _, _ in small_sharded], F32))
    w_full = {n: w_loc[n] for n, _, _ in replicated}
    w_full.update(dict(zip(MATMUL_WEIGHTS, gathered[:-1])))
    flat = gathered[-1].reshape(N_DEV, -1)
    off = 0
    for n, shape, axis in small_sharded:
        blk = _block_shape(shape, axis)
        w_full[n] = _from_shards(flat[:, off:off + _numel(blk)].reshape((N_DEV,) + blk), axis)
        off += _numel(blk)

    loss_local, grad_x, g_full = local_step(x[0], loss_target[0], w_full)
    loss = lax.psum(loss_local, ("x", "y", "c"))

    by_dev = [_to_shards(g_full[n], axis) for n, _, axis in small_sharded]
    by_dev += [jnp.broadcast_to(g_full[n][None], (N_DEV,) + tuple(shape)) for n, shape, _ in replicated]
    parts = exchange_gradients([g_full[n] for n in MATMUL_WEIGHTS] + [[_pack_rows(by_dev, F32)]])

    out = {}
    for n, part in zip(MATMUL_WEIGHTS, parts[:-1]):
        nl, r, cols = w_loc[n].shape
        out[n] = reduce_adamw(part, w_loc[n], m_loc[n], v_loc[n], _adamw_tile(r, cols))
    pk = lambda d: _pack([d[n] for n, _, _ in small_list], F32)[None]
    rows = parts[-1].shape[2]
    small_out = reduce_adamw(parts[-1], pk(w_loc), pk(m_loc), pk(v_loc), _adamw_tile(rows, PACK_W))
    flats = [buf.reshape(-1) for buf in small_out]
    off = 0
    for n, shape, axis in small_list:
        blk = _block_shape(shape, axis)
        out[n] = tuple(f[off:off + _numel(blk)].reshape(blk) for f in flats)
        off += _numel(blk)
    return (loss, grad_x[None], *[out[n][0] for n in names], *[out[n][1] for n in names],
            *[out[n][2] for n in names], *[out[n][3] for n in names])
```

```python
import functools
import math

import jax
import jax.numpy as jnp
from jax import lax
from jax.experimental import pallas as pl
from jax.experimental.pallas import tpu as pltpu

F32 = jnp.float32
MXU_DTYPE = jnp.bfloat16
GRAD_WIRE_DTYPE = jnp.bfloat16
HIGHEST = lax.Precision.HIGHEST

D_MODEL = 1024
N_META = 16
BLOCK = 128
FRONT_PAD = BLOCK - N_META
ATT_HEADS = 8
ATT_KV_HEADS = 2
HEAD_DIM = 64
SSM_HEADS = 16
SSM_HEAD_DIM = 64
SSM_INNER = 1024
SSM_GROUPS = 2
SSM_STATE = 64
SSM_CONV = 4
CONF_KERNEL = 31
D_FF = 4096
EPS = 1e-6
LN_EPS = 1e-5
Q_W = 512
KV_W = 128
IN_W = 3088
MIX_W = 1536
DEPTH = 4
N_DEV = 8

ADAM_LR = 0.001
ADAM_B1 = 0.9
ADAM_B2 = 0.999
ADAM_EPS = 1e-08
ADAM_WD = 0.01
ADAM_STEP = 10

PROJ_W = 3200
COL_Z, COL_XS, COL_Q, COL_BC, COL_K, COL_V, COL_DT = 0, 1024, 2048, 2560, 2816, 2944, 3072

ROW_TILE = 640
VMEM_LIMIT = 56 * 1024 * 1024
LANE = 128
PACK_W = 1024


def _params(*sem):
    return pltpu.CompilerParams(dimension_semantics=sem, vmem_limit_bytes=VMEM_LIMIT)


def _mx(x):
    return x.astype(MXU_DTYPE)


def _dot(a, b):
    return jnp.dot(_mx(a), _mx(b), preferred_element_type=F32)


def _dot_nt(a, b):
    return lax.dot_general(_mx(a), _mx(b), (((1,), (1,)), ((), ())), preferred_element_type=F32)


def _dot_tn(a, b):
    return lax.dot_general(_mx(a), _mx(b), (((0,), (0,)), ((), ())), preferred_element_type=F32)


def _dot_hi(a, b):
    return jnp.dot(a, b, precision=HIGHEST, preferred_element_type=F32)


def _dot_tn_hi(a, b):
    return lax.dot_general(a, b, (((0,), (0,)), ((), ())), precision=HIGHEST, preferred_element_type=F32)


def _sigmoid(x):
    return 1.0 / (1.0 + jnp.exp(-x))


def _row_ids(start, rows, cols):
    return start + lax.broadcasted_iota(jnp.int32, (rows, cols), 0)


def rms_fwd(h, g, name="rms_fwd"):
    m, d = h.shape
    tm = ROW_TILE

    def body(h_ref, g_ref, u_ref):
        x = h_ref[...]
        r = lax.rsqrt(jnp.mean(x * x, -1, keepdims=True) + EPS)
        u_ref[...] = ((x * r) * g_ref[...]).astype(u_ref.dtype)

    return pl.pallas_call(
        body, grid=(m // tm,),
        in_specs=[pl.BlockSpec((tm, d), lambda i: (i, 0)), pl.BlockSpec((1, d), lambda i: (0, 0))],
        out_specs=pl.BlockSpec((tm, d), lambda i: (i, 0)),
        out_shape=jax.ShapeDtypeStruct((m, d), MXU_DTYPE), name=name, compiler_params=_params("arbitrary"),
    )(h, g)


def rms_bwd(h, g, du, dh_out, name="rms_bwd"):
    m, d = h.shape
    tm = ROW_TILE

    def body(h_ref, g_ref, du_ref, dho_ref, dh_ref, dg_ref):
        @pl.when(pl.program_id(0) == 0)
        def _():
            dg_ref[...] = jnp.zeros_like(dg_ref)

        x = h_ref[...]
        du_ = du_ref[...]
        r = lax.rsqrt(jnp.mean(x * x, -1, keepdims=True) + EPS)
        gy = du_ * g_ref[...]
        dx = r * gy - x * ((r * r * r) * jnp.mean(x * gy, -1, keepdims=True))
        dh_ref[...] = dho_ref[...] + dx
        dg_ref[...] += jnp.sum(du_ * (x * r), axis=0, keepdims=True)

    row = pl.BlockSpec((tm, d), lambda i: (i, 0))
    vec = pl.BlockSpec((1, d), lambda i: (0, 0))
    return pl.pallas_call(
        body, grid=(m // tm,), in_specs=[row, vec, row, row], out_specs=[row, vec],
        out_shape=[jax.ShapeDtypeStruct((m, d), F32), jax.ShapeDtypeStruct((1, d), F32)],
        name=name, compiler_params=_params("arbitrary"),
    )(h, g, du, dh_out)


def loss_fwd_bwd(h, target, name="loss"):
    m, d = h.shape
    nb = m // BLOCK

    def body(h_ref, t_ref, dh_ref, l_ref):
        i = pl.program_id(0)

        @pl.when(i == 0)
        def _():
            l_ref[...] = jnp.zeros_like(l_ref)
            dh_ref[...] = jnp.zeros_like(dh_ref)

        @pl.when(i > 0)
        def _():
            e = h_ref[...] - t_ref[...]
            dh_ref[...] = e * (1.0 / d)
            s = jnp.sum(jnp.sum(e * e, axis=-1, keepdims=True), axis=0, keepdims=True)
            l_ref[...] += jnp.broadcast_to(s * (0.5 / d), l_ref.shape)

    return pl.pallas_call(
        body, grid=(nb,),
        in_specs=[pl.BlockSpec((BLOCK, d), lambda i: (i, 0)),
                  pl.BlockSpec((BLOCK, d), lambda i: (jnp.maximum(i - 1, 0), 0))],
        out_specs=[pl.BlockSpec((BLOCK, d), lambda i: (i, 0)), pl.BlockSpec((1, LANE), lambda i: (0, 0))],
        out_shape=[jax.ShapeDtypeStruct((m, d), F32), jax.ShapeDtypeStruct((1, LANE), F32)],
        name=name, compiler_params=_params("arbitrary"),
    )(h, target)


def col_sum(x, name="col_sum"):
    m, n = x.shape
    tm = ROW_TILE

    def body(x_ref, o_ref):
        @pl.when(pl.program_id(0) == 0)
        def _():
            o_ref[...] = jnp.zeros_like(o_ref)

        o_ref[...] += jnp.sum(x_ref[...].astype(F32), axis=0, keepdims=True)

    return pl.pallas_call(
        body, grid=(m // tm,), in_specs=[pl.BlockSpec((tm, n), lambda i: (i, 0))],
        out_specs=pl.BlockSpec((1, n), lambda i: (0, 0)), out_shape=jax.ShapeDtypeStruct((1, n), F32),
        name=name, compiler_params=_params("arbitrary"),
    )(x)


def matmul(a, b, *, b_kind="full", layer=0, trans_b=False, tn=None, epilogue=None, bias=None, extra=None,
           out_dtype=F32, name="matmul"):
    m, k = a.shape
    tm = ROW_TILE
    merge = False
    if b_kind == "full":
        n = b.shape[0] if trans_b else b.shape[1]
        tn = n if tn is None else tn
        b_spec = pl.BlockSpec((tn, k), lambda i, j: (j, 0)) if trans_b else pl.BlockSpec((k, tn), lambda i, j: (0, j))
    elif b_kind == "colshard":
        assert not trans_b and b.shape[2] == k
        tn = b.shape[3]
        n = N_DEV * tn
        b_spec = pl.BlockSpec((None, None, k, tn), lambda i, j: (j, layer, 0, 0))
    elif b_kind == "rowshard":
        ks, wn = b.shape[2], b.shape[3]
        if trans_b and tn == ks:
            assert wn == k
            n = N_DEV * ks
            b_spec = pl.BlockSpec((None, None, ks, wn), lambda i, j: (j, layer, 0, 0))
        else:
            assert tn is None
            merge = True
            n = N_DEV * ks if trans_b else wn
            assert (wn if trans_b else N_DEV * ks) == k
            tn = n
            b_spec = pl.BlockSpec((N_DEV, None, ks, wn), lambda i, j: (0, layer, 0, 0))
    else:
        raise ValueError(b_kind)
    has_bias = bias is not None
    has_extra = extra is not None

    def body(*refs):
        a_ref, b_ref = refs[0], refs[1]
        pos = 2
        bias_ref = extra_ref = None
        if has_bias:
            bias_ref = refs[pos]
            pos += 1
        if has_extra:
            extra_ref = refs[pos]
            pos += 1
        outs = refs[pos:]
        w = b_ref[...]
        if merge:
            w = w.reshape(N_DEV * w.shape[1], w.shape[2])
        if trans_b:
            acc = _dot_nt(a_ref[...], w)
        else:
            acc = _dot(a_ref[...], w)
        if has_bias:
            acc = acc + bias_ref[...]
        if epilogue is None:
            outs[0][...] = acc.astype(outs[0].dtype)
        elif epilogue == "relu2":
            outs[0][...] = acc
            r = jnp.maximum(acc, 0.0)
            outs[1][...] = (r * r).astype(outs[1].dtype)
        elif epilogue == "drelu2":
            outs[0][...] = (acc * (2.0 * jnp.maximum(extra_ref[...], 0.0))).astype(outs[0].dtype)
        elif epilogue == "resid":
            rows = _row_ids(pl.program_id(0) * tm, tm, tn)
            outs[0][...] = extra_ref[...] + jnp.where(rows >= FRONT_PAD, acc, 0.0)
        else:
            raise ValueError(epilogue)

    in_specs = [pl.BlockSpec((tm, k), lambda i, j: (i, 0)), b_spec]
    args = [a, b]
    if has_bias:
        in_specs.append(pl.BlockSpec((1, tn), lambda i, j: (0, j)))
        args.append(bias)
    if has_extra:
        in_specs.append(pl.BlockSpec((tm, tn), lambda i, j: (i, j)))
        args.append(extra)
    tile = pl.BlockSpec((tm, tn), lambda i, j: (i, j))
    if epilogue == "relu2":
        out_specs = [tile, tile]
        out_shape = [jax.ShapeDtypeStruct((m, n), F32), jax.ShapeDtypeStruct((m, n), MXU_DTYPE)]
    else:
        out_specs = tile
        out_shape = jax.ShapeDtypeStruct((m, n), out_dtype)
    return pl.pallas_call(
        body, grid=(m // tm, n // tn), in_specs=in_specs, out_specs=out_specs, out_shape=out_shape,
        name=name, compiler_params=_params("arbitrary", "arbitrary"),
    )(*args)


def matmul_nt_acc(a, b, *, layer=0, name="matmul_nt_acc"):
    m = a.shape[0]
    k, ns = b.shape[2], b.shape[3]
    tm = ROW_TILE

    def body(a_ref, b_ref, o_ref):
        @pl.when(pl.program_id(1) == 0)
        def _():
            o_ref[...] = jnp.zeros_like(o_ref)

        o_ref[...] += _dot_nt(a_ref[...], b_ref[...])

    return pl.pallas_call(
        body, grid=(m // tm, N_DEV),
        in_specs=[pl.BlockSpec((tm, ns), lambda i, d: (i, d)),
                  pl.BlockSpec((None, None, k, ns), lambda i, d: (d, layer, 0, 0))],
        out_specs=pl.BlockSpec((tm, k), lambda i, d: (i, 0)),
        out_shape=jax.ShapeDtypeStruct((m, k), F32), name=name, compiler_params=_params("arbitrary", "arbitrary"),
    )(a, b)


def matmul_tn(x, dy, *, ti, tn, colshard=False, out_dtype=F32, name="matmul_tn"):
    m, k1 = x.shape
    n = dy.shape[1]
    tm = ROW_TILE
    last = m // tm - 1

    def body(x_ref, dy_ref, o_ref, acc_ref):
        r = pl.program_id(2)

        @pl.when(r == 0)
        def _():
            acc_ref[...] = jnp.zeros_like(acc_ref)

        acc_ref[...] += _dot_tn(x_ref[...], dy_ref[...])

        @pl.when(r == last)
        def _():
            o_ref[...] = acc_ref[...].astype(o_ref.dtype)

    if colshard:
        assert n == N_DEV * tn
        out_specs = pl.BlockSpec((None, ti, tn), lambda i, j, r: (j, i, 0))
        out_shape = jax.ShapeDtypeStruct((N_DEV, k1, tn), out_dtype)
    else:
        out_specs = pl.BlockSpec((ti, tn), lambda i, j, r: (i, j))
        out_shape = jax.ShapeDtypeStruct((k1, n), out_dtype)
    return pl.pallas_call(
        body, grid=(k1 // ti, n // tn, m // tm),
        in_specs=[pl.BlockSpec((tm, ti), lambda i, j, r: (r, i)), pl.BlockSpec((tm, tn), lambda i, j, r: (r, j))],
        out_specs=out_specs, out_shape=out_shape, scratch_shapes=[pltpu.VMEM((ti, tn), F32)], name=name,
        compiler_params=_params("arbitrary", "arbitrary", "arbitrary"),
    )(x, dy)


FF_BLOCK = D_FF // N_DEV


def _ff_cols(d):
    return slice(FF_BLOCK * d, FF_BLOCK * (d + 1))


def mlp_up(u, w_up, layer, name="mlp_up"):
    m = u.shape[0]
    tm = ROW_TILE

    def body(u_ref, w_ref, o_ref):
        u_ = u_ref[...]
        for d in range(N_DEV):
            r = jnp.maximum(_dot(u_, w_ref[d]), 0.0)
            o_ref[:, _ff_cols(d)] = (r * r).astype(o_ref.dtype)

    return pl.pallas_call(
        body, grid=(m // tm,),
        in_specs=[pl.BlockSpec((tm, D_MODEL), lambda i: (i, 0)),
                  pl.BlockSpec((N_DEV, None, D_MODEL, FF_BLOCK), lambda i: (0, layer, 0, 0))],
        out_specs=pl.BlockSpec((tm, D_FF), lambda i: (i, 0)),
        out_shape=jax.ShapeDtypeStruct((m, D_FF), MXU_DTYPE), name=name, compiler_params=_params("arbitrary"),
    )(u, w_up)


def mlp_dact(dh, w_down, act, layer, name="mlp_dact"):
    m = dh.shape[0]
    tm = ROW_TILE

    def body(dh_ref, w_ref, act_ref, o_ref):
        dh_ = dh_ref[...]
        for d in range(N_DEV):
            r = jnp.sqrt(act_ref[:, _ff_cols(d)].astype(F32))
            o_ref[:, _ff_cols(d)] = (_dot_nt(dh_, w_ref[d]) * (2.0 * r)).astype(o_ref.dtype)

    return pl.pallas_call(
        body, grid=(m // tm,),
        in_specs=[pl.BlockSpec((tm, D_MODEL), lambda i: (i, 0)),
                  pl.BlockSpec((N_DEV, None, FF_BLOCK, D_MODEL), lambda i: (0, layer, 0, 0)),
                  pl.BlockSpec((tm, D_FF), lambda i: (i, 0))],
        out_specs=pl.BlockSpec((tm, D_FF), lambda i: (i, 0)),
        out_shape=jax.ShapeDtypeStruct((m, D_FF), MXU_DTYPE), name=name, compiler_params=_params("arbitrary"),
    )(dh, w_down, act)


def mlp_du_rms_bwd(da, w_up, layer, h, g, dh_out, name="mlp_du"):
    m = da.shape[0]
    tm = ROW_TILE

    def body(da_ref, w_ref, h_ref, g_ref, dho_ref, dh_ref, dg_ref):
        @pl.when(pl.program_id(0) == 0)
        def _():
            dg_ref[...] = jnp.zeros_like(dg_ref)

        du = _dot_nt(da_ref[:, _ff_cols(0)], w_ref[0])
        for d in range(1, N_DEV):
            du = du + _dot_nt(da_ref[:, _ff_cols(d)], w_ref[d])
        x = h_ref[...]
        r = lax.rsqrt(jnp.mean(x * x, -1, keepdims=True) + EPS)
        gy = du * g_ref[...]
        dx = r * gy - x * ((r * r * r) * jnp.mean(x * gy, -1, keepdims=True))
        dh_ref[...] = dho_ref[...] + dx
        dg_ref[...] += jnp.sum(du * (x * r), axis=0, keepdims=True)

    row = pl.BlockSpec((tm, D_MODEL), lambda i: (i, 0))
    vec = pl.BlockSpec((1, D_MODEL), lambda i: (0, 0))
    return pl.pallas_call(
        body, grid=(m // tm,),
        in_specs=[pl.BlockSpec((tm, D_FF), lambda i: (i, 0)),
                  pl.BlockSpec((N_DEV, None, D_MODEL, FF_BLOCK), lambda i: (0, layer, 0, 0)), row, vec, row],
        out_specs=[row, vec],
        out_shape=[jax.ShapeDtypeStruct((m, D_MODEL), F32), jax.ShapeDtypeStruct((1, D_MODEL), F32)],
        name=name, compiler_params=_params("arbitrary"),
    )(da, w_up, h, g, dh_out)


def mlp_dw_up(u, da, name="mlp_dw_up"):
    m = u.shape[0]
    tm = ROW_TILE
    last = m // tm - 1

    def body(u_ref, da_ref, o_ref, acc_ref):
        i = pl.program_id(0)

        @pl.when(i == 0)
        def _():
            acc_ref[...] = jnp.zeros_like(acc_ref)

        acc_ref[...] += _dot_tn(u_ref[...], da_ref[...])

        @pl.when(i == last)
        def _():
            for d in range(N_DEV):
                o_ref[d] = acc_ref[:, _ff_cols(d)].astype(o_ref.dtype)

    return pl.pallas_call(
        body, grid=(m // tm,),
        in_specs=[pl.BlockSpec((tm, D_MODEL), lambda i: (i, 0)), pl.BlockSpec((tm, D_FF), lambda i: (i, 0))],
        out_specs=pl.BlockSpec((N_DEV, D_MODEL, FF_BLOCK), lambda i: (0, 0, 0)),
        out_shape=jax.ShapeDtypeStruct((N_DEV, D_MODEL, FF_BLOCK), GRAD_WIRE_DTYPE),
        scratch_shapes=[pltpu.VMEM((D_MODEL, D_FF), F32)], name=name, compiler_params=_params("arbitrary"),
    )(u, da)


def mlp_dw_down(act, dh, name="mlp_dw_down"):
    m = act.shape[0]
    tm = ROW_TILE
    last = m // tm - 1

    def body(a_ref, dh_ref, o_ref, acc_ref):
        i = pl.program_id(0)

        @pl.when(i == 0)
        def _():
            acc_ref[...] = jnp.zeros_like(acc_ref)

        acc_ref[...] += _dot_tn(a_ref[...], dh_ref[...])

        @pl.when(i == last)
        def _():
            o_ref[...] = acc_ref[...].astype(o_ref.dtype)

    return pl.pallas_call(
        body, grid=(m // tm,),
        in_specs=[pl.BlockSpec((tm, D_FF), lambda i: (i, 0)), pl.BlockSpec((tm, D_MODEL), lambda i: (i, 0))],
        out_specs=pl.BlockSpec((D_FF, D_MODEL), lambda i: (0, 0)),
        out_shape=jax.ShapeDtypeStruct((D_FF, D_MODEL), GRAD_WIRE_DTYPE),
        scratch_shapes=[pltpu.VMEM((D_FF, D_MODEL), F32)], name=name, compiler_params=_params("arbitrary"),
    )(act, dh)


_ATT_SCALE = HEAD_DIM ** -0.5


def _alibi_slope(h):
    return 2.0 ** (-8.0 * (h + 1) / ATT_HEADS)


HEADS_PER_KV = ATT_HEADS // ATT_KV_HEADS
STACK = HEADS_PER_KV * BLOCK
N_KEYS = 3 * BLOCK


def _head_select(width):
    c = jnp.arange(width)[:, None]
    h = jnp.arange(LANE)[None, :]
    return (c // HEAD_DIM == h).astype(F32)


def _head_fold(width):
    c = jnp.arange(width)[:, None]
    j = jnp.arange(LANE)[None, :]
    return (c % HEAD_DIM == j).astype(F32)


def _head_rms(x, sel, sel_t):
    r = lax.rsqrt(_dot_hi(x * x, sel) * (1.0 / HEAD_DIM) + EPS)
    return r, _dot_hi(r, sel_t)


def _head_norm_bwd(x, r, r_full, w_t, dy, sel, sel_t):
    gy = dy * w_t
    coef = _dot_hi((r * r * r) * _dot_hi(x * gy, sel) * (1.0 / HEAD_DIM), sel_t)
    return r_full * gy - x * coef, jnp.sum(dy * (x * r_full), axis=0, keepdims=True)


def _low_lanes(rows):
    return lax.broadcasted_iota(jnp.int32, (rows, LANE), 1) < HEAD_DIM


def _dup_half(a, g):
    rolled = pltpu.roll(a, HEAD_DIM, 1)
    low = _low_lanes(a.shape[0])
    return jnp.where(low, a, rolled) if g == 0 else jnp.where(low, rolled, a)


def _stack_heads(x, g):
    low = _low_lanes(BLOCK)
    parts = []
    for pair in range(2):
        p = x[:, 2 * LANE * g + LANE * pair:2 * LANE * g + LANE * (pair + 1)]
        parts += [jnp.where(low, p, 0.0), jnp.where(low, 0.0, p)]
    return jnp.concatenate(parts, axis=0)


def _unstack_heads(groups):
    low = _low_lanes(BLOCK)
    cols = []
    for o in groups:
        for pair in range(2):
            cols.append(jnp.where(low, o[2 * pair * BLOCK:(2 * pair + 1) * BLOCK], o[(2 * pair + 1) * BLOCK:(2 * pair + 2) * BLOCK]))
    return jnp.concatenate(cols, axis=1)


def _fold_halves(a, g):
    s = a + pltpu.roll(a, HEAD_DIM, 1)
    low = _low_lanes(a.shape[0])
    return jnp.where(low if g == 0 else jnp.logical_not(low), s, 0.0)


def _att_bias(b):
    r = lax.broadcasted_iota(jnp.int32, (STACK, N_KEYS), 0) & (BLOCK - 1)
    col = lax.broadcasted_iota(jnp.int32, (STACK, N_KEYS), 1)
    cc = col & (BLOCK - 1)
    is_meta = col < BLOCK
    is_prev = jnp.logical_and(col >= BLOCK, col < 2 * BLOCK)
    q_pos = b * BLOCK + r - FRONT_PAD
    meta_j = cc - FRONT_PAD
    valid_m = jnp.logical_and(cc >= FRONT_PAD, q_pos >= meta_j)
    valid_p = jnp.logical_and(cc > r, b >= 2)
    valid_c = jnp.logical_and(cc <= r, b >= 1)
    is_cur = col >= 2 * BLOCK
    valid = jnp.logical_or(jnp.logical_and(is_meta, valid_m),
                           jnp.logical_or(jnp.logical_and(is_prev, valid_p), jnp.logical_and(is_cur, valid_c)))
    dist = jnp.where(is_meta, jnp.minimum(q_pos - meta_j, BLOCK), jnp.where(is_prev, r - cc + BLOCK, r - cc))
    return valid, dist.astype(F32)


def _per_head_column(values):
    hid = lax.broadcasted_iota(jnp.int32, (STACK, 1), 0) >> 7
    col = jnp.where(hid == 0, values[0], values[1])
    for j in range(2, HEADS_PER_KV):
        col = jnp.where(hid == j, values[j], col)
    return col


def _att_group_probs(qs, kd, valid, dist, g, sk_ref):
    slope = _per_head_column([_alibi_slope(HEADS_PER_KV * g + j) for j in range(HEADS_PER_KV)])
    sink = _per_head_column([sk_ref[HEADS_PER_KV * g + j] for j in range(HEADS_PER_KV)])
    s = jnp.where(valid, _dot_nt(qs, kd) * _ATT_SCALE - slope * dist, -1e30)
    mx = jnp.maximum(jnp.max(s, axis=-1, keepdims=True), sink)
    p = jnp.exp(s - mx)
    p_sink = jnp.exp(sink - mx)
    inv = 1.0 / (jnp.sum(p, axis=-1, keepdims=True) + p_sink)
    return p * inv, p_sink * inv


def attention_fwd(proj, q_w, k_w, sinks, name="att_fwd"):
    m = proj.shape[0]
    nb = m // BLOCK
    cq, ck, cv = COL_Q // Q_W, COL_K // KV_W, COL_V // KV_W
    sel_q, sel_k = _head_select(Q_W), _head_select(KV_W)

    def body(q_ref, kc_ref, vc_ref, vp_ref, vm_ref, qw_ref, kw_ref, sk_ref, sq_ref, sqt_ref, skk_ref, skt_ref,
             o_ref, kpn_s, kmn_s):
        b = pl.program_id(0)
        q, kc = q_ref[...], kc_ref[...]
        _, rq = _head_rms(q, sq_ref[...], sqt_ref[...])
        qn = q * rq * qw_ref[...]
        _, rk = _head_rms(kc, skk_ref[...], skt_ref[...])
        kcn = kc * rk * kw_ref[...]

        @pl.when(b == 0)
        def _():
            kmn_s[...] = kcn
            kpn_s[...] = kcn

        kpn, kmn = kpn_s[...], kmn_s[...]
        kpn_s[...] = kcn
        vc, vp, vm = vc_ref[...], vp_ref[...], vm_ref[...]
        valid, dist = _att_bias(b)
        outs = []
        for g in range(ATT_KV_HEADS):
            kd = jnp.concatenate([_dup_half(kmn, g), _dup_half(kpn, g), _dup_half(kcn, g)], axis=0)
            vd = jnp.concatenate([_dup_half(vm, g), _dup_half(vp, g), _dup_half(vc, g)], axis=0)
            probs, _ = _att_group_probs(_stack_heads(qn, g), kd, valid, dist, g, sk_ref)
            outs.append(_dot(probs, vd))
        o_ref[...] = _unstack_heads(outs).astype(o_ref.dtype)

    prev = lambda i: jnp.maximum(i - 1, 0)
    full = lambda s: pl.BlockSpec(s, lambda i: (0,) * len(s))
    return pl.pallas_call(
        body, grid=(nb,),
        in_specs=[pl.BlockSpec((BLOCK, Q_W), lambda i: (i, cq)),
                  pl.BlockSpec((BLOCK, KV_W), lambda i: (i, ck)), pl.BlockSpec((BLOCK, KV_W), lambda i: (i, cv)),
                  pl.BlockSpec((BLOCK, KV_W), lambda i: (prev(i), cv)), pl.BlockSpec((BLOCK, KV_W), lambda i: (0, cv)),
                  full((1, Q_W)), full((1, KV_W)), pl.BlockSpec(memory_space=pltpu.SMEM),
                  full((Q_W, LANE)), full((LANE, Q_W)), full((KV_W, LANE)), full((LANE, KV_W))],
        out_specs=pl.BlockSpec((BLOCK, Q_W), lambda i: (i, 0)),
        out_shape=jax.ShapeDtypeStruct((m, Q_W), MXU_DTYPE),
        scratch_shapes=[pltpu.VMEM((BLOCK, KV_W), F32), pltpu.VMEM((BLOCK, KV_W), F32)],
        name=name, compiler_params=_params("arbitrary"),
    )(proj, proj, proj, proj, proj, q_w, k_w, sinks, sel_q, sel_q.T, sel_k, sel_k.T)


def attention_bwd(proj, dmix, q_w, k_w, sinks, name="att_bwd"):
    m = proj.shape[0]
    nb = m // BLOCK
    cq, ck, cv = COL_Q // Q_W, COL_K // KV_W, COL_V // KV_W
    c_datt = 0
    sel_q, sel_k = _head_select(Q_W), _head_select(KV_W)
    fold_q, fold_k = _head_fold(Q_W), _head_fold(KV_W)

    def body(do_ref, q_ref, kc_ref, vc_ref, kp_ref, vp_ref, km_ref, vm_ref, qw_ref, kw_ref, sk_ref,
             sq_ref, sqt_ref, skk_ref, skt_ref, fq_ref, fk_ref,
             dq_ref, dk_ref, dv_ref, dqw_ref, dkw_ref, dsk_ref, car_k, car_v, met_k, met_v, kmn_s, qw_acc, kw_acc):
        i = pl.program_id(0)
        b = nb - 1 - i
        sel_q_, sel_qt, sel_k_, sel_kt = sq_ref[...], sqt_ref[...], skk_ref[...], skt_ref[...]
        qw, kw = qw_ref[...], kw_ref[...]

        @pl.when(i == 0)
        def _():
            for r in (car_k, car_v, met_k, met_v, qw_acc, kw_acc, dsk_ref):
                r[...] = jnp.zeros_like(r)
            km = km_ref[...]
            kmn_s[...] = km * _head_rms(km, sel_k_, sel_kt)[1] * kw

        q, kc, kp = q_ref[...], kc_ref[...], kp_ref[...]
        rq, rq_full = _head_rms(q, sel_q_, sel_qt)
        qn = q * rq_full * qw
        rk, rk_full = _head_rms(kc, sel_k_, sel_kt)
        kcn = kc * rk_full * kw
        kpn = kp * _head_rms(kp, sel_k_, sel_kt)[1] * kw
        kmn = kmn_s[...]
        vc, vp, vm = vc_ref[...], vp_ref[...], vm_ref[...]
        do = do_ref[...]
        valid, dist = _att_bias(b)
        lane = lax.broadcasted_iota(jnp.int32, (1, LANE), 1)
        dsk = jnp.zeros((1, LANE), F32)
        dkd_sum = jnp.zeros((N_KEYS, KV_W), F32)
        dvd_sum = jnp.zeros((N_KEYS, KV_W), F32)
        dqd = []
        for g in range(ATT_KV_HEADS):
            kd = jnp.concatenate([_dup_half(kmn, g), _dup_half(kpn, g), _dup_half(kcn, g)], axis=0)
            vd = jnp.concatenate([_dup_half(vm, g), _dup_half(vp, g), _dup_half(vc, g)], axis=0)
            qs = _stack_heads(qn, g)
            dos = _stack_heads(do, g)
            probs, p_sink = _att_group_probs(qs, kd, valid, dist, g, sk_ref)
            o = _dot(probs, vd)
            delta = jnp.sum(dos * o, axis=-1, keepdims=True)
            ds = probs * (_dot_nt(dos, vd) - delta)
            dqd.append(_dot(ds, kd) * _ATT_SCALE)
            dkd_sum = dkd_sum + _fold_halves(_dot_tn(ds, qs) * _ATT_SCALE, g)
            dvd_sum = dvd_sum + _fold_halves(_dot_tn(probs, dos), g)
            sink_grad = p_sink * delta
            for j in range(HEADS_PER_KV):
                part = jnp.sum(sink_grad[BLOCK * j:BLOCK * (j + 1)], axis=0, keepdims=True)
                dsk = dsk - jnp.where(lane == HEADS_PER_KV * g + j, part, 0.0)
        dq, dqw = _head_norm_bwd(q, rq, rq_full, qw, _unstack_heads(dqd), sel_q_, sel_qt)
        dq_ref[...] = dq
        qw_acc[...] += dqw
        dsk_ref[...] += dsk

        met_k[...] += dkd_sum[0:BLOCK]
        met_v[...] += dvd_sum[0:BLOCK]
        first = (b == 0).astype(F32)
        dkn_tot = dkd_sum[2 * BLOCK:3 * BLOCK] + car_k[...] + first * met_k[...]
        dv_ref[...] = dvd_sum[2 * BLOCK:3 * BLOCK] + car_v[...] + first * met_v[...]
        car_k[...] = dkd_sum[BLOCK:2 * BLOCK]
        car_v[...] = dvd_sum[BLOCK:2 * BLOCK]
        dk, dkw = _head_norm_bwd(kc, rk, rk_full, kw, dkn_tot, sel_k_, sel_kt)
        dk_ref[...] = dk
        kw_acc[...] += dkw

        @pl.when(i == nb - 1)
        def _():
            dqw_ref[...] = _dot_hi(jnp.broadcast_to(qw_acc[...], (8, Q_W)), fq_ref[...])[0:1]
            dkw_ref[...] = _dot_hi(jnp.broadcast_to(kw_acc[...], (8, KV_W)), fk_ref[...])[0:1]

    blk = lambda i: nb - 1 - i
    prev = lambda i: jnp.maximum(nb - 2 - i, 0)
    full = lambda s: pl.BlockSpec(s, lambda i: (0,) * len(s))
    kv_scratch = pltpu.VMEM((BLOCK, KV_W), F32)
    return pl.pallas_call(
        body, grid=(nb,),
        in_specs=[pl.BlockSpec((BLOCK, Q_W), lambda i: (blk(i), c_datt)),
                  pl.BlockSpec((BLOCK, Q_W), lambda i: (blk(i), cq)),
                  pl.BlockSpec((BLOCK, KV_W), lambda i: (blk(i), ck)), pl.BlockSpec((BLOCK, KV_W), lambda i: (blk(i), cv)),
                  pl.BlockSpec((BLOCK, KV_W), lambda i: (prev(i), ck)), pl.BlockSpec((BLOCK, KV_W), lambda i: (prev(i), cv)),
                  pl.BlockSpec((BLOCK, KV_W), lambda i: (0, ck)), pl.BlockSpec((BLOCK, KV_W), lambda i: (0, cv)),
                  full((1, Q_W)), full((1, KV_W)), pl.BlockSpec(memory_space=pltpu.SMEM),
                  full((Q_W, LANE)), full((LANE, Q_W)), full((KV_W, LANE)), full((LANE, KV_W)),
                  full((Q_W, LANE)), full((KV_W, LANE))],
        out_specs=[pl.BlockSpec((BLOCK, Q_W), lambda i: (blk(i), 0)),
                   pl.BlockSpec((BLOCK, KV_W), lambda i: (blk(i), 0)), pl.BlockSpec((BLOCK, KV_W), lambda i: (blk(i), 0)),
                   full((1, LANE)), full((1, LANE)), full((1, LANE))],
        out_shape=[jax.ShapeDtypeStruct((m, Q_W), F32), jax.ShapeDtypeStruct((m, KV_W), F32),
                   jax.ShapeDtypeStruct((m, KV_W), F32), jax.ShapeDtypeStruct((1, LANE), F32),
                   jax.ShapeDtypeStruct((1, LANE), F32), jax.ShapeDtypeStruct((1, LANE), F32)],
        scratch_shapes=[kv_scratch, kv_scratch, kv_scratch, kv_scratch, kv_scratch,
                        pltpu.VMEM((1, Q_W), F32), pltpu.VMEM((1, KV_W), F32)],
        name=name, compiler_params=_params("arbitrary"),
    )(dmix, proj, proj, proj, proj, proj, proj, proj, q_w, k_w, sinks, sel_q, sel_q.T, sel_k, sel_k.T, fold_q, fold_k)


HALO = 8
GROUP_W = SSM_INNER // SSM_GROUPS
HEADS_PER_GROUP = SSM_HEADS // SSM_GROUPS


def _head_expand():
    h = jnp.arange(LANE)[:, None]
    c = jnp.arange(SSM_INNER)[None, :]
    return (c // SSM_HEAD_DIM == h).astype(F32)


def _softplus(x):
    return jnp.maximum(x, 0.0) + jnp.log1p(jnp.exp(-jnp.abs(x)))


def _ssd_decays(dt, a_log_row):
    row = lax.broadcasted_iota(jnp.int32, (BLOCK, BLOCK), 0)
    col = lax.broadcasted_iota(jnp.int32, (BLOCK, BLOCK), 1)
    lower = row >= col
    a = -jnp.exp(a_log_row)
    a_cs = _dot_hi(lower.astype(F32), dt * a)
    return a, a_cs, lower


def _decay_matrix(a_cs, a_cs_t, h, lower):
    diff = a_cs[:, h:h + 1] - a_cs_t[h:h + 1, :]
    return jnp.where(lower, jnp.exp(jnp.where(lower, diff, 0.0)), 0.0)


def _conv_taps(s_ref, w_ref, first, rows):
    acc = w_ref[0:1, :] * s_ref[pl.ds(first, rows), :]
    for j in range(1, SSM_CONV):
        acc = acc + w_ref[j:j + 1, :] * s_ref[pl.ds(first + j, rows), :]
    return acc


def ssd_fwd(proj, cw_x, cw_bc, cb_x, cb_bc, dt_bias, a_log, d_exp, norm_w, name="ssd_fwd"):
    m = proj.shape[0]
    nb = m // BLOCK
    expand = _head_expand()
    expand_t = expand.T

    def body(z_ref, xs_ref, bc_ref, dtr_ref, cwx_ref, cwbc_ref, cbx_ref, cbbc_ref, dtb_ref, alog_ref, dexp_ref,
             nw_ref, e_ref, et_ref, out_ref, prex_ref, prebc_ref, dt_ref, ypre_ref, st_ref, sx, sbc, state):
        c = pl.program_id(0)

        @pl.when(c == 0)
        def _():
            sx[0:HALO, :] = jnp.zeros((HALO, SSM_INNER), F32)
            sbc[0:HALO, :] = jnp.zeros((HALO, 2 * LANE), F32)
            state[...] = jnp.zeros_like(state)

        sx[HALO:HALO + BLOCK, :] = xs_ref[...]
        sbc[HALO:HALO + BLOCK, :] = bc_ref[...]
        first = HALO - (SSM_CONV - 1)
        pre_x = _conv_taps(sx, cwx_ref, first, BLOCK) + cbx_ref[...]
        pre_bc = _conv_taps(sbc, cwbc_ref, first, BLOCK) + cbbc_ref[...]
        sx[0:HALO, :] = xs_ref[BLOCK - HALO:BLOCK, :]
        sbc[0:HALO, :] = bc_ref[BLOCK - HALO:BLOCK, :]
        prex_ref[...] = pre_x
        prebc_ref[...] = pre_bc
        xc = pre_x * _sigmoid(pre_x)
        bcv = pre_bc * _sigmoid(pre_bc)

        rows = _row_ids(c * BLOCK, BLOCK, LANE)
        lanes = lax.broadcasted_iota(jnp.int32, (BLOCK, LANE), 1)
        live = jnp.logical_and(rows >= FRONT_PAD, lanes < SSM_HEADS)
        dt = jnp.where(live, _softplus(dtr_ref[...] + dtb_ref[...]), 0.0)
        dt_ref[...] = dt
        a, a_cs, lower = _ssd_decays(dt, alog_ref[...])
        a_cs_t = a_cs.T
        dt_t = dt.T
        e = e_ref[...]
        es_full = _dot_hi(jnp.exp(a_cs), e)
        wx_full = _dot_hi(jnp.exp(a_cs[BLOCK - 1:BLOCK, :] - a_cs) * dt, e)
        end_col = jnp.exp(a_cs_t[:, BLOCK - 1:BLOCK])
        dec_full = _dot_hi(et_ref[...], jnp.broadcast_to(end_col, (LANE, SSM_STATE)))

        st_ref[0] = state[...]
        ys = []
        for g in range(SSM_GROUPS):
            b_g = bcv[:, SSM_STATE * g:SSM_STATE * (g + 1)]
            c_g = bcv[:, LANE + SSM_STATE * g:LANE + SSM_STATE * (g + 1)]
            gs = slice(GROUP_W * g, GROUP_W * (g + 1))
            cb = _dot_nt(c_g, b_g)
            yd = []
            for hh in range(HEADS_PER_GROUP):
                h = g * HEADS_PER_GROUP + hh
                w = cb * _decay_matrix(a_cs, a_cs_t, h, lower) * dt_t[h:h + 1, :]
                yd.append(_dot(w, xc[:, SSM_HEAD_DIM * h:SSM_HEAD_DIM * (h + 1)]))
            h_g = state[gs, :]
            y_off = _dot_nt(c_g, h_g) * es_full[:, gs]
            ys.append(jnp.concatenate(yd, axis=1) + y_off)
            new_state = _dot_tn(xc[:, gs] * wx_full[:, gs], b_g)
            state[gs, :] = h_g * dec_full[gs, :] + new_state
        y_pre = jnp.concatenate(ys, axis=1) + xc * dexp_ref[...]
        ypre_ref[...] = y_pre
        z = z_ref[...]
        gt = y_pre * (z * _sigmoid(z))
        outs = []
        for g in range(SSM_GROUPS):
            gg = gt[:, GROUP_W * g:GROUP_W * (g + 1)]
            r = lax.rsqrt(jnp.mean(gg * gg, -1, keepdims=True) + EPS)
            outs.append(gg * r)
        out_ref[...] = (jnp.concatenate(outs, axis=1) * nw_ref[...]).astype(out_ref.dtype)

    full = lambda s: pl.BlockSpec(s, lambda i: (0,) * len(s))
    rowblk = lambda w, cidx: pl.BlockSpec((BLOCK, w), lambda i: (i, cidx))
    return pl.pallas_call(
        body, grid=(nb,),
        in_specs=[rowblk(SSM_INNER, COL_Z // SSM_INNER), rowblk(SSM_INNER, COL_XS // SSM_INNER),
                  rowblk(2 * LANE, COL_BC // (2 * LANE)), rowblk(LANE, COL_DT // LANE),
                  full((SSM_CONV, SSM_INNER)), full((SSM_CONV, 2 * LANE)), full((1, SSM_INNER)), full((1, 2 * LANE)),
                  full((1, LANE)), full((1, LANE)), full((1, SSM_INNER)), full((1, SSM_INNER)),
                  full((LANE, SSM_INNER)), full((SSM_INNER, LANE))],
        out_specs=[rowblk(SSM_INNER, 0), rowblk(SSM_INNER, 0), rowblk(2 * LANE, 0), rowblk(LANE, 0),
                   rowblk(SSM_INNER, 0), pl.BlockSpec((1, SSM_INNER, SSM_STATE), lambda i: (i, 0, 0))],
        out_shape=[jax.ShapeDtypeStruct((m, SSM_INNER), MXU_DTYPE), jax.ShapeDtypeStruct((m, SSM_INNER), F32),
                   jax.ShapeDtypeStruct((m, 2 * LANE), F32), jax.ShapeDtypeStruct((m, LANE), F32),
                   jax.ShapeDtypeStruct((m, SSM_INNER), F32), jax.ShapeDtypeStruct((nb, SSM_INNER, SSM_STATE), F32)],
        scratch_shapes=[pltpu.VMEM((HALO + BLOCK, SSM_INNER), F32), pltpu.VMEM((HALO + BLOCK, 2 * LANE), F32),
                        pltpu.VMEM((SSM_INNER, SSM_STATE), F32)],
        name=name, compiler_params=_params("arbitrary"),
    )(proj, proj, proj, proj, cw_x, cw_bc, cb_x, cb_bc, dt_bias, a_log, d_exp, norm_w, expand, expand_t)


def ssd_bwd(proj, dmix, pre_x, pre_bc, dt, y_pre, states, cw_x, cw_bc, dt_bias, a_log, d_exp, norm_w,
            name="ssd_bwd"):
    m = proj.shape[0]
    nb = m // BLOCK
    expand = _head_expand()
    expand_t = expand.T

    def body(do0_ref, do1_ref, z_ref, xs_ref, xsp_ref, bc_ref, bcp_ref, dtr_ref, prex_ref, prebc_ref, dt_ref, ypre_ref,
             st_ref,
             cwx_ref, cwbc_ref, dtb_ref, alog_ref, dexp_ref, nw_ref, e_ref, et_ref,
             dz_ref, dxs_ref, dbc_ref, ddt_ref, dcwx_ref, dcwbc_ref, dcbx_ref, dcbbc_ref, ddtb_ref, dalog_ref,
             dd_ref, dnw_ref,
             dstate, hnext, tx, tbc, sx, sbc, dlane):
        i = pl.program_id(0)
        c = nb - 1 - i

        @pl.when(i == 0)
        def _():
            dstate[...] = jnp.zeros_like(dstate)
            hnext[...] = jnp.zeros_like(hnext)
            tx[BLOCK:BLOCK + HALO, :] = jnp.zeros((HALO, SSM_INNER), F32)
            tbc[BLOCK:BLOCK + HALO, :] = jnp.zeros((HALO, 2 * LANE), F32)
            dlane[...] = jnp.zeros_like(dlane)
            for r in (dcwx_ref, dcwbc_ref, dcbx_ref, dcbbc_ref, ddtb_ref, dalog_ref, dd_ref, dnw_ref):
                r[...] = jnp.zeros_like(r)

        e = e_ref[...]
        et = et_ref[...]
        pre_x = prex_ref[...]
        pre_bc = prebc_ref[...]
        sig_x = _sigmoid(pre_x)
        sig_bc = _sigmoid(pre_bc)
        xc = pre_x * sig_x
        bcv = pre_bc * sig_bc
        dt = dt_ref[...]
        a, a_cs, lower = _ssd_decays(dt, alog_ref[...])
        a_cs_t = a_cs.T
        es_full = _dot_hi(jnp.exp(a_cs), e)
        ed_full = _dot_hi(jnp.exp(a_cs[BLOCK - 1:BLOCK, :] - a_cs), e)
        dt_full = _dot_hi(dt, e)
        end_col = jnp.exp(a_cs_t[:, BLOCK - 1:BLOCK])
        dec_full = _dot_hi(et, jnp.broadcast_to(end_col, (LANE, SSM_STATE)))
        dexp = dexp_ref[...]

        z = z_ref[...]
        zs = _sigmoid(z)
        sz = z * zs
        y_pre = ypre_ref[...]
        gt = y_pre * sz
        do = jnp.concatenate([do0_ref[...], do1_ref[...]], axis=1)
        nw = nw_ref[...]
        dgt = []
        dnw = []
        for g in range(SSM_GROUPS):
            gs = slice(GROUP_W * g, GROUP_W * (g + 1))
            gg = gt[:, gs]
            r = lax.rsqrt(jnp.mean(gg * gg, -1, keepdims=True) + EPS)
            gn = do[:, gs] * nw[:, gs]
            dgt.append(r * gn - gg * ((r * r * r) * jnp.mean(gg * gn, -1, keepdims=True)))
            dnw.append(jnp.sum(do[:, gs] * (gg * r), axis=0, keepdims=True))
        dgt = jnp.concatenate(dgt, axis=1)
        dnw_ref[...] += jnp.concatenate(dnw, axis=1)
        dy = dgt * sz
        dz_ref[...] = dgt * y_pre * (zs * (1.0 + z * (1.0 - zs)))
        dlane[...] += jnp.sum(dy * xc, axis=0, keepdims=True)
        xd = xc * dt_full

        lane_id = lax.broadcasted_iota(jnp.int32, (BLOCK, LANE), 1)
        sub_id = lax.broadcasted_iota(jnp.int32, (LANE, BLOCK), 0)
        ds_to = jnp.zeros((BLOCK, LANE), F32)
        ds_from_t = jnp.zeros((LANE, BLOCK), F32)
        dxd_parts, inter_parts = [], []
        dbs, dcs = [], []
        for g in range(SSM_GROUPS):
            gs = slice(GROUP_W * g, GROUP_W * (g + 1))
            b_g = bcv[:, SSM_STATE * g:SSM_STATE * (g + 1)]
            c_g = bcv[:, LANE + SSM_STATE * g:LANE + SSM_STATE * (g + 1)]
            cb = _dot_nt(c_g, b_g)
            dcb = jnp.zeros((BLOCK, BLOCK), F32)
            dxd_h = []
            for hh in range(HEADS_PER_GROUP):
                h = g * HEADS_PER_GROUP + hh
                hs = slice(SSM_HEAD_DIM * h, SSM_HEAD_DIM * (h + 1))
                lm = _decay_matrix(a_cs, a_cs_t, h, lower)
                dy_h = dy[:, hs]
                gl = _dot_nt(dy_h, xd[:, hs]) * lm
                dcb = dcb + gl
                e_h = gl * cb
                ds_to = ds_to + jnp.where(lane_id == h, jnp.sum(e_h, axis=-1, keepdims=True), 0.0)
                ds_from_t = ds_from_t + jnp.where(sub_id == h, jnp.sum(e_h, axis=0, keepdims=True), 0.0)
                dxd_h.append(_dot_tn(cb * lm, dy_h))
            h_g = st_ref[0, gs, :]
            dh_g = dstate[gs, :]
            dys_g = dy[:, gs] * es_full[:, gs]
            xde_g = xd[:, gs] * ed_full[:, gs]
            dcs.append(_dot(dcb, b_g) + _dot(dys_g, h_g))
            dbs.append(_dot_tn(dcb, c_g) + _dot(xde_g, dh_g))
            y_off = _dot_nt(c_g, h_g) * es_full[:, gs]
            dxd_state = _dot_nt(b_g, dh_g) * ed_full[:, gs]
            inter_parts.append(dy[:, gs] * y_off - xd[:, gs] * dxd_state)
            dxd_parts.append(jnp.concatenate(dxd_h, axis=1) + dxd_state)
            dstate[gs, :] = dh_g * dec_full[gs, :] + _dot_tn(dys_g, c_g)
            if g == 0:
                end_dot = hnext[gs, :] * dh_g
            else:
                end_dot = jnp.concatenate([end_dot, hnext[gs, :] * dh_g], axis=0)
        dxd = jnp.concatenate(dxd_parts, axis=1)
        hnext[...] = st_ref[0]

        ds = ds_to - ds_from_t.T + _dot_hi(jnp.concatenate(inter_parts, axis=1), et)
        ds_end = jnp.sum(_dot_tn_hi(end_dot, et), axis=0, keepdims=True)
        rows_l = lax.broadcasted_iota(jnp.int32, (BLOCK, LANE), 0)
        ds = ds + jnp.where(rows_l == BLOCK - 1, ds_end, 0.0)
        row = lax.broadcasted_iota(jnp.int32, (BLOCK, BLOCK), 0)
        col = lax.broadcasted_iota(jnp.int32, (BLOCK, BLOCK), 1)
        dadt = _dot_hi((col >= row).astype(F32), ds)
        ddt = dadt * a + _dot_hi(dxd * xc, et)
        dalog_ref[...] += jnp.sum(dadt * dt, axis=0, keepdims=True) * a
        rows = _row_ids(c * BLOCK, BLOCK, LANE)
        lanes = lax.broadcasted_iota(jnp.int32, (BLOCK, LANE), 1)
        live = jnp.logical_and(rows >= FRONT_PAD, lanes < SSM_HEADS)
        ddt_raw = jnp.where(live, ddt * _sigmoid(dtr_ref[...] + dtb_ref[...]), 0.0)
        ddt_ref[...] = ddt_raw
        ddtb_ref[...] += jnp.sum(ddt_raw, axis=0, keepdims=True)

        dxc = dxd * dt_full + dy * dexp
        dpre_x = dxc * (sig_x * (1.0 + pre_x * (1.0 - sig_x)))
        dpre_bc = jnp.concatenate(dbs + dcs, axis=1) * (sig_bc * (1.0 + pre_bc * (1.0 - sig_bc)))
        dcbx_ref[...] += jnp.sum(dpre_x, axis=0, keepdims=True)
        dcbbc_ref[...] += jnp.sum(dpre_bc, axis=0, keepdims=True)
        keep_x = _row_ids(c * BLOCK, BLOCK, SSM_INNER) >= FRONT_PAD
        keep_bc = _row_ids(c * BLOCK, BLOCK, 2 * LANE) >= FRONT_PAD
        prev_live = (c > 0).astype(F32)
        for (dpre, t_ref, s_ref, cur_ref, prv_ref, w_ref, dw_ref, dx_ref, keep) in (
                (dpre_x, tx, sx, xs_ref, xsp_ref, cwx_ref, dcwx_ref, dxs_ref, keep_x),
                (dpre_bc, tbc, sbc, bc_ref, bcp_ref, cwbc_ref, dcwbc_ref, dbc_ref, keep_bc)):
            t_ref[0:BLOCK, :] = dpre
            acc = w_ref[0:1, :] * t_ref[pl.ds(SSM_CONV - 1, BLOCK), :]
            for j in range(1, SSM_CONV):
                acc = acc + w_ref[j:j + 1, :] * t_ref[pl.ds(SSM_CONV - 1 - j, BLOCK), :]
            dx_ref[...] = jnp.where(keep, acc, 0.0)
            t_ref[BLOCK:BLOCK + HALO, :] = dpre[0:HALO, :]
            s_ref[0:HALO, :] = prv_ref[BLOCK - HALO:BLOCK, :] * prev_live
            s_ref[HALO:HALO + BLOCK, :] = cur_ref[...]
            first = HALO - (SSM_CONV - 1)
            for j in range(SSM_CONV):
                dw_ref[j:j + 1, :] += jnp.sum(dpre * s_ref[pl.ds(first + j, BLOCK), :], axis=0, keepdims=True)

        @pl.when(i == nb - 1)
        def _():
            dd_ref[...] = _dot_hi(jnp.broadcast_to(dlane[...], (HALO, SSM_INNER)), et)[0:1, :]

    blk = lambda i: nb - 1 - i
    prv = lambda i: jnp.maximum(nb - 2 - i, 0)
    full = lambda s: pl.BlockSpec(s, lambda i: (0,) * len(s))
    rowblk = lambda w, cidx: pl.BlockSpec((BLOCK, w), lambda i: (blk(i), cidx))
    prvblk = lambda w, cidx: pl.BlockSpec((BLOCK, w), lambda i: (prv(i), cidx))
    return pl.pallas_call(
        body, grid=(nb,),
        in_specs=[rowblk(GROUP_W, Q_W // GROUP_W), rowblk(GROUP_W, Q_W // GROUP_W + 1),
                  rowblk(SSM_INNER, COL_Z // SSM_INNER),
                  rowblk(SSM_INNER, COL_XS // SSM_INNER), prvblk(SSM_INNER, COL_XS // SSM_INNER),
                  rowblk(2 * LANE, COL_BC // (2 * LANE)), prvblk(2 * LANE, COL_BC // (2 * LANE)),
                  rowblk(LANE, COL_DT // LANE),
                  rowblk(SSM_INNER, 0), rowblk(2 * LANE, 0), rowblk(LANE, 0), rowblk(SSM_INNER, 0),
                  pl.BlockSpec((1, SSM_INNER, SSM_STATE), lambda i: (blk(i), 0, 0)),
                  full((SSM_CONV, SSM_INNER)), full((SSM_CONV, 2 * LANE)), full((1, LANE)), full((1, LANE)),
                  full((1, SSM_INNER)), full((1, SSM_INNER)), full((LANE, SSM_INNER)), full((SSM_INNER, LANE))],
        out_specs=[rowblk(SSM_INNER, 0), rowblk(SSM_INNER, 0), rowblk(2 * LANE, 0), rowblk(LANE, 0),
                   full((SSM_CONV, SSM_INNER)), full((SSM_CONV, 2 * LANE)), full((1, SSM_INNER)), full((1, 2 * LANE)),
                   full((1, LANE)), full((1, LANE)), full((1, LANE)), full((1, SSM_INNER))],
        out_shape=[jax.ShapeDtypeStruct((m, SSM_INNER), F32), jax.ShapeDtypeStruct((m, SSM_INNER), F32),
                   jax.ShapeDtypeStruct((m, 2 * LANE), F32), jax.ShapeDtypeStruct((m, LANE), F32),
                   jax.ShapeDtypeStruct((SSM_CONV, SSM_INNER), F32), jax.ShapeDtypeStruct((SSM_CONV, 2 * LANE), F32),
                   jax.ShapeDtypeStruct((1, SSM_INNER), F32), jax.ShapeDtypeStruct((1, 2 * LANE), F32),
                   jax.ShapeDtypeStruct((1, LANE), F32), jax.ShapeDtypeStruct((1, LANE), F32),
                   jax.ShapeDtypeStruct((1, LANE), F32), jax.ShapeDtypeStruct((1, SSM_INNER), F32)],
        scratch_shapes=[pltpu.VMEM((SSM_INNER, SSM_STATE), F32), pltpu.VMEM((SSM_INNER, SSM_STATE), F32),
                        pltpu.VMEM((BLOCK + HALO, SSM_INNER), F32), pltpu.VMEM((BLOCK + HALO, 2 * LANE), F32),
                        pltpu.VMEM((HALO + BLOCK, SSM_INNER), F32), pltpu.VMEM((HALO + BLOCK, 2 * LANE), F32),
                        pltpu.VMEM((1, SSM_INNER), F32)],
        name=name, compiler_params=_params("arbitrary"),
    )(dmix, dmix, proj, proj, proj, proj, proj, proj, pre_x, pre_bc, dt, y_pre, states,
      cw_x, cw_bc, dt_bias, a_log, d_exp, norm_w, expand, expand_t)


CONF_HALO = 32


def _glu_masked(v, first_row):
    a = v[:, :D_MODEL]
    s = _sigmoid(v[:, D_MODEL:])
    rows = _row_ids(first_row, v.shape[0], D_MODEL)
    return jnp.where(rows >= FRONT_PAD, a * s, 0.0), a, s


def _layer_norm_stats(c):
    mu = jnp.mean(c, -1, keepdims=True)
    xc = c - mu
    rstd = lax.rsqrt(jnp.mean(xc * xc, -1, keepdims=True) + LN_EPS)
    return xc * rstd, rstd


def conformer_mid_fwd(v, dw_w, dw_b, ln_g, ln_b, name="conf_mid_fwd"):
    m = v.shape[0]
    nb = m // BLOCK
    kpad = dw_w.shape[0]

    def body(vc_ref, vp_ref, w_ref, b_ref, g_ref, beta_ref, c_ref, s_ref, sg):
        i = pl.program_id(0)
        g_prev, _, _ = _glu_masked(vp_ref[BLOCK - CONF_HALO:BLOCK, :], (i - 1) * BLOCK + BLOCK - CONF_HALO)
        sg[0:CONF_HALO, :] = g_prev * (i > 0).astype(F32)
        g_cur, _, _ = _glu_masked(vc_ref[...], i * BLOCK)
        sg[CONF_HALO:CONF_HALO + BLOCK, :] = g_cur
        first = CONF_HALO - (CONF_KERNEL - 1)
        acc = b_ref[...] + w_ref[0:1, :] * sg[pl.ds(first, BLOCK), :]
        for j in range(1, CONF_KERNEL):
            acc = acc + w_ref[j:j + 1, :] * sg[pl.ds(first + j, BLOCK), :]
        c_ref[...] = acc
        xhat, _ = _layer_norm_stats(acc)
        nrm = xhat * g_ref[...] + beta_ref[...]
        s_ref[...] = (nrm * _sigmoid(nrm)).astype(s_ref.dtype)

    full = lambda s: pl.BlockSpec(s, lambda i: (0,) * len(s))
    return pl.pallas_call(
        body, grid=(nb,),
        in_specs=[pl.BlockSpec((BLOCK, 2 * D_MODEL), lambda i: (i, 0)),
                  pl.BlockSpec((BLOCK, 2 * D_MODEL), lambda i: (jnp.maximum(i - 1, 0), 0)),
                  full((kpad, D_MODEL)), full((1, D_MODEL)), full((1, D_MODEL)), full((1, D_MODEL))],
        out_specs=[pl.BlockSpec((BLOCK, D_MODEL), lambda i: (i, 0)), pl.BlockSpec((BLOCK, D_MODEL), lambda i: (i, 0))],
        out_shape=[jax.ShapeDtypeStruct((m, D_MODEL), F32), jax.ShapeDtypeStruct((m, D_MODEL), MXU_DTYPE)],
        scratch_shapes=[pltpu.VMEM((CONF_HALO + BLOCK, D_MODEL), F32)],
        name=name, compiler_params=_params("arbitrary"),
    )(v, v, dw_w, dw_b, ln_g, ln_b)


def conformer_ln_bwd(ds, c, ln_g, ln_b, name="conf_ln_bwd"):
    m, d = c.shape
    tm = ROW_TILE

    def body(ds_ref, c_ref, g_ref, beta_ref, dc_ref, dg_ref, db_ref):
        @pl.when(pl.program_id(0) == 0)
        def _():
            dg_ref[...] = jnp.zeros_like(dg_ref)
            db_ref[...] = jnp.zeros_like(db_ref)

        xhat, rstd = _layer_norm_stats(c_ref[...])
        g = g_ref[...]
        nrm = xhat * g + beta_ref[...]
        sg = _sigmoid(nrm)
        dn = ds_ref[...] * (sg * (1.0 + nrm * (1.0 - sg)))
        db_ref[...] += jnp.sum(dn, axis=0, keepdims=True)
        dg_ref[...] += jnp.sum(dn * xhat, axis=0, keepdims=True)
        dx = dn * g
        dc_ref[...] = rstd * (dx - jnp.mean(dx, -1, keepdims=True) - xhat * jnp.mean(dx * xhat, -1, keepdims=True))

    row = pl.BlockSpec((tm, d), lambda i: (i, 0))
    vec = pl.BlockSpec((1, d), lambda i: (0, 0))
    return pl.pallas_call(
        body, grid=(m // tm,), in_specs=[row, row, vec, vec], out_specs=[row, vec, vec],
        out_shape=[jax.ShapeDtypeStruct((m, d), F32), jax.ShapeDtypeStruct((1, d), F32), jax.ShapeDtypeStruct((1, d), F32)],
        name=name, compiler_params=_params("arbitrary"),
    )(ds, c, ln_g, ln_b)


def conformer_conv_bwd(dc, v, dw_w, name="conf_conv_bwd"):
    m = v.shape[0]
    nb = m // BLOCK
    kpad = dw_w.shape[0]

    def body(dcc_ref, dcn_ref, vc_ref, vp_ref, w_ref, dv_ref, dw_ref, db_ref, dvb_ref, tg, sg):
        i = pl.program_id(0)

        @pl.when(i == 0)
        def _():
            dw_ref[...] = jnp.zeros_like(dw_ref)
            db_ref[...] = jnp.zeros_like(db_ref)
            dvb_ref[...] = jnp.zeros_like(dvb_ref)

        dc_cur = dcc_ref[...]
        tg[0:BLOCK, :] = dc_cur
        tg[BLOCK:BLOCK + CONF_HALO, :] = dcn_ref[0:CONF_HALO, :] * (i < nb - 1).astype(F32)
        g_prev, _, _ = _glu_masked(vp_ref[BLOCK - CONF_HALO:BLOCK, :], (i - 1) * BLOCK + BLOCK - CONF_HALO)
        sg[0:CONF_HALO, :] = g_prev * (i > 0).astype(F32)
        g_cur, a, s = _glu_masked(vc_ref[...], i * BLOCK)
        sg[CONF_HALO:CONF_HALO + BLOCK, :] = g_cur
        db_ref[...] += jnp.sum(dc_cur, axis=0, keepdims=True)
        first = CONF_HALO - (CONF_KERNEL - 1)
        dg = w_ref[0:1, :] * tg[pl.ds(CONF_KERNEL - 1, BLOCK), :]
        for j in range(1, CONF_KERNEL):
            dg = dg + w_ref[j:j + 1, :] * tg[pl.ds(CONF_KERNEL - 1 - j, BLOCK), :]
        for j in range(CONF_KERNEL):
            dw_ref[j:j + 1, :] += jnp.sum(dc_cur * sg[pl.ds(first + j, BLOCK), :], axis=0, keepdims=True)
        rows = _row_ids(i * BLOCK, BLOCK, D_MODEL)
        dg = jnp.where(rows >= FRONT_PAD, dg, 0.0)
        da = dg * s
        dbv = dg * a * (s * (1.0 - s))
        dv = jnp.concatenate([da, dbv], axis=1)
        dv_ref[...] = dv.astype(dv_ref.dtype)
        dvb_ref[...] += jnp.sum(dv, axis=0, keepdims=True)

    full = lambda s: pl.BlockSpec(s, lambda i: (0,) * len(s))
    return pl.pallas_call(
        body, grid=(nb,),
        in_specs=[pl.BlockSpec((BLOCK, D_MODEL), lambda i: (i, 0)),
                  pl.BlockSpec((BLOCK, D_MODEL), lambda i: (jnp.minimum(i + 1, nb - 1), 0)),
                  pl.BlockSpec((BLOCK, 2 * D_MODEL), lambda i: (i, 0)),
                  pl.BlockSpec((BLOCK, 2 * D_MODEL), lambda i: (jnp.maximum(i - 1, 0), 0)),
                  full((kpad, D_MODEL))],
        out_specs=[pl.BlockSpec((BLOCK, 2 * D_MODEL), lambda i: (i, 0)), full((kpad, D_MODEL)),
                   full((1, D_MODEL)), full((1, 2 * D_MODEL))],
        out_shape=[jax.ShapeDtypeStruct((m, 2 * D_MODEL), MXU_DTYPE), jax.ShapeDtypeStruct((kpad, D_MODEL), F32),
                   jax.ShapeDtypeStruct((1, D_MODEL), F32), jax.ShapeDtypeStruct((1, 2 * D_MODEL), F32)],
        scratch_shapes=[pltpu.VMEM((BLOCK + CONF_HALO, D_MODEL), F32), pltpu.VMEM((CONF_HALO + BLOCK, D_MODEL), F32)],
        name=name, compiler_params=_params("arbitrary"),
    )(dc, dc, v, v, dw_w)


def _row(v, width=None):
    v = v.reshape(1, -1).astype(F32)
    if width is not None and v.shape[1] < width:
        v = jnp.pad(v, ((0, 0), (0, width - v.shape[1])))
    return v


def _w_in_to_kernel(w):
    pad = jnp.zeros((w.shape[0], PROJ_W - COL_DT - SSM_HEADS), w.dtype)
    return jnp.concatenate([w[:, 768:1792], w[:, 1792:2816], w[:, 0:512], w[:, 2816:3072], w[:, 512:640],
                            w[:, 640:768], w[:, 3072:3088], pad], axis=1)


def _w_in_from_kernel(g):
    return jnp.concatenate([g[:, COL_Q:COL_Q + Q_W], g[:, COL_K:COL_K + KV_W], g[:, COL_V:COL_V + KV_W],
                            g[:, COL_Z:COL_Z + SSM_INNER], g[:, COL_XS:COL_XS + SSM_INNER],
                            g[:, COL_BC:COL_BC + 2 * LANE], g[:, COL_DT:COL_DT + SSM_HEADS]], axis=1)


def even_fwd(h, p):
    u = rms_fwd(h, p["norm"])
    proj = matmul(u, p["w_in"], name="mm_proj")
    att = attention_fwd(proj, p["q_norm"], p["k_norm"], p["sinks"])
    ssm, pre_x, pre_bc, dt, y_pre, states = ssd_fwd(proj, p["cw_x"], p["cw_bc"], p["cb_x"], p["cb_bc"], p["dt_bias"],
                                                    p["a_log"], p["d_exp"], p["ssm_norm"])
    mix = jnp.concatenate([att, ssm], axis=1)
    out = matmul(mix, p["w_out"], b_kind="rowshard", layer=p["layer"], epilogue="resid", extra=h, name="mm_mix_out")
    return out, (h, u, proj, mix, pre_x, pre_bc, dt, y_pre, states)


def even_bwd(dh, p, saved):
    h, u, proj, mix, pre_x, pre_bc, dt, y_pre, states = saved
    dmix = matmul(dh, p["w_out"], b_kind="rowshard", layer=p["layer"], trans_b=True, name="mm_dmix")
    dw_out = matmul_tn(mix, dh, ti=512, tn=D_MODEL, out_dtype=GRAD_WIRE_DTYPE, name="mm_dw_out")
    dw_out = dw_out.reshape(N_DEV, MIX_W // N_DEV, D_MODEL)
    dq, dk, dv, dqw, dkw, dsk = attention_bwd(proj, dmix, p["q_norm"], p["k_norm"], p["sinks"])
    (dz, dxs, dbc, ddt, dcwx, dcwbc, dcbx, dcbbc, ddtb, dalog, dd, dnw) = ssd_bwd(
        proj, dmix, pre_x, pre_bc, dt, y_pre, states, p["cw_x"], p["cw_bc"], p["dt_bias"], p["a_log"], p["d_exp"],
        p["ssm_norm"])
    dproj = jnp.concatenate([dz, dxs, dq, dbc, dk, dv, ddt], axis=1).astype(MXU_DTYPE)
    du = matmul(dproj, p["w_in"], trans_b=True, name="mm_du_in")
    dw_in = matmul_tn(u, dproj, ti=512, tn=PROJ_W, name="mm_dw_in")
    dw_in = _to_shards(_w_in_from_kernel(dw_in), 1).astype(GRAD_WIRE_DTYPE)
    dh_in, dg = rms_bwd(h, p["norm"], du, dh)
    grads = dict(norm=dg, w_in=dw_in, w_out=dw_out, cw_x=dcwx, cw_bc=dcwbc, cb_x=dcbx, cb_bc=dcbbc, dt_bias=ddtb,
                 a_log=dalog, d_skip=dd, ssm_norm=dnw, q_norm=dqw, k_norm=dkw, sinks=dsk)
    return dh_in, grads


def conf_fwd(h, p):
    u = rms_fwd(h, p["norm"])
    v = matmul(u, p["pw1_w"], b_kind="colshard", layer=p["layer"], bias=p["pw1_b"], name="mm_pw1")
    c, s = conformer_mid_fwd(v, p["dw_w"], p["dw_b"], p["ln_g"], p["ln_b"])
    out = matmul(s, p["pw2_w"], b_kind="rowshard", layer=p["layer"], bias=p["pw2_b"], epilogue="resid", extra=h,
                 name="mm_pw2")
    return out, (h, u, v, c, s)


def conf_bwd(dh, p, saved):
    h, u, v, c, s = saved
    dpw2_b = col_sum(dh)
    ds = matmul(dh, p["pw2_w"], b_kind="rowshard", layer=p["layer"], trans_b=True, name="mm_ds")
    dpw2_w = matmul_tn(s, dh, ti=D_MODEL, tn=D_MODEL, out_dtype=GRAD_WIRE_DTYPE, name="mm_dpw2")
    dpw2_w = dpw2_w.reshape(N_DEV, D_MODEL // N_DEV, D_MODEL)
    dc, dln_g, dln_b = conformer_ln_bwd(ds, c, p["ln_g"], p["ln_b"])
    dv, ddw_w, ddw_b, dpw1_b = conformer_conv_bwd(dc, v, p["dw_w"])
    du = matmul_nt_acc(dv, p["pw1_w"], layer=p["layer"], name="mm_du_pw1")
    dpw1_w = matmul_tn(u, dv, ti=D_MODEL, tn=2 * D_MODEL // N_DEV, colshard=True, out_dtype=GRAD_WIRE_DTYPE,
                       name="mm_dpw1")
    dh_in, dg = rms_bwd(h, p["norm"], du, dh)
    grads = dict(norm=dg, pw1_w=dpw1_w, pw1_b=dpw1_b, dw_w=ddw_w, dw_b=ddw_b, ln_g=dln_g, ln_b=dln_b, pw2_w=dpw2_w,
                 pw2_b=dpw2_b)
    return dh_in, grads


def mlp_fwd(h, p):
    u = rms_fwd(h, p["norm"])
    act = mlp_up(u, p["w_up"], p["layer"])
    out = matmul(act, p["w_down"], b_kind="rowshard", layer=p["layer"], epilogue="resid", extra=h, name="mm_down")
    return out, (h, u, act)


def mlp_bwd(dh, p, saved):
    h, u, act = saved
    da = mlp_dact(dh, p["w_down"], act, p["layer"])
    dw_down = mlp_dw_down(act, dh).reshape(N_DEV, FF_BLOCK, D_MODEL)
    dw_up = mlp_dw_up(u, da)
    dh_in, dg = mlp_du_rms_bwd(da, p["w_up"], p["layer"], h, p["norm"], dh)
    return dh_in, dict(norm=dg, w_up=dw_up, w_down=dw_down)


def local_step(x, target, w):
    n_even, n_odd = (DEPTH + 1) // 2, DEPTH // 2
    h = jnp.concatenate([jnp.zeros((FRONT_PAD, D_MODEL), F32), w["meta_tokens"].astype(F32), x], axis=0)
    even_p, odd_p, mlp_p = [], [], []
    for i in range(n_even):
        cw = w["ssm_conv_w"][i]
        even_p.append(dict(
            layer=i, norm=_row(w["mix_norm_even"][i]), w_in=_w_in_to_kernel(_from_shards(w["w_in"][:, i], 1)),
            w_out=w["w_out"],
            cw_x=cw[:, :SSM_INNER], cw_bc=cw[:, SSM_INNER:], cb_x=_row(w["ssm_conv_b"][i][:SSM_INNER]),
            cb_bc=_row(w["ssm_conv_b"][i][SSM_INNER:]), dt_bias=_row(w["dt_bias"][i], LANE),
            a_log=_row(w["a_log"][i], LANE), d_exp=_row(jnp.repeat(w["d_skip"][i], SSM_HEAD_DIM)),
            ssm_norm=_row(w["ssm_norm_w"][i]), q_norm=_row(jnp.tile(w["q_norm"][i], ATT_HEADS)),
            k_norm=_row(jnp.tile(w["k_norm"][i], ATT_KV_HEADS)), sinks=w["sinks"][i].astype(F32)))
    for i in range(n_odd):
        odd_p.append(dict(
            layer=i, norm=_row(w["mix_norm_odd"][i]), pw1_w=w["pw1_w"], pw1_b=_row(w["pw1_b"][i]),
            dw_w=jnp.pad(w["dw_w"][i], ((0, CONF_HALO - CONF_KERNEL), (0, 0))), dw_b=_row(w["dw_b"][i]),
            ln_g=_row(w["ln_g"][i]), ln_b=_row(w["ln_b"][i]), pw2_w=w["pw2_w"], pw2_b=_row(w["pw2_b"][i])))
    for layer in range(DEPTH):
        mlp_p.append(dict(layer=layer, norm=_row(w["mlp_norm"][layer]), w_up=w["w_up"], w_down=w["w_down"]))

    tape = []
    for layer in range(DEPTH):
        if layer % 2 == 0:
            h, saved = even_fwd(h, even_p[layer // 2])
        else:
            h, saved = conf_fwd(h, odd_p[layer // 2])
        tape.append(saved)
        h, saved = mlp_fwd(h, mlp_p[layer])
        tape.append(saved)
    dh, loss_row = loss_fwd_bwd(h, target)

    ge = [None] * n_even
    go = [None] * n_odd
    gm = [None] * DEPTH
    for layer in reversed(range(DEPTH)):
        dh, gm[layer] = mlp_bwd(dh, mlp_p[layer], tape.pop())
        if layer % 2 == 0:
            dh, ge[layer // 2] = even_bwd(dh, even_p[layer // 2], tape.pop())
        else:
            dh, go[layer // 2] = conf_bwd(dh, odd_p[layer // 2], tape.pop())

    stack = lambda gs, f: jnp.stack([f(g) for g in gs])
    grads = dict(
        meta_tokens=dh[FRONT_PAD:BLOCK],
        mix_norm_even=stack(ge, lambda g: g["norm"][0]),
        w_in=[g["w_in"] for g in ge],
        ssm_conv_w=stack(ge, lambda g: jnp.concatenate([g["cw_x"], g["cw_bc"]], axis=1)),
        ssm_conv_b=stack(ge, lambda g: jnp.concatenate([g["cb_x"][0], g["cb_bc"][0]])),
        dt_bias=stack(ge, lambda g: g["dt_bias"][0, :SSM_HEADS]),
        a_log=stack(ge, lambda g: g["a_log"][0, :SSM_HEADS]),
        d_skip=stack(ge, lambda g: g["d_skip"][0, :SSM_HEADS]),
        ssm_norm_w=stack(ge, lambda g: g["ssm_norm"][0]),
        q_norm=stack(ge, lambda g: g["q_norm"][0, :HEAD_DIM]),
        k_norm=stack(ge, lambda g: g["k_norm"][0, :HEAD_DIM]),
        sinks=stack(ge, lambda g: g["sinks"][0, :ATT_HEADS]),
        w_out=[g["w_out"] for g in ge],
        mix_norm_odd=stack(go, lambda g: g["norm"][0]),
        pw1_w=[g["pw1_w"] for g in go],
        pw1_b=stack(go, lambda g: g["pw1_b"][0]),
        dw_w=stack(go, lambda g: g["dw_w"][:CONF_KERNEL]),
        dw_b=stack(go, lambda g: g["dw_b"][0]),
        ln_g=stack(go, lambda g: g["ln_g"][0]),
        ln_b=stack(go, lambda g: g["ln_b"][0]),
        pw2_w=[g["pw2_w"] for g in go],
        pw2_b=stack(go, lambda g: g["pw2_b"][0]),
        mlp_norm=stack(gm, lambda g: g["norm"][0]),
        w_up=[g["w_up"] for g in gm],
        w_down=[g["w_down"] for g in gm],
    )
    return loss_row[0, 0], dh[BLOCK:], grads


PARAMS = (
    ("meta_tokens", (16, 1024), 1), ("mix_norm_even", (2, 1024), None), ("w_in", (2, 1024, 3088), 2),
    ("ssm_conv_w", (2, 4, 1280), 2), ("ssm_conv_b", (2, 1280), None), ("dt_bias", (2, 16), None),
    ("a_log", (2, 16), None), ("d_skip", (2, 16), None), ("ssm_norm_w", (2, 1024), None), ("q_norm", (2, 64), None),
    ("k_norm", (2, 64), None), ("sinks", (2, 8), None), ("w_out", (2, 1536, 1024), 1), ("mix_norm_odd", (2, 1024), 1),
    ("pw1_w", (2, 1024, 2048), 2), ("pw1_b", (2, 2048), 1), ("dw_w", (2, 31, 1024), 2), ("dw_b", (2, 1024), 1),
    ("ln_g", (2, 1024), 1), ("ln_b", (2, 1024), 1), ("pw2_w", (2, 1024, 1024), 1), ("pw2_b", (2, 1024), 1),
    ("mlp_norm", (4, 1024), None), ("w_up", (4, 1024, 4096), 2), ("w_down", (4, 4096, 1024), 1),
)
MATMUL_WEIGHTS = ("w_in", "w_out", "pw1_w", "pw2_w", "w_up", "w_down")
PACK_ROW_ALIGN = 16 * PACK_W


def _block_shape(shape, axis):
    if axis is None:
        return tuple(shape)
    return tuple(s // N_DEV if a == axis else s for a, s in enumerate(shape))


def _numel(shape):
    return math.prod(shape)


def _pack(arrays, dtype):
    flat = jnp.concatenate([a.reshape(-1).astype(dtype) for a in arrays])
    n = flat.shape[0]
    padded = -(-n // PACK_ROW_ALIGN) * PACK_ROW_ALIGN
    return jnp.pad(flat, (0, padded - n)).reshape(-1, PACK_W)


def _pack_rows(arrays_by_dev, dtype):
    flat = jnp.concatenate([a.reshape(N_DEV, -1).astype(dtype) for a in arrays_by_dev], axis=1)
    n = flat.shape[1]
    padded = -(-n // PACK_ROW_ALIGN) * PACK_ROW_ALIGN
    return jnp.pad(flat, ((0, 0), (0, padded - n))).reshape(N_DEV, -1, PACK_W)


def _to_shards(full, axis):
    shape = full.shape
    split = full.reshape(shape[:axis] + (N_DEV, shape[axis] // N_DEV) + shape[axis + 1:])
    return jnp.moveaxis(split, axis, 0)


def _from_shards(blocks, axis):
    moved = jnp.moveaxis(blocks, 0, axis)
    shape = moved.shape
    return moved.reshape(shape[:axis] + (shape[axis] * shape[axis + 1],) + shape[axis + 2:])


_MESH = pl.DeviceIdType.MESH
_ANY = pl.BlockSpec(memory_space=pl.ANY)


def _mesh_place():
    x, y, c = lax.axis_index("x"), lax.axis_index("y"), lax.axis_index("c")
    return x, y, c


def _peer(x, y, c, rel):
    dx, dy, dc = (rel >> 2) & 1, (rel >> 1) & 1, rel & 1
    return (x ^ dx if dx else x, y ^ dy if dy else y, c ^ dc if dc else c)


def _dev_index(x, y, c):
    return 4 * x + 2 * y + c


def all_gather_weights(bigs, small):
    nt = len(bigs)

    def body(*refs):
        big_refs, small_ref = refs[:nt], refs[nt]
        big_outs, small_out = refs[nt + 1:2 * nt + 1], refs[2 * nt + 1]
        send_sems, recv_sems, small_send, small_recv, local_sems = refs[2 * nt + 2:]
        x, y, c = _mesh_place()
        me = (x, y, c)
        sibling = (x, y, 1 - c)
        chips = [(1 - x, y), (x, 1 - y), (1 - x, 1 - y)]

        def big_copy(t, k, block, to, from_input=False):
            dst = big_outs[t].at[_dev_index(*block)]
            return pltpu.make_async_remote_copy(src_ref=big_refs[t] if from_input else dst, dst_ref=dst,
                                                send_sem=send_sems.at[t, k], recv_sem=recv_sems.at[t, k],
                                                device_id=to, device_id_type=_MESH)

        def small_copy(rel, block, to):
            return pltpu.make_async_remote_copy(src_ref=small_ref, dst_ref=small_out.at[_dev_index(*block)],
                                                send_sem=small_send.at[rel - 1], recv_sem=small_recv.at[rel - 1],
                                                device_id=to, device_id_type=_MESH)

        mine = [pltpu.make_async_copy(big_refs[t], big_outs[t].at[_dev_index(*me)], local_sems.at[t]) for t in range(nt)]
        mine.append(pltpu.make_async_copy(small_ref, small_out.at[_dev_index(*me)], local_sems.at[nt]))
        for cp in mine:
            cp.start()
        first = []
        for t in range(nt):
            first.append(big_copy(t, 0, me, sibling, from_input=True))
            first += [big_copy(t, 1 + j, me, (*chip, c), from_input=True) for j, chip in enumerate(chips)]
        for cp in first:
            cp.start()
        smalls = [small_copy(rel, me, _peer(x, y, c, rel)) for rel in range(1, N_DEV)]
        for cp in smalls:
            cp.start()
        passed = []
        for j, chip in enumerate(chips):
            for t in range(nt):
                big_copy(t, 1 + j, (*chip, c), me).wait_recv()
                fwd = big_copy(t, 4 + j, (*chip, c), sibling)
                fwd.start()
                passed.append(fwd)
        for t in range(nt):
            big_copy(t, 0, sibling, me).wait_recv()
            for j, chip in enumerate(chips):
                big_copy(t, 4 + j, (*chip, 1 - c), me).wait_recv()
        for rel in range(1, N_DEV):
            small_copy(rel, _peer(x, y, c, rel), me).wait_recv()
        for cp in first + passed + smalls:
            cp.wait_send()
        for cp in mine:
            cp.wait()

    return pl.pallas_call(
        body, in_specs=[_ANY] * (nt + 1), out_specs=[_ANY] * (nt + 1),
        out_shape=[jax.ShapeDtypeStruct((N_DEV,) + b.shape, b.dtype) for b in bigs]
        + [jax.ShapeDtypeStruct((N_DEV,) + small.shape, small.dtype)],
        scratch_shapes=[pltpu.SemaphoreType.DMA((nt, N_DEV - 1)), pltpu.SemaphoreType.DMA((nt, N_DEV - 1)),
                        pltpu.SemaphoreType.DMA((N_DEV - 1,)), pltpu.SemaphoreType.DMA((N_DEV - 1,)),
                        pltpu.SemaphoreType.DMA((nt + 1,))],
        name="all_gather_weights",
    )(*bigs, small)


def exchange_gradients(groups):
    flat = [(gi, li, a) for gi, group in enumerate(groups) for li, a in enumerate(group)]
    n_in = len(flat)
    n_out = len(groups)

    def body(*refs):
        in_refs = refs[:n_in]
        out_refs = refs[n_in:n_in + n_out]
        send_sems, recv_sems, local_sems = refs[n_in + n_out:]
        x, y, c = _mesh_place()
        me = _dev_index(x, y, c)

        def copy(rel, p):
            gi, li, _ = flat[p]
            peer = _peer(x, y, c, rel)
            return pltpu.make_async_remote_copy(src_ref=in_refs[p].at[_dev_index(*peer)], dst_ref=out_refs[gi].at[me, li],
                                                send_sem=send_sems.at[rel - 1, p], recv_sem=recv_sems.at[rel - 1, p],
                                                device_id=peer, device_id_type=_MESH)

        def arrival(rel, p):
            gi, li, _ = flat[p]
            peer = _peer(x, y, c, rel)
            return pltpu.make_async_remote_copy(src_ref=in_refs[p].at[me], dst_ref=out_refs[gi].at[_dev_index(*peer), li],
                                                send_sem=send_sems.at[rel - 1, p], recv_sem=recv_sems.at[rel - 1, p],
                                                device_id=peer, device_id_type=_MESH)

        mine = [pltpu.make_async_copy(in_refs[p].at[me], out_refs[flat[p][0]].at[me, flat[p][1]], local_sems.at[p])
                for p in range(n_in)]
        for cp in mine:
            cp.start()
        copies = [copy(rel, p) for p in range(n_in) for rel in range(1, N_DEV)]
        for cp in copies:
            cp.start()
        for p in range(n_in):
            for rel in range(1, N_DEV):
                arrival(rel, p).wait_recv()
        for cp in copies:
            cp.wait_send()
        for cp in mine:
            cp.wait()

    return pl.pallas_call(
        body, in_specs=[_ANY] * n_in, out_specs=[_ANY] * n_out,
        out_shape=[jax.ShapeDtypeStruct((N_DEV, len(group)) + group[0].shape[1:], group[0].dtype) for group in groups],
        scratch_shapes=[pltpu.SemaphoreType.DMA((N_DEV - 1, n_in)), pltpu.SemaphoreType.DMA((N_DEV - 1, n_in)),
                        pltpu.SemaphoreType.DMA((n_in,))],
        name="exchange_gradients",
    )(*[a for _, _, a in flat])


def reduce_adamw(parts, w, m, v, tr):
    nl, r, cols = w.shape

    def body(p_ref, w_ref, m_ref, v_ref, g_ref, d_ref, nm_ref, nv_ref):
        g = p_ref[0].astype(F32)
        for d in range(1, N_DEV):
            g = g + p_ref[d].astype(F32)
        g_ref[...] = g
        nm = ADAM_B1 * m_ref[...] + (1.0 - ADAM_B1) * g
        nv = ADAM_B2 * v_ref[...] + (1.0 - ADAM_B2) * (g * g)
        nm_ref[...] = nm
        nv_ref[...] = nv
        m_hat = nm / (1.0 - ADAM_B1 ** ADAM_STEP)
        v_hat = nv / (1.0 - ADAM_B2 ** ADAM_STEP)
        d_ref[...] = -ADAM_LR * (m_hat / (jnp.sqrt(v_hat) + ADAM_EPS) + ADAM_WD * w_ref[...])

    row = pl.BlockSpec((None, tr, cols), lambda l, i: (l, i, 0))
    return pl.pallas_call(
        body, grid=(nl, r // tr),
        in_specs=[pl.BlockSpec((N_DEV, None, tr, cols), lambda l, i: (0, l, i, 0)), row, row, row],
        out_specs=[row, row, row, row], out_shape=[jax.ShapeDtypeStruct((nl, r, cols), F32)] * 4,
        name="reduce_adamw", compiler_params=_params("arbitrary", "arbitrary"),
    )(parts, w, m, v)


ADAMW_TILE_BYTES = 1 << 20


def _adamw_tile(rows, cols):
    lanes = -(-cols // LANE) * LANE
    best = None
    for tr in range(16, rows + 1, 16):
        if rows % tr == 0 and tr * lanes * 4 <= ADAMW_TILE_BYTES:
            best = tr
    if best is None:
        raise ValueError((rows, cols))
    return best


def kernel(x, meta_tokens, mix_norm_even, w_in, ssm_conv_w, ssm_conv_b, dt_bias, a_log, d_skip, ssm_norm_w, q_norm, k_norm, sinks, w_out, mix_norm_odd, pw1_w, pw1_b, dw_w, dw_b, ln_g, ln_b, pw2_w, pw2_b, mlp_norm, w_up, w_down, loss_target, m_meta_tokens, m_mix_norm_even, m_w_in, m_ssm_conv_w, m_ssm_conv_b, m_dt_bias, m_a_log, m_d_skip, m_ssm_norm_w, m_q_norm, m_k_norm, m_sinks, m_w_out, m_mix_norm_odd, m_pw1_w, m_pw1_b, m_dw_w, m_dw_b, m_ln_g, m_ln_b, m_pw2_w, m_pw2_b, m_mlp_norm, m_w_up, m_w_down, v_meta_tokens, v_mix_norm_even, v_w_in, v_ssm_conv_w, v_ssm_conv_b, v_dt_bias, v_a_log, v_d_skip, v_ssm_norm_w, v_q_norm, v_k_norm, v_sinks, v_w_out, v_mix_norm_odd, v_pw1_w, v_pw1_b, v_dw_w, v_dw_b, v_ln_g, v_ln_b, v_pw2_w, v_pw2_b, v_mlp_norm, v_w_up, v_w_down):
    names = [p[0] for p in PARAMS]
    w_loc = dict(zip(names, (meta_tokens, mix_norm_even, w_in, ssm_conv_w, ssm_conv_b, dt_bias, a_log, d_skip, ssm_norm_w, q_norm, k_norm, sinks, w_out, mix_norm_odd, pw1_w, pw1_b, dw_w, dw_b, ln_g, ln_b, pw2_w, pw2_b, mlp_norm, w_up, w_down)))
    m_loc = dict(zip(names, (m_meta_tokens, m_mix_norm_even, m_w_in, m_ssm_conv_w, m_ssm_conv_b, m_dt_bias, m_a_log, m_d_skip, m_ssm_norm_w, m_q_norm, m_k_norm, m_sinks, m_w_out, m_mix_norm_odd, m_pw1_w, m_pw1_b, m_dw_w, m_dw_b, m_ln_g, m_ln_b, m_pw2_w, m_pw2_b, m_mlp_norm, m_w_up, m_w_down)))
    v_loc = dict(zip(names, (v_meta_tokens, v_mix_norm_even, v_w_in, v_ssm_conv_w, v_ssm_conv_b, v_dt_bias, v_a_log, v_d_skip, v_ssm_norm_w, v_q_norm, v_k_norm, v_sinks, v_w_out, v_mix_norm_odd, v_pw1_w, v_pw1_b, v_dw_w, v_dw_b, v_ln_g, v_ln_b, v_pw2_w, v_pw2_b, v_mlp_norm, v_w_up, v_w_down)))
    small_sharded = [p for p in PARAMS if p[2] is not None and p[0] not in MATMUL_WEIGHTS]
    replicated = [p for p in PARAMS if p[2] is None]
    small_list = small_sharded + replicated

    gathered = all_gather_weights([w_loc[n].astype(MXU_DTYPE) for n in MATMUL_WEIGHTS],
                                  _pack([w_loc[n] for n, _, _ in small_sharded], F32))
    w_full = {n: w_loc[n] for n, _, _ in replicated}
    w_full.update(dict(zip(MATMUL_WEIGHTS, gathered[:-1])))
    flat = gathered[-1].reshape(N_DEV, -1)
    off = 0
    for n, shape, axis in small_sharded:
        blk = _block_shape(shape, axis)
        w_full[n] = _from_shards(flat[:, off:off + _numel(blk)].reshape((N_DEV,) + blk), axis)
        off += _numel(blk)

    loss_local, grad_x, g_full = local_step(x[0], loss_target[0], w_full)
    loss = lax.psum(loss_local, ("x", "y", "c"))

    by_dev = [_to_shards(g_full[n], axis) for n, _, axis in small_sharded]
    by_dev += [jnp.broadcast_to(g_full[n][None], (N_DEV,) + tuple(shape)) for n, shape, _ in replicated]
    parts = exchange_gradients([g_full[n] for n in MATMUL_WEIGHTS] + [[_pack_rows(by_dev, F32)]])

    out = {}
    for n, part in zip(MATMUL_WEIGHTS, parts[:-1]):
        nl, r, cols = w_loc[n].shape
        out[n] = reduce_adamw(part, w_loc[n], m_loc[n], v_loc[n], _adamw_tile(r, cols))
    pk = lambda d: _pack([d[n] for n, _, _ in small_list], F32)[None]
    rows = parts[-1].shape[2]
    small_out = reduce_adamw(parts[-1], pk(w_loc), pk(m_loc), pk(v_loc), _adamw_tile(rows, PACK_W))
    flats = [buf.reshape(-1) for buf in small_out]
    off = 0
    for n, shape, axis in small_list:
        blk = _block_shape(shape, axis)
        out[n] = tuple(f[off:off + _numel(blk)].reshape(blk) for f in flats)
        off += _numel(blk)
    return (loss, grad_x[None], *[out[n][0] for n in names], *[out[n][1] for n in names],
            *[out[n][2] for n in names], *[out[n][3] for n in names])
```

```python
import math

import jax
import jax.numpy as jnp
from jax import lax
from jax.experimental import pallas as pl
from jax.experimental.pallas import tpu as pltpu

F32 = jnp.float32
MXU_DTYPE = jnp.bfloat16
GRAD_WIRE_DTYPE = jnp.bfloat16
HIGHEST = lax.Precision.HIGHEST

D_MODEL = 1024
N_META = 16
BLOCK = 128
FRONT_PAD = BLOCK - N_META
ATT_HEADS = 8
ATT_KV_HEADS = 2
HEAD_DIM = 64
SSM_HEADS = 16
SSM_HEAD_DIM = 64
SSM_INNER = 1024
SSM_GROUPS = 2
SSM_STATE = 64
SSM_CONV = 4
CONF_KERNEL = 31
D_FF = 4096
EPS = 1e-6
LN_EPS = 1e-5
Q_W = 512
KV_W = 128
IN_W = 3088
MIX_W = 1536
DEPTH = 4
N_DEV = 8

ADAM_LR = 0.001
ADAM_B1 = 0.9
ADAM_B2 = 0.999
ADAM_EPS = 1e-08
ADAM_WD = 0.01
ADAM_STEP = 10

PROJ_W = 3200
COL_Z, COL_XS, COL_Q, COL_BC, COL_K, COL_V, COL_DT = 0, 1024, 2048, 2560, 2816, 2944, 3072

ROW_TILE = 640
VMEM_LIMIT = 56 * 1024 * 1024
LANE = 128
PACK_W = 1024


def _params(*sem):
    return pltpu.CompilerParams(dimension_semantics=sem, vmem_limit_bytes=VMEM_LIMIT)


def _mx(x):
    return x.astype(MXU_DTYPE)


def _dot(a, b):
    return jnp.dot(_mx(a), _mx(b), preferred_element_type=F32)


def _dot_nt(a, b):
    return lax.dot_general(_mx(a), _mx(b), (((1,), (1,)), ((), ())), preferred_element_type=F32)


def _dot_tn(a, b):
    return lax.dot_general(_mx(a), _mx(b), (((0,), (0,)), ((), ())), preferred_element_type=F32)


def _split3(x):
    hi = x.astype(jnp.bfloat16)
    r1 = x - hi.astype(F32)
    mid = r1.astype(jnp.bfloat16)
    lo = (r1 - mid.astype(F32)).astype(jnp.bfloat16)
    return hi, mid, lo


def _sel_dot(x, sel, dims):
    x_first = dims[2]
    if sel.dtype == jnp.bool_:
        sel = jnp.where(sel, 1.0, 0.0)
    one = sel.astype(jnp.bfloat16)
    acc = None
    for part in _split3(x):
        args = (part, one) if x_first else (one, part)
        t = lax.dot_general(*args, (dims[:2], ((), ())), preferred_element_type=F32)
        acc = t if acc is None else acc + t
    return acc


def _dot_hi(a, b, exact="a"):
    if exact == "a":
        return _sel_dot(a, b, ((1,), (0,), True))
    return _sel_dot(b, a, ((1,), (0,), False))


def _dot_tn_hi(a, b):
    return _sel_dot(a, b, ((0,), (0,), True))


def _sigmoid(x):
    return 1.0 / (1.0 + jnp.exp(-x))


def _row_ids(start, rows, cols):
    return start + lax.broadcasted_iota(jnp.int32, (rows, cols), 0)


def rms_fwd(h, g, name="rms_fwd"):
    m, d = h.shape
    tm = ROW_TILE

    def body(h_ref, g_ref, u_ref):
        x = h_ref[...]
        r = lax.rsqrt(jnp.mean(x * x, -1, keepdims=True) + EPS)
        u_ref[...] = ((x * r) * g_ref[...]).astype(u_ref.dtype)

    return pl.pallas_call(
        body, grid=(m // tm,),
        in_specs=[pl.BlockSpec((tm, d), lambda i: (i, 0)), pl.BlockSpec((1, d), lambda i: (0, 0))],
        out_specs=pl.BlockSpec((tm, d), lambda i: (i, 0)),
        out_shape=jax.ShapeDtypeStruct((m, d), MXU_DTYPE), name=name, compiler_params=_params("arbitrary"),
    )(h, g)


def rms_bwd(h, g, du, dh_out, name="rms_bwd"):
    m, d = h.shape
    tm = ROW_TILE

    def body(h_ref, g_ref, du_ref, dho_ref, dh_ref, dg_ref):
        @pl.when(pl.program_id(0) == 0)
        def _():
            dg_ref[...] = jnp.zeros_like(dg_ref)

        x = h_ref[...]
        du_ = du_ref[...]
        r = lax.rsqrt(jnp.mean(x * x, -1, keepdims=True) + EPS)
        gy = du_ * g_ref[...]
        dx = r * gy - x * ((r * r * r) * jnp.mean(x * gy, -1, keepdims=True))
        dh_ref[...] = dho_ref[...] + dx
        dg_ref[...] += jnp.sum(du_ * (x * r), axis=0, keepdims=True)

    row = pl.BlockSpec((tm, d), lambda i: (i, 0))
    vec = pl.BlockSpec((1, d), lambda i: (0, 0))
    return pl.pallas_call(
        body, grid=(m // tm,), in_specs=[row, vec, row, row], out_specs=[row, vec],
        out_shape=[jax.ShapeDtypeStruct((m, d), F32), jax.ShapeDtypeStruct((1, d), F32)],
        name=name, compiler_params=_params("arbitrary"),
    )(h, g, du, dh_out)


def loss_fwd_bwd(h, target, name="loss"):
    m, d = h.shape
    nb = m // BLOCK

    def body(h_ref, t_ref, dh_ref, l_ref):
        i = pl.program_id(0)

        @pl.when(i == 0)
        def _():
            l_ref[...] = jnp.zeros_like(l_ref)
            dh_ref[...] = jnp.zeros_like(dh_ref)

        @pl.when(i > 0)
        def _():
            e = h_ref[...] - t_ref[...]
            dh_ref[...] = e * (1.0 / d)
            s = jnp.sum(jnp.sum(e * e, axis=-1, keepdims=True), axis=0, keepdims=True)
            l_ref[...] += jnp.broadcast_to(s * (0.5 / d), l_ref.shape)

    return pl.pallas_call(
        body, grid=(nb,),
        in_specs=[pl.BlockSpec((BLOCK, d), lambda i: (i, 0)),
                  pl.BlockSpec((BLOCK, d), lambda i: (jnp.maximum(i - 1, 0), 0))],
        out_specs=[pl.BlockSpec((BLOCK, d), lambda i: (i, 0)), pl.BlockSpec((1, LANE), lambda i: (0, 0))],
        out_shape=[jax.ShapeDtypeStruct((m, d), F32), jax.ShapeDtypeStruct((1, LANE), F32)],
        name=name, compiler_params=_params("arbitrary"),
    )(h, target)


def col_sum(x, name="col_sum"):
    m, n = x.shape
    tm = ROW_TILE

    def body(x_ref, o_ref):
        @pl.when(pl.program_id(0) == 0)
        def _():
            o_ref[...] = jnp.zeros_like(o_ref)

        o_ref[...] += jnp.sum(x_ref[...].astype(F32), axis=0, keepdims=True)

    return pl.pallas_call(
        body, grid=(m // tm,), in_specs=[pl.BlockSpec((tm, n), lambda i: (i, 0))],
        out_specs=pl.BlockSpec((1, n), lambda i: (0, 0)), out_shape=jax.ShapeDtypeStruct((1, n), F32),
        name=name, compiler_params=_params("arbitrary"),
    )(x)


def matmul(a, b, *, b_kind="full", layer=0, trans_b=False, tn=None, epilogue=None, bias=None, extra=None,
           out_dtype=F32, name="matmul"):
    m, k = a.shape
    tm = ROW_TILE
    merge = False
    if b_kind == "full":
        n = b.shape[0] if trans_b else b.shape[1]
        tn = n if tn is None else tn
        b_spec = pl.BlockSpec((tn, k), lambda i, j: (j, 0)) if trans_b else pl.BlockSpec((k, tn), lambda i, j: (0, j))
    elif b_kind == "rowshard":
        ks, wn = b.shape[2], b.shape[3]
        if trans_b and tn == ks:
            assert wn == k
            n = N_DEV * ks
            b_spec = pl.BlockSpec((None, None, ks, wn), lambda i, j: (j, layer, 0, 0))
        else:
            assert tn is None
            merge = True
            n = N_DEV * ks if trans_b else wn
            assert (wn if trans_b else N_DEV * ks) == k
            tn = n
            b_spec = pl.BlockSpec((N_DEV, None, ks, wn), lambda i, j: (0, layer, 0, 0))
    else:
        raise ValueError(b_kind)
    has_bias = bias is not None
    has_extra = extra is not None

    def body(*refs):
        a_ref, b_ref = refs[0], refs[1]
        pos = 2
        bias_ref = extra_ref = None
        if has_bias:
            bias_ref = refs[pos]
            pos += 1
        if has_extra:
            extra_ref = refs[pos]
            pos += 1
        outs = refs[pos:]
        w = b_ref[...]
        if merge:
            w = w.reshape(N_DEV * w.shape[1], w.shape[2])
        if trans_b:
            acc = _dot_nt(a_ref[...], w)
        else:
            acc = _dot(a_ref[...], w)
        if has_bias:
            acc = acc + bias_ref[...]
        if epilogue is None:
            outs[0][...] = acc.astype(outs[0].dtype)
        elif epilogue == "relu2":
            outs[0][...] = acc
            r = jnp.maximum(acc, 0.0)
            outs[1][...] = (r * r).astype(outs[1].dtype)
        elif epilogue == "drelu2":
            outs[0][...] = (acc * (2.0 * jnp.maximum(extra_ref[...], 0.0))).astype(outs[0].dtype)
        elif epilogue == "resid":
            rows = _row_ids(pl.program_id(0) * tm, tm, tn)
            outs[0][...] = extra_ref[...] + jnp.where(rows >= FRONT_PAD, acc, 0.0)
        else:
            raise ValueError(epilogue)

    in_specs = [pl.BlockSpec((tm, k), lambda i, j: (i, 0)), b_spec]
    args = [a, b]
    if has_bias:
        in_specs.append(pl.BlockSpec((1, tn), lambda i, j: (0, j)))
        args.append(bias)
    if has_extra:
        in_specs.append(pl.BlockSpec((tm, tn), lambda i, j: (i, j)))
        args.append(extra)
    tile = pl.BlockSpec((tm, tn), lambda i, j: (i, j))
    if epilogue == "relu2":
        out_specs = [tile, tile]
        out_shape = [jax.ShapeDtypeStruct((m, n), F32), jax.ShapeDtypeStruct((m, n), MXU_DTYPE)]
    else:
        out_specs = tile
        out_shape = jax.ShapeDtypeStruct((m, n), out_dtype)
    return pl.pallas_call(
        body, grid=(m // tm, n // tn), in_specs=in_specs, out_specs=out_specs, out_shape=out_shape,
        name=name, compiler_params=_params("arbitrary", "arbitrary"),
    )(*args)


def matmul_tn(x, dy, *, ti, tn, out_dtype=F32, name="matmul_tn"):
    m, k1 = x.shape
    n = dy.shape[1]
    tm = ROW_TILE
    last = m // tm - 1

    def body(x_ref, dy_ref, o_ref, acc_ref):
        r = pl.program_id(2)

        @pl.when(r == 0)
        def _():
            acc_ref[...] = jnp.zeros_like(acc_ref)

        acc_ref[...] += _dot_tn(x_ref[...], dy_ref[...])

        @pl.when(r == last)
        def _():
            o_ref[...] = acc_ref[...].astype(o_ref.dtype)

    out_specs = pl.BlockSpec((ti, tn), lambda i, j, r: (i, j))
    out_shape = jax.ShapeDtypeStruct((k1, n), out_dtype)
    return pl.pallas_call(
        body, grid=(k1 // ti, n // tn, m // tm),
        in_specs=[pl.BlockSpec((tm, ti), lambda i, j, r: (r, i)), pl.BlockSpec((tm, tn), lambda i, j, r: (r, j))],
        out_specs=out_specs, out_shape=out_shape, scratch_shapes=[pltpu.VMEM((ti, tn), F32)], name=name,
        compiler_params=_params("arbitrary", "arbitrary", "arbitrary"),
    )(x, dy)


FF_BLOCK = D_FF // N_DEV


def _ff_cols(d):
    return slice(FF_BLOCK * d, FF_BLOCK * (d + 1))


def mlp_up(u, w, layer, *, bias=None, relu2=True, out_dtype=None, name="mlp_up"):
    m = u.shape[0]
    ns = w.shape[3]
    n = N_DEV * ns
    tm = ROW_TILE
    out_dtype = MXU_DTYPE if relu2 else out_dtype
    has_bias = bias is not None

    def body(*refs):
        u_ref, w_ref = refs[0], refs[1]
        bias_ref = refs[2] if has_bias else None
        o_ref = refs[-1]
        u_ = u_ref[...]
        for d in range(N_DEV):
            cols = slice(ns * d, ns * (d + 1))
            r = _dot(u_, w_ref[d])
            if has_bias:
                r = r + bias_ref[:, cols]
            if relu2:
                r = jnp.maximum(r, 0.0)
                r = r * r
            o_ref[:, cols] = r.astype(o_ref.dtype)

    in_specs = [pl.BlockSpec((tm, D_MODEL), lambda i: (i, 0)),
                pl.BlockSpec((N_DEV, None, D_MODEL, ns), lambda i: (0, layer, 0, 0))]
    args = [u, w]
    if has_bias:
        in_specs.append(pl.BlockSpec((1, n), lambda i: (0, 0)))
        args.append(bias)
    return pl.pallas_call(
        body, grid=(m // tm,), in_specs=in_specs, out_specs=pl.BlockSpec((tm, n), lambda i: (i, 0)),
        out_shape=jax.ShapeDtypeStruct((m, n), out_dtype), name=name, compiler_params=_params("arbitrary"),
    )(*args)


def mlp_dact(dh, w_down, act, layer, name="mlp_dact"):
    m = dh.shape[0]
    tm = ROW_TILE

    def body(dh_ref, w_ref, act_ref, o_ref):
        dh_ = dh_ref[...]
        for d in range(N_DEV):
            r = jnp.sqrt(act_ref[:, _ff_cols(d)].astype(F32))
            o_ref[:, _ff_cols(d)] = (_dot_nt(dh_, w_ref[d]) * (2.0 * r)).astype(o_ref.dtype)

    return pl.pallas_call(
        body, grid=(m // tm,),
        in_specs=[pl.BlockSpec((tm, D_MODEL), lambda i: (i, 0)),
                  pl.BlockSpec((N_DEV, None, FF_BLOCK, D_MODEL), lambda i: (0, layer, 0, 0)),
                  pl.BlockSpec((tm, D_FF), lambda i: (i, 0))],
        out_specs=pl.BlockSpec((tm, D_FF), lambda i: (i, 0)),
        out_shape=jax.ShapeDtypeStruct((m, D_FF), MXU_DTYPE), name=name, compiler_params=_params("arbitrary"),
    )(dh, w_down, act)


def mlp_du_rms_bwd(da, w_up, layer, h, g, dh_out, name="mlp_du"):
    m, n = da.shape
    ns = w_up.shape[3]
    assert n == N_DEV * ns
    tm = ROW_TILE

    def body(da_ref, w_ref, h_ref, g_ref, dho_ref, dh_ref, dg_ref):
        @pl.when(pl.program_id(0) == 0)
        def _():
            dg_ref[...] = jnp.zeros_like(dg_ref)

        du = _dot_nt(da_ref[:, 0:ns], w_ref[0])
        for d in range(1, N_DEV):
            du = du + _dot_nt(da_ref[:, ns * d:ns * (d + 1)], w_ref[d])
        x = h_ref[...]
        r = lax.rsqrt(jnp.mean(x * x, -1, keepdims=True) + EPS)
        gy = du * g_ref[...]
        dx = r * gy - x * ((r * r * r) * jnp.mean(x * gy, -1, keepdims=True))
        dh_ref[...] = dho_ref[...] + dx
        dg_ref[...] += jnp.sum(du * (x * r), axis=0, keepdims=True)

    row = pl.BlockSpec((tm, D_MODEL), lambda i: (i, 0))
    vec = pl.BlockSpec((1, D_MODEL), lambda i: (0, 0))
    return pl.pallas_call(
        body, grid=(m // tm,),
        in_specs=[pl.BlockSpec((tm, n), lambda i: (i, 0)),
                  pl.BlockSpec((N_DEV, None, D_MODEL, ns), lambda i: (0, layer, 0, 0)), row, vec, row],
        out_specs=[row, vec],
        out_shape=[jax.ShapeDtypeStruct((m, D_MODEL), F32), jax.ShapeDtypeStruct((1, D_MODEL), F32)],
        name=name, compiler_params=_params("arbitrary"),
    )(da, w_up, h, g, dh_out)


def mlp_dw_up(u, da, name="mlp_dw_up"):
    m, n = da.shape
    ns = n // N_DEV
    tm = ROW_TILE
    last = m // tm - 1

    def body(u_ref, da_ref, o_ref, acc_ref):
        i = pl.program_id(0)

        @pl.when(i == 0)
        def _():
            acc_ref[...] = jnp.zeros_like(acc_ref)

        acc_ref[...] += _dot_tn(u_ref[...], da_ref[...])

        @pl.when(i == last)
        def _():
            for d in range(N_DEV):
                o_ref[d] = acc_ref[:, ns * d:ns * (d + 1)].astype(o_ref.dtype)

    return pl.pallas_call(
        body, grid=(m // tm,),
        in_specs=[pl.BlockSpec((tm, D_MODEL), lambda i: (i, 0)), pl.BlockSpec((tm, n), lambda i: (i, 0))],
        out_specs=pl.BlockSpec((N_DEV, D_MODEL, ns), lambda i: (0, 0, 0)),
        out_shape=jax.ShapeDtypeStruct((N_DEV, D_MODEL, ns), GRAD_WIRE_DTYPE),
        scratch_shapes=[pltpu.VMEM((D_MODEL, n), F32)], name=name, compiler_params=_params("arbitrary"),
    )(u, da)


def mlp_dw_down(act, dh, name="mlp_dw_down"):
    m = act.shape[0]
    tm = ROW_TILE
    last = m // tm - 1

    def body(a_ref, dh_ref, o_ref, acc_ref):
        i = pl.program_id(0)

        @pl.when(i == 0)
        def _():
            acc_ref[...] = jnp.zeros_like(acc_ref)

        acc_ref[...] += _dot_tn(a_ref[...], dh_ref[...])

        @pl.when(i == last)
        def _():
            o_ref[...] = acc_ref[...].astype(o_ref.dtype)

    return pl.pallas_call(
        body, grid=(m // tm,),
        in_specs=[pl.BlockSpec((tm, D_FF), lambda i: (i, 0)), pl.BlockSpec((tm, D_MODEL), lambda i: (i, 0))],
        out_specs=pl.BlockSpec((D_FF, D_MODEL), lambda i: (0, 0)),
        out_shape=jax.ShapeDtypeStruct((D_FF, D_MODEL), GRAD_WIRE_DTYPE),
        scratch_shapes=[pltpu.VMEM((D_FF, D_MODEL), F32)], name=name, compiler_params=_params("arbitrary"),
    )(act, dh)


_ATT_SCALE = HEAD_DIM ** -0.5


def _alibi_slope(h):
    return 2.0 ** (-8.0 * (h + 1) / ATT_HEADS)


HEADS_PER_KV = ATT_HEADS // ATT_KV_HEADS
STACK = HEADS_PER_KV * BLOCK
N_KEYS = 3 * BLOCK


def _head_select(width):
    c = jnp.arange(width)[:, None]
    h = jnp.arange(LANE)[None, :]
    return (c // HEAD_DIM == h).astype(F32)


def _head_fold(width):
    c = jnp.arange(width)[:, None]
    j = jnp.arange(LANE)[None, :]
    return (c % HEAD_DIM == j).astype(F32)


def _head_rms(x, sel, sel_t):
    r = lax.rsqrt(_dot_hi(x * x, sel) * (1.0 / HEAD_DIM) + EPS)
    return r, _dot_hi(r, sel_t)


def _head_norm_bwd(x, r, r_full, w_t, dy, sel, sel_t):
    gy = dy * w_t
    coef = _dot_hi((r * r * r) * _dot_hi(x * gy, sel) * (1.0 / HEAD_DIM), sel_t)
    return r_full * gy - x * coef, jnp.sum(dy * (x * r_full), axis=0, keepdims=True)


def _low_lanes(rows):
    return lax.broadcasted_iota(jnp.int32, (rows, LANE), 1) < HEAD_DIM


def _dup_half(a, g):
    rolled = pltpu.roll(a, HEAD_DIM, 1)
    low = _low_lanes(a.shape[0])
    return jnp.where(low, a, rolled) if g == 0 else jnp.where(low, rolled, a)


def _stack_heads(x, g):
    low = _low_lanes(BLOCK)
    parts = []
    for pair in range(2):
        p = x[:, 2 * LANE * g + LANE * pair:2 * LANE * g + LANE * (pair + 1)]
        parts += [jnp.where(low, p, 0.0), jnp.where(low, 0.0, p)]
    return jnp.concatenate(parts, axis=0)


def _unstack_heads(groups):
    low = _low_lanes(BLOCK)
    cols = []
    for o in groups:
        for pair in range(2):
            cols.append(jnp.where(low, o[2 * pair * BLOCK:(2 * pair + 1) * BLOCK], o[(2 * pair + 1) * BLOCK:(2 * pair + 2) * BLOCK]))
    return jnp.concatenate(cols, axis=1)


def _fold_halves(a, g):
    s = a + pltpu.roll(a, HEAD_DIM, 1)
    low = _low_lanes(a.shape[0])
    return jnp.where(low if g == 0 else jnp.logical_not(low), s, 0.0)


def _att_bias(b):
    r = lax.broadcasted_iota(jnp.int32, (STACK, N_KEYS), 0) & (BLOCK - 1)
    col = lax.broadcasted_iota(jnp.int32, (STACK, N_KEYS), 1)
    cc = col & (BLOCK - 1)
    is_meta = col < BLOCK
    is_prev = jnp.logical_and(col >= BLOCK, col < 2 * BLOCK)
    q_pos = b * BLOCK + r - FRONT_PAD
    meta_j = cc - FRONT_PAD
    valid_m = jnp.logical_and(cc >= FRONT_PAD, q_pos >= meta_j)
    valid_p = jnp.logical_and(cc > r, b >= 2)
    valid_c = jnp.logical_and(cc <= r, b >= 1)
    is_cur = col >= 2 * BLOCK
    valid = jnp.logical_or(jnp.logical_and(is_meta, valid_m),
                           jnp.logical_or(jnp.logical_and(is_prev, valid_p), jnp.logical_and(is_cur, valid_c)))
    dist = jnp.where(is_meta, jnp.minimum(q_pos - meta_j, BLOCK), jnp.where(is_prev, r - cc + BLOCK, r - cc))
    return valid, dist.astype(F32)


def _per_head_column(values):
    hid = lax.broadcasted_iota(jnp.int32, (STACK, 1), 0) >> 7
    col = jnp.where(hid == 0, values[0], values[1])
    for j in range(2, HEADS_PER_KV):
        col = jnp.where(hid == j, values[j], col)
    return col


def _att_group_probs(qs, kd, valid, dist, g, sk_ref):
    slope = _per_head_column([_alibi_slope(HEADS_PER_KV * g + j) for j in range(HEADS_PER_KV)])
    sink = _per_head_column([sk_ref[HEADS_PER_KV * g + j] for j in range(HEADS_PER_KV)])
    s = jnp.where(valid, _dot_nt(qs, kd) * _ATT_SCALE - slope * dist, -1e30)
    mx = jnp.maximum(jnp.max(s, axis=-1, keepdims=True), sink)
    p = jnp.exp(s - mx)
    p_sink = jnp.exp(sink - mx)
    inv = 1.0 / (jnp.sum(p, axis=-1, keepdims=True) + p_sink)
    return p * inv, p_sink * inv


def attention_fwd(proj, q_w, k_w, sinks, name="att_fwd"):
    m = proj.shape[0]
    nb = m // BLOCK
    cq, ck, cv = COL_Q // Q_W, COL_K // KV_W, COL_V // KV_W
    sel_q, sel_k = _head_select(Q_W), _head_select(KV_W)

    def body(q_ref, kc_ref, vc_ref, vp_ref, vm_ref, qw_ref, kw_ref, sk_ref, sq_ref, sqt_ref, skk_ref, skt_ref,
             o_ref, kpn_s, kmn_s):
        b = pl.program_id(0)
        q, kc = q_ref[...], kc_ref[...]
        _, rq = _head_rms(q, sq_ref[...], sqt_ref[...])
        qn = q * rq * qw_ref[...]
        _, rk = _head_rms(kc, skk_ref[...], skt_ref[...])
        kcn = kc * rk * kw_ref[...]

        @pl.when(b == 0)
        def _():
            kmn_s[...] = kcn
            kpn_s[...] = kcn

        kpn, kmn = kpn_s[...], kmn_s[...]
        kpn_s[...] = kcn
        vc, vp, vm = vc_ref[...], vp_ref[...], vm_ref[...]
        valid, dist = _att_bias(b)
        outs = []
        for g in range(ATT_KV_HEADS):
            kd = jnp.concatenate([_dup_half(kmn, g), _dup_half(kpn, g), _dup_half(kcn, g)], axis=0)
            vd = jnp.concatenate([_dup_half(vm, g), _dup_half(vp, g), _dup_half(vc, g)], axis=0)
            probs, _ = _att_group_probs(_stack_heads(qn, g), kd, valid, dist, g, sk_ref)
            outs.append(_dot(probs, vd))
        o_ref[...] = _unstack_heads(outs).astype(o_ref.dtype)

    prev = lambda i: jnp.maximum(i - 1, 0)
    full = lambda s: pl.BlockSpec(s, lambda i: (0,) * len(s))
    return pl.pallas_call(
        body, grid=(nb,),
        in_specs=[pl.BlockSpec((BLOCK, Q_W), lambda i: (i, cq)),
                  pl.BlockSpec((BLOCK, KV_W), lambda i: (i, ck)), pl.BlockSpec((BLOCK, KV_W), lambda i: (i, cv)),
                  pl.BlockSpec((BLOCK, KV_W), lambda i: (prev(i), cv)), pl.BlockSpec((BLOCK, KV_W), lambda i: (0, cv)),
                  full((1, Q_W)), full((1, KV_W)), pl.BlockSpec(memory_space=pltpu.SMEM),
                  full((Q_W, LANE)), full((LANE, Q_W)), full((KV_W, LANE)), full((LANE, KV_W))],
        out_specs=pl.BlockSpec((BLOCK, Q_W), lambda i: (i, 0)),
        out_shape=jax.ShapeDtypeStruct((m, Q_W), MXU_DTYPE),
        scratch_shapes=[pltpu.VMEM((BLOCK, KV_W), F32), pltpu.VMEM((BLOCK, KV_W), F32)],
        name=name, compiler_params=_params("arbitrary"),
    )(proj, proj, proj, proj, proj, q_w, k_w, sinks, sel_q, sel_q.T, sel_k, sel_k.T)


def attention_bwd(proj, dmix, q_w, k_w, sinks, name="att_bwd"):
    m = proj.shape[0]
    nb = m // BLOCK
    cq, ck, cv = COL_Q // Q_W, COL_K // KV_W, COL_V // KV_W
    c_datt = 0
    sel_q, sel_k = _head_select(Q_W), _head_select(KV_W)
    fold_q, fold_k = _head_fold(Q_W), _head_fold(KV_W)

    def body(do_ref, q_ref, kc_ref, vc_ref, kp_ref, vp_ref, km_ref, vm_ref, qw_ref, kw_ref, sk_ref,
             sq_ref, sqt_ref, skk_ref, skt_ref, fq_ref, fk_ref,
             dq_ref, dk_ref, dv_ref, dqw_ref, dkw_ref, dsk_ref, car_k, car_v, met_k, met_v, kmn_s, qw_acc, kw_acc):
        i = pl.program_id(0)
        b = nb - 1 - i
        sel_q_, sel_qt, sel_k_, sel_kt = sq_ref[...], sqt_ref[...], skk_ref[...], skt_ref[...]
        qw, kw = qw_ref[...], kw_ref[...]

        @pl.when(i == 0)
        def _():
            for r in (car_k, car_v, met_k, met_v, qw_acc, kw_acc, dsk_ref):
                r[...] = jnp.zeros_like(r)
            km = km_ref[...]
            kmn_s[...] = km * _head_rms(km, sel_k_, sel_kt)[1] * kw

        q, kc, kp = q_ref[...], kc_ref[...], kp_ref[...]
        rq, rq_full = _head_rms(q, sel_q_, sel_qt)
        qn = q * rq_full * qw
        rk, rk_full = _head_rms(kc, sel_k_, sel_kt)
        kcn = kc * rk_full * kw
        kpn = kp * _head_rms(kp, sel_k_, sel_kt)[1] * kw
        kmn = kmn_s[...]
        vc, vp, vm = vc_ref[...], vp_ref[...], vm_ref[...]
        do = do_ref[...]
        valid, dist = _att_bias(b)
        lane = lax.broadcasted_iota(jnp.int32, (1, LANE), 1)
        dsk = jnp.zeros((1, LANE), F32)
        dkd_sum = jnp.zeros((N_KEYS, KV_W), F32)
        dvd_sum = jnp.zeros((N_KEYS, KV_W), F32)
        dqd = []
        for g in range(ATT_KV_HEADS):
            kd = jnp.concatenate([_dup_half(kmn, g), _dup_half(kpn, g), _dup_half(kcn, g)], axis=0)
            vd = jnp.concatenate([_dup_half(vm, g), _dup_half(vp, g), _dup_half(vc, g)], axis=0)
            qs = _stack_heads(qn, g)
            dos = _stack_heads(do, g)
            probs, p_sink = _att_group_probs(qs, kd, valid, dist, g, sk_ref)
            o = _dot(probs, vd)
            delta = jnp.sum(dos * o, axis=-1, keepdims=True)
            ds = probs * (_dot_nt(dos, vd) - delta)
            dqd.append(_dot(ds, kd) * _ATT_SCALE)
            dkd_sum = dkd_sum + _fold_halves(_dot_tn(ds, qs) * _ATT_SCALE, g)
            dvd_sum = dvd_sum + _fold_halves(_dot_tn(probs, dos), g)
            sink_grad = p_sink * delta
            for j in range(HEADS_PER_KV):
                part = jnp.sum(sink_grad[BLOCK * j:BLOCK * (j + 1)], axis=0, keepdims=True)
                dsk = dsk - jnp.where(lane == HEADS_PER_KV * g + j, part, 0.0)
        dq, dqw = _head_norm_bwd(q, rq, rq_full, qw, _unstack_heads(dqd), sel_q_, sel_qt)
        dq_ref[...] = dq
        qw_acc[...] += dqw
        dsk_ref[...] += dsk

        met_k[...] += dkd_sum[0:BLOCK]
        met_v[...] += dvd_sum[0:BLOCK]
        first = (b == 0).astype(F32)
        dkn_tot = dkd_sum[2 * BLOCK:3 * BLOCK] + car_k[...] + first * met_k[...]
        dv_ref[...] = dvd_sum[2 * BLOCK:3 * BLOCK] + car_v[...] + first * met_v[...]
        car_k[...] = dkd_sum[BLOCK:2 * BLOCK]
        car_v[...] = dvd_sum[BLOCK:2 * BLOCK]
        dk, dkw = _head_norm_bwd(kc, rk, rk_full, kw, dkn_tot, sel_k_, sel_kt)
        dk_ref[...] = dk
        kw_acc[...] += dkw

        @pl.when(i == nb - 1)
        def _():
            dqw_ref[...] = _dot_hi(jnp.broadcast_to(qw_acc[...], (8, Q_W)), fq_ref[...])[0:1]
            dkw_ref[...] = _dot_hi(jnp.broadcast_to(kw_acc[...], (8, KV_W)), fk_ref[...])[0:1]

    blk = lambda i: nb - 1 - i
    prev = lambda i: jnp.maximum(nb - 2 - i, 0)
    full = lambda s: pl.BlockSpec(s, lambda i: (0,) * len(s))
    kv_scratch = pltpu.VMEM((BLOCK, KV_W), F32)
    return pl.pallas_call(
        body, grid=(nb,),
        in_specs=[pl.BlockSpec((BLOCK, Q_W), lambda i: (blk(i), c_datt)),
                  pl.BlockSpec((BLOCK, Q_W), lambda i: (blk(i), cq)),
                  pl.BlockSpec((BLOCK, KV_W), lambda i: (blk(i), ck)), pl.BlockSpec((BLOCK, KV_W), lambda i: (blk(i), cv)),
                  pl.BlockSpec((BLOCK, KV_W), lambda i: (prev(i), ck)), pl.BlockSpec((BLOCK, KV_W), lambda i: (prev(i), cv)),
                  pl.BlockSpec((BLOCK, KV_W), lambda i: (0, ck)), pl.BlockSpec((BLOCK, KV_W), lambda i: (0, cv)),
                  full((1, Q_W)), full((1, KV_W)), pl.BlockSpec(memory_space=pltpu.SMEM),
                  full((Q_W, LANE)), full((LANE, Q_W)), full((KV_W, LANE)), full((LANE, KV_W)),
                  full((Q_W, LANE)), full((KV_W, LANE))],
        out_specs=[pl.BlockSpec((BLOCK, Q_W), lambda i: (blk(i), 0)),
                   pl.BlockSpec((BLOCK, KV_W), lambda i: (blk(i), 0)), pl.BlockSpec((BLOCK, KV_W), lambda i: (blk(i), 0)),
                   full((1, LANE)), full((1, LANE)), full((1, LANE))],
        out_shape=[jax.ShapeDtypeStruct((m, Q_W), F32), jax.ShapeDtypeStruct((m, KV_W), F32),
                   jax.ShapeDtypeStruct((m, KV_W), F32), jax.ShapeDtypeStruct((1, LANE), F32),
                   jax.ShapeDtypeStruct((1, LANE), F32), jax.ShapeDtypeStruct((1, LANE), F32)],
        scratch_shapes=[kv_scratch, kv_scratch, kv_scratch, kv_scratch, kv_scratch,
                        pltpu.VMEM((1, Q_W), F32), pltpu.VMEM((1, KV_W), F32)],
        name=name, compiler_params=_params("arbitrary"),
    )(dmix, proj, proj, proj, proj, proj, proj, proj, q_w, k_w, sinks, sel_q, sel_q.T, sel_k, sel_k.T, fold_q, fold_k)


HALO = 8
GROUP_W = SSM_INNER // SSM_GROUPS
HEADS_PER_GROUP = SSM_HEADS // SSM_GROUPS


def _head_expand():
    h = jnp.arange(LANE)[:, None]
    c = jnp.arange(SSM_INNER)[None, :]
    return (c // SSM_HEAD_DIM == h).astype(F32)


def _softplus(x):
    return jnp.maximum(x, 0.0) + jnp.log1p(jnp.exp(-jnp.abs(x)))


def _ssd_decays(dt, a_log_row):
    row = lax.broadcasted_iota(jnp.int32, (BLOCK, BLOCK), 0)
    col = lax.broadcasted_iota(jnp.int32, (BLOCK, BLOCK), 1)
    lower = row >= col
    a = -jnp.exp(a_log_row)
    a_cs = _dot_hi(lower, dt * a, exact="b")
    return a, a_cs, lower


def _decay_matrix(a_cs, a_cs_t, h, lower):
    diff = a_cs[:, h:h + 1] - a_cs_t[h:h + 1, :]
    return jnp.where(lower, jnp.exp(jnp.where(lower, diff, 0.0)), 0.0)


def _conv_taps(s_ref, w_ref, first, rows):
    acc = w_ref[0:1, :] * s_ref[pl.ds(first, rows), :]
    for j in range(1, SSM_CONV):
        acc = acc + w_ref[j:j + 1, :] * s_ref[pl.ds(first + j, rows), :]
    return acc


def ssd_fwd(proj, cw_x, cw_bc, cb_x, cb_bc, dt_bias, a_log, d_exp, norm_w, name="ssd_fwd"):
    m = proj.shape[0]
    nb = m // BLOCK
    expand = _head_expand()
    expand_t = expand.T

    def body(z_ref, xs_ref, bc_ref, dtr_ref, cwx_ref, cwbc_ref, cbx_ref, cbbc_ref, dtb_ref, alog_ref, dexp_ref,
             nw_ref, e_ref, et_ref, out_ref, prex_ref, prebc_ref, dt_ref, ypre_ref, st_ref, sx, sbc, state):
        c = pl.program_id(0)

        @pl.when(c == 0)
        def _():
            sx[0:HALO, :] = jnp.zeros((HALO, SSM_INNER), F32)
            sbc[0:HALO, :] = jnp.zeros((HALO, 2 * LANE), F32)
            state[...] = jnp.zeros_like(state)

        sx[HALO:HALO + BLOCK, :] = xs_ref[...]
        sbc[HALO:HALO + BLOCK, :] = bc_ref[...]
        first = HALO - (SSM_CONV - 1)
        pre_x = _conv_taps(sx, cwx_ref, first, BLOCK) + cbx_ref[...]
        pre_bc = _conv_taps(sbc, cwbc_ref, first, BLOCK) + cbbc_ref[...]
        sx[0:HALO, :] = xs_ref[BLOCK - HALO:BLOCK, :]
        sbc[0:HALO, :] = bc_ref[BLOCK - HALO:BLOCK, :]
        prex_ref[...] = pre_x
        prebc_ref[...] = pre_bc
        xc = pre_x * _sigmoid(pre_x)
        bcv = pre_bc * _sigmoid(pre_bc)

        rows = _row_ids(c * BLOCK, BLOCK, LANE)
        lanes = lax.broadcasted_iota(jnp.int32, (BLOCK, LANE), 1)
        live = jnp.logical_and(rows >= FRONT_PAD, lanes < SSM_HEADS)
        dt = jnp.where(live, _softplus(dtr_ref[...] + dtb_ref[...]), 0.0)
        dt_ref[...] = dt
        a, a_cs, lower = _ssd_decays(dt, alog_ref[...])
        a_cs_t = a_cs.T
        dt_t = dt.T
        e = e_ref[...]
        es_full = _dot_hi(jnp.exp(a_cs), e)
        wx_full = _dot_hi(jnp.exp(a_cs[BLOCK - 1:BLOCK, :] - a_cs) * dt, e)
        end_col = jnp.exp(a_cs_t[:, BLOCK - 1:BLOCK])
        dec_full = _dot_hi(et_ref[...], jnp.broadcast_to(end_col, (LANE, SSM_STATE)), exact="b")

        st_ref[0] = state[...]
        ys = []
        for g in range(SSM_GROUPS):
            b_g = bcv[:, SSM_STATE * g:SSM_STATE * (g + 1)]
            c_g = bcv[:, LANE + SSM_STATE * g:LANE + SSM_STATE * (g + 1)]
            gs = slice(GROUP_W * g, GROUP_W * (g + 1))
            cb = _dot_nt(c_g, b_g)
            yd = []
            for hh in range(HEADS_PER_GROUP):
                h = g * HEADS_PER_GROUP + hh
                w = cb * _decay_matrix(a_cs, a_cs_t, h, lower) * dt_t[h:h + 1, :]
                yd.append(_dot(w, xc[:, SSM_HEAD_DIM * h:SSM_HEAD_DIM * (h + 1)]))
            h_g = state[gs, :]
            y_off = _dot_nt(c_g, h_g) * es_full[:, gs]
            ys.append(jnp.concatenate(yd, axis=1) + y_off)
            new_state = _dot_tn(xc[:, gs] * wx_full[:, gs], b_g)
            state[gs, :] = h_g * dec_full[gs, :] + new_state
        y_pre = jnp.concatenate(ys, axis=1) + xc * dexp_ref[...]
        ypre_ref[...] = y_pre
        z = z_ref[...]
        gt = y_pre * (z * _sigmoid(z))
        outs = []
        for g in range(SSM_GROUPS):
            gg = gt[:, GROUP_W * g:GROUP_W * (g + 1)]
            r = lax.rsqrt(jnp.mean(gg * gg, -1, keepdims=True) + EPS)
            outs.append(gg * r)
        out_ref[...] = (jnp.concatenate(outs, axis=1) * nw_ref[...]).astype(out_ref.dtype)

    full = lambda s: pl.BlockSpec(s, lambda i: (0,) * len(s))
    rowblk = lambda w, cidx: pl.BlockSpec((BLOCK, w), lambda i: (i, cidx))
    return pl.pallas_call(
        body, grid=(nb,),
        in_specs=[rowblk(SSM_INNER, COL_Z // SSM_INNER), rowblk(SSM_INNER, COL_XS // SSM_INNER),
                  rowblk(2 * LANE, COL_BC // (2 * LANE)), rowblk(LANE, COL_DT // LANE),
                  full((SSM_CONV, SSM_INNER)), full((SSM_CONV, 2 * LANE)), full((1, SSM_INNER)), full((1, 2 * LANE)),
                  full((1, LANE)), full((1, LANE)), full((1, SSM_INNER)), full((1, SSM_INNER)),
                  full((LANE, SSM_INNER)), full((SSM_INNER, LANE))],
        out_specs=[rowblk(SSM_INNER, 0), rowblk(SSM_INNER, 0), rowblk(2 * LANE, 0), rowblk(LANE, 0),
                   rowblk(SSM_INNER, 0), pl.BlockSpec((1, SSM_INNER, SSM_STATE), lambda i: (i, 0, 0))],
        out_shape=[jax.ShapeDtypeStruct((m, SSM_INNER), MXU_DTYPE), jax.ShapeDtypeStruct((m, SSM_INNER), F32),
                   jax.ShapeDtypeStruct((m, 2 * LANE), F32), jax.ShapeDtypeStruct((m, LANE), F32),
                   jax.ShapeDtypeStruct((m, SSM_INNER), F32), jax.ShapeDtypeStruct((nb, SSM_INNER, SSM_STATE), F32)],
        scratch_shapes=[pltpu.VMEM((HALO + BLOCK, SSM_INNER), F32), pltpu.VMEM((HALO + BLOCK, 2 * LANE), F32),
                        pltpu.VMEM((SSM_INNER, SSM_STATE), F32)],
        name=name, compiler_params=_params("arbitrary"),
    )(proj, proj, proj, proj, cw_x, cw_bc, cb_x, cb_bc, dt_bias, a_log, d_exp, norm_w, expand, expand_t)


def ssd_bwd(proj, dmix, pre_x, pre_bc, dt, y_pre, states, cw_x, cw_bc, dt_bias, a_log, d_exp, norm_w,
            name="ssd_bwd"):
    m = proj.shape[0]
    nb = m // BLOCK
    expand = _head_expand()
    expand_t = expand.T

    def body(do0_ref, do1_ref, z_ref, xs_ref, xsp_ref, bc_ref, bcp_ref, dtr_ref, prex_ref, prebc_ref, dt_ref, ypre_ref,
             st_ref,
             cwx_ref, cwbc_ref, dtb_ref, alog_ref, dexp_ref, nw_ref, e_ref, et_ref,
             dz_ref, dxs_ref, dbc_ref, ddt_ref, dcwx_ref, dcwbc_ref, dcbx_ref, dcbbc_ref, ddtb_ref, dalog_ref,
             dd_ref, dnw_ref,
             dstate, hnext, tx, tbc, sx, sbc, dlane):
        i = pl.program_id(0)
        c = nb - 1 - i

        @pl.when(i == 0)
        def _():
            dstate[...] = jnp.zeros_like(dstate)
            hnext[...] = jnp.zeros_like(hnext)
            tx[BLOCK:BLOCK + HALO, :] = jnp.zeros((HALO, SSM_INNER), F32)
            tbc[BLOCK:BLOCK + HALO, :] = jnp.zeros((HALO, 2 * LANE), F32)
            dlane[...] = jnp.zeros_like(dlane)
            for r in (dcwx_ref, dcwbc_ref, dcbx_ref, dcbbc_ref, ddtb_ref, dalog_ref, dd_ref, dnw_ref):
                r[...] = jnp.zeros_like(r)

        e = e_ref[...]
        et = et_ref[...]
        pre_x = prex_ref[...]
        pre_bc = prebc_ref[...]
        sig_x = _sigmoid(pre_x)
        sig_bc = _sigmoid(pre_bc)
        xc = pre_x * sig_x
        bcv = pre_bc * sig_bc
        dt = dt_ref[...]
        a, a_cs, lower = _ssd_decays(dt, alog_ref[...])
        a_cs_t = a_cs.T
        es_full = _dot_hi(jnp.exp(a_cs), e)
        ed_full = _dot_hi(jnp.exp(a_cs[BLOCK - 1:BLOCK, :] - a_cs), e)
        dt_full = _dot_hi(dt, e)
        end_col = jnp.exp(a_cs_t[:, BLOCK - 1:BLOCK])
        dec_full = _dot_hi(et, jnp.broadcast_to(end_col, (LANE, SSM_STATE)), exact="b")
        dexp = dexp_ref[...]

        z = z_ref[...]
        zs = _sigmoid(z)
        sz = z * zs
        y_pre = ypre_ref[...]
        gt = y_pre * sz
        do = jnp.concatenate([do0_ref[...], do1_ref[...]], axis=1)
        nw = nw_ref[...]
        dgt = []
        dnw = []
        for g in range(SSM_GROUPS):
            gs = slice(GROUP_W * g, GROUP_W * (g + 1))
            gg = gt[:, gs]
            r = lax.rsqrt(jnp.mean(gg * gg, -1, keepdims=True) + EPS)
            gn = do[:, gs] * nw[:, gs]
            dgt.append(r * gn - gg * ((r * r * r) * jnp.mean(gg * gn, -1, keepdims=True)))
            dnw.append(jnp.sum(do[:, gs] * (gg * r), axis=0, keepdims=True))
        dgt = jnp.concatenate(dgt, axis=1)
        dnw_ref[...] += jnp.concatenate(dnw, axis=1)
        dy = dgt * sz
        dz_ref[...] = dgt * y_pre * (zs * (1.0 + z * (1.0 - zs)))
        dlane[...] += jnp.sum(dy * xc, axis=0, keepdims=True)
        xd = xc * dt_full

        lane_id = lax.broadcasted_iota(jnp.int32, (BLOCK, LANE), 1)
        sub_id = lax.broadcasted_iota(jnp.int32, (LANE, BLOCK), 0)
        ds_to = jnp.zeros((BLOCK, LANE), F32)
        ds_from_t = jnp.zeros((LANE, BLOCK), F32)
        dxd_parts, inter_parts = [], []
        dbs, dcs = [], []
        for g in range(SSM_GROUPS):
            gs = slice(GROUP_W * g, GROUP_W * (g + 1))
            b_g = bcv[:, SSM_STATE * g:SSM_STATE * (g + 1)]
            c_g = bcv[:, LANE + SSM_STATE * g:LANE + SSM_STATE * (g + 1)]
            cb = _dot_nt(c_g, b_g)
            dcb = jnp.zeros((BLOCK, BLOCK), F32)
            dxd_h = []
            for hh in range(HEADS_PER_GROUP):
                h = g * HEADS_PER_GROUP + hh
                hs = slice(SSM_HEAD_DIM * h, SSM_HEAD_DIM * (h + 1))
                lm = _decay_matrix(a_cs, a_cs_t, h, lower)
                dy_h = dy[:, hs]
                gl = _dot_nt(dy_h, xd[:, hs]) * lm
                dcb = dcb + gl
                e_h = gl * cb
                ds_to = ds_to + jnp.where(lane_id == h, jnp.sum(e_h, axis=-1, keepdims=True), 0.0)
                ds_from_t = ds_from_t + jnp.where(sub_id == h, jnp.sum(e_h, axis=0, keepdims=True), 0.0)
                dxd_h.append(_dot_tn(cb * lm, dy_h))
            h_g = st_ref[0, gs, :]
            dh_g = dstate[gs, :]
            dys_g = dy[:, gs] * es_full[:, gs]
            xde_g = xd[:, gs] * ed_full[:, gs]
            dcs.append(_dot(dcb, b_g) + _dot(dys_g, h_g))
            dbs.append(_dot_tn(dcb, c_g) + _dot(xde_g, dh_g))
            y_off = _dot_nt(c_g, h_g) * es_full[:, gs]
            dxd_state = _dot_nt(b_g, dh_g) * ed_full[:, gs]
            inter_parts.append(dy[:, gs] * y_off - xd[:, gs] * dxd_state)
            dxd_parts.append(jnp.concatenate(dxd_h, axis=1) + dxd_state)
            dstate[gs, :] = dh_g * dec_full[gs, :] + _dot_tn(dys_g, c_g)
            if g == 0:
                end_dot = hnext[gs, :] * dh_g
            else:
                end_dot = jnp.concatenate([end_dot, hnext[gs, :] * dh_g], axis=0)
        dxd = jnp.concatenate(dxd_parts, axis=1)
        hnext[...] = st_ref[0]

        ds = ds_to - ds_from_t.T + _dot_hi(jnp.concatenate(inter_parts, axis=1), et)
        ds_end = jnp.sum(_dot_tn_hi(end_dot, et), axis=0, keepdims=True)
        rows_l = lax.broadcasted_iota(jnp.int32, (BLOCK, LANE), 0)
        ds = ds + jnp.where(rows_l == BLOCK - 1, ds_end, 0.0)
        row = lax.broadcasted_iota(jnp.int32, (BLOCK, BLOCK), 0)
        col = lax.broadcasted_iota(jnp.int32, (BLOCK, BLOCK), 1)
        dadt = _dot_hi(col >= row, ds, exact="b")
        ddt = dadt * a + _dot_hi(dxd * xc, et)
        dalog_ref[...] += jnp.sum(dadt * dt, axis=0, keepdims=True) * a
        rows = _row_ids(c * BLOCK, BLOCK, LANE)
        lanes = lax.broadcasted_iota(jnp.int32, (BLOCK, LANE), 1)
        live = jnp.logical_and(rows >= FRONT_PAD, lanes < SSM_HEADS)
        ddt_raw = jnp.where(live, ddt * _sigmoid(dtr_ref[...] + dtb_ref[...]), 0.0)
        ddt_ref[...] = ddt_raw
        ddtb_ref[...] += jnp.sum(ddt_raw, axis=0, keepdims=True)

        dxc = dxd * dt_full + dy * dexp
        dpre_x = dxc * (sig_x * (1.0 + pre_x * (1.0 - sig_x)))
        dpre_bc = jnp.concatenate(dbs + dcs, axis=1) * (sig_bc * (1.0 + pre_bc * (1.0 - sig_bc)))
        dcbx_ref[...] += jnp.sum(dpre_x, axis=0, keepdims=True)
        dcbbc_ref[...] += jnp.sum(dpre_bc, axis=0, keepdims=True)
        keep_x = _row_ids(c * BLOCK, BLOCK, SSM_INNER) >= FRONT_PAD
        keep_bc = _row_ids(c * BLOCK, BLOCK, 2 * LANE) >= FRONT_PAD
        prev_live = (c > 0).astype(F32)
        for (dpre, t_ref, s_ref, cur_ref, prv_ref, w_ref, dw_ref, dx_ref, keep) in (
                (dpre_x, tx, sx, xs_ref, xsp_ref, cwx_ref, dcwx_ref, dxs_ref, keep_x),
                (dpre_bc, tbc, sbc, bc_ref, bcp_ref, cwbc_ref, dcwbc_ref, dbc_ref, keep_bc)):
            t_ref[0:BLOCK, :] = dpre
            acc = w_ref[0:1, :] * t_ref[pl.ds(SSM_CONV - 1, BLOCK), :]
            for j in range(1, SSM_CONV):
                acc = acc + w_ref[j:j + 1, :] * t_ref[pl.ds(SSM_CONV - 1 - j, BLOCK), :]
            dx_ref[...] = jnp.where(keep, acc, 0.0)
            t_ref[BLOCK:BLOCK + HALO, :] = dpre[0:HALO, :]
            s_ref[0:HALO, :] = prv_ref[BLOCK - HALO:BLOCK, :] * prev_live
            s_ref[HALO:HALO + BLOCK, :] = cur_ref[...]
            first = HALO - (SSM_CONV - 1)
            for j in range(SSM_CONV):
                dw_ref[j:j + 1, :] += jnp.sum(dpre * s_ref[pl.ds(first + j, BLOCK), :], axis=0, keepdims=True)

        @pl.when(i == nb - 1)
        def _():
            dd_ref[...] = _dot_hi(jnp.broadcast_to(dlane[...], (HALO, SSM_INNER)), et)[0:1, :]

    blk = lambda i: nb - 1 - i
    prv = lambda i: jnp.maximum(nb - 2 - i, 0)
    full = lambda s: pl.BlockSpec(s, lambda i: (0,) * len(s))
    rowblk = lambda w, cidx: pl.BlockSpec((BLOCK, w), lambda i: (blk(i), cidx))
    prvblk = lambda w, cidx: pl.BlockSpec((BLOCK, w), lambda i: (prv(i), cidx))
    return pl.pallas_call(
        body, grid=(nb,),
        in_specs=[rowblk(GROUP_W, Q_W // GROUP_W), rowblk(GROUP_W, Q_W // GROUP_W + 1),
                  rowblk(SSM_INNER, COL_Z // SSM_INNER),
                  rowblk(SSM_INNER, COL_XS // SSM_INNER), prvblk(SSM_INNER, COL_XS // SSM_INNER),
                  rowblk(2 * LANE, COL_BC // (2 * LANE)), prvblk(2 * LANE, COL_BC // (2 * LANE)),
                  rowblk(LANE, COL_DT // LANE),
                  rowblk(SSM_INNER, 0), rowblk(2 * LANE, 0), rowblk(LANE, 0), rowblk(SSM_INNER, 0),
                  pl.BlockSpec((1, SSM_INNER, SSM_STATE), lambda i: (blk(i), 0, 0)),
                  full((SSM_CONV, SSM_INNER)), full((SSM_CONV, 2 * LANE)), full((1, LANE)), full((1, LANE)),
                  full((1, SSM_INNER)), full((1, SSM_INNER)), full((LANE, SSM_INNER)), full((SSM_INNER, LANE))],
        out_specs=[rowblk(SSM_INNER, 0), rowblk(SSM_INNER, 0), rowblk(2 * LANE, 0), rowblk(LANE, 0),
                   full((SSM_CONV, SSM_INNER)), full((SSM_CONV, 2 * LANE)), full((1, SSM_INNER)), full((1, 2 * LANE)),
                   full((1, LANE)), full((1, LANE)), full((1, LANE)), full((1, SSM_INNER))],
        out_shape=[jax.ShapeDtypeStruct((m, SSM_INNER), F32), jax.ShapeDtypeStruct((m, SSM_INNER), F32),
                   jax.ShapeDtypeStruct((m, 2 * LANE), F32), jax.ShapeDtypeStruct((m, LANE), F32),
                   jax.ShapeDtypeStruct((SSM_CONV, SSM_INNER), F32), jax.ShapeDtypeStruct((SSM_CONV, 2 * LANE), F32),
                   jax.ShapeDtypeStruct((1, SSM_INNER), F32), jax.ShapeDtypeStruct((1, 2 * LANE), F32),
                   jax.ShapeDtypeStruct((1, LANE), F32), jax.ShapeDtypeStruct((1, LANE), F32),
                   jax.ShapeDtypeStruct((1, LANE), F32), jax.ShapeDtypeStruct((1, SSM_INNER), F32)],
        scratch_shapes=[pltpu.VMEM((SSM_INNER, SSM_STATE), F32), pltpu.VMEM((SSM_INNER, SSM_STATE), F32),
                        pltpu.VMEM((BLOCK + HALO, SSM_INNER), F32), pltpu.VMEM((BLOCK + HALO, 2 * LANE), F32),
                        pltpu.VMEM((HALO + BLOCK, SSM_INNER), F32), pltpu.VMEM((HALO + BLOCK, 2 * LANE), F32),
                        pltpu.VMEM((1, SSM_INNER), F32)],
        name=name, compiler_params=_params("arbitrary"),
    )(dmix, dmix, proj, proj, proj, proj, proj, proj, pre_x, pre_bc, dt, y_pre, states,
      cw_x, cw_bc, dt_bias, a_log, d_exp, norm_w, expand, expand_t)


CONF_HALO = 32
SUBLANES = 8


def _for_each_window(s, offsets, rows, fn):
    total = s.shape[0]
    assert max(offsets) + rows <= total
    for b in range(SUBLANES):
        offs = [o for o in offsets if o % SUBLANES == b]
        if not offs:
            continue
        rot = s if b == 0 else pltpu.roll(s, total - b, 0)
        for o in offs:
            fn(o, rot[o - b:o - b + rows])


def _glu_masked(v, first_row):
    a = v[:, :D_MODEL]
    s = _sigmoid(v[:, D_MODEL:])
    rows = _row_ids(first_row, v.shape[0], D_MODEL)
    return jnp.where(rows >= FRONT_PAD, a * s, 0.0), a, s


def _layer_norm_stats(c):
    mu = jnp.mean(c, -1, keepdims=True)
    xc = c - mu
    rstd = lax.rsqrt(jnp.mean(xc * xc, -1, keepdims=True) + LN_EPS)
    return xc * rstd, rstd


def conformer_mid_fwd(v, dw_w, dw_b, ln_g, ln_b, name="conf_mid_fwd"):
    m = v.shape[0]
    nb = m // BLOCK
    kpad = dw_w.shape[0]

    def body(vc_ref, vp_ref, w_ref, b_ref, g_ref, beta_ref, c_ref, s_ref):
        i = pl.program_id(0)
        g_prev, _, _ = _glu_masked(vp_ref[BLOCK - CONF_HALO:BLOCK, :], (i - 1) * BLOCK + BLOCK - CONF_HALO)
        g_cur, _, _ = _glu_masked(vc_ref[...], i * BLOCK)
        sg = jnp.concatenate([g_prev * (i > 0).astype(F32), g_cur], axis=0)
        first = CONF_HALO - (CONF_KERNEL - 1)
        acc = [jnp.broadcast_to(b_ref[...], (BLOCK, D_MODEL))]

        def tap(off, win):
            j = off - first
            acc[0] = acc[0] + w_ref[j:j + 1, :] * win

        _for_each_window(sg, [first + j for j in range(CONF_KERNEL)], BLOCK, tap)
        acc = acc[0]
        c_ref[...] = acc
        xhat, _ = _layer_norm_stats(acc)
        nrm = xhat * g_ref[...] + beta_ref[...]
        s_ref[...] = (nrm * _sigmoid(nrm)).astype(s_ref.dtype)

    full = lambda s: pl.BlockSpec(s, lambda i: (0,) * len(s))
    return pl.pallas_call(
        body, grid=(nb,),
        in_specs=[pl.BlockSpec((BLOCK, 2 * D_MODEL), lambda i: (i, 0)),
                  pl.BlockSpec((BLOCK, 2 * D_MODEL), lambda i: (jnp.maximum(i - 1, 0), 0)),
                  full((kpad, D_MODEL)), full((1, D_MODEL)), full((1, D_MODEL)), full((1, D_MODEL))],
        out_specs=[pl.BlockSpec((BLOCK, D_MODEL), lambda i: (i, 0)), pl.BlockSpec((BLOCK, D_MODEL), lambda i: (i, 0))],
        out_shape=[jax.ShapeDtypeStruct((m, D_MODEL), F32), jax.ShapeDtypeStruct((m, D_MODEL), MXU_DTYPE)],
        name=name, compiler_params=_params("arbitrary"),
    )(v, v, dw_w, dw_b, ln_g, ln_b)


def conformer_ln_bwd(ds, c, ln_g, ln_b, name="conf_ln_bwd"):
    m, d = c.shape
    tm = ROW_TILE

    def body(ds_ref, c_ref, g_ref, beta_ref, dc_ref, dg_ref, db_ref):
        @pl.when(pl.program_id(0) == 0)
        def _():
            dg_ref[...] = jnp.zeros_like(dg_ref)
            db_ref[...] = jnp.zeros_like(db_ref)

        xhat, rstd = _layer_norm_stats(c_ref[...])
        g = g_ref[...]
        nrm = xhat * g + beta_ref[...]
        sg = _sigmoid(nrm)
        dn = ds_ref[...] * (sg * (1.0 + nrm * (1.0 - sg)))
        db_ref[...] += jnp.sum(dn, axis=0, keepdims=True)
        dg_ref[...] += jnp.sum(dn * xhat, axis=0, keepdims=True)
        dx = dn * g
        dc_ref[...] = rstd * (dx - jnp.mean(dx, -1, keepdims=True) - xhat * jnp.mean(dx * xhat, -1, keepdims=True))

    row = pl.BlockSpec((tm, d), lambda i: (i, 0))
    vec = pl.BlockSpec((1, d), lambda i: (0, 0))
    return pl.pallas_call(
        body, grid=(m // tm,), in_specs=[row, row, vec, vec], out_specs=[row, vec, vec],
        out_shape=[jax.ShapeDtypeStruct((m, d), F32), jax.ShapeDtypeStruct((1, d), F32), jax.ShapeDtypeStruct((1, d), F32)],
        name=name, compiler_params=_params("arbitrary"),
    )(ds, c, ln_g, ln_b)


def conformer_conv_bwd(dc, v, dw_w, name="conf_conv_bwd"):
    m = v.shape[0]
    nb = m // BLOCK
    kpad = dw_w.shape[0]

    def body(dcc_ref, dcn_ref, vc_ref, vp_ref, w_ref, dv_ref, dw_ref, db_ref, dvb_ref):
        i = pl.program_id(0)

        @pl.when(i == 0)
        def _():
            dw_ref[...] = jnp.zeros_like(dw_ref)
            db_ref[...] = jnp.zeros_like(db_ref)
            dvb_ref[...] = jnp.zeros_like(dvb_ref)

        dc_cur = dcc_ref[...]
        tg = jnp.concatenate([dc_cur, dcn_ref[0:CONF_HALO, :] * (i < nb - 1).astype(F32)], axis=0)
        g_prev, _, _ = _glu_masked(vp_ref[BLOCK - CONF_HALO:BLOCK, :], (i - 1) * BLOCK + BLOCK - CONF_HALO)
        g_cur, a, s = _glu_masked(vc_ref[...], i * BLOCK)
        sg = jnp.concatenate([g_prev * (i > 0).astype(F32), g_cur], axis=0)
        db_ref[...] += jnp.sum(dc_cur, axis=0, keepdims=True)
        first = CONF_HALO - (CONF_KERNEL - 1)
        dg_acc = [jnp.zeros((BLOCK, D_MODEL), F32)]

        def tap_dg(off, win):
            j = CONF_KERNEL - 1 - off
            dg_acc[0] = dg_acc[0] + w_ref[j:j + 1, :] * win

        def tap_dw(off, win):
            j = off - first
            dw_ref[j:j + 1, :] += jnp.sum(dc_cur * win, axis=0, keepdims=True)

        _for_each_window(tg, list(range(CONF_KERNEL)), BLOCK, tap_dg)
        _for_each_window(sg, [first + j for j in range(CONF_KERNEL)], BLOCK, tap_dw)
        dg = dg_acc[0]
        rows = _row_ids(i * BLOCK, BLOCK, D_MODEL)
        dg = jnp.where(rows >= FRONT_PAD, dg, 0.0)
        da = dg * s
        dbv = dg * a * (s * (1.0 - s))
        dv = jnp.concatenate([da, dbv], axis=1)
        dv_ref[...] = dv.astype(dv_ref.dtype)
        dvb_ref[...] += jnp.sum(dv, axis=0, keepdims=True)

    full = lambda s: pl.BlockSpec(s, lambda i: (0,) * len(s))
    return pl.pallas_call(
        body, grid=(nb,),
        in_specs=[pl.BlockSpec((BLOCK, D_MODEL), lambda i: (i, 0)),
                  pl.BlockSpec((BLOCK, D_MODEL), lambda i: (jnp.minimum(i + 1, nb - 1), 0)),
                  pl.BlockSpec((BLOCK, 2 * D_MODEL), lambda i: (i, 0)),
                  pl.BlockSpec((BLOCK, 2 * D_MODEL), lambda i: (jnp.maximum(i - 1, 0), 0)),
                  full((kpad, D_MODEL))],
        out_specs=[pl.BlockSpec((BLOCK, 2 * D_MODEL), lambda i: (i, 0)), full((kpad, D_MODEL)),
                   full((1, D_MODEL)), full((1, 2 * D_MODEL))],
        out_shape=[jax.ShapeDtypeStruct((m, 2 * D_MODEL), MXU_DTYPE), jax.ShapeDtypeStruct((kpad, D_MODEL), F32),
                   jax.ShapeDtypeStruct((1, D_MODEL), F32), jax.ShapeDtypeStruct((1, 2 * D_MODEL), F32)],
        name=name, compiler_params=_params("arbitrary"),
    )(dc, dc, v, v, dw_w)


def _row(v, width=None):
    v = v.reshape(1, -1).astype(F32)
    if width is not None and v.shape[1] < width:
        v = jnp.pad(v, ((0, 0), (0, width - v.shape[1])))
    return v


def _w_in_to_kernel(w):
    pad = jnp.zeros((w.shape[0], PROJ_W - COL_DT - SSM_HEADS), w.dtype)
    return jnp.concatenate([w[:, 768:1792], w[:, 1792:2816], w[:, 0:512], w[:, 2816:3072], w[:, 512:640],
                            w[:, 640:768], w[:, 3072:3088], pad], axis=1)


def _w_in_from_kernel(g):
    return jnp.concatenate([g[:, COL_Q:COL_Q + Q_W], g[:, COL_K:COL_K + KV_W], g[:, COL_V:COL_V + KV_W],
                            g[:, COL_Z:COL_Z + SSM_INNER], g[:, COL_XS:COL_XS + SSM_INNER],
                            g[:, COL_BC:COL_BC + 2 * LANE], g[:, COL_DT:COL_DT + SSM_HEADS]], axis=1)


def even_fwd(h, p):
    u = rms_fwd(h, p["norm"])
    proj = matmul(u, p["w_in"], name="mm_proj")
    att = attention_fwd(proj, p["q_norm"], p["k_norm"], p["sinks"])
    ssm, pre_x, pre_bc, dt, y_pre, states = ssd_fwd(proj, p["cw_x"], p["cw_bc"], p["cb_x"], p["cb_bc"], p["dt_bias"],
                                                    p["a_log"], p["d_exp"], p["ssm_norm"])
    mix = jnp.concatenate([att, ssm], axis=1)
    out = matmul(mix, p["w_out"], b_kind="rowshard", layer=p["layer"], epilogue="resid", extra=h, name="mm_mix_out")
    return out, (h, u, proj, mix, pre_x, pre_bc, dt, y_pre, states)


def even_bwd(dh, p, saved):
    h, u, proj, mix, pre_x, pre_bc, dt, y_pre, states = saved
    dmix = matmul(dh, p["w_out"], b_kind="rowshard", layer=p["layer"], trans_b=True, name="mm_dmix")
    dw_out = matmul_tn(mix, dh, ti=512, tn=D_MODEL, out_dtype=GRAD_WIRE_DTYPE, name="mm_dw_out")
    dw_out = dw_out.reshape(N_DEV, MIX_W // N_DEV, D_MODEL)
    dq, dk, dv, dqw, dkw, dsk = attention_bwd(proj, dmix, p["q_norm"], p["k_norm"], p["sinks"])
    (dz, dxs, dbc, ddt, dcwx, dcwbc, dcbx, dcbbc, ddtb, dalog, dd, dnw) = ssd_bwd(
        proj, dmix, pre_x, pre_bc, dt, y_pre, states, p["cw_x"], p["cw_bc"], p["dt_bias"], p["a_log"], p["d_exp"],
        p["ssm_norm"])
    dproj = jnp.concatenate([dz, dxs, dq, dbc, dk, dv, ddt], axis=1).astype(MXU_DTYPE)
    du = matmul(dproj, p["w_in"], trans_b=True, name="mm_du_in")
    dw_in = matmul_tn(u, dproj, ti=512, tn=PROJ_W, name="mm_dw_in")
    dw_in = _to_shards(_w_in_from_kernel(dw_in), 1).astype(GRAD_WIRE_DTYPE)
    dh_in, dg = rms_bwd(h, p["norm"], du, dh)
    grads = dict(norm=dg, w_in=dw_in, w_out=dw_out, cw_x=dcwx, cw_bc=dcwbc, cb_x=dcbx, cb_bc=dcbbc, dt_bias=ddtb,
                 a_log=dalog, d_skip=dd, ssm_norm=dnw, q_norm=dqw, k_norm=dkw, sinks=dsk)
    return dh_in, grads


def conf_fwd(h, p):
    u = rms_fwd(h, p["norm"])
    v = mlp_up(u, p["pw1_w"], p["layer"], bias=p["pw1_b"], relu2=False, out_dtype=F32, name="mm_pw1")
    c, s = conformer_mid_fwd(v, p["dw_w"], p["dw_b"], p["ln_g"], p["ln_b"])
    out = matmul(s, p["pw2_w"], b_kind="rowshard", layer=p["layer"], bias=p["pw2_b"], epilogue="resid", extra=h,
                 name="mm_pw2")
    return out, (h, u, v, c, s)


def conf_bwd(dh, p, saved):
    h, u, v, c, s = saved
    dpw2_b = col_sum(dh)
    ds = matmul(dh, p["pw2_w"], b_kind="rowshard", layer=p["layer"], trans_b=True, name="mm_ds")
    dpw2_w = matmul_tn(s, dh, ti=D_MODEL, tn=D_MODEL, out_dtype=GRAD_WIRE_DTYPE, name="mm_dpw2")
    dpw2_w = dpw2_w.reshape(N_DEV, D_MODEL // N_DEV, D_MODEL)
    dc, dln_g, dln_b = conformer_ln_bwd(ds, c, p["ln_g"], p["ln_b"])
    dv, ddw_w, ddw_b, dpw1_b = conformer_conv_bwd(dc, v, p["dw_w"])
    dpw1_w = mlp_dw_up(u, dv, name="mm_dpw1")
    dh_in, dg = mlp_du_rms_bwd(dv, p["pw1_w"], p["layer"], h, p["norm"], dh, name="mm_du_pw1")
    grads = dict(norm=dg, pw1_w=dpw1_w, pw1_b=dpw1_b, dw_w=ddw_w, dw_b=ddw_b, ln_g=dln_g, ln_b=dln_b, pw2_w=dpw2_w,
                 pw2_b=dpw2_b)
    return dh_in, grads


def mlp_fwd(h, p):
    u = rms_fwd(h, p["norm"])
    act = mlp_up(u, p["w_up"], p["layer"])
    out = matmul(act, p["w_down"], b_kind="rowshard", layer=p["layer"], epilogue="resid", extra=h, name="mm_down")
    return out, (h, u, act)


def mlp_bwd(dh, p, saved):
    h, u, act = saved
    da = mlp_dact(dh, p["w_down"], act, p["layer"])
    dw_down = mlp_dw_down(act, dh).reshape(N_DEV, FF_BLOCK, D_MODEL)
    dw_up = mlp_dw_up(u, da)
    dh_in, dg = mlp_du_rms_bwd(da, p["w_up"], p["layer"], h, p["norm"], dh)
    return dh_in, dict(norm=dg, w_up=dw_up, w_down=dw_down)


def local_step(x, target, w):
    n_even, n_odd = (DEPTH + 1) // 2, DEPTH // 2
    h = jnp.concatenate([jnp.zeros((FRONT_PAD, D_MODEL), F32), w["meta_tokens"].astype(F32), x], axis=0)
    even_p, odd_p, mlp_p = [], [], []
    for i in range(n_even):
        cw = w["ssm_conv_w"][i]
        even_p.append(dict(
            layer=i, norm=_row(w["mix_norm_even"][i]), w_in=_w_in_to_kernel(_from_shards(w["w_in"][:, i], 1)),
            w_out=w["w_out"],
            cw_x=cw[:, :SSM_INNER], cw_bc=cw[:, SSM_INNER:], cb_x=_row(w["ssm_conv_b"][i][:SSM_INNER]),
            cb_bc=_row(w["ssm_conv_b"][i][SSM_INNER:]), dt_bias=_row(w["dt_bias"][i], LANE),
            a_log=_row(w["a_log"][i], LANE), d_exp=_row(jnp.repeat(w["d_skip"][i], SSM_HEAD_DIM)),
            ssm_norm=_row(w["ssm_norm_w"][i]), q_norm=_row(jnp.tile(w["q_norm"][i], ATT_HEADS)),
            k_norm=_row(jnp.tile(w["k_norm"][i], ATT_KV_HEADS)), sinks=w["sinks"][i].astype(F32)))
    for i in range(n_odd):
        odd_p.append(dict(
            layer=i, norm=_row(w["mix_norm_odd"][i]), pw1_w=w["pw1_w"], pw1_b=_row(w["pw1_b"][i]),
            dw_w=jnp.pad(w["dw_w"][i], ((0, CONF_HALO - CONF_KERNEL), (0, 0))), dw_b=_row(w["dw_b"][i]),
            ln_g=_row(w["ln_g"][i]), ln_b=_row(w["ln_b"][i]), pw2_w=w["pw2_w"], pw2_b=_row(w["pw2_b"][i])))
    for layer in range(DEPTH):
        mlp_p.append(dict(layer=layer, norm=_row(w["mlp_norm"][layer]), w_up=w["w_up"], w_down=w["w_down"]))

    tape = []
    for layer in range(DEPTH):
        if layer % 2 == 0:
            h, saved = even_fwd(h, even_p[layer // 2])
        else:
            h, saved = conf_fwd(h, odd_p[layer // 2])
        tape.append(saved)
        h, saved = mlp_fwd(h, mlp_p[layer])
        tape.append(saved)
    dh, loss_row = loss_fwd_bwd(h, target)

    ge = [None] * n_even
    go = [None] * n_odd
    gm = [None] * DEPTH
    for layer in reversed(range(DEPTH)):
        dh, gm[layer] = mlp_bwd(dh, mlp_p[layer], tape.pop())
        if layer % 2 == 0:
            dh, ge[layer // 2] = even_bwd(dh, even_p[layer // 2], tape.pop())
        else:
            dh, go[layer // 2] = conf_bwd(dh, odd_p[layer // 2], tape.pop())

    stack = lambda gs, f: jnp.stack([f(g) for g in gs])
    grads = dict(
        meta_tokens=dh[FRONT_PAD:BLOCK],
        mix_norm_even=stack(ge, lambda g: g["norm"][0]),
        w_in=[g["w_in"] for g in ge],
        ssm_conv_w=stack(ge, lambda g: jnp.concatenate([g["cw_x"], g["cw_bc"]], axis=1)),
        ssm_conv_b=stack(ge, lambda g: jnp.concatenate([g["cb_x"][0], g["cb_bc"][0]])),
        dt_bias=stack(ge, lambda g: g["dt_bias"][0, :SSM_HEADS]),
        a_log=stack(ge, lambda g: g["a_log"][0, :SSM_HEADS]),
        d_skip=stack(ge, lambda g: g["d_skip"][0, :SSM_HEADS]),
        ssm_norm_w=stack(ge, lambda g: g["ssm_norm"][0]),
        q_norm=stack(ge, lambda g: g["q_norm"][0, :HEAD_DIM]),
        k_norm=stack(ge, lambda g: g["k_norm"][0, :HEAD_DIM]),
        sinks=stack(ge, lambda g: g["sinks"][0, :ATT_HEADS]),
        w_out=[g["w_out"] for g in ge],
        mix_norm_odd=stack(go, lambda g: g["norm"][0]),
        pw1_w=[g["pw1_w"] for g in go],
        pw1_b=stack(go, lambda g: g["pw1_b"][0]),
        dw_w=stack(go, lambda g: g["dw_w"][:CONF_KERNEL]),
        dw_b=stack(go, lambda g: g["dw_b"][0]),
        ln_g=stack(go, lambda g: g["ln_g"][0]),
        ln_b=stack(go, lambda g: g["ln_b"][0]),
        pw2_w=[g["pw2_w"] for g in go],
        pw2_b=stack(go, lambda g: g["pw2_b"][0]),
        mlp_norm=stack(gm, lambda g: g["norm"][0]),
        w_up=[g["w_up"] for g in gm],
        w_down=[g["w_down"] for g in gm],
    )
    return loss_row[0, 0], dh[BLOCK:], grads


PARAMS = (
    ("meta_tokens", (16, 1024), 1), ("mix_norm_even", (2, 1024), None), ("w_in", (2, 1024, 3088), 2),
    ("ssm_conv_w", (2, 4, 1280), 2), ("ssm_conv_b", (2, 1280), None), ("dt_bias", (2, 16), None),
    ("a_log", (2, 16), None), ("d_skip", (2, 16), None), ("ssm_norm_w", (2, 1024), None), ("q_norm", (2, 64), None),
    ("k_norm", (2, 64), None), ("sinks", (2, 8), None), ("w_out", (2, 1536, 1024), 1), ("mix_norm_odd", (2, 1024), 1),
    ("pw1_w", (2, 1024, 2048), 2), ("pw1_b", (2, 2048), 1), ("dw_w", (2, 31, 1024), 2), ("dw_b", (2, 1024), 1),
    ("ln_g", (2, 1024), 1), ("ln_b", (2, 1024), 1), ("pw2_w", (2, 1024, 1024), 1), ("pw2_b", (2, 1024), 1),
    ("mlp_norm", (4, 1024), None), ("w_up", (4, 1024, 4096), 2), ("w_down", (4, 4096, 1024), 1),
)
MATMUL_WEIGHTS = ("w_in", "w_out", "pw1_w", "pw2_w", "w_up", "w_down")
PACK_ROW_ALIGN = 16 * PACK_W


def _block_shape(shape, axis):
    if axis is None:
        return tuple(shape)
    return tuple(s // N_DEV if a == axis else s for a, s in enumerate(shape))


def _numel(shape):
    return math.prod(shape)


def _pack(arrays, dtype):
    flat = jnp.concatenate([a.reshape(-1).astype(dtype) for a in arrays])
    n = flat.shape[0]
    padded = -(-n // PACK_ROW_ALIGN) * PACK_ROW_ALIGN
    return jnp.pad(flat, (0, padded - n)).reshape(-1, PACK_W)


def _pack_rows(arrays_by_dev, dtype):
    flat = jnp.concatenate([a.reshape(N_DEV, -1).astype(dtype) for a in arrays_by_dev], axis=1)
    n = flat.shape[1]
    padded = -(-n // PACK_ROW_ALIGN) * PACK_ROW_ALIGN
    return jnp.pad(flat, ((0, 0), (0, padded - n))).reshape(N_DEV, -1, PACK_W)


def _to_shards(full, axis):
    shape = full.shape
    split = full.reshape(shape[:axis] + (N_DEV, shape[axis] // N_DEV) + shape[axis + 1:])
    return jnp.moveaxis(split, axis, 0)


def _from_shards(blocks, axis):
    moved = jnp.moveaxis(blocks, 0, axis)
    shape = moved.shape
    return moved.reshape(shape[:axis] + (shape[axis] * shape[axis + 1],) + shape[axis + 2:])


_MESH = pl.DeviceIdType.MESH
_ANY = pl.BlockSpec(memory_space=pl.ANY)


def _mesh_place():
    x, y, c = lax.axis_index("x"), lax.axis_index("y"), lax.axis_index("c")
    return x, y, c


def _peer(x, y, c, rel):
    dx, dy, dc = (rel >> 2) & 1, (rel >> 1) & 1, rel & 1
    return (x ^ dx if dx else x, y ^ dy if dy else y, c ^ dc if dc else c)


def _dev_index(x, y, c):
    return 4 * x + 2 * y + c


def all_gather_weights(bigs, small):
    nt = len(bigs)

    def body(*refs):
        big_refs, small_ref = refs[:nt], refs[nt]
        big_outs, small_out = refs[nt + 1:2 * nt + 1], refs[2 * nt + 1]
        send_sems, recv_sems, small_send, small_recv, local_sems = refs[2 * nt + 2:]
        x, y, c = _mesh_place()
        me = (x, y, c)
        sibling = (x, y, 1 - c)
        chips = [(1 - x, y), (x, 1 - y), (1 - x, 1 - y)]

        def big_copy(t, k, block, to, from_input=False):
            dst = big_outs[t].at[_dev_index(*block)]
            return pltpu.make_async_remote_copy(src_ref=big_refs[t] if from_input else dst, dst_ref=dst,
                                                send_sem=send_sems.at[t, k], recv_sem=recv_sems.at[t, k],
                                                device_id=to, device_id_type=_MESH)

        def small_copy(rel, block, to):
            return pltpu.make_async_remote_copy(src_ref=small_ref, dst_ref=small_out.at[_dev_index(*block)],
                                                send_sem=small_send.at[rel - 1], recv_sem=small_recv.at[rel - 1],
                                                device_id=to, device_id_type=_MESH)

        mine = [pltpu.make_async_copy(big_refs[t], big_outs[t].at[_dev_index(*me)], local_sems.at[t]) for t in range(nt)]
        mine.append(pltpu.make_async_copy(small_ref, small_out.at[_dev_index(*me)], local_sems.at[nt]))
        for cp in mine:
            cp.start()
        first = []
        for t in range(nt):
            first.append(big_copy(t, 0, me, sibling, from_input=True))
            first += [big_copy(t, 1 + j, me, (*chip, c), from_input=True) for j, chip in enumerate(chips)]
        for cp in first:
            cp.start()
        smalls = [small_copy(rel, me, _peer(x, y, c, rel)) for rel in range(1, N_DEV)]
        for cp in smalls:
            cp.start()
        passed = []
        for j, chip in enumerate(chips):
            for t in range(nt):
                big_copy(t, 1 + j, (*chip, c), me).wait_recv()
                fwd = big_copy(t, 4 + j, (*chip, c), sibling)
                fwd.start()
                passed.append(fwd)
        for t in range(nt):
            big_copy(t, 0, sibling, me).wait_recv()
            for j, chip in enumerate(chips):
                big_copy(t, 4 + j, (*chip, 1 - c), me).wait_recv()
        for rel in range(1, N_DEV):
            small_copy(rel, _peer(x, y, c, rel), me).wait_recv()
        for cp in first + passed + smalls:
            cp.wait_send()
        for cp in mine:
            cp.wait()

    return pl.pallas_call(
        body, in_specs=[_ANY] * (nt + 1), out_specs=[_ANY] * (nt + 1),
        out_shape=[jax.ShapeDtypeStruct((N_DEV,) + b.shape, b.dtype) for b in bigs]
        + [jax.ShapeDtypeStruct((N_DEV,) + small.shape, small.dtype)],
        scratch_shapes=[pltpu.SemaphoreType.DMA((nt, N_DEV - 1)), pltpu.SemaphoreType.DMA((nt, N_DEV - 1)),
                        pltpu.SemaphoreType.DMA((N_DEV - 1,)), pltpu.SemaphoreType.DMA((N_DEV - 1,)),
                        pltpu.SemaphoreType.DMA((nt + 1,))],
        name="all_gather_weights",
    )(*bigs, small)


def exchange_gradients(groups):
    flat = [(gi, li, a) for gi, group in enumerate(groups) for li, a in enumerate(group)]
    n_in = len(flat)
    n_out = len(groups)

    def body(*refs):
        in_refs = refs[:n_in]
        out_refs = refs[n_in:n_in + n_out]
        send_sems, recv_sems, local_sems = refs[n_in + n_out:]
        x, y, c = _mesh_place()
        me = _dev_index(x, y, c)

        def copy(rel, p):
            gi, li, _ = flat[p]
            peer = _peer(x, y, c, rel)
            return pltpu.make_async_remote_copy(src_ref=in_refs[p].at[_dev_index(*peer)], dst_ref=out_refs[gi].at[me, li],
                                                send_sem=send_sems.at[rel - 1, p], recv_sem=recv_sems.at[rel - 1, p],
                                                device_id=peer, device_id_type=_MESH)

        def arrival(rel, p):
            gi, li, _ = flat[p]
            peer = _peer(x, y, c, rel)
            return pltpu.make_async_remote_copy(src_ref=in_refs[p].at[me], dst_ref=out_refs[gi].at[_dev_index(*peer), li],
                                                send_sem=send_sems.at[rel - 1, p], recv_sem=recv_sems.at[rel - 1, p],
                                                device_id=peer, device_id_type=_MESH)

        mine = [pltpu.make_async_copy(in_refs[p].at[me], out_refs[flat[p][0]].at[me, flat[p][1]], local_sems.at[p])
                for p in range(n_in)]
        for cp in mine:
            cp.start()
        copies = [copy(rel, p) for p in range(n_in) for rel in range(1, N_DEV)]
        for cp in copies:
            cp.start()
        for p in range(n_in):
            for rel in range(1, N_DEV):
                arrival(rel, p).wait_recv()
        for cp in copies:
            cp.wait_send()
        for cp in mine:
            cp.wait()

    return pl.pallas_call(
        body, in_specs=[_ANY] * n_in, out_specs=[_ANY] * n_out,
        out_shape=[jax.ShapeDtypeStruct((N_DEV, len(group)) + group[0].shape[1:], group[0].dtype) for group in groups],
        scratch_shapes=[pltpu.SemaphoreType.DMA((N_DEV - 1, n_in)), pltpu.SemaphoreType.DMA((N_DEV - 1, n_in)),
                        pltpu.SemaphoreType.DMA((n_in,))],
        name="exchange_gradients",
    )(*[a for _, _, a in flat])


def reduce_adamw(parts, w, m, v, tr):
    nl, r, cols = w.shape

    def body(p_ref, w_ref, m_ref, v_ref, g_ref, d_ref, nm_ref, nv_ref):
        g = p_ref[0].astype(F32)
        for d in range(1, N_DEV):
            g = g + p_ref[d].astype(F32)
        g_ref[...] = g
        nm = ADAM_B1 * m_ref[...] + (1.0 - ADAM_B1) * g
        nv = ADAM_B2 * v_ref[...] + (1.0 - ADAM_B2) * (g * g)
        nm_ref[...] = nm
        nv_ref[...] = nv
        m_hat = nm / (1.0 - ADAM_B1 ** ADAM_STEP)
        v_hat = nv / (1.0 - ADAM_B2 ** ADAM_STEP)
        d_ref[...] = -ADAM_LR * (m_hat / (jnp.sqrt(v_hat) + ADAM_EPS) + ADAM_WD * w_ref[...])

    row = pl.BlockSpec((None, tr, cols), lambda l, i: (l, i, 0))
    return pl.pallas_call(
        body, grid=(nl, r // tr),
        in_specs=[pl.BlockSpec((N_DEV, None, tr, cols), lambda l, i: (0, l, i, 0)), row, row, row],
        out_specs=[row, row, row, row], out_shape=[jax.ShapeDtypeStruct((nl, r, cols), F32)] * 4,
        name="reduce_adamw", compiler_params=_params("arbitrary", "arbitrary"),
    )(parts, w, m, v)


ADAMW_TILE_BYTES = 1 << 20


def _adamw_tile(rows, cols):
    lanes = -(-cols // LANE) * LANE
    best = None
    for tr in range(16, rows + 1, 16):
        if rows % tr == 0 and tr * lanes * 4 <= ADAMW_TILE_BYTES:
            best = tr
    if best is None:
        raise ValueError((rows, cols))
    return best


def kernel(x, meta_tokens, mix_norm_even, w_in, ssm_conv_w, ssm_conv_b, dt_bias, a_log, d_skip, ssm_norm_w, q_norm, k_norm, sinks, w_out, mix_norm_odd, pw1_w, pw1_b, dw_w, dw_b, ln_g, ln_b, pw2_w, pw2_b, mlp_norm, w_up, w_down, loss_target, m_meta_tokens, m_mix_norm_even, m_w_in, m_ssm_conv_w, m_ssm_conv_b, m_dt_bias, m_a_log, m_d_skip, m_ssm_norm_w, m_q_norm, m_k_norm, m_sinks, m_w_out, m_mix_norm_odd, m_pw1_w, m_pw1_b, m_dw_w, m_dw_b, m_ln_g, m_ln_b, m_pw2_w, m_pw2_b, m_mlp_norm, m_w_up, m_w_down, v_meta_tokens, v_mix_norm_even, v_w_in, v_ssm_conv_w, v_ssm_conv_b, v_dt_bias, v_a_log, v_d_skip, v_ssm_norm_w, v_q_norm, v_k_norm, v_sinks, v_w_out, v_mix_norm_odd, v_pw1_w, v_pw1_b, v_dw_w, v_dw_b, v_ln_g, v_ln_b, v_pw2_w, v_pw2_b, v_mlp_norm, v_w_up, v_w_down):
    names = [p[0] for p in PARAMS]
    w_loc = dict(zip(names, (meta_tokens, mix_norm_even, w_in, ssm_conv_w, ssm_conv_b, dt_bias, a_log, d_skip, ssm_norm_w, q_norm, k_norm, sinks, w_out, mix_norm_odd, pw1_w, pw1_b, dw_w, dw_b, ln_g, ln_b, pw2_w, pw2_b, mlp_norm, w_up, w_down)))
    m_loc = dict(zip(names, (m_meta_tokens, m_mix_norm_even, m_w_in, m_ssm_conv_w, m_ssm_conv_b, m_dt_bias, m_a_log, m_d_skip, m_ssm_norm_w, m_q_norm, m_k_norm, m_sinks, m_w_out, m_mix_norm_odd, m_pw1_w, m_pw1_b, m_dw_w, m_dw_b, m_ln_g, m_ln_b, m_pw2_w, m_pw2_b, m_mlp_norm, m_w_up, m_w_down)))
    v_loc = dict(zip(names, (v_meta_tokens, v_mix_norm_even, v_w_in, v_ssm_conv_w, v_ssm_conv_b, v_dt_bias, v_a_log, v_d_skip, v_ssm_norm_w, v_q_norm, v_k_norm, v_sinks, v_w_out, v_mix_norm_odd, v_pw1_w, v_pw1_b, v_dw_w, v_dw_b, v_ln_g, v_ln_b, v_pw2_w, v_pw2_b, v_mlp_norm, v_w_up, v_w_down)))
    small_sharded = [p for p in PARAMS if p[2] is not None and p[0] not in MATMUL_WEIGHTS]
    replicated = [p for p in PARAMS if p[2] is None]
    small_list = small_sharded + replicated

    gathered = all_gather_weights([w_loc[n].astype(MXU_DTYPE) for n in MATMUL_WEIGHTS],
                                  _pack([w_loc[n] for n, _, _ in small_sharded], F32))
    w_full = {n: w_loc[n] for n, _, _ in replicated}
    w_full.update(dict(zip(MATMUL_WEIGHTS, gathered[:-1])))
    flat = gathered[-1].reshape(N_DEV, -1)
    off = 0
    for n, shape, axis in small_sharded:
        blk = _block_shape(shape, axis)
        w_full[n] = _from_shards(flat[:, off:off + _numel(blk)].reshape((N_DEV,) + blk), axis)
        off += _numel(blk)

    loss_local, grad_x, g_full = local_step(x[0], loss_target[0], w_full)
    loss = lax.psum(loss_local, ("x", "y", "c"))

    by_dev = [_to_shards(g_full[n], axis) for n, _, axis in small_sharded]
    by_dev += [jnp.broadcast_to(g_full[n][None], (N_DEV,) + tuple(shape)) for n, shape, _ in replicated]
    parts = exchange_gradients([g_full[n] for n in MATMUL_WEIGHTS] + [[_pack_rows(by_dev, F32)]])

    out = {}
    for n, part in zip(MATMUL_WEIGHTS, parts[:-1]):
        nl, r, cols = w_loc[n].shape
        out[n] = reduce_adamw(part, w_loc[n], m_loc[n], v_loc[n], _adamw_tile(r, cols))
    pk = lambda d: _pack([d[n] for n, _, _ in small_list], F32)[None]
    rows = parts[-1].shape[2]
    small_out = reduce_adamw(parts[-1], pk(w_loc), pk(m_loc), pk(v_loc), _adamw_tile(rows, PACK_W))
    flats = [buf.reshape(-1) for buf in small_out]
    off = 0
    for n, shape, axis in small_list:
        blk = _block_shape(shape, axis)
        out[n] = tuple(f[off:off + _numel(blk)].reshape(blk) for f in flats)
        off += _numel(blk)
    return (loss, grad_x[None], *[out[n][0] for n in names], *[out[n][1] for n in names],
            *[out[n][2] for n in names], *[out[n][3] for n in names])
```

```python
import math

import jax
import jax.numpy as jnp
from jax import lax
from jax.experimental import pallas as pl
from jax.experimental.pallas import tpu as pltpu

F32 = jnp.float32
MXU_DTYPE = jnp.bfloat16
GRAD_WIRE_DTYPE = jnp.bfloat16
HIGHEST = lax.Precision.HIGHEST

D_MODEL = 1024
N_META = 16
BLOCK = 128
FRONT_PAD = BLOCK - N_META
ATT_HEADS = 8
ATT_KV_HEADS = 2
HEAD_DIM = 64
SSM_HEADS = 16
SSM_HEAD_DIM = 64
SSM_INNER = 1024
SSM_GROUPS = 2
SSM_STATE = 64
SSM_CONV = 4
CONF_KERNEL = 31
D_FF = 4096
EPS = 1e-6
LN_EPS = 1e-5
Q_W = 512
KV_W = 128
IN_W = 3088
MIX_W = 1536
DEPTH = 4
N_DEV = 8

ADAM_LR = 0.001
ADAM_B1 = 0.9
ADAM_B2 = 0.999
ADAM_EPS = 1e-08
ADAM_WD = 0.01
ADAM_STEP = 10

PROJ_W = 3200
COL_Z, COL_XS, COL_Q, COL_BC, COL_K, COL_V, COL_DT = 0, 1024, 2048, 2560, 2816, 2944, 3072

ROW_TILE = 640
VMEM_LIMIT = 56 * 1024 * 1024
LANE = 128
PACK_W = 1024


def _params(*sem):
    return pltpu.CompilerParams(dimension_semantics=sem, vmem_limit_bytes=VMEM_LIMIT)


def _mx(x):
    return x.astype(MXU_DTYPE)


def _dot(a, b):
    return jnp.dot(_mx(a), _mx(b), preferred_element_type=F32)


def _dot_nt(a, b):
    return lax.dot_general(_mx(a), _mx(b), (((1,), (1,)), ((), ())), preferred_element_type=F32)


def _dot_tn(a, b):
    return lax.dot_general(_mx(a), _mx(b), (((0,), (0,)), ((), ())), preferred_element_type=F32)


def _split3(x):
    hi = x.astype(jnp.bfloat16)
    r1 = x - hi.astype(F32)
    mid = r1.astype(jnp.bfloat16)
    lo = (r1 - mid.astype(F32)).astype(jnp.bfloat16)
    return hi, mid, lo


def _sel_dot(x, sel, dims):
    x_first = dims[2]
    if sel.dtype == jnp.bool_:
        sel = jnp.where(sel, 1.0, 0.0)
    one = sel.astype(jnp.bfloat16)
    acc = None
    for part in _split3(x):
        args = (part, one) if x_first else (one, part)
        t = lax.dot_general(*args, (dims[:2], ((), ())), preferred_element_type=F32)
        acc = t if acc is None else acc + t
    return acc


def _dot_hi(a, b, exact="a"):
    if exact == "a":
        return _sel_dot(a, b, ((1,), (0,), True))
    return _sel_dot(b, a, ((1,), (0,), False))


def _dot_tn_hi(a, b):
    return _sel_dot(a, b, ((0,), (0,), True))


def _sigmoid(x):
    return 1.0 / (1.0 + jnp.exp(-x))


def _row_ids(start, rows, cols):
    return start + lax.broadcasted_iota(jnp.int32, (rows, cols), 0)


def rms_fwd(h, g, name="rms_fwd"):
    m, d = h.shape
    tm = ROW_TILE

    def body(h_ref, g_ref, u_ref):
        x = h_ref[...]
        r = lax.rsqrt(jnp.mean(x * x, -1, keepdims=True) + EPS)
        u_ref[...] = ((x * r) * g_ref[...]).astype(u_ref.dtype)

    return pl.pallas_call(
        body, grid=(m // tm,),
        in_specs=[pl.BlockSpec((tm, d), lambda i: (i, 0)), pl.BlockSpec((1, d), lambda i: (0, 0))],
        out_specs=pl.BlockSpec((tm, d), lambda i: (i, 0)),
        out_shape=jax.ShapeDtypeStruct((m, d), MXU_DTYPE), name=name, compiler_params=_params("arbitrary"),
    )(h, g)


def rms_bwd(h, g, du, dh_out, name="rms_bwd"):
    m, d = h.shape
    tm = ROW_TILE

    def body(h_ref, g_ref, du_ref, dho_ref, dh_ref, dg_ref):
        @pl.when(pl.program_id(0) == 0)
        def _():
            dg_ref[...] = jnp.zeros_like(dg_ref)

        x = h_ref[...]
        du_ = du_ref[...]
        r = lax.rsqrt(jnp.mean(x * x, -1, keepdims=True) + EPS)
        gy = du_ * g_ref[...]
        dx = r * gy - x * ((r * r * r) * jnp.mean(x * gy, -1, keepdims=True))
        dh_ref[...] = dho_ref[...] + dx
        dg_ref[...] += jnp.sum(du_ * (x * r), axis=0, keepdims=True)

    row = pl.BlockSpec((tm, d), lambda i: (i, 0))
    vec = pl.BlockSpec((1, d), lambda i: (0, 0))
    return pl.pallas_call(
        body, grid=(m // tm,), in_specs=[row, vec, row, row], out_specs=[row, vec],
        out_shape=[jax.ShapeDtypeStruct((m, d), F32), jax.ShapeDtypeStruct((1, d), F32)],
        name=name, compiler_params=_params("arbitrary"),
    )(h, g, du, dh_out)


def loss_fwd_bwd(h, target, name="loss"):
    m, d = h.shape
    nb = m // BLOCK

    def body(h_ref, t_ref, dh_ref, l_ref):
        i = pl.program_id(0)

        @pl.when(i == 0)
        def _():
            l_ref[...] = jnp.zeros_like(l_ref)
            dh_ref[...] = jnp.zeros_like(dh_ref)

        @pl.when(i > 0)
        def _():
            e = h_ref[...] - t_ref[...]
            dh_ref[...] = e * (1.0 / d)
            s = jnp.sum(jnp.sum(e * e, axis=-1, keepdims=True), axis=0, keepdims=True)
            l_ref[...] += jnp.broadcast_to(s * (0.5 / d), l_ref.shape)

    return pl.pallas_call(
        body, grid=(nb,),
        in_specs=[pl.BlockSpec((BLOCK, d), lambda i: (i, 0)),
                  pl.BlockSpec((BLOCK, d), lambda i: (jnp.maximum(i - 1, 0), 0))],
        out_specs=[pl.BlockSpec((BLOCK, d), lambda i: (i, 0)), pl.BlockSpec((1, LANE), lambda i: (0, 0))],
        out_shape=[jax.ShapeDtypeStruct((m, d), F32), jax.ShapeDtypeStruct((1, LANE), F32)],
        name=name, compiler_params=_params("arbitrary"),
    )(h, target)


def col_sum(x, name="col_sum"):
    m, n = x.shape
    tm = ROW_TILE

    def body(x_ref, o_ref):
        @pl.when(pl.program_id(0) == 0)
        def _():
            o_ref[...] = jnp.zeros_like(o_ref)

        o_ref[...] += jnp.sum(x_ref[...].astype(F32), axis=0, keepdims=True)

    return pl.pallas_call(
        body, grid=(m // tm,), in_specs=[pl.BlockSpec((tm, n), lambda i: (i, 0))],
        out_specs=pl.BlockSpec((1, n), lambda i: (0, 0)), out_shape=jax.ShapeDtypeStruct((1, n), F32),
        name=name, compiler_params=_params("arbitrary"),
    )(x)


def matmul(a, b, *, b_kind="full", layer=0, trans_b=False, tn=None, epilogue=None, bias=None, extra=None,
           out_dtype=F32, name="matmul"):
    m, k = a.shape
    tm = ROW_TILE
    merge = False
    if b_kind == "full":
        n = b.shape[0] if trans_b else b.shape[1]
        tn = n if tn is None else tn
        b_spec = pl.BlockSpec((tn, k), lambda i, j: (j, 0)) if trans_b else pl.BlockSpec((k, tn), lambda i, j: (0, j))
    elif b_kind == "rowshard":
        ks, wn = b.shape[2], b.shape[3]
        if trans_b and tn == ks:
            assert wn == k
            n = N_DEV * ks
            b_spec = pl.BlockSpec((None, None, ks, wn), lambda i, j: (j, layer, 0, 0))
        else:
            assert tn is None
            merge = True
            n = N_DEV * ks if trans_b else wn
            assert (wn if trans_b else N_DEV * ks) == k
            tn = n
            b_spec = pl.BlockSpec((N_DEV, None, ks, wn), lambda i, j: (0, layer, 0, 0))
    else:
        raise ValueError(b_kind)
    has_bias = bias is not None
    has_extra = extra is not None

    def body(*refs):
        a_ref, b_ref = refs[0], refs[1]
        pos = 2
        bias_ref = extra_ref = None
        if has_bias:
            bias_ref = refs[pos]
            pos += 1
        if has_extra:
            extra_ref = refs[pos]
            pos += 1
        outs = refs[pos:]
        w = b_ref[...]
        if merge:
            w = w.reshape(N_DEV * w.shape[1], w.shape[2])
        if trans_b:
            acc = _dot_nt(a_ref[...], w)
        else:
            acc = _dot(a_ref[...], w)
        if has_bias:
            acc = acc + bias_ref[...]
        if epilogue is None:
            outs[0][...] = acc.astype(outs[0].dtype)
        elif epilogue == "relu2":
            outs[0][...] = acc
            r = jnp.maximum(acc, 0.0)
            outs[1][...] = (r * r).astype(outs[1].dtype)
        elif epilogue == "drelu2":
            outs[0][...] = (acc * (2.0 * jnp.maximum(extra_ref[...], 0.0))).astype(outs[0].dtype)
        elif epilogue == "resid":
            rows = _row_ids(pl.program_id(0) * tm, tm, tn)
            outs[0][...] = extra_ref[...] + jnp.where(rows >= FRONT_PAD, acc, 0.0)
        else:
            raise ValueError(epilogue)

    in_specs = [pl.BlockSpec((tm, k), lambda i, j: (i, 0)), b_spec]
    args = [a, b]
    if has_bias:
        in_specs.append(pl.BlockSpec((1, tn), lambda i, j: (0, j)))
        args.append(bias)
    if has_extra:
        in_specs.append(pl.BlockSpec((tm, tn), lambda i, j: (i, j)))
        args.append(extra)
    tile = pl.BlockSpec((tm, tn), lambda i, j: (i, j))
    if epilogue == "relu2":
        out_specs = [tile, tile]
        out_shape = [jax.ShapeDtypeStruct((m, n), F32), jax.ShapeDtypeStruct((m, n), MXU_DTYPE)]
    else:
        out_specs = tile
        out_shape = jax.ShapeDtypeStruct((m, n), out_dtype)
    return pl.pallas_call(
        body, grid=(m // tm, n // tn), in_specs=in_specs, out_specs=out_specs, out_shape=out_shape,
        name=name, compiler_params=_params("arbitrary", "arbitrary"),
    )(*args)


def matmul_tn(x, dy, *, ti, tn, out_dtype=F32, name="matmul_tn"):
    m, k1 = x.shape
    n = dy.shape[1]
    tm = ROW_TILE
    last = m // tm - 1

    def body(x_ref, dy_ref, o_ref, acc_ref):
        r = pl.program_id(2)

        @pl.when(r == 0)
        def _():
            acc_ref[...] = jnp.zeros_like(acc_ref)

        acc_ref[...] += _dot_tn(x_ref[...], dy_ref[...])

        @pl.when(r == last)
        def _():
            o_ref[...] = acc_ref[...].astype(o_ref.dtype)

    out_specs = pl.BlockSpec((ti, tn), lambda i, j, r: (i, j))
    out_shape = jax.ShapeDtypeStruct((k1, n), out_dtype)
    return pl.pallas_call(
        body, grid=(k1 // ti, n // tn, m // tm),
        in_specs=[pl.BlockSpec((tm, ti), lambda i, j, r: (r, i)), pl.BlockSpec((tm, tn), lambda i, j, r: (r, j))],
        out_specs=out_specs, out_shape=out_shape, scratch_shapes=[pltpu.VMEM((ti, tn), F32)], name=name,
        compiler_params=_params("arbitrary", "arbitrary", "arbitrary"),
    )(x, dy)


FF_BLOCK = D_FF // N_DEV


def _ff_cols(d):
    return slice(FF_BLOCK * d, FF_BLOCK * (d + 1))


def mlp_up(u, w, layer, *, bias=None, relu2=True, out_dtype=None, name="mlp_up"):
    m = u.shape[0]
    ns = w.shape[3]
    n = N_DEV * ns
    tm = ROW_TILE
    out_dtype = MXU_DTYPE if relu2 else out_dtype
    has_bias = bias is not None

    def body(*refs):
        u_ref, w_ref = refs[0], refs[1]
        bias_ref = refs[2] if has_bias else None
        o_ref = refs[-1]
        u_ = u_ref[...]
        for d in range(N_DEV):
            cols = slice(ns * d, ns * (d + 1))
            r = _dot(u_, w_ref[d])
            if has_bias:
                r = r + bias_ref[:, cols]
            if relu2:
                r = jnp.maximum(r, 0.0)
                r = r * r
            o_ref[:, cols] = r.astype(o_ref.dtype)

    in_specs = [pl.BlockSpec((tm, D_MODEL), lambda i: (i, 0)),
                pl.BlockSpec((N_DEV, None, D_MODEL, ns), lambda i: (0, layer, 0, 0))]
    args = [u, w]
    if has_bias:
        in_specs.append(pl.BlockSpec((1, n), lambda i: (0, 0)))
        args.append(bias)
    return pl.pallas_call(
        body, grid=(m // tm,), in_specs=in_specs, out_specs=pl.BlockSpec((tm, n), lambda i: (i, 0)),
        out_shape=jax.ShapeDtypeStruct((m, n), out_dtype), name=name, compiler_params=_params("arbitrary"),
    )(*args)


def mlp_dact(dh, w_down, act, layer, name="mlp_dact"):
    m = dh.shape[0]
    tm = ROW_TILE

    def body(dh_ref, w_ref, act_ref, o_ref):
        dh_ = dh_ref[...]
        for d in range(N_DEV):
            r = jnp.sqrt(act_ref[:, _ff_cols(d)].astype(F32))
            o_ref[:, _ff_cols(d)] = (_dot_nt(dh_, w_ref[d]) * (2.0 * r)).astype(o_ref.dtype)

    return pl.pallas_call(
        body, grid=(m // tm,),
        in_specs=[pl.BlockSpec((tm, D_MODEL), lambda i: (i, 0)),
                  pl.BlockSpec((N_DEV, None, FF_BLOCK, D_MODEL), lambda i: (0, layer, 0, 0)),
                  pl.BlockSpec((tm, D_FF), lambda i: (i, 0))],
        out_specs=pl.BlockSpec((tm, D_FF), lambda i: (i, 0)),
        out_shape=jax.ShapeDtypeStruct((m, D_FF), MXU_DTYPE), name=name, compiler_params=_params("arbitrary"),
    )(dh, w_down, act)


def mlp_du_rms_bwd(da, w_up, layer, h, g, dh_out, name="mlp_du"):
    m, n = da.shape
    ns = w_up.shape[3]
    assert n == N_DEV * ns
    tm = ROW_TILE

    def body(da_ref, w_ref, h_ref, g_ref, dho_ref, dh_ref, dg_ref):
        @pl.when(pl.program_id(0) == 0)
        def _():
            dg_ref[...] = jnp.zeros_like(dg_ref)

        du = _dot_nt(da_ref[:, 0:ns], w_ref[0])
        for d in range(1, N_DEV):
            du = du + _dot_nt(da_ref[:, ns * d:ns * (d + 1)], w_ref[d])
        x = h_ref[...]
        r = lax.rsqrt(jnp.mean(x * x, -1, keepdims=True) + EPS)
        gy = du * g_ref[...]
        dx = r * gy - x * ((r * r * r) * jnp.mean(x * gy, -1, keepdims=True))
        dh_ref[...] = dho_ref[...] + dx
        dg_ref[...] += jnp.sum(du * (x * r), axis=0, keepdims=True)

    row = pl.BlockSpec((tm, D_MODEL), lambda i: (i, 0))
    vec = pl.BlockSpec((1, D_MODEL), lambda i: (0, 0))
    return pl.pallas_call(
        body, grid=(m // tm,),
        in_specs=[pl.BlockSpec((tm, n), lambda i: (i, 0)),
                  pl.BlockSpec((N_DEV, None, D_MODEL, ns), lambda i: (0, layer, 0, 0)), row, vec, row],
        out_specs=[row, vec],
        out_shape=[jax.ShapeDtypeStruct((m, D_MODEL), F32), jax.ShapeDtypeStruct((1, D_MODEL), F32)],
        name=name, compiler_params=_params("arbitrary"),
    )(da, w_up, h, g, dh_out)


def mlp_dw_up(u, da, name="mlp_dw_up"):
    m, n = da.shape
    ns = n // N_DEV
    tm = ROW_TILE
    last = m // tm - 1

    def body(u_ref, da_ref, o_ref, acc_ref):
        i = pl.program_id(0)

        @pl.when(i == 0)
        def _():
            acc_ref[...] = jnp.zeros_like(acc_ref)

        acc_ref[...] += _dot_tn(u_ref[...], da_ref[...])

        @pl.when(i == last)
        def _():
            for d in range(N_DEV):
                o_ref[d] = acc_ref[:, ns * d:ns * (d + 1)].astype(o_ref.dtype)

    return pl.pallas_call(
        body, grid=(m // tm,),
        in_specs=[pl.BlockSpec((tm, D_MODEL), lambda i: (i, 0)), pl.BlockSpec((tm, n), lambda i: (i, 0))],
        out_specs=pl.BlockSpec((N_DEV, D_MODEL, ns), lambda i: (0, 0, 0)),
        out_shape=jax.ShapeDtypeStruct((N_DEV, D_MODEL, ns), GRAD_WIRE_DTYPE),
        scratch_shapes=[pltpu.VMEM((D_MODEL, n), F32)], name=name, compiler_params=_params("arbitrary"),
    )(u, da)


def mlp_dw_down(act, dh, name="mlp_dw_down"):
    m = act.shape[0]
    tm = ROW_TILE
    last = m // tm - 1

    def body(a_ref, dh_ref, o_ref, acc_ref):
        i = pl.program_id(0)

        @pl.when(i == 0)
        def _():
            acc_ref[...] = jnp.zeros_like(acc_ref)

        acc_ref[...] += _dot_tn(a_ref[...], dh_ref[...])

        @pl.when(i == last)
        def _():
            o_ref[...] = acc_ref[...].astype(o_ref.dtype)

    return pl.pallas_call(
        body, grid=(m // tm,),
        in_specs=[pl.BlockSpec((tm, D_FF), lambda i: (i, 0)), pl.BlockSpec((tm, D_MODEL), lambda i: (i, 0))],
        out_specs=pl.BlockSpec((D_FF, D_MODEL), lambda i: (0, 0)),
        out_shape=jax.ShapeDtypeStruct((D_FF, D_MODEL), GRAD_WIRE_DTYPE),
        scratch_shapes=[pltpu.VMEM((D_FF, D_MODEL), F32)], name=name, compiler_params=_params("arbitrary"),
    )(act, dh)


_ATT_SCALE = HEAD_DIM ** -0.5


def _alibi_slope(h):
    return 2.0 ** (-8.0 * (h + 1) / ATT_HEADS)


HEADS_PER_KV = ATT_HEADS // ATT_KV_HEADS
STACK = HEADS_PER_KV * BLOCK
N_KEYS = 3 * BLOCK


def _head_select(width):
    c = jnp.arange(width)[:, None]
    h = jnp.arange(LANE)[None, :]
    return (c // HEAD_DIM == h).astype(F32)


def _head_fold(width):
    c = jnp.arange(width)[:, None]
    j = jnp.arange(LANE)[None, :]
    return (c % HEAD_DIM == j).astype(F32)


def _head_rms(x, sel, sel_t):
    r = lax.rsqrt(_dot_hi(x * x, sel) * (1.0 / HEAD_DIM) + EPS)
    return r, _dot_hi(r, sel_t)


def _head_norm_bwd(x, r, r_full, w_t, dy, sel, sel_t):
    gy = dy * w_t
    coef = _dot_hi((r * r * r) * _dot_hi(x * gy, sel) * (1.0 / HEAD_DIM), sel_t)
    return r_full * gy - x * coef, jnp.sum(dy * (x * r_full), axis=0, keepdims=True)


def _low_lanes(rows):
    return lax.broadcasted_iota(jnp.int32, (rows, LANE), 1) < HEAD_DIM


def _dup_half(a, g):
    rolled = pltpu.roll(a, HEAD_DIM, 1)
    low = _low_lanes(a.shape[0])
    return jnp.where(low, a, rolled) if g == 0 else jnp.where(low, rolled, a)


def _stack_heads(x, g):
    low = _low_lanes(BLOCK)
    parts = []
    for pair in range(2):
        p = x[:, 2 * LANE * g + LANE * pair:2 * LANE * g + LANE * (pair + 1)]
        parts += [jnp.where(low, p, 0.0), jnp.where(low, 0.0, p)]
    return jnp.concatenate(parts, axis=0)


def _unstack_heads(groups):
    low = _low_lanes(BLOCK)
    cols = []
    for o in groups:
        for pair in range(2):
            cols.append(jnp.where(low, o[2 * pair * BLOCK:(2 * pair + 1) * BLOCK], o[(2 * pair + 1) * BLOCK:(2 * pair + 2) * BLOCK]))
    return jnp.concatenate(cols, axis=1)


def _fold_halves(a, g):
    s = a + pltpu.roll(a, HEAD_DIM, 1)
    low = _low_lanes(a.shape[0])
    return jnp.where(low if g == 0 else jnp.logical_not(low), s, 0.0)


def _att_bias(b):
    r = lax.broadcasted_iota(jnp.int32, (STACK, N_KEYS), 0) & (BLOCK - 1)
    col = lax.broadcasted_iota(jnp.int32, (STACK, N_KEYS), 1)
    cc = col & (BLOCK - 1)
    is_meta = col < BLOCK
    is_prev = jnp.logical_and(col >= BLOCK, col < 2 * BLOCK)
    q_pos = b * BLOCK + r - FRONT_PAD
    meta_j = cc - FRONT_PAD
    valid_m = jnp.logical_and(cc >= FRONT_PAD, q_pos >= meta_j)
    valid_p = jnp.logical_and(cc > r, b >= 2)
    valid_c = jnp.logical_and(cc <= r, b >= 1)
    is_cur = col >= 2 * BLOCK
    valid = jnp.logical_or(jnp.logical_and(is_meta, valid_m),
                           jnp.logical_or(jnp.logical_and(is_prev, valid_p), jnp.logical_and(is_cur, valid_c)))
    dist = jnp.where(is_meta, jnp.minimum(q_pos - meta_j, BLOCK), jnp.where(is_prev, r - cc + BLOCK, r - cc))
    return valid, dist.astype(F32)


def _per_head_column(values):
    hid = lax.broadcasted_iota(jnp.int32, (STACK, 1), 0) >> 7
    col = jnp.where(hid == 0, values[0], values[1])
    for j in range(2, HEADS_PER_KV):
        col = jnp.where(hid == j, values[j], col)
    return col


def _att_group_probs(qs, kd, valid, dist, g, sk_ref):
    slope = _per_head_column([_alibi_slope(HEADS_PER_KV * g + j) for j in range(HEADS_PER_KV)])
    sink = _per_head_column([sk_ref[HEADS_PER_KV * g + j] for j in range(HEADS_PER_KV)])
    s = jnp.where(valid, _dot_nt(qs, kd) * _ATT_SCALE - slope * dist, -1e30)
    mx = jnp.maximum(jnp.max(s, axis=-1, keepdims=True), sink)
    p = jnp.exp(s - mx)
    p_sink = jnp.exp(sink - mx)
    inv = 1.0 / (jnp.sum(p, axis=-1, keepdims=True) + p_sink)
    return p * inv, p_sink * inv


def attention_fwd(proj, q_w, k_w, sinks, name="att_fwd"):
    m = proj.shape[0]
    nb = m // BLOCK
    cq, ck, cv = COL_Q // Q_W, COL_K // KV_W, COL_V // KV_W
    sel_q, sel_k = _head_select(Q_W), _head_select(KV_W)

    def body(q_ref, kc_ref, vc_ref, vp_ref, vm_ref, qw_ref, kw_ref, sk_ref, sq_ref, sqt_ref, skk_ref, skt_ref,
             o_ref, kpn_s, kmn_s):
        b = pl.program_id(0)
        q, kc = q_ref[...], kc_ref[...]
        _, rq = _head_rms(q, sq_ref[...], sqt_ref[...])
        qn = q * rq * qw_ref[...]
        _, rk = _head_rms(kc, skk_ref[...], skt_ref[...])
        kcn = kc * rk * kw_ref[...]

        @pl.when(b == 0)
        def _():
            kmn_s[...] = kcn
            kpn_s[...] = kcn

        kpn, kmn = kpn_s[...], kmn_s[...]
        kpn_s[...] = kcn
        vc, vp, vm = vc_ref[...], vp_ref[...], vm_ref[...]
        valid, dist = _att_bias(b)
        outs = []
        for g in range(ATT_KV_HEADS):
            kd = jnp.concatenate([_dup_half(kmn, g), _dup_half(kpn, g), _dup_half(kcn, g)], axis=0)
            vd = jnp.concatenate([_dup_half(vm, g), _dup_half(vp, g), _dup_half(vc, g)], axis=0)
            probs, _ = _att_group_probs(_stack_heads(qn, g), kd, valid, dist, g, sk_ref)
            outs.append(_dot(probs, vd))
        o_ref[...] = _unstack_heads(outs).astype(o_ref.dtype)

    prev = lambda i: jnp.maximum(i - 1, 0)
    full = lambda s: pl.BlockSpec(s, lambda i: (0,) * len(s))
    return pl.pallas_call(
        body, grid=(nb,),
        in_specs=[pl.BlockSpec((BLOCK, Q_W), lambda i: (i, cq)),
                  pl.BlockSpec((BLOCK, KV_W), lambda i: (i, ck)), pl.BlockSpec((BLOCK, KV_W), lambda i: (i, cv)),
                  pl.BlockSpec((BLOCK, KV_W), lambda i: (prev(i), cv)), pl.BlockSpec((BLOCK, KV_W), lambda i: (0, cv)),
                  full((1, Q_W)), full((1, KV_W)), pl.BlockSpec(memory_space=pltpu.SMEM),
                  full((Q_W, LANE)), full((LANE, Q_W)), full((KV_W, LANE)), full((LANE, KV_W))],
        out_specs=pl.BlockSpec((BLOCK, Q_W), lambda i: (i, 0)),
        out_shape=jax.ShapeDtypeStruct((m, Q_W), MXU_DTYPE),
        scratch_shapes=[pltpu.VMEM((BLOCK, KV_W), F32), pltpu.VMEM((BLOCK, KV_W), F32)],
        name=name, compiler_params=_params("arbitrary"),
    )(proj, proj, proj, proj, proj, q_w, k_w, sinks, sel_q, sel_q.T, sel_k, sel_k.T)


def attention_bwd(proj, dmix, q_w, k_w, sinks, carry=(), name="att_bwd"):
    m = proj.shape[0]
    nb = m // BLOCK
    cq, ck, cv = COL_Q // Q_W, COL_K // KV_W, COL_V // KV_W
    c_datt = 0
    sel_q, sel_k = _head_select(Q_W), _head_select(KV_W)
    fold_q, fold_k = _head_fold(Q_W), _head_fold(KV_W)

    def body(do_ref, q_ref, kc_ref, vc_ref, kp_ref, vp_ref, km_ref, vm_ref, qw_ref, kw_ref, sk_ref,
             sq_ref, sqt_ref, skk_ref, skt_ref, fq_ref, fk_ref,
             dq_ref, dk_ref, dv_ref, dqw_ref, dkw_ref, dsk_ref, car_k, car_v, met_k, met_v, kmn_s, qw_acc, kw_acc):
        i = pl.program_id(0)
        b = nb - 1 - i
        sel_q_, sel_qt, sel_k_, sel_kt = sq_ref[...], sqt_ref[...], skk_ref[...], skt_ref[...]
        qw, kw = qw_ref[...], kw_ref[...]

        @pl.when(i == 0)
        def _():
            for r in (car_k, car_v, met_k, met_v, qw_acc, kw_acc, dsk_ref):
                r[...] = jnp.zeros_like(r)
            km = km_ref[...]
            kmn_s[...] = km * _head_rms(km, sel_k_, sel_kt)[1] * kw

        q, kc, kp = q_ref[...], kc_ref[...], kp_ref[...]
        rq, rq_full = _head_rms(q, sel_q_, sel_qt)
        qn = q * rq_full * qw
        rk, rk_full = _head_rms(kc, sel_k_, sel_kt)
        kcn = kc * rk_full * kw
        kpn = kp * _head_rms(kp, sel_k_, sel_kt)[1] * kw
        kmn = kmn_s[...]
        vc, vp, vm = vc_ref[...], vp_ref[...], vm_ref[...]
        do = do_ref[...]
        valid, dist = _att_bias(b)
        lane = lax.broadcasted_iota(jnp.int32, (1, LANE), 1)
        dsk = jnp.zeros((1, LANE), F32)
        dkd_sum = jnp.zeros((N_KEYS, KV_W), F32)
        dvd_sum = jnp.zeros((N_KEYS, KV_W), F32)
        dqd = []
        for g in range(ATT_KV_HEADS):
            kd = jnp.concatenate([_dup_half(kmn, g), _dup_half(kpn, g), _dup_half(kcn, g)], axis=0)
            vd = jnp.concatenate([_dup_half(vm, g), _dup_half(vp, g), _dup_half(vc, g)], axis=0)
            qs = _stack_heads(qn, g)
            dos = _stack_heads(do, g)
            probs, p_sink = _att_group_probs(qs, kd, valid, dist, g, sk_ref)
            o = _dot(probs, vd)
            delta = jnp.sum(dos * o, axis=-1, keepdims=True)
            ds = probs * (_dot_nt(dos, vd) - delta)
            dqd.append(_dot(ds, kd) * _ATT_SCALE)
            dkd_sum = dkd_sum + _fold_halves(_dot_tn(ds, qs) * _ATT_SCALE, g)
            dvd_sum = dvd_sum + _fold_halves(_dot_tn(probs, dos), g)
            sink_grad = p_sink * delta
            for j in range(HEADS_PER_KV):
                part = jnp.sum(sink_grad[BLOCK * j:BLOCK * (j + 1)], axis=0, keepdims=True)
                dsk = dsk - jnp.where(lane == HEADS_PER_KV * g + j, part, 0.0)
        dq, dqw = _head_norm_bwd(q, rq, rq_full, qw, _unstack_heads(dqd), sel_q_, sel_qt)
        dq_ref[...] = dq
        qw_acc[...] += dqw
        dsk_ref[...] += dsk

        met_k[...] += dkd_sum[0:BLOCK]
        met_v[...] += dvd_sum[0:BLOCK]
        first = (b == 0).astype(F32)
        dkn_tot = dkd_sum[2 * BLOCK:3 * BLOCK] + car_k[...] + first * met_k[...]
        dv_ref[...] = dvd_sum[2 * BLOCK:3 * BLOCK] + car_v[...] + first * met_v[...]
        car_k[...] = dkd_sum[BLOCK:2 * BLOCK]
        car_v[...] = dvd_sum[BLOCK:2 * BLOCK]
        dk, dkw = _head_norm_bwd(kc, rk, rk_full, kw, dkn_tot, sel_k_, sel_kt)
        dk_ref[...] = dk
        kw_acc[...] += dkw

        @pl.when(i == nb - 1)
        def _():
            dqw_ref[...] = _dot_hi(jnp.broadcast_to(qw_acc[...], (8, Q_W)), fq_ref[...])[0:1]
            dkw_ref[...] = _dot_hi(jnp.broadcast_to(kw_acc[...], (8, KV_W)), fk_ref[...])[0:1]

    blk = lambda i: nb - 1 - i
    prev = lambda i: jnp.maximum(nb - 2 - i, 0)
    full = lambda s: pl.BlockSpec(s, lambda i: (0,) * len(s))
    kv_scratch = pltpu.VMEM((BLOCK, KV_W), F32)
    body, ex_in, ex_out, ex_shape, ex_scratch = _with_exchange(body, 17, 6, carry, nb)
    outs = pl.pallas_call(
        body, grid=(nb,),
        in_specs=[pl.BlockSpec((BLOCK, Q_W), lambda i: (blk(i), c_datt)),
                  pl.BlockSpec((BLOCK, Q_W), lambda i: (blk(i), cq)),
                  pl.BlockSpec((BLOCK, KV_W), lambda i: (blk(i), ck)), pl.BlockSpec((BLOCK, KV_W), lambda i: (blk(i), cv)),
                  pl.BlockSpec((BLOCK, KV_W), lambda i: (prev(i), ck)), pl.BlockSpec((BLOCK, KV_W), lambda i: (prev(i), cv)),
                  pl.BlockSpec((BLOCK, KV_W), lambda i: (0, ck)), pl.BlockSpec((BLOCK, KV_W), lambda i: (0, cv)),
                  full((1, Q_W)), full((1, KV_W)), pl.BlockSpec(memory_space=pltpu.SMEM),
                  full((Q_W, LANE)), full((LANE, Q_W)), full((KV_W, LANE)), full((LANE, KV_W)),
                  full((Q_W, LANE)), full((KV_W, LANE))] + ex_in,
        out_specs=[pl.BlockSpec((BLOCK, Q_W), lambda i: (blk(i), 0)),
                   pl.BlockSpec((BLOCK, KV_W), lambda i: (blk(i), 0)), pl.BlockSpec((BLOCK, KV_W), lambda i: (blk(i), 0)),
                   full((1, LANE)), full((1, LANE)), full((1, LANE))] + ex_out,
        out_shape=[jax.ShapeDtypeStruct((m, Q_W), F32), jax.ShapeDtypeStruct((m, KV_W), F32),
                   jax.ShapeDtypeStruct((m, KV_W), F32), jax.ShapeDtypeStruct((1, LANE), F32),
                   jax.ShapeDtypeStruct((1, LANE), F32), jax.ShapeDtypeStruct((1, LANE), F32)] + ex_shape,
        scratch_shapes=[kv_scratch, kv_scratch, kv_scratch, kv_scratch, kv_scratch,
                        pltpu.VMEM((1, Q_W), F32), pltpu.VMEM((1, KV_W), F32)] + ex_scratch,
        name=name, compiler_params=_params("arbitrary"),
    )(dmix, proj, proj, proj, proj, proj, proj, proj, q_w, k_w, sinks, sel_q, sel_q.T, sel_k, sel_k.T, fold_q, fold_k,
      *carry)
    return outs[:6], outs[6:]


HALO = 8
GROUP_W = SSM_INNER // SSM_GROUPS
HEADS_PER_GROUP = SSM_HEADS // SSM_GROUPS


def _head_expand():
    h = jnp.arange(LANE)[:, None]
    c = jnp.arange(SSM_INNER)[None, :]
    return (c // SSM_HEAD_DIM == h).astype(F32)


def _softplus(x):
    return jnp.maximum(x, 0.0) + jnp.log1p(jnp.exp(-jnp.abs(x)))


def _ssd_decays(dt, a_log_row):
    row = lax.broadcasted_iota(jnp.int32, (BLOCK, BLOCK), 0)
    col = lax.broadcasted_iota(jnp.int32, (BLOCK, BLOCK), 1)
    lower = row >= col
    a = -jnp.exp(a_log_row)
    a_cs = _dot_hi(lower, dt * a, exact="b")
    return a, a_cs, lower


def _decay_matrix(a_cs, a_cs_t, h, lower):
    diff = a_cs[:, h:h + 1] - a_cs_t[h:h + 1, :]
    return jnp.where(lower, jnp.exp(jnp.where(lower, diff, 0.0)), 0.0)


def _conv_taps(s_ref, w_ref, first, rows):
    acc = w_ref[0:1, :] * s_ref[pl.ds(first, rows), :]
    for j in range(1, SSM_CONV):
        acc = acc + w_ref[j:j + 1, :] * s_ref[pl.ds(first + j, rows), :]
    return acc


def ssd_fwd(proj, cw_x, cw_bc, cb_x, cb_bc, dt_bias, a_log, d_exp, norm_w, name="ssd_fwd"):
    m = proj.shape[0]
    nb = m // BLOCK
    expand = _head_expand()
    expand_t = expand.T

    def body(z_ref, xs_ref, bc_ref, dtr_ref, cwx_ref, cwbc_ref, cbx_ref, cbbc_ref, dtb_ref, alog_ref, dexp_ref,
             nw_ref, e_ref, et_ref, out_ref, prex_ref, prebc_ref, dt_ref, ypre_ref, st_ref, sx, sbc, state):
        c = pl.program_id(0)

        @pl.when(c == 0)
        def _():
            sx[0:HALO, :] = jnp.zeros((HALO, SSM_INNER), F32)
            sbc[0:HALO, :] = jnp.zeros((HALO, 2 * LANE), F32)
            state[...] = jnp.zeros_like(state)

        sx[HALO:HALO + BLOCK, :] = xs_ref[...]
        sbc[HALO:HALO + BLOCK, :] = bc_ref[...]
        first = HALO - (SSM_CONV - 1)
        pre_x = _conv_taps(sx, cwx_ref, first, BLOCK) + cbx_ref[...]
        pre_bc = _conv_taps(sbc, cwbc_ref, first, BLOCK) + cbbc_ref[...]
        sx[0:HALO, :] = xs_ref[BLOCK - HALO:BLOCK, :]
        sbc[0:HALO, :] = bc_ref[BLOCK - HALO:BLOCK, :]
        prex_ref[...] = pre_x
        prebc_ref[...] = pre_bc
        xc = pre_x * _sigmoid(pre_x)
        bcv = pre_bc * _sigmoid(pre_bc)

        rows = _row_ids(c * BLOCK, BLOCK, LANE)
        lanes = lax.broadcasted_iota(jnp.int32, (BLOCK, LANE), 1)
        live = jnp.logical_and(rows >= FRONT_PAD, lanes < SSM_HEADS)
        dt = jnp.where(live, _softplus(dtr_ref[...] + dtb_ref[...]), 0.0)
        dt_ref[...] = dt
        a, a_cs, lower = _ssd_decays(dt, alog_ref[...])
        a_cs_t = a_cs.T
        dt_t = dt.T
        e = e_ref[...]
        es_full = _dot_hi(jnp.exp(a_cs), e)
        wx_full = _dot_hi(jnp.exp(a_cs[BLOCK - 1:BLOCK, :] - a_cs) * dt, e)
        end_col = jnp.exp(a_cs_t[:, BLOCK - 1:BLOCK])
        dec_full = _dot_hi(et_ref[...], jnp.broadcast_to(end_col, (LANE, SSM_STATE)), exact="b")

        st_ref[0] = state[...]
        ys = []
        for g in range(SSM_GROUPS):
            b_g = bcv[:, SSM_STATE * g:SSM_STATE * (g + 1)]
            c_g = bcv[:, LANE + SSM_STATE * g:LANE + SSM_STATE * (g + 1)]
            gs = slice(GROUP_W * g, GROUP_W * (g + 1))
            cb = _dot_nt(c_g, b_g)
            yd = []
            for hh in range(HEADS_PER_GROUP):
                h = g * HEADS_PER_GROUP + hh
                w = cb * _decay_matrix(a_cs, a_cs_t, h, lower) * dt_t[h:h + 1, :]
                yd.append(_dot(w, xc[:, SSM_HEAD_DIM * h:SSM_HEAD_DIM * (h + 1)]))
            h_g = state[gs, :]
            y_off = _dot_nt(c_g, h_g) * es_full[:, gs]
            ys.append(jnp.concatenate(yd, axis=1) + y_off)
            new_state = _dot_tn(xc[:, gs] * wx_full[:, gs], b_g)
            state[gs, :] = h_g * dec_full[gs, :] + new_state
        y_pre = jnp.concatenate(ys, axis=1) + xc * dexp_ref[...]
        ypre_ref[...] = y_pre
        z = z_ref[...]
        gt = y_pre * (z * _sigmoid(z))
        outs = []
        for g in range(SSM_GROUPS):
            gg = gt[:, GROUP_W * g:GROUP_W * (g + 1)]
            r = lax.rsqrt(jnp.mean(gg * gg, -1, keepdims=True) + EPS)
            outs.append(gg * r)
        out_ref[...] = (jnp.concatenate(outs, axis=1) * nw_ref[...]).astype(out_ref.dtype)

    full = lambda s: pl.BlockSpec(s, lambda i: (0,) * len(s))
    rowblk = lambda w, cidx: pl.BlockSpec((BLOCK, w), lambda i: (i, cidx))
    return pl.pallas_call(
        body, grid=(nb,),
        in_specs=[rowblk(SSM_INNER, COL_Z // SSM_INNER), rowblk(SSM_INNER, COL_XS // SSM_INNER),
                  rowblk(2 * LANE, COL_BC // (2 * LANE)), rowblk(LANE, COL_DT // LANE),
                  full((SSM_CONV, SSM_INNER)), full((SSM_CONV, 2 * LANE)), full((1, SSM_INNER)), full((1, 2 * LANE)),
                  full((1, LANE)), full((1, LANE)), full((1, SSM_INNER)), full((1, SSM_INNER)),
                  full((LANE, SSM_INNER)), full((SSM_INNER, LANE))],
        out_specs=[rowblk(SSM_INNER, 0), rowblk(SSM_INNER, 0), rowblk(2 * LANE, 0), rowblk(LANE, 0),
                   rowblk(SSM_INNER, 0), pl.BlockSpec((1, SSM_INNER, SSM_STATE), lambda i: (i, 0, 0))],
        out_shape=[jax.ShapeDtypeStruct((m, SSM_INNER), MXU_DTYPE), jax.ShapeDtypeStruct((m, SSM_INNER), F32),
                   jax.ShapeDtypeStruct((m, 2 * LANE), F32), jax.ShapeDtypeStruct((m, LANE), F32),
                   jax.ShapeDtypeStruct((m, SSM_INNER), F32), jax.ShapeDtypeStruct((nb, SSM_INNER, SSM_STATE), F32)],
        scratch_shapes=[pltpu.VMEM((HALO + BLOCK, SSM_INNER), F32), pltpu.VMEM((HALO + BLOCK, 2 * LANE), F32),
                        pltpu.VMEM((SSM_INNER, SSM_STATE), F32)],
        name=name, compiler_params=_params("arbitrary"),
    )(proj, proj, proj, proj, cw_x, cw_bc, cb_x, cb_bc, dt_bias, a_log, d_exp, norm_w, expand, expand_t)


def ssd_bwd(proj, dmix, pre_x, pre_bc, dt, y_pre, states, cw_x, cw_bc, dt_bias, a_log, d_exp, norm_w,
            carry=(), name="ssd_bwd"):
    m = proj.shape[0]
    nb = m // BLOCK
    expand = _head_expand()
    expand_t = expand.T

    def body(do0_ref, do1_ref, z_ref, xs_ref, xsp_ref, bc_ref, bcp_ref, dtr_ref, prex_ref, prebc_ref, dt_ref, ypre_ref,
             st_ref,
             cwx_ref, cwbc_ref, dtb_ref, alog_ref, dexp_ref, nw_ref, e_ref, et_ref,
             dz_ref, dxs_ref, dbc_ref, ddt_ref, dcwx_ref, dcwbc_ref, dcbx_ref, dcbbc_ref, ddtb_ref, dalog_ref,
             dd_ref, dnw_ref,
             dstate, hnext, tx, tbc, sx, sbc, dlane):
        i = pl.program_id(0)
        c = nb - 1 - i

        @pl.when(i == 0)
        def _():
            dstate[...] = jnp.zeros_like(dstate)
            hnext[...] = jnp.zeros_like(hnext)
            tx[BLOCK:BLOCK + HALO, :] = jnp.zeros((HALO, SSM_INNER), F32)
            tbc[BLOCK:BLOCK + HALO, :] = jnp.zeros((HALO, 2 * LANE), F32)
            dlane[...] = jnp.zeros_like(dlane)
            for r in (dcwx_ref, dcwbc_ref, dcbx_ref, dcbbc_ref, ddtb_ref, dalog_ref, dd_ref, dnw_ref):
                r[...] = jnp.zeros_like(r)

        e = e_ref[...]
        et = et_ref[...]
        pre_x = prex_ref[...]
        pre_bc = prebc_ref[...]
        sig_x = _sigmoid(pre_x)
        sig_bc = _sigmoid(pre_bc)
        xc = pre_x * sig_x
        bcv = pre_bc * sig_bc
        dt = dt_ref[...]
        a, a_cs, lower = _ssd_decays(dt, alog_ref[...])
        a_cs_t = a_cs.T
        es_full = _dot_hi(jnp.exp(a_cs), e)
        ed_full = _dot_hi(jnp.exp(a_cs[BLOCK - 1:BLOCK, :] - a_cs), e)
        dt_full = _dot_hi(dt, e)
        end_col = jnp.exp(a_cs_t[:, BLOCK - 1:BLOCK])
        dec_full = _dot_hi(et, jnp.broadcast_to(end_col, (LANE, SSM_STATE)), exact="b")
        dexp = dexp_ref[...]

        z = z_ref[...]
        zs = _sigmoid(z)
        sz = z * zs
        y_pre = ypre_ref[...]
        gt = y_pre * sz
        do = jnp.concatenate([do0_ref[...], do1_ref[...]], axis=1)
        nw = nw_ref[...]
        dgt = []
        dnw = []
        for g in range(SSM_GROUPS):
            gs = slice(GROUP_W * g, GROUP_W * (g + 1))
            gg = gt[:, gs]
            r = lax.rsqrt(jnp.mean(gg * gg, -1, keepdims=True) + EPS)
            gn = do[:, gs] * nw[:, gs]
            dgt.append(r * gn - gg * ((r * r * r) * jnp.mean(gg * gn, -1, keepdims=True)))
            dnw.append(jnp.sum(do[:, gs] * (gg * r), axis=0, keepdims=True))
        dgt = jnp.concatenate(dgt, axis=1)
        dnw_ref[...] += jnp.concatenate(dnw, axis=1)
        dy = dgt * sz
        dz_ref[...] = dgt * y_pre * (zs * (1.0 + z * (1.0 - zs)))
        dlane[...] += jnp.sum(dy * xc, axis=0, keepdims=True)
        xd = xc * dt_full

        lane_id = lax.broadcasted_iota(jnp.int32, (BLOCK, LANE), 1)
        sub_id = lax.broadcasted_iota(jnp.int32, (LANE, BLOCK), 0)
        ds_to = jnp.zeros((BLOCK, LANE), F32)
        ds_from_t = jnp.zeros((LANE, BLOCK), F32)
        dxd_parts, inter_parts = [], []
        dbs, dcs = [], []
        for g in range(SSM_GROUPS):
            gs = slice(GROUP_W * g, GROUP_W * (g + 1))
            b_g = bcv[:, SSM_STATE * g:SSM_STATE * (g + 1)]
            c_g = bcv[:, LANE + SSM_STATE * g:LANE + SSM_STATE * (g + 1)]
            cb = _dot_nt(c_g, b_g)
            dcb = jnp.zeros((BLOCK, BLOCK), F32)
            dxd_h = []
            for hh in range(HEADS_PER_GROUP):
                h = g * HEADS_PER_GROUP + hh
                hs = slice(SSM_HEAD_DIM * h, SSM_HEAD_DIM * (h + 1))
                lm = _decay_matrix(a_cs, a_cs_t, h, lower)
                dy_h = dy[:, hs]
                gl = _dot_nt(dy_h, xd[:, hs]) * lm
                dcb = dcb + gl
                e_h = gl * cb
                ds_to = ds_to + jnp.where(lane_id == h, jnp.sum(e_h, axis=-1, keepdims=True), 0.0)
                ds_from_t = ds_from_t + jnp.where(sub_id == h, jnp.sum(e_h, axis=0, keepdims=True), 0.0)
                dxd_h.append(_dot_tn(cb * lm, dy_h))
            h_g = st_ref[0, gs, :]
            dh_g = dstate[gs, :]
            dys_g = dy[:, gs] * es_full[:, gs]
            xde_g = xd[:, gs] * ed_full[:, gs]
            dcs.append(_dot(dcb, b_g) + _dot(dys_g, h_g))
            dbs.append(_dot_tn(dcb, c_g) + _dot(xde_g, dh_g))
            y_off = _dot_nt(c_g, h_g) * es_full[:, gs]
            dxd_state = _dot_nt(b_g, dh_g) * ed_full[:, gs]
            inter_parts.append(dy[:, gs] * y_off - xd[:, gs] * dxd_state)
            dxd_parts.append(jnp.concatenate(dxd_h, axis=1) + dxd_state)
            dstate[gs, :] = dh_g * dec_full[gs, :] + _dot_tn(dys_g, c_g)
            if g == 0:
                end_dot = hnext[gs, :] * dh_g
            else:
                end_dot = jnp.concatenate([end_dot, hnext[gs, :] * dh_g], axis=0)
        dxd = jnp.concatenate(dxd_parts, axis=1)
        hnext[...] = st_ref[0]

        ds = ds_to - ds_from_t.T + _dot_hi(jnp.concatenate(inter_parts, axis=1), et)
        ds_end = jnp.sum(_dot_tn_hi(end_dot, et), axis=0, keepdims=True)
        rows_l = lax.broadcasted_iota(jnp.int32, (BLOCK, LANE), 0)
        ds = ds + jnp.where(rows_l == BLOCK - 1, ds_end, 0.0)
        row = lax.broadcasted_iota(jnp.int32, (BLOCK, BLOCK), 0)
        col = lax.broadcasted_iota(jnp.int32, (BLOCK, BLOCK), 1)
        dadt = _dot_hi(col >= row, ds, exact="b")
        ddt = dadt * a + _dot_hi(dxd * xc, et)
        dalog_ref[...] += jnp.sum(dadt * dt, axis=0, keepdims=True) * a
        rows = _row_ids(c * BLOCK, BLOCK, LANE)
        lanes = lax.broadcasted_iota(jnp.int32, (BLOCK, LANE), 1)
        live = jnp.logical_and(rows >= FRONT_PAD, lanes < SSM_HEADS)
        ddt_raw = jnp.where(live, ddt * _sigmoid(dtr_ref[...] + dtb_ref[...]), 0.0)
        ddt_ref[...] = ddt_raw
        ddtb_ref[...] += jnp.sum(ddt_raw, axis=0, keepdims=True)

        dxc = dxd * dt_full + dy * dexp
        dpre_x = dxc * (sig_x * (1.0 + pre_x * (1.0 - sig_x)))
        dpre_bc = jnp.concatenate(dbs + dcs, axis=1) * (sig_bc * (1.0 + pre_bc * (1.0 - sig_bc)))
        dcbx_ref[...] += jnp.sum(dpre_x, axis=0, keepdims=True)
        dcbbc_ref[...] += jnp.sum(dpre_bc, axis=0, keepdims=True)
        keep_x = _row_ids(c * BLOCK, BLOCK, SSM_INNER) >= FRONT_PAD
        keep_bc = _row_ids(c * BLOCK, BLOCK, 2 * LANE) >= FRONT_PAD
        prev_live = (c > 0).astype(F32)
        for (dpre, t_ref, s_ref, cur_ref, prv_ref, w_ref, dw_ref, dx_ref, keep) in (
                (dpre_x, tx, sx, xs_ref, xsp_ref, cwx_ref, dcwx_ref, dxs_ref, keep_x),
                (dpre_bc, tbc, sbc, bc_ref, bcp_ref, cwbc_ref, dcwbc_ref, dbc_ref, keep_bc)):
            t_ref[0:BLOCK, :] = dpre
            acc = w_ref[0:1, :] * t_ref[pl.ds(SSM_CONV - 1, BLOCK), :]
            for j in range(1, SSM_CONV):
                acc = acc + w_ref[j:j + 1, :] * t_ref[pl.ds(SSM_CONV - 1 - j, BLOCK), :]
            dx_ref[...] = jnp.where(keep, acc, 0.0)
            t_ref[BLOCK:BLOCK + HALO, :] = dpre[0:HALO, :]
            s_ref[0:HALO, :] = prv_ref[BLOCK - HALO:BLOCK, :] * prev_live
            s_ref[HALO:HALO + BLOCK, :] = cur_ref[...]
            first = HALO - (SSM_CONV - 1)
            for j in range(SSM_CONV):
                dw_ref[j:j + 1, :] += jnp.sum(dpre * s_ref[pl.ds(first + j, BLOCK), :], axis=0, keepdims=True)

        @pl.when(i == nb - 1)
        def _():
            dd_ref[...] = _dot_hi(jnp.broadcast_to(dlane[...], (HALO, SSM_INNER)), et)[0:1, :]

    blk = lambda i: nb - 1 - i
    prv = lambda i: jnp.maximum(nb - 2 - i, 0)
    full = lambda s: pl.BlockSpec(s, lambda i: (0,) * len(s))
    rowblk = lambda w, cidx: pl.BlockSpec((BLOCK, w), lambda i: (blk(i), cidx))
    prvblk = lambda w, cidx: pl.BlockSpec((BLOCK, w), lambda i: (prv(i), cidx))
    body, ex_in, ex_out, ex_shape, ex_scratch = _with_exchange(body, 21, 12, carry, nb)
    outs = pl.pallas_call(
        body, grid=(nb,),
        in_specs=[rowblk(GROUP_W, Q_W // GROUP_W), rowblk(GROUP_W, Q_W // GROUP_W + 1),
                  rowblk(SSM_INNER, COL_Z // SSM_INNER),
                  rowblk(SSM_INNER, COL_XS // SSM_INNER), prvblk(SSM_INNER, COL_XS // SSM_INNER),
                  rowblk(2 * LANE, COL_BC // (2 * LANE)), prvblk(2 * LANE, COL_BC // (2 * LANE)),
                  rowblk(LANE, COL_DT // LANE),
                  rowblk(SSM_INNER, 0), rowblk(2 * LANE, 0), rowblk(LANE, 0), rowblk(SSM_INNER, 0),
                  pl.BlockSpec((1, SSM_INNER, SSM_STATE), lambda i: (blk(i), 0, 0)),
                  full((SSM_CONV, SSM_INNER)), full((SSM_CONV, 2 * LANE)), full((1, LANE)), full((1, LANE)),
                  full((1, SSM_INNER)), full((1, SSM_INNER)), full((LANE, SSM_INNER)), full((SSM_INNER, LANE))] + ex_in,
        out_specs=[rowblk(SSM_INNER, 0), rowblk(SSM_INNER, 0), rowblk(2 * LANE, 0), rowblk(LANE, 0),
                   full((SSM_CONV, SSM_INNER)), full((SSM_CONV, 2 * LANE)), full((1, SSM_INNER)), full((1, 2 * LANE)),
                   full((1, LANE)), full((1, LANE)), full((1, LANE)), full((1, SSM_INNER))] + ex_out,
        out_shape=[jax.ShapeDtypeStruct((m, SSM_INNER), F32), jax.ShapeDtypeStruct((m, SSM_INNER), F32),
                   jax.ShapeDtypeStruct((m, 2 * LANE), F32), jax.ShapeDtypeStruct((m, LANE), F32),
                   jax.ShapeDtypeStruct((SSM_CONV, SSM_INNER), F32), jax.ShapeDtypeStruct((SSM_CONV, 2 * LANE), F32),
                   jax.ShapeDtypeStruct((1, SSM_INNER), F32), jax.ShapeDtypeStruct((1, 2 * LANE), F32),
                   jax.ShapeDtypeStruct((1, LANE), F32), jax.ShapeDtypeStruct((1, LANE), F32),
                   jax.ShapeDtypeStruct((1, LANE), F32), jax.ShapeDtypeStruct((1, SSM_INNER), F32)] + ex_shape,
        scratch_shapes=[pltpu.VMEM((SSM_INNER, SSM_STATE), F32), pltpu.VMEM((SSM_INNER, SSM_STATE), F32),
                        pltpu.VMEM((BLOCK + HALO, SSM_INNER), F32), pltpu.VMEM((BLOCK + HALO, 2 * LANE), F32),
                        pltpu.VMEM((HALO + BLOCK, SSM_INNER), F32), pltpu.VMEM((HALO + BLOCK, 2 * LANE), F32),
                        pltpu.VMEM((1, SSM_INNER), F32)] + ex_scratch,
        name=name, compiler_params=_params("arbitrary"),
    )(dmix, dmix, proj, proj, proj, proj, proj, proj, pre_x, pre_bc, dt, y_pre, states,
      cw_x, cw_bc, dt_bias, a_log, d_exp, norm_w, expand, expand_t, *carry)
    return outs[:12], outs[12:]


CONF_HALO = 32
SUBLANES = 8


def _for_each_window(s, offsets, rows, fn):
    total = s.shape[0]
    assert max(offsets) + rows <= total
    for b in range(SUBLANES):
        offs = [o for o in offsets if o % SUBLANES == b]
        if not offs:
            continue
        rot = s if b == 0 else pltpu.roll(s, total - b, 0)
        for o in offs:
            fn(o, rot[o - b:o - b + rows])


def _glu_masked(v, first_row):
    a = v[:, :D_MODEL]
    s = _sigmoid(v[:, D_MODEL:])
    rows = _row_ids(first_row, v.shape[0], D_MODEL)
    return jnp.where(rows >= FRONT_PAD, a * s, 0.0), a, s


def _layer_norm_stats(c):
    mu = jnp.mean(c, -1, keepdims=True)
    xc = c - mu
    rstd = lax.rsqrt(jnp.mean(xc * xc, -1, keepdims=True) + LN_EPS)
    return xc * rstd, rstd


def conformer_mid_fwd(v, dw_w, dw_b, ln_g, ln_b, name="conf_mid_fwd"):
    m = v.shape[0]
    nb = m // BLOCK
    kpad = dw_w.shape[0]

    def body(vc_ref, vp_ref, w_ref, b_ref, g_ref, beta_ref, c_ref, s_ref):
        i = pl.program_id(0)
        g_prev, _, _ = _glu_masked(vp_ref[BLOCK - CONF_HALO:BLOCK, :], (i - 1) * BLOCK + BLOCK - CONF_HALO)
        g_cur, _, _ = _glu_masked(vc_ref[...], i * BLOCK)
        sg = jnp.concatenate([g_prev * (i > 0).astype(F32), g_cur], axis=0)
        first = CONF_HALO - (CONF_KERNEL - 1)
        acc = [jnp.broadcast_to(b_ref[...], (BLOCK, D_MODEL))]

        def tap(off, win):
            j = off - first
            acc[0] = acc[0] + w_ref[j:j + 1, :] * win

        _for_each_window(sg, [first + j for j in range(CONF_KERNEL)], BLOCK, tap)
        acc = acc[0]
        c_ref[...] = acc
        xhat, _ = _layer_norm_stats(acc)
        nrm = xhat * g_ref[...] + beta_ref[...]
        s_ref[...] = (nrm * _sigmoid(nrm)).astype(s_ref.dtype)

    full = lambda s: pl.BlockSpec(s, lambda i: (0,) * len(s))
    return pl.pallas_call(
        body, grid=(nb,),
        in_specs=[pl.BlockSpec((BLOCK, 2 * D_MODEL), lambda i: (i, 0)),
                  pl.BlockSpec((BLOCK, 2 * D_MODEL), lambda i: (jnp.maximum(i - 1, 0), 0)),
                  full((kpad, D_MODEL)), full((1, D_MODEL)), full((1, D_MODEL)), full((1, D_MODEL))],
        out_specs=[pl.BlockSpec((BLOCK, D_MODEL), lambda i: (i, 0)), pl.BlockSpec((BLOCK, D_MODEL), lambda i: (i, 0))],
        out_shape=[jax.ShapeDtypeStruct((m, D_MODEL), F32), jax.ShapeDtypeStruct((m, D_MODEL), MXU_DTYPE)],
        name=name, compiler_params=_params("arbitrary"),
    )(v, v, dw_w, dw_b, ln_g, ln_b)


def conformer_ln_bwd(ds, c, ln_g, ln_b, name="conf_ln_bwd"):
    m, d = c.shape
    tm = ROW_TILE

    def body(ds_ref, c_ref, g_ref, beta_ref, dc_ref, dg_ref, db_ref):
        @pl.when(pl.program_id(0) == 0)
        def _():
            dg_ref[...] = jnp.zeros_like(dg_ref)
            db_ref[...] = jnp.zeros_like(db_ref)

        xhat, rstd = _layer_norm_stats(c_ref[...])
        g = g_ref[...]
        nrm = xhat * g + beta_ref[...]
        sg = _sigmoid(nrm)
        dn = ds_ref[...] * (sg * (1.0 + nrm * (1.0 - sg)))
        db_ref[...] += jnp.sum(dn, axis=0, keepdims=True)
        dg_ref[...] += jnp.sum(dn * xhat, axis=0, keepdims=True)
        dx = dn * g
        dc_ref[...] = rstd * (dx - jnp.mean(dx, -1, keepdims=True) - xhat * jnp.mean(dx * xhat, -1, keepdims=True))

    row = pl.BlockSpec((tm, d), lambda i: (i, 0))
    vec = pl.BlockSpec((1, d), lambda i: (0, 0))
    return pl.pallas_call(
        body, grid=(m // tm,), in_specs=[row, row, vec, vec], out_specs=[row, vec, vec],
        out_shape=[jax.ShapeDtypeStruct((m, d), F32), jax.ShapeDtypeStruct((1, d), F32), jax.ShapeDtypeStruct((1, d), F32)],
        name=name, compiler_params=_params("arbitrary"),
    )(ds, c, ln_g, ln_b)


def conformer_conv_bwd(dc, v, dw_w, carry=(), name="conf_conv_bwd"):
    m = v.shape[0]
    nb = m // BLOCK
    kpad = dw_w.shape[0]

    def body(dcc_ref, dcn_ref, vc_ref, vp_ref, w_ref, dv_ref, dw_ref, db_ref, dvb_ref):
        i = pl.program_id(0)

        @pl.when(i == 0)
        def _():
            dw_ref[...] = jnp.zeros_like(dw_ref)
            db_ref[...] = jnp.zeros_like(db_ref)
            dvb_ref[...] = jnp.zeros_like(dvb_ref)

        dc_cur = dcc_ref[...]
        tg = jnp.concatenate([dc_cur, dcn_ref[0:CONF_HALO, :] * (i < nb - 1).astype(F32)], axis=0)
        g_prev, _, _ = _glu_masked(vp_ref[BLOCK - CONF_HALO:BLOCK, :], (i - 1) * BLOCK + BLOCK - CONF_HALO)
        g_cur, a, s = _glu_masked(vc_ref[...], i * BLOCK)
        sg = jnp.concatenate([g_prev * (i > 0).astype(F32), g_cur], axis=0)
        db_ref[...] += jnp.sum(dc_cur, axis=0, keepdims=True)
        first = CONF_HALO - (CONF_KERNEL - 1)
        dg_acc = [jnp.zeros((BLOCK, D_MODEL), F32)]

        def tap_dg(off, win):
            j = CONF_KERNEL - 1 - off
            dg_acc[0] = dg_acc[0] + w_ref[j:j + 1, :] * win

        def tap_dw(off, win):
            j = off - first
            dw_ref[j:j + 1, :] += jnp.sum(dc_cur * win, axis=0, keepdims=True)

        _for_each_window(tg, list(range(CONF_KERNEL)), BLOCK, tap_dg)
        _for_each_window(sg, [first + j for j in range(CONF_KERNEL)], BLOCK, tap_dw)
        dg = dg_acc[0]
        rows = _row_ids(i * BLOCK, BLOCK, D_MODEL)
        dg = jnp.where(rows >= FRONT_PAD, dg, 0.0)
        da = dg * s
        dbv = dg * a * (s * (1.0 - s))
        dv = jnp.concatenate([da, dbv], axis=1)
        dv_ref[...] = dv.astype(dv_ref.dtype)
        dvb_ref[...] += jnp.sum(dv, axis=0, keepdims=True)

    full = lambda s: pl.BlockSpec(s, lambda i: (0,) * len(s))
    body, ex_in, ex_out, ex_shape, ex_scratch = _with_exchange(body, 5, 4, carry, nb)
    outs = pl.pallas_call(
        body, grid=(nb,),
        in_specs=[pl.BlockSpec((BLOCK, D_MODEL), lambda i: (i, 0)),
                  pl.BlockSpec((BLOCK, D_MODEL), lambda i: (jnp.minimum(i + 1, nb - 1), 0)),
                  pl.BlockSpec((BLOCK, 2 * D_MODEL), lambda i: (i, 0)),
                  pl.BlockSpec((BLOCK, 2 * D_MODEL), lambda i: (jnp.maximum(i - 1, 0), 0)),
                  full((kpad, D_MODEL))] + ex_in,
        out_specs=[pl.BlockSpec((BLOCK, 2 * D_MODEL), lambda i: (i, 0)), full((kpad, D_MODEL)),
                   full((1, D_MODEL)), full((1, 2 * D_MODEL))] + ex_out,
        out_shape=[jax.ShapeDtypeStruct((m, 2 * D_MODEL), MXU_DTYPE), jax.ShapeDtypeStruct((kpad, D_MODEL), F32),
                   jax.ShapeDtypeStruct((1, D_MODEL), F32), jax.ShapeDtypeStruct((1, 2 * D_MODEL), F32)] + ex_shape,
        scratch_shapes=ex_scratch, name=name, compiler_params=_params("arbitrary"),
    )(dc, dc, v, v, dw_w, *carry)
    return outs[:4], outs[4:]


def _row(v, width=None):
    v = v.reshape(1, -1).astype(F32)
    if width is not None and v.shape[1] < width:
        v = jnp.pad(v, ((0, 0), (0, width - v.shape[1])))
    return v


def _w_in_to_kernel(w):
    pad = jnp.zeros((w.shape[0], PROJ_W - COL_DT - SSM_HEADS), w.dtype)
    return jnp.concatenate([w[:, 768:1792], w[:, 1792:2816], w[:, 0:512], w[:, 2816:3072], w[:, 512:640],
                            w[:, 640:768], w[:, 3072:3088], pad], axis=1)


def _w_in_from_kernel(g):
    return jnp.concatenate([g[:, COL_Q:COL_Q + Q_W], g[:, COL_K:COL_K + KV_W], g[:, COL_V:COL_V + KV_W],
                            g[:, COL_Z:COL_Z + SSM_INNER], g[:, COL_XS:COL_XS + SSM_INNER],
                            g[:, COL_BC:COL_BC + 2 * LANE], g[:, COL_DT:COL_DT + SSM_HEADS]], axis=1)


def even_fwd(h, p):
    u = rms_fwd(h, p["norm"])
    proj = matmul(u, p["w_in"], name="mm_proj")
    att = attention_fwd(proj, p["q_norm"], p["k_norm"], p["sinks"])
    ssm, pre_x, pre_bc, dt, y_pre, states = ssd_fwd(proj, p["cw_x"], p["cw_bc"], p["cb_x"], p["cb_bc"], p["dt_bias"],
                                                    p["a_log"], p["d_exp"], p["ssm_norm"])
    mix = jnp.concatenate([att, ssm], axis=1)
    out = matmul(mix, p["w_out"], b_kind="rowshard", layer=p["layer"], epilogue="resid", extra=h, name="mm_mix_out")
    return out, (h, u, proj, mix, pre_x, pre_bc, dt, y_pre, states)


def even_bwd(dh, p, saved, carry_att=(), carry_ssd=()):
    h, u, proj, mix, pre_x, pre_bc, dt, y_pre, states = saved
    dmix = matmul(dh, p["w_out"], b_kind="rowshard", layer=p["layer"], trans_b=True, name="mm_dmix")
    dw_out = matmul_tn(mix, dh, ti=512, tn=D_MODEL, out_dtype=GRAD_WIRE_DTYPE, name="mm_dw_out")
    dw_out = dw_out.reshape(N_DEV, MIX_W // N_DEV, D_MODEL)
    (dq, dk, dv, dqw, dkw, dsk), got_att = attention_bwd(proj, dmix, p["q_norm"], p["k_norm"], p["sinks"],
                                                         carry=list(carry_att))
    (dz, dxs, dbc, ddt, dcwx, dcwbc, dcbx, dcbbc, ddtb, dalog, dd, dnw), got_ssd = ssd_bwd(
        proj, dmix, pre_x, pre_bc, dt, y_pre, states, p["cw_x"], p["cw_bc"], p["dt_bias"], p["a_log"], p["d_exp"],
        p["ssm_norm"], carry=[dw_out] + list(carry_ssd))
    dproj = jnp.concatenate([dz, dxs, dq, dbc, dk, dv, ddt], axis=1).astype(MXU_DTYPE)
    du = matmul(dproj, p["w_in"], trans_b=True, name="mm_du_in")
    dw_in = matmul_tn(u, dproj, ti=512, tn=PROJ_W, name="mm_dw_in")
    dw_in = _to_shards(_w_in_from_kernel(dw_in), 1).astype(GRAD_WIRE_DTYPE)
    dh_in, dg = rms_bwd(h, p["norm"], du, dh)
    grads = dict(norm=dg, w_in=dw_in, cw_x=dcwx, cw_bc=dcwbc, cb_x=dcbx, cb_bc=dcbbc, dt_bias=ddtb,
                 a_log=dalog, d_skip=dd, ssm_norm=dnw, q_norm=dqw, k_norm=dkw, sinks=dsk)
    return dh_in, grads, got_att, got_ssd


def conf_fwd(h, p):
    u = rms_fwd(h, p["norm"])
    v = mlp_up(u, p["pw1_w"], p["layer"], bias=p["pw1_b"], relu2=False, out_dtype=F32, name="mm_pw1")
    c, s = conformer_mid_fwd(v, p["dw_w"], p["dw_b"], p["ln_g"], p["ln_b"])
    out = matmul(s, p["pw2_w"], b_kind="rowshard", layer=p["layer"], bias=p["pw2_b"], epilogue="resid", extra=h,
                 name="mm_pw2")
    return out, (h, u, v, c, s)


def conf_bwd(dh, p, saved, carry=()):
    h, u, v, c, s = saved
    dpw2_b = col_sum(dh)
    ds = matmul(dh, p["pw2_w"], b_kind="rowshard", layer=p["layer"], trans_b=True, name="mm_ds")
    dpw2_w = matmul_tn(s, dh, ti=D_MODEL, tn=D_MODEL, out_dtype=GRAD_WIRE_DTYPE, name="mm_dpw2")
    dpw2_w = dpw2_w.reshape(N_DEV, D_MODEL // N_DEV, D_MODEL)
    dc, dln_g, dln_b = conformer_ln_bwd(ds, c, p["ln_g"], p["ln_b"])
    (dv, ddw_w, ddw_b, dpw1_b), got = conformer_conv_bwd(dc, v, p["dw_w"], carry=[dpw2_w] + list(carry))
    dpw1_w = mlp_dw_up(u, dv, name="mm_dpw1")
    dh_in, dg = mlp_du_rms_bwd(dv, p["pw1_w"], p["layer"], h, p["norm"], dh, name="mm_du_pw1")
    grads = dict(norm=dg, pw1_w=dpw1_w, pw1_b=dpw1_b, dw_w=ddw_w, dw_b=ddw_b, ln_g=dln_g, ln_b=dln_b, pw2_b=dpw2_b)
    return dh_in, grads, got


def mlp_fwd(h, p):
    u = rms_fwd(h, p["norm"])
    act = mlp_up(u, p["w_up"], p["layer"])
    out = matmul(act, p["w_down"], b_kind="rowshard", layer=p["layer"], epilogue="resid", extra=h, name="mm_down")
    return out, (h, u, act)


def mlp_bwd(dh, p, saved):
    h, u, act = saved
    da = mlp_dact(dh, p["w_down"], act, p["layer"])
    dw_down = mlp_dw_down(act, dh).reshape(N_DEV, FF_BLOCK, D_MODEL)
    dw_up = mlp_dw_up(u, da)
    dh_in, dg = mlp_du_rms_bwd(da, p["w_up"], p["layer"], h, p["norm"], dh)
    return dh_in, dict(norm=dg, w_up=dw_up, w_down=dw_down)


def local_step(x, target, w):
    n_even, n_odd = (DEPTH + 1) // 2, DEPTH // 2
    h = jnp.concatenate([jnp.zeros((FRONT_PAD, D_MODEL), F32), w["meta_tokens"].astype(F32), x], axis=0)
    even_p, odd_p, mlp_p = [], [], []
    for i in range(n_even):
        cw = w["ssm_conv_w"][i]
        even_p.append(dict(
            layer=i, norm=_row(w["mix_norm_even"][i]), w_in=_w_in_to_kernel(_from_shards(w["w_in"][:, i], 1)),
            w_out=w["w_out"],
            cw_x=cw[:, :SSM_INNER], cw_bc=cw[:, SSM_INNER:], cb_x=_row(w["ssm_conv_b"][i][:SSM_INNER]),
            cb_bc=_row(w["ssm_conv_b"][i][SSM_INNER:]), dt_bias=_row(w["dt_bias"][i], LANE),
            a_log=_row(w["a_log"][i], LANE), d_exp=_row(jnp.repeat(w["d_skip"][i], SSM_HEAD_DIM)),
            ssm_norm=_row(w["ssm_norm_w"][i]), q_norm=_row(jnp.tile(w["q_norm"][i], ATT_HEADS)),
            k_norm=_row(jnp.tile(w["k_norm"][i], ATT_KV_HEADS)), sinks=w["sinks"][i].astype(F32)))
    for i in range(n_odd):
        odd_p.append(dict(
            layer=i, norm=_row(w["mix_norm_odd"][i]), pw1_w=w["pw1_w"], pw1_b=_row(w["pw1_b"][i]),
            dw_w=jnp.pad(w["dw_w"][i], ((0, CONF_HALO - CONF_KERNEL), (0, 0))), dw_b=_row(w["dw_b"][i]),
            ln_g=_row(w["ln_g"][i]), ln_b=_row(w["ln_b"][i]), pw2_w=w["pw2_w"], pw2_b=_row(w["pw2_b"][i])))
    for layer in range(DEPTH):
        mlp_p.append(dict(layer=layer, norm=_row(w["mlp_norm"][layer]), w_up=w["w_up"], w_down=w["w_down"]))

    tape = []
    for layer in range(DEPTH):
        if layer % 2 == 0:
            h, saved = even_fwd(h, even_p[layer // 2])
        else:
            h, saved = conf_fwd(h, odd_p[layer // 2])
        tape.append(saved)
        h, saved = mlp_fwd(h, mlp_p[layer])
        tape.append(saved)
    dh, loss_row = loss_fwd_bwd(h, target)

    ge = [None] * n_even
    go = [None] * n_odd
    gm = [None] * DEPTH
    received = {n: [None] * shape[0] for n, shape, _ in PARAMS if n in MATMUL_WEIGHTS}
    pending = []

    def store(tags, arrays):
        for (n, l), a in zip(tags, arrays):
            received[n][l] = a

    for layer in reversed(range(DEPTH)):
        i = layer // 2
        dh, gm[layer] = mlp_bwd(dh, mlp_p[layer], tape.pop())
        mlp_tags = [("w_up", layer), ("w_down", layer)]
        mlp_parts = [gm[layer]["w_up"], gm[layer]["w_down"]]
        if layer % 2 == 0:
            riders, pending = pending, []
            dh, ge[i], got_att, got_ssd = even_bwd(dh, even_p[i], tape.pop(), carry_att=mlp_parts,
                                                   carry_ssd=[a for _, _, a in riders])
            store(mlp_tags, got_att)
            store([("w_out", i)] + [(n, l) for n, l, _ in riders], got_ssd)
            pending.append(("w_in", i, ge[i]["w_in"]))
        else:
            dh, go[i], got = conf_bwd(dh, odd_p[i], tape.pop(), carry=mlp_parts)
            store([("pw2_w", i)] + mlp_tags, got)
            pending.append(("pw1_w", i, go[i]["pw1_w"]))

    stack = lambda gs, f: jnp.stack([f(g) for g in gs])
    grads = dict(
        meta_tokens=dh[FRONT_PAD:BLOCK],
        mix_norm_even=stack(ge, lambda g: g["norm"][0]),
        ssm_conv_w=stack(ge, lambda g: jnp.concatenate([g["cw_x"], g["cw_bc"]], axis=1)),
        ssm_conv_b=stack(ge, lambda g: jnp.concatenate([g["cb_x"][0], g["cb_bc"][0]])),
        dt_bias=stack(ge, lambda g: g["dt_bias"][0, :SSM_HEADS]),
        a_log=stack(ge, lambda g: g["a_log"][0, :SSM_HEADS]),
        d_skip=stack(ge, lambda g: g["d_skip"][0, :SSM_HEADS]),
        ssm_norm_w=stack(ge, lambda g: g["ssm_norm"][0]),
        q_norm=stack(ge, lambda g: g["q_norm"][0, :HEAD_DIM]),
        k_norm=stack(ge, lambda g: g["k_norm"][0, :HEAD_DIM]),
        sinks=stack(ge, lambda g: g["sinks"][0, :ATT_HEADS]),
        mix_norm_odd=stack(go, lambda g: g["norm"][0]),
        pw1_b=stack(go, lambda g: g["pw1_b"][0]),
        dw_w=stack(go, lambda g: g["dw_w"][:CONF_KERNEL]),
        dw_b=stack(go, lambda g: g["dw_b"][0]),
        ln_g=stack(go, lambda g: g["ln_g"][0]),
        ln_b=stack(go, lambda g: g["ln_b"][0]),
        pw2_b=stack(go, lambda g: g["pw2_b"][0]),
        mlp_norm=stack(gm, lambda g: g["norm"][0]),
    )
    return loss_row[0, 0], dh[BLOCK:], grads, received, pending


PARAMS = (
    ("meta_tokens", (16, 1024), 1), ("mix_norm_even", (2, 1024), None), ("w_in", (2, 1024, 3088), 2),
    ("ssm_conv_w", (2, 4, 1280), 2), ("ssm_conv_b", (2, 1280), None), ("dt_bias", (2, 16), None),
    ("a_log", (2, 16), None), ("d_skip", (2, 16), None), ("ssm_norm_w", (2, 1024), None), ("q_norm", (2, 64), None),
    ("k_norm", (2, 64), None), ("sinks", (2, 8), None), ("w_out", (2, 1536, 1024), 1), ("mix_norm_odd", (2, 1024), 1),
    ("pw1_w", (2, 1024, 2048), 2), ("pw1_b", (2, 2048), 1), ("dw_w", (2, 31, 1024), 2), ("dw_b", (2, 1024), 1),
    ("ln_g", (2, 1024), 1), ("ln_b", (2, 1024), 1), ("pw2_w", (2, 1024, 1024), 1), ("pw2_b", (2, 1024), 1),
    ("mlp_norm", (4, 1024), None), ("w_up", (4, 1024, 4096), 2), ("w_down", (4, 4096, 1024), 1),
)
MATMUL_WEIGHTS = ("w_in", "w_out", "pw1_w", "pw2_w", "w_up", "w_down")
PACK_ROW_ALIGN = 16 * PACK_W


def _block_shape(shape, axis):
    if axis is None:
        return tuple(shape)
    return tuple(s // N_DEV if a == axis else s for a, s in enumerate(shape))


def _numel(shape):
    return math.prod(shape)


def _pack(arrays, dtype):
    flat = jnp.concatenate([a.reshape(-1).astype(dtype) for a in arrays])
    n = flat.shape[0]
    padded = -(-n // PACK_ROW_ALIGN) * PACK_ROW_ALIGN
    return jnp.pad(flat, (0, padded - n)).reshape(-1, PACK_W)


def _pack_rows(arrays_by_dev, dtype):
    flat = jnp.concatenate([a.reshape(N_DEV, -1).astype(dtype) for a in arrays_by_dev], axis=1)
    n = flat.shape[1]
    padded = -(-n // PACK_ROW_ALIGN) * PACK_ROW_ALIGN
    return jnp.pad(flat, ((0, 0), (0, padded - n))).reshape(N_DEV, -1, PACK_W)


def _to_shards(full, axis):
    shape = full.shape
    split = full.reshape(shape[:axis] + (N_DEV, shape[axis] // N_DEV) + shape[axis + 1:])
    return jnp.moveaxis(split, axis, 0)


def _from_shards(blocks, axis):
    moved = jnp.moveaxis(blocks, 0, axis)
    shape = moved.shape
    return moved.reshape(shape[:axis] + (shape[axis] * shape[axis + 1],) + shape[axis + 2:])


_MESH = pl.DeviceIdType.MESH
_ANY = pl.BlockSpec(memory_space=pl.ANY)


def _mesh_place():
    x, y, c = lax.axis_index("x"), lax.axis_index("y"), lax.axis_index("c")
    return x, y, c


def _peer(x, y, c, rel):
    dx, dy, dc = (rel >> 2) & 1, (rel >> 1) & 1, rel & 1
    return (x ^ dx if dx else x, y ^ dy if dy else y, c ^ dc if dc else c)


def _dev_index(x, y, c):
    return 4 * x + 2 * y + c


def all_gather_weights(bigs, small):
    nt = len(bigs)

    def body(*refs):
        big_refs, small_ref = refs[:nt], refs[nt]
        big_outs, small_out = refs[nt + 1:2 * nt + 1], refs[2 * nt + 1]
        send_sems, recv_sems, small_send, small_recv, local_sems = refs[2 * nt + 2:]
        x, y, c = _mesh_place()
        me = (x, y, c)
        sibling = (x, y, 1 - c)
        chips = [(1 - x, y), (x, 1 - y), (1 - x, 1 - y)]

        def big_copy(t, k, block, to, from_input=False):
            dst = big_outs[t].at[_dev_index(*block)]
            return pltpu.make_async_remote_copy(src_ref=big_refs[t] if from_input else dst, dst_ref=dst,
                                                send_sem=send_sems.at[t, k], recv_sem=recv_sems.at[t, k],
                                                device_id=to, device_id_type=_MESH)

        def small_copy(rel, block, to):
            return pltpu.make_async_remote_copy(src_ref=small_ref, dst_ref=small_out.at[_dev_index(*block)],
                                                send_sem=small_send.at[rel - 1], recv_sem=small_recv.at[rel - 1],
                                                device_id=to, device_id_type=_MESH)

        mine = [pltpu.make_async_copy(big_refs[t], big_outs[t].at[_dev_index(*me)], local_sems.at[t]) for t in range(nt)]
        mine.append(pltpu.make_async_copy(small_ref, small_out.at[_dev_index(*me)], local_sems.at[nt]))
        for cp in mine:
            cp.start()
        first = []
        for t in range(nt):
            first.append(big_copy(t, 0, me, sibling, from_input=True))
            first += [big_copy(t, 1 + j, me, (*chip, c), from_input=True) for j, chip in enumerate(chips)]
        for cp in first:
            cp.start()
        smalls = [small_copy(rel, me, _peer(x, y, c, rel)) for rel in range(1, N_DEV)]
        for cp in smalls:
            cp.start()
        passed = []
        for j, chip in enumerate(chips):
            for t in range(nt):
                big_copy(t, 1 + j, (*chip, c), me).wait_recv()
                fwd = big_copy(t, 4 + j, (*chip, c), sibling)
                fwd.start()
                passed.append(fwd)
        for t in range(nt):
            big_copy(t, 0, sibling, me).wait_recv()
            for j, chip in enumerate(chips):
                big_copy(t, 4 + j, (*chip, 1 - c), me).wait_recv()
        for rel in range(1, N_DEV):
            small_copy(rel, _peer(x, y, c, rel), me).wait_recv()
        for cp in first + passed + smalls:
            cp.wait_send()
        for cp in mine:
            cp.wait()

    return pl.pallas_call(
        body, in_specs=[_ANY] * (nt + 1), out_specs=[_ANY] * (nt + 1),
        out_shape=[jax.ShapeDtypeStruct((N_DEV,) + b.shape, b.dtype) for b in bigs]
        + [jax.ShapeDtypeStruct((N_DEV,) + small.shape, small.dtype)],
        scratch_shapes=[pltpu.SemaphoreType.DMA((nt, N_DEV - 1)), pltpu.SemaphoreType.DMA((nt, N_DEV - 1)),
                        pltpu.SemaphoreType.DMA((N_DEV - 1,)), pltpu.SemaphoreType.DMA((N_DEV - 1,)),
                        pltpu.SemaphoreType.DMA((nt + 1,))],
        name="all_gather_weights",
    )(*bigs, small)


def _exchange_copies(in_refs, out_refs, send_sems, recv_sems, local_sems):
    x, y, c = _mesh_place()
    me = _dev_index(x, y, c)
    mine, sends, arrivals = [], [], []
    for p, (src, dst) in enumerate(zip(in_refs, out_refs)):
        mine.append(pltpu.make_async_copy(src.at[me], dst.at[me], local_sems.at[p]))
        for rel in range(1, N_DEV):
            peer = _peer(x, y, c, rel)
            there = _dev_index(*peer)
            sems = dict(send_sem=send_sems.at[rel - 1, p], recv_sem=recv_sems.at[rel - 1, p], device_id=peer,
                        device_id_type=_MESH)
            sends.append(pltpu.make_async_remote_copy(src_ref=src.at[there], dst_ref=dst.at[me], **sems))
            arrivals.append(pltpu.make_async_remote_copy(src_ref=src.at[me], dst_ref=dst.at[there], **sems))
    return mine, sends, arrivals


def _with_exchange(body, n_in, n_out, carry, steps):
    n = len(carry)
    if n == 0:
        return body, [], [], [], []

    def wrapped(*refs):
        ins, ex_in = refs[:n_in], refs[n_in:n_in + n]
        outs, ex_out = refs[n_in + n:n_in + n + n_out], refs[n_in + n + n_out:n_in + 2 * n + n_out]
        scratch = refs[n_in + 2 * n + n_out:len(refs) - 3]
        send_sems, recv_sems, local_sems = refs[len(refs) - 3:]
        i = pl.program_id(0)

        @pl.when(i == 0)
        def _():
            mine, sends, _ = _exchange_copies(ex_in, ex_out, send_sems, recv_sems, local_sems)
            for cp in mine + sends:
                cp.start()

        body(*ins, *outs, *scratch)

        @pl.when(i == steps - 1)
        def _():
            mine, sends, arrivals = _exchange_copies(ex_in, ex_out, send_sems, recv_sems, local_sems)
            for cp in arrivals:
                cp.wait_recv()
            for cp in sends:
                cp.wait_send()
            for cp in mine:
                cp.wait()

    return (wrapped, [_ANY] * n, [_ANY] * n, [jax.ShapeDtypeStruct(a.shape, a.dtype) for a in carry],
            [pltpu.SemaphoreType.DMA((N_DEV - 1, n)), pltpu.SemaphoreType.DMA((N_DEV - 1, n)),
             pltpu.SemaphoreType.DMA((n,))])


def exchange_gradients(arrays):
    n = len(arrays)

    def body(*refs):
        mine, sends, arrivals = _exchange_copies(refs[:n], refs[n:2 * n], *refs[2 * n:])
        for cp in mine + sends:
            cp.start()
        for cp in arrivals:
            cp.wait_recv()
        for cp in sends:
            cp.wait_send()
        for cp in mine:
            cp.wait()

    return pl.pallas_call(
        body, in_specs=[_ANY] * n, out_specs=[_ANY] * n,
        out_shape=[jax.ShapeDtypeStruct(a.shape, a.dtype) for a in arrays],
        scratch_shapes=[pltpu.SemaphoreType.DMA((N_DEV - 1, n)), pltpu.SemaphoreType.DMA((N_DEV - 1, n)),
                        pltpu.SemaphoreType.DMA((n,))],
        name="exchange_gradients",
    )(*arrays)


def reduce_adamw(parts, w, m, v, tr):
    nl, r, cols = w.shape
    assert len(parts) == nl

    def body(*refs):
        p_refs = refs[:nl]
        w_ref, m_ref, v_ref, g_ref, d_ref, nm_ref, nv_ref, g_acc = refs[nl:]
        layer = pl.program_id(0)
        for l in range(nl):
            @pl.when(layer == l)
            def _(l=l):
                g = p_refs[l][0].astype(F32)
                for d in range(1, N_DEV):
                    g = g + p_refs[l][d].astype(F32)
                g_acc[...] = g

        g = g_acc[...]
        g_ref[...] = g
        nm = ADAM_B1 * m_ref[...] + (1.0 - ADAM_B1) * g
        nv = ADAM_B2 * v_ref[...] + (1.0 - ADAM_B2) * (g * g)
        nm_ref[...] = nm
        nv_ref[...] = nv
        m_hat = nm / (1.0 - ADAM_B1 ** ADAM_STEP)
        v_hat = nv / (1.0 - ADAM_B2 ** ADAM_STEP)
        d_ref[...] = -ADAM_LR * (m_hat / (jnp.sqrt(v_hat) + ADAM_EPS) + ADAM_WD * w_ref[...])

    row = pl.BlockSpec((None, tr, cols), lambda l, i: (l, i, 0))

    def part_spec(own):
        def index(l, i):
            return (0, jnp.where(l == own, i, jnp.where(l < own, 0, r // tr - 1)), 0)
        return pl.BlockSpec((N_DEV, tr, cols), index)

    return pl.pallas_call(
        body, grid=(nl, r // tr),
        in_specs=[part_spec(l) for l in range(nl)] + [row, row, row],
        out_specs=[row, row, row, row], out_shape=[jax.ShapeDtypeStruct((nl, r, cols), F32)] * 4,
        scratch_shapes=[pltpu.VMEM((tr, cols), F32)],
        name="reduce_adamw", compiler_params=_params("arbitrary", "arbitrary"),
    )(*parts, w, m, v)


ADAMW_TILE_BYTES = 1 << 19


def _adamw_tile(rows, cols):
    lanes = -(-cols // LANE) * LANE
    best = None
    for tr in range(16, rows + 1, 16):
        if rows % tr == 0 and tr * lanes * 4 <= ADAMW_TILE_BYTES:
            best = tr
    if best is None:
        raise ValueError((rows, cols))
    return best


def kernel(x, meta_tokens, mix_norm_even, w_in, ssm_conv_w, ssm_conv_b, dt_bias, a_log, d_skip, ssm_norm_w, q_norm, k_norm, sinks, w_out, mix_norm_odd, pw1_w, pw1_b, dw_w, dw_b, ln_g, ln_b, pw2_w, pw2_b, mlp_norm, w_up, w_down, loss_target, m_meta_tokens, m_mix_norm_even, m_w_in, m_ssm_conv_w, m_ssm_conv_b, m_dt_bias, m_a_log, m_d_skip, m_ssm_norm_w, m_q_norm, m_k_norm, m_sinks, m_w_out, m_mix_norm_odd, m_pw1_w, m_pw1_b, m_dw_w, m_dw_b, m_ln_g, m_ln_b, m_pw2_w, m_pw2_b, m_mlp_norm, m_w_up, m_w_down, v_meta_tokens, v_mix_norm_even, v_w_in, v_ssm_conv_w, v_ssm_conv_b, v_dt_bias, v_a_log, v_d_skip, v_ssm_norm_w, v_q_norm, v_k_norm, v_sinks, v_w_out, v_mix_norm_odd, v_pw1_w, v_pw1_b, v_dw_w, v_dw_b, v_ln_g, v_ln_b, v_pw2_w, v_pw2_b, v_mlp_norm, v_w_up, v_w_down):
    names = [p[0] for p in PARAMS]
    w_loc = dict(zip(names, (meta_tokens, mix_norm_even, w_in, ssm_conv_w, ssm_conv_b, dt_bias, a_log, d_skip, ssm_norm_w, q_norm, k_norm, sinks, w_out, mix_norm_odd, pw1_w, pw1_b, dw_w, dw_b, ln_g, ln_b, pw2_w, pw2_b, mlp_norm, w_up, w_down)))
    m_loc = dict(zip(names, (m_meta_tokens, m_mix_norm_even, m_w_in, m_ssm_conv_w, m_ssm_conv_b, m_dt_bias, m_a_log, m_d_skip, m_ssm_norm_w, m_q_norm, m_k_norm, m_sinks, m_w_out, m_mix_norm_odd, m_pw1_w, m_pw1_b, m_dw_w, m_dw_b, m_ln_g, m_ln_b, m_pw2_w, m_pw2_b, m_mlp_norm, m_w_up, m_w_down)))
    v_loc = dict(zip(names, (v_meta_tokens, v_mix_norm_even, v_w_in, v_ssm_conv_w, v_ssm_conv_b, v_dt_bias, v_a_log, v_d_skip, v_ssm_norm_w, v_q_norm, v_k_norm, v_sinks, v_w_out, v_mix_norm_odd, v_pw1_w, v_pw1_b, v_dw_w, v_dw_b, v_ln_g, v_ln_b, v_pw2_w, v_pw2_b, v_mlp_norm, v_w_up, v_w_down)))
    small_sharded = [p for p in PARAMS if p[2] is not None and p[0] not in MATMUL_WEIGHTS]
    replicated = [p for p in PARAMS if p[2] is None]
    small_list = small_sharded + replicated

    gathered = all_gather_weights([w_loc[n].astype(MXU_DTYPE) for n in MATMUL_WEIGHTS],
                                  _pack([w_loc[n] for n, _, _ in small_sharded], F32))
    w_full = {n: w_loc[n] for n, _, _ in replicated}
    w_full.update(dict(zip(MATMUL_WEIGHTS, gathered[:-1])))
    flat = gathered[-1].reshape(N_DEV, -1)
    off = 0
    for n, shape, axis in small_sharded:
        blk = _block_shape(shape, axis)
        w_full[n] = _from_shards(flat[:, off:off + _numel(blk)].reshape((N_DEV,) + blk), axis)
        off += _numel(blk)

    loss_local, grad_x, g_full, received, pending = local_step(x[0], loss_target[0], w_full)
    loss = lax.psum(loss_local, ("x", "y", "c"))

    by_dev = [_to_shards(g_full[n], axis) for n, _, axis in small_sharded]
    by_dev += [jnp.broadcast_to(g_full[n][None], (N_DEV,) + tuple(shape)) for n, shape, _ in replicated]
    last = exchange_gradients([a for _, _, a in pending] + [_pack_rows(by_dev, F32)])
    for (n, l, _), a in zip(pending, last[:-1]):
        received[n][l] = a

    out = {}
    for n in MATMUL_WEIGHTS:
        nl, r, cols = w_loc[n].shape
        out[n] = reduce_adamw(received[n], w_loc[n], m_loc[n], v_loc[n], _adamw_tile(r, cols))
    pk = lambda d: _pack([d[n] for n, _, _ in small_list], F32)[None]
    rows = last[-1].shape[1]
    small_out = reduce_adamw([last[-1]], pk(w_loc), pk(m_loc), pk(v_loc), _adamw_tile(rows, PACK_W))
    flats = [buf.reshape(-1) for buf in small_out]
    off = 0
    for n, shape, axis in small_list:
        blk = _block_shape(shape, axis)
        out[n] = tuple(f[off:off + _numel(blk)].reshape(blk) for f in flats)
        off += _numel(blk)
    return (loss, grad_x[None], *[out[n][0] for n in names], *[out[n][1] for n in names],
            *[out[n][2] for n in names], *[out[n][3] for n in names])
```

```python
import math

import jax
import jax.numpy as jnp
from jax import lax
from jax.experimental import pallas as pl
from jax.experimental.pallas import tpu as pltpu

F32 = jnp.float32
MXU_DTYPE = jnp.bfloat16
GRAD_WIRE_DTYPE = jnp.bfloat16
HIGHEST = lax.Precision.HIGHEST

D_MODEL = 1024
N_META = 16
BLOCK = 128
FRONT_PAD = BLOCK - N_META
ATT_HEADS = 8
ATT_KV_HEADS = 2
HEAD_DIM = 64
SSM_HEADS = 16
SSM_HEAD_DIM = 64
SSM_INNER = 1024
SSM_GROUPS = 2
SSM_STATE = 64
SSM_CONV = 4
CONF_KERNEL = 31
D_FF = 4096
EPS = 1e-6
LN_EPS = 1e-5
Q_W = 512
KV_W = 128
IN_W = 3088
MIX_W = 1536
DEPTH = 4
N_DEV = 8

ADAM_LR = 0.001
ADAM_B1 = 0.9
ADAM_B2 = 0.999
ADAM_EPS = 1e-08
ADAM_WD = 0.01
ADAM_STEP = 10

PROJ_W = 3200
COL_Z, COL_XS, COL_Q, COL_BC, COL_K, COL_V, COL_DT = 0, 1024, 2048, 2560, 2816, 2944, 3072

ROW_TILE = 640
TN_ROW_TILE = 1664
ACC_BYTES = 8 * 1024 * 1024
VMEM_LIMIT = 56 * 1024 * 1024
LANE = 128
PACK_W = 1024


def _params(*sem):
    return pltpu.CompilerParams(dimension_semantics=sem, vmem_limit_bytes=VMEM_LIMIT)


def _mx(x):
    return x.astype(MXU_DTYPE)


def _dot(a, b):
    return jnp.dot(_mx(a), _mx(b), preferred_element_type=F32)


def _dot_nt(a, b):
    return lax.dot_general(_mx(a), _mx(b), (((1,), (1,)), ((), ())), preferred_element_type=F32)


def _dot_tn(a, b):
    return lax.dot_general(_mx(a), _mx(b), (((0,), (0,)), ((), ())), preferred_element_type=F32)


def _split3(x):
    hi = x.astype(jnp.bfloat16)
    r1 = x - hi.astype(F32)
    mid = r1.astype(jnp.bfloat16)
    lo = (r1 - mid.astype(F32)).astype(jnp.bfloat16)
    return hi, mid, lo


def _sel_dot(x, sel, dims):
    x_first = dims[2]
    if sel.dtype == jnp.bool_:
        sel = jnp.where(sel, 1.0, 0.0)
    one = sel.astype(jnp.bfloat16)
    acc = None
    for part in _split3(x):
        args = (part, one) if x_first else (one, part)
        t = lax.dot_general(*args, (dims[:2], ((), ())), preferred_element_type=F32)
        acc = t if acc is None else acc + t
    return acc


def _dot_hi(a, b, exact="a"):
    if exact == "a":
        return _sel_dot(a, b, ((1,), (0,), True))
    return _sel_dot(b, a, ((1,), (0,), False))


def _dot_tn_hi(a, b):
    return _sel_dot(a, b, ((0,), (0,), True))


def _sigmoid(x):
    return 1.0 / (1.0 + jnp.exp(-x))


def _row_ids(start, rows, cols):
    return start + lax.broadcasted_iota(jnp.int32, (rows, cols), 0)


def rms_fwd(h, g, name="rms_fwd"):
    m, d = h.shape
    tm = ROW_TILE

    def body(h_ref, g_ref, u_ref):
        x = h_ref[...]
        r = lax.rsqrt(jnp.mean(x * x, -1, keepdims=True) + EPS)
        u_ref[...] = ((x * r) * g_ref[...]).astype(u_ref.dtype)

    return pl.pallas_call(
        body, grid=(m // tm,),
        in_specs=[pl.BlockSpec((tm, d), lambda i: (i, 0)), pl.BlockSpec((1, d), lambda i: (0, 0))],
        out_specs=pl.BlockSpec((tm, d), lambda i: (i, 0)),
        out_shape=jax.ShapeDtypeStruct((m, d), MXU_DTYPE), name=name, compiler_params=_params("arbitrary"),
    )(h, g)


def rms_bwd(h, g, du, dh_out, name="rms_bwd"):
    m, d = h.shape
    tm = ROW_TILE

    def body(h_ref, g_ref, du_ref, dho_ref, dh_ref, dg_ref):
        @pl.when(pl.program_id(0) == 0)
        def _():
            dg_ref[...] = jnp.zeros_like(dg_ref)

        x = h_ref[...]
        du_ = du_ref[...]
        r = lax.rsqrt(jnp.mean(x * x, -1, keepdims=True) + EPS)
        gy = du_ * g_ref[...]
        dx = r * gy - x * ((r * r * r) * jnp.mean(x * gy, -1, keepdims=True))
        dh_ref[...] = dho_ref[...] + dx
        dg_ref[...] += jnp.sum(du_ * (x * r), axis=0, keepdims=True)

    row = pl.BlockSpec((tm, d), lambda i: (i, 0))
    vec = pl.BlockSpec((1, d), lambda i: (0, 0))
    return pl.pallas_call(
        body, grid=(m // tm,), in_specs=[row, vec, row, row], out_specs=[row, vec],
        out_shape=[jax.ShapeDtypeStruct((m, d), F32), jax.ShapeDtypeStruct((1, d), F32)],
        name=name, compiler_params=_params("arbitrary"),
    )(h, g, du, dh_out)


def loss_fwd_bwd(h, target, name="loss"):
    m, d = h.shape
    nb = m // BLOCK

    def body(h_ref, t_ref, dh_ref, l_ref):
        i = pl.program_id(0)

        @pl.when(i == 0)
        def _():
            l_ref[...] = jnp.zeros_like(l_ref)
            dh_ref[...] = jnp.zeros_like(dh_ref)

        @pl.when(i > 0)
        def _():
            e = h_ref[...] - t_ref[...]
            dh_ref[...] = e * (1.0 / d)
            s = jnp.sum(jnp.sum(e * e, axis=-1, keepdims=True), axis=0, keepdims=True)
            l_ref[...] += jnp.broadcast_to(s * (0.5 / d), l_ref.shape)

    return pl.pallas_call(
        body, grid=(nb,),
        in_specs=[pl.BlockSpec((BLOCK, d), lambda i: (i, 0)),
                  pl.BlockSpec((BLOCK, d), lambda i: (jnp.maximum(i - 1, 0), 0))],
        out_specs=[pl.BlockSpec((BLOCK, d), lambda i: (i, 0)), pl.BlockSpec((1, LANE), lambda i: (0, 0))],
        out_shape=[jax.ShapeDtypeStruct((m, d), F32), jax.ShapeDtypeStruct((1, LANE), F32)],
        name=name, compiler_params=_params("arbitrary"),
    )(h, target)


def col_sum(x, name="col_sum"):
    m, n = x.shape
    tm = ROW_TILE

    def body(x_ref, o_ref):
        @pl.when(pl.program_id(0) == 0)
        def _():
            o_ref[...] = jnp.zeros_like(o_ref)

        o_ref[...] += jnp.sum(x_ref[...].astype(F32), axis=0, keepdims=True)

    return pl.pallas_call(
        body, grid=(m // tm,), in_specs=[pl.BlockSpec((tm, n), lambda i: (i, 0))],
        out_specs=pl.BlockSpec((1, n), lambda i: (0, 0)), out_shape=jax.ShapeDtypeStruct((1, n), F32),
        name=name, compiler_params=_params("arbitrary"),
    )(x)


def matmul(a, b, *, b_kind="full", layer=0, trans_b=False, tn=None, epilogue=None, bias=None, extra=None,
           out_dtype=F32, name="matmul"):
    m, k = a.shape
    tm = ROW_TILE
    merge = False
    if b_kind == "full":
        n = b.shape[0] if trans_b else b.shape[1]
        tn = n if tn is None else tn
        b_spec = pl.BlockSpec((tn, k), lambda i, j: (j, 0)) if trans_b else pl.BlockSpec((k, tn), lambda i, j: (0, j))
    elif b_kind == "rowshard":
        ks, wn = b.shape[2], b.shape[3]
        if trans_b and tn == ks:
            assert wn == k
            n = N_DEV * ks
            b_spec = pl.BlockSpec((None, None, ks, wn), lambda i, j: (j, layer, 0, 0))
        else:
            assert tn is None
            merge = True
            n = N_DEV * ks if trans_b else wn
            assert (wn if trans_b else N_DEV * ks) == k
            tn = n
            b_spec = pl.BlockSpec((N_DEV, None, ks, wn), lambda i, j: (0, layer, 0, 0))
    else:
        raise ValueError(b_kind)
    has_bias = bias is not None
    has_extra = extra is not None

    def body(*refs):
        a_ref, b_ref = refs[0], refs[1]
        pos = 2
        bias_ref = extra_ref = None
        if has_bias:
            bias_ref = refs[pos]
            pos += 1
        if has_extra:
            extra_ref = refs[pos]
            pos += 1
        outs = refs[pos:]
        w = b_ref[...]
        if merge:
            w = w.reshape(N_DEV * w.shape[1], w.shape[2])
        if trans_b:
            acc = _dot_nt(a_ref[...], w)
        else:
            acc = _dot(a_ref[...], w)
        if has_bias:
            acc = acc + bias_ref[...]
        if epilogue is None:
            outs[0][...] = acc.astype(outs[0].dtype)
        elif epilogue == "relu2":
            outs[0][...] = acc
            r = jnp.maximum(acc, 0.0)
            outs[1][...] = (r * r).astype(outs[1].dtype)
        elif epilogue == "drelu2":
            outs[0][...] = (acc * (2.0 * jnp.maximum(extra_ref[...], 0.0))).astype(outs[0].dtype)
        elif epilogue == "resid":
            rows = _row_ids(pl.program_id(0) * tm, tm, tn)
            outs[0][...] = extra_ref[...] + jnp.where(rows >= FRONT_PAD, acc, 0.0)
        else:
            raise ValueError(epilogue)

    in_specs = [pl.BlockSpec((tm, k), lambda i, j: (i, 0)), b_spec]
    args = [a, b]
    if has_bias:
        in_specs.append(pl.BlockSpec((1, tn), lambda i, j: (0, j)))
        args.append(bias)
    if has_extra:
        in_specs.append(pl.BlockSpec((tm, tn), lambda i, j: (i, j)))
        args.append(extra)
    tile = pl.BlockSpec((tm, tn), lambda i, j: (i, j))
    if epilogue == "relu2":
        out_specs = [tile, tile]
        out_shape = [jax.ShapeDtypeStruct((m, n), F32), jax.ShapeDtypeStruct((m, n), MXU_DTYPE)]
    else:
        out_specs = tile
        out_shape = jax.ShapeDtypeStruct((m, n), out_dtype)
    return pl.pallas_call(
        body, grid=(m // tm, n // tn), in_specs=in_specs, out_specs=out_specs, out_shape=out_shape,
        name=name, compiler_params=_params("arbitrary", "arbitrary"),
    )(*args)


def matmul_tn(x, dy, *, ti, tn, out_dtype=F32, name="matmul_tn"):
    m, k1 = x.shape
    n = dy.shape[1]
    tm = TN_ROW_TILE
    last = m // tm - 1

    def body(x_ref, dy_ref, o_ref, acc_ref):
        r = pl.program_id(2)

        @pl.when(r == 0)
        def _():
            acc_ref[...] = jnp.zeros_like(acc_ref)

        acc_ref[...] += _dot_tn(x_ref[...], dy_ref[...])

        @pl.when(r == last)
        def _():
            o_ref[...] = acc_ref[...].astype(o_ref.dtype)

    out_specs = pl.BlockSpec((ti, tn), lambda i, j, r: (i, j))
    out_shape = jax.ShapeDtypeStruct((k1, n), out_dtype)
    return pl.pallas_call(
        body, grid=(k1 // ti, n // tn, m // tm),
        in_specs=[pl.BlockSpec((tm, ti), lambda i, j, r: (r, i)), pl.BlockSpec((tm, tn), lambda i, j, r: (r, j))],
        out_specs=out_specs, out_shape=out_shape, scratch_shapes=[pltpu.VMEM((ti, tn), F32)], name=name,
        compiler_params=_params("arbitrary", "arbitrary", "arbitrary"),
    )(x, dy)


FF_BLOCK = D_FF // N_DEV
SQRT_FLOOR = 1.1754944e-38


def _ff_cols(d):
    return slice(FF_BLOCK * d, FF_BLOCK * (d + 1))


def mlp_up(u, w, layer, *, bias=None, relu2=True, out_dtype=None, name="mlp_up"):
    m = u.shape[0]
    ns = w.shape[3]
    n = N_DEV * ns
    tm = ROW_TILE
    out_dtype = MXU_DTYPE if relu2 else out_dtype
    has_bias = bias is not None

    def body(*refs):
        u_ref, w_ref = refs[0], refs[1]
        bias_ref = refs[2] if has_bias else None
        o_ref = refs[-1]
        u_ = u_ref[...]
        for d in range(N_DEV):
            cols = slice(ns * d, ns * (d + 1))
            r = _dot(u_, w_ref[d])
            if has_bias:
                r = r + bias_ref[:, cols]
            if relu2:
                r = jnp.maximum(r, 0.0)
                r = r * r
            o_ref[:, cols] = r.astype(o_ref.dtype)

    in_specs = [pl.BlockSpec((tm, D_MODEL), lambda i: (i, 0)),
                pl.BlockSpec((N_DEV, None, D_MODEL, ns), lambda i: (0, layer, 0, 0))]
    args = [u, w]
    if has_bias:
        in_specs.append(pl.BlockSpec((1, n), lambda i: (0, 0)))
        args.append(bias)
    return pl.pallas_call(
        body, grid=(m // tm,), in_specs=in_specs, out_specs=pl.BlockSpec((tm, n), lambda i: (i, 0)),
        out_shape=jax.ShapeDtypeStruct((m, n), out_dtype), name=name, compiler_params=_params("arbitrary"),
    )(*args)


def mlp_dact(dh, w_down, act, layer, name="mlp_dact"):
    m = dh.shape[0]
    tm = ROW_TILE

    def body(dh_ref, w_ref, act_ref, o_ref):
        dh_ = dh_ref[...]
        for d in range(N_DEV):
            p = act_ref[:, _ff_cols(d)].astype(F32)
            r = p * lax.rsqrt(jnp.maximum(p, SQRT_FLOOR))
            o_ref[:, _ff_cols(d)] = (_dot_nt(dh_, w_ref[d]) * (2.0 * r)).astype(o_ref.dtype)

    return pl.pallas_call(
        body, grid=(m // tm,),
        in_specs=[pl.BlockSpec((tm, D_MODEL), lambda i: (i, 0)),
                  pl.BlockSpec((N_DEV, None, FF_BLOCK, D_MODEL), lambda i: (0, layer, 0, 0)),
                  pl.BlockSpec((tm, D_FF), lambda i: (i, 0))],
        out_specs=pl.BlockSpec((tm, D_FF), lambda i: (i, 0)),
        out_shape=jax.ShapeDtypeStruct((m, D_FF), MXU_DTYPE), name=name, compiler_params=_params("arbitrary"),
    )(dh, w_down, act)


def mlp_du_rms_bwd(da, w_up, layer, h, g, dh_out, name="mlp_du"):
    m, n = da.shape
    ns = w_up.shape[3]
    assert n == N_DEV * ns
    tm = ROW_TILE

    def body(da_ref, w_ref, h_ref, g_ref, dho_ref, dh_ref, dg_ref):
        @pl.when(pl.program_id(0) == 0)
        def _():
            dg_ref[...] = jnp.zeros_like(dg_ref)

        du = _dot_nt(da_ref[:, 0:ns], w_ref[0])
        for d in range(1, N_DEV):
            du = du + _dot_nt(da_ref[:, ns * d:ns * (d + 1)], w_ref[d])
        x = h_ref[...]
        r = lax.rsqrt(jnp.mean(x * x, -1, keepdims=True) + EPS)
        gy = du * g_ref[...]
        dx = r * gy - x * ((r * r * r) * jnp.mean(x * gy, -1, keepdims=True))
        dh_ref[...] = dho_ref[...] + dx
        dg_ref[...] += jnp.sum(du * (x * r), axis=0, keepdims=True)

    row = pl.BlockSpec((tm, D_MODEL), lambda i: (i, 0))
    vec = pl.BlockSpec((1, D_MODEL), lambda i: (0, 0))
    return pl.pallas_call(
        body, grid=(m // tm,),
        in_specs=[pl.BlockSpec((tm, n), lambda i: (i, 0)),
                  pl.BlockSpec((N_DEV, None, D_MODEL, ns), lambda i: (0, layer, 0, 0)), row, vec, row],
        out_specs=[row, vec],
        out_shape=[jax.ShapeDtypeStruct((m, D_MODEL), F32), jax.ShapeDtypeStruct((1, D_MODEL), F32)],
        name=name, compiler_params=_params("arbitrary"),
    )(da, w_up, h, g, dh_out)


def mlp_dw_up(u, da, name="mlp_dw_up"):
    m, n = da.shape
    ns = n // N_DEV
    tm = TN_ROW_TILE
    last = m // tm - 1
    parts = -(-D_MODEL * n * 4 // ACC_BYTES)
    per = N_DEV // parts

    def body(u_ref, da_ref, o_ref, acc_ref):
        r = pl.program_id(1)

        @pl.when(r == 0)
        def _():
            acc_ref[...] = jnp.zeros_like(acc_ref)

        acc_ref[...] += _dot_tn(u_ref[...], da_ref[...])

        @pl.when(r == last)
        def _():
            for d in range(per):
                o_ref[d] = acc_ref[:, ns * d:ns * (d + 1)].astype(o_ref.dtype)

    return pl.pallas_call(
        body, grid=(parts, m // tm),
        in_specs=[pl.BlockSpec((tm, D_MODEL), lambda h, r: (r, 0)), pl.BlockSpec((tm, n // parts), lambda h, r: (r, h))],
        out_specs=pl.BlockSpec((per, D_MODEL, ns), lambda h, r: (h, 0, 0)),
        out_shape=jax.ShapeDtypeStruct((N_DEV, D_MODEL, ns), GRAD_WIRE_DTYPE),
        scratch_shapes=[pltpu.VMEM((D_MODEL, n // parts), F32)], name=name,
        compiler_params=_params("arbitrary", "arbitrary"),
    )(u, da)


def mlp_dw_down(act, dh, name="mlp_dw_down"):
    m = act.shape[0]
    tm = TN_ROW_TILE
    last = m // tm - 1
    parts = D_FF * D_MODEL * 4 // ACC_BYTES
    rows = D_FF // parts

    def body(a_ref, dh_ref, o_ref, acc_ref):
        r = pl.program_id(1)

        @pl.when(r == 0)
        def _():
            acc_ref[...] = jnp.zeros_like(acc_ref)

        acc_ref[...] += _dot_tn(a_ref[...], dh_ref[...])

        @pl.when(r == last)
        def _():
            o_ref[...] = acc_ref[...].astype(o_ref.dtype)

    return pl.pallas_call(
        body, grid=(parts, m // tm),
        in_specs=[pl.BlockSpec((tm, rows), lambda h, r: (r, h)), pl.BlockSpec((tm, D_MODEL), lambda h, r: (r, 0))],
        out_specs=pl.BlockSpec((rows, D_MODEL), lambda h, r: (h, 0)),
        out_shape=jax.ShapeDtypeStruct((D_FF, D_MODEL), GRAD_WIRE_DTYPE),
        scratch_shapes=[pltpu.VMEM((rows, D_MODEL), F32)], name=name,
        compiler_params=_params("arbitrary", "arbitrary"),
    )(act, dh)


_ATT_SCALE = HEAD_DIM ** -0.5


def _alibi_slope(h):
    return 2.0 ** (-8.0 * (h + 1) / ATT_HEADS)


HEADS_PER_KV = ATT_HEADS // ATT_KV_HEADS
STACK = HEADS_PER_KV * BLOCK
N_KEYS = 3 * BLOCK


def _head_select(width):
    c = jnp.arange(width)[:, None]
    h = jnp.arange(LANE)[None, :]
    return (c // HEAD_DIM == h).astype(F32)


def _head_fold(width):
    c = jnp.arange(width)[:, None]
    j = jnp.arange(LANE)[None, :]
    return (c % HEAD_DIM == j).astype(F32)


def _head_rms(x, sel, sel_t):
    r = lax.rsqrt(_dot_hi(x * x, sel) * (1.0 / HEAD_DIM) + EPS)
    return r, _dot_hi(r, sel_t)


def _head_norm_bwd(x, r, r_full, w_t, dy, sel, sel_t):
    gy = dy * w_t
    coef = _dot_hi((r * r * r) * _dot_hi(x * gy, sel) * (1.0 / HEAD_DIM), sel_t)
    return r_full * gy - x * coef, jnp.sum(dy * (x * r_full), axis=0, keepdims=True)


def _low_lanes(rows):
    return lax.broadcasted_iota(jnp.int32, (rows, LANE), 1) < HEAD_DIM


def _dup_half(a, g):
    rolled = pltpu.roll(a, HEAD_DIM, 1)
    low = _low_lanes(a.shape[0])
    return jnp.where(low, a, rolled) if g == 0 else jnp.where(low, rolled, a)


def _stack_heads(x, g):
    low = _low_lanes(BLOCK)
    parts = []
    for pair in range(2):
        p = x[:, 2 * LANE * g + LANE * pair:2 * LANE * g + LANE * (pair + 1)]
        parts += [jnp.where(low, p, 0.0), jnp.where(low, 0.0, p)]
    return jnp.concatenate(parts, axis=0)


def _unstack_heads(groups):
    low = _low_lanes(BLOCK)
    cols = []
    for o in groups:
        for pair in range(2):
            cols.append(jnp.where(low, o[2 * pair * BLOCK:(2 * pair + 1) * BLOCK], o[(2 * pair + 1) * BLOCK:(2 * pair + 2) * BLOCK]))
    return jnp.concatenate(cols, axis=1)


def _fold_halves(a, g):
    s = a + pltpu.roll(a, HEAD_DIM, 1)
    low = _low_lanes(a.shape[0])
    return jnp.where(low if g == 0 else jnp.logical_not(low), s, 0.0)


def _att_bias(b):
    r = lax.broadcasted_iota(jnp.int32, (STACK, N_KEYS), 0) & (BLOCK - 1)
    col = lax.broadcasted_iota(jnp.int32, (STACK, N_KEYS), 1)
    cc = col & (BLOCK - 1)
    is_meta = col < BLOCK
    is_prev = jnp.logical_and(col >= BLOCK, col < 2 * BLOCK)
    q_pos = b * BLOCK + r - FRONT_PAD
    meta_j = cc - FRONT_PAD
    valid_m = jnp.logical_and(cc >= FRONT_PAD, q_pos >= meta_j)
    valid_p = jnp.logical_and(cc > r, b >= 2)
    valid_c = jnp.logical_and(cc <= r, b >= 1)
    is_cur = col >= 2 * BLOCK
    valid = jnp.logical_or(jnp.logical_and(is_meta, valid_m),
                           jnp.logical_or(jnp.logical_and(is_prev, valid_p), jnp.logical_and(is_cur, valid_c)))
    dist = jnp.where(is_meta, jnp.minimum(q_pos - meta_j, BLOCK), jnp.where(is_prev, r - cc + BLOCK, r - cc))
    return valid, dist.astype(F32)


def _per_head_column(values):
    hid = lax.broadcasted_iota(jnp.int32, (STACK, 1), 0) >> 7
    col = jnp.where(hid == 0, values[0], values[1])
    for j in range(2, HEADS_PER_KV):
        col = jnp.where(hid == j, values[j], col)
    return col


def _att_group_probs(qs, kd, valid, dist, g, sk_ref):
    slope = _per_head_column([_alibi_slope(HEADS_PER_KV * g + j) for j in range(HEADS_PER_KV)])
    sink = _per_head_column([sk_ref[HEADS_PER_KV * g + j] for j in range(HEADS_PER_KV)])
    s = jnp.where(valid, _dot_nt(qs, kd) * _ATT_SCALE - slope * dist, -1e30)
    mx = jnp.maximum(jnp.max(s, axis=-1, keepdims=True), sink)
    p = jnp.exp(s - mx)
    p_sink = jnp.exp(sink - mx)
    inv = 1.0 / (jnp.sum(p, axis=-1, keepdims=True) + p_sink)
    return p * inv, p_sink * inv


def attention_fwd(proj, q_w, k_w, sinks, gather=(), name="att_fwd"):
    m = proj.shape[0]
    nb = m // BLOCK
    cq, ck, cv = COL_Q // Q_W, COL_K // KV_W, COL_V // KV_W
    sel_q, sel_k = _head_select(Q_W), _head_select(KV_W)

    def body(q_ref, kc_ref, vc_ref, vp_ref, vm_ref, qw_ref, kw_ref, sk_ref, sq_ref, sqt_ref, skk_ref, skt_ref,
             o_ref, kpn_s, kmn_s):
        b = pl.program_id(0)
        q, kc = q_ref[...], kc_ref[...]
        _, rq = _head_rms(q, sq_ref[...], sqt_ref[...])
        qn = q * rq * qw_ref[...]
        _, rk = _head_rms(kc, skk_ref[...], skt_ref[...])
        kcn = kc * rk * kw_ref[...]

        @pl.when(b == 0)
        def _():
            kmn_s[...] = kcn
            kpn_s[...] = kcn

        kpn, kmn = kpn_s[...], kmn_s[...]
        kpn_s[...] = kcn
        vc, vp, vm = vc_ref[...], vp_ref[...], vm_ref[...]
        valid, dist = _att_bias(b)
        outs = []
        for g in range(ATT_KV_HEADS):
            kd = jnp.concatenate([_dup_half(kmn, g), _dup_half(kpn, g), _dup_half(kcn, g)], axis=0)
            vd = jnp.concatenate([_dup_half(vm, g), _dup_half(vp, g), _dup_half(vc, g)], axis=0)
            probs, _ = _att_group_probs(_stack_heads(qn, g), kd, valid, dist, g, sk_ref)
            outs.append(_dot(probs, vd))
        o_ref[...] = _unstack_heads(outs).astype(o_ref.dtype)

    prev = lambda i: jnp.maximum(i - 1, 0)
    full = lambda s: pl.BlockSpec(s, lambda i: (0,) * len(s))
    body, g_in, g_out, g_shape, g_scratch = _with_gather(body, 12, 1, gather, nb)
    outs = pl.pallas_call(
        body, grid=(nb,),
        in_specs=[pl.BlockSpec((BLOCK, Q_W), lambda i: (i, cq)),
                  pl.BlockSpec((BLOCK, KV_W), lambda i: (i, ck)), pl.BlockSpec((BLOCK, KV_W), lambda i: (i, cv)),
                  pl.BlockSpec((BLOCK, KV_W), lambda i: (prev(i), cv)), pl.BlockSpec((BLOCK, KV_W), lambda i: (0, cv)),
                  full((1, Q_W)), full((1, KV_W)), pl.BlockSpec(memory_space=pltpu.SMEM),
                  full((Q_W, LANE)), full((LANE, Q_W)), full((KV_W, LANE)), full((LANE, KV_W))] + g_in,
        out_specs=[pl.BlockSpec((BLOCK, Q_W), lambda i: (i, 0))] + g_out,
        out_shape=[jax.ShapeDtypeStruct((m, Q_W), MXU_DTYPE)] + g_shape,
        scratch_shapes=[pltpu.VMEM((BLOCK, KV_W), F32), pltpu.VMEM((BLOCK, KV_W), F32)] + g_scratch,
        name=name, compiler_params=_params("arbitrary"),
    )(proj, proj, proj, proj, proj, q_w, k_w, sinks, sel_q, sel_q.T, sel_k, sel_k.T, *gather)
    return outs[0], outs[1:]


def attention_bwd(proj, dmix, q_w, k_w, sinks, carry=(), name="att_bwd"):
    m = proj.shape[0]
    nb = m // BLOCK
    cq, ck, cv = COL_Q // Q_W, COL_K // KV_W, COL_V // KV_W
    c_datt = 0
    sel_q, sel_k = _head_select(Q_W), _head_select(KV_W)
    fold_q, fold_k = _head_fold(Q_W), _head_fold(KV_W)

    def body(do_ref, q_ref, kc_ref, vc_ref, kp_ref, vp_ref, km_ref, vm_ref, qw_ref, kw_ref, sk_ref,
             sq_ref, sqt_ref, skk_ref, skt_ref, fq_ref, fk_ref,
             dq_ref, dk_ref, dv_ref, dqw_ref, dkw_ref, dsk_ref, car_k, car_v, met_k, met_v, kmn_s, qw_acc, kw_acc):
        i = pl.program_id(0)
        b = nb - 1 - i
        sel_q_, sel_qt, sel_k_, sel_kt = sq_ref[...], sqt_ref[...], skk_ref[...], skt_ref[...]
        qw, kw = qw_ref[...], kw_ref[...]

        @pl.when(i == 0)
        def _():
            for r in (car_k, car_v, met_k, met_v, qw_acc, kw_acc, dsk_ref):
                r[...] = jnp.zeros_like(r)
            km = km_ref[...]
            kmn_s[...] = km * _head_rms(km, sel_k_, sel_kt)[1] * kw

        q, kc, kp = q_ref[...], kc_ref[...], kp_ref[...]
        rq, rq_full = _head_rms(q, sel_q_, sel_qt)
        qn = q * rq_full * qw
        rk, rk_full = _head_rms(kc, sel_k_, sel_kt)
        kcn = kc * rk_full * kw
        kpn = kp * _head_rms(kp, sel_k_, sel_kt)[1] * kw
        kmn = kmn_s[...]
        vc, vp, vm = vc_ref[...], vp_ref[...], vm_ref[...]
        do = do_ref[...]
        valid, dist = _att_bias(b)
        lane = lax.broadcasted_iota(jnp.int32, (1, LANE), 1)
        dsk = jnp.zeros((1, LANE), F32)
        dkd_sum = jnp.zeros((N_KEYS, KV_W), F32)
        dvd_sum = jnp.zeros((N_KEYS, KV_W), F32)
        dqd = []
        for g in range(ATT_KV_HEADS):
            kd = jnp.concatenate([_dup_half(kmn, g), _dup_half(kpn, g), _dup_half(kcn, g)], axis=0)
            vd = jnp.concatenate([_dup_half(vm, g), _dup_half(vp, g), _dup_half(vc, g)], axis=0)
            qs = _stack_heads(qn, g)
            dos = _stack_heads(do, g)
            probs, p_sink = _att_group_probs(qs, kd, valid, dist, g, sk_ref)
            o = _dot(probs, vd)
            delta = jnp.sum(dos * o, axis=-1, keepdims=True)
            ds = probs * (_dot_nt(dos, vd) - delta)
            dqd.append(_dot(ds, kd) * _ATT_SCALE)
            dkd_sum = dkd_sum + _fold_halves(_dot_tn(ds, qs) * _ATT_SCALE, g)
            dvd_sum = dvd_sum + _fold_halves(_dot_tn(probs, dos), g)
            sink_grad = p_sink * delta
            for j in range(HEADS_PER_KV):
                part = jnp.sum(sink_grad[BLOCK * j:BLOCK * (j + 1)], axis=0, keepdims=True)
                dsk = dsk - jnp.where(lane == HEADS_PER_KV * g + j, part, 0.0)
        dq, dqw = _head_norm_bwd(q, rq, rq_full, qw, _unstack_heads(dqd), sel_q_, sel_qt)
        dq_ref[...] = dq
        qw_acc[...] += dqw
        dsk_ref[...] += dsk

        met_k[...] += dkd_sum[0:BLOCK]
        met_v[...] += dvd_sum[0:BLOCK]
        first = (b == 0).astype(F32)
        dkn_tot = dkd_sum[2 * BLOCK:3 * BLOCK] + car_k[...] + first * met_k[...]
        dv_ref[...] = dvd_sum[2 * BLOCK:3 * BLOCK] + car_v[...] + first * met_v[...]
        car_k[...] = dkd_sum[BLOCK:2 * BLOCK]
        car_v[...] = dvd_sum[BLOCK:2 * BLOCK]
        dk, dkw = _head_norm_bwd(kc, rk, rk_full, kw, dkn_tot, sel_k_, sel_kt)
        dk_ref[...] = dk
        kw_acc[...] += dkw

        @pl.when(i == nb - 1)
        def _():
            dqw_ref[...] = _dot_hi(jnp.broadcast_to(qw_acc[...], (8, Q_W)), fq_ref[...])[0:1]
            dkw_ref[...] = _dot_hi(jnp.broadcast_to(kw_acc[...], (8, KV_W)), fk_ref[...])[0:1]

    blk = lambda i: nb - 1 - i
    prev = lambda i: jnp.maximum(nb - 2 - i, 0)
    full = lambda s: pl.BlockSpec(s, lambda i: (0,) * len(s))
    kv_scratch = pltpu.VMEM((BLOCK, KV_W), F32)
    body, ex_in, ex_out, ex_shape, ex_scratch = _with_exchange(body, 17, 6, carry, nb)
    outs = pl.pallas_call(
        body, grid=(nb,),
        in_specs=[pl.BlockSpec((BLOCK, Q_W), lambda i: (blk(i), c_datt)),
                  pl.BlockSpec((BLOCK, Q_W), lambda i: (blk(i), cq)),
                  pl.BlockSpec((BLOCK, KV_W), lambda i: (blk(i), ck)), pl.BlockSpec((BLOCK, KV_W), lambda i: (blk(i), cv)),
                  pl.BlockSpec((BLOCK, KV_W), lambda i: (prev(i), ck)), pl.BlockSpec((BLOCK, KV_W), lambda i: (prev(i), cv)),
                  pl.BlockSpec((BLOCK, KV_W), lambda i: (0, ck)), pl.BlockSpec((BLOCK, KV_W), lambda i: (0, cv)),
                  full((1, Q_W)), full((1, KV_W)), pl.BlockSpec(memory_space=pltpu.SMEM),
                  full((Q_W, LANE)), full((LANE, Q_W)), full((KV_W, LANE)), full((LANE, KV_W)),
                  full((Q_W, LANE)), full((KV_W, LANE))] + ex_in,
        out_specs=[pl.BlockSpec((BLOCK, Q_W), lambda i: (blk(i), 0)),
                   pl.BlockSpec((BLOCK, KV_W), lambda i: (blk(i), 0)), pl.BlockSpec((BLOCK, KV_W), lambda i: (blk(i), 0)),
                   full((1, LANE)), full((1, LANE)), full((1, LANE))] + ex_out,
        out_shape=[jax.ShapeDtypeStruct((m, Q_W), F32), jax.ShapeDtypeStruct((m, KV_W), F32),
                   jax.ShapeDtypeStruct((m, KV_W), F32), jax.ShapeDtypeStruct((1, LANE), F32),
                   jax.ShapeDtypeStruct((1, LANE), F32), jax.ShapeDtypeStruct((1, LANE), F32)] + ex_shape,
        scratch_shapes=[kv_scratch, kv_scratch, kv_scratch, kv_scratch, kv_scratch,
                        pltpu.VMEM((1, Q_W), F32), pltpu.VMEM((1, KV_W), F32)] + ex_scratch,
        name=name, compiler_params=_params("arbitrary"),
    )(dmix, proj, proj, proj, proj, proj, proj, proj, q_w, k_w, sinks, sel_q, sel_q.T, sel_k, sel_k.T, fold_q, fold_k,
      *carry)
    return outs[:6], outs[6:]


HALO = 8
GROUP_W = SSM_INNER // SSM_GROUPS
HEADS_PER_GROUP = SSM_HEADS // SSM_GROUPS


def _head_expand():
    h = jnp.arange(LANE)[:, None]
    c = jnp.arange(SSM_INNER)[None, :]
    return (c // SSM_HEAD_DIM == h).astype(F32)


def _softplus(x):
    return jnp.maximum(x, 0.0) + jnp.log1p(jnp.exp(-jnp.abs(x)))


def _ssd_decays(dt, a_log_row):
    row = lax.broadcasted_iota(jnp.int32, (BLOCK, BLOCK), 0)
    col = lax.broadcasted_iota(jnp.int32, (BLOCK, BLOCK), 1)
    lower = row >= col
    a = -jnp.exp(a_log_row)
    a_cs = _dot_hi(lower, dt * a, exact="b")
    return a, a_cs, lower


def _decay_matrix(a_cs, a_cs_t, h, lower):
    diff = a_cs[:, h:h + 1] - a_cs_t[h:h + 1, :]
    return jnp.where(lower, jnp.exp(jnp.where(lower, diff, 0.0)), 0.0)


def _conv_taps(s_ref, w_ref, first, rows):
    acc = w_ref[0:1, :] * s_ref[pl.ds(first, rows), :]
    for j in range(1, SSM_CONV):
        acc = acc + w_ref[j:j + 1, :] * s_ref[pl.ds(first + j, rows), :]
    return acc


def ssd_fwd(proj, cw_x, cw_bc, cb_x, cb_bc, dt_bias, a_log, d_exp, norm_w, gather=(), name="ssd_fwd"):
    m = proj.shape[0]
    nb = m // BLOCK
    expand = _head_expand()
    expand_t = expand.T

    def body(z_ref, xs_ref, bc_ref, dtr_ref, cwx_ref, cwbc_ref, cbx_ref, cbbc_ref, dtb_ref, alog_ref, dexp_ref,
             nw_ref, e_ref, et_ref, out_ref, prex_ref, prebc_ref, dt_ref, ypre_ref, st_ref, sx, sbc, state):
        c = pl.program_id(0)

        @pl.when(c == 0)
        def _():
            sx[0:HALO, :] = jnp.zeros((HALO, SSM_INNER), F32)
            sbc[0:HALO, :] = jnp.zeros((HALO, 2 * LANE), F32)
            state[...] = jnp.zeros_like(state)

        sx[HALO:HALO + BLOCK, :] = xs_ref[...]
        sbc[HALO:HALO + BLOCK, :] = bc_ref[...]
        first = HALO - (SSM_CONV - 1)
        pre_x = _conv_taps(sx, cwx_ref, first, BLOCK) + cbx_ref[...]
        pre_bc = _conv_taps(sbc, cwbc_ref, first, BLOCK) + cbbc_ref[...]
        sx[0:HALO, :] = xs_ref[BLOCK - HALO:BLOCK, :]
        sbc[0:HALO, :] = bc_ref[BLOCK - HALO:BLOCK, :]
        prex_ref[...] = pre_x
        prebc_ref[...] = pre_bc
        xc = pre_x * _sigmoid(pre_x)
        bcv = pre_bc * _sigmoid(pre_bc)

        rows = _row_ids(c * BLOCK, BLOCK, LANE)
        lanes = lax.broadcasted_iota(jnp.int32, (BLOCK, LANE), 1)
        live = jnp.logical_and(rows >= FRONT_PAD, lanes < SSM_HEADS)
        dt = jnp.where(live, _softplus(dtr_ref[...] + dtb_ref[...]), 0.0)
        dt_ref[...] = dt
        a, a_cs, lower = _ssd_decays(dt, alog_ref[...])
        a_cs_t = a_cs.T
        dt_t = dt.T
        e = e_ref[...]
        es_full = _dot_hi(jnp.exp(a_cs), e)
        wx_full = _dot_hi(jnp.exp(a_cs[BLOCK - 1:BLOCK, :] - a_cs) * dt, e)
        end_col = jnp.exp(a_cs_t[:, BLOCK - 1:BLOCK])
        dec_full = _dot_hi(et_ref[...], jnp.broadcast_to(end_col, (LANE, SSM_STATE)), exact="b")

        st_ref[0] = state[...]
        ys = []
        for g in range(SSM_GROUPS):
            b_g = bcv[:, SSM_STATE * g:SSM_STATE * (g + 1)]
            c_g = bcv[:, LANE + SSM_STATE * g:LANE + SSM_STATE * (g + 1)]
            gs = slice(GROUP_W * g, GROUP_W * (g + 1))
            cb = _dot_nt(c_g, b_g)
            yd = []
            for hh in range(HEADS_PER_GROUP):
                h = g * HEADS_PER_GROUP + hh
                w = cb * _decay_matrix(a_cs, a_cs_t, h, lower) * dt_t[h:h + 1, :]
                yd.append(_dot(w, xc[:, SSM_HEAD_DIM * h:SSM_HEAD_DIM * (h + 1)]))
            h_g = state[gs, :]
            y_off = _dot_nt(c_g, h_g) * es_full[:, gs]
            ys.append(jnp.concatenate(yd, axis=1) + y_off)
            new_state = _dot_tn(xc[:, gs] * wx_full[:, gs], b_g)
            state[gs, :] = h_g * dec_full[gs, :] + new_state
        y_pre = jnp.concatenate(ys, axis=1) + xc * dexp_ref[...]
        ypre_ref[...] = y_pre
        z = z_ref[...]
        gt = y_pre * (z * _sigmoid(z))
        outs = []
        for g in range(SSM_GROUPS):
            gg = gt[:, GROUP_W * g:GROUP_W * (g + 1)]
            r = lax.rsqrt(jnp.mean(gg * gg, -1, keepdims=True) + EPS)
            outs.append(gg * r)
        out_ref[...] = (jnp.concatenate(outs, axis=1) * nw_ref[...]).astype(out_ref.dtype)

    full = lambda s: pl.BlockSpec(s, lambda i: (0,) * len(s))
    rowblk = lambda w, cidx: pl.BlockSpec((BLOCK, w), lambda i: (i, cidx))
    body, g_in, g_out, g_shape, g_scratch = _with_gather(body, 14, 6, gather, nb)
    outs = pl.pallas_call(
        body, grid=(nb,),
        in_specs=[rowblk(SSM_INNER, COL_Z // SSM_INNER), rowblk(SSM_INNER, COL_XS // SSM_INNER),
                  rowblk(2 * LANE, COL_BC // (2 * LANE)), rowblk(LANE, COL_DT // LANE),
                  full((SSM_CONV, SSM_INNER)), full((SSM_CONV, 2 * LANE)), full((1, SSM_INNER)), full((1, 2 * LANE)),
                  full((1, LANE)), full((1, LANE)), full((1, SSM_INNER)), full((1, SSM_INNER)),
                  full((LANE, SSM_INNER)), full((SSM_INNER, LANE))] + g_in,
        out_specs=[rowblk(SSM_INNER, 0), rowblk(SSM_INNER, 0), rowblk(2 * LANE, 0), rowblk(LANE, 0),
                   rowblk(SSM_INNER, 0), pl.BlockSpec((1, SSM_INNER, SSM_STATE), lambda i: (i, 0, 0))] + g_out,
        out_shape=[jax.ShapeDtypeStruct((m, SSM_INNER), MXU_DTYPE), jax.ShapeDtypeStruct((m, SSM_INNER), F32),
                   jax.ShapeDtypeStruct((m, 2 * LANE), F32), jax.ShapeDtypeStruct((m, LANE), F32),
                   jax.ShapeDtypeStruct((m, SSM_INNER), F32),
                   jax.ShapeDtypeStruct((nb, SSM_INNER, SSM_STATE), F32)] + g_shape,
        scratch_shapes=[pltpu.VMEM((HALO + BLOCK, SSM_INNER), F32), pltpu.VMEM((HALO + BLOCK, 2 * LANE), F32),
                        pltpu.VMEM((SSM_INNER, SSM_STATE), F32)] + g_scratch,
        name=name, compiler_params=_params("arbitrary"),
    )(proj, proj, proj, proj, cw_x, cw_bc, cb_x, cb_bc, dt_bias, a_log, d_exp, norm_w, expand, expand_t, *gather)
    return outs[:6], outs[6:]


def ssd_bwd(proj, dmix, pre_x, pre_bc, dt, y_pre, states, cw_x, cw_bc, dt_bias, a_log, d_exp, norm_w,
            carry=(), name="ssd_bwd"):
    m = proj.shape[0]
    nb = m // BLOCK
    expand = _head_expand()
    expand_t = expand.T

    def body(do0_ref, do1_ref, z_ref, xs_ref, xsp_ref, bc_ref, bcp_ref, dtr_ref, prex_ref, prebc_ref, dt_ref, ypre_ref,
             st_ref,
             cwx_ref, cwbc_ref, dtb_ref, alog_ref, dexp_ref, nw_ref, e_ref, et_ref,
             dz_ref, dxs_ref, dbc_ref, ddt_ref, dcwx_ref, dcwbc_ref, dcbx_ref, dcbbc_ref, ddtb_ref, dalog_ref,
             dd_ref, dnw_ref,
             dstate, hnext, tx, tbc, sx, sbc, dlane):
        i = pl.program_id(0)
        c = nb - 1 - i

        @pl.when(i == 0)
        def _():
            dstate[...] = jnp.zeros_like(dstate)
            hnext[...] = jnp.zeros_like(hnext)
            tx[BLOCK:BLOCK + HALO, :] = jnp.zeros((HALO, SSM_INNER), F32)
            tbc[BLOCK:BLOCK + HALO, :] = jnp.zeros((HALO, 2 * LANE), F32)
            dlane[...] = jnp.zeros_like(dlane)
            for r in (dcwx_ref, dcwbc_ref, dcbx_ref, dcbbc_ref, ddtb_ref, dalog_ref, dd_ref, dnw_ref):
                r[...] = jnp.zeros_like(r)

        e = e_ref[...]
        et = et_ref[...]
        pre_x = prex_ref[...]
        pre_bc = prebc_ref[...]
        sig_x = _sigmoid(pre_x)
        sig_bc = _sigmoid(pre_bc)
        xc = pre_x * sig_x
        bcv = pre_bc * sig_bc
        dt = dt_ref[...]
        a, a_cs, lower = _ssd_decays(dt, alog_ref[...])
        a_cs_t = a_cs.T
        es_full = _dot_hi(jnp.exp(a_cs), e)
        ed_full = _dot_hi(jnp.exp(a_cs[BLOCK - 1:BLOCK, :] - a_cs), e)
        dt_full = _dot_hi(dt, e)
        end_col = jnp.exp(a_cs_t[:, BLOCK - 1:BLOCK])
        dec_full = _dot_hi(et, jnp.broadcast_to(end_col, (LANE, SSM_STATE)), exact="b")
        dexp = dexp_ref[...]

        z = z_ref[...]
        zs = _sigmoid(z)
        sz = z * zs
        y_pre = ypre_ref[...]
        gt = y_pre * sz
        do = jnp.concatenate([do0_ref[...], do1_ref[...]], axis=1)
        nw = nw_ref[...]
        dgt = []
        dnw = []
        for g in range(SSM_GROUPS):
            gs = slice(GROUP_W * g, GROUP_W * (g + 1))
            gg = gt[:, gs]
            r = lax.rsqrt(jnp.mean(gg * gg, -1, keepdims=True) + EPS)
            gn = do[:, gs] * nw[:, gs]
            dgt.append(r * gn - gg * ((r * r * r) * jnp.mean(gg * gn, -1, keepdims=True)))
            dnw.append(jnp.sum(do[:, gs] * (gg * r), axis=0, keepdims=True))
        dgt = jnp.concatenate(dgt, axis=1)
        dnw_ref[...] += jnp.concatenate(dnw, axis=1)
        dy = dgt * sz
        dz_ref[...] = dgt * y_pre * (zs * (1.0 + z * (1.0 - zs)))
        dlane[...] += jnp.sum(dy * xc, axis=0, keepdims=True)
        xd = xc * dt_full

        lane_id = lax.broadcasted_iota(jnp.int32, (BLOCK, LANE), 1)
        sub_id = lax.broadcasted_iota(jnp.int32, (LANE, BLOCK), 0)
        ds_to = jnp.zeros((BLOCK, LANE), F32)
        ds_from_t = jnp.zeros((LANE, BLOCK), F32)
        dxd_parts, inter_parts = [], []
        dbs, dcs = [], []
        for g in range(SSM_GROUPS):
            gs = slice(GROUP_W * g, GROUP_W * (g + 1))
            b_g = bcv[:, SSM_STATE * g:SSM_STATE * (g + 1)]
            c_g = bcv[:, LANE + SSM_STATE * g:LANE + SSM_STATE * (g + 1)]
            cb = _dot_nt(c_g, b_g)
            dcb = jnp.zeros((BLOCK, BLOCK), F32)
            dxd_h = []
            for hh in range(HEADS_PER_GROUP):
                h = g * HEADS_PER_GROUP + hh
                hs = slice(SSM_HEAD_DIM * h, SSM_HEAD_DIM * (h + 1))
                lm = _decay_matrix(a_cs, a_cs_t, h, lower)
                dy_h = dy[:, hs]
                gl = _dot_nt(dy_h, xd[:, hs]) * lm
                dcb = dcb + gl
                e_h = gl * cb
                ds_to = ds_to + jnp.where(lane_id == h, jnp.sum(e_h, axis=-1, keepdims=True), 0.0)
                ds_from_t = ds_from_t + jnp.where(sub_id == h, jnp.sum(e_h, axis=0, keepdims=True), 0.0)
                dxd_h.append(_dot_tn(cb * lm, dy_h))
            h_g = st_ref[0, gs, :]
            dh_g = dstate[gs, :]
            dys_g = dy[:, gs] * es_full[:, gs]
            xde_g = xd[:, gs] * ed_full[:, gs]
            dcs.append(_dot(dcb, b_g) + _dot(dys_g, h_g))
            dbs.append(_dot_tn(dcb, c_g) + _dot(xde_g, dh_g))
            y_off = _dot_nt(c_g, h_g) * es_full[:, gs]
            dxd_state = _dot_nt(b_g, dh_g) * ed_full[:, gs]
            inter_parts.append(dy[:, gs] * y_off - xd[:, gs] * dxd_state)
            dxd_parts.append(jnp.concatenate(dxd_h, axis=1) + dxd_state)
            dstate[gs, :] = dh_g * dec_full[gs, :] + _dot_tn(dys_g, c_g)
            if g == 0:
                end_dot = hnext[gs, :] * dh_g
            else:
                end_dot = jnp.concatenate([end_dot, hnext[gs, :] * dh_g], axis=0)
        dxd = jnp.concatenate(dxd_parts, axis=1)
        hnext[...] = st_ref[0]

        ds = ds_to - ds_from_t.T + _dot_hi(jnp.concatenate(inter_parts, axis=1), et)
        ds_end = jnp.sum(_dot_tn_hi(end_dot, et), axis=0, keepdims=True)
        rows_l = lax.broadcasted_iota(jnp.int32, (BLOCK, LANE), 0)
        ds = ds + jnp.where(rows_l == BLOCK - 1, ds_end, 0.0)
        row = lax.broadcasted_iota(jnp.int32, (BLOCK, BLOCK), 0)
        col = lax.broadcasted_iota(jnp.int32, (BLOCK, BLOCK), 1)
        dadt = _dot_hi(col >= row, ds, exact="b")
        ddt = dadt * a + _dot_hi(dxd * xc, et)
        dalog_ref[...] += jnp.sum(dadt * dt, axis=0, keepdims=True) * a
        rows = _row_ids(c * BLOCK, BLOCK, LANE)
        lanes = lax.broadcasted_iota(jnp.int32, (BLOCK, LANE), 1)
        live = jnp.logical_and(rows >= FRONT_PAD, lanes < SSM_HEADS)
        ddt_raw = jnp.where(live, ddt * _sigmoid(dtr_ref[...] + dtb_ref[...]), 0.0)
        ddt_ref[...] = ddt_raw
        ddtb_ref[...] += jnp.sum(ddt_raw, axis=0, keepdims=True)

        dxc = dxd * dt_full + dy * dexp
        dpre_x = dxc * (sig_x * (1.0 + pre_x * (1.0 - sig_x)))
        dpre_bc = jnp.concatenate(dbs + dcs, axis=1) * (sig_bc * (1.0 + pre_bc * (1.0 - sig_bc)))
        dcbx_ref[...] += jnp.sum(dpre_x, axis=0, keepdims=True)
        dcbbc_ref[...] += jnp.sum(dpre_bc, axis=0, keepdims=True)
        keep_x = _row_ids(c * BLOCK, BLOCK, SSM_INNER) >= FRONT_PAD
        keep_bc = _row_ids(c * BLOCK, BLOCK, 2 * LANE) >= FRONT_PAD
        prev_live = (c > 0).astype(F32)
        for (dpre, t_ref, s_ref, cur_ref, prv_ref, w_ref, dw_ref, dx_ref, keep) in (
                (dpre_x, tx, sx, xs_ref, xsp_ref, cwx_ref, dcwx_ref, dxs_ref, keep_x),
                (dpre_bc, tbc, sbc, bc_ref, bcp_ref, cwbc_ref, dcwbc_ref, dbc_ref, keep_bc)):
            t_ref[0:BLOCK, :] = dpre
            acc = w_ref[0:1, :] * t_ref[pl.ds(SSM_CONV - 1, BLOCK), :]
            for j in range(1, SSM_CONV):
                acc = acc + w_ref[j:j + 1, :] * t_ref[pl.ds(SSM_CONV - 1 - j, BLOCK), :]
            dx_ref[...] = jnp.where(keep, acc, 0.0)
            t_ref[BLOCK:BLOCK + HALO, :] = dpre[0:HALO, :]
            s_ref[0:HALO, :] = prv_ref[BLOCK - HALO:BLOCK, :] * prev_live
            s_ref[HALO:HALO + BLOCK, :] = cur_ref[...]
            first = HALO - (SSM_CONV - 1)
            for j in range(SSM_CONV):
                dw_ref[j:j + 1, :] += jnp.sum(dpre * s_ref[pl.ds(first + j, BLOCK), :], axis=0, keepdims=True)

        @pl.when(i == nb - 1)
        def _():
            dd_ref[...] = _dot_hi(jnp.broadcast_to(dlane[...], (HALO, SSM_INNER)), et)[0:1, :]

    blk = lambda i: nb - 1 - i
    prv = lambda i: jnp.maximum(nb - 2 - i, 0)
    full = lambda s: pl.BlockSpec(s, lambda i: (0,) * len(s))
    rowblk = lambda w, cidx: pl.BlockSpec((BLOCK, w), lambda i: (blk(i), cidx))
    prvblk = lambda w, cidx: pl.BlockSpec((BLOCK, w), lambda i: (prv(i), cidx))
    body, ex_in, ex_out, ex_shape, ex_scratch = _with_exchange(body, 21, 12, carry, nb)
    outs = pl.pallas_call(
        body, grid=(nb,),
        in_specs=[rowblk(GROUP_W, Q_W // GROUP_W), rowblk(GROUP_W, Q_W // GROUP_W + 1),
                  rowblk(SSM_INNER, COL_Z // SSM_INNER),
                  rowblk(SSM_INNER, COL_XS // SSM_INNER), prvblk(SSM_INNER, COL_XS // SSM_INNER),
                  rowblk(2 * LANE, COL_BC // (2 * LANE)), prvblk(2 * LANE, COL_BC // (2 * LANE)),
                  rowblk(LANE, COL_DT // LANE),
                  rowblk(SSM_INNER, 0), rowblk(2 * LANE, 0), rowblk(LANE, 0), rowblk(SSM_INNER, 0),
                  pl.BlockSpec((1, SSM_INNER, SSM_STATE), lambda i: (blk(i), 0, 0)),
                  full((SSM_CONV, SSM_INNER)), full((SSM_CONV, 2 * LANE)), full((1, LANE)), full((1, LANE)),
                  full((1, SSM_INNER)), full((1, SSM_INNER)), full((LANE, SSM_INNER)), full((SSM_INNER, LANE))] + ex_in,
        out_specs=[rowblk(SSM_INNER, 0), rowblk(SSM_INNER, 0), rowblk(2 * LANE, 0), rowblk(LANE, 0),
                   full((SSM_CONV, SSM_INNER)), full((SSM_CONV, 2 * LANE)), full((1, SSM_INNER)), full((1, 2 * LANE)),
                   full((1, LANE)), full((1, LANE)), full((1, LANE)), full((1, SSM_INNER))] + ex_out,
        out_shape=[jax.ShapeDtypeStruct((m, SSM_INNER), F32), jax.ShapeDtypeStruct((m, SSM_INNER), F32),
                   jax.ShapeDtypeStruct((m, 2 * LANE), F32), jax.ShapeDtypeStruct((m, LANE), F32),
                   jax.ShapeDtypeStruct((SSM_CONV, SSM_INNER), F32), jax.ShapeDtypeStruct((SSM_CONV, 2 * LANE), F32),
                   jax.ShapeDtypeStruct((1, SSM_INNER), F32), jax.ShapeDtypeStruct((1, 2 * LANE), F32),
                   jax.ShapeDtypeStruct((1, LANE), F32), jax.ShapeDtypeStruct((1, LANE), F32),
                   jax.ShapeDtypeStruct((1, LANE), F32), jax.ShapeDtypeStruct((1, SSM_INNER), F32)] + ex_shape,
        scratch_shapes=[pltpu.VMEM((SSM_INNER, SSM_STATE), F32), pltpu.VMEM((SSM_INNER, SSM_STATE), F32),
                        pltpu.VMEM((BLOCK + HALO, SSM_INNER), F32), pltpu.VMEM((BLOCK + HALO, 2 * LANE), F32),
                        pltpu.VMEM((HALO + BLOCK, SSM_INNER), F32), pltpu.VMEM((HALO + BLOCK, 2 * LANE), F32),
                        pltpu.VMEM((1, SSM_INNER), F32)] + ex_scratch,
        name=name, compiler_params=_params("arbitrary"),
    )(dmix, dmix, proj, proj, proj, proj, proj, proj, pre_x, pre_bc, dt, y_pre, states,
      cw_x, cw_bc, dt_bias, a_log, d_exp, norm_w, expand, expand_t, *carry)
    return outs[:12], outs[12:]


CONF_HALO = 32
SUBLANES = 8


def _for_each_window(s, offsets, rows, fn):
    total = s.shape[0]
    assert max(offsets) + rows <= total
    for b in range(SUBLANES):
        offs = [o for o in offsets if o % SUBLANES == b]
        if not offs:
            continue
        rot = s if b == 0 else pltpu.roll(s, total - b, 0)
        for o in offs:
            fn(o, rot[o - b:o - b + rows])


def _glu_masked(v, first_row):
    a = v[:, :D_MODEL]
    s = _sigmoid(v[:, D_MODEL:])
    rows = _row_ids(first_row, v.shape[0], D_MODEL)
    return jnp.where(rows >= FRONT_PAD, a * s, 0.0), a, s


def _layer_norm_stats(c):
    mu = jnp.mean(c, -1, keepdims=True)
    xc = c - mu
    rstd = lax.rsqrt(jnp.mean(xc * xc, -1, keepdims=True) + LN_EPS)
    return xc * rstd, rstd


def conformer_mid_fwd(v, dw_w, dw_b, ln_g, ln_b, name="conf_mid_fwd"):
    m = v.shape[0]
    nb = m // BLOCK
    kpad = dw_w.shape[0]

    def body(vc_ref, vp_ref, w_ref, b_ref, g_ref, beta_ref, c_ref, s_ref):
        i = pl.program_id(0)
        g_prev, _, _ = _glu_masked(vp_ref[BLOCK - CONF_HALO:BLOCK, :], (i - 1) * BLOCK + BLOCK - CONF_HALO)
        g_cur, _, _ = _glu_masked(vc_ref[...], i * BLOCK)
        sg = jnp.concatenate([g_prev * (i > 0).astype(F32), g_cur], axis=0)
        first = CONF_HALO - (CONF_KERNEL - 1)
        acc = [jnp.broadcast_to(b_ref[...], (BLOCK, D_MODEL))]

        def tap(off, win):
            j = off - first
            acc[0] = acc[0] + w_ref[j:j + 1, :] * win

        _for_each_window(sg, [first + j for j in range(CONF_KERNEL)], BLOCK, tap)
        acc = acc[0]
        c_ref[...] = acc
        xhat, _ = _layer_norm_stats(acc)
        nrm = xhat * g_ref[...] + beta_ref[...]
        s_ref[...] = (nrm * _sigmoid(nrm)).astype(s_ref.dtype)

    full = lambda s: pl.BlockSpec(s, lambda i: (0,) * len(s))
    return pl.pallas_call(
        body, grid=(nb,),
        in_specs=[pl.BlockSpec((BLOCK, 2 * D_MODEL), lambda i: (i, 0)),
                  pl.BlockSpec((BLOCK, 2 * D_MODEL), lambda i: (jnp.maximum(i - 1, 0), 0)),
                  full((kpad, D_MODEL)), full((1, D_MODEL)), full((1, D_MODEL)), full((1, D_MODEL))],
        out_specs=[pl.BlockSpec((BLOCK, D_MODEL), lambda i: (i, 0)), pl.BlockSpec((BLOCK, D_MODEL), lambda i: (i, 0))],
        out_shape=[jax.ShapeDtypeStruct((m, D_MODEL), F32), jax.ShapeDtypeStruct((m, D_MODEL), MXU_DTYPE)],
        name=name, compiler_params=_params("arbitrary"),
    )(v, v, dw_w, dw_b, ln_g, ln_b)


def conformer_ln_bwd(ds, c, ln_g, ln_b, name="conf_ln_bwd"):
    m, d = c.shape
    tm = ROW_TILE

    def body(ds_ref, c_ref, g_ref, beta_ref, dc_ref, dg_ref, db_ref):
        @pl.when(pl.program_id(0) == 0)
        def _():
            dg_ref[...] = jnp.zeros_like(dg_ref)
            db_ref[...] = jnp.zeros_like(db_ref)

        xhat, rstd = _layer_norm_stats(c_ref[...])
        g = g_ref[...]
        nrm = xhat * g + beta_ref[...]
        sg = _sigmoid(nrm)
        dn = ds_ref[...] * (sg * (1.0 + nrm * (1.0 - sg)))
        db_ref[...] += jnp.sum(dn, axis=0, keepdims=True)
        dg_ref[...] += jnp.sum(dn * xhat, axis=0, keepdims=True)
        dx = dn * g
        dc_ref[...] = rstd * (dx - jnp.mean(dx, -1, keepdims=True) - xhat * jnp.mean(dx * xhat, -1, keepdims=True))

    row = pl.BlockSpec((tm, d), lambda i: (i, 0))
    vec = pl.BlockSpec((1, d), lambda i: (0, 0))
    return pl.pallas_call(
        body, grid=(m // tm,), in_specs=[row, row, vec, vec], out_specs=[row, vec, vec],
        out_shape=[jax.ShapeDtypeStruct((m, d), F32), jax.ShapeDtypeStruct((1, d), F32), jax.ShapeDtypeStruct((1, d), F32)],
        name=name, compiler_params=_params("arbitrary"),
    )(ds, c, ln_g, ln_b)


def conformer_conv_bwd(dc, v, dw_w, carry=(), name="conf_conv_bwd"):
    m = v.shape[0]
    nb = m // BLOCK
    kpad = dw_w.shape[0]

    def body(dcc_ref, dcn_ref, vc_ref, vp_ref, w_ref, dv_ref, dw_ref, db_ref, dvb_ref):
        i = pl.program_id(0)

        @pl.when(i == 0)
        def _():
            dw_ref[...] = jnp.zeros_like(dw_ref)
            db_ref[...] = jnp.zeros_like(db_ref)
            dvb_ref[...] = jnp.zeros_like(dvb_ref)

        dc_cur = dcc_ref[...]
        tg = jnp.concatenate([dc_cur, dcn_ref[0:CONF_HALO, :] * (i < nb - 1).astype(F32)], axis=0)
        g_prev, _, _ = _glu_masked(vp_ref[BLOCK - CONF_HALO:BLOCK, :], (i - 1) * BLOCK + BLOCK - CONF_HALO)
        g_cur, a, s = _glu_masked(vc_ref[...], i * BLOCK)
        sg = jnp.concatenate([g_prev * (i > 0).astype(F32), g_cur], axis=0)
        db_ref[...] += jnp.sum(dc_cur, axis=0, keepdims=True)
        first = CONF_HALO - (CONF_KERNEL - 1)
        dg_acc = [jnp.zeros((BLOCK, D_MODEL), F32)]

        def tap_dg(off, win):
            j = CONF_KERNEL - 1 - off
            dg_acc[0] = dg_acc[0] + w_ref[j:j + 1, :] * win

        def tap_dw(off, win):
            j = off - first
            dw_ref[j:j + 1, :] += jnp.sum(dc_cur * win, axis=0, keepdims=True)

        _for_each_window(tg, list(range(CONF_KERNEL)), BLOCK, tap_dg)
        _for_each_window(sg, [first + j for j in range(CONF_KERNEL)], BLOCK, tap_dw)
        dg = dg_acc[0]
        rows = _row_ids(i * BLOCK, BLOCK, D_MODEL)
        dg = jnp.where(rows >= FRONT_PAD, dg, 0.0)
        da = dg * s
        dbv = dg * a * (s * (1.0 - s))
        dv = jnp.concatenate([da, dbv], axis=1)
        dv_ref[...] = dv.astype(dv_ref.dtype)
        dvb_ref[...] += jnp.sum(dv, axis=0, keepdims=True)

    full = lambda s: pl.BlockSpec(s, lambda i: (0,) * len(s))
    body, ex_in, ex_out, ex_shape, ex_scratch = _with_exchange(body, 5, 4, carry, nb)
    outs = pl.pallas_call(
        body, grid=(nb,),
        in_specs=[pl.BlockSpec((BLOCK, D_MODEL), lambda i: (i, 0)),
                  pl.BlockSpec((BLOCK, D_MODEL), lambda i: (jnp.minimum(i + 1, nb - 1), 0)),
                  pl.BlockSpec((BLOCK, 2 * D_MODEL), lambda i: (i, 0)),
                  pl.BlockSpec((BLOCK, 2 * D_MODEL), lambda i: (jnp.maximum(i - 1, 0), 0)),
                  full((kpad, D_MODEL))] + ex_in,
        out_specs=[pl.BlockSpec((BLOCK, 2 * D_MODEL), lambda i: (i, 0)), full((kpad, D_MODEL)),
                   full((1, D_MODEL)), full((1, 2 * D_MODEL))] + ex_out,
        out_shape=[jax.ShapeDtypeStruct((m, 2 * D_MODEL), MXU_DTYPE), jax.ShapeDtypeStruct((kpad, D_MODEL), F32),
                   jax.ShapeDtypeStruct((1, D_MODEL), F32), jax.ShapeDtypeStruct((1, 2 * D_MODEL), F32)] + ex_shape,
        scratch_shapes=ex_scratch, name=name, compiler_params=_params("arbitrary"),
    )(dc, dc, v, v, dw_w, *carry)
    return outs[:4], outs[4:]


def _row(v, width=None):
    v = v.reshape(1, -1).astype(F32)
    if width is not None and v.shape[1] < width:
        v = jnp.pad(v, ((0, 0), (0, width - v.shape[1])))
    return v


def _w_in_to_kernel(w):
    pad = jnp.zeros((w.shape[0], PROJ_W - COL_DT - SSM_HEADS), w.dtype)
    return jnp.concatenate([w[:, 768:1792], w[:, 1792:2816], w[:, 0:512], w[:, 2816:3072], w[:, 512:640],
                            w[:, 640:768], w[:, 3072:3088], pad], axis=1)


def _w_in_from_kernel(g):
    return jnp.concatenate([g[:, COL_Q:COL_Q + Q_W], g[:, COL_K:COL_K + KV_W], g[:, COL_V:COL_V + KV_W],
                            g[:, COL_Z:COL_Z + SSM_INNER], g[:, COL_XS:COL_XS + SSM_INNER],
                            g[:, COL_BC:COL_BC + 2 * LANE], g[:, COL_DT:COL_DT + SSM_HEADS]], axis=1)


def even_fwd(h, p, gather_att=(), gather_ssd=()):
    u = rms_fwd(h, p["norm"])
    proj = matmul(u, p["w_in"], name="mm_proj")
    att, got_att = attention_fwd(proj, p["q_norm"], p["k_norm"], p["sinks"], gather=list(gather_att))
    (ssm, pre_x, pre_bc, dt, y_pre, states), got_ssd = ssd_fwd(
        proj, p["cw_x"], p["cw_bc"], p["cb_x"], p["cb_bc"], p["dt_bias"], p["a_log"], p["d_exp"], p["ssm_norm"],
        gather=list(gather_ssd))
    mix = jnp.concatenate([att, ssm], axis=1)
    out = matmul(mix, p["w_out"], b_kind="rowshard", layer=p["layer"], epilogue="resid", extra=h, name="mm_mix_out")
    return out, (h, u, proj, mix, pre_x, pre_bc, dt, y_pre, states), got_att, got_ssd


def even_bwd(dh, p, saved, carry_att=(), carry_ssd=()):
    h, u, proj, mix, pre_x, pre_bc, dt, y_pre, states = saved
    dmix = matmul(dh, p["w_out"], b_kind="rowshard", layer=p["layer"], trans_b=True, name="mm_dmix")
    dw_out = matmul_tn(mix, dh, ti=512, tn=D_MODEL, out_dtype=GRAD_WIRE_DTYPE, name="mm_dw_out")
    dw_out = dw_out.reshape(N_DEV, MIX_W // N_DEV, D_MODEL)
    (dq, dk, dv, dqw, dkw, dsk), got_att = attention_bwd(proj, dmix, p["q_norm"], p["k_norm"], p["sinks"],
                                                         carry=list(carry_att))
    (dz, dxs, dbc, ddt, dcwx, dcwbc, dcbx, dcbbc, ddtb, dalog, dd, dnw), got_ssd = ssd_bwd(
        proj, dmix, pre_x, pre_bc, dt, y_pre, states, p["cw_x"], p["cw_bc"], p["dt_bias"], p["a_log"], p["d_exp"],
        p["ssm_norm"], carry=[dw_out] + list(carry_ssd))
    dproj = jnp.concatenate([dz, dxs, dq, dbc, dk, dv, ddt], axis=1).astype(MXU_DTYPE)
    du = matmul(dproj, p["w_in"], trans_b=True, name="mm_du_in")
    dw_in = matmul_tn(u, dproj, ti=512, tn=PROJ_W, name="mm_dw_in")
    dw_in = _to_shards(_w_in_from_kernel(dw_in), 1).astype(GRAD_WIRE_DTYPE)
    dh_in, dg = rms_bwd(h, p["norm"], du, dh)
    grads = dict(norm=dg, w_in=dw_in, cw_x=dcwx, cw_bc=dcwbc, cb_x=dcbx, cb_bc=dcbbc, dt_bias=ddtb,
                 a_log=dalog, d_skip=dd, ssm_norm=dnw, q_norm=dqw, k_norm=dkw, sinks=dsk)
    return dh_in, grads, got_att, got_ssd


def conf_fwd(h, p):
    u = rms_fwd(h, p["norm"])
    v = mlp_up(u, p["pw1_w"], p["layer"], bias=p["pw1_b"], relu2=False, out_dtype=F32, name="mm_pw1")
    c, s = conformer_mid_fwd(v, p["dw_w"], p["dw_b"], p["ln_g"], p["ln_b"])
    out = matmul(s, p["pw2_w"], b_kind="rowshard", layer=p["layer"], bias=p["pw2_b"], epilogue="resid", extra=h,
                 name="mm_pw2")
    return out, (h, u, v, c, s)


def conf_bwd(dh, p, saved, carry=()):
    h, u, v, c, s = saved
    dpw2_b = col_sum(dh)
    ds = matmul(dh, p["pw2_w"], b_kind="rowshard", layer=p["layer"], trans_b=True, name="mm_ds")
    dpw2_w = matmul_tn(s, dh, ti=D_MODEL, tn=D_MODEL, out_dtype=GRAD_WIRE_DTYPE, name="mm_dpw2")
    dpw2_w = dpw2_w.reshape(N_DEV, D_MODEL // N_DEV, D_MODEL)
    dc, dln_g, dln_b = conformer_ln_bwd(ds, c, p["ln_g"], p["ln_b"])
    (dv, ddw_w, ddw_b, dpw1_b), got = conformer_conv_bwd(dc, v, p["dw_w"], carry=[dpw2_w] + list(carry))
    dpw1_w = mlp_dw_up(u, dv, name="mm_dpw1")
    dh_in, dg = mlp_du_rms_bwd(dv, p["pw1_w"], p["layer"], h, p["norm"], dh, name="mm_du_pw1")
    grads = dict(norm=dg, pw1_w=dpw1_w, pw1_b=dpw1_b, dw_w=ddw_w, dw_b=ddw_b, ln_g=dln_g, ln_b=dln_b, pw2_b=dpw2_b)
    return dh_in, grads, got


def mlp_fwd(h, p):
    u = rms_fwd(h, p["norm"])
    act = mlp_up(u, p["w_up"], p["layer"])
    out = matmul(act, p["w_down"], b_kind="rowshard", layer=p["layer"], epilogue="resid", extra=h, name="mm_down")
    return out, (h, u, act)


def mlp_bwd(dh, p, saved):
    h, u, act = saved
    da = mlp_dact(dh, p["w_down"], act, p["layer"])
    dw_down = mlp_dw_down(act, dh).reshape(N_DEV, FF_BLOCK, D_MODEL)
    dw_up = mlp_dw_up(u, da)
    dh_in, dg = mlp_du_rms_bwd(da, p["w_up"], p["layer"], h, p["norm"], dh)
    return dh_in, dict(norm=dg, w_up=dw_up, w_down=dw_down)


def local_step(x, target, w, shards, first):
    n_even, n_odd = (DEPTH + 1) // 2, DEPTH // 2
    h = jnp.concatenate([jnp.zeros((FRONT_PAD, D_MODEL), F32), w["meta_tokens"].astype(F32), x], axis=0)
    even_p, odd_p, mlp_p = [None] * n_even, [None] * n_odd, [None] * DEPTH

    def even_params(i, g):
        cw = w["ssm_conv_w"][i]
        return dict(
            layer=0, norm=_row(w["mix_norm_even"][i]), w_in=_w_in_to_kernel(_from_shards(g[0][:, 0], 1)), w_out=g[1],
            cw_x=cw[:, :SSM_INNER], cw_bc=cw[:, SSM_INNER:], cb_x=_row(w["ssm_conv_b"][i][:SSM_INNER]),
            cb_bc=_row(w["ssm_conv_b"][i][SSM_INNER:]), dt_bias=_row(w["dt_bias"][i], LANE),
            a_log=_row(w["a_log"][i], LANE), d_exp=_row(jnp.repeat(w["d_skip"][i], SSM_HEAD_DIM)),
            ssm_norm=_row(w["ssm_norm_w"][i]), q_norm=_row(jnp.tile(w["q_norm"][i], ATT_HEADS)),
            k_norm=_row(jnp.tile(w["k_norm"][i], ATT_KV_HEADS)), sinks=w["sinks"][i].astype(F32))

    def odd_params(i, g):
        return dict(
            layer=0, norm=_row(w["mix_norm_odd"][i]), pw1_w=g[0], pw1_b=_row(w["pw1_b"][i]),
            dw_w=jnp.pad(w["dw_w"][i], ((0, CONF_HALO - CONF_KERNEL), (0, 0))), dw_b=_row(w["dw_b"][i]),
            ln_g=_row(w["ln_g"][i]), ln_b=_row(w["ln_b"][i]), pw2_w=g[1], pw2_b=_row(w["pw2_b"][i]))

    gathered = {0: first}
    tape = []
    for layer in range(DEPTH):
        g = gathered.pop(layer)
        mlp_p[layer] = dict(layer=0, norm=_row(w["mlp_norm"][layer]), w_up=g[2], w_down=g[3])
        if layer % 2 == 0:
            even_p[layer // 2] = even_params(layer // 2, g)
            ahead = [l for l in (layer + 1, layer + 2) if l < DEPTH and l not in gathered]
            ride_att = shards[ahead[0]] if len(ahead) > 0 else ()
            ride_ssd = shards[ahead[1]] if len(ahead) > 1 else ()
            h, saved, got_att, got_ssd = even_fwd(h, even_p[layer // 2], gather_att=ride_att, gather_ssd=ride_ssd)
            for l, got in zip(ahead, (got_att, got_ssd)):
                gathered[l] = got
        else:
            odd_p[layer // 2] = odd_params(layer // 2, g)
            h, saved = conf_fwd(h, odd_p[layer // 2])
        tape.append(saved)
        h, saved = mlp_fwd(h, mlp_p[layer])
        tape.append(saved)
    dh, loss_row = loss_fwd_bwd(h, target)

    ge = [None] * n_even
    go = [None] * n_odd
    gm = [None] * DEPTH
    received = {n: [None] * shape[0] for n, shape, _ in PARAMS if n in MATMUL_WEIGHTS}
    pending = []

    def store(tags, arrays):
        for (n, l), a in zip(tags, arrays):
            received[n][l] = a

    for layer in reversed(range(DEPTH)):
        i = layer // 2
        dh, gm[layer] = mlp_bwd(dh, mlp_p[layer], tape.pop())
        mlp_tags = [("w_up", layer), ("w_down", layer)]
        mlp_parts = [gm[layer]["w_up"], gm[layer]["w_down"]]
        if layer % 2 == 0:
            riders, pending = pending, []
            dh, ge[i], got_att, got_ssd = even_bwd(dh, even_p[i], tape.pop(), carry_att=mlp_parts,
                                                   carry_ssd=[a for _, _, a in riders])
            store(mlp_tags, got_att)
            store([("w_out", i)] + [(n, l) for n, l, _ in riders], got_ssd)
            pending.append(("w_in", i, ge[i]["w_in"]))
        else:
            dh, go[i], got = conf_bwd(dh, odd_p[i], tape.pop(), carry=mlp_parts)
            store([("pw2_w", i)] + mlp_tags, got)
            pending.append(("pw1_w", i, go[i]["pw1_w"]))

    stack = lambda gs, f: jnp.stack([f(g) for g in gs])
    grads = dict(
        meta_tokens=dh[FRONT_PAD:BLOCK],
        mix_norm_even=stack(ge, lambda g: g["norm"][0]),
        ssm_conv_w=stack(ge, lambda g: jnp.concatenate([g["cw_x"], g["cw_bc"]], axis=1)),
        ssm_conv_b=stack(ge, lambda g: jnp.concatenate([g["cb_x"][0], g["cb_bc"][0]])),
        dt_bias=stack(ge, lambda g: g["dt_bias"][0, :SSM_HEADS]),
        a_log=stack(ge, lambda g: g["a_log"][0, :SSM_HEADS]),
        d_skip=stack(ge, lambda g: g["d_skip"][0, :SSM_HEADS]),
        ssm_norm_w=stack(ge, lambda g: g["ssm_norm"][0]),
        q_norm=stack(ge, lambda g: g["q_norm"][0, :HEAD_DIM]),
        k_norm=stack(ge, lambda g: g["k_norm"][0, :HEAD_DIM]),
        sinks=stack(ge, lambda g: g["sinks"][0, :ATT_HEADS]),
        mix_norm_odd=stack(go, lambda g: g["norm"][0]),
        pw1_b=stack(go, lambda g: g["pw1_b"][0]),
        dw_w=stack(go, lambda g: g["dw_w"][:CONF_KERNEL]),
        dw_b=stack(go, lambda g: g["dw_b"][0]),
        ln_g=stack(go, lambda g: g["ln_g"][0]),
        ln_b=stack(go, lambda g: g["ln_b"][0]),
        pw2_b=stack(go, lambda g: g["pw2_b"][0]),
        mlp_norm=stack(gm, lambda g: g["norm"][0]),
    )
    return loss_row[0, 0], dh[BLOCK:], grads, received, pending


PARAMS = (
    ("meta_tokens", (16, 1024), 1), ("mix_norm_even", (2, 1024), None), ("w_in", (2, 1024, 3088), 2),
    ("ssm_conv_w", (2, 4, 1280), 2), ("ssm_conv_b", (2, 1280), None), ("dt_bias", (2, 16), None),
    ("a_log", (2, 16), None), ("d_skip", (2, 16), None), ("ssm_norm_w", (2, 1024), None), ("q_norm", (2, 64), None),
    ("k_norm", (2, 64), None), ("sinks", (2, 8), None), ("w_out", (2, 1536, 1024), 1), ("mix_norm_odd", (2, 1024), 1),
    ("pw1_w", (2, 1024, 2048), 2), ("pw1_b", (2, 2048), 1), ("dw_w", (2, 31, 1024), 2), ("dw_b", (2, 1024), 1),
    ("ln_g", (2, 1024), 1), ("ln_b", (2, 1024), 1), ("pw2_w", (2, 1024, 1024), 1), ("pw2_b", (2, 1024), 1),
    ("mlp_norm", (4, 1024), None), ("w_up", (4, 1024, 4096), 2), ("w_down", (4, 4096, 1024), 1),
)
MATMUL_WEIGHTS = ("w_in", "w_out", "pw1_w", "pw2_w", "w_up", "w_down")
PACK_ROW_ALIGN = 16 * PACK_W


def _block_shape(shape, axis):
    if axis is None:
        return tuple(shape)
    return tuple(s // N_DEV if a == axis else s for a, s in enumerate(shape))


def _numel(shape):
    return math.prod(shape)


def _pack(arrays, dtype):
    flat = jnp.concatenate([a.reshape(-1).astype(dtype) for a in arrays])
    n = flat.shape[0]
    padded = -(-n // PACK_ROW_ALIGN) * PACK_ROW_ALIGN
    return jnp.pad(flat, (0, padded - n)).reshape(-1, PACK_W)


def _pack_rows(arrays_by_dev, dtype):
    flat = jnp.concatenate([a.reshape(N_DEV, -1).astype(dtype) for a in arrays_by_dev], axis=1)
    n = flat.shape[1]
    padded = -(-n // PACK_ROW_ALIGN) * PACK_ROW_ALIGN
    return jnp.pad(flat, ((0, 0), (0, padded - n))).reshape(N_DEV, -1, PACK_W)


def _to_shards(full, axis):
    shape = full.shape
    split = full.reshape(shape[:axis] + (N_DEV, shape[axis] // N_DEV) + shape[axis + 1:])
    return jnp.moveaxis(split, axis, 0)


def _from_shards(blocks, axis):
    moved = jnp.moveaxis(blocks, 0, axis)
    shape = moved.shape
    return moved.reshape(shape[:axis] + (shape[axis] * shape[axis + 1],) + shape[axis + 2:])


_MESH = pl.DeviceIdType.MESH
_ANY = pl.BlockSpec(memory_space=pl.ANY)


def _mesh_place():
    x, y, c = lax.axis_index("x"), lax.axis_index("y"), lax.axis_index("c")
    return x, y, c


def _peer(x, y, c, rel):
    dx, dy, dc = (rel >> 2) & 1, (rel >> 1) & 1, rel & 1
    return (x ^ dx if dx else x, y ^ dy if dy else y, c ^ dc if dc else c)


def _dev_index(x, y, c):
    return 4 * x + 2 * y + c


def all_gather_weights(bigs, small):
    nt = len(bigs)

    def body(*refs):
        big_refs, small_ref = refs[:nt], refs[nt]
        big_outs, small_out = refs[nt + 1:2 * nt + 1], refs[2 * nt + 1]
        send_sems, recv_sems, small_send, small_recv, local_sems = refs[2 * nt + 2:]
        x, y, c = _mesh_place()
        me = (x, y, c)
        sibling = (x, y, 1 - c)
        chips = [(1 - x, y), (x, 1 - y), (1 - x, 1 - y)]

        def big_copy(t, k, block, to, from_input=False):
            dst = big_outs[t].at[_dev_index(*block)]
            return pltpu.make_async_remote_copy(src_ref=big_refs[t] if from_input else dst, dst_ref=dst,
                                                send_sem=send_sems.at[t, k], recv_sem=recv_sems.at[t, k],
                                                device_id=to, device_id_type=_MESH)

        def small_copy(rel, block, to):
            return pltpu.make_async_remote_copy(src_ref=small_ref, dst_ref=small_out.at[_dev_index(*block)],
                                                send_sem=small_send.at[rel - 1], recv_sem=small_recv.at[rel - 1],
                                                device_id=to, device_id_type=_MESH)

        mine = [pltpu.make_async_copy(big_refs[t], big_outs[t].at[_dev_index(*me)], local_sems.at[t]) for t in range(nt)]
        mine.append(pltpu.make_async_copy(small_ref, small_out.at[_dev_index(*me)], local_sems.at[nt]))
        for cp in mine:
            cp.start()
        first = []
        for t in range(nt):
            first.append(big_copy(t, 0, me, sibling, from_input=True))
            first += [big_copy(t, 1 + j, me, (*chip, c), from_input=True) for j, chip in enumerate(chips)]
        for cp in first:
            cp.start()
        smalls = [small_copy(rel, me, _peer(x, y, c, rel)) for rel in range(1, N_DEV)]
        for cp in smalls:
            cp.start()
        passed = []
        for j, chip in enumerate(chips):
            for t in range(nt):
                big_copy(t, 1 + j, (*chip, c), me).wait_recv()
                fwd = big_copy(t, 4 + j, (*chip, c), sibling)
                fwd.start()
                passed.append(fwd)
        for t in range(nt):
            big_copy(t, 0, sibling, me).wait_recv()
            for j, chip in enumerate(chips):
                big_copy(t, 4 + j, (*chip, 1 - c), me).wait_recv()
        for rel in range(1, N_DEV):
            small_copy(rel, _peer(x, y, c, rel), me).wait_recv()
        for cp in first + passed + smalls:
            cp.wait_send()
        for cp in mine:
            cp.wait()

    return pl.pallas_call(
        body, in_specs=[_ANY] * (nt + 1), out_specs=[_ANY] * (nt + 1),
        out_shape=[jax.ShapeDtypeStruct((N_DEV,) + b.shape, b.dtype) for b in bigs]
        + [jax.ShapeDtypeStruct((N_DEV,) + small.shape, small.dtype)],
        scratch_shapes=[pltpu.SemaphoreType.DMA((nt, N_DEV - 1)), pltpu.SemaphoreType.DMA((nt, N_DEV - 1)),
                        pltpu.SemaphoreType.DMA((N_DEV - 1,)), pltpu.SemaphoreType.DMA((N_DEV - 1,)),
                        pltpu.SemaphoreType.DMA((nt + 1,))],
        name="all_gather_weights",
    )(*bigs, small)


def _gather_copies(in_refs, out_refs, send_sems, recv_sems, local_sems):
    x, y, c = _mesh_place()
    me = (x, y, c)
    sibling = (x, y, 1 - c)
    chips = [(1 - x, y), (x, 1 - y), (1 - x, 1 - y)]
    nt = len(in_refs)

    def copy(t, k, block, to, from_input=False):
        dst = out_refs[t].at[_dev_index(*block)]
        return pltpu.make_async_remote_copy(src_ref=in_refs[t] if from_input else dst, dst_ref=dst,
                                            send_sem=send_sems.at[t, k], recv_sem=recv_sems.at[t, k],
                                            device_id=to, device_id_type=_MESH)

    mine = [pltpu.make_async_copy(in_refs[t], out_refs[t].at[_dev_index(*me)], local_sems.at[t]) for t in range(nt)]
    first, landed, forward, last = [], [], [], []
    for t in range(nt):
        first.append(copy(t, 0, me, sibling, from_input=True))
        last.append(copy(t, 0, sibling, me))
        for j, chip in enumerate(chips):
            first.append(copy(t, 1 + j, me, (*chip, c), from_input=True))
            landed.append(copy(t, 1 + j, (*chip, c), me))
            forward.append(copy(t, 4 + j, (*chip, c), sibling))
            last.append(copy(t, 4 + j, (*chip, 1 - c), me))
    return mine, first, landed, forward, last


GATHER_FORWARD_LEAD = 8


def _with_gather(body, n_in, n_out, shards, steps):
    n = len(shards)
    if n == 0:
        return body, [], [], [], []
    fwd_step = max(steps - 1 - GATHER_FORWARD_LEAD, 0)

    def wrapped(*refs):
        ins, g_in = refs[:n_in], refs[n_in:n_in + n]
        outs, g_out = refs[n_in + n:n_in + n + n_out], refs[n_in + n + n_out:n_in + 2 * n + n_out]
        scratch = refs[n_in + 2 * n + n_out:len(refs) - 3]
        sems = refs[len(refs) - 3:]
        i = pl.program_id(0)

        @pl.when(i == 0)
        def _():
            mine, first, _, _, _ = _gather_copies(g_in, g_out, *sems)
            for cp in mine + first:
                cp.start()

        @pl.when(i == fwd_step)
        def _():
            _, _, landed, forward, _ = _gather_copies(g_in, g_out, *sems)
            for arrived, onward in zip(landed, forward):
                arrived.wait_recv()
                onward.start()

        body(*ins, *outs, *scratch)

        @pl.when(i == steps - 1)
        def _():
            mine, first, _, forward, last = _gather_copies(g_in, g_out, *sems)
            for cp in last:
                cp.wait_recv()
            for cp in first + forward:
                cp.wait_send()
            for cp in mine:
                cp.wait()

    return (wrapped, [_ANY] * n, [_ANY] * n, [jax.ShapeDtypeStruct((N_DEV,) + a.shape, a.dtype) for a in shards],
            [pltpu.SemaphoreType.DMA((n, N_DEV - 1)), pltpu.SemaphoreType.DMA((n, N_DEV - 1)),
             pltpu.SemaphoreType.DMA((n,))])


def _exchange_copies(in_refs, out_refs, send_sems, recv_sems, local_sems):
    x, y, c = _mesh_place()
    me = _dev_index(x, y, c)
    mine, sends, arrivals = [], [], []
    for p, (src, dst) in enumerate(zip(in_refs, out_refs)):
        mine.append(pltpu.make_async_copy(src.at[me], dst.at[me], local_sems.at[p]))
        for rel in range(1, N_DEV):
            peer = _peer(x, y, c, rel)
            there = _dev_index(*peer)
            sems = dict(send_sem=send_sems.at[rel - 1, p], recv_sem=recv_sems.at[rel - 1, p], device_id=peer,
                        device_id_type=_MESH)
            sends.append(pltpu.make_async_remote_copy(src_ref=src.at[there], dst_ref=dst.at[me], **sems))
            arrivals.append(pltpu.make_async_remote_copy(src_ref=src.at[me], dst_ref=dst.at[there], **sems))
    return mine, sends, arrivals


def _with_exchange(body, n_in, n_out, carry, steps):
    n = len(carry)
    if n == 0:
        return body, [], [], [], []

    def wrapped(*refs):
        ins, ex_in = refs[:n_in], refs[n_in:n_in + n]
        outs, ex_out = refs[n_in + n:n_in + n + n_out], refs[n_in + n + n_out:n_in + 2 * n + n_out]
        scratch = refs[n_in + 2 * n + n_out:len(refs) - 3]
        send_sems, recv_sems, local_sems = refs[len(refs) - 3:]
        i = pl.program_id(0)

        @pl.when(i == 0)
        def _():
            mine, sends, _ = _exchange_copies(ex_in, ex_out, send_sems, recv_sems, local_sems)
            for cp in mine + sends:
                cp.start()

        body(*ins, *outs, *scratch)

        @pl.when(i == steps - 1)
        def _():
            mine, sends, arrivals = _exchange_copies(ex_in, ex_out, send_sems, recv_sems, local_sems)
            for cp in arrivals:
                cp.wait_recv()
            for cp in sends:
                cp.wait_send()
            for cp in mine:
                cp.wait()

    return (wrapped, [_ANY] * n, [_ANY] * n, [jax.ShapeDtypeStruct(a.shape, a.dtype) for a in carry],
            [pltpu.SemaphoreType.DMA((N_DEV - 1, n)), pltpu.SemaphoreType.DMA((N_DEV - 1, n)),
             pltpu.SemaphoreType.DMA((n,))])


def exchange_gradients(arrays):
    n = len(arrays)

    def body(*refs):
        mine, sends, arrivals = _exchange_copies(refs[:n], refs[n:2 * n], *refs[2 * n:])
        for cp in mine + sends:
            cp.start()
        for cp in arrivals:
            cp.wait_recv()
        for cp in sends:
            cp.wait_send()
        for cp in mine:
            cp.wait()

    return pl.pallas_call(
        body, in_specs=[_ANY] * n, out_specs=[_ANY] * n,
        out_shape=[jax.ShapeDtypeStruct(a.shape, a.dtype) for a in arrays],
        scratch_shapes=[pltpu.SemaphoreType.DMA((N_DEV - 1, n)), pltpu.SemaphoreType.DMA((N_DEV - 1, n)),
                        pltpu.SemaphoreType.DMA((n,))],
        name="exchange_gradients",
    )(*arrays)


def reduce_adamw(parts, w, m, v, tr):
    nl, r, cols = w.shape
    assert len(parts) == nl

    def body(*refs):
        p_refs = refs[:nl]
        w_ref, m_ref, v_ref, g_ref, d_ref, nm_ref, nv_ref, g_acc = refs[nl:]
        layer = pl.program_id(0)
        for l in range(nl):
            @pl.when(layer == l)
            def _(l=l):
                g = p_refs[l][0].astype(F32)
                for d in range(1, N_DEV):
                    g = g + p_refs[l][d].astype(F32)
                g_acc[...] = g

        g = g_acc[...]
        g_ref[...] = g
        nm = ADAM_B1 * m_ref[...] + (1.0 - ADAM_B1) * g
        nv = ADAM_B2 * v_ref[...] + (1.0 - ADAM_B2) * (g * g)
        nm_ref[...] = nm
        nv_ref[...] = nv
        m_hat = nm / (1.0 - ADAM_B1 ** ADAM_STEP)
        v_hat = nv / (1.0 - ADAM_B2 ** ADAM_STEP)
        d_ref[...] = -ADAM_LR * (m_hat / (jnp.sqrt(v_hat) + ADAM_EPS) + ADAM_WD * w_ref[...])

    row = pl.BlockSpec((None, tr, cols), lambda l, i: (l, i, 0))

    def part_spec(own):
        def index(l, i):
            return (0, jnp.where(l == own, i, jnp.where(l < own, 0, r // tr - 1)), 0)
        return pl.BlockSpec((N_DEV, tr, cols), index)

    return pl.pallas_call(
        body, grid=(nl, r // tr),
        in_specs=[part_spec(l) for l in range(nl)] + [row, row, row],
        out_specs=[row, row, row, row], out_shape=[jax.ShapeDtypeStruct((nl, r, cols), F32)] * 4,
        scratch_shapes=[pltpu.VMEM((tr, cols), F32)],
        name="reduce_adamw", compiler_params=_params("arbitrary", "arbitrary"),
    )(*parts, w, m, v)


ADAMW_TILE_BYTES = 1 << 19


def _adamw_tile(rows, cols):
    lanes = -(-cols // LANE) * LANE
    best = None
    for tr in range(16, rows + 1, 16):
        if rows % tr == 0 and tr * lanes * 4 <= ADAMW_TILE_BYTES:
            best = tr
    if best is None:
        raise ValueError((rows, cols))
    return best


def kernel(x, meta_tokens, mix_norm_even, w_in, ssm_conv_w, ssm_conv_b, dt_bias, a_log, d_skip, ssm_norm_w, q_norm, k_norm, sinks, w_out, mix_norm_odd, pw1_w, pw1_b, dw_w, dw_b, ln_g, ln_b, pw2_w, pw2_b, mlp_norm, w_up, w_down, loss_target, m_meta_tokens, m_mix_norm_even, m_w_in, m_ssm_conv_w, m_ssm_conv_b, m_dt_bias, m_a_log, m_d_skip, m_ssm_norm_w, m_q_norm, m_k_norm, m_sinks, m_w_out, m_mix_norm_odd, m_pw1_w, m_pw1_b, m_dw_w, m_dw_b, m_ln_g, m_ln_b, m_pw2_w, m_pw2_b, m_mlp_norm, m_w_up, m_w_down, v_meta_tokens, v_mix_norm_even, v_w_in, v_ssm_conv_w, v_ssm_conv_b, v_dt_bias, v_a_log, v_d_skip, v_ssm_norm_w, v_q_norm, v_k_norm, v_sinks, v_w_out, v_mix_norm_odd, v_pw1_w, v_pw1_b, v_dw_w, v_dw_b, v_ln_g, v_ln_b, v_pw2_w, v_pw2_b, v_mlp_norm, v_w_up, v_w_down):
    names = [p[0] for p in PARAMS]
    w_loc = dict(zip(names, (meta_tokens, mix_norm_even, w_in, ssm_conv_w, ssm_conv_b, dt_bias, a_log, d_skip, ssm_norm_w, q_norm, k_norm, sinks, w_out, mix_norm_odd, pw1_w, pw1_b, dw_w, dw_b, ln_g, ln_b, pw2_w, pw2_b, mlp_norm, w_up, w_down)))
    m_loc = dict(zip(names, (m_meta_tokens, m_mix_norm_even, m_w_in, m_ssm_conv_w, m_ssm_conv_b, m_dt_bias, m_a_log, m_d_skip, m_ssm_norm_w, m_q_norm, m_k_norm, m_sinks, m_w_out, m_mix_norm_odd, m_pw1_w, m_pw1_b, m_dw_w, m_dw_b, m_ln_g, m_ln_b, m_pw2_w, m_pw2_b, m_mlp_norm, m_w_up, m_w_down)))
    v_loc = dict(zip(names, (v_meta_tokens, v_mix_norm_even, v_w_in, v_ssm_conv_w, v_ssm_conv_b, v_dt_bias, v_a_log, v_d_skip, v_ssm_norm_w, v_q_norm, v_k_norm, v_sinks, v_w_out, v_mix_norm_odd, v_pw1_w, v_pw1_b, v_dw_w, v_dw_b, v_ln_g, v_ln_b, v_pw2_w, v_pw2_b, v_mlp_norm, v_w_up, v_w_down)))
    small_sharded = [p for p in PARAMS if p[2] is not None and p[0] not in MATMUL_WEIGHTS]
    replicated = [p for p in PARAMS if p[2] is None]
    small_list = small_sharded + replicated

    def layer_shards(layer):
        i = layer // 2
        mixer = ("w_in", "w_out") if layer % 2 == 0 else ("pw1_w", "pw2_w")
        return [w_loc[n][i:i + 1].astype(MXU_DTYPE) for n in mixer] + [
            w_loc[n][layer:layer + 1].astype(MXU_DTYPE) for n in ("w_up", "w_down")]

    shards = [layer_shards(layer) for layer in range(DEPTH)]
    gathered = all_gather_weights(shards[0], _pack([w_loc[n] for n, _, _ in small_sharded], F32))
    w_full = {n: w_loc[n] for n, _, _ in replicated}
    flat = gathered[-1].reshape(N_DEV, -1)
    off = 0
    for n, shape, axis in small_sharded:
        blk = _block_shape(shape, axis)
        w_full[n] = _from_shards(flat[:, off:off + _numel(blk)].reshape((N_DEV,) + blk), axis)
        off += _numel(blk)

    loss_local, grad_x, g_full, received, pending = local_step(x[0], loss_target[0], w_full, shards, gathered[:-1])
    loss = lax.psum(loss_local, ("x", "y", "c"))

    by_dev = [_to_shards(g_full[n], axis) for n, _, axis in small_sharded]
    by_dev += [jnp.broadcast_to(g_full[n][None], (N_DEV,) + tuple(shape)) for n, shape, _ in replicated]
    last = exchange_gradients([a for _, _, a in pending] + [_pack_rows(by_dev, F32)])
    for (n, l, _), a in zip(pending, last[:-1]):
        received[n][l] = a

    out = {}
    for n in MATMUL_WEIGHTS:
        nl, r, cols = w_loc[n].shape
        out[n] = reduce_adamw(received[n], w_loc[n], m_loc[n], v_loc[n], _adamw_tile(r, cols))
    pk = lambda d: _pack([d[n] for n, _, _ in small_list], F32)[None]
    rows = last[-1].shape[1]
    small_out = reduce_adamw([last[-1]], pk(w_loc), pk(m_loc), pk(v_loc), _adamw_tile(rows, PACK_W))
    flats = [buf.reshape(-1) for buf in small_out]
    off = 0
    for n, shape, axis in small_list:
        blk = _block_shape(shape, axis)
        out[n] = tuple(f[off:off + _numel(blk)].reshape(blk) for f in flats)
        off += _numel(blk)
    return (loss, grad_x[None], *[out[n][0] for n in names], *[out[n][1] for n in names],
            *[out[n][2] for n in names], *[out[n][3] for n in names])
```

```python
import math

import jax
import jax.numpy as jnp
from jax import lax
from jax.experimental import pallas as pl
from jax.experimental.pallas import tpu as pltpu

F32 = jnp.float32
MXU_DTYPE = jnp.bfloat16
GRAD_WIRE_DTYPE = jnp.bfloat16
HIGHEST = lax.Precision.HIGHEST

D_MODEL = 1024
N_META = 16
BLOCK = 128
FRONT_PAD = BLOCK - N_META
ATT_HEADS = 8
ATT_KV_HEADS = 2
HEAD_DIM = 64
SSM_HEADS = 16
SSM_HEAD_DIM = 64
SSM_INNER = 1024
SSM_GROUPS = 2
SSM_STATE = 64
SSM_CONV = 4
CONF_KERNEL = 31
D_FF = 4096
EPS = 1e-6
LN_EPS = 1e-5
Q_W = 512
KV_W = 128
IN_W = 3088
MIX_W = 1536
DEPTH = 4
N_DEV = 8

ADAM_LR = 0.001
ADAM_B1 = 0.9
ADAM_B2 = 0.999
ADAM_EPS = 1e-08
ADAM_WD = 0.01
ADAM_STEP = 10

PROJ_W = 3200
COL_Z, COL_XS, COL_Q, COL_BC, COL_K, COL_V, COL_DT = 0, 1024, 2048, 2560, 2816, 2944, 3072

ROW_TILE = 640
TN_ROW_TILE = 1664
ACC_BYTES = 8 * 1024 * 1024
VMEM_LIMIT = 56 * 1024 * 1024
LANE = 128
PACK_W = 1024


def _params(*sem):
    return pltpu.CompilerParams(dimension_semantics=sem, vmem_limit_bytes=VMEM_LIMIT)


def _mx(x):
    return x.astype(MXU_DTYPE)


def _dot(a, b):
    return jnp.dot(_mx(a), _mx(b), preferred_element_type=F32)


def _dot_nt(a, b):
    return lax.dot_general(_mx(a), _mx(b), (((1,), (1,)), ((), ())), preferred_element_type=F32)


def _dot_tn(a, b):
    return lax.dot_general(_mx(a), _mx(b), (((0,), (0,)), ((), ())), preferred_element_type=F32)


def _split3(x):
    hi = x.astype(jnp.bfloat16)
    r1 = x - hi.astype(F32)
    mid = r1.astype(jnp.bfloat16)
    lo = (r1 - mid.astype(F32)).astype(jnp.bfloat16)
    return hi, mid, lo


def _sel_dot(x, sel, dims):
    x_first = dims[2]
    if sel.dtype == jnp.bool_:
        sel = jnp.where(sel, 1.0, 0.0)
    one = sel.astype(jnp.bfloat16)
    acc = None
    for part in _split3(x):
        args = (part, one) if x_first else (one, part)
        t = lax.dot_general(*args, (dims[:2], ((), ())), preferred_element_type=F32)
        acc = t if acc is None else acc + t
    return acc


def _dot_hi(a, b, exact="a"):
    if exact == "a":
        return _sel_dot(a, b, ((1,), (0,), True))
    return _sel_dot(b, a, ((1,), (0,), False))


def _dot_tn_hi(a, b):
    return _sel_dot(a, b, ((0,), (0,), True))


def _sigmoid(x):
    return 1.0 / (1.0 + jnp.exp(-x))


def _row_ids(start, rows, cols):
    return start + lax.broadcasted_iota(jnp.int32, (rows, cols), 0)


def rms_fwd(h, g, name="rms_fwd"):
    m, d = h.shape
    tm = ROW_TILE

    def body(h_ref, g_ref, u_ref):
        x = h_ref[...]
        r = lax.rsqrt(jnp.mean(x * x, -1, keepdims=True) + EPS)
        u_ref[...] = ((x * r) * g_ref[...]).astype(u_ref.dtype)

    return pl.pallas_call(
        body, grid=(m // tm,),
        in_specs=[pl.BlockSpec((tm, d), lambda i: (i, 0)), pl.BlockSpec((1, d), lambda i: (0, 0))],
        out_specs=pl.BlockSpec((tm, d), lambda i: (i, 0)),
        out_shape=jax.ShapeDtypeStruct((m, d), MXU_DTYPE), name=name, compiler_params=_params("arbitrary"),
    )(h, g)


def rms_bwd(h, g, du, dh_out, name="rms_bwd"):
    m, d = h.shape
    tm = ROW_TILE

    def body(h_ref, g_ref, du_ref, dho_ref, dh_ref, dg_ref):
        @pl.when(pl.program_id(0) == 0)
        def _():
            dg_ref[...] = jnp.zeros_like(dg_ref)

        x = h_ref[...]
        du_ = du_ref[...]
        r = lax.rsqrt(jnp.mean(x * x, -1, keepdims=True) + EPS)
        gy = du_ * g_ref[...]
        dx = r * gy - x * ((r * r * r) * jnp.mean(x * gy, -1, keepdims=True))
        dh_ref[...] = dho_ref[...] + dx
        dg_ref[...] += jnp.sum(du_ * (x * r), axis=0, keepdims=True)

    row = pl.BlockSpec((tm, d), lambda i: (i, 0))
    vec = pl.BlockSpec((1, d), lambda i: (0, 0))
    return pl.pallas_call(
        body, grid=(m // tm,), in_specs=[row, vec, row, row], out_specs=[row, vec],
        out_shape=[jax.ShapeDtypeStruct((m, d), F32), jax.ShapeDtypeStruct((1, d), F32)],
        name=name, compiler_params=_params("arbitrary"),
    )(h, g, du, dh_out)


def loss_fwd_bwd(h, target, name="loss"):
    m, d = h.shape
    nb = m // BLOCK

    def body(h_ref, t_ref, dh_ref, l_ref):
        i = pl.program_id(0)

        @pl.when(i == 0)
        def _():
            l_ref[...] = jnp.zeros_like(l_ref)
            dh_ref[...] = jnp.zeros_like(dh_ref)

        @pl.when(i > 0)
        def _():
            e = h_ref[...] - t_ref[...]
            dh_ref[...] = e * (1.0 / d)
            s = jnp.sum(jnp.sum(e * e, axis=-1, keepdims=True), axis=0, keepdims=True)
            l_ref[...] += jnp.broadcast_to(s * (0.5 / d), l_ref.shape)

    return pl.pallas_call(
        body, grid=(nb,),
        in_specs=[pl.BlockSpec((BLOCK, d), lambda i: (i, 0)),
                  pl.BlockSpec((BLOCK, d), lambda i: (jnp.maximum(i - 1, 0), 0))],
        out_specs=[pl.BlockSpec((BLOCK, d), lambda i: (i, 0)), pl.BlockSpec((1, LANE), lambda i: (0, 0))],
        out_shape=[jax.ShapeDtypeStruct((m, d), F32), jax.ShapeDtypeStruct((1, LANE), F32)],
        name=name, compiler_params=_params("arbitrary"),
    )(h, target)


def col_sum(x, name="col_sum"):
    m, n = x.shape
    tm = ROW_TILE

    def body(x_ref, o_ref):
        @pl.when(pl.program_id(0) == 0)
        def _():
            o_ref[...] = jnp.zeros_like(o_ref)

        o_ref[...] += jnp.sum(x_ref[...].astype(F32), axis=0, keepdims=True)

    return pl.pallas_call(
        body, grid=(m // tm,), in_specs=[pl.BlockSpec((tm, n), lambda i: (i, 0))],
        out_specs=pl.BlockSpec((1, n), lambda i: (0, 0)), out_shape=jax.ShapeDtypeStruct((1, n), F32),
        name=name, compiler_params=_params("arbitrary"),
    )(x)


def matmul(a, b, *, b_kind="full", layer=0, trans_b=False, tn=None, epilogue=None, bias=None, extra=None,
           out_dtype=F32, name="matmul"):
    m, k = a.shape
    tm = ROW_TILE
    merge = False
    if b_kind == "full":
        n = b.shape[0] if trans_b else b.shape[1]
        tn = n if tn is None else tn
        b_spec = pl.BlockSpec((tn, k), lambda i, j: (j, 0)) if trans_b else pl.BlockSpec((k, tn), lambda i, j: (0, j))
    elif b_kind == "rowshard":
        ks, wn = b.shape[2], b.shape[3]
        if trans_b and tn == ks:
            assert wn == k
            n = N_DEV * ks
            b_spec = pl.BlockSpec((None, None, ks, wn), lambda i, j: (j, layer, 0, 0))
        else:
            assert tn is None
            merge = True
            n = N_DEV * ks if trans_b else wn
            assert (wn if trans_b else N_DEV * ks) == k
            tn = n
            b_spec = pl.BlockSpec((N_DEV, None, ks, wn), lambda i, j: (0, layer, 0, 0))
    else:
        raise ValueError(b_kind)
    has_bias = bias is not None
    has_extra = extra is not None

    def body(*refs):
        a_ref, b_ref = refs[0], refs[1]
        pos = 2
        bias_ref = extra_ref = None
        if has_bias:
            bias_ref = refs[pos]
            pos += 1
        if has_extra:
            extra_ref = refs[pos]
            pos += 1
        outs = refs[pos:]
        w = b_ref[...]
        if merge:
            w = w.reshape(N_DEV * w.shape[1], w.shape[2])
        if trans_b:
            acc = _dot_nt(a_ref[...], w)
        else:
            acc = _dot(a_ref[...], w)
        if has_bias:
            acc = acc + bias_ref[...]
        if epilogue is None:
            outs[0][...] = acc.astype(outs[0].dtype)
        elif epilogue == "relu2":
            outs[0][...] = acc
            r = jnp.maximum(acc, 0.0)
            outs[1][...] = (r * r).astype(outs[1].dtype)
        elif epilogue == "drelu2":
            outs[0][...] = (acc * (2.0 * jnp.maximum(extra_ref[...], 0.0))).astype(outs[0].dtype)
        elif epilogue == "resid":
            rows = _row_ids(pl.program_id(0) * tm, tm, tn)
            outs[0][...] = extra_ref[...] + jnp.where(rows >= FRONT_PAD, acc, 0.0)
        else:
            raise ValueError(epilogue)

    in_specs = [pl.BlockSpec((tm, k), lambda i, j: (i, 0)), b_spec]
    args = [a, b]
    if has_bias:
        in_specs.append(pl.BlockSpec((1, tn), lambda i, j: (0, j)))
        args.append(bias)
    if has_extra:
        in_specs.append(pl.BlockSpec((tm, tn), lambda i, j: (i, j)))
        args.append(extra)
    tile = pl.BlockSpec((tm, tn), lambda i, j: (i, j))
    if epilogue == "relu2":
        out_specs = [tile, tile]
        out_shape = [jax.ShapeDtypeStruct((m, n), F32), jax.ShapeDtypeStruct((m, n), MXU_DTYPE)]
    else:
        out_specs = tile
        out_shape = jax.ShapeDtypeStruct((m, n), out_dtype)
    return pl.pallas_call(
        body, grid=(m // tm, n // tn), in_specs=in_specs, out_specs=out_specs, out_shape=out_shape,
        name=name, compiler_params=_params("arbitrary", "arbitrary"),
    )(*args)


def matmul_tn(x, dy, *, ti, tn, out_dtype=F32, name="matmul_tn"):
    m, k1 = x.shape
    n = dy.shape[1]
    tm = TN_ROW_TILE
    last = m // tm - 1

    def body(x_ref, dy_ref, o_ref, acc_ref):
        r = pl.program_id(2)

        @pl.when(r == 0)
        def _():
            acc_ref[...] = jnp.zeros_like(acc_ref)

        acc_ref[...] += _dot_tn(x_ref[...], dy_ref[...])

        @pl.when(r == last)
        def _():
            o_ref[...] = acc_ref[...].astype(o_ref.dtype)

    out_specs = pl.BlockSpec((ti, tn), lambda i, j, r: (i, j))
    out_shape = jax.ShapeDtypeStruct((k1, n), out_dtype)
    return pl.pallas_call(
        body, grid=(k1 // ti, n // tn, m // tm),
        in_specs=[pl.BlockSpec((tm, ti), lambda i, j, r: (r, i)), pl.BlockSpec((tm, tn), lambda i, j, r: (r, j))],
        out_specs=out_specs, out_shape=out_shape, scratch_shapes=[pltpu.VMEM((ti, tn), F32)], name=name,
        compiler_params=_params("arbitrary", "arbitrary", "arbitrary"),
    )(x, dy)


FF_BLOCK = D_FF // N_DEV
SQRT_FLOOR = 1.1754944e-38


def _ff_cols(d):
    return slice(FF_BLOCK * d, FF_BLOCK * (d + 1))


def mlp_up(h, norm_g, w, layer, *, bias=None, relu2=True, out_dtype=None, name="mlp_up"):
    m = h.shape[0]
    ns = w.shape[3]
    n = N_DEV * ns
    tm = ROW_TILE
    out_dtype = MXU_DTYPE if relu2 else out_dtype
    has_bias = bias is not None

    def body(*refs):
        h_ref, g_ref, w_ref = refs[0], refs[1], refs[2]
        bias_ref = refs[3] if has_bias else None
        o_ref, u_ref = refs[-2], refs[-1]
        x = h_ref[...]
        u_ = _mx((x * lax.rsqrt(jnp.mean(x * x, -1, keepdims=True) + EPS)) * g_ref[...])
        u_ref[...] = u_
        for d in range(N_DEV):
            cols = slice(ns * d, ns * (d + 1))
            r = _dot(u_, w_ref[d])
            if has_bias:
                r = r + bias_ref[:, cols]
            if relu2:
                r = jnp.maximum(r, 0.0)
                r = r * r
            o_ref[:, cols] = r.astype(o_ref.dtype)

    row = pl.BlockSpec((tm, D_MODEL), lambda i: (i, 0))
    in_specs = [row, pl.BlockSpec((1, D_MODEL), lambda i: (0, 0)),
                pl.BlockSpec((N_DEV, None, D_MODEL, ns), lambda i: (0, layer, 0, 0))]
    args = [h, norm_g, w]
    if has_bias:
        in_specs.append(pl.BlockSpec((1, n), lambda i: (0, 0)))
        args.append(bias)
    return pl.pallas_call(
        body, grid=(m // tm,), in_specs=in_specs, out_specs=[pl.BlockSpec((tm, n), lambda i: (i, 0)), row],
        out_shape=[jax.ShapeDtypeStruct((m, n), out_dtype), jax.ShapeDtypeStruct((m, D_MODEL), MXU_DTYPE)],
        name=name, compiler_params=_params("arbitrary"),
    )(*args)


def mlp_dact(dh, w_down, act, layer, name="mlp_dact"):
    m = dh.shape[0]
    tm = ROW_TILE

    def body(dh_ref, w_ref, act_ref, o_ref):
        dh_ = dh_ref[...]
        for d in range(N_DEV):
            p = act_ref[:, _ff_cols(d)].astype(F32)
            r = p * lax.rsqrt(jnp.maximum(p, SQRT_FLOOR))
            o_ref[:, _ff_cols(d)] = (_dot_nt(dh_, w_ref[d]) * (2.0 * r)).astype(o_ref.dtype)

    return pl.pallas_call(
        body, grid=(m // tm,),
        in_specs=[pl.BlockSpec((tm, D_MODEL), lambda i: (i, 0)),
                  pl.BlockSpec((N_DEV, None, FF_BLOCK, D_MODEL), lambda i: (0, layer, 0, 0)),
                  pl.BlockSpec((tm, D_FF), lambda i: (i, 0))],
        out_specs=pl.BlockSpec((tm, D_FF), lambda i: (i, 0)),
        out_shape=jax.ShapeDtypeStruct((m, D_FF), MXU_DTYPE), name=name, compiler_params=_params("arbitrary"),
    )(dh, w_down, act)


def mlp_du_rms_bwd(da, w_up, layer, h, g, dh_out, name="mlp_du"):
    m, n = da.shape
    ns = w_up.shape[3]
    assert n == N_DEV * ns
    tm = ROW_TILE

    def body(da_ref, w_ref, h_ref, g_ref, dho_ref, dh_ref, dg_ref):
        @pl.when(pl.program_id(0) == 0)
        def _():
            dg_ref[...] = jnp.zeros_like(dg_ref)

        du = _dot_nt(da_ref[:, 0:ns], w_ref[0])
        for d in range(1, N_DEV):
            du = du + _dot_nt(da_ref[:, ns * d:ns * (d + 1)], w_ref[d])
        x = h_ref[...]
        r = lax.rsqrt(jnp.mean(x * x, -1, keepdims=True) + EPS)
        gy = du * g_ref[...]
        dx = r * gy - x * ((r * r * r) * jnp.mean(x * gy, -1, keepdims=True))
        dh_ref[...] = dho_ref[...] + dx
        dg_ref[...] += jnp.sum(du * (x * r), axis=0, keepdims=True)

    row = pl.BlockSpec((tm, D_MODEL), lambda i: (i, 0))
    vec = pl.BlockSpec((1, D_MODEL), lambda i: (0, 0))
    return pl.pallas_call(
        body, grid=(m // tm,),
        in_specs=[pl.BlockSpec((tm, n), lambda i: (i, 0)),
                  pl.BlockSpec((N_DEV, None, D_MODEL, ns), lambda i: (0, layer, 0, 0)), row, vec, row],
        out_specs=[row, vec],
        out_shape=[jax.ShapeDtypeStruct((m, D_MODEL), F32), jax.ShapeDtypeStruct((1, D_MODEL), F32)],
        name=name, compiler_params=_params("arbitrary"),
    )(da, w_up, h, g, dh_out)


def mlp_dw_up(u, da, name="mlp_dw_up"):
    m, n = da.shape
    ns = n // N_DEV
    tm = TN_ROW_TILE
    last = m // tm - 1
    parts = -(-D_MODEL * n * 4 // ACC_BYTES)
    per = N_DEV // parts

    def body(u_ref, da_ref, o_ref, acc_ref):
        r = pl.program_id(1)

        @pl.when(r == 0)
        def _():
            acc_ref[...] = jnp.zeros_like(acc_ref)

        acc_ref[...] += _dot_tn(u_ref[...], da_ref[...])

        @pl.when(r == last)
        def _():
            for d in range(per):
                o_ref[d] = acc_ref[:, ns * d:ns * (d + 1)].astype(o_ref.dtype)

    return pl.pallas_call(
        body, grid=(parts, m // tm),
        in_specs=[pl.BlockSpec((tm, D_MODEL), lambda h, r: (r, 0)), pl.BlockSpec((tm, n // parts), lambda h, r: (r, h))],
        out_specs=pl.BlockSpec((per, D_MODEL, ns), lambda h, r: (h, 0, 0)),
        out_shape=jax.ShapeDtypeStruct((N_DEV, D_MODEL, ns), GRAD_WIRE_DTYPE),
        scratch_shapes=[pltpu.VMEM((D_MODEL, n // parts), F32)], name=name,
        compiler_params=_params("arbitrary", "arbitrary"),
    )(u, da)


def mlp_dw_down(act, dh, name="mlp_dw_down"):
    m = act.shape[0]
    tm = TN_ROW_TILE
    last = m // tm - 1
    parts = D_FF * D_MODEL * 4 // ACC_BYTES
    rows = D_FF // parts

    def body(a_ref, dh_ref, o_ref, acc_ref):
        r = pl.program_id(1)

        @pl.when(r == 0)
        def _():
            acc_ref[...] = jnp.zeros_like(acc_ref)

        acc_ref[...] += _dot_tn(a_ref[...], dh_ref[...])

        @pl.when(r == last)
        def _():
            o_ref[...] = acc_ref[...].astype(o_ref.dtype)

    return pl.pallas_call(
        body, grid=(parts, m // tm),
        in_specs=[pl.BlockSpec((tm, rows), lambda h, r: (r, h)), pl.BlockSpec((tm, D_MODEL), lambda h, r: (r, 0))],
        out_specs=pl.BlockSpec((rows, D_MODEL), lambda h, r: (h, 0)),
        out_shape=jax.ShapeDtypeStruct((D_FF, D_MODEL), GRAD_WIRE_DTYPE),
        scratch_shapes=[pltpu.VMEM((rows, D_MODEL), F32)], name=name,
        compiler_params=_params("arbitrary", "arbitrary"),
    )(act, dh)


_ATT_SCALE = HEAD_DIM ** -0.5


def _alibi_slope(h):
    return 2.0 ** (-8.0 * (h + 1) / ATT_HEADS)


HEADS_PER_KV = ATT_HEADS // ATT_KV_HEADS
STACK = HEADS_PER_KV * BLOCK
N_KEYS = 3 * BLOCK


def _head_select(width):
    c = jnp.arange(width)[:, None]
    h = jnp.arange(LANE)[None, :]
    return (c // HEAD_DIM == h).astype(F32)


def _head_fold(width):
    c = jnp.arange(width)[:, None]
    j = jnp.arange(LANE)[None, :]
    return (c % HEAD_DIM == j).astype(F32)


def _head_rms(x, sel, sel_t):
    r = lax.rsqrt(_dot_hi(x * x, sel) * (1.0 / HEAD_DIM) + EPS)
    return r, _dot_hi(r, sel_t)


def _head_norm_bwd(x, r, r_full, w_t, dy, sel, sel_t):
    gy = dy * w_t
    coef = _dot_hi((r * r * r) * _dot_hi(x * gy, sel) * (1.0 / HEAD_DIM), sel_t)
    return r_full * gy - x * coef, jnp.sum(dy * (x * r_full), axis=0, keepdims=True)


def _low_lanes(rows):
    return lax.broadcasted_iota(jnp.int32, (rows, LANE), 1) < HEAD_DIM


def _dup_half(a, g):
    rolled = pltpu.roll(a, HEAD_DIM, 1)
    low = _low_lanes(a.shape[0])
    return jnp.where(low, a, rolled) if g == 0 else jnp.where(low, rolled, a)


def _stack_heads(x, g):
    low = _low_lanes(BLOCK)
    parts = []
    for pair in range(2):
        p = x[:, 2 * LANE * g + LANE * pair:2 * LANE * g + LANE * (pair + 1)]
        parts += [jnp.where(low, p, 0.0), jnp.where(low, 0.0, p)]
    return jnp.concatenate(parts, axis=0)


def _unstack_heads(groups):
    low = _low_lanes(BLOCK)
    cols = []
    for o in groups:
        for pair in range(2):
            cols.append(jnp.where(low, o[2 * pair * BLOCK:(2 * pair + 1) * BLOCK], o[(2 * pair + 1) * BLOCK:(2 * pair + 2) * BLOCK]))
    return jnp.concatenate(cols, axis=1)


def _fold_halves(a, g):
    s = a + pltpu.roll(a, HEAD_DIM, 1)
    low = _low_lanes(a.shape[0])
    return jnp.where(low if g == 0 else jnp.logical_not(low), s, 0.0)


def _att_bias(b):
    r = lax.broadcasted_iota(jnp.int32, (STACK, N_KEYS), 0) & (BLOCK - 1)
    col = lax.broadcasted_iota(jnp.int32, (STACK, N_KEYS), 1)
    cc = col & (BLOCK - 1)
    is_meta = col < BLOCK
    is_prev = jnp.logical_and(col >= BLOCK, col < 2 * BLOCK)
    q_pos = b * BLOCK + r - FRONT_PAD
    meta_j = cc - FRONT_PAD
    valid_m = jnp.logical_and(cc >= FRONT_PAD, q_pos >= meta_j)
    valid_p = jnp.logical_and(cc > r, b >= 2)
    valid_c = jnp.logical_and(cc <= r, b >= 1)
    is_cur = col >= 2 * BLOCK
    valid = jnp.logical_or(jnp.logical_and(is_meta, valid_m),
                           jnp.logical_or(jnp.logical_and(is_prev, valid_p), jnp.logical_and(is_cur, valid_c)))
    dist = jnp.where(is_meta, jnp.minimum(q_pos - meta_j, BLOCK), jnp.where(is_prev, r - cc + BLOCK, r - cc))
    return valid, dist.astype(F32)


def _per_head_column(values):
    hid = lax.broadcasted_iota(jnp.int32, (STACK, 1), 0) >> 7
    col = jnp.where(hid == 0, values[0], values[1])
    for j in range(2, HEADS_PER_KV):
        col = jnp.where(hid == j, values[j], col)
    return col


def _att_group_probs(qs, kd, valid, dist, g, sk_ref):
    slope = _per_head_column([_alibi_slope(HEADS_PER_KV * g + j) for j in range(HEADS_PER_KV)])
    sink = _per_head_column([sk_ref[HEADS_PER_KV * g + j] for j in range(HEADS_PER_KV)])
    s = jnp.where(valid, _dot_nt(qs, kd) * _ATT_SCALE - slope * dist, -1e30)
    mx = jnp.maximum(jnp.max(s, axis=-1, keepdims=True), sink)
    p = jnp.exp(s - mx)
    p_sink = jnp.exp(sink - mx)
    inv = 1.0 / (jnp.sum(p, axis=-1, keepdims=True) + p_sink)
    return p * inv, p_sink * inv


def attention_fwd(proj, q_w, k_w, sinks, gather=(), name="att_fwd"):
    m = proj.shape[0]
    nb = m // BLOCK
    cq, ck, cv = COL_Q // Q_W, COL_K // KV_W, COL_V // KV_W
    sel_q, sel_k = _head_select(Q_W), _head_select(KV_W)

    def body(q_ref, kc_ref, vc_ref, vp_ref, vm_ref, qw_ref, kw_ref, sk_ref, sq_ref, sqt_ref, skk_ref, skt_ref,
             o_ref, kpn_s, kmn_s):
        b = pl.program_id(0)
        q, kc = q_ref[...], kc_ref[...]
        _, rq = _head_rms(q, sq_ref[...], sqt_ref[...])
        qn = q * rq * qw_ref[...]
        _, rk = _head_rms(kc, skk_ref[...], skt_ref[...])
        kcn = kc * rk * kw_ref[...]

        @pl.when(b == 0)
        def _():
            kmn_s[...] = kcn
            kpn_s[...] = kcn

        kpn, kmn = kpn_s[...], kmn_s[...]
        kpn_s[...] = kcn
        vc, vp, vm = vc_ref[...], vp_ref[...], vm_ref[...]
        valid, dist = _att_bias(b)
        outs = []
        for g in range(ATT_KV_HEADS):
            kd = jnp.concatenate([_dup_half(kmn, g), _dup_half(kpn, g), _dup_half(kcn, g)], axis=0)
            vd = jnp.concatenate([_dup_half(vm, g), _dup_half(vp, g), _dup_half(vc, g)], axis=0)
            probs, _ = _att_group_probs(_stack_heads(qn, g), kd, valid, dist, g, sk_ref)
            outs.append(_dot(probs, vd))
        o_ref[...] = _unstack_heads(outs).astype(o_ref.dtype)

    prev = lambda i: jnp.maximum(i - 1, 0)
    full = lambda s: pl.BlockSpec(s, lambda i: (0,) * len(s))
    body, g_in, g_out, g_shape, g_scratch = _with_gather(body, 12, 1, gather, nb)
    outs = pl.pallas_call(
        body, grid=(nb,),
        in_specs=[pl.BlockSpec((BLOCK, Q_W), lambda i: (i, cq)),
                  pl.BlockSpec((BLOCK, KV_W), lambda i: (i, ck)), pl.BlockSpec((BLOCK, KV_W), lambda i: (i, cv)),
                  pl.BlockSpec((BLOCK, KV_W), lambda i: (prev(i), cv)), pl.BlockSpec((BLOCK, KV_W), lambda i: (0, cv)),
                  full((1, Q_W)), full((1, KV_W)), pl.BlockSpec(memory_space=pltpu.SMEM),
                  full((Q_W, LANE)), full((LANE, Q_W)), full((KV_W, LANE)), full((LANE, KV_W))] + g_in,
        out_specs=[pl.BlockSpec((BLOCK, Q_W), lambda i: (i, 0))] + g_out,
        out_shape=[jax.ShapeDtypeStruct((m, Q_W), MXU_DTYPE)] + g_shape,
        scratch_shapes=[pltpu.VMEM((BLOCK, KV_W), F32), pltpu.VMEM((BLOCK, KV_W), F32)] + g_scratch,
        name=name, compiler_params=_params("arbitrary"),
    )(proj, proj, proj, proj, proj, q_w, k_w, sinks, sel_q, sel_q.T, sel_k, sel_k.T, *gather)
    return outs[0], outs[1:]


def attention_bwd(proj, dmix, q_w, k_w, sinks, carry=(), name="att_bwd"):
    m = proj.shape[0]
    nb = m // BLOCK
    cq, ck, cv = COL_Q // Q_W, COL_K // KV_W, COL_V // KV_W
    c_datt = 0
    sel_q, sel_k = _head_select(Q_W), _head_select(KV_W)
    fold_q, fold_k = _head_fold(Q_W), _head_fold(KV_W)

    def body(do_ref, q_ref, kc_ref, vc_ref, kp_ref, vp_ref, km_ref, vm_ref, qw_ref, kw_ref, sk_ref,
             sq_ref, sqt_ref, skk_ref, skt_ref, fq_ref, fk_ref,
             dq_ref, dk_ref, dv_ref, dqw_ref, dkw_ref, dsk_ref, car_k, car_v, met_k, met_v, kmn_s, qw_acc, kw_acc):
        i = pl.program_id(0)
        b = nb - 1 - i
        sel_q_, sel_qt, sel_k_, sel_kt = sq_ref[...], sqt_ref[...], skk_ref[...], skt_ref[...]
        qw, kw = qw_ref[...], kw_ref[...]

        @pl.when(i == 0)
        def _():
            for r in (car_k, car_v, met_k, met_v, qw_acc, kw_acc, dsk_ref):
                r[...] = jnp.zeros_like(r)
            km = km_ref[...]
            kmn_s[...] = km * _head_rms(km, sel_k_, sel_kt)[1] * kw

        q, kc, kp = q_ref[...], kc_ref[...], kp_ref[...]
        rq, rq_full = _head_rms(q, sel_q_, sel_qt)
        qn = q * rq_full * qw
        rk, rk_full = _head_rms(kc, sel_k_, sel_kt)
        kcn = kc * rk_full * kw
        kpn = kp * _head_rms(kp, sel_k_, sel_kt)[1] * kw
        kmn = kmn_s[...]
        vc, vp, vm = vc_ref[...], vp_ref[...], vm_ref[...]
        do = do_ref[...]
        valid, dist = _att_bias(b)
        lane = lax.broadcasted_iota(jnp.int32, (1, LANE), 1)
        dsk = jnp.zeros((1, LANE), F32)
        dkd_sum = jnp.zeros((N_KEYS, KV_W), F32)
        dvd_sum = jnp.zeros((N_KEYS, KV_W), F32)
        dqd = []
        for g in range(ATT_KV_HEADS):
            kd = jnp.concatenate([_dup_half(kmn, g), _dup_half(kpn, g), _dup_half(kcn, g)], axis=0)
            vd = jnp.concatenate([_dup_half(vm, g), _dup_half(vp, g), _dup_half(vc, g)], axis=0)
            qs = _stack_heads(qn, g)
            dos = _stack_heads(do, g)
            probs, p_sink = _att_group_probs(qs, kd, valid, dist, g, sk_ref)
            o = _dot(probs, vd)
            delta = jnp.sum(dos * o, axis=-1, keepdims=True)
            ds = probs * (_dot_nt(dos, vd) - delta)
            dqd.append(_dot(ds, kd) * _ATT_SCALE)
            dkd_sum = dkd_sum + _fold_halves(_dot_tn(ds, qs) * _ATT_SCALE, g)
            dvd_sum = dvd_sum + _fold_halves(_dot_tn(probs, dos), g)
            sink_grad = p_sink * delta
            for j in range(HEADS_PER_KV):
                part = jnp.sum(sink_grad[BLOCK * j:BLOCK * (j + 1)], axis=0, keepdims=True)
                dsk = dsk - jnp.where(lane == HEADS_PER_KV * g + j, part, 0.0)
        dq, dqw = _head_norm_bwd(q, rq, rq_full, qw, _unstack_heads(dqd), sel_q_, sel_qt)
        dq_ref[...] = dq
        qw_acc[...] += dqw
        dsk_ref[...] += dsk

        met_k[...] += dkd_sum[0:BLOCK]
        met_v[...] += dvd_sum[0:BLOCK]
        first = (b == 0).astype(F32)
        dkn_tot = dkd_sum[2 * BLOCK:3 * BLOCK] + car_k[...] + first * met_k[...]
        dv_ref[...] = dvd_sum[2 * BLOCK:3 * BLOCK] + car_v[...] + first * met_v[...]
        car_k[...] = dkd_sum[BLOCK:2 * BLOCK]
        car_v[...] = dvd_sum[BLOCK:2 * BLOCK]
        dk, dkw = _head_norm_bwd(kc, rk, rk_full, kw, dkn_tot, sel_k_, sel_kt)
        dk_ref[...] = dk
        kw_acc[...] += dkw

        @pl.when(i == nb - 1)
        def _():
            dqw_ref[...] = _dot_hi(jnp.broadcast_to(qw_acc[...], (8, Q_W)), fq_ref[...])[0:1]
            dkw_ref[...] = _dot_hi(jnp.broadcast_to(kw_acc[...], (8, KV_W)), fk_ref[...])[0:1]

    blk = lambda i: nb - 1 - i
    prev = lambda i: jnp.maximum(nb - 2 - i, 0)
    full = lambda s: pl.BlockSpec(s, lambda i: (0,) * len(s))
    kv_scratch = pltpu.VMEM((BLOCK, KV_W), F32)
    body, ex_in, ex_out, ex_shape, ex_scratch = _with_exchange(body, 17, 6, carry, nb)
    outs = pl.pallas_call(
        body, grid=(nb,),
        in_specs=[pl.BlockSpec((BLOCK, Q_W), lambda i: (blk(i), c_datt)),
                  pl.BlockSpec((BLOCK, Q_W), lambda i: (blk(i), cq)),
                  pl.BlockSpec((BLOCK, KV_W), lambda i: (blk(i), ck)), pl.BlockSpec((BLOCK, KV_W), lambda i: (blk(i), cv)),
                  pl.BlockSpec((BLOCK, KV_W), lambda i: (prev(i), ck)), pl.BlockSpec((BLOCK, KV_W), lambda i: (prev(i), cv)),
                  pl.BlockSpec((BLOCK, KV_W), lambda i: (0, ck)), pl.BlockSpec((BLOCK, KV_W), lambda i: (0, cv)),
                  full((1, Q_W)), full((1, KV_W)), pl.BlockSpec(memory_space=pltpu.SMEM),
                  full((Q_W, LANE)), full((LANE, Q_W)), full((KV_W, LANE)), full((LANE, KV_W)),
                  full((Q_W, LANE)), full((KV_W, LANE))] + ex_in,
        out_specs=[pl.BlockSpec((BLOCK, Q_W), lambda i: (blk(i), 0)),
                   pl.BlockSpec((BLOCK, KV_W), lambda i: (blk(i), 0)), pl.BlockSpec((BLOCK, KV_W), lambda i: (blk(i), 0)),
                   full((1, LANE)), full((1, LANE)), full((1, LANE))] + ex_out,
        out_shape=[jax.ShapeDtypeStruct((m, Q_W), F32), jax.ShapeDtypeStruct((m, KV_W), F32),
                   jax.ShapeDtypeStruct((m, KV_W), F32), jax.ShapeDtypeStruct((1, LANE), F32),
                   jax.ShapeDtypeStruct((1, LANE), F32), jax.ShapeDtypeStruct((1, LANE), F32)] + ex_shape,
        scratch_shapes=[kv_scratch, kv_scratch, kv_scratch, kv_scratch, kv_scratch,
                        pltpu.VMEM((1, Q_W), F32), pltpu.VMEM((1, KV_W), F32)] + ex_scratch,
        name=name, compiler_params=_params("arbitrary"),
    )(dmix, proj, proj, proj, proj, proj, proj, proj, q_w, k_w, sinks, sel_q, sel_q.T, sel_k, sel_k.T, fold_q, fold_k,
      *carry)
    return outs[:6], outs[6:]


HALO = 8
GROUP_W = SSM_INNER // SSM_GROUPS
HEADS_PER_GROUP = SSM_HEADS // SSM_GROUPS


def _head_expand():
    h = jnp.arange(LANE)[:, None]
    c = jnp.arange(SSM_INNER)[None, :]
    return (c // SSM_HEAD_DIM == h).astype(F32)


def _softplus(x):
    return jnp.maximum(x, 0.0) + jnp.log1p(jnp.exp(-jnp.abs(x)))


def _ssd_decays(dt, a_log_row):
    row = lax.broadcasted_iota(jnp.int32, (BLOCK, BLOCK), 0)
    col = lax.broadcasted_iota(jnp.int32, (BLOCK, BLOCK), 1)
    lower = row >= col
    a = -jnp.exp(a_log_row)
    a_cs = _dot_hi(lower, dt * a, exact="b")
    return a, a_cs, lower


def _decay_matrix(a_cs, a_cs_t, h, lower):
    diff = a_cs[:, h:h + 1] - a_cs_t[h:h + 1, :]
    return jnp.where(lower, jnp.exp(jnp.where(lower, diff, 0.0)), 0.0)


def _conv_taps(s_ref, w_ref, first, rows):
    acc = w_ref[0:1, :] * s_ref[pl.ds(first, rows), :]
    for j in range(1, SSM_CONV):
        acc = acc + w_ref[j:j + 1, :] * s_ref[pl.ds(first + j, rows), :]
    return acc


def ssd_fwd(proj, att, cw_x, cw_bc, cb_x, cb_bc, dt_bias, a_log, d_exp, norm_w, gather=(), name="ssd_fwd"):
    m = proj.shape[0]
    nb = m // BLOCK
    expand = _head_expand()
    expand_t = expand.T

    def body(z_ref, xs_ref, bc_ref, dtr_ref, att_ref, cwx_ref, cwbc_ref, cbx_ref, cbbc_ref, dtb_ref, alog_ref, dexp_ref,
             nw_ref, e_ref, et_ref, out_ref, prex_ref, prebc_ref, dt_ref, ypre_ref, st_ref, sx, sbc, state):
        c = pl.program_id(0)

        @pl.when(c == 0)
        def _():
            sx[0:HALO, :] = jnp.zeros((HALO, SSM_INNER), F32)
            sbc[0:HALO, :] = jnp.zeros((HALO, 2 * LANE), F32)
            state[...] = jnp.zeros_like(state)

        sx[HALO:HALO + BLOCK, :] = xs_ref[...]
        sbc[HALO:HALO + BLOCK, :] = bc_ref[...]
        first = HALO - (SSM_CONV - 1)
        pre_x = _conv_taps(sx, cwx_ref, first, BLOCK) + cbx_ref[...]
        pre_bc = _conv_taps(sbc, cwbc_ref, first, BLOCK) + cbbc_ref[...]
        sx[0:HALO, :] = xs_ref[BLOCK - HALO:BLOCK, :]
        sbc[0:HALO, :] = bc_ref[BLOCK - HALO:BLOCK, :]
        prex_ref[...] = pre_x
        prebc_ref[...] = pre_bc
        xc = pre_x * _sigmoid(pre_x)
        bcv = pre_bc * _sigmoid(pre_bc)

        rows = _row_ids(c * BLOCK, BLOCK, LANE)
        lanes = lax.broadcasted_iota(jnp.int32, (BLOCK, LANE), 1)
        live = jnp.logical_and(rows >= FRONT_PAD, lanes < SSM_HEADS)
        dt = jnp.where(live, _softplus(dtr_ref[...] + dtb_ref[...]), 0.0)
        dt_ref[...] = dt
        a, a_cs, lower = _ssd_decays(dt, alog_ref[...])
        a_cs_t = a_cs.T
        dt_t = dt.T
        e = e_ref[...]
        es_full = _dot_hi(jnp.exp(a_cs), e)
        wx_full = _dot_hi(jnp.exp(a_cs[BLOCK - 1:BLOCK, :] - a_cs) * dt, e)
        end_col = jnp.exp(a_cs_t[:, BLOCK - 1:BLOCK])
        dec_full = _dot_hi(et_ref[...], jnp.broadcast_to(end_col, (LANE, SSM_STATE)), exact="b")

        st_ref[0] = state[...]
        ys = []
        for g in range(SSM_GROUPS):
            b_g = bcv[:, SSM_STATE * g:SSM_STATE * (g + 1)]
            c_g = bcv[:, LANE + SSM_STATE * g:LANE + SSM_STATE * (g + 1)]
            gs = slice(GROUP_W * g, GROUP_W * (g + 1))
            cb = _dot_nt(c_g, b_g)
            yd = []
            for hh in range(HEADS_PER_GROUP):
                h = g * HEADS_PER_GROUP + hh
                w = cb * _decay_matrix(a_cs, a_cs_t, h, lower) * dt_t[h:h + 1, :]
                yd.append(_dot(w, xc[:, SSM_HEAD_DIM * h:SSM_HEAD_DIM * (h + 1)]))
            h_g = state[gs, :]
            y_off = _dot_nt(c_g, h_g) * es_full[:, gs]
            ys.append(jnp.concatenate(yd, axis=1) + y_off)
            new_state = _dot_tn(xc[:, gs] * wx_full[:, gs], b_g)
            state[gs, :] = h_g * dec_full[gs, :] + new_state
        y_pre = jnp.concatenate(ys, axis=1) + xc * dexp_ref[...]
        ypre_ref[...] = y_pre
        z = z_ref[...]
        gt = y_pre * (z * _sigmoid(z))
        outs = []
        for g in range(SSM_GROUPS):
            gg = gt[:, GROUP_W * g:GROUP_W * (g + 1)]
            r = lax.rsqrt(jnp.mean(gg * gg, -1, keepdims=True) + EPS)
            outs.append(gg * r)
        out_ref[:, 0:Q_W] = att_ref[...]
        out_ref[:, Q_W:MIX_W] = (jnp.concatenate(outs, axis=1) * nw_ref[...]).astype(out_ref.dtype)

    full = lambda s: pl.BlockSpec(s, lambda i: (0,) * len(s))
    rowblk = lambda w, cidx: pl.BlockSpec((BLOCK, w), lambda i: (i, cidx))
    body, g_in, g_out, g_shape, g_scratch = _with_gather(body, 15, 6, gather, nb)
    outs = pl.pallas_call(
        body, grid=(nb,),
        in_specs=[rowblk(SSM_INNER, COL_Z // SSM_INNER), rowblk(SSM_INNER, COL_XS // SSM_INNER),
                  rowblk(2 * LANE, COL_BC // (2 * LANE)), rowblk(LANE, COL_DT // LANE), rowblk(Q_W, 0),
                  full((SSM_CONV, SSM_INNER)), full((SSM_CONV, 2 * LANE)), full((1, SSM_INNER)), full((1, 2 * LANE)),
                  full((1, LANE)), full((1, LANE)), full((1, SSM_INNER)), full((1, SSM_INNER)),
                  full((LANE, SSM_INNER)), full((SSM_INNER, LANE))] + g_in,
        out_specs=[rowblk(MIX_W, 0), rowblk(SSM_INNER, 0), rowblk(2 * LANE, 0), rowblk(LANE, 0),
                   rowblk(SSM_INNER, 0), pl.BlockSpec((1, SSM_INNER, SSM_STATE), lambda i: (i, 0, 0))] + g_out,
        out_shape=[jax.ShapeDtypeStruct((m, MIX_W), MXU_DTYPE), jax.ShapeDtypeStruct((m, SSM_INNER), F32),
                   jax.ShapeDtypeStruct((m, 2 * LANE), F32), jax.ShapeDtypeStruct((m, LANE), F32),
                   jax.ShapeDtypeStruct((m, SSM_INNER), F32),
                   jax.ShapeDtypeStruct((nb, SSM_INNER, SSM_STATE), F32)] + g_shape,
        scratch_shapes=[pltpu.VMEM((HALO + BLOCK, SSM_INNER), F32), pltpu.VMEM((HALO + BLOCK, 2 * LANE), F32),
                        pltpu.VMEM((SSM_INNER, SSM_STATE), F32)] + g_scratch,
        name=name, compiler_params=_params("arbitrary"),
    )(proj, proj, proj, proj, att, cw_x, cw_bc, cb_x, cb_bc, dt_bias, a_log, d_exp, norm_w, expand, expand_t, *gather)
    return outs[:6], outs[6:]


def ssd_bwd(proj, dmix, dq, dk, dv, pre_x, pre_bc, dt, y_pre, states, cw_x, cw_bc, dt_bias, a_log, d_exp, norm_w,
            carry=(), name="ssd_bwd"):
    m = proj.shape[0]
    nb = m // BLOCK
    expand = _head_expand()
    expand_t = expand.T

    def body(do0_ref, do1_ref, z_ref, xs_ref, xsp_ref, bc_ref, bcp_ref, dtr_ref, prex_ref, prebc_ref, dt_ref, ypre_ref,
             st_ref, dq_ref, dk_ref, dv_ref,
             cwx_ref, cwbc_ref, dtb_ref, alog_ref, dexp_ref, nw_ref, e_ref, et_ref,
             dproj_ref, dcwx_ref, dcwbc_ref, dcbx_ref, dcbbc_ref, ddtb_ref, dalog_ref,
             dd_ref, dnw_ref,
             dstate, hnext, tx, tbc, sx, sbc, dlane):

        def put(col, value):
            dproj_ref[:, col:col + value.shape[1]] = value.astype(dproj_ref.dtype)

        put(COL_Q, dq_ref[...])
        put(COL_K, dk_ref[...])
        put(COL_V, dv_ref[...])
        i = pl.program_id(0)
        c = nb - 1 - i

        @pl.when(i == 0)
        def _():
            dstate[...] = jnp.zeros_like(dstate)
            hnext[...] = jnp.zeros_like(hnext)
            tx[BLOCK:BLOCK + HALO, :] = jnp.zeros((HALO, SSM_INNER), F32)
            tbc[BLOCK:BLOCK + HALO, :] = jnp.zeros((HALO, 2 * LANE), F32)
            dlane[...] = jnp.zeros_like(dlane)
            for r in (dcwx_ref, dcwbc_ref, dcbx_ref, dcbbc_ref, ddtb_ref, dalog_ref, dd_ref, dnw_ref):
                r[...] = jnp.zeros_like(r)

        e = e_ref[...]
        et = et_ref[...]
        pre_x = prex_ref[...]
        pre_bc = prebc_ref[...]
        sig_x = _sigmoid(pre_x)
        sig_bc = _sigmoid(pre_bc)
        xc = pre_x * sig_x
        bcv = pre_bc * sig_bc
        dt = dt_ref[...]
        a, a_cs, lower = _ssd_decays(dt, alog_ref[...])
        a_cs_t = a_cs.T
        es_full = _dot_hi(jnp.exp(a_cs), e)
        ed_full = _dot_hi(jnp.exp(a_cs[BLOCK - 1:BLOCK, :] - a_cs), e)
        dt_full = _dot_hi(dt, e)
        end_col = jnp.exp(a_cs_t[:, BLOCK - 1:BLOCK])
        dec_full = _dot_hi(et, jnp.broadcast_to(end_col, (LANE, SSM_STATE)), exact="b")
        dexp = dexp_ref[...]

        z = z_ref[...]
        zs = _sigmoid(z)
        sz = z * zs
        y_pre = ypre_ref[...]
        gt = y_pre * sz
        do = jnp.concatenate([do0_ref[...], do1_ref[...]], axis=1)
        nw = nw_ref[...]
        dgt = []
        dnw = []
        for g in range(SSM_GROUPS):
            gs = slice(GROUP_W * g, GROUP_W * (g + 1))
            gg = gt[:, gs]
            r = lax.rsqrt(jnp.mean(gg * gg, -1, keepdims=True) + EPS)
            gn = do[:, gs] * nw[:, gs]
            dgt.append(r * gn - gg * ((r * r * r) * jnp.mean(gg * gn, -1, keepdims=True)))
            dnw.append(jnp.sum(do[:, gs] * (gg * r), axis=0, keepdims=True))
        dgt = jnp.concatenate(dgt, axis=1)
        dnw_ref[...] += jnp.concatenate(dnw, axis=1)
        dy = dgt * sz
        put(COL_Z, dgt * y_pre * (zs * (1.0 + z * (1.0 - zs))))
        dlane[...] += jnp.sum(dy * xc, axis=0, keepdims=True)
        xd = xc * dt_full

        lane_id = lax.broadcasted_iota(jnp.int32, (BLOCK, LANE), 1)
        sub_id = lax.broadcasted_iota(jnp.int32, (LANE, BLOCK), 0)
        ds_to = jnp.zeros((BLOCK, LANE), F32)
        ds_from_t = jnp.zeros((LANE, BLOCK), F32)
        dxd_parts, inter_parts = [], []
        dbs, dcs = [], []
        for g in range(SSM_GROUPS):
            gs = slice(GROUP_W * g, GROUP_W * (g + 1))
            b_g = bcv[:, SSM_STATE * g:SSM_STATE * (g + 1)]
            c_g = bcv[:, LANE + SSM_STATE * g:LANE + SSM_STATE * (g + 1)]
            cb = _dot_nt(c_g, b_g)
            dcb = jnp.zeros((BLOCK, BLOCK), F32)
            dxd_h = []
            for hh in range(HEADS_PER_GROUP):
                h = g * HEADS_PER_GROUP + hh
                hs = slice(SSM_HEAD_DIM * h, SSM_HEAD_DIM * (h + 1))
                lm = _decay_matrix(a_cs, a_cs_t, h, lower)
                dy_h = dy[:, hs]
                gl = _dot_nt(dy_h, xd[:, hs]) * lm
                dcb = dcb + gl
                e_h = gl * cb
                ds_to = ds_to + jnp.where(lane_id == h, jnp.sum(e_h, axis=-1, keepdims=True), 0.0)
                ds_from_t = ds_from_t + jnp.where(sub_id == h, jnp.sum(e_h, axis=0, keepdims=True), 0.0)
                dxd_h.append(_dot_tn(cb * lm, dy_h))
            h_g = st_ref[0, gs, :]
            dh_g = dstate[gs, :]
            dys_g = dy[:, gs] * es_full[:, gs]
            xde_g = xd[:, gs] * ed_full[:, gs]
            dcs.append(_dot(dcb, b_g) + _dot(dys_g, h_g))
            dbs.append(_dot_tn(dcb, c_g) + _dot(xde_g, dh_g))
            y_off = _dot_nt(c_g, h_g) * es_full[:, gs]
            dxd_state = _dot_nt(b_g, dh_g) * ed_full[:, gs]
            inter_parts.append(dy[:, gs] * y_off - xd[:, gs] * dxd_state)
            dxd_parts.append(jnp.concatenate(dxd_h, axis=1) + dxd_state)
            dstate[gs, :] = dh_g * dec_full[gs, :] + _dot_tn(dys_g, c_g)
            if g == 0:
                end_dot = hnext[gs, :] * dh_g
            else:
                end_dot = jnp.concatenate([end_dot, hnext[gs, :] * dh_g], axis=0)
        dxd = jnp.concatenate(dxd_parts, axis=1)
        hnext[...] = st_ref[0]

        ds = ds_to - ds_from_t.T + _dot_hi(jnp.concatenate(inter_parts, axis=1), et)
        ds_end = jnp.sum(_dot_tn_hi(end_dot, et), axis=0, keepdims=True)
        rows_l = lax.broadcasted_iota(jnp.int32, (BLOCK, LANE), 0)
        ds = ds + jnp.where(rows_l == BLOCK - 1, ds_end, 0.0)
        row = lax.broadcasted_iota(jnp.int32, (BLOCK, BLOCK), 0)
        col = lax.broadcasted_iota(jnp.int32, (BLOCK, BLOCK), 1)
        dadt = _dot_hi(col >= row, ds, exact="b")
        ddt = dadt * a + _dot_hi(dxd * xc, et)
        dalog_ref[...] += jnp.sum(dadt * dt, axis=0, keepdims=True) * a
        rows = _row_ids(c * BLOCK, BLOCK, LANE)
        lanes = lax.broadcasted_iota(jnp.int32, (BLOCK, LANE), 1)
        live = jnp.logical_and(rows >= FRONT_PAD, lanes < SSM_HEADS)
        ddt_raw = jnp.where(live, ddt * _sigmoid(dtr_ref[...] + dtb_ref[...]), 0.0)
        put(COL_DT, ddt_raw)
        ddtb_ref[...] += jnp.sum(ddt_raw, axis=0, keepdims=True)

        dxc = dxd * dt_full + dy * dexp
        dpre_x = dxc * (sig_x * (1.0 + pre_x * (1.0 - sig_x)))
        dpre_bc = jnp.concatenate(dbs + dcs, axis=1) * (sig_bc * (1.0 + pre_bc * (1.0 - sig_bc)))
        dcbx_ref[...] += jnp.sum(dpre_x, axis=0, keepdims=True)
        dcbbc_ref[...] += jnp.sum(dpre_bc, axis=0, keepdims=True)
        keep_x = _row_ids(c * BLOCK, BLOCK, SSM_INNER) >= FRONT_PAD
        keep_bc = _row_ids(c * BLOCK, BLOCK, 2 * LANE) >= FRONT_PAD
        prev_live = (c > 0).astype(F32)
        for (dpre, t_ref, s_ref, cur_ref, prv_ref, w_ref, dw_ref, col, keep) in (
                (dpre_x, tx, sx, xs_ref, xsp_ref, cwx_ref, dcwx_ref, COL_XS, keep_x),
                (dpre_bc, tbc, sbc, bc_ref, bcp_ref, cwbc_ref, dcwbc_ref, COL_BC, keep_bc)):
            t_ref[0:BLOCK, :] = dpre
            acc = w_ref[0:1, :] * t_ref[pl.ds(SSM_CONV - 1, BLOCK), :]
            for j in range(1, SSM_CONV):
                acc = acc + w_ref[j:j + 1, :] * t_ref[pl.ds(SSM_CONV - 1 - j, BLOCK), :]
            put(col, jnp.where(keep, acc, 0.0))
            t_ref[BLOCK:BLOCK + HALO, :] = dpre[0:HALO, :]
            s_ref[0:HALO, :] = prv_ref[BLOCK - HALO:BLOCK, :] * prev_live
            s_ref[HALO:HALO + BLOCK, :] = cur_ref[...]
            first = HALO - (SSM_CONV - 1)
            for j in range(SSM_CONV):
                dw_ref[j:j + 1, :] += jnp.sum(dpre * s_ref[pl.ds(first + j, BLOCK), :], axis=0, keepdims=True)

        @pl.when(i == nb - 1)
        def _():
            dd_ref[...] = _dot_hi(jnp.broadcast_to(dlane[...], (HALO, SSM_INNER)), et)[0:1, :]

    blk = lambda i: nb - 1 - i
    prv = lambda i: jnp.maximum(nb - 2 - i, 0)
    full = lambda s: pl.BlockSpec(s, lambda i: (0,) * len(s))
    rowblk = lambda w, cidx: pl.BlockSpec((BLOCK, w), lambda i: (blk(i), cidx))
    prvblk = lambda w, cidx: pl.BlockSpec((BLOCK, w), lambda i: (prv(i), cidx))
    body, ex_in, ex_out, ex_shape, ex_scratch = _with_exchange(body, 24, 9, carry, nb)
    outs = pl.pallas_call(
        body, grid=(nb,),
        in_specs=[rowblk(GROUP_W, Q_W // GROUP_W), rowblk(GROUP_W, Q_W // GROUP_W + 1),
                  rowblk(SSM_INNER, COL_Z // SSM_INNER),
                  rowblk(SSM_INNER, COL_XS // SSM_INNER), prvblk(SSM_INNER, COL_XS // SSM_INNER),
                  rowblk(2 * LANE, COL_BC // (2 * LANE)), prvblk(2 * LANE, COL_BC // (2 * LANE)),
                  rowblk(LANE, COL_DT // LANE),
                  rowblk(SSM_INNER, 0), rowblk(2 * LANE, 0), rowblk(LANE, 0), rowblk(SSM_INNER, 0),
                  pl.BlockSpec((1, SSM_INNER, SSM_STATE), lambda i: (blk(i), 0, 0)),
                  rowblk(Q_W, 0), rowblk(KV_W, 0), rowblk(KV_W, 0),
                  full((SSM_CONV, SSM_INNER)), full((SSM_CONV, 2 * LANE)), full((1, LANE)), full((1, LANE)),
                  full((1, SSM_INNER)), full((1, SSM_INNER)), full((LANE, SSM_INNER)), full((SSM_INNER, LANE))] + ex_in,
        out_specs=[rowblk(PROJ_W, 0),
                   full((SSM_CONV, SSM_INNER)), full((SSM_CONV, 2 * LANE)), full((1, SSM_INNER)), full((1, 2 * LANE)),
                   full((1, LANE)), full((1, LANE)), full((1, LANE)), full((1, SSM_INNER))] + ex_out,
        out_shape=[jax.ShapeDtypeStruct((m, PROJ_W), MXU_DTYPE),
                   jax.ShapeDtypeStruct((SSM_CONV, SSM_INNER), F32), jax.ShapeDtypeStruct((SSM_CONV, 2 * LANE), F32),
                   jax.ShapeDtypeStruct((1, SSM_INNER), F32), jax.ShapeDtypeStruct((1, 2 * LANE), F32),
                   jax.ShapeDtypeStruct((1, LANE), F32), jax.ShapeDtypeStruct((1, LANE), F32),
                   jax.ShapeDtypeStruct((1, LANE), F32), jax.ShapeDtypeStruct((1, SSM_INNER), F32)] + ex_shape,
        scratch_shapes=[pltpu.VMEM((SSM_INNER, SSM_STATE), F32), pltpu.VMEM((SSM_INNER, SSM_STATE), F32),
                        pltpu.VMEM((BLOCK + HALO, SSM_INNER), F32), pltpu.VMEM((BLOCK + HALO, 2 * LANE), F32),
                        pltpu.VMEM((HALO + BLOCK, SSM_INNER), F32), pltpu.VMEM((HALO + BLOCK, 2 * LANE), F32),
                        pltpu.VMEM((1, SSM_INNER), F32)] + ex_scratch,
        name=name, compiler_params=_params("arbitrary"),
    )(dmix, dmix, proj, proj, proj, proj, proj, proj, pre_x, pre_bc, dt, y_pre, states, dq, dk, dv,
      cw_x, cw_bc, dt_bias, a_log, d_exp, norm_w, expand, expand_t, *carry)
    return outs[:9], outs[9:]


CONF_HALO = 32
SUBLANES = 8


def _for_each_window(s, offsets, rows, fn):
    total = s.shape[0]
    assert max(offsets) + rows <= total
    for b in range(SUBLANES):
        offs = [o for o in offsets if o % SUBLANES == b]
        if not offs:
            continue
        rot = s if b == 0 else pltpu.roll(s, total - b, 0)
        for o in offs:
            fn(o, rot[o - b:o - b + rows])


def _glu_masked(v, first_row):
    a = v[:, :D_MODEL]
    s = _sigmoid(v[:, D_MODEL:])
    rows = _row_ids(first_row, v.shape[0], D_MODEL)
    return jnp.where(rows >= FRONT_PAD, a * s, 0.0), a, s


def _layer_norm_stats(c):
    mu = jnp.mean(c, -1, keepdims=True)
    xc = c - mu
    rstd = lax.rsqrt(jnp.mean(xc * xc, -1, keepdims=True) + LN_EPS)
    return xc * rstd, rstd


def conformer_mid_fwd(v, dw_w, dw_b, ln_g, ln_b, name="conf_mid_fwd"):
    m = v.shape[0]
    nb = m // BLOCK
    kpad = dw_w.shape[0]

    def body(vc_ref, vp_ref, w_ref, b_ref, g_ref, beta_ref, c_ref, s_ref):
        i = pl.program_id(0)
        g_prev, _, _ = _glu_masked(vp_ref[BLOCK - CONF_HALO:BLOCK, :], (i - 1) * BLOCK + BLOCK - CONF_HALO)
        g_cur, _, _ = _glu_masked(vc_ref[...], i * BLOCK)
        sg = jnp.concatenate([g_prev * (i > 0).astype(F32), g_cur], axis=0)
        first = CONF_HALO - (CONF_KERNEL - 1)
        acc = [jnp.broadcast_to(b_ref[...], (BLOCK, D_MODEL))]

        def tap(off, win):
            j = off - first
            acc[0] = acc[0] + w_ref[j:j + 1, :] * win

        _for_each_window(sg, [first + j for j in range(CONF_KERNEL)], BLOCK, tap)
        acc = acc[0]
        c_ref[...] = acc
        xhat, _ = _layer_norm_stats(acc)
        nrm = xhat * g_ref[...] + beta_ref[...]
        s_ref[...] = (nrm * _sigmoid(nrm)).astype(s_ref.dtype)

    full = lambda s: pl.BlockSpec(s, lambda i: (0,) * len(s))
    return pl.pallas_call(
        body, grid=(nb,),
        in_specs=[pl.BlockSpec((BLOCK, 2 * D_MODEL), lambda i: (i, 0)),
                  pl.BlockSpec((BLOCK, 2 * D_MODEL), lambda i: (jnp.maximum(i - 1, 0), 0)),
                  full((kpad, D_MODEL)), full((1, D_MODEL)), full((1, D_MODEL)), full((1, D_MODEL))],
        out_specs=[pl.BlockSpec((BLOCK, D_MODEL), lambda i: (i, 0)), pl.BlockSpec((BLOCK, D_MODEL), lambda i: (i, 0))],
        out_shape=[jax.ShapeDtypeStruct((m, D_MODEL), F32), jax.ShapeDtypeStruct((m, D_MODEL), MXU_DTYPE)],
        name=name, compiler_params=_params("arbitrary"),
    )(v, v, dw_w, dw_b, ln_g, ln_b)


def conformer_ln_bwd(ds, c, ln_g, ln_b, name="conf_ln_bwd"):
    m, d = c.shape
    tm = ROW_TILE

    def body(ds_ref, c_ref, g_ref, beta_ref, dc_ref, dg_ref, db_ref):
        @pl.when(pl.program_id(0) == 0)
        def _():
            dg_ref[...] = jnp.zeros_like(dg_ref)
            db_ref[...] = jnp.zeros_like(db_ref)

        xhat, rstd = _layer_norm_stats(c_ref[...])
        g = g_ref[...]
        nrm = xhat * g + beta_ref[...]
        sg = _sigmoid(nrm)
        dn = ds_ref[...] * (sg * (1.0 + nrm * (1.0 - sg)))
        db_ref[...] += jnp.sum(dn, axis=0, keepdims=True)
        dg_ref[...] += jnp.sum(dn * xhat, axis=0, keepdims=True)
        dx = dn * g
        dc_ref[...] = rstd * (dx - jnp.mean(dx, -1, keepdims=True) - xhat * jnp.mean(dx * xhat, -1, keepdims=True))

    row = pl.BlockSpec((tm, d), lambda i: (i, 0))
    vec = pl.BlockSpec((1, d), lambda i: (0, 0))
    return pl.pallas_call(
        body, grid=(m // tm,), in_specs=[row, row, vec, vec], out_specs=[row, vec, vec],
        out_shape=[jax.ShapeDtypeStruct((m, d), F32), jax.ShapeDtypeStruct((1, d), F32), jax.ShapeDtypeStruct((1, d), F32)],
        name=name, compiler_params=_params("arbitrary"),
    )(ds, c, ln_g, ln_b)


def conformer_conv_bwd(dc, v, dw_w, carry=(), name="conf_conv_bwd"):
    m = v.shape[0]
    nb = m // BLOCK
    kpad = dw_w.shape[0]

    def body(dcc_ref, dcn_ref, vc_ref, vp_ref, w_ref, dv_ref, dw_ref, db_ref, dvb_ref):
        i = pl.program_id(0)

        @pl.when(i == 0)
        def _():
            dw_ref[...] = jnp.zeros_like(dw_ref)
            db_ref[...] = jnp.zeros_like(db_ref)
            dvb_ref[...] = jnp.zeros_like(dvb_ref)

        dc_cur = dcc_ref[...]
        tg = jnp.concatenate([dc_cur, dcn_ref[0:CONF_HALO, :] * (i < nb - 1).astype(F32)], axis=0)
        g_prev, _, _ = _glu_masked(vp_ref[BLOCK - CONF_HALO:BLOCK, :], (i - 1) * BLOCK + BLOCK - CONF_HALO)
        g_cur, a, s = _glu_masked(vc_ref[...], i * BLOCK)
        sg = jnp.concatenate([g_prev * (i > 0).astype(F32), g_cur], axis=0)
        db_ref[...] += jnp.sum(dc_cur, axis=0, keepdims=True)
        first = CONF_HALO - (CONF_KERNEL - 1)
        dg_acc = [jnp.zeros((BLOCK, D_MODEL), F32)]

        def tap_dg(off, win):
            j = CONF_KERNEL - 1 - off
            dg_acc[0] = dg_acc[0] + w_ref[j:j + 1, :] * win

        def tap_dw(off, win):
            j = off - first
            dw_ref[j:j + 1, :] += jnp.sum(dc_cur * win, axis=0, keepdims=True)

        _for_each_window(tg, list(range(CONF_KERNEL)), BLOCK, tap_dg)
        _for_each_window(sg, [first + j for j in range(CONF_KERNEL)], BLOCK, tap_dw)
        dg = dg_acc[0]
        rows = _row_ids(i * BLOCK, BLOCK, D_MODEL)
        dg = jnp.where(rows >= FRONT_PAD, dg, 0.0)
        da = dg * s
        dbv = dg * a * (s * (1.0 - s))
        dv = jnp.concatenate([da, dbv], axis=1)
        dv_ref[...] = dv.astype(dv_ref.dtype)
        dvb_ref[...] += jnp.sum(dv, axis=0, keepdims=True)

    full = lambda s: pl.BlockSpec(s, lambda i: (0,) * len(s))
    body, ex_in, ex_out, ex_shape, ex_scratch = _with_exchange(body, 5, 4, carry, nb)
    outs = pl.pallas_call(
        body, grid=(nb,),
        in_specs=[pl.BlockSpec((BLOCK, D_MODEL), lambda i: (i, 0)),
                  pl.BlockSpec((BLOCK, D_MODEL), lambda i: (jnp.minimum(i + 1, nb - 1), 0)),
                  pl.BlockSpec((BLOCK, 2 * D_MODEL), lambda i: (i, 0)),
                  pl.BlockSpec((BLOCK, 2 * D_MODEL), lambda i: (jnp.maximum(i - 1, 0), 0)),
                  full((kpad, D_MODEL))] + ex_in,
        out_specs=[pl.BlockSpec((BLOCK, 2 * D_MODEL), lambda i: (i, 0)), full((kpad, D_MODEL)),
                   full((1, D_MODEL)), full((1, 2 * D_MODEL))] + ex_out,
        out_shape=[jax.ShapeDtypeStruct((m, 2 * D_MODEL), MXU_DTYPE), jax.ShapeDtypeStruct((kpad, D_MODEL), F32),
                   jax.ShapeDtypeStruct((1, D_MODEL), F32), jax.ShapeDtypeStruct((1, 2 * D_MODEL), F32)] + ex_shape,
        scratch_shapes=ex_scratch, name=name, compiler_params=_params("arbitrary"),
    )(dc, dc, v, v, dw_w, *carry)
    return outs[:4], outs[4:]


def _row(v, width=None):
    v = v.reshape(1, -1).astype(F32)
    if width is not None and v.shape[1] < width:
        v = jnp.pad(v, ((0, 0), (0, width - v.shape[1])))
    return v


def _w_in_to_kernel(w):
    pad = jnp.zeros((w.shape[0], PROJ_W - COL_DT - SSM_HEADS), w.dtype)
    return jnp.concatenate([w[:, 768:1792], w[:, 1792:2816], w[:, 0:512], w[:, 2816:3072], w[:, 512:640],
                            w[:, 640:768], w[:, 3072:3088], pad], axis=1)


def _w_in_from_kernel(g):
    return jnp.concatenate([g[:, COL_Q:COL_Q + Q_W], g[:, COL_K:COL_K + KV_W], g[:, COL_V:COL_V + KV_W],
                            g[:, COL_Z:COL_Z + SSM_INNER], g[:, COL_XS:COL_XS + SSM_INNER],
                            g[:, COL_BC:COL_BC + 2 * LANE], g[:, COL_DT:COL_DT + SSM_HEADS]], axis=1)


def even_fwd(h, p, gather_att=(), gather_ssd=()):
    u = rms_fwd(h, p["norm"])
    proj = matmul(u, p["w_in"], name="mm_proj")
    att, got_att = attention_fwd(proj, p["q_norm"], p["k_norm"], p["sinks"], gather=list(gather_att))
    (mix, pre_x, pre_bc, dt, y_pre, states), got_ssd = ssd_fwd(
        proj, att, p["cw_x"], p["cw_bc"], p["cb_x"], p["cb_bc"], p["dt_bias"], p["a_log"], p["d_exp"], p["ssm_norm"],
        gather=list(gather_ssd))
    out = matmul(mix, p["w_out"], b_kind="rowshard", layer=p["layer"], epilogue="resid", extra=h, name="mm_mix_out")
    return out, (h, u, proj, mix, pre_x, pre_bc, dt, y_pre, states), got_att, got_ssd


def even_bwd(dh, p, saved, carry_att=(), carry_ssd=()):
    h, u, proj, mix, pre_x, pre_bc, dt, y_pre, states = saved
    dmix = matmul(dh, p["w_out"], b_kind="rowshard", layer=p["layer"], trans_b=True, name="mm_dmix")
    dw_out = matmul_tn(mix, dh, ti=512, tn=D_MODEL, out_dtype=GRAD_WIRE_DTYPE, name="mm_dw_out")
    dw_out = dw_out.reshape(N_DEV, MIX_W // N_DEV, D_MODEL)
    (dq, dk, dv, dqw, dkw, dsk), got_att = attention_bwd(proj, dmix, p["q_norm"], p["k_norm"], p["sinks"],
                                                         carry=list(carry_att))
    (dproj, dcwx, dcwbc, dcbx, dcbbc, ddtb, dalog, dd, dnw), got_ssd = ssd_bwd(
        proj, dmix, dq, dk, dv, pre_x, pre_bc, dt, y_pre, states, p["cw_x"], p["cw_bc"], p["dt_bias"], p["a_log"],
        p["d_exp"], p["ssm_norm"], carry=[dw_out] + list(carry_ssd))
    du = matmul(dproj, p["w_in"], trans_b=True, name="mm_du_in")
    dw_in = matmul_tn(u, dproj, ti=512, tn=PROJ_W, name="mm_dw_in")
    dw_in = _to_shards(_w_in_from_kernel(dw_in), 1).astype(GRAD_WIRE_DTYPE)
    dh_in, dg = rms_bwd(h, p["norm"], du, dh)
    grads = dict(norm=dg, w_in=dw_in, cw_x=dcwx, cw_bc=dcwbc, cb_x=dcbx, cb_bc=dcbbc, dt_bias=ddtb,
                 a_log=dalog, d_skip=dd, ssm_norm=dnw, q_norm=dqw, k_norm=dkw, sinks=dsk)
    return dh_in, grads, got_att, got_ssd


def conf_fwd(h, p):
    v, u = mlp_up(h, p["norm"], p["pw1_w"], p["layer"], bias=p["pw1_b"], relu2=False, out_dtype=F32, name="mm_pw1")
    c, s = conformer_mid_fwd(v, p["dw_w"], p["dw_b"], p["ln_g"], p["ln_b"])
    out = matmul(s, p["pw2_w"], b_kind="rowshard", layer=p["layer"], bias=p["pw2_b"], epilogue="resid", extra=h,
                 name="mm_pw2")
    return out, (h, u, v, c, s)


def conf_bwd(dh, p, saved, carry=()):
    h, u, v, c, s = saved
    dpw2_b = col_sum(dh)
    ds = matmul(dh, p["pw2_w"], b_kind="rowshard", layer=p["layer"], trans_b=True, name="mm_ds")
    dpw2_w = matmul_tn(s, dh, ti=D_MODEL, tn=D_MODEL, out_dtype=GRAD_WIRE_DTYPE, name="mm_dpw2")
    dpw2_w = dpw2_w.reshape(N_DEV, D_MODEL // N_DEV, D_MODEL)
    dc, dln_g, dln_b = conformer_ln_bwd(ds, c, p["ln_g"], p["ln_b"])
    (dv, ddw_w, ddw_b, dpw1_b), got = conformer_conv_bwd(dc, v, p["dw_w"], carry=[dpw2_w] + list(carry))
    dpw1_w = mlp_dw_up(u, dv, name="mm_dpw1")
    dh_in, dg = mlp_du_rms_bwd(dv, p["pw1_w"], p["layer"], h, p["norm"], dh, name="mm_du_pw1")
    grads = dict(norm=dg, pw1_w=dpw1_w, pw1_b=dpw1_b, dw_w=ddw_w, dw_b=ddw_b, ln_g=dln_g, ln_b=dln_b, pw2_b=dpw2_b)
    return dh_in, grads, got


def mlp_fwd(h, p):
    act, u = mlp_up(h, p["norm"], p["w_up"], p["layer"])
    out = matmul(act, p["w_down"], b_kind="rowshard", layer=p["layer"], epilogue="resid", extra=h, name="mm_down")
    return out, (h, u, act)


def mlp_bwd(dh, p, saved):
    h, u, act = saved
    da = mlp_dact(dh, p["w_down"], act, p["layer"])
    dw_down = mlp_dw_down(act, dh).reshape(N_DEV, FF_BLOCK, D_MODEL)
    dw_up = mlp_dw_up(u, da)
    dh_in, dg = mlp_du_rms_bwd(da, p["w_up"], p["layer"], h, p["norm"], dh)
    return dh_in, dict(norm=dg, w_up=dw_up, w_down=dw_down)


def local_step(x, target, w, shards, first):
    n_even, n_odd = (DEPTH + 1) // 2, DEPTH // 2
    h = jnp.concatenate([jnp.zeros((FRONT_PAD, D_MODEL), F32), w["meta_tokens"].astype(F32), x], axis=0)
    even_p, odd_p, mlp_p = [None] * n_even, [None] * n_odd, [None] * DEPTH

    def even_params(i, g):
        cw = w["ssm_conv_w"][i]
        return dict(
            layer=0, norm=_row(w["mix_norm_even"][i]), w_in=_w_in_to_kernel(_from_shards(g[0][:, 0], 1)), w_out=g[1],
            cw_x=cw[:, :SSM_INNER], cw_bc=cw[:, SSM_INNER:], cb_x=_row(w["ssm_conv_b"][i][:SSM_INNER]),
            cb_bc=_row(w["ssm_conv_b"][i][SSM_INNER:]), dt_bias=_row(w["dt_bias"][i], LANE),
            a_log=_row(w["a_log"][i], LANE), d_exp=_row(jnp.repeat(w["d_skip"][i], SSM_HEAD_DIM)),
            ssm_norm=_row(w["ssm_norm_w"][i]), q_norm=_row(jnp.tile(w["q_norm"][i], ATT_HEADS)),
            k_norm=_row(jnp.tile(w["k_norm"][i], ATT_KV_HEADS)), sinks=w["sinks"][i].astype(F32))

    def odd_params(i, g):
        return dict(
            layer=0, norm=_row(w["mix_norm_odd"][i]), pw1_w=g[0], pw1_b=_row(w["pw1_b"][i]),
            dw_w=jnp.pad(w["dw_w"][i], ((0, CONF_HALO - CONF_KERNEL), (0, 0))), dw_b=_row(w["dw_b"][i]),
            ln_g=_row(w["ln_g"][i]), ln_b=_row(w["ln_b"][i]), pw2_w=g[1], pw2_b=_row(w["pw2_b"][i]))

    gathered = {0: first}
    tape = []
    for layer in range(DEPTH):
        g = gathered.pop(layer)
        mlp_p[layer] = dict(layer=0, norm=_row(w["mlp_norm"][layer]), w_up=g[2], w_down=g[3])
        if layer % 2 == 0:
            even_p[layer // 2] = even_params(layer // 2, g)
            ahead = [l for l in (layer + 1, layer + 2) if l < DEPTH and l not in gathered]
            ride_att = shards[ahead[0]] if len(ahead) > 0 else ()
            ride_ssd = shards[ahead[1]] if len(ahead) > 1 else ()
            h, saved, got_att, got_ssd = even_fwd(h, even_p[layer // 2], gather_att=ride_att, gather_ssd=ride_ssd)
            for l, got in zip(ahead, (got_att, got_ssd)):
                gathered[l] = got
        else:
            odd_p[layer // 2] = odd_params(layer // 2, g)
            h, saved = conf_fwd(h, odd_p[layer // 2])
        tape.append(saved)
        h, saved = mlp_fwd(h, mlp_p[layer])
        tape.append(saved)
    dh, loss_row = loss_fwd_bwd(h, target)

    ge = [None] * n_even
    go = [None] * n_odd
    gm = [None] * DEPTH
    received = {n: [None] * shape[0] for n, shape, _ in PARAMS if n in MATMUL_WEIGHTS}
    pending = []

    def store(tags, arrays):
        for (n, l), a in zip(tags, arrays):
            received[n][l] = a

    for layer in reversed(range(DEPTH)):
        i = layer // 2
        dh, gm[layer] = mlp_bwd(dh, mlp_p[layer], tape.pop())
        mlp_tags = [("w_up", layer), ("w_down", layer)]
        mlp_parts = [gm[layer]["w_up"], gm[layer]["w_down"]]
        if layer % 2 == 0:
            riders, pending = pending, []
            dh, ge[i], got_att, got_ssd = even_bwd(dh, even_p[i], tape.pop(), carry_att=mlp_parts,
                                                   carry_ssd=[a for _, _, a in riders])
            store(mlp_tags, got_att)
            store([("w_out", i)] + [(n, l) for n, l, _ in riders], got_ssd)
            pending.append(("w_in", i, ge[i]["w_in"]))
        else:
            dh, go[i], got = conf_bwd(dh, odd_p[i], tape.pop(), carry=mlp_parts)
            store([("pw2_w", i)] + mlp_tags, got)
            pending.append(("pw1_w", i, go[i]["pw1_w"]))

    stack = lambda gs, f: jnp.stack([f(g) for g in gs])
    grads = dict(
        meta_tokens=dh[FRONT_PAD:BLOCK],
        mix_norm_even=stack(ge, lambda g: g["norm"][0]),
        ssm_conv_w=stack(ge, lambda g: jnp.concatenate([g["cw_x"], g["cw_bc"]], axis=1)),
        ssm_conv_b=stack(ge, lambda g: jnp.concatenate([g["cb_x"][0], g["cb_bc"][0]])),
        dt_bias=stack(ge, lambda g: g["dt_bias"][0, :SSM_HEADS]),
        a_log=stack(ge, lambda g: g["a_log"][0, :SSM_HEADS]),
        d_skip=stack(ge, lambda g: g["d_skip"][0, :SSM_HEADS]),
        ssm_norm_w=stack(ge, lambda g: g["ssm_norm"][0]),
        q_norm=stack(ge, lambda g: g["q_norm"][0, :HEAD_DIM]),
        k_norm=stack(ge, lambda g: g["k_norm"][0, :HEAD_DIM]),
        sinks=stack(ge, lambda g: g["sinks"][0, :ATT_HEADS]),
        mix_norm_odd=stack(go, lambda g: g["norm"][0]),
        pw1_b=stack(go, lambda g: g["pw1_b"][0]),
        dw_w=stack(go, lambda g: g["dw_w"][:CONF_KERNEL]),
        dw_b=stack(go, lambda g: g["dw_b"][0]),
        ln_g=stack(go, lambda g: g["ln_g"][0]),
        ln_b=stack(go, lambda g: g["ln_b"][0]),
        pw2_b=stack(go, lambda g: g["pw2_b"][0]),
        mlp_norm=stack(gm, lambda g: g["norm"][0]),
    )
    return loss_row[0, 0], dh[BLOCK:], grads, received, pending


PARAMS = (
    ("meta_tokens", (16, 1024), 1), ("mix_norm_even", (2, 1024), None), ("w_in", (2, 1024, 3088), 2),
    ("ssm_conv_w", (2, 4, 1280), 2), ("ssm_conv_b", (2, 1280), None), ("dt_bias", (2, 16), None),
    ("a_log", (2, 16), None), ("d_skip", (2, 16), None), ("ssm_norm_w", (2, 1024), None), ("q_norm", (2, 64), None),
    ("k_norm", (2, 64), None), ("sinks", (2, 8), None), ("w_out", (2, 1536, 1024), 1), ("mix_norm_odd", (2, 1024), 1),
    ("pw1_w", (2, 1024, 2048), 2), ("pw1_b", (2, 2048), 1), ("dw_w", (2, 31, 1024), 2), ("dw_b", (2, 1024), 1),
    ("ln_g", (2, 1024), 1), ("ln_b", (2, 1024), 1), ("pw2_w", (2, 1024, 1024), 1), ("pw2_b", (2, 1024), 1),
    ("mlp_norm", (4, 1024), None), ("w_up", (4, 1024, 4096), 2), ("w_down", (4, 4096, 1024), 1),
)
MATMUL_WEIGHTS = ("w_in", "w_out", "pw1_w", "pw2_w", "w_up", "w_down")
PACK_ROW_ALIGN = 16 * PACK_W


def _block_shape(shape, axis):
    if axis is None:
        return tuple(shape)
    return tuple(s // N_DEV if a == axis else s for a, s in enumerate(shape))


def _numel(shape):
    return math.prod(shape)


def _pack(arrays, dtype):
    flat = jnp.concatenate([a.reshape(-1).astype(dtype) for a in arrays])
    n = flat.shape[0]
    padded = -(-n // PACK_ROW_ALIGN) * PACK_ROW_ALIGN
    return jnp.pad(flat, (0, padded - n)).reshape(-1, PACK_W)


def _pack_rows(arrays_by_dev, dtype):
    flat = jnp.concatenate([a.reshape(N_DEV, -1).astype(dtype) for a in arrays_by_dev], axis=1)
    n = flat.shape[1]
    padded = -(-n // PACK_ROW_ALIGN) * PACK_ROW_ALIGN
    return jnp.pad(flat, ((0, 0), (0, padded - n))).reshape(N_DEV, -1, PACK_W)


def _to_shards(full, axis):
    shape = full.shape
    split = full.reshape(shape[:axis] + (N_DEV, shape[axis] // N_DEV) + shape[axis + 1:])
    return jnp.moveaxis(split, axis, 0)


def _from_shards(blocks, axis):
    moved = jnp.moveaxis(blocks, 0, axis)
    shape = moved.shape
    return moved.reshape(shape[:axis] + (shape[axis] * shape[axis + 1],) + shape[axis + 2:])


_MESH = pl.DeviceIdType.MESH
_ANY = pl.BlockSpec(memory_space=pl.ANY)


def _mesh_place():
    x, y, c = lax.axis_index("x"), lax.axis_index("y"), lax.axis_index("c")
    return x, y, c


def _peer(x, y, c, rel):
    dx, dy, dc = (rel >> 2) & 1, (rel >> 1) & 1, rel & 1
    return (x ^ dx if dx else x, y ^ dy if dy else y, c ^ dc if dc else c)


def _dev_index(x, y, c):
    return 4 * x + 2 * y + c


def all_gather_weights(bigs, small):
    nt = len(bigs)

    def body(*refs):
        big_refs, small_ref = refs[:nt], refs[nt]
        big_outs, small_out = refs[nt + 1:2 * nt + 1], refs[2 * nt + 1]
        send_sems, recv_sems, small_send, small_recv, local_sems = refs[2 * nt + 2:]
        x, y, c = _mesh_place()
        me = (x, y, c)
        sibling = (x, y, 1 - c)
        chips = [(1 - x, y), (x, 1 - y), (1 - x, 1 - y)]

        def big_copy(t, k, block, to, from_input=False):
            dst = big_outs[t].at[_dev_index(*block)]
            return pltpu.make_async_remote_copy(src_ref=big_refs[t] if from_input else dst, dst_ref=dst,
                                                send_sem=send_sems.at[t, k], recv_sem=recv_sems.at[t, k],
                                                device_id=to, device_id_type=_MESH)

        def small_copy(rel, block, to):
            return pltpu.make_async_remote_copy(src_ref=small_ref, dst_ref=small_out.at[_dev_index(*block)],
                                                send_sem=small_send.at[rel - 1], recv_sem=small_recv.at[rel - 1],
                                                device_id=to, device_id_type=_MESH)

        mine = [pltpu.make_async_copy(big_refs[t], big_outs[t].at[_dev_index(*me)], local_sems.at[t]) for t in range(nt)]
        mine.append(pltpu.make_async_copy(small_ref, small_out.at[_dev_index(*me)], local_sems.at[nt]))
        for cp in mine:
            cp.start()
        first = []
        for t in range(nt):
            first.append(big_copy(t, 0, me, sibling, from_input=True))
            first += [big_copy(t, 1 + j, me, (*chip, c), from_input=True) for j, chip in enumerate(chips)]
        for cp in first:
            cp.start()
        smalls = [small_copy(rel, me, _peer(x, y, c, rel)) for rel in range(1, N_DEV)]
        for cp in smalls:
            cp.start()
        passed = []
        for j, chip in enumerate(chips):
            for t in range(nt):
                big_copy(t, 1 + j, (*chip, c), me).wait_recv()
                fwd = big_copy(t, 4 + j, (*chip, c), sibling)
                fwd.start()
                passed.append(fwd)
        for t in range(nt):
            big_copy(t, 0, sibling, me).wait_recv()
            for j, chip in enumerate(chips):
                big_copy(t, 4 + j, (*chip, 1 - c), me).wait_recv()
        for rel in range(1, N_DEV):
            small_copy(rel, _peer(x, y, c, rel), me).wait_recv()
        for cp in first + passed + smalls:
            cp.wait_send()
        for cp in mine:
            cp.wait()

    return pl.pallas_call(
        body, in_specs=[_ANY] * (nt + 1), out_specs=[_ANY] * (nt + 1),
        out_shape=[jax.ShapeDtypeStruct((N_DEV,) + b.shape, b.dtype) for b in bigs]
        + [jax.ShapeDtypeStruct((N_DEV,) + small.shape, small.dtype)],
        scratch_shapes=[pltpu.SemaphoreType.DMA((nt, N_DEV - 1)), pltpu.SemaphoreType.DMA((nt, N_DEV - 1)),
                        pltpu.SemaphoreType.DMA((N_DEV - 1,)), pltpu.SemaphoreType.DMA((N_DEV - 1,)),
                        pltpu.SemaphoreType.DMA((nt + 1,))],
        name="all_gather_weights",
    )(*bigs, small)


def _gather_copies(in_refs, out_refs, send_sems, recv_sems, local_sems):
    x, y, c = _mesh_place()
    me = (x, y, c)
    sibling = (x, y, 1 - c)
    chips = [(1 - x, y), (x, 1 - y), (1 - x, 1 - y)]
    nt = len(in_refs)

    def copy(t, k, block, to, from_input=False):
        dst = out_refs[t].at[_dev_index(*block)]
        return pltpu.make_async_remote_copy(src_ref=in_refs[t] if from_input else dst, dst_ref=dst,
                                            send_sem=send_sems.at[t, k], recv_sem=recv_sems.at[t, k],
                                            device_id=to, device_id_type=_MESH)

    mine = [pltpu.make_async_copy(in_refs[t], out_refs[t].at[_dev_index(*me)], local_sems.at[t]) for t in range(nt)]
    first, landed, forward, last = [], [], [], []
    for t in range(nt):
        first.append(copy(t, 0, me, sibling, from_input=True))
        last.append(copy(t, 0, sibling, me))
        for j, chip in enumerate(chips):
            first.append(copy(t, 1 + j, me, (*chip, c), from_input=True))
            landed.append(copy(t, 1 + j, (*chip, c), me))
            forward.append(copy(t, 4 + j, (*chip, c), sibling))
            last.append(copy(t, 4 + j, (*chip, 1 - c), me))
    return mine, first, landed, forward, last


GATHER_FORWARD_LEAD = 8


def _with_gather(body, n_in, n_out, shards, steps):
    n = len(shards)
    if n == 0:
        return body, [], [], [], []
    fwd_step = max(steps - 1 - GATHER_FORWARD_LEAD, 0)

    def wrapped(*refs):
        ins, g_in = refs[:n_in], refs[n_in:n_in + n]
        outs, g_out = refs[n_in + n:n_in + n + n_out], refs[n_in + n + n_out:n_in + 2 * n + n_out]
        scratch = refs[n_in + 2 * n + n_out:len(refs) - 3]
        sems = refs[len(refs) - 3:]
        i = pl.program_id(0)

        @pl.when(i == 0)
        def _():
            mine, first, _, _, _ = _gather_copies(g_in, g_out, *sems)
            for cp in mine + first:
                cp.start()

        @pl.when(i == fwd_step)
        def _():
            _, _, landed, forward, _ = _gather_copies(g_in, g_out, *sems)
            for arrived, onward in zip(landed, forward):
                arrived.wait_recv()
                onward.start()

        body(*ins, *outs, *scratch)

        @pl.when(i == steps - 1)
        def _():
            mine, first, _, forward, last = _gather_copies(g_in, g_out, *sems)
            for cp in last:
                cp.wait_recv()
            for cp in first + forward:
                cp.wait_send()
            for cp in mine:
                cp.wait()

    return (wrapped, [_ANY] * n, [_ANY] * n, [jax.ShapeDtypeStruct((N_DEV,) + a.shape, a.dtype) for a in shards],
            [pltpu.SemaphoreType.DMA((n, N_DEV - 1)), pltpu.SemaphoreType.DMA((n, N_DEV - 1)),
             pltpu.SemaphoreType.DMA((n,))])


def _exchange_copies(in_refs, out_refs, send_sems, recv_sems, local_sems):
    x, y, c = _mesh_place()
    me = _dev_index(x, y, c)
    mine, sends, arrivals = [], [], []
    for p, (src, dst) in enumerate(zip(in_refs, out_refs)):
        mine.append(pltpu.make_async_copy(src.at[me], dst.at[me], local_sems.at[p]))
        for rel in range(1, N_DEV):
            peer = _peer(x, y, c, rel)
            there = _dev_index(*peer)
            sems = dict(send_sem=send_sems.at[rel - 1, p], recv_sem=recv_sems.at[rel - 1, p], device_id=peer,
                        device_id_type=_MESH)
            sends.append(pltpu.make_async_remote_copy(src_ref=src.at[there], dst_ref=dst.at[me], **sems))
            arrivals.append(pltpu.make_async_remote_copy(src_ref=src.at[me], dst_ref=dst.at[there], **sems))
    return mine, sends, arrivals


def _with_exchange(body, n_in, n_out, carry, steps):
    n = len(carry)
    if n == 0:
        return body, [], [], [], []

    def wrapped(*refs):
        ins, ex_in = refs[:n_in], refs[n_in:n_in + n]
        outs, ex_out = refs[n_in + n:n_in + n + n_out], refs[n_in + n + n_out:n_in + 2 * n + n_out]
        scratch = refs[n_in + 2 * n + n_out:len(refs) - 3]
        send_sems, recv_sems, local_sems = refs[len(refs) - 3:]
        i = pl.program_id(0)

        @pl.when(i == 0)
        def _():
            mine, sends, _ = _exchange_copies(ex_in, ex_out, send_sems, recv_sems, local_sems)
            for cp in mine + sends:
                cp.start()

        body(*ins, *outs, *scratch)

        @pl.when(i == steps - 1)
        def _():
            mine, sends, arrivals = _exchange_copies(ex_in, ex_out, send_sems, recv_sems, local_sems)
            for cp in arrivals:
                cp.wait_recv()
            for cp in sends:
                cp.wait_send()
            for cp in mine:
                cp.wait()

    return (wrapped, [_ANY] * n, [_ANY] * n, [jax.ShapeDtypeStruct(a.shape, a.dtype) for a in carry],
            [pltpu.SemaphoreType.DMA((N_DEV - 1, n)), pltpu.SemaphoreType.DMA((N_DEV - 1, n)),
             pltpu.SemaphoreType.DMA((n,))])


def exchange_gradients(arrays):
    n = len(arrays)

    def body(*refs):
        mine, sends, arrivals = _exchange_copies(refs[:n], refs[n:2 * n], *refs[2 * n:])
        for cp in mine + sends:
            cp.start()
        for cp in arrivals:
            cp.wait_recv()
        for cp in sends:
            cp.wait_send()
        for cp in mine:
            cp.wait()

    return pl.pallas_call(
        body, in_specs=[_ANY] * n, out_specs=[_ANY] * n,
        out_shape=[jax.ShapeDtypeStruct(a.shape, a.dtype) for a in arrays],
        scratch_shapes=[pltpu.SemaphoreType.DMA((N_DEV - 1, n)), pltpu.SemaphoreType.DMA((N_DEV - 1, n)),
                        pltpu.SemaphoreType.DMA((n,))],
        name="exchange_gradients",
    )(*arrays)


def reduce_adamw(parts, w, m, v, tr):
    nl, r, cols = w.shape
    assert len(parts) == nl

    def body(*refs):
        p_refs = refs[:nl]
        w_ref, m_ref, v_ref, g_ref, d_ref, nm_ref, nv_ref, g_acc = refs[nl:]
        layer = pl.program_id(0)
        for l in range(nl):
            @pl.when(layer == l)
            def _(l=l):
                g = p_refs[l][0].astype(F32)
                for d in range(1, N_DEV):
                    g = g + p_refs[l][d].astype(F32)
                g_acc[...] = g

        g = g_acc[...]
        g_ref[...] = g
        nm = ADAM_B1 * m_ref[...] + (1.0 - ADAM_B1) * g
        nv = ADAM_B2 * v_ref[...] + (1.0 - ADAM_B2) * (g * g)
        nm_ref[...] = nm
        nv_ref[...] = nv
        m_hat = nm / (1.0 - ADAM_B1 ** ADAM_STEP)
        v_hat = nv / (1.0 - ADAM_B2 ** ADAM_STEP)
        d_ref[...] = -ADAM_LR * (m_hat / (jnp.sqrt(v_hat) + ADAM_EPS) + ADAM_WD * w_ref[...])

    row = pl.BlockSpec((None, tr, cols), lambda l, i: (l, i, 0))

    def part_spec(own):
        def index(l, i):
            return (0, jnp.where(l == own, i, jnp.where(l < own, 0, r // tr - 1)), 0)
        return pl.BlockSpec((N_DEV, tr, cols), index)

    return pl.pallas_call(
        body, grid=(nl, r // tr),
        in_specs=[part_spec(l) for l in range(nl)] + [row, row, row],
        out_specs=[row, row, row, row], out_shape=[jax.ShapeDtypeStruct((nl, r, cols), F32)] * 4,
        scratch_shapes=[pltpu.VMEM((tr, cols), F32)],
        name="reduce_adamw", compiler_params=_params("arbitrary", "arbitrary"),
    )(*parts, w, m, v)


ADAMW_TILE_BYTES = 1 << 19


def _adamw_tile(rows, cols):
    lanes = -(-cols // LANE) * LANE
    best = None
    for tr in range(16, rows + 1, 16):
        if rows % tr == 0 and tr * lanes * 4 <= ADAMW_TILE_BYTES:
            best = tr
    if best is None:
        raise ValueError((rows, cols))
    return best


def kernel(x, meta_tokens, mix_norm_even, w_in, ssm_conv_w, ssm_conv_b, dt_bias, a_log, d_skip, ssm_norm_w, q_norm, k_norm, sinks, w_out, mix_norm_odd, pw1_w, pw1_b, dw_w, dw_b, ln_g, ln_b, pw2_w, pw2_b, mlp_norm, w_up, w_down, loss_target, m_meta_tokens, m_mix_norm_even, m_w_in, m_ssm_conv_w, m_ssm_conv_b, m_dt_bias, m_a_log, m_d_skip, m_ssm_norm_w, m_q_norm, m_k_norm, m_sinks, m_w_out, m_mix_norm_odd, m_pw1_w, m_pw1_b, m_dw_w, m_dw_b, m_ln_g, m_ln_b, m_pw2_w, m_pw2_b, m_mlp_norm, m_w_up, m_w_down, v_meta_tokens, v_mix_norm_even, v_w_in, v_ssm_conv_w, v_ssm_conv_b, v_dt_bias, v_a_log, v_d_skip, v_ssm_norm_w, v_q_norm, v_k_norm, v_sinks, v_w_out, v_mix_norm_odd, v_pw1_w, v_pw1_b, v_dw_w, v_dw_b, v_ln_g, v_ln_b, v_pw2_w, v_pw2_b, v_mlp_norm, v_w_up, v_w_down):
    names = [p[0] for p in PARAMS]
    w_loc = dict(zip(names, (meta_tokens, mix_norm_even, w_in, ssm_conv_w, ssm_conv_b, dt_bias, a_log, d_skip, ssm_norm_w, q_norm, k_norm, sinks, w_out, mix_norm_odd, pw1_w, pw1_b, dw_w, dw_b, ln_g, ln_b, pw2_w, pw2_b, mlp_norm, w_up, w_down)))
    m_loc = dict(zip(names, (m_meta_tokens, m_mix_norm_even, m_w_in, m_ssm_conv_w, m_ssm_conv_b, m_dt_bias, m_a_log, m_d_skip, m_ssm_norm_w, m_q_norm, m_k_norm, m_sinks, m_w_out, m_mix_norm_odd, m_pw1_w, m_pw1_b, m_dw_w, m_dw_b, m_ln_g, m_ln_b, m_pw2_w, m_pw2_b, m_mlp_norm, m_w_up, m_w_down)))
    v_loc = dict(zip(names, (v_meta_tokens, v_mix_norm_even, v_w_in, v_ssm_conv_w, v_ssm_conv_b, v_dt_bias, v_a_log, v_d_skip, v_ssm_norm_w, v_q_norm, v_k_norm, v_sinks, v_w_out, v_mix_norm_odd, v_pw1_w, v_pw1_b, v_dw_w, v_dw_b, v_ln_g, v_ln_b, v_pw2_w, v_pw2_b, v_mlp_norm, v_w_up, v_w_down)))
    small_sharded = [p for p in PARAMS if p[2] is not None and p[0] not in MATMUL_WEIGHTS]
    replicated = [p for p in PARAMS if p[2] is None]
    small_list = small_sharded + replicated

    def layer_shards(layer):
        i = layer // 2
        mixer = ("w_in", "w_out") if layer % 2 == 0 else ("pw1_w", "pw2_w")
        return [w_loc[n][i:i + 1].astype(MXU_DTYPE) for n in mixer] + [
            w_loc[n][layer:layer + 1].astype(MXU_DTYPE) for n in ("w_up", "w_down")]

    shards = [layer_shards(layer) for layer in range(DEPTH)]
    gathered = all_gather_weights(shards[0], _pack([w_loc[n] for n, _, _ in small_sharded], F32))
    w_full = {n: w_loc[n] for n, _, _ in replicated}
    flat = gathered[-1].reshape(N_DEV, -1)
    off = 0
    for n, shape, axis in small_sharded:
        blk = _block_shape(shape, axis)
        w_full[n] = _from_shards(flat[:, off:off + _numel(blk)].reshape((N_DEV,) + blk), axis)
        off += _numel(blk)

    loss_local, grad_x, g_full, received, pending = local_step(x[0], loss_target[0], w_full, shards, gathered[:-1])
    loss = lax.psum(loss_local, ("x", "y", "c"))

    by_dev = [_to_shards(g_full[n], axis) for n, _, axis in small_sharded]
    by_dev += [jnp.broadcast_to(g_full[n][None], (N_DEV,) + tuple(shape)) for n, shape, _ in replicated]
    last = exchange_gradients([a for _, _, a in pending] + [_pack_rows(by_dev, F32)])
    for (n, l, _), a in zip(pending, last[:-1]):
        received[n][l] = a

    out = {}
    for n in MATMUL_WEIGHTS:
        nl, r, cols = w_loc[n].shape
        out[n] = reduce_adamw(received[n], w_loc[n], m_loc[n], v_loc[n], _adamw_tile(r, cols))
    pk = lambda d: _pack([d[n] for n, _, _ in small_list], F32)[None]
    rows = last[-1].shape[1]
    small_out = reduce_adamw([last[-1]], pk(w_loc), pk(m_loc), pk(v_loc), _adamw_tile(rows, PACK_W))
    flats = [buf.reshape(-1) for buf in small_out]
    off = 0
    for n, shape, axis in small_list:
        blk = _block_shape(shape, axis)
        out[n] = tuple(f[off:off + _numel(blk)].reshape(blk) for f in flats)
        off += _numel(blk)
    return (loss, grad_x[None], *[out[n][0] for n in names], *[out[n][1] for n in names],
            *[out[n][2] for n in names], *[out[n][3] for n in names])
```

```python
import math

import jax
import jax.numpy as jnp
from jax import lax
from jax.experimental import pallas as pl
from jax.experimental.pallas import tpu as pltpu

F32 = jnp.float32
MXU_DTYPE = jnp.bfloat16
GRAD_WIRE_DTYPE = jnp.bfloat16
HIGHEST = lax.Precision.HIGHEST

D_MODEL = 1024
N_META = 16
BLOCK = 128
FRONT_PAD = BLOCK - N_META
ATT_HEADS = 8
ATT_KV_HEADS = 2
HEAD_DIM = 64
SSM_HEADS = 16
SSM_HEAD_DIM = 64
SSM_INNER = 1024
SSM_GROUPS = 2
SSM_STATE = 64
SSM_CONV = 4
CONF_KERNEL = 31
D_FF = 4096
EPS = 1e-6
LN_EPS = 1e-5
Q_W = 512
KV_W = 128
IN_W = 3088
MIX_W = 1536
DEPTH = 4
N_DEV = 8

ADAM_LR = 0.001
ADAM_B1 = 0.9
ADAM_B2 = 0.999
ADAM_EPS = 1e-08
ADAM_WD = 0.01
ADAM_STEP = 10

PROJ_W = 3200
COL_Z, COL_XS, COL_Q, COL_BC, COL_K, COL_V, COL_DT = 0, 1024, 2048, 2560, 2816, 2944, 3072

ROW_TILE = 640
TN_ROW_TILE = 1664
ACC_BYTES = 8 * 1024 * 1024
VMEM_LIMIT = 56 * 1024 * 1024
LANE = 128
PACK_W = 1024


def _params(*sem):
    return pltpu.CompilerParams(dimension_semantics=sem, vmem_limit_bytes=VMEM_LIMIT)


def _mx(x):
    return x.astype(MXU_DTYPE)


def _dot(a, b):
    return jnp.dot(_mx(a), _mx(b), preferred_element_type=F32)


def _dot_nt(a, b):
    return lax.dot_general(_mx(a), _mx(b), (((1,), (1,)), ((), ())), preferred_element_type=F32)


def _dot_tn(a, b):
    return lax.dot_general(_mx(a), _mx(b), (((0,), (0,)), ((), ())), preferred_element_type=F32)


def _split3(x):
    hi = x.astype(jnp.bfloat16)
    r1 = x - hi.astype(F32)
    mid = r1.astype(jnp.bfloat16)
    lo = (r1 - mid.astype(F32)).astype(jnp.bfloat16)
    return hi, mid, lo


def _sel_dot(x, sel, dims):
    x_first = dims[2]
    if sel.dtype == jnp.bool_:
        sel = jnp.where(sel, 1.0, 0.0)
    one = sel.astype(jnp.bfloat16)
    acc = None
    for part in _split3(x):
        args = (part, one) if x_first else (one, part)
        t = lax.dot_general(*args, (dims[:2], ((), ())), preferred_element_type=F32)
        acc = t if acc is None else acc + t
    return acc


def _dot_hi(a, b, exact="a"):
    if exact == "a":
        return _sel_dot(a, b, ((1,), (0,), True))
    return _sel_dot(b, a, ((1,), (0,), False))


def _dot_tn_hi(a, b):
    return _sel_dot(a, b, ((0,), (0,), True))


def _sigmoid(x):
    return 1.0 / (1.0 + jnp.exp(-x))


def _row_ids(start, rows, cols):
    return start + lax.broadcasted_iota(jnp.int32, (rows, cols), 0)


def rms_fwd(h, g, name="rms_fwd"):
    m, d = h.shape
    tm = ROW_TILE

    def body(h_ref, g_ref, u_ref):
        x = h_ref[...]
        r = lax.rsqrt(jnp.mean(x * x, -1, keepdims=True) + EPS)
        u_ref[...] = ((x * r) * g_ref[...]).astype(u_ref.dtype)

    return pl.pallas_call(
        body, grid=(m // tm,),
        in_specs=[pl.BlockSpec((tm, d), lambda i: (i, 0)), pl.BlockSpec((1, d), lambda i: (0, 0))],
        out_specs=pl.BlockSpec((tm, d), lambda i: (i, 0)),
        out_shape=jax.ShapeDtypeStruct((m, d), MXU_DTYPE), name=name, compiler_params=_params("arbitrary"),
    )(h, g)


def rms_bwd(h, g, du, dh_out, name="rms_bwd"):
    m, d = h.shape
    tm = ROW_TILE

    def body(h_ref, g_ref, du_ref, dho_ref, dh_ref, dg_ref):
        @pl.when(pl.program_id(0) == 0)
        def _():
            dg_ref[...] = jnp.zeros_like(dg_ref)

        x = h_ref[...]
        du_ = du_ref[...]
        r = lax.rsqrt(jnp.mean(x * x, -1, keepdims=True) + EPS)
        gy = du_ * g_ref[...]
        dx = r * gy - x * ((r * r * r) * jnp.mean(x * gy, -1, keepdims=True))
        dh_ref[...] = dho_ref[...] + dx
        dg_ref[...] += jnp.sum(du_ * (x * r), axis=0, keepdims=True)

    row = pl.BlockSpec((tm, d), lambda i: (i, 0))
    vec = pl.BlockSpec((1, d), lambda i: (0, 0))
    return pl.pallas_call(
        body, grid=(m // tm,), in_specs=[row, vec, row, row], out_specs=[row, vec],
        out_shape=[jax.ShapeDtypeStruct((m, d), F32), jax.ShapeDtypeStruct((1, d), F32)],
        name=name, compiler_params=_params("arbitrary"),
    )(h, g, du, dh_out)


def loss_fwd_bwd(h, target, name="loss"):
    m, d = h.shape
    nb = m // BLOCK

    def body(h_ref, t_ref, dh_ref, l_ref):
        i = pl.program_id(0)

        @pl.when(i == 0)
        def _():
            l_ref[...] = jnp.zeros_like(l_ref)
            dh_ref[...] = jnp.zeros_like(dh_ref)

        @pl.when(i > 0)
        def _():
            e = h_ref[...] - t_ref[...]
            dh_ref[...] = e * (1.0 / d)
            s = jnp.sum(jnp.sum(e * e, axis=-1, keepdims=True), axis=0, keepdims=True)
            l_ref[...] += jnp.broadcast_to(s * (0.5 / d), l_ref.shape)

    return pl.pallas_call(
        body, grid=(nb,),
        in_specs=[pl.BlockSpec((BLOCK, d), lambda i: (i, 0)),
                  pl.BlockSpec((BLOCK, d), lambda i: (jnp.maximum(i - 1, 0), 0))],
        out_specs=[pl.BlockSpec((BLOCK, d), lambda i: (i, 0)), pl.BlockSpec((1, LANE), lambda i: (0, 0))],
        out_shape=[jax.ShapeDtypeStruct((m, d), F32), jax.ShapeDtypeStruct((1, LANE), F32)],
        name=name, compiler_params=_params("arbitrary"),
    )(h, target)


def col_sum(x, name="col_sum"):
    m, n = x.shape
    tm = ROW_TILE

    def body(x_ref, o_ref):
        @pl.when(pl.program_id(0) == 0)
        def _():
            o_ref[...] = jnp.zeros_like(o_ref)

        o_ref[...] += jnp.sum(x_ref[...].astype(F32), axis=0, keepdims=True)

    return pl.pallas_call(
        body, grid=(m // tm,), in_specs=[pl.BlockSpec((tm, n), lambda i: (i, 0))],
        out_specs=pl.BlockSpec((1, n), lambda i: (0, 0)), out_shape=jax.ShapeDtypeStruct((1, n), F32),
        name=name, compiler_params=_params("arbitrary"),
    )(x)


def matmul(a, b, *, b_kind="full", layer=0, trans_b=False, tn=None, epilogue=None, bias=None, extra=None,
           out_dtype=F32, name="matmul"):
    m, k = a.shape
    tm = ROW_TILE
    merge = False
    if b_kind == "full":
        n = b.shape[0] if trans_b else b.shape[1]
        tn = n if tn is None else tn
        b_spec = pl.BlockSpec((tn, k), lambda i, j: (j, 0)) if trans_b else pl.BlockSpec((k, tn), lambda i, j: (0, j))
    elif b_kind == "rowshard":
        ks, wn = b.shape[2], b.shape[3]
        if trans_b and tn == ks:
            assert wn == k
            n = N_DEV * ks
            b_spec = pl.BlockSpec((None, None, ks, wn), lambda i, j: (j, layer, 0, 0))
        else:
            assert tn is None
            merge = True
            n = N_DEV * ks if trans_b else wn
            assert (wn if trans_b else N_DEV * ks) == k
            tn = n
            b_spec = pl.BlockSpec((N_DEV, None, ks, wn), lambda i, j: (0, layer, 0, 0))
    else:
        raise ValueError(b_kind)
    has_bias = bias is not None
    has_extra = extra is not None

    def body(*refs):
        a_ref, b_ref = refs[0], refs[1]
        pos = 2
        bias_ref = extra_ref = None
        if has_bias:
            bias_ref = refs[pos]
            pos += 1
        if has_extra:
            extra_ref = refs[pos]
            pos += 1
        outs = refs[pos:]
        w = b_ref[...]
        if merge:
            w = w.reshape(N_DEV * w.shape[1], w.shape[2])
        if trans_b:
            acc = _dot_nt(a_ref[...], w)
        else:
            acc = _dot(a_ref[...], w)
        if has_bias:
            acc = acc + bias_ref[...]
        if epilogue is None:
            outs[0][...] = acc.astype(outs[0].dtype)
        elif epilogue == "relu2":
            outs[0][...] = acc
            r = jnp.maximum(acc, 0.0)
            outs[1][...] = (r * r).astype(outs[1].dtype)
        elif epilogue == "drelu2":
            outs[0][...] = (acc * (2.0 * jnp.maximum(extra_ref[...], 0.0))).astype(outs[0].dtype)
        elif epilogue == "resid":
            rows = _row_ids(pl.program_id(0) * tm, tm, tn)
            outs[0][...] = extra_ref[...] + jnp.where(rows >= FRONT_PAD, acc, 0.0)
        else:
            raise ValueError(epilogue)

    in_specs = [pl.BlockSpec((tm, k), lambda i, j: (i, 0)), b_spec]
    args = [a, b]
    if has_bias:
        in_specs.append(pl.BlockSpec((1, tn), lambda i, j: (0, j)))
        args.append(bias)
    if has_extra:
        in_specs.append(pl.BlockSpec((tm, tn), lambda i, j: (i, j)))
        args.append(extra)
    tile = pl.BlockSpec((tm, tn), lambda i, j: (i, j))
    if epilogue == "relu2":
        out_specs = [tile, tile]
        out_shape = [jax.ShapeDtypeStruct((m, n), F32), jax.ShapeDtypeStruct((m, n), MXU_DTYPE)]
    else:
        out_specs = tile
        out_shape = jax.ShapeDtypeStruct((m, n), out_dtype)
    return pl.pallas_call(
        body, grid=(m // tm, n // tn), in_specs=in_specs, out_specs=out_specs, out_shape=out_shape,
        name=name, compiler_params=_params("arbitrary", "arbitrary"),
    )(*args)


def matmul_tn(x, dy, *, ti, tn, out_dtype=F32, name="matmul_tn"):
    m, k1 = x.shape
    n = dy.shape[1]
    tm = TN_ROW_TILE
    last = m // tm - 1

    def body(x_ref, dy_ref, o_ref, acc_ref):
        r = pl.program_id(2)

        @pl.when(r == 0)
        def _():
            acc_ref[...] = jnp.zeros_like(acc_ref)

        acc_ref[...] += _dot_tn(x_ref[...], dy_ref[...])

        @pl.when(r == last)
        def _():
            o_ref[...] = acc_ref[...].astype(o_ref.dtype)

    out_specs = pl.BlockSpec((ti, tn), lambda i, j, r: (i, j))
    out_shape = jax.ShapeDtypeStruct((k1, n), out_dtype)
    return pl.pallas_call(
        body, grid=(k1 // ti, n // tn, m // tm),
        in_specs=[pl.BlockSpec((tm, ti), lambda i, j, r: (r, i)), pl.BlockSpec((tm, tn), lambda i, j, r: (r, j))],
        out_specs=out_specs, out_shape=out_shape, scratch_shapes=[pltpu.VMEM((ti, tn), F32)], name=name,
        compiler_params=_params("arbitrary", "arbitrary", "arbitrary"),
    )(x, dy)


FF_BLOCK = D_FF // N_DEV
SQRT_FLOOR = 1.1754944e-38


def _ff_cols(d):
    return slice(FF_BLOCK * d, FF_BLOCK * (d + 1))


def mlp_up(h, norm_g, w, layer, *, bias=None, relu2=True, out_dtype=None, name="mlp_up"):
    m = h.shape[0]
    ns = w.shape[3]
    n = N_DEV * ns
    tm = ROW_TILE
    out_dtype = MXU_DTYPE if relu2 else out_dtype
    has_bias = bias is not None

    def body(*refs):
        h_ref, g_ref, w_ref = refs[0], refs[1], refs[2]
        bias_ref = refs[3] if has_bias else None
        o_ref, u_ref = refs[-2], refs[-1]
        x = h_ref[...]
        u_ = _mx((x * lax.rsqrt(jnp.mean(x * x, -1, keepdims=True) + EPS)) * g_ref[...])
        u_ref[...] = u_
        for d in range(N_DEV):
            cols = slice(ns * d, ns * (d + 1))
            r = _dot(u_, w_ref[d])
            if has_bias:
                r = r + bias_ref[:, cols]
            if relu2:
                r = jnp.maximum(r, 0.0)
                r = r * r
            o_ref[:, cols] = r.astype(o_ref.dtype)

    row = pl.BlockSpec((tm, D_MODEL), lambda i: (i, 0))
    in_specs = [row, pl.BlockSpec((1, D_MODEL), lambda i: (0, 0)),
                pl.BlockSpec((N_DEV, None, D_MODEL, ns), lambda i: (0, layer, 0, 0))]
    args = [h, norm_g, w]
    if has_bias:
        in_specs.append(pl.BlockSpec((1, n), lambda i: (0, 0)))
        args.append(bias)
    return pl.pallas_call(
        body, grid=(m // tm,), in_specs=in_specs, out_specs=[pl.BlockSpec((tm, n), lambda i: (i, 0)), row],
        out_shape=[jax.ShapeDtypeStruct((m, n), out_dtype), jax.ShapeDtypeStruct((m, D_MODEL), MXU_DTYPE)],
        name=name, compiler_params=_params("arbitrary"),
    )(*args)


def mlp_dact(dh, w_down, act, layer, name="mlp_dact"):
    m = dh.shape[0]
    tm = ROW_TILE

    def body(dh_ref, w_ref, act_ref, o_ref):
        dh_ = dh_ref[...]
        for d in range(N_DEV):
            p = act_ref[:, _ff_cols(d)].astype(F32)
            r = p * lax.rsqrt(jnp.maximum(p, SQRT_FLOOR))
            o_ref[:, _ff_cols(d)] = (_dot_nt(dh_, w_ref[d]) * (2.0 * r)).astype(o_ref.dtype)

    return pl.pallas_call(
        body, grid=(m // tm,),
        in_specs=[pl.BlockSpec((tm, D_MODEL), lambda i: (i, 0)),
                  pl.BlockSpec((N_DEV, None, FF_BLOCK, D_MODEL), lambda i: (0, layer, 0, 0)),
                  pl.BlockSpec((tm, D_FF), lambda i: (i, 0))],
        out_specs=pl.BlockSpec((tm, D_FF), lambda i: (i, 0)),
        out_shape=jax.ShapeDtypeStruct((m, D_FF), MXU_DTYPE), name=name, compiler_params=_params("arbitrary"),
    )(dh, w_down, act)


def mlp_du_rms_bwd(da, w_up, layer, h, g, dh_out, name="mlp_du"):
    m, n = da.shape
    ns = w_up.shape[3]
    assert n == N_DEV * ns
    tm = ROW_TILE

    def body(da_ref, w_ref, h_ref, g_ref, dho_ref, dh_ref, dg_ref):
        @pl.when(pl.program_id(0) == 0)
        def _():
            dg_ref[...] = jnp.zeros_like(dg_ref)

        du = _dot_nt(da_ref[:, 0:ns], w_ref[0])
        for d in range(1, N_DEV):
            du = du + _dot_nt(da_ref[:, ns * d:ns * (d + 1)], w_ref[d])
        x = h_ref[...]
        r = lax.rsqrt(jnp.mean(x * x, -1, keepdims=True) + EPS)
        gy = du * g_ref[...]
        dx = r * gy - x * ((r * r * r) * jnp.mean(x * gy, -1, keepdims=True))
        dh_ref[...] = dho_ref[...] + dx
        dg_ref[...] += jnp.sum(du * (x * r), axis=0, keepdims=True)

    row = pl.BlockSpec((tm, D_MODEL), lambda i: (i, 0))
    vec = pl.BlockSpec((1, D_MODEL), lambda i: (0, 0))
    return pl.pallas_call(
        body, grid=(m // tm,),
        in_specs=[pl.BlockSpec((tm, n), lambda i: (i, 0)),
                  pl.BlockSpec((N_DEV, None, D_MODEL, ns), lambda i: (0, layer, 0, 0)), row, vec, row],
        out_specs=[row, vec],
        out_shape=[jax.ShapeDtypeStruct((m, D_MODEL), F32), jax.ShapeDtypeStruct((1, D_MODEL), F32)],
        name=name, compiler_params=_params("arbitrary"),
    )(da, w_up, h, g, dh_out)


def mlp_dw_up(u, da, name="mlp_dw_up"):
    m, n = da.shape
    ns = n // N_DEV
    tm = TN_ROW_TILE
    last = m // tm - 1
    parts = -(-D_MODEL * n * 4 // ACC_BYTES)
    per = N_DEV // parts

    def body(u_ref, da_ref, o_ref, acc_ref):
        r = pl.program_id(1)

        @pl.when(r == 0)
        def _():
            acc_ref[...] = jnp.zeros_like(acc_ref)

        acc_ref[...] += _dot_tn(u_ref[...], da_ref[...])

        @pl.when(r == last)
        def _():
            for d in range(per):
                o_ref[d] = acc_ref[:, ns * d:ns * (d + 1)].astype(o_ref.dtype)

    return pl.pallas_call(
        body, grid=(parts, m // tm),
        in_specs=[pl.BlockSpec((tm, D_MODEL), lambda h, r: (r, 0)), pl.BlockSpec((tm, n // parts), lambda h, r: (r, h))],
        out_specs=pl.BlockSpec((per, D_MODEL, ns), lambda h, r: (h, 0, 0)),
        out_shape=jax.ShapeDtypeStruct((N_DEV, D_MODEL, ns), GRAD_WIRE_DTYPE),
        scratch_shapes=[pltpu.VMEM((D_MODEL, n // parts), F32)], name=name,
        compiler_params=_params("arbitrary", "arbitrary"),
    )(u, da)


def mlp_dw_down(act, dh, name="mlp_dw_down"):
    m = act.shape[0]
    tm = TN_ROW_TILE
    last = m // tm - 1
    parts = D_FF * D_MODEL * 4 // ACC_BYTES
    rows = D_FF // parts

    def body(a_ref, dh_ref, o_ref, acc_ref):
        r = pl.program_id(1)

        @pl.when(r == 0)
        def _():
            acc_ref[...] = jnp.zeros_like(acc_ref)

        acc_ref[...] += _dot_tn(a_ref[...], dh_ref[...])

        @pl.when(r == last)
        def _():
            o_ref[...] = acc_ref[...].astype(o_ref.dtype)

    return pl.pallas_call(
        body, grid=(parts, m // tm),
        in_specs=[pl.BlockSpec((tm, rows), lambda h, r: (r, h)), pl.BlockSpec((tm, D_MODEL), lambda h, r: (r, 0))],
        out_specs=pl.BlockSpec((rows, D_MODEL), lambda h, r: (h, 0)),
        out_shape=jax.ShapeDtypeStruct((D_FF, D_MODEL), GRAD_WIRE_DTYPE),
        scratch_shapes=[pltpu.VMEM((rows, D_MODEL), F32)], name=name,
        compiler_params=_params("arbitrary", "arbitrary"),
    )(act, dh)


_ATT_SCALE = HEAD_DIM ** -0.5


def _alibi_slope(h):
    return 2.0 ** (-8.0 * (h + 1) / ATT_HEADS)


HEADS_PER_KV = ATT_HEADS // ATT_KV_HEADS
STACK = HEADS_PER_KV * BLOCK
N_KEYS = 3 * BLOCK


def _head_select(width):
    c = jnp.arange(width)[:, None]
    h = jnp.arange(LANE)[None, :]
    return (c // HEAD_DIM == h).astype(F32)


def _head_fold(width):
    c = jnp.arange(width)[:, None]
    j = jnp.arange(LANE)[None, :]
    return (c % HEAD_DIM == j).astype(F32)


def _head_rms(x, sel, sel_t):
    r = lax.rsqrt(_dot_hi(x * x, sel) * (1.0 / HEAD_DIM) + EPS)
    return r, _dot_hi(r, sel_t)


def _head_norm_bwd(x, r, r_full, w_t, dy, sel, sel_t):
    gy = dy * w_t
    coef = _dot_hi((r * r * r) * _dot_hi(x * gy, sel) * (1.0 / HEAD_DIM), sel_t)
    return r_full * gy - x * coef, jnp.sum(dy * (x * r_full), axis=0, keepdims=True)


def _low_lanes(rows):
    return lax.broadcasted_iota(jnp.int32, (rows, LANE), 1) < HEAD_DIM


def _dup_half(a, g):
    rolled = pltpu.roll(a, HEAD_DIM, 1)
    low = _low_lanes(a.shape[0])
    return jnp.where(low, a, rolled) if g == 0 else jnp.where(low, rolled, a)


def _stack_heads(x, g):
    low = _low_lanes(BLOCK)
    parts = []
    for pair in range(2):
        p = x[:, 2 * LANE * g + LANE * pair:2 * LANE * g + LANE * (pair + 1)]
        parts += [jnp.where(low, p, 0.0), jnp.where(low, 0.0, p)]
    return jnp.concatenate(parts, axis=0)


def _unstack_heads(groups):
    low = _low_lanes(BLOCK)
    cols = []
    for o in groups:
        for pair in range(2):
            cols.append(jnp.where(low, o[2 * pair * BLOCK:(2 * pair + 1) * BLOCK], o[(2 * pair + 1) * BLOCK:(2 * pair + 2) * BLOCK]))
    return jnp.concatenate(cols, axis=1)


def _fold_halves(a, g):
    s = a + pltpu.roll(a, HEAD_DIM, 1)
    low = _low_lanes(a.shape[0])
    return jnp.where(low if g == 0 else jnp.logical_not(low), s, 0.0)


def _att_bias(b):
    r = lax.broadcasted_iota(jnp.int32, (STACK, N_KEYS), 0) & (BLOCK - 1)
    col = lax.broadcasted_iota(jnp.int32, (STACK, N_KEYS), 1)
    cc = col & (BLOCK - 1)
    is_meta = col < BLOCK
    is_prev = jnp.logical_and(col >= BLOCK, col < 2 * BLOCK)
    q_pos = b * BLOCK + r - FRONT_PAD
    meta_j = cc - FRONT_PAD
    valid_m = jnp.logical_and(cc >= FRONT_PAD, q_pos >= meta_j)
    valid_p = jnp.logical_and(cc > r, b >= 2)
    valid_c = jnp.logical_and(cc <= r, b >= 1)
    is_cur = col >= 2 * BLOCK
    valid = jnp.logical_or(jnp.logical_and(is_meta, valid_m),
                           jnp.logical_or(jnp.logical_and(is_prev, valid_p), jnp.logical_and(is_cur, valid_c)))
    dist = jnp.where(is_meta, jnp.minimum(q_pos - meta_j, BLOCK), jnp.where(is_prev, r - cc + BLOCK, r - cc))
    return valid, dist.astype(F32)


def _per_head_column(values):
    hid = lax.broadcasted_iota(jnp.int32, (STACK, 1), 0) >> 7
    col = jnp.where(hid == 0, values[0], values[1])
    for j in range(2, HEADS_PER_KV):
        col = jnp.where(hid == j, values[j], col)
    return col


def _att_group_probs(qs, kd, valid, dist, g, sk_ref):
    slope = _per_head_column([_alibi_slope(HEADS_PER_KV * g + j) for j in range(HEADS_PER_KV)])
    sink = _per_head_column([sk_ref[HEADS_PER_KV * g + j] for j in range(HEADS_PER_KV)])
    s = jnp.where(valid, _dot_nt(qs, kd) * _ATT_SCALE - slope * dist, -1e30)
    mx = jnp.maximum(jnp.max(s, axis=-1, keepdims=True), sink)
    p = jnp.exp(s - mx)
    p_sink = jnp.exp(sink - mx)
    inv = 1.0 / (jnp.sum(p, axis=-1, keepdims=True) + p_sink)
    return p * inv, p_sink * inv


def attention_fwd(proj, q_w, k_w, sinks, gather=(), name="att_fwd"):
    m = proj.shape[0]
    nb = m // BLOCK
    cq, ck, cv = COL_Q // Q_W, COL_K // KV_W, COL_V // KV_W
    sel_q, sel_k = _head_select(Q_W), _head_select(KV_W)

    def body(q_ref, kc_ref, vc_ref, vp_ref, vm_ref, qw_ref, kw_ref, sk_ref, sq_ref, sqt_ref, skk_ref, skt_ref,
             o_ref, kpn_s, kmn_s):
        b = pl.program_id(0)
        q, kc = q_ref[...], kc_ref[...]
        _, rq = _head_rms(q, sq_ref[...], sqt_ref[...])
        qn = q * rq * qw_ref[...]
        _, rk = _head_rms(kc, skk_ref[...], skt_ref[...])
        kcn = kc * rk * kw_ref[...]

        @pl.when(b == 0)
        def _():
            kmn_s[...] = kcn
            kpn_s[...] = kcn

        kpn, kmn = kpn_s[...], kmn_s[...]
        kpn_s[...] = kcn
        vc, vp, vm = vc_ref[...], vp_ref[...], vm_ref[...]
        valid, dist = _att_bias(b)
        outs = []
        for g in range(ATT_KV_HEADS):
            kd = jnp.concatenate([_dup_half(kmn, g), _dup_half(kpn, g), _dup_half(kcn, g)], axis=0)
            vd = jnp.concatenate([_dup_half(vm, g), _dup_half(vp, g), _dup_half(vc, g)], axis=0)
            probs, _ = _att_group_probs(_stack_heads(qn, g), kd, valid, dist, g, sk_ref)
            outs.append(_dot(probs, vd))
        o_ref[...] = _unstack_heads(outs).astype(o_ref.dtype)

    prev = lambda i: jnp.maximum(i - 1, 0)
    full = lambda s: pl.BlockSpec(s, lambda i: (0,) * len(s))
    body, g_in, g_out, g_shape, g_scratch = _with_gather(body, 12, 1, gather, nb)
    outs = pl.pallas_call(
        body, grid=(nb,),
        in_specs=[pl.BlockSpec((BLOCK, Q_W), lambda i: (i, cq)),
                  pl.BlockSpec((BLOCK, KV_W), lambda i: (i, ck)), pl.BlockSpec((BLOCK, KV_W), lambda i: (i, cv)),
                  pl.BlockSpec((BLOCK, KV_W), lambda i: (prev(i), cv)), pl.BlockSpec((BLOCK, KV_W), lambda i: (0, cv)),
                  full((1, Q_W)), full((1, KV_W)), pl.BlockSpec(memory_space=pltpu.SMEM),
                  full((Q_W, LANE)), full((LANE, Q_W)), full((KV_W, LANE)), full((LANE, KV_W))] + g_in,
        out_specs=[pl.BlockSpec((BLOCK, Q_W), lambda i: (i, 0))] + g_out,
        out_shape=[jax.ShapeDtypeStruct((m, Q_W), MXU_DTYPE)] + g_shape,
        scratch_shapes=[pltpu.VMEM((BLOCK, KV_W), F32), pltpu.VMEM((BLOCK, KV_W), F32)] + g_scratch,
        name=name, compiler_params=_params("arbitrary"),
    )(proj, proj, proj, proj, proj, q_w, k_w, sinks, sel_q, sel_q.T, sel_k, sel_k.T, *gather)
    return outs[0], outs[1:]


def attention_bwd(proj, dmix, q_w, k_w, sinks, carry=(), name="att_bwd"):
    m = proj.shape[0]
    nb = m // BLOCK
    cq, ck, cv = COL_Q // Q_W, COL_K // KV_W, COL_V // KV_W
    c_datt = 0
    sel_q, sel_k = _head_select(Q_W), _head_select(KV_W)
    fold_q, fold_k = _head_fold(Q_W), _head_fold(KV_W)

    def body(do_ref, q_ref, kc_ref, vc_ref, kp_ref, vp_ref, km_ref, vm_ref, qw_ref, kw_ref, sk_ref,
             sq_ref, sqt_ref, skk_ref, skt_ref, fq_ref, fk_ref,
             dq_ref, dk_ref, dv_ref, dqw_ref, dkw_ref, dsk_ref, car_k, car_v, met_k, met_v, kmn_s, qw_acc, kw_acc):
        i = pl.program_id(0)
        b = nb - 1 - i
        sel_q_, sel_qt, sel_k_, sel_kt = sq_ref[...], sqt_ref[...], skk_ref[...], skt_ref[...]
        qw, kw = qw_ref[...], kw_ref[...]

        @pl.when(i == 0)
        def _():
            for r in (car_k, car_v, met_k, met_v, qw_acc, kw_acc, dsk_ref):
                r[...] = jnp.zeros_like(r)
            km = km_ref[...]
            kmn_s[...] = km * _head_rms(km, sel_k_, sel_kt)[1] * kw

        q, kc, kp = q_ref[...], kc_ref[...], kp_ref[...]
        rq, rq_full = _head_rms(q, sel_q_, sel_qt)
        qn = q * rq_full * qw
        rk, rk_full = _head_rms(kc, sel_k_, sel_kt)
        kcn = kc * rk_full * kw
        kpn = kp * _head_rms(kp, sel_k_, sel_kt)[1] * kw
        kmn = kmn_s[...]
        vc, vp, vm = vc_ref[...], vp_ref[...], vm_ref[...]
        do = do_ref[...]
        valid, dist = _att_bias(b)
        lane = lax.broadcasted_iota(jnp.int32, (1, LANE), 1)
        dsk = jnp.zeros((1, LANE), F32)
        dkd_sum = jnp.zeros((N_KEYS, KV_W), F32)
        dvd_sum = jnp.zeros((N_KEYS, KV_W), F32)
        dqd = []
        for g in range(ATT_KV_HEADS):
            kd = jnp.concatenate([_dup_half(kmn, g), _dup_half(kpn, g), _dup_half(kcn, g)], axis=0)
            vd = jnp.concatenate([_dup_half(vm, g), _dup_half(vp, g), _dup_half(vc, g)], axis=0)
            qs = _stack_heads(qn, g)
            dos = _stack_heads(do, g)
            probs, p_sink = _att_group_probs(qs, kd, valid, dist, g, sk_ref)
            o = _dot(probs, vd)
            delta = jnp.sum(dos * o, axis=-1, keepdims=True)
            ds = probs * (_dot_nt(dos, vd) - delta)
            dqd.append(_dot(ds, kd) * _ATT_SCALE)
            dkd_sum = dkd_sum + _fold_halves(_dot_tn(ds, qs) * _ATT_SCALE, g)
            dvd_sum = dvd_sum + _fold_halves(_dot_tn(probs, dos), g)
            sink_grad = p_sink * delta
            for j in range(HEADS_PER_KV):
                part = jnp.sum(sink_grad[BLOCK * j:BLOCK * (j + 1)], axis=0, keepdims=True)
                dsk = dsk - jnp.where(lane == HEADS_PER_KV * g + j, part, 0.0)
        dq, dqw = _head_norm_bwd(q, rq, rq_full, qw, _unstack_heads(dqd), sel_q_, sel_qt)
        dq_ref[...] = dq
        qw_acc[...] += dqw
        dsk_ref[...] += dsk

        met_k[...] += dkd_sum[0:BLOCK]
        met_v[...] += dvd_sum[0:BLOCK]
        first = (b == 0).astype(F32)
        dkn_tot = dkd_sum[2 * BLOCK:3 * BLOCK] + car_k[...] + first * met_k[...]
        dv_ref[...] = dvd_sum[2 * BLOCK:3 * BLOCK] + car_v[...] + first * met_v[...]
        car_k[...] = dkd_sum[BLOCK:2 * BLOCK]
        car_v[...] = dvd_sum[BLOCK:2 * BLOCK]
        dk, dkw = _head_norm_bwd(kc, rk, rk_full, kw, dkn_tot, sel_k_, sel_kt)
        dk_ref[...] = dk
        kw_acc[...] += dkw

        @pl.when(i == nb - 1)
        def _():
            dqw_ref[...] = _dot_hi(jnp.broadcast_to(qw_acc[...], (8, Q_W)), fq_ref[...])[0:1]
            dkw_ref[...] = _dot_hi(jnp.broadcast_to(kw_acc[...], (8, KV_W)), fk_ref[...])[0:1]

    blk = lambda i: nb - 1 - i
    prev = lambda i: jnp.maximum(nb - 2 - i, 0)
    full = lambda s: pl.BlockSpec(s, lambda i: (0,) * len(s))
    kv_scratch = pltpu.VMEM((BLOCK, KV_W), F32)
    body, ex_in, ex_out, ex_shape, ex_scratch = _with_exchange(body, 17, 6, carry, nb)
    outs = pl.pallas_call(
        body, grid=(nb,),
        in_specs=[pl.BlockSpec((BLOCK, Q_W), lambda i: (blk(i), c_datt)),
                  pl.BlockSpec((BLOCK, Q_W), lambda i: (blk(i), cq)),
                  pl.BlockSpec((BLOCK, KV_W), lambda i: (blk(i), ck)), pl.BlockSpec((BLOCK, KV_W), lambda i: (blk(i), cv)),
                  pl.BlockSpec((BLOCK, KV_W), lambda i: (prev(i), ck)), pl.BlockSpec((BLOCK, KV_W), lambda i: (prev(i), cv)),
                  pl.BlockSpec((BLOCK, KV_W), lambda i: (0, ck)), pl.BlockSpec((BLOCK, KV_W), lambda i: (0, cv)),
                  full((1, Q_W)), full((1, KV_W)), pl.BlockSpec(memory_space=pltpu.SMEM),
                  full((Q_W, LANE)), full((LANE, Q_W)), full((KV_W, LANE)), full((LANE, KV_W)),
                  full((Q_W, LANE)), full((KV_W, LANE))] + ex_in,
        out_specs=[pl.BlockSpec((BLOCK, Q_W), lambda i: (blk(i), 0)),
                   pl.BlockSpec((BLOCK, KV_W), lambda i: (blk(i), 0)), pl.BlockSpec((BLOCK, KV_W), lambda i: (blk(i), 0)),
                   full((1, LANE)), full((1, LANE)), full((1, LANE))] + ex_out,
        out_shape=[jax.ShapeDtypeStruct((m, Q_W), F32), jax.ShapeDtypeStruct((m, KV_W), F32),
                   jax.ShapeDtypeStruct((m, KV_W), F32), jax.ShapeDtypeStruct((1, LANE), F32),
                   jax.ShapeDtypeStruct((1, LANE), F32), jax.ShapeDtypeStruct((1, LANE), F32)] + ex_shape,
        scratch_shapes=[kv_scratch, kv_scratch, kv_scratch, kv_scratch, kv_scratch,
                        pltpu.VMEM((1, Q_W), F32), pltpu.VMEM((1, KV_W), F32)] + ex_scratch,
        name=name, compiler_params=_params("arbitrary"),
    )(dmix, proj, proj, proj, proj, proj, proj, proj, q_w, k_w, sinks, sel_q, sel_q.T, sel_k, sel_k.T, fold_q, fold_k,
      *carry)
    return outs[:6], outs[6:]


HALO = 8
GROUP_W = SSM_INNER // SSM_GROUPS
HEADS_PER_GROUP = SSM_HEADS // SSM_GROUPS


def _head_expand():
    h = jnp.arange(LANE)[:, None]
    c = jnp.arange(SSM_INNER)[None, :]
    return (c // SSM_HEAD_DIM == h).astype(F32)


def _softplus(x):
    return jnp.maximum(x, 0.0) + jnp.log1p(jnp.exp(-jnp.abs(x)))


def _ssd_decays(dt, a_log_row):
    row = lax.broadcasted_iota(jnp.int32, (BLOCK, BLOCK), 0)
    col = lax.broadcasted_iota(jnp.int32, (BLOCK, BLOCK), 1)
    lower = row >= col
    a = -jnp.exp(a_log_row)
    a_cs = _dot_hi(lower, dt * a, exact="b")
    return a, a_cs, lower


def _decay_matrix(a_cs, a_cs_t, h, lower):
    diff = a_cs[:, h:h + 1] - a_cs_t[h:h + 1, :]
    return jnp.where(lower, jnp.exp(jnp.where(lower, diff, 0.0)), 0.0)


def _conv_taps(s_ref, w_ref, first, rows):
    acc = w_ref[0:1, :] * s_ref[pl.ds(first, rows), :]
    for j in range(1, SSM_CONV):
        acc = acc + w_ref[j:j + 1, :] * s_ref[pl.ds(first + j, rows), :]
    return acc


def ssd_fwd(proj, att, cw_x, cw_bc, cb_x, cb_bc, dt_bias, a_log, d_exp, norm_w, gather=(), name="ssd_fwd"):
    m = proj.shape[0]
    nb = m // BLOCK
    expand = _head_expand()
    expand_t = expand.T

    def body(z_ref, xs_ref, bc_ref, dtr_ref, att_ref, cwx_ref, cwbc_ref, cbx_ref, cbbc_ref, dtb_ref, alog_ref, dexp_ref,
             nw_ref, e_ref, et_ref, out_ref, prex_ref, prebc_ref, dt_ref, ypre_ref, st_ref, sx, sbc, state):
        c = pl.program_id(0)

        @pl.when(c == 0)
        def _():
            sx[0:HALO, :] = jnp.zeros((HALO, SSM_INNER), F32)
            sbc[0:HALO, :] = jnp.zeros((HALO, 2 * LANE), F32)
            state[...] = jnp.zeros_like(state)

        sx[HALO:HALO + BLOCK, :] = xs_ref[...]
        sbc[HALO:HALO + BLOCK, :] = bc_ref[...]
        first = HALO - (SSM_CONV - 1)
        pre_x = _conv_taps(sx, cwx_ref, first, BLOCK) + cbx_ref[...]
        pre_bc = _conv_taps(sbc, cwbc_ref, first, BLOCK) + cbbc_ref[...]
        sx[0:HALO, :] = xs_ref[BLOCK - HALO:BLOCK, :]
        sbc[0:HALO, :] = bc_ref[BLOCK - HALO:BLOCK, :]
        prex_ref[...] = pre_x
        prebc_ref[...] = pre_bc
        xc = pre_x * _sigmoid(pre_x)
        bcv = pre_bc * _sigmoid(pre_bc)

        rows = _row_ids(c * BLOCK, BLOCK, LANE)
        lanes = lax.broadcasted_iota(jnp.int32, (BLOCK, LANE), 1)
        live = jnp.logical_and(rows >= FRONT_PAD, lanes < SSM_HEADS)
        dt = jnp.where(live, _softplus(dtr_ref[...] + dtb_ref[...]), 0.0)
        dt_ref[...] = dt
        a, a_cs, lower = _ssd_decays(dt, alog_ref[...])
        a_cs_t = a_cs.T
        dt_t = dt.T
        e = e_ref[...]
        es_full = _dot_hi(jnp.exp(a_cs), e)
        wx_full = _dot_hi(jnp.exp(a_cs[BLOCK - 1:BLOCK, :] - a_cs) * dt, e)
        end_col = jnp.exp(a_cs_t[:, BLOCK - 1:BLOCK])
        dec_full = _dot_hi(et_ref[...], jnp.broadcast_to(end_col, (LANE, SSM_STATE)), exact="b")

        st_ref[0] = state[...]
        ys = []
        for g in range(SSM_GROUPS):
            b_g = bcv[:, SSM_STATE * g:SSM_STATE * (g + 1)]
            c_g = bcv[:, LANE + SSM_STATE * g:LANE + SSM_STATE * (g + 1)]
            gs = slice(GROUP_W * g, GROUP_W * (g + 1))
            cb = _dot_nt(c_g, b_g)
            yd = []
            for hh in range(HEADS_PER_GROUP):
                h = g * HEADS_PER_GROUP + hh
                w = cb * _decay_matrix(a_cs, a_cs_t, h, lower) * dt_t[h:h + 1, :]
                yd.append(_dot(w, xc[:, SSM_HEAD_DIM * h:SSM_HEAD_DIM * (h + 1)]))
            h_g = state[gs, :]
            y_off = _dot_nt(c_g, h_g) * es_full[:, gs]
            ys.append(jnp.concatenate(yd, axis=1) + y_off)
            new_state = _dot_tn(xc[:, gs] * wx_full[:, gs], b_g)
            state[gs, :] = h_g * dec_full[gs, :] + new_state
        y_pre = jnp.concatenate(ys, axis=1) + xc * dexp_ref[...]
        ypre_ref[...] = y_pre
        z = z_ref[...]
        gt = y_pre * (z * _sigmoid(z))
        outs = []
        for g in range(SSM_GROUPS):
            gg = gt[:, GROUP_W * g:GROUP_W * (g + 1)]
            r = lax.rsqrt(jnp.mean(gg * gg, -1, keepdims=True) + EPS)
            outs.append(gg * r)
        out_ref[:, 0:Q_W] = att_ref[...]
        out_ref[:, Q_W:MIX_W] = (jnp.concatenate(outs, axis=1) * nw_ref[...]).astype(out_ref.dtype)

    full = lambda s: pl.BlockSpec(s, lambda i: (0,) * len(s))
    rowblk = lambda w, cidx: pl.BlockSpec((BLOCK, w), lambda i: (i, cidx))
    body, g_in, g_out, g_shape, g_scratch = _with_gather(body, 15, 6, gather, nb)
    outs = pl.pallas_call(
        body, grid=(nb,),
        in_specs=[rowblk(SSM_INNER, COL_Z // SSM_INNER), rowblk(SSM_INNER, COL_XS // SSM_INNER),
                  rowblk(2 * LANE, COL_BC // (2 * LANE)), rowblk(LANE, COL_DT // LANE), rowblk(Q_W, 0),
                  full((SSM_CONV, SSM_INNER)), full((SSM_CONV, 2 * LANE)), full((1, SSM_INNER)), full((1, 2 * LANE)),
                  full((1, LANE)), full((1, LANE)), full((1, SSM_INNER)), full((1, SSM_INNER)),
                  full((LANE, SSM_INNER)), full((SSM_INNER, LANE))] + g_in,
        out_specs=[rowblk(MIX_W, 0), rowblk(SSM_INNER, 0), rowblk(2 * LANE, 0), rowblk(LANE, 0),
                   rowblk(SSM_INNER, 0), pl.BlockSpec((1, SSM_INNER, SSM_STATE), lambda i: (i, 0, 0))] + g_out,
        out_shape=[jax.ShapeDtypeStruct((m, MIX_W), MXU_DTYPE), jax.ShapeDtypeStruct((m, SSM_INNER), F32),
                   jax.ShapeDtypeStruct((m, 2 * LANE), F32), jax.ShapeDtypeStruct((m, LANE), F32),
                   jax.ShapeDtypeStruct((m, SSM_INNER), F32),
                   jax.ShapeDtypeStruct((nb, SSM_INNER, SSM_STATE), F32)] + g_shape,
        scratch_shapes=[pltpu.VMEM((HALO + BLOCK, SSM_INNER), F32), pltpu.VMEM((HALO + BLOCK, 2 * LANE), F32),
                        pltpu.VMEM((SSM_INNER, SSM_STATE), F32)] + g_scratch,
        name=name, compiler_params=_params("arbitrary"),
    )(proj, proj, proj, proj, att, cw_x, cw_bc, cb_x, cb_bc, dt_bias, a_log, d_exp, norm_w, expand, expand_t, *gather)
    return outs[:6], outs[6:]


def ssd_bwd(proj, dmix, dq, dk, dv, pre_x, pre_bc, dt, y_pre, states, cw_x, cw_bc, dt_bias, a_log, d_exp, norm_w,
            carry=(), name="ssd_bwd"):
    m = proj.shape[0]
    nb = m // BLOCK
    expand = _head_expand()
    expand_t = expand.T

    def body(do0_ref, do1_ref, z_ref, xs_ref, xsp_ref, bc_ref, bcp_ref, dtr_ref, prex_ref, prebc_ref, dt_ref, ypre_ref,
             st_ref, dq_ref, dk_ref, dv_ref,
             cwx_ref, cwbc_ref, dtb_ref, alog_ref, dexp_ref, nw_ref, e_ref, et_ref,
             dproj_ref, dcwx_ref, dcwbc_ref, dcbx_ref, dcbbc_ref, ddtb_ref, dalog_ref,
             dd_ref, dnw_ref,
             dstate, hnext, tx, tbc, sx, sbc, dlane):

        def put(col, value):
            dproj_ref[:, col:col + value.shape[1]] = value.astype(dproj_ref.dtype)

        put(COL_Q, dq_ref[...])
        put(COL_K, dk_ref[...])
        put(COL_V, dv_ref[...])
        i = pl.program_id(0)
        c = nb - 1 - i

        @pl.when(i == 0)
        def _():
            dstate[...] = jnp.zeros_like(dstate)
            hnext[...] = jnp.zeros_like(hnext)
            tx[BLOCK:BLOCK + HALO, :] = jnp.zeros((HALO, SSM_INNER), F32)
            tbc[BLOCK:BLOCK + HALO, :] = jnp.zeros((HALO, 2 * LANE), F32)
            dlane[...] = jnp.zeros_like(dlane)
            for r in (dcwx_ref, dcwbc_ref, dcbx_ref, dcbbc_ref, ddtb_ref, dalog_ref, dd_ref, dnw_ref):
                r[...] = jnp.zeros_like(r)

        e = e_ref[...]
        et = et_ref[...]
        pre_x = prex_ref[...]
        pre_bc = prebc_ref[...]
        sig_x = _sigmoid(pre_x)
        sig_bc = _sigmoid(pre_bc)
        xc = pre_x * sig_x
        bcv = pre_bc * sig_bc
        dt = dt_ref[...]
        a, a_cs, lower = _ssd_decays(dt, alog_ref[...])
        a_cs_t = a_cs.T
        es_full = _dot_hi(jnp.exp(a_cs), e)
        ed_full = _dot_hi(jnp.exp(a_cs[BLOCK - 1:BLOCK, :] - a_cs), e)
        dt_full = _dot_hi(dt, e)
        end_col = jnp.exp(a_cs_t[:, BLOCK - 1:BLOCK])
        dec_full = _dot_hi(et, jnp.broadcast_to(end_col, (LANE, SSM_STATE)), exact="b")
        dexp = dexp_ref[...]

        z = z_ref[...]
        zs = _sigmoid(z)
        sz = z * zs
        y_pre = ypre_ref[...]
        gt = y_pre * sz
        do = jnp.concatenate([do0_ref[...], do1_ref[...]], axis=1)
        nw = nw_ref[...]
        dgt = []
        dnw = []
        for g in range(SSM_GROUPS):
            gs = slice(GROUP_W * g, GROUP_W * (g + 1))
            gg = gt[:, gs]
            r = lax.rsqrt(jnp.mean(gg * gg, -1, keepdims=True) + EPS)
            gn = do[:, gs] * nw[:, gs]
            dgt.append(r * gn - gg * ((r * r * r) * jnp.mean(gg * gn, -1, keepdims=True)))
            dnw.append(jnp.sum(do[:, gs] * (gg * r), axis=0, keepdims=True))
        dgt = jnp.concatenate(dgt, axis=1)
        dnw_ref[...] += jnp.concatenate(dnw, axis=1)
        dy = dgt * sz
        put(COL_Z, dgt * y_pre * (zs * (1.0 + z * (1.0 - zs))))
        dlane[...] += jnp.sum(dy * xc, axis=0, keepdims=True)
        xd = xc * dt_full

        lane_id = lax.broadcasted_iota(jnp.int32, (BLOCK, LANE), 1)
        sub_id = lax.broadcasted_iota(jnp.int32, (LANE, BLOCK), 0)
        ds_to = jnp.zeros((BLOCK, LANE), F32)
        ds_from_t = jnp.zeros((LANE, BLOCK), F32)
        dxd_parts, inter_parts = [], []
        dbs, dcs = [], []
        for g in range(SSM_GROUPS):
            gs = slice(GROUP_W * g, GROUP_W * (g + 1))
            b_g = bcv[:, SSM_STATE * g:SSM_STATE * (g + 1)]
            c_g = bcv[:, LANE + SSM_STATE * g:LANE + SSM_STATE * (g + 1)]
            cb = _dot_nt(c_g, b_g)
            dcb = jnp.zeros((BLOCK, BLOCK), F32)
            dxd_h = []
            for hh in range(HEADS_PER_GROUP):
                h = g * HEADS_PER_GROUP + hh
                hs = slice(SSM_HEAD_DIM * h, SSM_HEAD_DIM * (h + 1))
                lm = _decay_matrix(a_cs, a_cs_t, h, lower)
                dy_h = dy[:, hs]
                gl = _dot_nt(dy_h, xd[:, hs]) * lm
                dcb = dcb + gl
                e_h = gl * cb
                ds_to = ds_to + jnp.where(lane_id == h, jnp.sum(e_h, axis=-1, keepdims=True), 0.0)
                ds_from_t = ds_from_t + jnp.where(sub_id == h, jnp.sum(e_h, axis=0, keepdims=True), 0.0)
                dxd_h.append(_dot_tn(cb * lm, dy_h))
            h_g = st_ref[0, gs, :]
            dh_g = dstate[gs, :]
            dys_g = dy[:, gs] * es_full[:, gs]
            xde_g = xd[:, gs] * ed_full[:, gs]
            dcs.append(_dot(dcb, b_g) + _dot(dys_g, h_g))
            dbs.append(_dot_tn(dcb, c_g) + _dot(xde_g, dh_g))
            y_off = _dot_nt(c_g, h_g) * es_full[:, gs]
            dxd_state = _dot_nt(b_g, dh_g) * ed_full[:, gs]
            inter_parts.append(dy[:, gs] * y_off - xd[:, gs] * dxd_state)
            dxd_parts.append(jnp.concatenate(dxd_h, axis=1) + dxd_state)
            dstate[gs, :] = dh_g * dec_full[gs, :] + _dot_tn(dys_g, c_g)
            if g == 0:
                end_dot = hnext[gs, :] * dh_g
            else:
                end_dot = jnp.concatenate([end_dot, hnext[gs, :] * dh_g], axis=0)
        dxd = jnp.concatenate(dxd_parts, axis=1)
        hnext[...] = st_ref[0]

        ds = ds_to - ds_from_t.T + _dot_hi(jnp.concatenate(inter_parts, axis=1), et)
        ds_end = jnp.sum(_dot_tn_hi(end_dot, et), axis=0, keepdims=True)
        rows_l = lax.broadcasted_iota(jnp.int32, (BLOCK, LANE), 0)
        ds = ds + jnp.where(rows_l == BLOCK - 1, ds_end, 0.0)
        row = lax.broadcasted_iota(jnp.int32, (BLOCK, BLOCK), 0)
        col = lax.broadcasted_iota(jnp.int32, (BLOCK, BLOCK), 1)
        dadt = _dot_hi(col >= row, ds, exact="b")
        ddt = dadt * a + _dot_hi(dxd * xc, et)
        dalog_ref[...] += jnp.sum(dadt * dt, axis=0, keepdims=True) * a
        rows = _row_ids(c * BLOCK, BLOCK, LANE)
        lanes = lax.broadcasted_iota(jnp.int32, (BLOCK, LANE), 1)
        live = jnp.logical_and(rows >= FRONT_PAD, lanes < SSM_HEADS)
        ddt_raw = jnp.where(live, ddt * _sigmoid(dtr_ref[...] + dtb_ref[...]), 0.0)
        put(COL_DT, ddt_raw)
        ddtb_ref[...] += jnp.sum(ddt_raw, axis=0, keepdims=True)

        dxc = dxd * dt_full + dy * dexp
        dpre_x = dxc * (sig_x * (1.0 + pre_x * (1.0 - sig_x)))
        dpre_bc = jnp.concatenate(dbs + dcs, axis=1) * (sig_bc * (1.0 + pre_bc * (1.0 - sig_bc)))
        dcbx_ref[...] += jnp.sum(dpre_x, axis=0, keepdims=True)
        dcbbc_ref[...] += jnp.sum(dpre_bc, axis=0, keepdims=True)
        keep_x = _row_ids(c * BLOCK, BLOCK, SSM_INNER) >= FRONT_PAD
        keep_bc = _row_ids(c * BLOCK, BLOCK, 2 * LANE) >= FRONT_PAD
        prev_live = (c > 0).astype(F32)
        for (dpre, t_ref, s_ref, cur_ref, prv_ref, w_ref, dw_ref, col, keep) in (
                (dpre_x, tx, sx, xs_ref, xsp_ref, cwx_ref, dcwx_ref, COL_XS, keep_x),
                (dpre_bc, tbc, sbc, bc_ref, bcp_ref, cwbc_ref, dcwbc_ref, COL_BC, keep_bc)):
            t_ref[0:BLOCK, :] = dpre
            acc = w_ref[0:1, :] * t_ref[pl.ds(SSM_CONV - 1, BLOCK), :]
            for j in range(1, SSM_CONV):
                acc = acc + w_ref[j:j + 1, :] * t_ref[pl.ds(SSM_CONV - 1 - j, BLOCK), :]
            put(col, jnp.where(keep, acc, 0.0))
            t_ref[BLOCK:BLOCK + HALO, :] = dpre[0:HALO, :]
            s_ref[0:HALO, :] = prv_ref[BLOCK - HALO:BLOCK, :] * prev_live
            s_ref[HALO:HALO + BLOCK, :] = cur_ref[...]
            first = HALO - (SSM_CONV - 1)
            for j in range(SSM_CONV):
                dw_ref[j:j + 1, :] += jnp.sum(dpre * s_ref[pl.ds(first + j, BLOCK), :], axis=0, keepdims=True)

        @pl.when(i == nb - 1)
        def _():
            dd_ref[...] = _dot_hi(jnp.broadcast_to(dlane[...], (HALO, SSM_INNER)), et)[0:1, :]

    blk = lambda i: nb - 1 - i
    prv = lambda i: jnp.maximum(nb - 2 - i, 0)
    full = lambda s: pl.BlockSpec(s, lambda i: (0,) * len(s))
    rowblk = lambda w, cidx: pl.BlockSpec((BLOCK, w), lambda i: (blk(i), cidx))
    prvblk = lambda w, cidx: pl.BlockSpec((BLOCK, w), lambda i: (prv(i), cidx))
    body, ex_in, ex_out, ex_shape, ex_scratch = _with_exchange(body, 24, 9, carry, nb)
    outs = pl.pallas_call(
        body, grid=(nb,),
        in_specs=[rowblk(GROUP_W, Q_W // GROUP_W), rowblk(GROUP_W, Q_W // GROUP_W + 1),
                  rowblk(SSM_INNER, COL_Z // SSM_INNER),
                  rowblk(SSM_INNER, COL_XS // SSM_INNER), prvblk(SSM_INNER, COL_XS // SSM_INNER),
                  rowblk(2 * LANE, COL_BC // (2 * LANE)), prvblk(2 * LANE, COL_BC // (2 * LANE)),
                  rowblk(LANE, COL_DT // LANE),
                  rowblk(SSM_INNER, 0), rowblk(2 * LANE, 0), rowblk(LANE, 0), rowblk(SSM_INNER, 0),
                  pl.BlockSpec((1, SSM_INNER, SSM_STATE), lambda i: (blk(i), 0, 0)),
                  rowblk(Q_W, 0), rowblk(KV_W, 0), rowblk(KV_W, 0),
                  full((SSM_CONV, SSM_INNER)), full((SSM_CONV, 2 * LANE)), full((1, LANE)), full((1, LANE)),
                  full((1, SSM_INNER)), full((1, SSM_INNER)), full((LANE, SSM_INNER)), full((SSM_INNER, LANE))] + ex_in,
        out_specs=[rowblk(PROJ_W, 0),
                   full((SSM_CONV, SSM_INNER)), full((SSM_CONV, 2 * LANE)), full((1, SSM_INNER)), full((1, 2 * LANE)),
                   full((1, LANE)), full((1, LANE)), full((1, LANE)), full((1, SSM_INNER))] + ex_out,
        out_shape=[jax.ShapeDtypeStruct((m, PROJ_W), MXU_DTYPE),
                   jax.ShapeDtypeStruct((SSM_CONV, SSM_INNER), F32), jax.ShapeDtypeStruct((SSM_CONV, 2 * LANE), F32),
                   jax.ShapeDtypeStruct((1, SSM_INNER), F32), jax.ShapeDtypeStruct((1, 2 * LANE), F32),
                   jax.ShapeDtypeStruct((1, LANE), F32), jax.ShapeDtypeStruct((1, LANE), F32),
                   jax.ShapeDtypeStruct((1, LANE), F32), jax.ShapeDtypeStruct((1, SSM_INNER), F32)] + ex_shape,
        scratch_shapes=[pltpu.VMEM((SSM_INNER, SSM_STATE), F32), pltpu.VMEM((SSM_INNER, SSM_STATE), F32),
                        pltpu.VMEM((BLOCK + HALO, SSM_INNER), F32), pltpu.VMEM((BLOCK + HALO, 2 * LANE), F32),
                        pltpu.VMEM((HALO + BLOCK, SSM_INNER), F32), pltpu.VMEM((HALO + BLOCK, 2 * LANE), F32),
                        pltpu.VMEM((1, SSM_INNER), F32)] + ex_scratch,
        name=name, compiler_params=_params("arbitrary"),
    )(dmix, dmix, proj, proj, proj, proj, proj, proj, pre_x, pre_bc, dt, y_pre, states, dq, dk, dv,
      cw_x, cw_bc, dt_bias, a_log, d_exp, norm_w, expand, expand_t, *carry)
    return outs[:9], outs[9:]


CONF_HALO = 32
SUBLANES = 8


def _for_each_window(s, offsets, rows, fn):
    total = s.shape[0]
    assert max(offsets) + rows <= total
    for b in range(SUBLANES):
        offs = [o for o in offsets if o % SUBLANES == b]
        if not offs:
            continue
        rot = s if b == 0 else pltpu.roll(s, total - b, 0)
        for o in offs:
            fn(o, rot[o - b:o - b + rows])


def _glu_masked(v, first_row):
    a = v[:, :D_MODEL]
    s = _sigmoid(v[:, D_MODEL:])
    rows = _row_ids(first_row, v.shape[0], D_MODEL)
    return jnp.where(rows >= FRONT_PAD, a * s, 0.0), a, s


def _layer_norm_stats(c):
    mu = jnp.mean(c, -1, keepdims=True)
    xc = c - mu
    rstd = lax.rsqrt(jnp.mean(xc * xc, -1, keepdims=True) + LN_EPS)
    return xc * rstd, rstd


def conformer_mid_fwd(v, dw_w, dw_b, ln_g, ln_b, gather=(), name="conf_mid_fwd"):
    m = v.shape[0]
    nb = m // BLOCK
    kpad = dw_w.shape[0]

    def body(vc_ref, vp_ref, w_ref, b_ref, g_ref, beta_ref, c_ref, s_ref):
        i = pl.program_id(0)
        g_prev, _, _ = _glu_masked(vp_ref[BLOCK - CONF_HALO:BLOCK, :], (i - 1) * BLOCK + BLOCK - CONF_HALO)
        g_cur, _, _ = _glu_masked(vc_ref[...], i * BLOCK)
        sg = jnp.concatenate([g_prev * (i > 0).astype(F32), g_cur], axis=0)
        first = CONF_HALO - (CONF_KERNEL - 1)
        acc = [jnp.broadcast_to(b_ref[...], (BLOCK, D_MODEL))]

        def tap(off, win):
            j = off - first
            acc[0] = acc[0] + w_ref[j:j + 1, :] * win

        _for_each_window(sg, [first + j for j in range(CONF_KERNEL)], BLOCK, tap)
        acc = acc[0]
        c_ref[...] = acc
        xhat, _ = _layer_norm_stats(acc)
        nrm = xhat * g_ref[...] + beta_ref[...]
        s_ref[...] = (nrm * _sigmoid(nrm)).astype(s_ref.dtype)

    full = lambda s: pl.BlockSpec(s, lambda i: (0,) * len(s))
    body, g_in, g_out, g_shape, g_scratch = _with_gather(body, 6, 2, gather, nb)
    outs = pl.pallas_call(
        body, grid=(nb,),
        in_specs=[pl.BlockSpec((BLOCK, 2 * D_MODEL), lambda i: (i, 0)),
                  pl.BlockSpec((BLOCK, 2 * D_MODEL), lambda i: (jnp.maximum(i - 1, 0), 0)),
                  full((kpad, D_MODEL)), full((1, D_MODEL)), full((1, D_MODEL)), full((1, D_MODEL))] + g_in,
        out_specs=[pl.BlockSpec((BLOCK, D_MODEL), lambda i: (i, 0)),
                   pl.BlockSpec((BLOCK, D_MODEL), lambda i: (i, 0))] + g_out,
        out_shape=[jax.ShapeDtypeStruct((m, D_MODEL), F32), jax.ShapeDtypeStruct((m, D_MODEL), MXU_DTYPE)] + g_shape,
        scratch_shapes=g_scratch, name=name, compiler_params=_params("arbitrary"),
    )(v, v, dw_w, dw_b, ln_g, ln_b, *gather)
    return outs[:2], outs[2:]


def conformer_ln_bwd(ds, c, ln_g, ln_b, name="conf_ln_bwd"):
    m, d = c.shape
    tm = ROW_TILE

    def body(ds_ref, c_ref, g_ref, beta_ref, dc_ref, dg_ref, db_ref):
        @pl.when(pl.program_id(0) == 0)
        def _():
            dg_ref[...] = jnp.zeros_like(dg_ref)
            db_ref[...] = jnp.zeros_like(db_ref)

        xhat, rstd = _layer_norm_stats(c_ref[...])
        g = g_ref[...]
        nrm = xhat * g + beta_ref[...]
        sg = _sigmoid(nrm)
        dn = ds_ref[...] * (sg * (1.0 + nrm * (1.0 - sg)))
        db_ref[...] += jnp.sum(dn, axis=0, keepdims=True)
        dg_ref[...] += jnp.sum(dn * xhat, axis=0, keepdims=True)
        dx = dn * g
        dc_ref[...] = rstd * (dx - jnp.mean(dx, -1, keepdims=True) - xhat * jnp.mean(dx * xhat, -1, keepdims=True))

    row = pl.BlockSpec((tm, d), lambda i: (i, 0))
    vec = pl.BlockSpec((1, d), lambda i: (0, 0))
    return pl.pallas_call(
        body, grid=(m // tm,), in_specs=[row, row, vec, vec], out_specs=[row, vec, vec],
        out_shape=[jax.ShapeDtypeStruct((m, d), F32), jax.ShapeDtypeStruct((1, d), F32), jax.ShapeDtypeStruct((1, d), F32)],
        name=name, compiler_params=_params("arbitrary"),
    )(ds, c, ln_g, ln_b)


def conformer_conv_bwd(dc, v, dw_w, carry=(), name="conf_conv_bwd"):
    m = v.shape[0]
    nb = m // BLOCK
    kpad = dw_w.shape[0]

    def body(dcc_ref, dcn_ref, vc_ref, vp_ref, w_ref, dv_ref, dw_ref, db_ref, dvb_ref):
        i = pl.program_id(0)

        @pl.when(i == 0)
        def _():
            dw_ref[...] = jnp.zeros_like(dw_ref)
            db_ref[...] = jnp.zeros_like(db_ref)
            dvb_ref[...] = jnp.zeros_like(dvb_ref)

        dc_cur = dcc_ref[...]
        tg = jnp.concatenate([dc_cur, dcn_ref[0:CONF_HALO, :] * (i < nb - 1).astype(F32)], axis=0)
        g_prev, _, _ = _glu_masked(vp_ref[BLOCK - CONF_HALO:BLOCK, :], (i - 1) * BLOCK + BLOCK - CONF_HALO)
        g_cur, a, s = _glu_masked(vc_ref[...], i * BLOCK)
        sg = jnp.concatenate([g_prev * (i > 0).astype(F32), g_cur], axis=0)
        db_ref[...] += jnp.sum(dc_cur, axis=0, keepdims=True)
        first = CONF_HALO - (CONF_KERNEL - 1)
        dg_acc = [jnp.zeros((BLOCK, D_MODEL), F32)]

        def tap_dg(off, win):
            j = CONF_KERNEL - 1 - off
            dg_acc[0] = dg_acc[0] + w_ref[j:j + 1, :] * win

        def tap_dw(off, win):
            j = off - first
            dw_ref[j:j + 1, :] += jnp.sum(dc_cur * win, axis=0, keepdims=True)

        _for_each_window(tg, list(range(CONF_KERNEL)), BLOCK, tap_dg)
        _for_each_window(sg, [first + j for j in range(CONF_KERNEL)], BLOCK, tap_dw)
        dg = dg_acc[0]
        rows = _row_ids(i * BLOCK, BLOCK, D_MODEL)
        dg = jnp.where(rows >= FRONT_PAD, dg, 0.0)
        da = dg * s
        dbv = dg * a * (s * (1.0 - s))
        dv = jnp.concatenate([da, dbv], axis=1)
        dv_ref[...] = dv.astype(dv_ref.dtype)
        dvb_ref[...] += jnp.sum(dv, axis=0, keepdims=True)

    full = lambda s: pl.BlockSpec(s, lambda i: (0,) * len(s))
    body, ex_in, ex_out, ex_shape, ex_scratch = _with_exchange(body, 5, 4, carry, nb)
    outs = pl.pallas_call(
        body, grid=(nb,),
        in_specs=[pl.BlockSpec((BLOCK, D_MODEL), lambda i: (i, 0)),
                  pl.BlockSpec((BLOCK, D_MODEL), lambda i: (jnp.minimum(i + 1, nb - 1), 0)),
                  pl.BlockSpec((BLOCK, 2 * D_MODEL), lambda i: (i, 0)),
                  pl.BlockSpec((BLOCK, 2 * D_MODEL), lambda i: (jnp.maximum(i - 1, 0), 0)),
                  full((kpad, D_MODEL))] + ex_in,
        out_specs=[pl.BlockSpec((BLOCK, 2 * D_MODEL), lambda i: (i, 0)), full((kpad, D_MODEL)),
                   full((1, D_MODEL)), full((1, 2 * D_MODEL))] + ex_out,
        out_shape=[jax.ShapeDtypeStruct((m, 2 * D_MODEL), MXU_DTYPE), jax.ShapeDtypeStruct((kpad, D_MODEL), F32),
                   jax.ShapeDtypeStruct((1, D_MODEL), F32), jax.ShapeDtypeStruct((1, 2 * D_MODEL), F32)] + ex_shape,
        scratch_shapes=ex_scratch, name=name, compiler_params=_params("arbitrary"),
    )(dc, dc, v, v, dw_w, *carry)
    return outs[:4], outs[4:]


def _row(v, width=None):
    v = v.reshape(1, -1).astype(F32)
    if width is not None and v.shape[1] < width:
        v = jnp.pad(v, ((0, 0), (0, width - v.shape[1])))
    return v


def _w_in_to_kernel(w):
    pad = jnp.zeros((w.shape[0], PROJ_W - COL_DT - SSM_HEADS), w.dtype)
    return jnp.concatenate([w[:, 768:1792], w[:, 1792:2816], w[:, 0:512], w[:, 2816:3072], w[:, 512:640],
                            w[:, 640:768], w[:, 3072:3088], pad], axis=1)


def _w_in_from_kernel(g):
    return jnp.concatenate([g[:, COL_Q:COL_Q + Q_W], g[:, COL_K:COL_K + KV_W], g[:, COL_V:COL_V + KV_W],
                            g[:, COL_Z:COL_Z + SSM_INNER], g[:, COL_XS:COL_XS + SSM_INNER],
                            g[:, COL_BC:COL_BC + 2 * LANE], g[:, COL_DT:COL_DT + SSM_HEADS]], axis=1)


def even_fwd(h, p, gather_att=(), gather_ssd=()):
    u = rms_fwd(h, p["norm"])
    proj = matmul(u, p["w_in"], name="mm_proj")
    att, got_att = attention_fwd(proj, p["q_norm"], p["k_norm"], p["sinks"], gather=list(gather_att))
    if p["w_out"] is None:
        p["w_out"] = got_att[0]
    (mix, pre_x, pre_bc, dt, y_pre, states), got_ssd = ssd_fwd(
        proj, att, p["cw_x"], p["cw_bc"], p["cb_x"], p["cb_bc"], p["dt_bias"], p["a_log"], p["d_exp"], p["ssm_norm"],
        gather=list(gather_ssd))
    out = matmul(mix, p["w_out"], b_kind="rowshard", layer=p["layer"], epilogue="resid", extra=h, name="mm_mix_out")
    return out, (h, u, proj, mix, pre_x, pre_bc, dt, y_pre, states), got_att, got_ssd


def even_bwd(dh, p, saved, carry_att=(), carry_ssd=()):
    h, u, proj, mix, pre_x, pre_bc, dt, y_pre, states = saved
    dmix = matmul(dh, p["w_out"], b_kind="rowshard", layer=p["layer"], trans_b=True, name="mm_dmix")
    dw_out = matmul_tn(mix, dh, ti=512, tn=D_MODEL, out_dtype=GRAD_WIRE_DTYPE, name="mm_dw_out")
    dw_out = dw_out.reshape(N_DEV, MIX_W // N_DEV, D_MODEL)
    (dq, dk, dv, dqw, dkw, dsk), got_att = attention_bwd(proj, dmix, p["q_norm"], p["k_norm"], p["sinks"],
                                                         carry=list(carry_att))
    (dproj, dcwx, dcwbc, dcbx, dcbbc, ddtb, dalog, dd, dnw), got_ssd = ssd_bwd(
        proj, dmix, dq, dk, dv, pre_x, pre_bc, dt, y_pre, states, p["cw_x"], p["cw_bc"], p["dt_bias"], p["a_log"],
        p["d_exp"], p["ssm_norm"], carry=[dw_out] + list(carry_ssd))
    du = matmul(dproj, p["w_in"], trans_b=True, name="mm_du_in")
    dw_in = matmul_tn(u, dproj, ti=512, tn=PROJ_W, name="mm_dw_in")
    dw_in = _to_shards(_w_in_from_kernel(dw_in), 1).astype(GRAD_WIRE_DTYPE)
    dh_in, dg = rms_bwd(h, p["norm"], du, dh)
    grads = dict(norm=dg, w_in=dw_in, cw_x=dcwx, cw_bc=dcwbc, cb_x=dcbx, cb_bc=dcbbc, dt_bias=ddtb,
                 a_log=dalog, d_skip=dd, ssm_norm=dnw, q_norm=dqw, k_norm=dkw, sinks=dsk)
    return dh_in, grads, got_att, got_ssd


def conf_fwd(h, p, gather=()):
    v, u = mlp_up(h, p["norm"], p["pw1_w"], p["layer"], bias=p["pw1_b"], relu2=False, out_dtype=F32, name="mm_pw1")
    (c, s), got = conformer_mid_fwd(v, p["dw_w"], p["dw_b"], p["ln_g"], p["ln_b"], gather=list(gather))
    out = matmul(s, p["pw2_w"], b_kind="rowshard", layer=p["layer"], bias=p["pw2_b"], epilogue="resid", extra=h,
                 name="mm_pw2")
    return out, (h, u, v, c, s), got


def conf_bwd(dh, p, saved, carry=()):
    h, u, v, c, s = saved
    dpw2_b = col_sum(dh)
    ds = matmul(dh, p["pw2_w"], b_kind="rowshard", layer=p["layer"], trans_b=True, name="mm_ds")
    dpw2_w = matmul_tn(s, dh, ti=D_MODEL, tn=D_MODEL, out_dtype=GRAD_WIRE_DTYPE, name="mm_dpw2")
    dpw2_w = dpw2_w.reshape(N_DEV, D_MODEL // N_DEV, D_MODEL)
    dc, dln_g, dln_b = conformer_ln_bwd(ds, c, p["ln_g"], p["ln_b"])
    (dv, ddw_w, ddw_b, dpw1_b), got = conformer_conv_bwd(dc, v, p["dw_w"], carry=[dpw2_w] + list(carry))
    dpw1_w = mlp_dw_up(u, dv, name="mm_dpw1")
    dh_in, dg = mlp_du_rms_bwd(dv, p["pw1_w"], p["layer"], h, p["norm"], dh, name="mm_du_pw1")
    grads = dict(norm=dg, pw1_w=dpw1_w, pw1_b=dpw1_b, dw_w=ddw_w, dw_b=ddw_b, ln_g=dln_g, ln_b=dln_b, pw2_b=dpw2_b)
    return dh_in, grads, got


def mlp_fwd(h, p):
    act, u = mlp_up(h, p["norm"], p["w_up"], p["layer"])
    out = matmul(act, p["w_down"], b_kind="rowshard", layer=p["layer"], epilogue="resid", extra=h, name="mm_down")
    return out, (h, u, act)


def mlp_bwd(dh, p, saved):
    h, u, act = saved
    da = mlp_dact(dh, p["w_down"], act, p["layer"])
    dw_down = mlp_dw_down(act, dh).reshape(N_DEV, FF_BLOCK, D_MODEL)
    dw_up = mlp_dw_up(u, da)
    dh_in, dg = mlp_du_rms_bwd(da, p["w_up"], p["layer"], h, p["norm"], dh)
    return dh_in, dict(norm=dg, w_up=dw_up, w_down=dw_down)


def local_step(x, target, w, shards, first):
    n_even, n_odd = (DEPTH + 1) // 2, DEPTH // 2
    h = jnp.concatenate([jnp.zeros((FRONT_PAD, D_MODEL), F32), w["meta_tokens"].astype(F32), x], axis=0)
    even_p, odd_p, mlp_p = [None] * n_even, [None] * n_odd, [None] * DEPTH

    def even_params(i, g):
        cw = w["ssm_conv_w"][i]
        return dict(
            layer=0, norm=_row(w["mix_norm_even"][i]), w_in=_w_in_to_kernel(_from_shards(g[0][:, 0], 1)), w_out=g[1],
            cw_x=cw[:, :SSM_INNER], cw_bc=cw[:, SSM_INNER:], cb_x=_row(w["ssm_conv_b"][i][:SSM_INNER]),
            cb_bc=_row(w["ssm_conv_b"][i][SSM_INNER:]), dt_bias=_row(w["dt_bias"][i], LANE),
            a_log=_row(w["a_log"][i], LANE), d_exp=_row(jnp.repeat(w["d_skip"][i], SSM_HEAD_DIM)),
            ssm_norm=_row(w["ssm_norm_w"][i]), q_norm=_row(jnp.tile(w["q_norm"][i], ATT_HEADS)),
            k_norm=_row(jnp.tile(w["k_norm"][i], ATT_KV_HEADS)), sinks=w["sinks"][i].astype(F32))

    def odd_params(i, g):
        return dict(
            layer=0, norm=_row(w["mix_norm_odd"][i]), pw1_w=g[0], pw1_b=_row(w["pw1_b"][i]),
            dw_w=jnp.pad(w["dw_w"][i], ((0, CONF_HALO - CONF_KERNEL), (0, 0))), dw_b=_row(w["dw_b"][i]),
            ln_g=_row(w["ln_g"][i]), ln_b=_row(w["ln_b"][i]), pw2_w=g[1], pw2_b=_row(w["pw2_b"][i]))

    gathered = {}
    tape = []
    for layer in range(DEPTH):
        i = layer // 2
        nxt = shards[layer + 1] if layer + 1 < DEPTH and layer + 1 not in gathered else ()
        if layer == 0:
            even_p[0] = even_params(0, [first[0], None])
            h, saved, got_att, gathered[1] = even_fwd(h, even_p[0], gather_att=shards[0][1:], gather_ssd=nxt)
            mlp_w = got_att[1:]
        else:
            g = gathered.pop(layer)
            mlp_w = g[2:]
            if layer % 2 == 0:
                even_p[i] = even_params(i, g[:2])
                h, saved, got, _ = even_fwd(h, even_p[i], gather_att=nxt)
            else:
                odd_p[i] = odd_params(i, g[:2])
                h, saved, got = conf_fwd(h, odd_p[i], gather=nxt)
            if nxt:
                gathered[layer + 1] = got
        tape.append(saved)
        mlp_p[layer] = dict(layer=0, norm=_row(w["mlp_norm"][layer]), w_up=mlp_w[0], w_down=mlp_w[1])
        h, saved = mlp_fwd(h, mlp_p[layer])
        tape.append(saved)
    dh, loss_row = loss_fwd_bwd(h, target)

    ge = [None] * n_even
    go = [None] * n_odd
    gm = [None] * DEPTH
    received = {n: [None] * shape[0] for n, shape, _ in PARAMS if n in MATMUL_WEIGHTS}
    pending = []

    def store(tags, arrays):
        for (n, l), a in zip(tags, arrays):
            received[n][l] = a

    for layer in reversed(range(DEPTH)):
        i = layer // 2
        dh, gm[layer] = mlp_bwd(dh, mlp_p[layer], tape.pop())
        mlp_tags = [("w_up", layer), ("w_down", layer)]
        mlp_parts = [gm[layer]["w_up"], gm[layer]["w_down"]]
        if layer % 2 == 0:
            riders, pending = pending, []
            dh, ge[i], got_att, got_ssd = even_bwd(dh, even_p[i], tape.pop(), carry_att=mlp_parts,
                                                   carry_ssd=[a for _, _, a in riders])
            store(mlp_tags, got_att)
            store([("w_out", i)] + [(n, l) for n, l, _ in riders], got_ssd)
            pending.append(("w_in", i, ge[i]["w_in"]))
        else:
            dh, go[i], got = conf_bwd(dh, odd_p[i], tape.pop(), carry=mlp_parts)
            store([("pw2_w", i)] + mlp_tags, got)
            pending.append(("pw1_w", i, go[i]["pw1_w"]))

    stack = lambda gs, f: jnp.stack([f(g) for g in gs])
    grads = dict(
        meta_tokens=dh[FRONT_PAD:BLOCK],
        mix_norm_even=stack(ge, lambda g: g["norm"][0]),
        ssm_conv_w=stack(ge, lambda g: jnp.concatenate([g["cw_x"], g["cw_bc"]], axis=1)),
        ssm_conv_b=stack(ge, lambda g: jnp.concatenate([g["cb_x"][0], g["cb_bc"][0]])),
        dt_bias=stack(ge, lambda g: g["dt_bias"][0, :SSM_HEADS]),
        a_log=stack(ge, lambda g: g["a_log"][0, :SSM_HEADS]),
        d_skip=stack(ge, lambda g: g["d_skip"][0, :SSM_HEADS]),
        ssm_norm_w=stack(ge, lambda g: g["ssm_norm"][0]),
        q_norm=stack(ge, lambda g: g["q_norm"][0, :HEAD_DIM]),
        k_norm=stack(ge, lambda g: g["k_norm"][0, :HEAD_DIM]),
        sinks=stack(ge, lambda g: g["sinks"][0, :ATT_HEADS]),
        mix_norm_odd=stack(go, lambda g: g["norm"][0]),
        pw1_b=stack(go, lambda g: g["pw1_b"][0]),
        dw_w=stack(go, lambda g: g["dw_w"][:CONF_KERNEL]),
        dw_b=stack(go, lambda g: g["dw_b"][0]),
        ln_g=stack(go, lambda g: g["ln_g"][0]),
        ln_b=stack(go, lambda g: g["ln_b"][0]),
        pw2_b=stack(go, lambda g: g["pw2_b"][0]),
        mlp_norm=stack(gm, lambda g: g["norm"][0]),
    )
    return loss_row[0, 0], dh[BLOCK:], grads, received, pending


PARAMS = (
    ("meta_tokens", (16, 1024), 1), ("mix_norm_even", (2, 1024), None), ("w_in", (2, 1024, 3088), 2),
    ("ssm_conv_w", (2, 4, 1280), 2), ("ssm_conv_b", (2, 1280), None), ("dt_bias", (2, 16), None),
    ("a_log", (2, 16), None), ("d_skip", (2, 16), None), ("ssm_norm_w", (2, 1024), None), ("q_norm", (2, 64), None),
    ("k_norm", (2, 64), None), ("sinks", (2, 8), None), ("w_out", (2, 1536, 1024), 1), ("mix_norm_odd", (2, 1024), 1),
    ("pw1_w", (2, 1024, 2048), 2), ("pw1_b", (2, 2048), 1), ("dw_w", (2, 31, 1024), 2), ("dw_b", (2, 1024), 1),
    ("ln_g", (2, 1024), 1), ("ln_b", (2, 1024), 1), ("pw2_w", (2, 1024, 1024), 1), ("pw2_b", (2, 1024), 1),
    ("mlp_norm", (4, 1024), None), ("w_up", (4, 1024, 4096), 2), ("w_down", (4, 4096, 1024), 1),
)
MATMUL_WEIGHTS = ("w_in", "w_out", "pw1_w", "pw2_w", "w_up", "w_down")
PACK_ROW_ALIGN = 16 * PACK_W


def _block_shape(shape, axis):
    if axis is None:
        return tuple(shape)
    return tuple(s // N_DEV if a == axis else s for a, s in enumerate(shape))


def _numel(shape):
    return math.prod(shape)


def _pack(arrays, dtype):
    flat = jnp.concatenate([a.reshape(-1).astype(dtype) for a in arrays])
    n = flat.shape[0]
    padded = -(-n // PACK_ROW_ALIGN) * PACK_ROW_ALIGN
    return jnp.pad(flat, (0, padded - n)).reshape(-1, PACK_W)


def _pack_rows(arrays_by_dev, dtype):
    flat = jnp.concatenate([a.reshape(N_DEV, -1).astype(dtype) for a in arrays_by_dev], axis=1)
    n = flat.shape[1]
    padded = -(-n // PACK_ROW_ALIGN) * PACK_ROW_ALIGN
    return jnp.pad(flat, ((0, 0), (0, padded - n))).reshape(N_DEV, -1, PACK_W)


def _to_shards(full, axis):
    shape = full.shape
    split = full.reshape(shape[:axis] + (N_DEV, shape[axis] // N_DEV) + shape[axis + 1:])
    return jnp.moveaxis(split, axis, 0)


def _from_shards(blocks, axis):
    moved = jnp.moveaxis(blocks, 0, axis)
    shape = moved.shape
    return moved.reshape(shape[:axis] + (shape[axis] * shape[axis + 1],) + shape[axis + 2:])


_MESH = pl.DeviceIdType.MESH
_ANY = pl.BlockSpec(memory_space=pl.ANY)


def _mesh_place():
    x, y, c = lax.axis_index("x"), lax.axis_index("y"), lax.axis_index("c")
    return x, y, c


def _peer(x, y, c, rel):
    dx, dy, dc = (rel >> 2) & 1, (rel >> 1) & 1, rel & 1
    return (x ^ dx if dx else x, y ^ dy if dy else y, c ^ dc if dc else c)


def _dev_index(x, y, c):
    return 4 * x + 2 * y + c


def all_gather_weights(bigs, small):
    nt = len(bigs)

    def body(*refs):
        big_refs, small_ref = refs[:nt], refs[nt]
        big_outs, small_out = refs[nt + 1:2 * nt + 1], refs[2 * nt + 1]
        send_sems, recv_sems, small_send, small_recv, local_sems = refs[2 * nt + 2:]
        x, y, c = _mesh_place()
        me = (x, y, c)
        sibling = (x, y, 1 - c)
        chips = [(1 - x, y), (x, 1 - y), (1 - x, 1 - y)]

        def big_copy(t, k, block, to, from_input=False):
            dst = big_outs[t].at[_dev_index(*block)]
            return pltpu.make_async_remote_copy(src_ref=big_refs[t] if from_input else dst, dst_ref=dst,
                                                send_sem=send_sems.at[t, k], recv_sem=recv_sems.at[t, k],
                                                device_id=to, device_id_type=_MESH)

        def small_copy(rel, block, to):
            return pltpu.make_async_remote_copy(src_ref=small_ref, dst_ref=small_out.at[_dev_index(*block)],
                                                send_sem=small_send.at[rel - 1], recv_sem=small_recv.at[rel - 1],
                                                device_id=to, device_id_type=_MESH)

        mine = [pltpu.make_async_copy(big_refs[t], big_outs[t].at[_dev_index(*me)], local_sems.at[t]) for t in range(nt)]
        mine.append(pltpu.make_async_copy(small_ref, small_out.at[_dev_index(*me)], local_sems.at[nt]))
        for cp in mine:
            cp.start()
        first = []
        for t in range(nt):
            first.append(big_copy(t, 0, me, sibling, from_input=True))
            first += [big_copy(t, 1 + j, me, (*chip, c), from_input=True) for j, chip in enumerate(chips)]
        for cp in first:
            cp.start()
        smalls = [small_copy(rel, me, _peer(x, y, c, rel)) for rel in range(1, N_DEV)]
        for cp in smalls:
            cp.start()
        passed = []
        for j, chip in enumerate(chips):
            for t in range(nt):
                big_copy(t, 1 + j, (*chip, c), me).wait_recv()
                fwd = big_copy(t, 4 + j, (*chip, c), sibling)
                fwd.start()
                passed.append(fwd)
        for t in range(nt):
            big_copy(t, 0, sibling, me).wait_recv()
            for j, chip in enumerate(chips):
                big_copy(t, 4 + j, (*chip, 1 - c), me).wait_recv()
        for rel in range(1, N_DEV):
            small_copy(rel, _peer(x, y, c, rel), me).wait_recv()
        for cp in first + passed + smalls:
            cp.wait_send()
        for cp in mine:
            cp.wait()

    return pl.pallas_call(
        body, in_specs=[_ANY] * (nt + 1), out_specs=[_ANY] * (nt + 1),
        out_shape=[jax.ShapeDtypeStruct((N_DEV,) + b.shape, b.dtype) for b in bigs]
        + [jax.ShapeDtypeStruct((N_DEV,) + small.shape, small.dtype)],
        scratch_shapes=[pltpu.SemaphoreType.DMA((nt, N_DEV - 1)), pltpu.SemaphoreType.DMA((nt, N_DEV - 1)),
                        pltpu.SemaphoreType.DMA((N_DEV - 1,)), pltpu.SemaphoreType.DMA((N_DEV - 1,)),
                        pltpu.SemaphoreType.DMA((nt + 1,))],
        name="all_gather_weights",
    )(*bigs, small)


def _gather_copies(in_refs, out_refs, send_sems, recv_sems, local_sems):
    x, y, c = _mesh_place()
    me = (x, y, c)
    sibling = (x, y, 1 - c)
    chips = [(1 - x, y), (x, 1 - y), (1 - x, 1 - y)]
    nt = len(in_refs)

    def copy(t, k, block, to, from_input=False):
        dst = out_refs[t].at[_dev_index(*block)]
        return pltpu.make_async_remote_copy(src_ref=in_refs[t] if from_input else dst, dst_ref=dst,
                                            send_sem=send_sems.at[t, k], recv_sem=recv_sems.at[t, k],
                                            device_id=to, device_id_type=_MESH)

    mine = [pltpu.make_async_copy(in_refs[t], out_refs[t].at[_dev_index(*me)], local_sems.at[t]) for t in range(nt)]
    first, landed, forward, last = [], [], [], []
    for t in range(nt):
        first.append(copy(t, 0, me, sibling, from_input=True))
        last.append(copy(t, 0, sibling, me))
        for j, chip in enumerate(chips):
            first.append(copy(t, 1 + j, me, (*chip, c), from_input=True))
            landed.append(copy(t, 1 + j, (*chip, c), me))
            forward.append(copy(t, 4 + j, (*chip, c), sibling))
            last.append(copy(t, 4 + j, (*chip, 1 - c), me))
    return mine, first, landed, forward, last


GATHER_FORWARD_LEAD = 8


def _with_gather(body, n_in, n_out, shards, steps):
    n = len(shards)
    if n == 0:
        return body, [], [], [], []
    fwd_step = max(steps - 1 - GATHER_FORWARD_LEAD, 0)

    def wrapped(*refs):
        ins, g_in = refs[:n_in], refs[n_in:n_in + n]
        outs, g_out = refs[n_in + n:n_in + n + n_out], refs[n_in + n + n_out:n_in + 2 * n + n_out]
        scratch = refs[n_in + 2 * n + n_out:len(refs) - 3]
        sems = refs[len(refs) - 3:]
        i = pl.program_id(0)

        @pl.when(i == 0)
        def _():
            mine, first, _, _, _ = _gather_copies(g_in, g_out, *sems)
            for cp in mine + first:
                cp.start()

        @pl.when(i == fwd_step)
        def _():
            _, _, landed, forward, _ = _gather_copies(g_in, g_out, *sems)
            for arrived, onward in zip(landed, forward):
                arrived.wait_recv()
                onward.start()

        body(*ins, *outs, *scratch)

        @pl.when(i == steps - 1)
        def _():
            mine, first, _, forward, last = _gather_copies(g_in, g_out, *sems)
            for cp in last:
                cp.wait_recv()
            for cp in first + forward:
                cp.wait_send()
            for cp in mine:
                cp.wait()

    return (wrapped, [_ANY] * n, [_ANY] * n, [jax.ShapeDtypeStruct((N_DEV,) + a.shape, a.dtype) for a in shards],
            [pltpu.SemaphoreType.DMA((n, N_DEV - 1)), pltpu.SemaphoreType.DMA((n, N_DEV - 1)),
             pltpu.SemaphoreType.DMA((n,))])


def _exchange_copies(in_refs, out_refs, send_sems, recv_sems, local_sems):
    x, y, c = _mesh_place()
    me = _dev_index(x, y, c)
    mine, sends, arrivals = [], [], []
    for p, (src, dst) in enumerate(zip(in_refs, out_refs)):
        mine.append(pltpu.make_async_copy(src.at[me], dst.at[me], local_sems.at[p]))
        for rel in range(1, N_DEV):
            peer = _peer(x, y, c, rel)
            there = _dev_index(*peer)
            sems = dict(send_sem=send_sems.at[rel - 1, p], recv_sem=recv_sems.at[rel - 1, p], device_id=peer,
                        device_id_type=_MESH)
            sends.append(pltpu.make_async_remote_copy(src_ref=src.at[there], dst_ref=dst.at[me], **sems))
            arrivals.append(pltpu.make_async_remote_copy(src_ref=src.at[me], dst_ref=dst.at[there], **sems))
    return mine, sends, arrivals


def _with_exchange(body, n_in, n_out, carry, grid):
    n = len(carry)
    if n == 0:
        return body, [], [], [], []
    grid = (grid,) if isinstance(grid, int) else tuple(grid)

    def at_step(corner):
        hit = pl.program_id(0) == corner[0]
        for axis in range(1, len(grid)):
            hit = jnp.logical_and(hit, pl.program_id(axis) == corner[axis])
        return hit

    def wrapped(*refs):
        ins, ex_in = refs[:n_in], refs[n_in:n_in + n]
        outs, ex_out = refs[n_in + n:n_in + n + n_out], refs[n_in + n + n_out:n_in + 2 * n + n_out]
        scratch = refs[n_in + 2 * n + n_out:len(refs) - 3]
        send_sems, recv_sems, local_sems = refs[len(refs) - 3:]

        @pl.when(at_step([0] * len(grid)))
        def _():
            mine, sends, _ = _exchange_copies(ex_in, ex_out, send_sems, recv_sems, local_sems)
            for cp in mine + sends:
                cp.start()

        body(*ins, *outs, *scratch)

        @pl.when(at_step([g - 1 for g in grid]))
        def _():
            mine, sends, arrivals = _exchange_copies(ex_in, ex_out, send_sems, recv_sems, local_sems)
            for cp in arrivals:
                cp.wait_recv()
            for cp in sends:
                cp.wait_send()
            for cp in mine:
                cp.wait()

    return (wrapped, [_ANY] * n, [_ANY] * n, [jax.ShapeDtypeStruct(a.shape, a.dtype) for a in carry],
            [pltpu.SemaphoreType.DMA((N_DEV - 1, n)), pltpu.SemaphoreType.DMA((N_DEV - 1, n)),
             pltpu.SemaphoreType.DMA((n,))])


def reduce_adamw(parts, w, m, v, tr, carry=()):
    nl, r, cols = w.shape
    assert len(parts) == nl

    def body(*refs):
        p_refs = refs[:nl]
        w_ref, m_ref, v_ref, g_ref, d_ref, nm_ref, nv_ref, g_acc = refs[nl:]
        layer = pl.program_id(0)
        for l in range(nl):
            @pl.when(layer == l)
            def _(l=l):
                g = p_refs[l][0].astype(F32)
                for d in range(1, N_DEV):
                    g = g + p_refs[l][d].astype(F32)
                g_acc[...] = g

        g = g_acc[...]
        g_ref[...] = g
        nm = ADAM_B1 * m_ref[...] + (1.0 - ADAM_B1) * g
        nv = ADAM_B2 * v_ref[...] + (1.0 - ADAM_B2) * (g * g)
        nm_ref[...] = nm
        nv_ref[...] = nv
        m_hat = nm / (1.0 - ADAM_B1 ** ADAM_STEP)
        v_hat = nv / (1.0 - ADAM_B2 ** ADAM_STEP)
        d_ref[...] = -ADAM_LR * (m_hat / (jnp.sqrt(v_hat) + ADAM_EPS) + ADAM_WD * w_ref[...])

    row = pl.BlockSpec((None, tr, cols), lambda l, i: (l, i, 0))

    def part_spec(own):
        def index(l, i):
            return (0, jnp.where(l == own, i, jnp.where(l < own, 0, r // tr - 1)), 0)
        return pl.BlockSpec((N_DEV, tr, cols), index)

    grid = (nl, r // tr)
    body, ex_in, ex_out, ex_shape, ex_scratch = _with_exchange(body, nl + 3, 4, carry, grid)
    outs = pl.pallas_call(
        body, grid=grid,
        in_specs=[part_spec(l) for l in range(nl)] + [row, row, row] + ex_in,
        out_specs=[row, row, row, row] + ex_out,
        out_shape=[jax.ShapeDtypeStruct((nl, r, cols), F32)] * 4 + ex_shape,
        scratch_shapes=[pltpu.VMEM((tr, cols), F32)] + ex_scratch,
        name="reduce_adamw", compiler_params=_params("arbitrary", "arbitrary"),
    )(*parts, w, m, v, *carry)
    return outs[:4], outs[4:]


ADAMW_TILE_BYTES = 1 << 19


def _adamw_tile(rows, cols):
    lanes = -(-cols // LANE) * LANE
    best = None
    for tr in range(16, rows + 1, 16):
        if rows % tr == 0 and tr * lanes * 4 <= ADAMW_TILE_BYTES:
            best = tr
    if best is None:
        raise ValueError((rows, cols))
    return best


def kernel(x, meta_tokens, mix_norm_even, w_in, ssm_conv_w, ssm_conv_b, dt_bias, a_log, d_skip, ssm_norm_w, q_norm, k_norm, sinks, w_out, mix_norm_odd, pw1_w, pw1_b, dw_w, dw_b, ln_g, ln_b, pw2_w, pw2_b, mlp_norm, w_up, w_down, loss_target, m_meta_tokens, m_mix_norm_even, m_w_in, m_ssm_conv_w, m_ssm_conv_b, m_dt_bias, m_a_log, m_d_skip, m_ssm_norm_w, m_q_norm, m_k_norm, m_sinks, m_w_out, m_mix_norm_odd, m_pw1_w, m_pw1_b, m_dw_w, m_dw_b, m_ln_g, m_ln_b, m_pw2_w, m_pw2_b, m_mlp_norm, m_w_up, m_w_down, v_meta_tokens, v_mix_norm_even, v_w_in, v_ssm_conv_w, v_ssm_conv_b, v_dt_bias, v_a_log, v_d_skip, v_ssm_norm_w, v_q_norm, v_k_norm, v_sinks, v_w_out, v_mix_norm_odd, v_pw1_w, v_pw1_b, v_dw_w, v_dw_b, v_ln_g, v_ln_b, v_pw2_w, v_pw2_b, v_mlp_norm, v_w_up, v_w_down):
    names = [p[0] for p in PARAMS]
    w_loc = dict(zip(names, (meta_tokens, mix_norm_even, w_in, ssm_conv_w, ssm_conv_b, dt_bias, a_log, d_skip, ssm_norm_w, q_norm, k_norm, sinks, w_out, mix_norm_odd, pw1_w, pw1_b, dw_w, dw_b, ln_g, ln_b, pw2_w, pw2_b, mlp_norm, w_up, w_down)))
    m_loc = dict(zip(names, (m_meta_tokens, m_mix_norm_even, m_w_in, m_ssm_conv_w, m_ssm_conv_b, m_dt_bias, m_a_log, m_d_skip, m_ssm_norm_w, m_q_norm, m_k_norm, m_sinks, m_w_out, m_mix_norm_odd, m_pw1_w, m_pw1_b, m_dw_w, m_dw_b, m_ln_g, m_ln_b, m_pw2_w, m_pw2_b, m_mlp_norm, m_w_up, m_w_down)))
    v_loc = dict(zip(names, (v_meta_tokens, v_mix_norm_even, v_w_in, v_ssm_conv_w, v_ssm_conv_b, v_dt_bias, v_a_log, v_d_skip, v_ssm_norm_w, v_q_norm, v_k_norm, v_sinks, v_w_out, v_mix_norm_odd, v_pw1_w, v_pw1_b, v_dw_w, v_dw_b, v_ln_g, v_ln_b, v_pw2_w, v_pw2_b, v_mlp_norm, v_w_up, v_w_down)))
    small_sharded = [p for p in PARAMS if p[2] is not None and p[0] not in MATMUL_WEIGHTS]
    replicated = [p for p in PARAMS if p[2] is None]
    small_list = small_sharded + replicated

    def layer_shards(layer):
        i = layer // 2
        mixer = ("w_in", "w_out") if layer % 2 == 0 else ("pw1_w", "pw2_w")
        return [w_loc[n][i:i + 1].astype(MXU_DTYPE) for n in mixer] + [
            w_loc[n][layer:layer + 1].astype(MXU_DTYPE) for n in ("w_up", "w_down")]

    shards = [layer_shards(layer) for layer in range(DEPTH)]
    gathered = all_gather_weights(shards[0][:1], _pack([w_loc[n] for n, _, _ in small_sharded], F32))
    w_full = {n: w_loc[n] for n, _, _ in replicated}
    flat = gathered[-1].reshape(N_DEV, -1)
    off = 0
    for n, shape, axis in small_sharded:
        blk = _block_shape(shape, axis)
        w_full[n] = _from_shards(flat[:, off:off + _numel(blk)].reshape((N_DEV,) + blk), axis)
        off += _numel(blk)

    loss_local, grad_x, g_full, received, pending = local_step(x[0], loss_target[0], w_full, shards, gathered[:-1])
    loss = lax.psum(loss_local, ("x", "y", "c"))

    by_dev = [_to_shards(g_full[n], axis) for n, _, axis in small_sharded]
    by_dev += [jnp.broadcast_to(g_full[n][None], (N_DEV,) + tuple(shape)) for n, shape, _ in replicated]
    late_names = {n for n, _, _ in pending}
    ready = sorted((n for n in MATMUL_WEIGHTS if n not in late_names), key=lambda n: -_numel(w_loc[n].shape))
    order = ready + [n for n in MATMUL_WEIGHTS if n in late_names]
    rides = {order[0]: [a for _, _, a in pending], order[1]: [_pack_rows(by_dev, F32)]}
    out = {}
    for n in order:
        nl, r, cols = w_loc[n].shape
        out[n], got = reduce_adamw(received[n], w_loc[n], m_loc[n], v_loc[n], _adamw_tile(r, cols),
                                   carry=rides.get(n, ()))
        if n == order[0]:
            for (pn, l, _), a in zip(pending, got):
                received[pn][l] = a
        elif n == order[1]:
            small_parts = got[0]
    pk = lambda d: _pack([d[n] for n, _, _ in small_list], F32)[None]
    rows = small_parts.shape[1]
    small_out, _ = reduce_adamw([small_parts], pk(w_loc), pk(m_loc), pk(v_loc), _adamw_tile(rows, PACK_W))
    flats = [buf.reshape(-1) for buf in small_out]
    off = 0
    for n, shape, axis in small_list:
        blk = _block_shape(shape, axis)
        out[n] = tuple(f[off:off + _numel(blk)].reshape(blk) for f in flats)
        off += _numel(blk)
    return (loss, grad_x[None], *[out[n][0] for n in names], *[out[n][1] for n in names],
            *[out[n][2] for n in names], *[out[n][3] for n in names])
```

```python
import math

import jax
import jax.numpy as jnp
from jax import lax
from jax.experimental import pallas as pl
from jax.experimental.pallas import tpu as pltpu

F32 = jnp.float32
MXU_DTYPE = jnp.bfloat16
GRAD_WIRE_DTYPE = jnp.bfloat16
HIGHEST = lax.Precision.HIGHEST

D_MODEL = 1024
N_META = 16
BLOCK = 128
FRONT_PAD = BLOCK - N_META
ATT_HEADS = 8
ATT_KV_HEADS = 2
HEAD_DIM = 64
SSM_HEADS = 16
SSM_HEAD_DIM = 64
SSM_INNER = 1024
SSM_GROUPS = 2
SSM_STATE = 64
SSM_CONV = 4
CONF_KERNEL = 31
D_FF = 4096
EPS = 1e-6
LN_EPS = 1e-5
Q_W = 512
KV_W = 128
IN_W = 3088
MIX_W = 1536
DEPTH = 4
N_DEV = 8

ADAM_LR = 0.001
ADAM_B1 = 0.9
ADAM_B2 = 0.999
ADAM_EPS = 1e-08
ADAM_WD = 0.01
ADAM_STEP = 10

PROJ_W = 3200
COL_Z, COL_XS, COL_Q, COL_BC, COL_K, COL_V, COL_DT = 0, 1024, 2048, 2560, 2816, 2944, 3072

ROW_TILE = 640
TN_ROW_TILE = 1664
ACC_BYTES = 8 * 1024 * 1024
VMEM_LIMIT = 56 * 1024 * 1024
LANE = 128
PACK_W = 1024


def _params(*sem):
    return pltpu.CompilerParams(dimension_semantics=sem, vmem_limit_bytes=VMEM_LIMIT)


def _mx(x):
    return x.astype(MXU_DTYPE)


def _dot(a, b):
    return jnp.dot(_mx(a), _mx(b), preferred_element_type=F32)


def _dot_nt(a, b):
    return lax.dot_general(_mx(a), _mx(b), (((1,), (1,)), ((), ())), preferred_element_type=F32)


def _dot_tn(a, b):
    return lax.dot_general(_mx(a), _mx(b), (((0,), (0,)), ((), ())), preferred_element_type=F32)


def _split3(x):
    hi = x.astype(jnp.bfloat16)
    r1 = x - hi.astype(F32)
    mid = r1.astype(jnp.bfloat16)
    lo = (r1 - mid.astype(F32)).astype(jnp.bfloat16)
    return hi, mid, lo


def _sel_dot(x, sel, dims):
    x_first = dims[2]
    if sel.dtype == jnp.bool_:
        sel = jnp.where(sel, 1.0, 0.0)
    one = sel.astype(jnp.bfloat16)
    acc = None
    for part in _split3(x):
        args = (part, one) if x_first else (one, part)
        t = lax.dot_general(*args, (dims[:2], ((), ())), preferred_element_type=F32)
        acc = t if acc is None else acc + t
    return acc


def _dot_hi(a, b, exact="a"):
    if exact == "a":
        return _sel_dot(a, b, ((1,), (0,), True))
    return _sel_dot(b, a, ((1,), (0,), False))


def _dot_tn_hi(a, b):
    return _sel_dot(a, b, ((0,), (0,), True))


def _sigmoid(x):
    return 1.0 / (1.0 + jnp.exp(-x))


def _row_ids(start, rows, cols):
    return start + lax.broadcasted_iota(jnp.int32, (rows, cols), 0)


def rms_fwd(h, g, name="rms_fwd"):
    m, d = h.shape
    tm = ROW_TILE

    def body(h_ref, g_ref, u_ref):
        x = h_ref[...]
        r = lax.rsqrt(jnp.mean(x * x, -1, keepdims=True) + EPS)
        u_ref[...] = ((x * r) * g_ref[...]).astype(u_ref.dtype)

    return pl.pallas_call(
        body, grid=(m // tm,),
        in_specs=[pl.BlockSpec((tm, d), lambda i: (i, 0)), pl.BlockSpec((1, d), lambda i: (0, 0))],
        out_specs=pl.BlockSpec((tm, d), lambda i: (i, 0)),
        out_shape=jax.ShapeDtypeStruct((m, d), MXU_DTYPE), name=name, compiler_params=_params("arbitrary"),
    )(h, g)


def rms_bwd(h, g, du, dh_out, name="rms_bwd"):
    m, d = h.shape
    tm = ROW_TILE

    def body(h_ref, g_ref, du_ref, dho_ref, dh_ref, dg_ref):
        @pl.when(pl.program_id(0) == 0)
        def _():
            dg_ref[...] = jnp.zeros_like(dg_ref)

        x = h_ref[...]
        du_ = du_ref[...]
        r = lax.rsqrt(jnp.mean(x * x, -1, keepdims=True) + EPS)
        gy = du_ * g_ref[...]
        dx = r * gy - x * ((r * r * r) * jnp.mean(x * gy, -1, keepdims=True))
        dh_ref[...] = dho_ref[...] + dx
        dg_ref[...] += jnp.sum(du_ * (x * r), axis=0, keepdims=True)

    row = pl.BlockSpec((tm, d), lambda i: (i, 0))
    vec = pl.BlockSpec((1, d), lambda i: (0, 0))
    return pl.pallas_call(
        body, grid=(m // tm,), in_specs=[row, vec, row, row], out_specs=[row, vec],
        out_shape=[jax.ShapeDtypeStruct((m, d), F32), jax.ShapeDtypeStruct((1, d), F32)],
        name=name, compiler_params=_params("arbitrary"),
    )(h, g, du, dh_out)


def loss_fwd_bwd(h, target, name="loss"):
    m, d = h.shape
    nb = m // BLOCK

    def body(h_ref, t_ref, dh_ref, l_ref):
        i = pl.program_id(0)

        @pl.when(i == 0)
        def _():
            l_ref[...] = jnp.zeros_like(l_ref)
            dh_ref[...] = jnp.zeros_like(dh_ref)

        @pl.when(i > 0)
        def _():
            e = h_ref[...] - t_ref[...]
            dh_ref[...] = e * (1.0 / d)
            s = jnp.sum(jnp.sum(e * e, axis=-1, keepdims=True), axis=0, keepdims=True)
            l_ref[...] += jnp.broadcast_to(s * (0.5 / d), l_ref.shape)

    return pl.pallas_call(
        body, grid=(nb,),
        in_specs=[pl.BlockSpec((BLOCK, d), lambda i: (i, 0)),
                  pl.BlockSpec((BLOCK, d), lambda i: (jnp.maximum(i - 1, 0), 0))],
        out_specs=[pl.BlockSpec((BLOCK, d), lambda i: (i, 0)), pl.BlockSpec((1, LANE), lambda i: (0, 0))],
        out_shape=[jax.ShapeDtypeStruct((m, d), F32), jax.ShapeDtypeStruct((1, LANE), F32)],
        name=name, compiler_params=_params("arbitrary"),
    )(h, target)


def col_sum(x, name="col_sum"):
    m, n = x.shape
    tm = ROW_TILE

    def body(x_ref, o_ref):
        @pl.when(pl.program_id(0) == 0)
        def _():
            o_ref[...] = jnp.zeros_like(o_ref)

        o_ref[...] += jnp.sum(x_ref[...].astype(F32), axis=0, keepdims=True)

    return pl.pallas_call(
        body, grid=(m // tm,), in_specs=[pl.BlockSpec((tm, n), lambda i: (i, 0))],
        out_specs=pl.BlockSpec((1, n), lambda i: (0, 0)), out_shape=jax.ShapeDtypeStruct((1, n), F32),
        name=name, compiler_params=_params("arbitrary"),
    )(x)


def matmul(a, b, *, b_kind="full", layer=0, trans_b=False, tn=None, epilogue=None, bias=None, extra=None,
           out_dtype=F32, carry=(), name="matmul"):
    m, k = a.shape
    tm = ROW_TILE
    merge = False
    if b_kind == "full":
        n = b.shape[0] if trans_b else b.shape[1]
        tn = n if tn is None else tn
        b_spec = pl.BlockSpec((tn, k), lambda i, j: (j, 0)) if trans_b else pl.BlockSpec((k, tn), lambda i, j: (0, j))
    elif b_kind == "rowshard":
        ks, wn = b.shape[2], b.shape[3]
        if trans_b and tn == ks:
            assert wn == k
            n = N_DEV * ks
            b_spec = pl.BlockSpec((None, None, ks, wn), lambda i, j: (j, layer, 0, 0))
        else:
            assert tn is None
            merge = True
            n = N_DEV * ks if trans_b else wn
            assert (wn if trans_b else N_DEV * ks) == k
            tn = n
            b_spec = pl.BlockSpec((N_DEV, None, ks, wn), lambda i, j: (0, layer, 0, 0))
    else:
        raise ValueError(b_kind)
    has_bias = bias is not None
    has_extra = extra is not None

    def body(*refs):
        a_ref, b_ref = refs[0], refs[1]
        pos = 2
        bias_ref = extra_ref = None
        if has_bias:
            bias_ref = refs[pos]
            pos += 1
        if has_extra:
            extra_ref = refs[pos]
            pos += 1
        outs = refs[pos:]
        w = b_ref[...]
        if merge:
            w = w.reshape(N_DEV * w.shape[1], w.shape[2])
        if trans_b:
            acc = _dot_nt(a_ref[...], w)
        else:
            acc = _dot(a_ref[...], w)
        if has_bias:
            acc = acc + bias_ref[...]
        if epilogue is None:
            outs[0][...] = acc.astype(outs[0].dtype)
        elif epilogue == "relu2":
            outs[0][...] = acc
            r = jnp.maximum(acc, 0.0)
            outs[1][...] = (r * r).astype(outs[1].dtype)
        elif epilogue == "drelu2":
            outs[0][...] = (acc * (2.0 * jnp.maximum(extra_ref[...], 0.0))).astype(outs[0].dtype)
        elif epilogue == "resid":
            rows = _row_ids(pl.program_id(0) * tm, tm, tn)
            outs[0][...] = extra_ref[...] + jnp.where(rows >= FRONT_PAD, acc, 0.0)
        else:
            raise ValueError(epilogue)

    in_specs = [pl.BlockSpec((tm, k), lambda i, j: (i, 0)), b_spec]
    args = [a, b]
    if has_bias:
        in_specs.append(pl.BlockSpec((1, tn), lambda i, j: (0, j)))
        args.append(bias)
    if has_extra:
        in_specs.append(pl.BlockSpec((tm, tn), lambda i, j: (i, j)))
        args.append(extra)
    tile = pl.BlockSpec((tm, tn), lambda i, j: (i, j))
    if epilogue == "relu2":
        out_specs = [tile, tile]
        out_shape = [jax.ShapeDtypeStruct((m, n), F32), jax.ShapeDtypeStruct((m, n), MXU_DTYPE)]
    else:
        out_specs = [tile]
        out_shape = [jax.ShapeDtypeStruct((m, n), out_dtype)]
    n_out = len(out_specs)
    grid = (m // tm, n // tn)
    body, ex_in, ex_out, ex_shape, ex_scratch = _with_exchange(body, len(args), n_out, carry, grid)
    outs = pl.pallas_call(
        body, grid=grid, in_specs=in_specs + ex_in, out_specs=out_specs + ex_out, out_shape=out_shape + ex_shape,
        scratch_shapes=ex_scratch, name=name, compiler_params=_params("arbitrary", "arbitrary"),
    )(*args, *carry)
    result = outs[0] if n_out == 1 else tuple(outs[:n_out])
    return (result, outs[n_out:]) if carry else result


def matmul_tn(x, dy, *, ti, tn, out_dtype=F32, name="matmul_tn"):
    m, k1 = x.shape
    n = dy.shape[1]
    tm = TN_ROW_TILE
    last = m // tm - 1

    def body(x_ref, dy_ref, o_ref, acc_ref):
        r = pl.program_id(2)

        @pl.when(r == 0)
        def _():
            acc_ref[...] = jnp.zeros_like(acc_ref)

        acc_ref[...] += _dot_tn(x_ref[...], dy_ref[...])

        @pl.when(r == last)
        def _():
            o_ref[...] = acc_ref[...].astype(o_ref.dtype)

    out_specs = pl.BlockSpec((ti, tn), lambda i, j, r: (i, j))
    out_shape = jax.ShapeDtypeStruct((k1, n), out_dtype)
    return pl.pallas_call(
        body, grid=(k1 // ti, n // tn, m // tm),
        in_specs=[pl.BlockSpec((tm, ti), lambda i, j, r: (r, i)), pl.BlockSpec((tm, tn), lambda i, j, r: (r, j))],
        out_specs=out_specs, out_shape=out_shape, scratch_shapes=[pltpu.VMEM((ti, tn), F32)], name=name,
        compiler_params=_params("arbitrary", "arbitrary", "arbitrary"),
    )(x, dy)


FF_BLOCK = D_FF // N_DEV
SQRT_FLOOR = 1.1754944e-38


def _ff_cols(d):
    return slice(FF_BLOCK * d, FF_BLOCK * (d + 1))


def mlp_up(h, norm_g, w, layer, *, bias=None, relu2=True, out_dtype=None, name="mlp_up"):
    m = h.shape[0]
    ns = w.shape[3]
    n = N_DEV * ns
    tm = ROW_TILE
    out_dtype = MXU_DTYPE if relu2 else out_dtype
    has_bias = bias is not None

    def body(*refs):
        h_ref, g_ref, w_ref = refs[0], refs[1], refs[2]
        bias_ref = refs[3] if has_bias else None
        o_ref, u_ref = refs[-2], refs[-1]
        x = h_ref[...]
        u_ = _mx((x * lax.rsqrt(jnp.mean(x * x, -1, keepdims=True) + EPS)) * g_ref[...])
        u_ref[...] = u_
        for d in range(N_DEV):
            cols = slice(ns * d, ns * (d + 1))
            r = _dot(u_, w_ref[d])
            if has_bias:
                r = r + bias_ref[:, cols]
            if relu2:
                r = jnp.maximum(r, 0.0)
                r = r * r
            o_ref[:, cols] = r.astype(o_ref.dtype)

    row = pl.BlockSpec((tm, D_MODEL), lambda i: (i, 0))
    in_specs = [row, pl.BlockSpec((1, D_MODEL), lambda i: (0, 0)),
                pl.BlockSpec((N_DEV, None, D_MODEL, ns), lambda i: (0, layer, 0, 0))]
    args = [h, norm_g, w]
    if has_bias:
        in_specs.append(pl.BlockSpec((1, n), lambda i: (0, 0)))
        args.append(bias)
    return pl.pallas_call(
        body, grid=(m // tm,), in_specs=in_specs, out_specs=[pl.BlockSpec((tm, n), lambda i: (i, 0)), row],
        out_shape=[jax.ShapeDtypeStruct((m, n), out_dtype), jax.ShapeDtypeStruct((m, D_MODEL), MXU_DTYPE)],
        name=name, compiler_params=_params("arbitrary"),
    )(*args)


def mlp_dact(dh, w_down, act, layer, name="mlp_dact"):
    m = dh.shape[0]
    tm = ROW_TILE

    def body(dh_ref, w_ref, act_ref, o_ref):
        dh_ = dh_ref[...]
        for d in range(N_DEV):
            p = act_ref[:, _ff_cols(d)].astype(F32)
            r = p * lax.rsqrt(jnp.maximum(p, SQRT_FLOOR))
            o_ref[:, _ff_cols(d)] = (_dot_nt(dh_, w_ref[d]) * (2.0 * r)).astype(o_ref.dtype)

    return pl.pallas_call(
        body, grid=(m // tm,),
        in_specs=[pl.BlockSpec((tm, D_MODEL), lambda i: (i, 0)),
                  pl.BlockSpec((N_DEV, None, FF_BLOCK, D_MODEL), lambda i: (0, layer, 0, 0)),
                  pl.BlockSpec((tm, D_FF), lambda i: (i, 0))],
        out_specs=pl.BlockSpec((tm, D_FF), lambda i: (i, 0)),
        out_shape=jax.ShapeDtypeStruct((m, D_FF), MXU_DTYPE), name=name, compiler_params=_params("arbitrary"),
    )(dh, w_down, act)


def mlp_du_rms_bwd(da, w_up, layer, h, g, dh_out, name="mlp_du"):
    m, n = da.shape
    ns = w_up.shape[3]
    assert n == N_DEV * ns
    tm = ROW_TILE

    def body(da_ref, w_ref, h_ref, g_ref, dho_ref, dh_ref, dg_ref):
        @pl.when(pl.program_id(0) == 0)
        def _():
            dg_ref[...] = jnp.zeros_like(dg_ref)

        du = _dot_nt(da_ref[:, 0:ns], w_ref[0])
        for d in range(1, N_DEV):
            du = du + _dot_nt(da_ref[:, ns * d:ns * (d + 1)], w_ref[d])
        x = h_ref[...]
        r = lax.rsqrt(jnp.mean(x * x, -1, keepdims=True) + EPS)
        gy = du * g_ref[...]
        dx = r * gy - x * ((r * r * r) * jnp.mean(x * gy, -1, keepdims=True))
        dh_ref[...] = dho_ref[...] + dx
        dg_ref[...] += jnp.sum(du * (x * r), axis=0, keepdims=True)

    row = pl.BlockSpec((tm, D_MODEL), lambda i: (i, 0))
    vec = pl.BlockSpec((1, D_MODEL), lambda i: (0, 0))
    return pl.pallas_call(
        body, grid=(m // tm,),
        in_specs=[pl.BlockSpec((tm, n), lambda i: (i, 0)),
                  pl.BlockSpec((N_DEV, None, D_MODEL, ns), lambda i: (0, layer, 0, 0)), row, vec, row],
        out_specs=[row, vec],
        out_shape=[jax.ShapeDtypeStruct((m, D_MODEL), F32), jax.ShapeDtypeStruct((1, D_MODEL), F32)],
        name=name, compiler_params=_params("arbitrary"),
    )(da, w_up, h, g, dh_out)


def mlp_dw_up(u, da, name="mlp_dw_up"):
    m, n = da.shape
    ns = n // N_DEV
    tm = TN_ROW_TILE
    last = m // tm - 1
    parts = -(-D_MODEL * n * 4 // ACC_BYTES)
    per = N_DEV // parts

    def body(u_ref, da_ref, o_ref, acc_ref):
        r = pl.program_id(1)

        @pl.when(r == 0)
        def _():
            acc_ref[...] = jnp.zeros_like(acc_ref)

        acc_ref[...] += _dot_tn(u_ref[...], da_ref[...])

        @pl.when(r == last)
        def _():
            for d in range(per):
                o_ref[d] = acc_ref[:, ns * d:ns * (d + 1)].astype(o_ref.dtype)

    return pl.pallas_call(
        body, grid=(parts, m // tm),
        in_specs=[pl.BlockSpec((tm, D_MODEL), lambda h, r: (r, 0)), pl.BlockSpec((tm, n // parts), lambda h, r: (r, h))],
        out_specs=pl.BlockSpec((per, D_MODEL, ns), lambda h, r: (h, 0, 0)),
        out_shape=jax.ShapeDtypeStruct((N_DEV, D_MODEL, ns), GRAD_WIRE_DTYPE),
        scratch_shapes=[pltpu.VMEM((D_MODEL, n // parts), F32)], name=name,
        compiler_params=_params("arbitrary", "arbitrary"),
    )(u, da)


def mlp_dw_down(act, dh, name="mlp_dw_down"):
    m = act.shape[0]
    tm = TN_ROW_TILE
    last = m // tm - 1
    parts = D_FF * D_MODEL * 4 // ACC_BYTES
    rows = D_FF // parts

    def body(a_ref, dh_ref, o_ref, acc_ref):
        r = pl.program_id(1)

        @pl.when(r == 0)
        def _():
            acc_ref[...] = jnp.zeros_like(acc_ref)

        acc_ref[...] += _dot_tn(a_ref[...], dh_ref[...])

        @pl.when(r == last)
        def _():
            o_ref[...] = acc_ref[...].astype(o_ref.dtype)

    return pl.pallas_call(
        body, grid=(parts, m // tm),
        in_specs=[pl.BlockSpec((tm, rows), lambda h, r: (r, h)), pl.BlockSpec((tm, D_MODEL), lambda h, r: (r, 0))],
        out_specs=pl.BlockSpec((rows, D_MODEL), lambda h, r: (h, 0)),
        out_shape=jax.ShapeDtypeStruct((D_FF, D_MODEL), GRAD_WIRE_DTYPE),
        scratch_shapes=[pltpu.VMEM((rows, D_MODEL), F32)], name=name,
        compiler_params=_params("arbitrary", "arbitrary"),
    )(act, dh)


_ATT_SCALE = HEAD_DIM ** -0.5


def _alibi_slope(h):
    return 2.0 ** (-8.0 * (h + 1) / ATT_HEADS)


HEADS_PER_KV = ATT_HEADS // ATT_KV_HEADS
STACK = HEADS_PER_KV * BLOCK
N_KEYS = 3 * BLOCK


def _head_select(width):
    c = jnp.arange(width)[:, None]
    h = jnp.arange(LANE)[None, :]
    return (c // HEAD_DIM == h).astype(F32)


def _head_fold(width):
    c = jnp.arange(width)[:, None]
    j = jnp.arange(LANE)[None, :]
    return (c % HEAD_DIM == j).astype(F32)


def _head_rms(x, sel, sel_t):
    r = lax.rsqrt(_dot_hi(x * x, sel) * (1.0 / HEAD_DIM) + EPS)
    return r, _dot_hi(r, sel_t)


def _head_norm_bwd(x, r, r_full, w_t, dy, sel, sel_t):
    gy = dy * w_t
    coef = _dot_hi((r * r * r) * _dot_hi(x * gy, sel) * (1.0 / HEAD_DIM), sel_t)
    return r_full * gy - x * coef, jnp.sum(dy * (x * r_full), axis=0, keepdims=True)


def _low_lanes(rows):
    return lax.broadcasted_iota(jnp.int32, (rows, LANE), 1) < HEAD_DIM


def _dup_half(a, g):
    rolled = pltpu.roll(a, HEAD_DIM, 1)
    low = _low_lanes(a.shape[0])
    return jnp.where(low, a, rolled) if g == 0 else jnp.where(low, rolled, a)


def _stack_heads(x, g):
    low = _low_lanes(BLOCK)
    parts = []
    for pair in range(2):
        p = x[:, 2 * LANE * g + LANE * pair:2 * LANE * g + LANE * (pair + 1)]
        parts += [jnp.where(low, p, 0.0), jnp.where(low, 0.0, p)]
    return jnp.concatenate(parts, axis=0)


def _unstack_heads(groups):
    low = _low_lanes(BLOCK)
    cols = []
    for o in groups:
        for pair in range(2):
            cols.append(jnp.where(low, o[2 * pair * BLOCK:(2 * pair + 1) * BLOCK], o[(2 * pair + 1) * BLOCK:(2 * pair + 2) * BLOCK]))
    return jnp.concatenate(cols, axis=1)


def _fold_halves(a, g):
    s = a + pltpu.roll(a, HEAD_DIM, 1)
    low = _low_lanes(a.shape[0])
    return jnp.where(low if g == 0 else jnp.logical_not(low), s, 0.0)


def _att_bias(b):
    r = lax.broadcasted_iota(jnp.int32, (STACK, N_KEYS), 0) & (BLOCK - 1)
    col = lax.broadcasted_iota(jnp.int32, (STACK, N_KEYS), 1)
    cc = col & (BLOCK - 1)
    is_meta = col < BLOCK
    is_prev = jnp.logical_and(col >= BLOCK, col < 2 * BLOCK)
    q_pos = b * BLOCK + r - FRONT_PAD
    meta_j = cc - FRONT_PAD
    valid_m = jnp.logical_and(cc >= FRONT_PAD, q_pos >= meta_j)
    valid_p = jnp.logical_and(cc > r, b >= 2)
    valid_c = jnp.logical_and(cc <= r, b >= 1)
    is_cur = col >= 2 * BLOCK
    valid = jnp.logical_or(jnp.logical_and(is_meta, valid_m),
                           jnp.logical_or(jnp.logical_and(is_prev, valid_p), jnp.logical_and(is_cur, valid_c)))
    dist = jnp.where(is_meta, jnp.minimum(q_pos - meta_j, BLOCK), jnp.where(is_prev, r - cc + BLOCK, r - cc))
    return valid, dist.astype(F32)


def _per_head_column(values):
    hid = lax.broadcasted_iota(jnp.int32, (STACK, 1), 0) >> 7
    col = jnp.where(hid == 0, values[0], values[1])
    for j in range(2, HEADS_PER_KV):
        col = jnp.where(hid == j, values[j], col)
    return col


def _att_group_probs(qs, kd, valid, dist, g, sk_ref):
    slope = _per_head_column([_alibi_slope(HEADS_PER_KV * g + j) for j in range(HEADS_PER_KV)])
    sink = _per_head_column([sk_ref[HEADS_PER_KV * g + j] for j in range(HEADS_PER_KV)])
    s = jnp.where(valid, _dot_nt(qs, kd) * _ATT_SCALE - slope * dist, -1e30)
    mx = jnp.maximum(jnp.max(s, axis=-1, keepdims=True), sink)
    p = jnp.exp(s - mx)
    p_sink = jnp.exp(sink - mx)
    inv = 1.0 / (jnp.sum(p, axis=-1, keepdims=True) + p_sink)
    return p * inv, p_sink * inv


def attention_fwd(proj, q_w, k_w, sinks, gather=(), name="att_fwd"):
    m = proj.shape[0]
    nb = m // BLOCK
    cq, ck, cv = COL_Q // Q_W, COL_K // KV_W, COL_V // KV_W
    sel_q, sel_k = _head_select(Q_W), _head_select(KV_W)

    def body(q_ref, kc_ref, vc_ref, vp_ref, vm_ref, qw_ref, kw_ref, sk_ref, sq_ref, sqt_ref, skk_ref, skt_ref,
             o_ref, kpn_s, kmn_s):
        b = pl.program_id(0)
        q, kc = q_ref[...], kc_ref[...]
        _, rq = _head_rms(q, sq_ref[...], sqt_ref[...])
        qn = q * rq * qw_ref[...]
        _, rk = _head_rms(kc, skk_ref[...], skt_ref[...])
        kcn = kc * rk * kw_ref[...]

        @pl.when(b == 0)
        def _():
            kmn_s[...] = kcn
            kpn_s[...] = kcn

        kpn, kmn = kpn_s[...], kmn_s[...]
        kpn_s[...] = kcn
        vc, vp, vm = vc_ref[...], vp_ref[...], vm_ref[...]
        valid, dist = _att_bias(b)
        outs = []
        for g in range(ATT_KV_HEADS):
            kd = jnp.concatenate([_dup_half(kmn, g), _dup_half(kpn, g), _dup_half(kcn, g)], axis=0)
            vd = jnp.concatenate([_dup_half(vm, g), _dup_half(vp, g), _dup_half(vc, g)], axis=0)
            probs, _ = _att_group_probs(_stack_heads(qn, g), kd, valid, dist, g, sk_ref)
            outs.append(_dot(probs, vd))
        o_ref[...] = _unstack_heads(outs).astype(o_ref.dtype)

    prev = lambda i: jnp.maximum(i - 1, 0)
    full = lambda s: pl.BlockSpec(s, lambda i: (0,) * len(s))
    body, g_in, g_out, g_shape, g_scratch = _with_gather(body, 12, 1, gather, nb)
    outs = pl.pallas_call(
        body, grid=(nb,),
        in_specs=[pl.BlockSpec((BLOCK, Q_W), lambda i: (i, cq)),
                  pl.BlockSpec((BLOCK, KV_W), lambda i: (i, ck)), pl.BlockSpec((BLOCK, KV_W), lambda i: (i, cv)),
                  pl.BlockSpec((BLOCK, KV_W), lambda i: (prev(i), cv)), pl.BlockSpec((BLOCK, KV_W), lambda i: (0, cv)),
                  full((1, Q_W)), full((1, KV_W)), pl.BlockSpec(memory_space=pltpu.SMEM),
                  full((Q_W, LANE)), full((LANE, Q_W)), full((KV_W, LANE)), full((LANE, KV_W))] + g_in,
        out_specs=[pl.BlockSpec((BLOCK, Q_W), lambda i: (i, 0))] + g_out,
        out_shape=[jax.ShapeDtypeStruct((m, Q_W), MXU_DTYPE)] + g_shape,
        scratch_shapes=[pltpu.VMEM((BLOCK, KV_W), F32), pltpu.VMEM((BLOCK, KV_W), F32)] + g_scratch,
        name=name, compiler_params=_params("arbitrary"),
    )(proj, proj, proj, proj, proj, q_w, k_w, sinks, sel_q, sel_q.T, sel_k, sel_k.T, *gather)
    return outs[0], outs[1:]


def attention_bwd(proj, dmix, q_w, k_w, sinks, carry=(), name="att_bwd"):
    m = proj.shape[0]
    nb = m // BLOCK
    cq, ck, cv = COL_Q // Q_W, COL_K // KV_W, COL_V // KV_W
    c_datt = 0
    sel_q, sel_k = _head_select(Q_W), _head_select(KV_W)
    fold_q, fold_k = _head_fold(Q_W), _head_fold(KV_W)

    def body(do_ref, q_ref, kc_ref, vc_ref, kp_ref, vp_ref, km_ref, vm_ref, qw_ref, kw_ref, sk_ref,
             sq_ref, sqt_ref, skk_ref, skt_ref, fq_ref, fk_ref,
             dq_ref, dk_ref, dv_ref, dqw_ref, dkw_ref, dsk_ref, car_k, car_v, met_k, met_v, kmn_s, qw_acc, kw_acc):
        i = pl.program_id(0)
        b = nb - 1 - i
        sel_q_, sel_qt, sel_k_, sel_kt = sq_ref[...], sqt_ref[...], skk_ref[...], skt_ref[...]
        qw, kw = qw_ref[...], kw_ref[...]

        @pl.when(i == 0)
        def _():
            for r in (car_k, car_v, met_k, met_v, qw_acc, kw_acc, dsk_ref):
                r[...] = jnp.zeros_like(r)
            km = km_ref[...]
            kmn_s[...] = km * _head_rms(km, sel_k_, sel_kt)[1] * kw

        q, kc, kp = q_ref[...], kc_ref[...], kp_ref[...]
        rq, rq_full = _head_rms(q, sel_q_, sel_qt)
        qn = q * rq_full * qw
        rk, rk_full = _head_rms(kc, sel_k_, sel_kt)
        kcn = kc * rk_full * kw
        kpn = kp * _head_rms(kp, sel_k_, sel_kt)[1] * kw
        kmn = kmn_s[...]
        vc, vp, vm = vc_ref[...], vp_ref[...], vm_ref[...]
        do = do_ref[...]
        valid, dist = _att_bias(b)
        lane = lax.broadcasted_iota(jnp.int32, (1, LANE), 1)
        dsk = jnp.zeros((1, LANE), F32)
        dkd_sum = jnp.zeros((N_KEYS, KV_W), F32)
        dvd_sum = jnp.zeros((N_KEYS, KV_W), F32)
        dqd = []
        for g in range(ATT_KV_HEADS):
            kd = jnp.concatenate([_dup_half(kmn, g), _dup_half(kpn, g), _dup_half(kcn, g)], axis=0)
            vd = jnp.concatenate([_dup_half(vm, g), _dup_half(vp, g), _dup_half(vc, g)], axis=0)
            qs = _stack_heads(qn, g)
            dos = _stack_heads(do, g)
            probs, p_sink = _att_group_probs(qs, kd, valid, dist, g, sk_ref)
            o = _dot(probs, vd)
            delta = jnp.sum(dos * o, axis=-1, keepdims=True)
            ds = probs * (_dot_nt(dos, vd) - delta)
            dqd.append(_dot(ds, kd) * _ATT_SCALE)
            dkd_sum = dkd_sum + _fold_halves(_dot_tn(ds, qs) * _ATT_SCALE, g)
            dvd_sum = dvd_sum + _fold_halves(_dot_tn(probs, dos), g)
            sink_grad = p_sink * delta
            for j in range(HEADS_PER_KV):
                part = jnp.sum(sink_grad[BLOCK * j:BLOCK * (j + 1)], axis=0, keepdims=True)
                dsk = dsk - jnp.where(lane == HEADS_PER_KV * g + j, part, 0.0)
        dq, dqw = _head_norm_bwd(q, rq, rq_full, qw, _unstack_heads(dqd), sel_q_, sel_qt)
        dq_ref[...] = dq
        qw_acc[...] += dqw
        dsk_ref[...] += dsk

        met_k[...] += dkd_sum[0:BLOCK]
        met_v[...] += dvd_sum[0:BLOCK]
        first = (b == 0).astype(F32)
        dkn_tot = dkd_sum[2 * BLOCK:3 * BLOCK] + car_k[...] + first * met_k[...]
        dv_ref[...] = dvd_sum[2 * BLOCK:3 * BLOCK] + car_v[...] + first * met_v[...]
        car_k[...] = dkd_sum[BLOCK:2 * BLOCK]
        car_v[...] = dvd_sum[BLOCK:2 * BLOCK]
        dk, dkw = _head_norm_bwd(kc, rk, rk_full, kw, dkn_tot, sel_k_, sel_kt)
        dk_ref[...] = dk
        kw_acc[...] += dkw

        @pl.when(i == nb - 1)
        def _():
            dqw_ref[...] = _dot_hi(jnp.broadcast_to(qw_acc[...], (8, Q_W)), fq_ref[...])[0:1]
            dkw_ref[...] = _dot_hi(jnp.broadcast_to(kw_acc[...], (8, KV_W)), fk_ref[...])[0:1]

    blk = lambda i: nb - 1 - i
    prev = lambda i: jnp.maximum(nb - 2 - i, 0)
    full = lambda s: pl.BlockSpec(s, lambda i: (0,) * len(s))
    kv_scratch = pltpu.VMEM((BLOCK, KV_W), F32)
    body, ex_in, ex_out, ex_shape, ex_scratch = _with_exchange(body, 17, 6, carry, nb)
    outs = pl.pallas_call(
        body, grid=(nb,),
        in_specs=[pl.BlockSpec((BLOCK, Q_W), lambda i: (blk(i), c_datt)),
                  pl.BlockSpec((BLOCK, Q_W), lambda i: (blk(i), cq)),
                  pl.BlockSpec((BLOCK, KV_W), lambda i: (blk(i), ck)), pl.BlockSpec((BLOCK, KV_W), lambda i: (blk(i), cv)),
                  pl.BlockSpec((BLOCK, KV_W), lambda i: (prev(i), ck)), pl.BlockSpec((BLOCK, KV_W), lambda i: (prev(i), cv)),
                  pl.BlockSpec((BLOCK, KV_W), lambda i: (0, ck)), pl.BlockSpec((BLOCK, KV_W), lambda i: (0, cv)),
                  full((1, Q_W)), full((1, KV_W)), pl.BlockSpec(memory_space=pltpu.SMEM),
                  full((Q_W, LANE)), full((LANE, Q_W)), full((KV_W, LANE)), full((LANE, KV_W)),
                  full((Q_W, LANE)), full((KV_W, LANE))] + ex_in,
        out_specs=[pl.BlockSpec((BLOCK, Q_W), lambda i: (blk(i), 0)),
                   pl.BlockSpec((BLOCK, KV_W), lambda i: (blk(i), 0)), pl.BlockSpec((BLOCK, KV_W), lambda i: (blk(i), 0)),
                   full((1, LANE)), full((1, LANE)), full((1, LANE))] + ex_out,
        out_shape=[jax.ShapeDtypeStruct((m, Q_W), F32), jax.ShapeDtypeStruct((m, KV_W), F32),
                   jax.ShapeDtypeStruct((m, KV_W), F32), jax.ShapeDtypeStruct((1, LANE), F32),
                   jax.ShapeDtypeStruct((1, LANE), F32), jax.ShapeDtypeStruct((1, LANE), F32)] + ex_shape,
        scratch_shapes=[kv_scratch, kv_scratch, kv_scratch, kv_scratch, kv_scratch,
                        pltpu.VMEM((1, Q_W), F32), pltpu.VMEM((1, KV_W), F32)] + ex_scratch,
        name=name, compiler_params=_params("arbitrary"),
    )(dmix, proj, proj, proj, proj, proj, proj, proj, q_w, k_w, sinks, sel_q, sel_q.T, sel_k, sel_k.T, fold_q, fold_k,
      *carry)
    return outs[:6], outs[6:]


HALO = 8
GROUP_W = SSM_INNER // SSM_GROUPS
HEADS_PER_GROUP = SSM_HEADS // SSM_GROUPS


def _head_expand():
    h = jnp.arange(LANE)[:, None]
    c = jnp.arange(SSM_INNER)[None, :]
    return (c // SSM_HEAD_DIM == h).astype(F32)


def _softplus(x):
    return jnp.maximum(x, 0.0) + jnp.log1p(jnp.exp(-jnp.abs(x)))


def _ssd_decays(dt, a_log_row):
    row = lax.broadcasted_iota(jnp.int32, (BLOCK, BLOCK), 0)
    col = lax.broadcasted_iota(jnp.int32, (BLOCK, BLOCK), 1)
    lower = row >= col
    a = -jnp.exp(a_log_row)
    a_cs = _dot_hi(lower, dt * a, exact="b")
    return a, a_cs, lower


def _decay_matrix(a_cs, a_cs_t, h, lower):
    diff = a_cs[:, h:h + 1] - a_cs_t[h:h + 1, :]
    return jnp.where(lower, jnp.exp(jnp.where(lower, diff, 0.0)), 0.0)


def _conv_taps(s_ref, w_ref, first, rows):
    acc = w_ref[0:1, :] * s_ref[pl.ds(first, rows), :]
    for j in range(1, SSM_CONV):
        acc = acc + w_ref[j:j + 1, :] * s_ref[pl.ds(first + j, rows), :]
    return acc


def ssd_fwd(proj, att, cw_x, cw_bc, cb_x, cb_bc, dt_bias, a_log, d_exp, norm_w, gather=(), name="ssd_fwd"):
    m = proj.shape[0]
    nb = m // BLOCK
    expand = _head_expand()
    expand_t = expand.T

    def body(z_ref, xs_ref, bc_ref, dtr_ref, att_ref, cwx_ref, cwbc_ref, cbx_ref, cbbc_ref, dtb_ref, alog_ref, dexp_ref,
             nw_ref, e_ref, et_ref, out_ref, prex_ref, prebc_ref, dt_ref, ypre_ref, st_ref, sx, sbc, state):
        c = pl.program_id(0)

        @pl.when(c == 0)
        def _():
            sx[0:HALO, :] = jnp.zeros((HALO, SSM_INNER), F32)
            sbc[0:HALO, :] = jnp.zeros((HALO, 2 * LANE), F32)
            state[...] = jnp.zeros_like(state)

        sx[HALO:HALO + BLOCK, :] = xs_ref[...]
        sbc[HALO:HALO + BLOCK, :] = bc_ref[...]
        first = HALO - (SSM_CONV - 1)
        pre_x = _conv_taps(sx, cwx_ref, first, BLOCK) + cbx_ref[...]
        pre_bc = _conv_taps(sbc, cwbc_ref, first, BLOCK) + cbbc_ref[...]
        sx[0:HALO, :] = xs_ref[BLOCK - HALO:BLOCK, :]
        sbc[0:HALO, :] = bc_ref[BLOCK - HALO:BLOCK, :]
        prex_ref[...] = pre_x
        prebc_ref[...] = pre_bc
        xc = pre_x * _sigmoid(pre_x)
        bcv = pre_bc * _sigmoid(pre_bc)

        rows = _row_ids(c * BLOCK, BLOCK, LANE)
        lanes = lax.broadcasted_iota(jnp.int32, (BLOCK, LANE), 1)
        live = jnp.logical_and(rows >= FRONT_PAD, lanes < SSM_HEADS)
        dt = jnp.where(live, _softplus(dtr_ref[...] + dtb_ref[...]), 0.0)
        dt_ref[...] = dt
        a, a_cs, lower = _ssd_decays(dt, alog_ref[...])
        a_cs_t = a_cs.T
        dt_t = dt.T
        e = e_ref[...]
        es_full = _dot_hi(jnp.exp(a_cs), e)
        wx_full = _dot_hi(jnp.exp(a_cs[BLOCK - 1:BLOCK, :] - a_cs) * dt, e)
        end_col = jnp.exp(a_cs_t[:, BLOCK - 1:BLOCK])
        dec_full = _dot_hi(et_ref[...], jnp.broadcast_to(end_col, (LANE, SSM_STATE)), exact="b")

        st_ref[0] = state[...]
        ys = []
        for g in range(SSM_GROUPS):
            b_g = bcv[:, SSM_STATE * g:SSM_STATE * (g + 1)]
            c_g = bcv[:, LANE + SSM_STATE * g:LANE + SSM_STATE * (g + 1)]
            gs = slice(GROUP_W * g, GROUP_W * (g + 1))
            cb = _dot_nt(c_g, b_g)
            yd = []
            for hh in range(HEADS_PER_GROUP):
                h = g * HEADS_PER_GROUP + hh
                w = cb * _decay_matrix(a_cs, a_cs_t, h, lower) * dt_t[h:h + 1, :]
                yd.append(_dot(w, xc[:, SSM_HEAD_DIM * h:SSM_HEAD_DIM * (h + 1)]))
            h_g = state[gs, :]
            y_off = _dot_nt(c_g, h_g) * es_full[:, gs]
            ys.append(jnp.concatenate(yd, axis=1) + y_off)
            new_state = _dot_tn(xc[:, gs] * wx_full[:, gs], b_g)
            state[gs, :] = h_g * dec_full[gs, :] + new_state
        y_pre = jnp.concatenate(ys, axis=1) + xc * dexp_ref[...]
        ypre_ref[...] = y_pre
        z = z_ref[...]
        gt = y_pre * (z * _sigmoid(z))
        outs = []
        for g in range(SSM_GROUPS):
            gg = gt[:, GROUP_W * g:GROUP_W * (g + 1)]
            r = lax.rsqrt(jnp.mean(gg * gg, -1, keepdims=True) + EPS)
            outs.append(gg * r)
        out_ref[:, 0:Q_W] = att_ref[...]
        out_ref[:, Q_W:MIX_W] = (jnp.concatenate(outs, axis=1) * nw_ref[...]).astype(out_ref.dtype)

    full = lambda s: pl.BlockSpec(s, lambda i: (0,) * len(s))
    rowblk = lambda w, cidx: pl.BlockSpec((BLOCK, w), lambda i: (i, cidx))
    body, g_in, g_out, g_shape, g_scratch = _with_gather(body, 15, 6, gather, nb)
    outs = pl.pallas_call(
        body, grid=(nb,),
        in_specs=[rowblk(SSM_INNER, COL_Z // SSM_INNER), rowblk(SSM_INNER, COL_XS // SSM_INNER),
                  rowblk(2 * LANE, COL_BC // (2 * LANE)), rowblk(LANE, COL_DT // LANE), rowblk(Q_W, 0),
                  full((SSM_CONV, SSM_INNER)), full((SSM_CONV, 2 * LANE)), full((1, SSM_INNER)), full((1, 2 * LANE)),
                  full((1, LANE)), full((1, LANE)), full((1, SSM_INNER)), full((1, SSM_INNER)),
                  full((LANE, SSM_INNER)), full((SSM_INNER, LANE))] + g_in,
        out_specs=[rowblk(MIX_W, 0), rowblk(SSM_INNER, 0), rowblk(2 * LANE, 0), rowblk(LANE, 0),
                   rowblk(SSM_INNER, 0), pl.BlockSpec((1, SSM_INNER, SSM_STATE), lambda i: (i, 0, 0))] + g_out,
        out_shape=[jax.ShapeDtypeStruct((m, MIX_W), MXU_DTYPE), jax.ShapeDtypeStruct((m, SSM_INNER), F32),
                   jax.ShapeDtypeStruct((m, 2 * LANE), F32), jax.ShapeDtypeStruct((m, LANE), F32),
                   jax.ShapeDtypeStruct((m, SSM_INNER), F32),
                   jax.ShapeDtypeStruct((nb, SSM_INNER, SSM_STATE), F32)] + g_shape,
        scratch_shapes=[pltpu.VMEM((HALO + BLOCK, SSM_INNER), F32), pltpu.VMEM((HALO + BLOCK, 2 * LANE), F32),
                        pltpu.VMEM((SSM_INNER, SSM_STATE), F32)] + g_scratch,
        name=name, compiler_params=_params("arbitrary"),
    )(proj, proj, proj, proj, att, cw_x, cw_bc, cb_x, cb_bc, dt_bias, a_log, d_exp, norm_w, expand, expand_t, *gather)
    return outs[:6], outs[6:]


def ssd_bwd(proj, dmix, dq, dk, dv, pre_x, pre_bc, dt, y_pre, states, cw_x, cw_bc, dt_bias, a_log, d_exp, norm_w,
            carry=(), name="ssd_bwd"):
    m = proj.shape[0]
    nb = m // BLOCK
    expand = _head_expand()
    expand_t = expand.T

    def body(do0_ref, do1_ref, z_ref, xs_ref, xsp_ref, bc_ref, bcp_ref, dtr_ref, prex_ref, prebc_ref, dt_ref, ypre_ref,
             st_ref, dq_ref, dk_ref, dv_ref,
             cwx_ref, cwbc_ref, dtb_ref, alog_ref, dexp_ref, nw_ref, e_ref, et_ref,
             dproj_ref, dcwx_ref, dcwbc_ref, dcbx_ref, dcbbc_ref, ddtb_ref, dalog_ref,
             dd_ref, dnw_ref,
             dstate, hnext, tx, tbc, sx, sbc, dlane):

        def put(col, value):
            dproj_ref[:, col:col + value.shape[1]] = value.astype(dproj_ref.dtype)

        put(COL_Q, dq_ref[...])
        put(COL_K, dk_ref[...])
        put(COL_V, dv_ref[...])
        i = pl.program_id(0)
        c = nb - 1 - i

        @pl.when(i == 0)
        def _():
            dstate[...] = jnp.zeros_like(dstate)
            hnext[...] = jnp.zeros_like(hnext)
            tx[BLOCK:BLOCK + HALO, :] = jnp.zeros((HALO, SSM_INNER), F32)
            tbc[BLOCK:BLOCK + HALO, :] = jnp.zeros((HALO, 2 * LANE), F32)
            dlane[...] = jnp.zeros_like(dlane)
            for r in (dcwx_ref, dcwbc_ref, dcbx_ref, dcbbc_ref, ddtb_ref, dalog_ref, dd_ref, dnw_ref):
                r[...] = jnp.zeros_like(r)

        e = e_ref[...]
        et = et_ref[...]
        pre_x = prex_ref[...]
        pre_bc = prebc_ref[...]
        sig_x = _sigmoid(pre_x)
        sig_bc = _sigmoid(pre_bc)
        xc = pre_x * sig_x
        bcv = pre_bc * sig_bc
        dt = dt_ref[...]
        a, a_cs, lower = _ssd_decays(dt, alog_ref[...])
        a_cs_t = a_cs.T
        es_full = _dot_hi(jnp.exp(a_cs), e)
        ed_full = _dot_hi(jnp.exp(a_cs[BLOCK - 1:BLOCK, :] - a_cs), e)
        dt_full = _dot_hi(dt, e)
        end_col = jnp.exp(a_cs_t[:, BLOCK - 1:BLOCK])
        dec_full = _dot_hi(et, jnp.broadcast_to(end_col, (LANE, SSM_STATE)), exact="b")
        dexp = dexp_ref[...]

        z = z_ref[...]
        zs = _sigmoid(z)
        sz = z * zs
        y_pre = ypre_ref[...]
        gt = y_pre * sz
        do = jnp.concatenate([do0_ref[...], do1_ref[...]], axis=1)
        nw = nw_ref[...]
        dgt = []
        dnw = []
        for g in range(SSM_GROUPS):
            gs = slice(GROUP_W * g, GROUP_W * (g + 1))
            gg = gt[:, gs]
            r = lax.rsqrt(jnp.mean(gg * gg, -1, keepdims=True) + EPS)
            gn = do[:, gs] * nw[:, gs]
            dgt.append(r * gn - gg * ((r * r * r) * jnp.mean(gg * gn, -1, keepdims=True)))
            dnw.append(jnp.sum(do[:, gs] * (gg * r), axis=0, keepdims=True))
        dgt = jnp.concatenate(dgt, axis=1)
        dnw_ref[...] += jnp.concatenate(dnw, axis=1)
        dy = dgt * sz
        put(COL_Z, dgt * y_pre * (zs * (1.0 + z * (1.0 - zs))))
        dlane[...] += jnp.sum(dy * xc, axis=0, keepdims=True)
        xd = xc * dt_full

        lane_id = lax.broadcasted_iota(jnp.int32, (BLOCK, LANE), 1)
        sub_id = lax.broadcasted_iota(jnp.int32, (LANE, BLOCK), 0)
        ds_to = jnp.zeros((BLOCK, LANE), F32)
        ds_from_t = jnp.zeros((LANE, BLOCK), F32)
        dxd_parts, inter_parts = [], []
        dbs, dcs = [], []
        for g in range(SSM_GROUPS):
            gs = slice(GROUP_W * g, GROUP_W * (g + 1))
            b_g = bcv[:, SSM_STATE * g:SSM_STATE * (g + 1)]
            c_g = bcv[:, LANE + SSM_STATE * g:LANE + SSM_STATE * (g + 1)]
            cb = _dot_nt(c_g, b_g)
            dcb = jnp.zeros((BLOCK, BLOCK), F32)
            dxd_h = []
            for hh in range(HEADS_PER_GROUP):
                h = g * HEADS_PER_GROUP + hh
                hs = slice(SSM_HEAD_DIM * h, SSM_HEAD_DIM * (h + 1))
                lm = _decay_matrix(a_cs, a_cs_t, h, lower)
                dy_h = dy[:, hs]
                gl = _dot_nt(dy_h, xd[:, hs]) * lm
                dcb = dcb + gl
                e_h = gl * cb
                ds_to = ds_to + jnp.where(lane_id == h, jnp.sum(e_h, axis=-1, keepdims=True), 0.0)
                ds_from_t = ds_from_t + jnp.where(sub_id == h, jnp.sum(e_h, axis=0, keepdims=True), 0.0)
                dxd_h.append(_dot_tn(cb * lm, dy_h))
            h_g = st_ref[0, gs, :]
            dh_g = dstate[gs, :]
            dys_g = dy[:, gs] * es_full[:, gs]
            xde_g = xd[:, gs] * ed_full[:, gs]
            dcs.append(_dot(dcb, b_g) + _dot(dys_g, h_g))
            dbs.append(_dot_tn(dcb, c_g) + _dot(xde_g, dh_g))
            y_off = _dot_nt(c_g, h_g) * es_full[:, gs]
            dxd_state = _dot_nt(b_g, dh_g) * ed_full[:, gs]
            inter_parts.append(dy[:, gs] * y_off - xd[:, gs] * dxd_state)
            dxd_parts.append(jnp.concatenate(dxd_h, axis=1) + dxd_state)
            dstate[gs, :] = dh_g * dec_full[gs, :] + _dot_tn(dys_g, c_g)
            if g == 0:
                end_dot = hnext[gs, :] * dh_g
            else:
                end_dot = jnp.concatenate([end_dot, hnext[gs, :] * dh_g], axis=0)
        dxd = jnp.concatenate(dxd_parts, axis=1)
        hnext[...] = st_ref[0]

        ds = ds_to - ds_from_t.T + _dot_hi(jnp.concatenate(inter_parts, axis=1), et)
        ds_end = jnp.sum(_dot_tn_hi(end_dot, et), axis=0, keepdims=True)
        rows_l = lax.broadcasted_iota(jnp.int32, (BLOCK, LANE), 0)
        ds = ds + jnp.where(rows_l == BLOCK - 1, ds_end, 0.0)
        row = lax.broadcasted_iota(jnp.int32, (BLOCK, BLOCK), 0)
        col = lax.broadcasted_iota(jnp.int32, (BLOCK, BLOCK), 1)
        dadt = _dot_hi(col >= row, ds, exact="b")
        ddt = dadt * a + _dot_hi(dxd * xc, et)
        dalog_ref[...] += jnp.sum(dadt * dt, axis=0, keepdims=True) * a
        rows = _row_ids(c * BLOCK, BLOCK, LANE)
        lanes = lax.broadcasted_iota(jnp.int32, (BLOCK, LANE), 1)
        live = jnp.logical_and(rows >= FRONT_PAD, lanes < SSM_HEADS)
        ddt_raw = jnp.where(live, ddt * _sigmoid(dtr_ref[...] + dtb_ref[...]), 0.0)
        put(COL_DT, ddt_raw)
        ddtb_ref[...] += jnp.sum(ddt_raw, axis=0, keepdims=True)

        dxc = dxd * dt_full + dy * dexp
        dpre_x = dxc * (sig_x * (1.0 + pre_x * (1.0 - sig_x)))
        dpre_bc = jnp.concatenate(dbs + dcs, axis=1) * (sig_bc * (1.0 + pre_bc * (1.0 - sig_bc)))
        dcbx_ref[...] += jnp.sum(dpre_x, axis=0, keepdims=True)
        dcbbc_ref[...] += jnp.sum(dpre_bc, axis=0, keepdims=True)
        keep_x = _row_ids(c * BLOCK, BLOCK, SSM_INNER) >= FRONT_PAD
        keep_bc = _row_ids(c * BLOCK, BLOCK, 2 * LANE) >= FRONT_PAD
        prev_live = (c > 0).astype(F32)
        for (dpre, t_ref, s_ref, cur_ref, prv_ref, w_ref, dw_ref, col, keep) in (
                (dpre_x, tx, sx, xs_ref, xsp_ref, cwx_ref, dcwx_ref, COL_XS, keep_x),
                (dpre_bc, tbc, sbc, bc_ref, bcp_ref, cwbc_ref, dcwbc_ref, COL_BC, keep_bc)):
            t_ref[0:BLOCK, :] = dpre
            acc = w_ref[0:1, :] * t_ref[pl.ds(SSM_CONV - 1, BLOCK), :]
            for j in range(1, SSM_CONV):
                acc = acc + w_ref[j:j + 1, :] * t_ref[pl.ds(SSM_CONV - 1 - j, BLOCK), :]
            put(col, jnp.where(keep, acc, 0.0))
            t_ref[BLOCK:BLOCK + HALO, :] = dpre[0:HALO, :]
            s_ref[0:HALO, :] = prv_ref[BLOCK - HALO:BLOCK, :] * prev_live
            s_ref[HALO:HALO + BLOCK, :] = cur_ref[...]
            first = HALO - (SSM_CONV - 1)
            for j in range(SSM_CONV):
                dw_ref[j:j + 1, :] += jnp.sum(dpre * s_ref[pl.ds(first + j, BLOCK), :], axis=0, keepdims=True)

        @pl.when(i == nb - 1)
        def _():
            dd_ref[...] = _dot_hi(jnp.broadcast_to(dlane[...], (HALO, SSM_INNER)), et)[0:1, :]

    blk = lambda i: nb - 1 - i
    prv = lambda i: jnp.maximum(nb - 2 - i, 0)
    full = lambda s: pl.BlockSpec(s, lambda i: (0,) * len(s))
    rowblk = lambda w, cidx: pl.BlockSpec((BLOCK, w), lambda i: (blk(i), cidx))
    prvblk = lambda w, cidx: pl.BlockSpec((BLOCK, w), lambda i: (prv(i), cidx))
    body, ex_in, ex_out, ex_shape, ex_scratch = _with_exchange(body, 24, 9, carry, nb)
    outs = pl.pallas_call(
        body, grid=(nb,),
        in_specs=[rowblk(GROUP_W, Q_W // GROUP_W), rowblk(GROUP_W, Q_W // GROUP_W + 1),
                  rowblk(SSM_INNER, COL_Z // SSM_INNER),
                  rowblk(SSM_INNER, COL_XS // SSM_INNER), prvblk(SSM_INNER, COL_XS // SSM_INNER),
                  rowblk(2 * LANE, COL_BC // (2 * LANE)), prvblk(2 * LANE, COL_BC // (2 * LANE)),
                  rowblk(LANE, COL_DT // LANE),
                  rowblk(SSM_INNER, 0), rowblk(2 * LANE, 0), rowblk(LANE, 0), rowblk(SSM_INNER, 0),
                  pl.BlockSpec((1, SSM_INNER, SSM_STATE), lambda i: (blk(i), 0, 0)),
                  rowblk(Q_W, 0), rowblk(KV_W, 0), rowblk(KV_W, 0),
                  full((SSM_CONV, SSM_INNER)), full((SSM_CONV, 2 * LANE)), full((1, LANE)), full((1, LANE)),
                  full((1, SSM_INNER)), full((1, SSM_INNER)), full((LANE, SSM_INNER)), full((SSM_INNER, LANE))] + ex_in,
        out_specs=[rowblk(PROJ_W, 0),
                   full((SSM_CONV, SSM_INNER)), full((SSM_CONV, 2 * LANE)), full((1, SSM_INNER)), full((1, 2 * LANE)),
                   full((1, LANE)), full((1, LANE)), full((1, LANE)), full((1, SSM_INNER))] + ex_out,
        out_shape=[jax.ShapeDtypeStruct((m, PROJ_W), MXU_DTYPE),
                   jax.ShapeDtypeStruct((SSM_CONV, SSM_INNER), F32), jax.ShapeDtypeStruct((SSM_CONV, 2 * LANE), F32),
                   jax.ShapeDtypeStruct((1, SSM_INNER), F32), jax.ShapeDtypeStruct((1, 2 * LANE), F32),
                   jax.ShapeDtypeStruct((1, LANE), F32), jax.ShapeDtypeStruct((1, LANE), F32),
                   jax.ShapeDtypeStruct((1, LANE), F32), jax.ShapeDtypeStruct((1, SSM_INNER), F32)] + ex_shape,
        scratch_shapes=[pltpu.VMEM((SSM_INNER, SSM_STATE), F32), pltpu.VMEM((SSM_INNER, SSM_STATE), F32),
                        pltpu.VMEM((BLOCK + HALO, SSM_INNER), F32), pltpu.VMEM((BLOCK + HALO, 2 * LANE), F32),
                        pltpu.VMEM((HALO + BLOCK, SSM_INNER), F32), pltpu.VMEM((HALO + BLOCK, 2 * LANE), F32),
                        pltpu.VMEM((1, SSM_INNER), F32)] + ex_scratch,
        name=name, compiler_params=_params("arbitrary"),
    )(dmix, dmix, proj, proj, proj, proj, proj, proj, pre_x, pre_bc, dt, y_pre, states, dq, dk, dv,
      cw_x, cw_bc, dt_bias, a_log, d_exp, norm_w, expand, expand_t, *carry)
    return outs[:9], outs[9:]


CONF_HALO = 32
SUBLANES = 8


def _for_each_window(s, offsets, rows, fn):
    total = s.shape[0]
    assert max(offsets) + rows <= total
    for b in range(SUBLANES):
        offs = [o for o in offsets if o % SUBLANES == b]
        if not offs:
            continue
        rot = s if b == 0 else pltpu.roll(s, total - b, 0)
        for o in offs:
            fn(o, rot[o - b:o - b + rows])


def _glu_masked(v, first_row):
    a = v[:, :D_MODEL]
    s = _sigmoid(v[:, D_MODEL:])
    rows = _row_ids(first_row, v.shape[0], D_MODEL)
    return jnp.where(rows >= FRONT_PAD, a * s, 0.0), a, s


def _layer_norm_stats(c):
    mu = jnp.mean(c, -1, keepdims=True)
    xc = c - mu
    rstd = lax.rsqrt(jnp.mean(xc * xc, -1, keepdims=True) + LN_EPS)
    return xc * rstd, rstd


def conformer_mid_fwd(v, dw_w, dw_b, ln_g, ln_b, gather=(), name="conf_mid_fwd"):
    m = v.shape[0]
    nb = m // BLOCK
    kpad = dw_w.shape[0]

    def body(vc_ref, vp_ref, w_ref, b_ref, g_ref, beta_ref, c_ref, s_ref):
        i = pl.program_id(0)
        g_prev, _, _ = _glu_masked(vp_ref[BLOCK - CONF_HALO:BLOCK, :], (i - 1) * BLOCK + BLOCK - CONF_HALO)
        g_cur, _, _ = _glu_masked(vc_ref[...], i * BLOCK)
        sg = jnp.concatenate([g_prev * (i > 0).astype(F32), g_cur], axis=0)
        first = CONF_HALO - (CONF_KERNEL - 1)
        acc = [jnp.broadcast_to(b_ref[...], (BLOCK, D_MODEL))]

        def tap(off, win):
            j = off - first
            acc[0] = acc[0] + w_ref[j:j + 1, :] * win

        _for_each_window(sg, [first + j for j in range(CONF_KERNEL)], BLOCK, tap)
        acc = acc[0]
        c_ref[...] = acc
        xhat, _ = _layer_norm_stats(acc)
        nrm = xhat * g_ref[...] + beta_ref[...]
        s_ref[...] = (nrm * _sigmoid(nrm)).astype(s_ref.dtype)

    full = lambda s: pl.BlockSpec(s, lambda i: (0,) * len(s))
    body, g_in, g_out, g_shape, g_scratch = _with_gather(body, 6, 2, gather, nb)
    outs = pl.pallas_call(
        body, grid=(nb,),
        in_specs=[pl.BlockSpec((BLOCK, 2 * D_MODEL), lambda i: (i, 0)),
                  pl.BlockSpec((BLOCK, 2 * D_MODEL), lambda i: (jnp.maximum(i - 1, 0), 0)),
                  full((kpad, D_MODEL)), full((1, D_MODEL)), full((1, D_MODEL)), full((1, D_MODEL))] + g_in,
        out_specs=[pl.BlockSpec((BLOCK, D_MODEL), lambda i: (i, 0)),
                   pl.BlockSpec((BLOCK, D_MODEL), lambda i: (i, 0))] + g_out,
        out_shape=[jax.ShapeDtypeStruct((m, D_MODEL), F32), jax.ShapeDtypeStruct((m, D_MODEL), MXU_DTYPE)] + g_shape,
        scratch_shapes=g_scratch, name=name, compiler_params=_params("arbitrary"),
    )(v, v, dw_w, dw_b, ln_g, ln_b, *gather)
    return outs[:2], outs[2:]


def conformer_ln_bwd(ds, c, ln_g, ln_b, name="conf_ln_bwd"):
    m, d = c.shape
    tm = ROW_TILE

    def body(ds_ref, c_ref, g_ref, beta_ref, dc_ref, dg_ref, db_ref):
        @pl.when(pl.program_id(0) == 0)
        def _():
            dg_ref[...] = jnp.zeros_like(dg_ref)
            db_ref[...] = jnp.zeros_like(db_ref)

        xhat, rstd = _layer_norm_stats(c_ref[...])
        g = g_ref[...]
        nrm = xhat * g + beta_ref[...]
        sg = _sigmoid(nrm)
        dn = ds_ref[...] * (sg * (1.0 + nrm * (1.0 - sg)))
        db_ref[...] += jnp.sum(dn, axis=0, keepdims=True)
        dg_ref[...] += jnp.sum(dn * xhat, axis=0, keepdims=True)
        dx = dn * g
        dc_ref[...] = rstd * (dx - jnp.mean(dx, -1, keepdims=True) - xhat * jnp.mean(dx * xhat, -1, keepdims=True))

    row = pl.BlockSpec((tm, d), lambda i: (i, 0))
    vec = pl.BlockSpec((1, d), lambda i: (0, 0))
    return pl.pallas_call(
        body, grid=(m // tm,), in_specs=[row, row, vec, vec], out_specs=[row, vec, vec],
        out_shape=[jax.ShapeDtypeStruct((m, d), F32), jax.ShapeDtypeStruct((1, d), F32), jax.ShapeDtypeStruct((1, d), F32)],
        name=name, compiler_params=_params("arbitrary"),
    )(ds, c, ln_g, ln_b)


def conformer_conv_bwd(dc, v, dw_w, carry=(), name="conf_conv_bwd"):
    m = v.shape[0]
    nb = m // BLOCK
    kpad = dw_w.shape[0]

    def body(dcc_ref, dcn_ref, vc_ref, vp_ref, w_ref, dv_ref, dw_ref, db_ref, dvb_ref):
        i = pl.program_id(0)

        @pl.when(i == 0)
        def _():
            dw_ref[...] = jnp.zeros_like(dw_ref)
            db_ref[...] = jnp.zeros_like(db_ref)
            dvb_ref[...] = jnp.zeros_like(dvb_ref)

        dc_cur = dcc_ref[...]
        tg = jnp.concatenate([dc_cur, dcn_ref[0:CONF_HALO, :] * (i < nb - 1).astype(F32)], axis=0)
        g_prev, _, _ = _glu_masked(vp_ref[BLOCK - CONF_HALO:BLOCK, :], (i - 1) * BLOCK + BLOCK - CONF_HALO)
        g_cur, a, s = _glu_masked(vc_ref[...], i * BLOCK)
        sg = jnp.concatenate([g_prev * (i > 0).astype(F32), g_cur], axis=0)
        db_ref[...] += jnp.sum(dc_cur, axis=0, keepdims=True)
        first = CONF_HALO - (CONF_KERNEL - 1)
        dg_acc = [jnp.zeros((BLOCK, D_MODEL), F32)]

        def tap_dg(off, win):
            j = CONF_KERNEL - 1 - off
            dg_acc[0] = dg_acc[0] + w_ref[j:j + 1, :] * win

        def tap_dw(off, win):
            j = off - first
            dw_ref[j:j + 1, :] += jnp.sum(dc_cur * win, axis=0, keepdims=True)

        _for_each_window(tg, list(range(CONF_KERNEL)), BLOCK, tap_dg)
        _for_each_window(sg, [first + j for j in range(CONF_KERNEL)], BLOCK, tap_dw)
        dg = dg_acc[0]
        rows = _row_ids(i * BLOCK, BLOCK, D_MODEL)
        dg = jnp.where(rows >= FRONT_PAD, dg, 0.0)
        da = dg * s
        dbv = dg * a * (s * (1.0 - s))
        dv = jnp.concatenate([da, dbv], axis=1)
        dv_ref[...] = dv.astype(dv_ref.dtype)
        dvb_ref[...] += jnp.sum(dv, axis=0, keepdims=True)

    full = lambda s: pl.BlockSpec(s, lambda i: (0,) * len(s))
    body, ex_in, ex_out, ex_shape, ex_scratch = _with_exchange(body, 5, 4, carry, nb)
    outs = pl.pallas_call(
        body, grid=(nb,),
        in_specs=[pl.BlockSpec((BLOCK, D_MODEL), lambda i: (i, 0)),
                  pl.BlockSpec((BLOCK, D_MODEL), lambda i: (jnp.minimum(i + 1, nb - 1), 0)),
                  pl.BlockSpec((BLOCK, 2 * D_MODEL), lambda i: (i, 0)),
                  pl.BlockSpec((BLOCK, 2 * D_MODEL), lambda i: (jnp.maximum(i - 1, 0), 0)),
                  full((kpad, D_MODEL))] + ex_in,
        out_specs=[pl.BlockSpec((BLOCK, 2 * D_MODEL), lambda i: (i, 0)), full((kpad, D_MODEL)),
                   full((1, D_MODEL)), full((1, 2 * D_MODEL))] + ex_out,
        out_shape=[jax.ShapeDtypeStruct((m, 2 * D_MODEL), MXU_DTYPE), jax.ShapeDtypeStruct((kpad, D_MODEL), F32),
                   jax.ShapeDtypeStruct((1, D_MODEL), F32), jax.ShapeDtypeStruct((1, 2 * D_MODEL), F32)] + ex_shape,
        scratch_shapes=ex_scratch, name=name, compiler_params=_params("arbitrary"),
    )(dc, dc, v, v, dw_w, *carry)
    return outs[:4], outs[4:]


def _row(v, width=None):
    v = v.reshape(1, -1).astype(F32)
    if width is not None and v.shape[1] < width:
        v = jnp.pad(v, ((0, 0), (0, width - v.shape[1])))
    return v


def _w_in_to_kernel(w):
    pad = jnp.zeros((w.shape[0], PROJ_W - COL_DT - SSM_HEADS), w.dtype)
    return jnp.concatenate([w[:, 768:1792], w[:, 1792:2816], w[:, 0:512], w[:, 2816:3072], w[:, 512:640],
                            w[:, 640:768], w[:, 3072:3088], pad], axis=1)


def _w_in_from_kernel(g):
    return jnp.concatenate([g[:, COL_Q:COL_Q + Q_W], g[:, COL_K:COL_K + KV_W], g[:, COL_V:COL_V + KV_W],
                            g[:, COL_Z:COL_Z + SSM_INNER], g[:, COL_XS:COL_XS + SSM_INNER],
                            g[:, COL_BC:COL_BC + 2 * LANE], g[:, COL_DT:COL_DT + SSM_HEADS]], axis=1)


def even_fwd(h, p, gather_att=(), gather_ssd=()):
    u = rms_fwd(h, p["norm"])
    proj = matmul(u, p["w_in"], name="mm_proj")
    att, got_att = attention_fwd(proj, p["q_norm"], p["k_norm"], p["sinks"], gather=list(gather_att))
    if p["w_out"] is None:
        p["w_out"] = got_att[0]
    (mix, pre_x, pre_bc, dt, y_pre, states), got_ssd = ssd_fwd(
        proj, att, p["cw_x"], p["cw_bc"], p["cb_x"], p["cb_bc"], p["dt_bias"], p["a_log"], p["d_exp"], p["ssm_norm"],
        gather=list(gather_ssd))
    out = matmul(mix, p["w_out"], b_kind="rowshard", layer=p["layer"], epilogue="resid", extra=h, name="mm_mix_out")
    return out, (h, u, proj, mix, pre_x, pre_bc, dt, y_pre, states), got_att, got_ssd


def even_bwd(dh, p, saved, carry_att=(), carry_ssd=(), send_w_in=False):
    h, u, proj, mix, pre_x, pre_bc, dt, y_pre, states = saved
    dmix = matmul(dh, p["w_out"], b_kind="rowshard", layer=p["layer"], trans_b=True, name="mm_dmix")
    dw_out = matmul_tn(mix, dh, ti=512, tn=D_MODEL, out_dtype=GRAD_WIRE_DTYPE, name="mm_dw_out")
    dw_out = dw_out.reshape(N_DEV, MIX_W // N_DEV, D_MODEL)
    (dq, dk, dv, dqw, dkw, dsk), got_att = attention_bwd(proj, dmix, p["q_norm"], p["k_norm"], p["sinks"],
                                                         carry=list(carry_att))
    (dproj, dcwx, dcwbc, dcbx, dcbbc, ddtb, dalog, dd, dnw), got_ssd = ssd_bwd(
        proj, dmix, dq, dk, dv, pre_x, pre_bc, dt, y_pre, states, p["cw_x"], p["cw_bc"], p["dt_bias"], p["a_log"],
        p["d_exp"], p["ssm_norm"], carry=[dw_out] + list(carry_ssd))
    dw_in = matmul_tn(u, dproj, ti=512, tn=PROJ_W, name="mm_dw_in")
    dw_in = _to_shards(_w_in_from_kernel(dw_in), 1).astype(GRAD_WIRE_DTYPE)
    if send_w_in:
        du, (dw_in,) = matmul(dproj, p["w_in"], trans_b=True, carry=[dw_in], name="mm_du_in")
    else:
        du = matmul(dproj, p["w_in"], trans_b=True, name="mm_du_in")
    dh_in, dg = rms_bwd(h, p["norm"], du, dh)
    grads = dict(norm=dg, w_in=dw_in, cw_x=dcwx, cw_bc=dcwbc, cb_x=dcbx, cb_bc=dcbbc, dt_bias=ddtb,
                 a_log=dalog, d_skip=dd, ssm_norm=dnw, q_norm=dqw, k_norm=dkw, sinks=dsk)
    return dh_in, grads, got_att, got_ssd


def conf_fwd(h, p, gather=()):
    v, u = mlp_up(h, p["norm"], p["pw1_w"], p["layer"], bias=p["pw1_b"], relu2=False, out_dtype=F32, name="mm_pw1")
    (c, s), got = conformer_mid_fwd(v, p["dw_w"], p["dw_b"], p["ln_g"], p["ln_b"], gather=list(gather))
    out = matmul(s, p["pw2_w"], b_kind="rowshard", layer=p["layer"], bias=p["pw2_b"], epilogue="resid", extra=h,
                 name="mm_pw2")
    return out, (h, u, v, c, s), got


def conf_bwd(dh, p, saved, carry=()):
    h, u, v, c, s = saved
    dpw2_b = col_sum(dh)
    ds = matmul(dh, p["pw2_w"], b_kind="rowshard", layer=p["layer"], trans_b=True, name="mm_ds")
    dpw2_w = matmul_tn(s, dh, ti=D_MODEL, tn=D_MODEL, out_dtype=GRAD_WIRE_DTYPE, name="mm_dpw2")
    dpw2_w = dpw2_w.reshape(N_DEV, D_MODEL // N_DEV, D_MODEL)
    dc, dln_g, dln_b = conformer_ln_bwd(ds, c, p["ln_g"], p["ln_b"])
    (dv, ddw_w, ddw_b, dpw1_b), got = conformer_conv_bwd(dc, v, p["dw_w"], carry=[dpw2_w] + list(carry))
    dpw1_w = mlp_dw_up(u, dv, name="mm_dpw1")
    dh_in, dg = mlp_du_rms_bwd(dv, p["pw1_w"], p["layer"], h, p["norm"], dh, name="mm_du_pw1")
    grads = dict(norm=dg, pw1_w=dpw1_w, pw1_b=dpw1_b, dw_w=ddw_w, dw_b=ddw_b, ln_g=dln_g, ln_b=dln_b, pw2_b=dpw2_b)
    return dh_in, grads, got


def mlp_fwd(h, p):
    act, u = mlp_up(h, p["norm"], p["w_up"], p["layer"])
    out = matmul(act, p["w_down"], b_kind="rowshard", layer=p["layer"], epilogue="resid", extra=h, name="mm_down")
    return out, (h, u, act)


def mlp_bwd(dh, p, saved):
    h, u, act = saved
    da = mlp_dact(dh, p["w_down"], act, p["layer"])
    dw_down = mlp_dw_down(act, dh).reshape(N_DEV, FF_BLOCK, D_MODEL)
    dw_up = mlp_dw_up(u, da)
    dh_in, dg = mlp_du_rms_bwd(da, p["w_up"], p["layer"], h, p["norm"], dh)
    return dh_in, dict(norm=dg, w_up=dw_up, w_down=dw_down)


def local_step(x, target, w, shards, first):
    n_even, n_odd = (DEPTH + 1) // 2, DEPTH // 2
    h = jnp.concatenate([jnp.zeros((FRONT_PAD, D_MODEL), F32), w["meta_tokens"].astype(F32), x], axis=0)
    even_p, odd_p, mlp_p = [None] * n_even, [None] * n_odd, [None] * DEPTH

    def even_params(i, g):
        cw = w["ssm_conv_w"][i]
        return dict(
            layer=0, norm=_row(w["mix_norm_even"][i]), w_in=_w_in_to_kernel(_from_shards(g[0][:, 0], 1)), w_out=g[1],
            cw_x=cw[:, :SSM_INNER], cw_bc=cw[:, SSM_INNER:], cb_x=_row(w["ssm_conv_b"][i][:SSM_INNER]),
            cb_bc=_row(w["ssm_conv_b"][i][SSM_INNER:]), dt_bias=_row(w["dt_bias"][i], LANE),
            a_log=_row(w["a_log"][i], LANE), d_exp=_row(jnp.repeat(w["d_skip"][i], SSM_HEAD_DIM)),
            ssm_norm=_row(w["ssm_norm_w"][i]), q_norm=_row(jnp.tile(w["q_norm"][i], ATT_HEADS)),
            k_norm=_row(jnp.tile(w["k_norm"][i], ATT_KV_HEADS)), sinks=w["sinks"][i].astype(F32))

    def odd_params(i, g):
        return dict(
            layer=0, norm=_row(w["mix_norm_odd"][i]), pw1_w=g[0], pw1_b=_row(w["pw1_b"][i]),
            dw_w=jnp.pad(w["dw_w"][i], ((0, CONF_HALO - CONF_KERNEL), (0, 0))), dw_b=_row(w["dw_b"][i]),
            ln_g=_row(w["ln_g"][i]), ln_b=_row(w["ln_b"][i]), pw2_w=g[1], pw2_b=_row(w["pw2_b"][i]))

    gathered = {}
    tape = []
    for layer in range(DEPTH):
        i = layer // 2
        nxt = shards[layer + 1] if layer + 1 < DEPTH and layer + 1 not in gathered else ()
        if layer == 0:
            even_p[0] = even_params(0, [first[0], None])
            h, saved, got_att, gathered[1] = even_fwd(h, even_p[0], gather_att=shards[0][1:], gather_ssd=nxt)
            mlp_w = got_att[1:]
        else:
            g = gathered.pop(layer)
            mlp_w = g[2:]
            if layer % 2 == 0:
                even_p[i] = even_params(i, g[:2])
                h, saved, got, _ = even_fwd(h, even_p[i], gather_att=nxt)
            else:
                odd_p[i] = odd_params(i, g[:2])
                h, saved, got = conf_fwd(h, odd_p[i], gather=nxt)
            if nxt:
                gathered[layer + 1] = got
        tape.append(saved)
        mlp_p[layer] = dict(layer=0, norm=_row(w["mlp_norm"][layer]), w_up=mlp_w[0], w_down=mlp_w[1])
        h, saved = mlp_fwd(h, mlp_p[layer])
        tape.append(saved)
    dh, loss_row = loss_fwd_bwd(h, target)

    ge = [None] * n_even
    go = [None] * n_odd
    gm = [None] * DEPTH
    received = {n: [None] * shape[0] for n, shape, _ in PARAMS if n in MATMUL_WEIGHTS}
    pending = []

    def store(tags, arrays):
        for (n, l), a in zip(tags, arrays):
            received[n][l] = a

    for layer in reversed(range(DEPTH)):
        i = layer // 2
        dh, gm[layer] = mlp_bwd(dh, mlp_p[layer], tape.pop())
        mlp_tags = [("w_up", layer), ("w_down", layer)]
        mlp_parts = [gm[layer]["w_up"], gm[layer]["w_down"]]
        if layer % 2 == 0:
            riders, pending = pending, []
            dh, ge[i], got_att, got_ssd = even_bwd(dh, even_p[i], tape.pop(), carry_att=mlp_parts,
                                                   carry_ssd=[a for _, _, a in riders], send_w_in=layer == 0)
            store(mlp_tags, got_att)
            store([("w_out", i)] + [(n, l) for n, l, _ in riders], got_ssd)
            if layer == 0:
                store([("w_in", i)], [ge[i]["w_in"]])
            else:
                pending.append(("w_in", i, ge[i]["w_in"]))
        else:
            dh, go[i], got = conf_bwd(dh, odd_p[i], tape.pop(), carry=mlp_parts)
            store([("pw2_w", i)] + mlp_tags, got)
            pending.append(("pw1_w", i, go[i]["pw1_w"]))

    stack = lambda gs, f: jnp.stack([f(g) for g in gs])
    grads = dict(
        meta_tokens=dh[FRONT_PAD:BLOCK],
        mix_norm_even=stack(ge, lambda g: g["norm"][0]),
        ssm_conv_w=stack(ge, lambda g: jnp.concatenate([g["cw_x"], g["cw_bc"]], axis=1)),
        ssm_conv_b=stack(ge, lambda g: jnp.concatenate([g["cb_x"][0], g["cb_bc"][0]])),
        dt_bias=stack(ge, lambda g: g["dt_bias"][0, :SSM_HEADS]),
        a_log=stack(ge, lambda g: g["a_log"][0, :SSM_HEADS]),
        d_skip=stack(ge, lambda g: g["d_skip"][0, :SSM_HEADS]),
        ssm_norm_w=stack(ge, lambda g: g["ssm_norm"][0]),
        q_norm=stack(ge, lambda g: g["q_norm"][0, :HEAD_DIM]),
        k_norm=stack(ge, lambda g: g["k_norm"][0, :HEAD_DIM]),
        sinks=stack(ge, lambda g: g["sinks"][0, :ATT_HEADS]),
        mix_norm_odd=stack(go, lambda g: g["norm"][0]),
        pw1_b=stack(go, lambda g: g["pw1_b"][0]),
        dw_w=stack(go, lambda g: g["dw_w"][:CONF_KERNEL]),
        dw_b=stack(go, lambda g: g["dw_b"][0]),
        ln_g=stack(go, lambda g: g["ln_g"][0]),
        ln_b=stack(go, lambda g: g["ln_b"][0]),
        pw2_b=stack(go, lambda g: g["pw2_b"][0]),
        mlp_norm=stack(gm, lambda g: g["norm"][0]),
    )
    return loss_row[0, 0], dh[BLOCK:], grads, received, pending


PARAMS = (
    ("meta_tokens", (16, 1024), 1), ("mix_norm_even", (2, 1024), None), ("w_in", (2, 1024, 3088), 2),
    ("ssm_conv_w", (2, 4, 1280), 2), ("ssm_conv_b", (2, 1280), None), ("dt_bias", (2, 16), None),
    ("a_log", (2, 16), None), ("d_skip", (2, 16), None), ("ssm_norm_w", (2, 1024), None), ("q_norm", (2, 64), None),
    ("k_norm", (2, 64), None), ("sinks", (2, 8), None), ("w_out", (2, 1536, 1024), 1), ("mix_norm_odd", (2, 1024), 1),
    ("pw1_w", (2, 1024, 2048), 2), ("pw1_b", (2, 2048), 1), ("dw_w", (2, 31, 1024), 2), ("dw_b", (2, 1024), 1),
    ("ln_g", (2, 1024), 1), ("ln_b", (2, 1024), 1), ("pw2_w", (2, 1024, 1024), 1), ("pw2_b", (2, 1024), 1),
    ("mlp_norm", (4, 1024), None), ("w_up", (4, 1024, 4096), 2), ("w_down", (4, 4096, 1024), 1),
)
MATMUL_WEIGHTS = ("w_in", "w_out", "pw1_w", "pw2_w", "w_up", "w_down")
PACK_ROW_ALIGN = 16 * PACK_W


def _block_shape(shape, axis):
    if axis is None:
        return tuple(shape)
    return tuple(s // N_DEV if a == axis else s for a, s in enumerate(shape))


def _numel(shape):
    return math.prod(shape)


def _pack(arrays, dtype):
    flat = jnp.concatenate([a.reshape(-1).astype(dtype) for a in arrays])
    n = flat.shape[0]
    padded = -(-n // PACK_ROW_ALIGN) * PACK_ROW_ALIGN
    return jnp.pad(flat, (0, padded - n)).reshape(-1, PACK_W)


def _pack_rows(arrays_by_dev, dtype):
    flat = jnp.concatenate([a.reshape(N_DEV, -1).astype(dtype) for a in arrays_by_dev], axis=1)
    n = flat.shape[1]
    padded = -(-n // PACK_ROW_ALIGN) * PACK_ROW_ALIGN
    return jnp.pad(flat, ((0, 0), (0, padded - n))).reshape(N_DEV, -1, PACK_W)


def _to_shards(full, axis):
    shape = full.shape
    split = full.reshape(shape[:axis] + (N_DEV, shape[axis] // N_DEV) + shape[axis + 1:])
    return jnp.moveaxis(split, axis, 0)


def _from_shards(blocks, axis):
    moved = jnp.moveaxis(blocks, 0, axis)
    shape = moved.shape
    return moved.reshape(shape[:axis] + (shape[axis] * shape[axis + 1],) + shape[axis + 2:])


_MESH = pl.DeviceIdType.MESH
_ANY = pl.BlockSpec(memory_space=pl.ANY)


def _mesh_place():
    x, y, c = lax.axis_index("x"), lax.axis_index("y"), lax.axis_index("c")
    return x, y, c


def _peer(x, y, c, rel):
    dx, dy, dc = (rel >> 2) & 1, (rel >> 1) & 1, rel & 1
    return (x ^ dx if dx else x, y ^ dy if dy else y, c ^ dc if dc else c)


def _dev_index(x, y, c):
    return 4 * x + 2 * y + c


def all_gather_weights(bigs, small):
    nt = len(bigs)

    def body(*refs):
        big_refs, small_ref = refs[:nt], refs[nt]
        big_outs, small_out = refs[nt + 1:2 * nt + 1], refs[2 * nt + 1]
        send_sems, recv_sems, small_send, small_recv, local_sems = refs[2 * nt + 2:]
        x, y, c = _mesh_place()
        me = (x, y, c)
        sibling = (x, y, 1 - c)
        chips = [(1 - x, y), (x, 1 - y), (1 - x, 1 - y)]

        def big_copy(t, k, block, to, from_input=False):
            dst = big_outs[t].at[_dev_index(*block)]
            return pltpu.make_async_remote_copy(src_ref=big_refs[t] if from_input else dst, dst_ref=dst,
                                                send_sem=send_sems.at[t, k], recv_sem=recv_sems.at[t, k],
                                                device_id=to, device_id_type=_MESH)

        def small_copy(rel, block, to):
            return pltpu.make_async_remote_copy(src_ref=small_ref, dst_ref=small_out.at[_dev_index(*block)],
                                                send_sem=small_send.at[rel - 1], recv_sem=small_recv.at[rel - 1],
                                                device_id=to, device_id_type=_MESH)

        mine = [pltpu.make_async_copy(big_refs[t], big_outs[t].at[_dev_index(*me)], local_sems.at[t]) for t in range(nt)]
        mine.append(pltpu.make_async_copy(small_ref, small_out.at[_dev_index(*me)], local_sems.at[nt]))
        for cp in mine:
            cp.start()
        first = []
        for t in range(nt):
            first.append(big_copy(t, 0, me, sibling, from_input=True))
            first += [big_copy(t, 1 + j, me, (*chip, c), from_input=True) for j, chip in enumerate(chips)]
        for cp in first:
            cp.start()
        smalls = [small_copy(rel, me, _peer(x, y, c, rel)) for rel in range(1, N_DEV)]
        for cp in smalls:
            cp.start()
        passed = []
        for j, chip in enumerate(chips):
            for t in range(nt):
                big_copy(t, 1 + j, (*chip, c), me).wait_recv()
                fwd = big_copy(t, 4 + j, (*chip, c), sibling)
                fwd.start()
                passed.append(fwd)
        for t in range(nt):
            big_copy(t, 0, sibling, me).wait_recv()
            for j, chip in enumerate(chips):
                big_copy(t, 4 + j, (*chip, 1 - c), me).wait_recv()
        for rel in range(1, N_DEV):
            small_copy(rel, _peer(x, y, c, rel), me).wait_recv()
        for cp in first + passed + smalls:
            cp.wait_send()
        for cp in mine:
            cp.wait()

    return pl.pallas_call(
        body, in_specs=[_ANY] * (nt + 1), out_specs=[_ANY] * (nt + 1),
        out_shape=[jax.ShapeDtypeStruct((N_DEV,) + b.shape, b.dtype) for b in bigs]
        + [jax.ShapeDtypeStruct((N_DEV,) + small.shape, small.dtype)],
        scratch_shapes=[pltpu.SemaphoreType.DMA((nt, N_DEV - 1)), pltpu.SemaphoreType.DMA((nt, N_DEV - 1)),
                        pltpu.SemaphoreType.DMA((N_DEV - 1,)), pltpu.SemaphoreType.DMA((N_DEV - 1,)),
                        pltpu.SemaphoreType.DMA((nt + 1,))],
        name="all_gather_weights",
    )(*bigs, small)


def _gather_copies(in_refs, out_refs, send_sems, recv_sems, local_sems):
    x, y, c = _mesh_place()
    me = (x, y, c)
    sibling = (x, y, 1 - c)
    chips = [(1 - x, y), (x, 1 - y), (1 - x, 1 - y)]
    nt = len(in_refs)

    def copy(t, k, block, to, from_input=False):
        dst = out_refs[t].at[_dev_index(*block)]
        return pltpu.make_async_remote_copy(src_ref=in_refs[t] if from_input else dst, dst_ref=dst,
                                            send_sem=send_sems.at[t, k], recv_sem=recv_sems.at[t, k],
                                            device_id=to, device_id_type=_MESH)

    mine = [pltpu.make_async_copy(in_refs[t], out_refs[t].at[_dev_index(*me)], local_sems.at[t]) for t in range(nt)]
    first, landed, forward, last = [], [], [], []
    for t in range(nt):
        first.append(copy(t, 0, me, sibling, from_input=True))
        last.append(copy(t, 0, sibling, me))
        for j, chip in enumerate(chips):
            first.append(copy(t, 1 + j, me, (*chip, c), from_input=True))
            landed.append(copy(t, 1 + j, (*chip, c), me))
            forward.append(copy(t, 4 + j, (*chip, c), sibling))
            last.append(copy(t, 4 + j, (*chip, 1 - c), me))
    return mine, first, landed, forward, last


GATHER_FORWARD_LEAD = 8


def _with_gather(body, n_in, n_out, shards, steps):
    n = len(shards)
    if n == 0:
        return body, [], [], [], []
    fwd_step = max(steps - 1 - GATHER_FORWARD_LEAD, 0)

    def wrapped(*refs):
        ins, g_in = refs[:n_in], refs[n_in:n_in + n]
        outs, g_out = refs[n_in + n:n_in + n + n_out], refs[n_in + n + n_out:n_in + 2 * n + n_out]
        scratch = refs[n_in + 2 * n + n_out:len(refs) - 3]
        sems = refs[len(refs) - 3:]
        i = pl.program_id(0)

        @pl.when(i == 0)
        def _():
            mine, first, _, _, _ = _gather_copies(g_in, g_out, *sems)
            for cp in mine + first:
                cp.start()

        @pl.when(i == fwd_step)
        def _():
            _, _, landed, forward, _ = _gather_copies(g_in, g_out, *sems)
            for arrived, onward in zip(landed, forward):
                arrived.wait_recv()
                onward.start()

        body(*ins, *outs, *scratch)

        @pl.when(i == steps - 1)
        def _():
            mine, first, _, forward, last = _gather_copies(g_in, g_out, *sems)
            for cp in last:
                cp.wait_recv()
            for cp in first + forward:
                cp.wait_send()
            for cp in mine:
                cp.wait()

    return (wrapped, [_ANY] * n, [_ANY] * n, [jax.ShapeDtypeStruct((N_DEV,) + a.shape, a.dtype) for a in shards],
            [pltpu.SemaphoreType.DMA((n, N_DEV - 1)), pltpu.SemaphoreType.DMA((n, N_DEV - 1)),
             pltpu.SemaphoreType.DMA((n,))])


def _exchange_copies(in_refs, out_refs, send_sems, recv_sems, local_sems):
    x, y, c = _mesh_place()
    me = _dev_index(x, y, c)
    mine, sends, arrivals = [], [], []
    for p, (src, dst) in enumerate(zip(in_refs, out_refs)):
        mine.append(pltpu.make_async_copy(src.at[me], dst.at[me], local_sems.at[p]))
        for rel in range(1, N_DEV):
            peer = _peer(x, y, c, rel)
            there = _dev_index(*peer)
            sems = dict(send_sem=send_sems.at[rel - 1, p], recv_sem=recv_sems.at[rel - 1, p], device_id=peer,
                        device_id_type=_MESH)
            sends.append(pltpu.make_async_remote_copy(src_ref=src.at[there], dst_ref=dst.at[me], **sems))
            arrivals.append(pltpu.make_async_remote_copy(src_ref=src.at[me], dst_ref=dst.at[there], **sems))
    return mine, sends, arrivals


def _with_exchange(body, n_in, n_out, carry, grid):
    n = len(carry)
    if n == 0:
        return body, [], [], [], []
    grid = (grid,) if isinstance(grid, int) else tuple(grid)

    def at_step(corner):
        hit = pl.program_id(0) == corner[0]
        for axis in range(1, len(grid)):
            hit = jnp.logical_and(hit, pl.program_id(axis) == corner[axis])
        return hit

    def wrapped(*refs):
        ins, ex_in = refs[:n_in], refs[n_in:n_in + n]
        outs, ex_out = refs[n_in + n:n_in + n + n_out], refs[n_in + n + n_out:n_in + 2 * n + n_out]
        scratch = refs[n_in + 2 * n + n_out:len(refs) - 3]
        send_sems, recv_sems, local_sems = refs[len(refs) - 3:]

        @pl.when(at_step([0] * len(grid)))
        def _():
            mine, sends, _ = _exchange_copies(ex_in, ex_out, send_sems, recv_sems, local_sems)
            for cp in mine + sends:
                cp.start()

        body(*ins, *outs, *scratch)

        @pl.when(at_step([g - 1 for g in grid]))
        def _():
            mine, sends, arrivals = _exchange_copies(ex_in, ex_out, send_sems, recv_sems, local_sems)
            for cp in arrivals:
                cp.wait_recv()
            for cp in sends:
                cp.wait_send()
            for cp in mine:
                cp.wait()

    return (wrapped, [_ANY] * n, [_ANY] * n, [jax.ShapeDtypeStruct(a.shape, a.dtype) for a in carry],
            [pltpu.SemaphoreType.DMA((N_DEV - 1, n)), pltpu.SemaphoreType.DMA((N_DEV - 1, n)),
             pltpu.SemaphoreType.DMA((n,))])


def reduce_adamw(parts, w, m, v, tr, carry=()):
    nl, r, cols = w.shape
    assert len(parts) == nl

    def body(*refs):
        p_refs = refs[:nl]
        w_ref, m_ref, v_ref, g_ref, d_ref, nm_ref, nv_ref, g_acc = refs[nl:]
        layer = pl.program_id(0)
        for l in range(nl):
            @pl.when(layer == l)
            def _(l=l):
                g = p_refs[l][0].astype(F32)
                for d in range(1, N_DEV):
                    g = g + p_refs[l][d].astype(F32)
                g_acc[...] = g

        g = g_acc[...]
        g_ref[...] = g
        nm = ADAM_B1 * m_ref[...] + (1.0 - ADAM_B1) * g
        nv = ADAM_B2 * v_ref[...] + (1.0 - ADAM_B2) * (g * g)
        nm_ref[...] = nm
        nv_ref[...] = nv
        m_hat = nm / (1.0 - ADAM_B1 ** ADAM_STEP)
        v_hat = nv / (1.0 - ADAM_B2 ** ADAM_STEP)
        d_ref[...] = -ADAM_LR * (m_hat / (jnp.sqrt(v_hat) + ADAM_EPS) + ADAM_WD * w_ref[...])

    row = pl.BlockSpec((None, tr, cols), lambda l, i: (l, i, 0))

    def part_spec(own):
        def index(l, i):
            return (0, jnp.where(l == own, i, jnp.where(l < own, 0, r // tr - 1)), 0)
        return pl.BlockSpec((N_DEV, tr, cols), index)

    grid = (nl, r // tr)
    body, ex_in, ex_out, ex_shape, ex_scratch = _with_exchange(body, nl + 3, 4, carry, grid)
    outs = pl.pallas_call(
        body, grid=grid,
        in_specs=[part_spec(l) for l in range(nl)] + [row, row, row] + ex_in,
        out_specs=[row, row, row, row] + ex_out,
        out_shape=[jax.ShapeDtypeStruct((nl, r, cols), F32)] * 4 + ex_shape,
        scratch_shapes=[pltpu.VMEM((tr, cols), F32)] + ex_scratch,
        name="reduce_adamw", compiler_params=_params("arbitrary", "arbitrary"),
    )(*parts, w, m, v, *carry)
    return outs[:4], outs[4:]


ADAMW_TILE_BYTES = 1 << 19


def _adamw_tile(rows, cols):
    lanes = -(-cols // LANE) * LANE
    best = None
    for tr in range(16, rows + 1, 16):
        if rows % tr == 0 and tr * lanes * 4 <= ADAMW_TILE_BYTES:
            best = tr
    if best is None:
        raise ValueError((rows, cols))
    return best


def kernel(x, meta_tokens, mix_norm_even, w_in, ssm_conv_w, ssm_conv_b, dt_bias, a_log, d_skip, ssm_norm_w, q_norm, k_norm, sinks, w_out, mix_norm_odd, pw1_w, pw1_b, dw_w, dw_b, ln_g, ln_b, pw2_w, pw2_b, mlp_norm, w_up, w_down, loss_target, m_meta_tokens, m_mix_norm_even, m_w_in, m_ssm_conv_w, m_ssm_conv_b, m_dt_bias, m_a_log, m_d_skip, m_ssm_norm_w, m_q_norm, m_k_norm, m_sinks, m_w_out, m_mix_norm_odd, m_pw1_w, m_pw1_b, m_dw_w, m_dw_b, m_ln_g, m_ln_b, m_pw2_w, m_pw2_b, m_mlp_norm, m_w_up, m_w_down, v_meta_tokens, v_mix_norm_even, v_w_in, v_ssm_conv_w, v_ssm_conv_b, v_dt_bias, v_a_log, v_d_skip, v_ssm_norm_w, v_q_norm, v_k_norm, v_sinks, v_w_out, v_mix_norm_odd, v_pw1_w, v_pw1_b, v_dw_w, v_dw_b, v_ln_g, v_ln_b, v_pw2_w, v_pw2_b, v_mlp_norm, v_w_up, v_w_down):
    names = [p[0] for p in PARAMS]
    w_loc = dict(zip(names, (meta_tokens, mix_norm_even, w_in, ssm_conv_w, ssm_conv_b, dt_bias, a_log, d_skip, ssm_norm_w, q_norm, k_norm, sinks, w_out, mix_norm_odd, pw1_w, pw1_b, dw_w, dw_b, ln_g, ln_b, pw2_w, pw2_b, mlp_norm, w_up, w_down)))
    m_loc = dict(zip(names, (m_meta_tokens, m_mix_norm_even, m_w_in, m_ssm_conv_w, m_ssm_conv_b, m_dt_bias, m_a_log, m_d_skip, m_ssm_norm_w, m_q_norm, m_k_norm, m_sinks, m_w_out, m_mix_norm_odd, m_pw1_w, m_pw1_b, m_dw_w, m_dw_b, m_ln_g, m_ln_b, m_pw2_w, m_pw2_b, m_mlp_norm, m_w_up, m_w_down)))
    v_loc = dict(zip(names, (v_meta_tokens, v_mix_norm_even, v_w_in, v_ssm_conv_w, v_ssm_conv_b, v_dt_bias, v_a_log, v_d_skip, v_ssm_norm_w, v_q_norm, v_k_norm, v_sinks, v_w_out, v_mix_norm_odd, v_pw1_w, v_pw1_b, v_dw_w, v_dw_b, v_ln_g, v_ln_b, v_pw2_w, v_pw2_b, v_mlp_norm, v_w_up, v_w_down)))
    small_sharded = [p for p in PARAMS if p[2] is not None and p[0] not in MATMUL_WEIGHTS]
    replicated = [p for p in PARAMS if p[2] is None]
    small_list = small_sharded + replicated

    def layer_shards(layer):
        i = layer // 2
        mixer = ("w_in", "w_out") if layer % 2 == 0 else ("pw1_w", "pw2_w")
        return [w_loc[n][i:i + 1].astype(MXU_DTYPE) for n in mixer] + [
            w_loc[n][layer:layer + 1].astype(MXU_DTYPE) for n in ("w_up", "w_down")]

    shards = [layer_shards(layer) for layer in range(DEPTH)]
    gathered = all_gather_weights(shards[0][:1], _pack([w_loc[n] for n, _, _ in small_sharded], F32))
    w_full = {n: w_loc[n] for n, _, _ in replicated}
    flat = gathered[-1].reshape(N_DEV, -1)
    off = 0
    for n, shape, axis in small_sharded:
        blk = _block_shape(shape, axis)
        w_full[n] = _from_shards(flat[:, off:off + _numel(blk)].reshape((N_DEV,) + blk), axis)
        off += _numel(blk)

    loss_local, grad_x, g_full, received, pending = local_step(x[0], loss_target[0], w_full, shards, gathered[:-1])
    loss = lax.psum(loss_local, ("x", "y", "c"))

    by_dev = [_to_shards(g_full[n], axis) for n, _, axis in small_sharded]
    by_dev += [jnp.broadcast_to(g_full[n][None], (N_DEV,) + tuple(shape)) for n, shape, _ in replicated]
    assert not pending
    order = sorted(MATMUL_WEIGHTS, key=lambda n: _numel(w_loc[n].shape))
    out = {}
    for n in order:
        nl, r, cols = w_loc[n].shape
        out[n], got = reduce_adamw(received[n], w_loc[n], m_loc[n], v_loc[n], _adamw_tile(r, cols),
                                   carry=[_pack_rows(by_dev, F32)] if n == order[0] else ())
        if n == order[0]:
            small_parts = got[0]
    pk = lambda d: _pack([d[n] for n, _, _ in small_list], F32)[None]
    rows = small_parts.shape[1]
    small_out, _ = reduce_adamw([small_parts], pk(w_loc), pk(m_loc), pk(v_loc), _adamw_tile(rows, PACK_W))
    flats = [buf.reshape(-1) for buf in small_out]
    off = 0
    for n, shape, axis in small_list:
        blk = _block_shape(shape, axis)
        out[n] = tuple(f[off:off + _numel(blk)].reshape(blk) for f in flats)
        off += _numel(blk)
    return (loss, grad_x[None], *[out[n][0] for n in names], *[out[n][1] for n in names],
            *[out[n][2] for n in names], *[out[n][3] for n in names])
```

```python
import math

import jax
import jax.numpy as jnp
from jax import lax
from jax.experimental import pallas as pl
from jax.experimental.pallas import tpu as pltpu

F32 = jnp.float32
MXU_DTYPE = jnp.bfloat16
GRAD_WIRE_DTYPE = jnp.bfloat16
HIGHEST = lax.Precision.HIGHEST

D_MODEL = 1024
N_META = 16
BLOCK = 128
FRONT_PAD = BLOCK - N_META
ATT_HEADS = 8
ATT_KV_HEADS = 2
HEAD_DIM = 64
SSM_HEADS = 16
SSM_HEAD_DIM = 64
SSM_INNER = 1024
SSM_GROUPS = 2
SSM_STATE = 64
SSM_CONV = 4
CONF_KERNEL = 31
D_FF = 4096
EPS = 1e-6
LN_EPS = 1e-5
Q_W = 512
KV_W = 128
IN_W = 3088
MIX_W = 1536
DEPTH = 4
N_DEV = 8

ADAM_LR = 0.001
ADAM_B1 = 0.9
ADAM_B2 = 0.999
ADAM_EPS = 1e-08
ADAM_WD = 0.01
ADAM_STEP = 10

PROJ_W = 3200
COL_Z, COL_XS, COL_Q, COL_BC, COL_K, COL_V, COL_DT = 0, 1024, 2048, 2560, 2816, 2944, 3072

ROW_TILE = 640
TN_ROW_TILE = 1664
ACC_BYTES = 8 * 1024 * 1024
VMEM_LIMIT = 56 * 1024 * 1024
LANE = 128
PACK_W = 1024


def _params(*sem):
    return pltpu.CompilerParams(dimension_semantics=sem, vmem_limit_bytes=VMEM_LIMIT)


def _mx(x):
    return x.astype(MXU_DTYPE)


def _dot(a, b):
    return jnp.dot(_mx(a), _mx(b), preferred_element_type=F32)


def _dot_nt(a, b):
    return lax.dot_general(_mx(a), _mx(b), (((1,), (1,)), ((), ())), preferred_element_type=F32)


def _dot_tn(a, b):
    return lax.dot_general(_mx(a), _mx(b), (((0,), (0,)), ((), ())), preferred_element_type=F32)


def _split3(x):
    hi = x.astype(jnp.bfloat16)
    r1 = x - hi.astype(F32)
    mid = r1.astype(jnp.bfloat16)
    lo = (r1 - mid.astype(F32)).astype(jnp.bfloat16)
    return hi, mid, lo


def _sel_dot(x, sel, dims):
    x_first = dims[2]
    if sel.dtype == jnp.bool_:
        sel = jnp.where(sel, 1.0, 0.0)
    one = sel.astype(jnp.bfloat16)
    acc = None
    for part in _split3(x):
        args = (part, one) if x_first else (one, part)
        t = lax.dot_general(*args, (dims[:2], ((), ())), preferred_element_type=F32)
        acc = t if acc is None else acc + t
    return acc


def _dot_hi(a, b, exact="a"):
    if exact == "a":
        return _sel_dot(a, b, ((1,), (0,), True))
    return _sel_dot(b, a, ((1,), (0,), False))


def _dot_tn_hi(a, b):
    return _sel_dot(a, b, ((0,), (0,), True))


def _sigmoid(x):
    return 1.0 / (1.0 + jnp.exp(-x))


def _row_ids(start, rows, cols):
    return start + lax.broadcasted_iota(jnp.int32, (rows, cols), 0)


def rms_fwd(h, g, name="rms_fwd"):
    m, d = h.shape
    tm = ROW_TILE

    def body(h_ref, g_ref, u_ref):
        x = h_ref[...]
        r = lax.rsqrt(jnp.mean(x * x, -1, keepdims=True) + EPS)
        u_ref[...] = ((x * r) * g_ref[...]).astype(u_ref.dtype)

    return pl.pallas_call(
        body, grid=(m // tm,),
        in_specs=[pl.BlockSpec((tm, d), lambda i: (i, 0)), pl.BlockSpec((1, d), lambda i: (0, 0))],
        out_specs=pl.BlockSpec((tm, d), lambda i: (i, 0)),
        out_shape=jax.ShapeDtypeStruct((m, d), MXU_DTYPE), name=name, compiler_params=_params("arbitrary"),
    )(h, g)


def rms_bwd(h, g, du, dh_out, name="rms_bwd"):
    m, d = h.shape
    tm = ROW_TILE

    def body(h_ref, g_ref, du_ref, dho_ref, dh_ref, dg_ref):
        @pl.when(pl.program_id(0) == 0)
        def _():
            dg_ref[...] = jnp.zeros_like(dg_ref)

        x = h_ref[...]
        du_ = du_ref[...]
        r = lax.rsqrt(jnp.mean(x * x, -1, keepdims=True) + EPS)
        gy = du_ * g_ref[...]
        dx = r * gy - x * ((r * r * r) * jnp.mean(x * gy, -1, keepdims=True))
        dh_ref[...] = dho_ref[...] + dx
        dg_ref[...] += jnp.sum(du_ * (x * r), axis=0, keepdims=True)

    row = pl.BlockSpec((tm, d), lambda i: (i, 0))
    vec = pl.BlockSpec((1, d), lambda i: (0, 0))
    return pl.pallas_call(
        body, grid=(m // tm,), in_specs=[row, vec, row, row], out_specs=[row, vec],
        out_shape=[jax.ShapeDtypeStruct((m, d), F32), jax.ShapeDtypeStruct((1, d), F32)],
        name=name, compiler_params=_params("arbitrary"),
    )(h, g, du, dh_out)


def loss_fwd_bwd(h, target, name="loss"):
    m, d = h.shape
    nb = m // BLOCK

    def body(h_ref, t_ref, dh_ref, l_ref):
        i = pl.program_id(0)

        @pl.when(i == 0)
        def _():
            l_ref[...] = jnp.zeros_like(l_ref)
            dh_ref[...] = jnp.zeros_like(dh_ref)

        @pl.when(i > 0)
        def _():
            e = h_ref[...] - t_ref[...]
            dh_ref[...] = e * (1.0 / d)
            s = jnp.sum(jnp.sum(e * e, axis=-1, keepdims=True), axis=0, keepdims=True)
            l_ref[...] += jnp.broadcast_to(s * (0.5 / d), l_ref.shape)

    return pl.pallas_call(
        body, grid=(nb,),
        in_specs=[pl.BlockSpec((BLOCK, d), lambda i: (i, 0)),
                  pl.BlockSpec((BLOCK, d), lambda i: (jnp.maximum(i - 1, 0), 0))],
        out_specs=[pl.BlockSpec((BLOCK, d), lambda i: (i, 0)), pl.BlockSpec((1, LANE), lambda i: (0, 0))],
        out_shape=[jax.ShapeDtypeStruct((m, d), F32), jax.ShapeDtypeStruct((1, LANE), F32)],
        name=name, compiler_params=_params("arbitrary"),
    )(h, target)


def col_sum(x, name="col_sum"):
    m, n = x.shape
    tm = ROW_TILE

    def body(x_ref, o_ref):
        @pl.when(pl.program_id(0) == 0)
        def _():
            o_ref[...] = jnp.zeros_like(o_ref)

        o_ref[...] += jnp.sum(x_ref[...].astype(F32), axis=0, keepdims=True)

    return pl.pallas_call(
        body, grid=(m // tm,), in_specs=[pl.BlockSpec((tm, n), lambda i: (i, 0))],
        out_specs=pl.BlockSpec((1, n), lambda i: (0, 0)), out_shape=jax.ShapeDtypeStruct((1, n), F32),
        name=name, compiler_params=_params("arbitrary"),
    )(x)


def matmul(a, b, *, b_kind="full", layer=0, trans_b=False, tn=None, epilogue=None, bias=None, extra=None,
           out_dtype=F32, carry=(), name="matmul"):
    m, k = a.shape
    tm = ROW_TILE
    merge = False
    if b_kind == "full":
        n = b.shape[0] if trans_b else b.shape[1]
        tn = n if tn is None else tn
        b_spec = pl.BlockSpec((tn, k), lambda i, j: (j, 0)) if trans_b else pl.BlockSpec((k, tn), lambda i, j: (0, j))
    elif b_kind == "rowshard":
        ks, wn = b.shape[2], b.shape[3]
        if trans_b and tn == ks:
            assert wn == k
            n = N_DEV * ks
            b_spec = pl.BlockSpec((None, None, ks, wn), lambda i, j: (j, layer, 0, 0))
        else:
            assert tn is None
            merge = True
            n = N_DEV * ks if trans_b else wn
            assert (wn if trans_b else N_DEV * ks) == k
            tn = n
            b_spec = pl.BlockSpec((N_DEV, None, ks, wn), lambda i, j: (0, layer, 0, 0))
    else:
        raise ValueError(b_kind)
    has_bias = bias is not None
    has_extra = extra is not None

    def body(*refs):
        a_ref, b_ref = refs[0], refs[1]
        pos = 2
        bias_ref = extra_ref = None
        if has_bias:
            bias_ref = refs[pos]
            pos += 1
        if has_extra:
            extra_ref = refs[pos]
            pos += 1
        outs = refs[pos:]
        w = b_ref[...]
        if merge:
            w = w.reshape(N_DEV * w.shape[1], w.shape[2])
        if trans_b:
            acc = _dot_nt(a_ref[...], w)
        else:
            acc = _dot(a_ref[...], w)
        if has_bias:
            acc = acc + bias_ref[...]
        if epilogue is None:
            outs[0][...] = acc.astype(outs[0].dtype)
        elif epilogue == "relu2":
            outs[0][...] = acc
            r = jnp.maximum(acc, 0.0)
            outs[1][...] = (r * r).astype(outs[1].dtype)
        elif epilogue == "drelu2":
            outs[0][...] = (acc * (2.0 * jnp.maximum(extra_ref[...], 0.0))).astype(outs[0].dtype)
        elif epilogue == "resid":
            rows = _row_ids(pl.program_id(0) * tm, tm, tn)
            outs[0][...] = extra_ref[...] + jnp.where(rows >= FRONT_PAD, acc, 0.0)
        else:
            raise ValueError(epilogue)

    in_specs = [pl.BlockSpec((tm, k), lambda i, j: (i, 0)), b_spec]
    args = [a, b]
    if has_bias:
        in_specs.append(pl.BlockSpec((1, tn), lambda i, j: (0, j)))
        args.append(bias)
    if has_extra:
        in_specs.append(pl.BlockSpec((tm, tn), lambda i, j: (i, j)))
        args.append(extra)
    tile = pl.BlockSpec((tm, tn), lambda i, j: (i, j))
    if epilogue == "relu2":
        out_specs = [tile, tile]
        out_shape = [jax.ShapeDtypeStruct((m, n), F32), jax.ShapeDtypeStruct((m, n), MXU_DTYPE)]
    else:
        out_specs = [tile]
        out_shape = [jax.ShapeDtypeStruct((m, n), out_dtype)]
    n_out = len(out_specs)
    grid = (m // tm, n // tn)
    body, ex_in, ex_out, ex_shape, ex_scratch = _with_exchange(body, len(args), n_out, carry, grid)
    outs = pl.pallas_call(
        body, grid=grid, in_specs=in_specs + ex_in, out_specs=out_specs + ex_out, out_shape=out_shape + ex_shape,
        scratch_shapes=ex_scratch, name=name, compiler_params=_params("arbitrary", "arbitrary"),
    )(*args, *carry)
    result = outs[0] if n_out == 1 else tuple(outs[:n_out])
    return (result, outs[n_out:]) if carry else result


def matmul_tn(x, dy, *, ti, tn, out_dtype=F32, name="matmul_tn"):
    m, k1 = x.shape
    n = dy.shape[1]
    tm = TN_ROW_TILE
    last = m // tm - 1

    def body(x_ref, dy_ref, o_ref, acc_ref):
        r = pl.program_id(2)

        @pl.when(r == 0)
        def _():
            acc_ref[...] = jnp.zeros_like(acc_ref)

        acc_ref[...] += _dot_tn(x_ref[...], dy_ref[...])

        @pl.when(r == last)
        def _():
            o_ref[...] = acc_ref[...].astype(o_ref.dtype)

    out_specs = pl.BlockSpec((ti, tn), lambda i, j, r: (i, j))
    out_shape = jax.ShapeDtypeStruct((k1, n), out_dtype)
    return pl.pallas_call(
        body, grid=(k1 // ti, n // tn, m // tm),
        in_specs=[pl.BlockSpec((tm, ti), lambda i, j, r: (r, i)), pl.BlockSpec((tm, tn), lambda i, j, r: (r, j))],
        out_specs=out_specs, out_shape=out_shape, scratch_shapes=[pltpu.VMEM((ti, tn), F32)], name=name,
        compiler_params=_params("arbitrary", "arbitrary", "arbitrary"),
    )(x, dy)


FF_BLOCK = D_FF // N_DEV
SQRT_FLOOR = 1.1754944e-38


def _ff_cols(d):
    return slice(FF_BLOCK * d, FF_BLOCK * (d + 1))


def mlp_up(h, norm_g, w, layer, *, bias=None, relu2=True, out_dtype=None, name="mlp_up"):
    m = h.shape[0]
    ns = w.shape[3]
    n = N_DEV * ns
    tm = ROW_TILE
    out_dtype = MXU_DTYPE if relu2 else out_dtype
    has_bias = bias is not None

    def body(*refs):
        h_ref, g_ref, w_ref = refs[0], refs[1], refs[2]
        bias_ref = refs[3] if has_bias else None
        o_ref, u_ref = refs[-2], refs[-1]
        x = h_ref[...]
        u_ = _mx((x * lax.rsqrt(jnp.mean(x * x, -1, keepdims=True) + EPS)) * g_ref[...])
        u_ref[...] = u_
        for d in range(N_DEV):
            cols = slice(ns * d, ns * (d + 1))
            r = _dot(u_, w_ref[d])
            if has_bias:
                r = r + bias_ref[:, cols]
            if relu2:
                r = jnp.maximum(r, 0.0)
                r = r * r
            o_ref[:, cols] = r.astype(o_ref.dtype)

    row = pl.BlockSpec((tm, D_MODEL), lambda i: (i, 0))
    in_specs = [row, pl.BlockSpec((1, D_MODEL), lambda i: (0, 0)),
                pl.BlockSpec((N_DEV, None, D_MODEL, ns), lambda i: (0, layer, 0, 0))]
    args = [h, norm_g, w]
    if has_bias:
        in_specs.append(pl.BlockSpec((1, n), lambda i: (0, 0)))
        args.append(bias)
    return pl.pallas_call(
        body, grid=(m // tm,), in_specs=in_specs, out_specs=[pl.BlockSpec((tm, n), lambda i: (i, 0)), row],
        out_shape=[jax.ShapeDtypeStruct((m, n), out_dtype), jax.ShapeDtypeStruct((m, D_MODEL), MXU_DTYPE)],
        name=name, compiler_params=_params("arbitrary"),
    )(*args)


def mlp_dact(dh, w_down, act, layer, name="mlp_dact"):
    m = dh.shape[0]
    tm = ROW_TILE

    def body(dh_ref, w_ref, act_ref, o_ref):
        dh_ = dh_ref[...]
        for d in range(N_DEV):
            p = act_ref[:, _ff_cols(d)].astype(F32)
            r = p * lax.rsqrt(jnp.maximum(p, SQRT_FLOOR))
            o_ref[:, _ff_cols(d)] = (_dot_nt(dh_, w_ref[d]) * (2.0 * r)).astype(o_ref.dtype)

    return pl.pallas_call(
        body, grid=(m // tm,),
        in_specs=[pl.BlockSpec((tm, D_MODEL), lambda i: (i, 0)),
                  pl.BlockSpec((N_DEV, None, FF_BLOCK, D_MODEL), lambda i: (0, layer, 0, 0)),
                  pl.BlockSpec((tm, D_FF), lambda i: (i, 0))],
        out_specs=pl.BlockSpec((tm, D_FF), lambda i: (i, 0)),
        out_shape=jax.ShapeDtypeStruct((m, D_FF), MXU_DTYPE), name=name, compiler_params=_params("arbitrary"),
    )(dh, w_down, act)


def mlp_du_rms_bwd(da, w_up, layer, h, g, dh_out, name="mlp_du"):
    m, n = da.shape
    ns = w_up.shape[3]
    assert n == N_DEV * ns
    tm = ROW_TILE

    def body(da_ref, w_ref, h_ref, g_ref, dho_ref, dh_ref, dg_ref):
        @pl.when(pl.program_id(0) == 0)
        def _():
            dg_ref[...] = jnp.zeros_like(dg_ref)

        du = _dot_nt(da_ref[:, 0:ns], w_ref[0])
        for d in range(1, N_DEV):
            du = du + _dot_nt(da_ref[:, ns * d:ns * (d + 1)], w_ref[d])
        x = h_ref[...]
        r = lax.rsqrt(jnp.mean(x * x, -1, keepdims=True) + EPS)
        gy = du * g_ref[...]
        dx = r * gy - x * ((r * r * r) * jnp.mean(x * gy, -1, keepdims=True))
        dh_ref[...] = dho_ref[...] + dx
        dg_ref[...] += jnp.sum(du * (x * r), axis=0, keepdims=True)

    row = pl.BlockSpec((tm, D_MODEL), lambda i: (i, 0))
    vec = pl.BlockSpec((1, D_MODEL), lambda i: (0, 0))
    return pl.pallas_call(
        body, grid=(m // tm,),
        in_specs=[pl.BlockSpec((tm, n), lambda i: (i, 0)),
                  pl.BlockSpec((N_DEV, None, D_MODEL, ns), lambda i: (0, layer, 0, 0)), row, vec, row],
        out_specs=[row, vec],
        out_shape=[jax.ShapeDtypeStruct((m, D_MODEL), F32), jax.ShapeDtypeStruct((1, D_MODEL), F32)],
        name=name, compiler_params=_params("arbitrary"),
    )(da, w_up, h, g, dh_out)


def mlp_dw_up(u, da, name="mlp_dw_up"):
    m, n = da.shape
    ns = n // N_DEV
    tm = TN_ROW_TILE
    last = m // tm - 1
    parts = -(-D_MODEL * n * 4 // ACC_BYTES)
    per = N_DEV // parts

    def body(u_ref, da_ref, o_ref, acc_ref):
        r = pl.program_id(1)

        @pl.when(r == 0)
        def _():
            acc_ref[...] = jnp.zeros_like(acc_ref)

        acc_ref[...] += _dot_tn(u_ref[...], da_ref[...])

        @pl.when(r == last)
        def _():
            for d in range(per):
                o_ref[d] = acc_ref[:, ns * d:ns * (d + 1)].astype(o_ref.dtype)

    return pl.pallas_call(
        body, grid=(parts, m // tm),
        in_specs=[pl.BlockSpec((tm, D_MODEL), lambda h, r: (r, 0)), pl.BlockSpec((tm, n // parts), lambda h, r: (r, h))],
        out_specs=pl.BlockSpec((per, D_MODEL, ns), lambda h, r: (h, 0, 0)),
        out_shape=jax.ShapeDtypeStruct((N_DEV, D_MODEL, ns), GRAD_WIRE_DTYPE),
        scratch_shapes=[pltpu.VMEM((D_MODEL, n // parts), F32)], name=name,
        compiler_params=_params("arbitrary", "arbitrary"),
    )(u, da)


def mlp_dw_down(act, dh, name="mlp_dw_down"):
    m = act.shape[0]
    tm = TN_ROW_TILE
    last = m // tm - 1
    parts = D_FF * D_MODEL * 4 // ACC_BYTES
    rows = D_FF // parts

    def body(a_ref, dh_ref, o_ref, acc_ref):
        r = pl.program_id(1)

        @pl.when(r == 0)
        def _():
            acc_ref[...] = jnp.zeros_like(acc_ref)

        acc_ref[...] += _dot_tn(a_ref[...], dh_ref[...])

        @pl.when(r == last)
        def _():
            o_ref[...] = acc_ref[...].astype(o_ref.dtype)

    return pl.pallas_call(
        body, grid=(parts, m // tm),
        in_specs=[pl.BlockSpec((tm, rows), lambda h, r: (r, h)), pl.BlockSpec((tm, D_MODEL), lambda h, r: (r, 0))],
        out_specs=pl.BlockSpec((rows, D_MODEL), lambda h, r: (h, 0)),
        out_shape=jax.ShapeDtypeStruct((D_FF, D_MODEL), GRAD_WIRE_DTYPE),
        scratch_shapes=[pltpu.VMEM((rows, D_MODEL), F32)], name=name,
        compiler_params=_params("arbitrary", "arbitrary"),
    )(act, dh)


_ATT_SCALE = HEAD_DIM ** -0.5


def _alibi_slope(h):
    return 2.0 ** (-8.0 * (h + 1) / ATT_HEADS)


HEADS_PER_KV = ATT_HEADS // ATT_KV_HEADS
STACK = HEADS_PER_KV * BLOCK
N_KEYS = 3 * BLOCK


def _head_select(width):
    c = jnp.arange(width)[:, None]
    h = jnp.arange(LANE)[None, :]
    return (c // HEAD_DIM == h).astype(F32)


def _head_fold(width):
    c = jnp.arange(width)[:, None]
    j = jnp.arange(LANE)[None, :]
    return (c % HEAD_DIM == j).astype(F32)


def _head_rms(x, sel, sel_t):
    r = lax.rsqrt(_dot_hi(x * x, sel) * (1.0 / HEAD_DIM) + EPS)
    return r, _dot_hi(r, sel_t)


def _head_norm_bwd(x, r, r_full, w_t, dy, sel, sel_t):
    gy = dy * w_t
    coef = _dot_hi((r * r * r) * _dot_hi(x * gy, sel) * (1.0 / HEAD_DIM), sel_t)
    return r_full * gy - x * coef, jnp.sum(dy * (x * r_full), axis=0, keepdims=True)


def _low_lanes(rows):
    return lax.broadcasted_iota(jnp.int32, (rows, LANE), 1) < HEAD_DIM


def _dup_half(a, g):
    rolled = pltpu.roll(a, HEAD_DIM, 1)
    low = _low_lanes(a.shape[0])
    return jnp.where(low, a, rolled) if g == 0 else jnp.where(low, rolled, a)


def _stack_heads(x, g):
    low = _low_lanes(BLOCK)
    parts = []
    for pair in range(2):
        p = x[:, 2 * LANE * g + LANE * pair:2 * LANE * g + LANE * (pair + 1)]
        parts += [jnp.where(low, p, 0.0), jnp.where(low, 0.0, p)]
    return jnp.concatenate(parts, axis=0)


def _unstack_heads(groups):
    low = _low_lanes(BLOCK)
    cols = []
    for o in groups:
        for pair in range(2):
            cols.append(jnp.where(low, o[2 * pair * BLOCK:(2 * pair + 1) * BLOCK], o[(2 * pair + 1) * BLOCK:(2 * pair + 2) * BLOCK]))
    return jnp.concatenate(cols, axis=1)


def _fold_halves(a, g):
    s = a + pltpu.roll(a, HEAD_DIM, 1)
    low = _low_lanes(a.shape[0])
    return jnp.where(low if g == 0 else jnp.logical_not(low), s, 0.0)


def _att_bias(b):
    r = lax.broadcasted_iota(jnp.int32, (STACK, N_KEYS), 0) & (BLOCK - 1)
    col = lax.broadcasted_iota(jnp.int32, (STACK, N_KEYS), 1)
    cc = col & (BLOCK - 1)
    is_meta = col < BLOCK
    is_prev = jnp.logical_and(col >= BLOCK, col < 2 * BLOCK)
    q_pos = b * BLOCK + r - FRONT_PAD
    meta_j = cc - FRONT_PAD
    valid_m = jnp.logical_and(cc >= FRONT_PAD, q_pos >= meta_j)
    valid_p = jnp.logical_and(cc > r, b >= 2)
    valid_c = jnp.logical_and(cc <= r, b >= 1)
    is_cur = col >= 2 * BLOCK
    valid = jnp.logical_or(jnp.logical_and(is_meta, valid_m),
                           jnp.logical_or(jnp.logical_and(is_prev, valid_p), jnp.logical_and(is_cur, valid_c)))
    dist = jnp.where(is_meta, jnp.minimum(q_pos - meta_j, BLOCK), jnp.where(is_prev, r - cc + BLOCK, r - cc))
    return valid, dist.astype(F32)


def _per_head_column(values):
    hid = lax.broadcasted_iota(jnp.int32, (STACK, 1), 0) >> 7
    col = jnp.where(hid == 0, values[0], values[1])
    for j in range(2, HEADS_PER_KV):
        col = jnp.where(hid == j, values[j], col)
    return col


def _att_group_probs(qs, kd, valid, dist, g, sk_ref):
    slope = _per_head_column([_alibi_slope(HEADS_PER_KV * g + j) for j in range(HEADS_PER_KV)])
    sink = _per_head_column([sk_ref[HEADS_PER_KV * g + j] for j in range(HEADS_PER_KV)])
    s = jnp.where(valid, _dot_nt(qs, kd) * _ATT_SCALE - slope * dist, -1e30)
    mx = jnp.maximum(jnp.max(s, axis=-1, keepdims=True), sink)
    p = jnp.exp(s - mx)
    p_sink = jnp.exp(sink - mx)
    inv = 1.0 / (jnp.sum(p, axis=-1, keepdims=True) + p_sink)
    return p * inv, p_sink * inv


def attention_fwd(proj, q_w, k_w, sinks, gather=(), name="att_fwd"):
    m = proj.shape[0]
    nb = m // BLOCK
    cq, ck, cv = COL_Q // Q_W, COL_K // KV_W, COL_V // KV_W
    sel_q, sel_k = _head_select(Q_W), _head_select(KV_W)

    def body(q_ref, kc_ref, vc_ref, vp_ref, vm_ref, qw_ref, kw_ref, sk_ref, sq_ref, sqt_ref, skk_ref, skt_ref,
             o_ref, kpn_s, kmn_s):
        b = pl.program_id(0)
        q, kc = q_ref[...], kc_ref[...]
        _, rq = _head_rms(q, sq_ref[...], sqt_ref[...])
        qn = q * rq * qw_ref[...]
        _, rk = _head_rms(kc, skk_ref[...], skt_ref[...])
        kcn = kc * rk * kw_ref[...]

        @pl.when(b == 0)
        def _():
            kmn_s[...] = kcn
            kpn_s[...] = kcn

        kpn, kmn = kpn_s[...], kmn_s[...]
        kpn_s[...] = kcn
        vc, vp, vm = vc_ref[...], vp_ref[...], vm_ref[...]
        valid, dist = _att_bias(b)
        outs = []
        for g in range(ATT_KV_HEADS):
            kd = jnp.concatenate([_dup_half(kmn, g), _dup_half(kpn, g), _dup_half(kcn, g)], axis=0)
            vd = jnp.concatenate([_dup_half(vm, g), _dup_half(vp, g), _dup_half(vc, g)], axis=0)
            probs, _ = _att_group_probs(_stack_heads(qn, g), kd, valid, dist, g, sk_ref)
            outs.append(_dot(probs, vd))
        o_ref[...] = _unstack_heads(outs).astype(o_ref.dtype)

    prev = lambda i: jnp.maximum(i - 1, 0)
    full = lambda s: pl.BlockSpec(s, lambda i: (0,) * len(s))
    body, g_in, g_out, g_shape, g_scratch = _with_gather(body, 12, 1, gather, nb)
    outs = pl.pallas_call(
        body, grid=(nb,),
        in_specs=[pl.BlockSpec((BLOCK, Q_W), lambda i: (i, cq)),
                  pl.BlockSpec((BLOCK, KV_W), lambda i: (i, ck)), pl.BlockSpec((BLOCK, KV_W), lambda i: (i, cv)),
                  pl.BlockSpec((BLOCK, KV_W), lambda i: (prev(i), cv)), pl.BlockSpec((BLOCK, KV_W), lambda i: (0, cv)),
                  full((1, Q_W)), full((1, KV_W)), pl.BlockSpec(memory_space=pltpu.SMEM),
                  full((Q_W, LANE)), full((LANE, Q_W)), full((KV_W, LANE)), full((LANE, KV_W))] + g_in,
        out_specs=[pl.BlockSpec((BLOCK, Q_W), lambda i: (i, 0))] + g_out,
        out_shape=[jax.ShapeDtypeStruct((m, Q_W), MXU_DTYPE)] + g_shape,
        scratch_shapes=[pltpu.VMEM((BLOCK, KV_W), F32), pltpu.VMEM((BLOCK, KV_W), F32)] + g_scratch,
        name=name, compiler_params=_params("arbitrary"),
    )(proj, proj, proj, proj, proj, q_w, k_w, sinks, sel_q, sel_q.T, sel_k, sel_k.T, *gather)
    return outs[0], outs[1:]


def attention_bwd(proj, dmix, q_w, k_w, sinks, carry=(), name="att_bwd"):
    m = proj.shape[0]
    nb = m // BLOCK
    cq, ck, cv = COL_Q // Q_W, COL_K // KV_W, COL_V // KV_W
    c_datt = 0
    sel_q, sel_k = _head_select(Q_W), _head_select(KV_W)
    fold_q, fold_k = _head_fold(Q_W), _head_fold(KV_W)

    def body(do_ref, q_ref, kc_ref, vc_ref, kp_ref, vp_ref, km_ref, vm_ref, qw_ref, kw_ref, sk_ref,
             sq_ref, sqt_ref, skk_ref, skt_ref, fq_ref, fk_ref,
             dq_ref, dk_ref, dv_ref, dqw_ref, dkw_ref, dsk_ref, car_k, car_v, met_k, met_v, kmn_s, qw_acc, kw_acc):
        i = pl.program_id(0)
        b = nb - 1 - i
        sel_q_, sel_qt, sel_k_, sel_kt = sq_ref[...], sqt_ref[...], skk_ref[...], skt_ref[...]
        qw, kw = qw_ref[...], kw_ref[...]

        @pl.when(i == 0)
        def _():
            for r in (car_k, car_v, met_k, met_v, qw_acc, kw_acc, dsk_ref):
                r[...] = jnp.zeros_like(r)
            km = km_ref[...]
            kmn_s[...] = km * _head_rms(km, sel_k_, sel_kt)[1] * kw

        q, kc, kp = q_ref[...], kc_ref[...], kp_ref[...]
        rq, rq_full = _head_rms(q, sel_q_, sel_qt)
        qn = q * rq_full * qw
        rk, rk_full = _head_rms(kc, sel_k_, sel_kt)
        kcn = kc * rk_full * kw
        kpn = kp * _head_rms(kp, sel_k_, sel_kt)[1] * kw
        kmn = kmn_s[...]
        vc, vp, vm = vc_ref[...], vp_ref[...], vm_ref[...]
        do = do_ref[...]
        valid, dist = _att_bias(b)
        lane = lax.broadcasted_iota(jnp.int32, (1, LANE), 1)
        dsk = jnp.zeros((1, LANE), F32)
        dkd_sum = jnp.zeros((N_KEYS, KV_W), F32)
        dvd_sum = jnp.zeros((N_KEYS, KV_W), F32)
        dqd = []
        for g in range(ATT_KV_HEADS):
            kd = jnp.concatenate([_dup_half(kmn, g), _dup_half(kpn, g), _dup_half(kcn, g)], axis=0)
            vd = jnp.concatenate([_dup_half(vm, g), _dup_half(vp, g), _dup_half(vc, g)], axis=0)
            qs = _stack_heads(qn, g)
            dos = _stack_heads(do, g)
            probs, p_sink = _att_group_probs(qs, kd, valid, dist, g, sk_ref)
            o = _dot(probs, vd)
            delta = jnp.sum(dos * o, axis=-1, keepdims=True)
            ds = probs * (_dot_nt(dos, vd) - delta)
            dqd.append(_dot(ds, kd) * _ATT_SCALE)
            dkd_sum = dkd_sum + _fold_halves(_dot_tn(ds, qs) * _ATT_SCALE, g)
            dvd_sum = dvd_sum + _fold_halves(_dot_tn(probs, dos), g)
            sink_grad = p_sink * delta
            for j in range(HEADS_PER_KV):
                part = jnp.sum(sink_grad[BLOCK * j:BLOCK * (j + 1)], axis=0, keepdims=True)
                dsk = dsk - jnp.where(lane == HEADS_PER_KV * g + j, part, 0.0)
        dq, dqw = _head_norm_bwd(q, rq, rq_full, qw, _unstack_heads(dqd), sel_q_, sel_qt)
        dq_ref[...] = dq
        qw_acc[...] += dqw
        dsk_ref[...] += dsk

        met_k[...] += dkd_sum[0:BLOCK]
        met_v[...] += dvd_sum[0:BLOCK]
        first = (b == 0).astype(F32)
        dkn_tot = dkd_sum[2 * BLOCK:3 * BLOCK] + car_k[...] + first * met_k[...]
        dv_ref[...] = dvd_sum[2 * BLOCK:3 * BLOCK] + car_v[...] + first * met_v[...]
        car_k[...] = dkd_sum[BLOCK:2 * BLOCK]
        car_v[...] = dvd_sum[BLOCK:2 * BLOCK]
        dk, dkw = _head_norm_bwd(kc, rk, rk_full, kw, dkn_tot, sel_k_, sel_kt)
        dk_ref[...] = dk
        kw_acc[...] += dkw

        @pl.when(i == nb - 1)
        def _():
            dqw_ref[...] = _dot_hi(jnp.broadcast_to(qw_acc[...], (8, Q_W)), fq_ref[...])[0:1]
            dkw_ref[...] = _dot_hi(jnp.broadcast_to(kw_acc[...], (8, KV_W)), fk_ref[...])[0:1]

    blk = lambda i: nb - 1 - i
    prev = lambda i: jnp.maximum(nb - 2 - i, 0)
    full = lambda s: pl.BlockSpec(s, lambda i: (0,) * len(s))
    kv_scratch = pltpu.VMEM((BLOCK, KV_W), F32)
    body, ex_in, ex_out, ex_shape, ex_scratch = _with_exchange(body, 17, 6, carry, nb)
    outs = pl.pallas_call(
        body, grid=(nb,),
        in_specs=[pl.BlockSpec((BLOCK, Q_W), lambda i: (blk(i), c_datt)),
                  pl.BlockSpec((BLOCK, Q_W), lambda i: (blk(i), cq)),
                  pl.BlockSpec((BLOCK, KV_W), lambda i: (blk(i), ck)), pl.BlockSpec((BLOCK, KV_W), lambda i: (blk(i), cv)),
                  pl.BlockSpec((BLOCK, KV_W), lambda i: (prev(i), ck)), pl.BlockSpec((BLOCK, KV_W), lambda i: (prev(i), cv)),
                  pl.BlockSpec((BLOCK, KV_W), lambda i: (0, ck)), pl.BlockSpec((BLOCK, KV_W), lambda i: (0, cv)),
                  full((1, Q_W)), full((1, KV_W)), pl.BlockSpec(memory_space=pltpu.SMEM),
                  full((Q_W, LANE)), full((LANE, Q_W)), full((KV_W, LANE)), full((LANE, KV_W)),
                  full((Q_W, LANE)), full((KV_W, LANE))] + ex_in,
        out_specs=[pl.BlockSpec((BLOCK, Q_W), lambda i: (blk(i), 0)),
                   pl.BlockSpec((BLOCK, KV_W), lambda i: (blk(i), 0)), pl.BlockSpec((BLOCK, KV_W), lambda i: (blk(i), 0)),
                   full((1, LANE)), full((1, LANE)), full((1, LANE))] + ex_out,
        out_shape=[jax.ShapeDtypeStruct((m, Q_W), F32), jax.ShapeDtypeStruct((m, KV_W), F32),
                   jax.ShapeDtypeStruct((m, KV_W), F32), jax.ShapeDtypeStruct((1, LANE), F32),
                   jax.ShapeDtypeStruct((1, LANE), F32), jax.ShapeDtypeStruct((1, LANE), F32)] + ex_shape,
        scratch_shapes=[kv_scratch, kv_scratch, kv_scratch, kv_scratch, kv_scratch,
                        pltpu.VMEM((1, Q_W), F32), pltpu.VMEM((1, KV_W), F32)] + ex_scratch,
        name=name, compiler_params=_params("arbitrary"),
    )(dmix, proj, proj, proj, proj, proj, proj, proj, q_w, k_w, sinks, sel_q, sel_q.T, sel_k, sel_k.T, fold_q, fold_k,
      *carry)
    return outs[:6], outs[6:]


HALO = 8
GROUP_W = SSM_INNER // SSM_GROUPS
HEADS_PER_GROUP = SSM_HEADS // SSM_GROUPS


def _head_expand():
    h = jnp.arange(LANE)[:, None]
    c = jnp.arange(SSM_INNER)[None, :]
    return (c // SSM_HEAD_DIM == h).astype(F32)


def _softplus(x):
    return jnp.maximum(x, 0.0) + jnp.log1p(jnp.exp(-jnp.abs(x)))


def _ssd_decays(dt, a_log_row):
    row = lax.broadcasted_iota(jnp.int32, (BLOCK, BLOCK), 0)
    col = lax.broadcasted_iota(jnp.int32, (BLOCK, BLOCK), 1)
    lower = row >= col
    a = -jnp.exp(a_log_row)
    a_cs = _dot_hi(lower, dt * a, exact="b")
    return a, a_cs, lower


def _decay_matrix(a_cs, a_cs_t, h, lower):
    diff = a_cs[:, h:h + 1] - a_cs_t[h:h + 1, :]
    return jnp.where(lower, jnp.exp(jnp.where(lower, diff, 0.0)), 0.0)


def _conv_taps(s_ref, w_ref, first, rows):
    acc = w_ref[0:1, :] * s_ref[pl.ds(first, rows), :]
    for j in range(1, SSM_CONV):
        acc = acc + w_ref[j:j + 1, :] * s_ref[pl.ds(first + j, rows), :]
    return acc


def ssd_fwd(proj, att, cw_x, cw_bc, cb_x, cb_bc, dt_bias, a_log, d_exp, norm_w, gather=(), name="ssd_fwd"):
    m = proj.shape[0]
    nb = m // BLOCK
    expand = _head_expand()
    expand_t = expand.T

    def body(z_ref, xs_ref, bc_ref, dtr_ref, att_ref, cwx_ref, cwbc_ref, cbx_ref, cbbc_ref, dtb_ref, alog_ref, dexp_ref,
             nw_ref, e_ref, et_ref, out_ref, prex_ref, prebc_ref, dt_ref, ypre_ref, st_ref, sx, sbc, state):
        c = pl.program_id(0)

        @pl.when(c == 0)
        def _():
            sx[0:HALO, :] = jnp.zeros((HALO, SSM_INNER), F32)
            sbc[0:HALO, :] = jnp.zeros((HALO, 2 * LANE), F32)
            state[...] = jnp.zeros_like(state)

        sx[HALO:HALO + BLOCK, :] = xs_ref[...]
        sbc[HALO:HALO + BLOCK, :] = bc_ref[...]
        first = HALO - (SSM_CONV - 1)
        pre_x = _conv_taps(sx, cwx_ref, first, BLOCK) + cbx_ref[...]
        pre_bc = _conv_taps(sbc, cwbc_ref, first, BLOCK) + cbbc_ref[...]
        sx[0:HALO, :] = xs_ref[BLOCK - HALO:BLOCK, :]
        sbc[0:HALO, :] = bc_ref[BLOCK - HALO:BLOCK, :]
        prex_ref[...] = pre_x
        prebc_ref[...] = pre_bc
        xc = pre_x * _sigmoid(pre_x)
        bcv = pre_bc * _sigmoid(pre_bc)

        rows = _row_ids(c * BLOCK, BLOCK, LANE)
        lanes = lax.broadcasted_iota(jnp.int32, (BLOCK, LANE), 1)
        live = jnp.logical_and(rows >= FRONT_PAD, lanes < SSM_HEADS)
        dt = jnp.where(live, _softplus(dtr_ref[...] + dtb_ref[...]), 0.0)
        dt_ref[...] = dt
        a, a_cs, lower = _ssd_decays(dt, alog_ref[...])
        a_cs_t = a_cs.T
        dt_t = dt.T
        e = e_ref[...]
        es_full = _dot_hi(jnp.exp(a_cs), e)
        wx_full = _dot_hi(jnp.exp(a_cs[BLOCK - 1:BLOCK, :] - a_cs) * dt, e)
        end_col = jnp.exp(a_cs_t[:, BLOCK - 1:BLOCK])
        dec_full = _dot_hi(et_ref[...], jnp.broadcast_to(end_col, (LANE, SSM_STATE)), exact="b")

        st_ref[0] = state[...]
        ys = []
        for g in range(SSM_GROUPS):
            b_g = bcv[:, SSM_STATE * g:SSM_STATE * (g + 1)]
            c_g = bcv[:, LANE + SSM_STATE * g:LANE + SSM_STATE * (g + 1)]
            gs = slice(GROUP_W * g, GROUP_W * (g + 1))
            cb = _dot_nt(c_g, b_g)
            yd = []
            for hh in range(HEADS_PER_GROUP):
                h = g * HEADS_PER_GROUP + hh
                w = cb * _decay_matrix(a_cs, a_cs_t, h, lower) * dt_t[h:h + 1, :]
                yd.append(_dot(w, xc[:, SSM_HEAD_DIM * h:SSM_HEAD_DIM * (h + 1)]))
            h_g = state[gs, :]
            y_off = _dot_nt(c_g, h_g) * es_full[:, gs]
            ys.append(jnp.concatenate(yd, axis=1) + y_off)
            new_state = _dot_tn(xc[:, gs] * wx_full[:, gs], b_g)
            state[gs, :] = h_g * dec_full[gs, :] + new_state
        y_pre = jnp.concatenate(ys, axis=1) + xc * dexp_ref[...]
        ypre_ref[...] = y_pre
        z = z_ref[...]
        gt = y_pre * (z * _sigmoid(z))
        outs = []
        for g in range(SSM_GROUPS):
            gg = gt[:, GROUP_W * g:GROUP_W * (g + 1)]
            r = lax.rsqrt(jnp.mean(gg * gg, -1, keepdims=True) + EPS)
            outs.append(gg * r)
        out_ref[:, 0:Q_W] = att_ref[...]
        out_ref[:, Q_W:MIX_W] = (jnp.concatenate(outs, axis=1) * nw_ref[...]).astype(out_ref.dtype)

    full = lambda s: pl.BlockSpec(s, lambda i: (0,) * len(s))
    rowblk = lambda w, cidx: pl.BlockSpec((BLOCK, w), lambda i: (i, cidx))
    body, g_in, g_out, g_shape, g_scratch = _with_gather(body, 15, 6, gather, nb)
    outs = pl.pallas_call(
        body, grid=(nb,),
        in_specs=[rowblk(SSM_INNER, COL_Z // SSM_INNER), rowblk(SSM_INNER, COL_XS // SSM_INNER),
                  rowblk(2 * LANE, COL_BC // (2 * LANE)), rowblk(LANE, COL_DT // LANE), rowblk(Q_W, 0),
                  full((SSM_CONV, SSM_INNER)), full((SSM_CONV, 2 * LANE)), full((1, SSM_INNER)), full((1, 2 * LANE)),
                  full((1, LANE)), full((1, LANE)), full((1, SSM_INNER)), full((1, SSM_INNER)),
                  full((LANE, SSM_INNER)), full((SSM_INNER, LANE))] + g_in,
        out_specs=[rowblk(MIX_W, 0), rowblk(SSM_INNER, 0), rowblk(2 * LANE, 0), rowblk(LANE, 0),
                   rowblk(SSM_INNER, 0), pl.BlockSpec((1, SSM_INNER, SSM_STATE), lambda i: (i, 0, 0))] + g_out,
        out_shape=[jax.ShapeDtypeStruct((m, MIX_W), MXU_DTYPE), jax.ShapeDtypeStruct((m, SSM_INNER), F32),
                   jax.ShapeDtypeStruct((m, 2 * LANE), F32), jax.ShapeDtypeStruct((m, LANE), F32),
                   jax.ShapeDtypeStruct((m, SSM_INNER), F32),
                   jax.ShapeDtypeStruct((nb, SSM_INNER, SSM_STATE), F32)] + g_shape,
        scratch_shapes=[pltpu.VMEM((HALO + BLOCK, SSM_INNER), F32), pltpu.VMEM((HALO + BLOCK, 2 * LANE), F32),
                        pltpu.VMEM((SSM_INNER, SSM_STATE), F32)] + g_scratch,
        name=name, compiler_params=_params("arbitrary"),
    )(proj, proj, proj, proj, att, cw_x, cw_bc, cb_x, cb_bc, dt_bias, a_log, d_exp, norm_w, expand, expand_t, *gather)
    return outs[:6], outs[6:]


def ssd_bwd(proj, dmix, dq, dk, dv, pre_x, pre_bc, dt, y_pre, states, cw_x, cw_bc, dt_bias, a_log, d_exp, norm_w,
            carry=(), name="ssd_bwd"):
    m = proj.shape[0]
    nb = m // BLOCK
    expand = _head_expand()
    expand_t = expand.T

    def body(do0_ref, do1_ref, z_ref, xs_ref, xsp_ref, bc_ref, bcp_ref, dtr_ref, prex_ref, prebc_ref, dt_ref, ypre_ref,
             st_ref, dq_ref, dk_ref, dv_ref,
             cwx_ref, cwbc_ref, dtb_ref, alog_ref, dexp_ref, nw_ref, e_ref, et_ref,
             dproj_ref, dcwx_ref, dcwbc_ref, dcbx_ref, dcbbc_ref, ddtb_ref, dalog_ref,
             dd_ref, dnw_ref,
             dstate, hnext, tx, tbc, sx, sbc, dlane):

        def put(col, value):
            dproj_ref[:, col:col + value.shape[1]] = value.astype(dproj_ref.dtype)

        put(COL_Q, dq_ref[...])
        put(COL_K, dk_ref[...])
        put(COL_V, dv_ref[...])
        i = pl.program_id(0)
        c = nb - 1 - i

        @pl.when(i == 0)
        def _():
            dstate[...] = jnp.zeros_like(dstate)
            hnext[...] = jnp.zeros_like(hnext)
            tx[BLOCK:BLOCK + HALO, :] = jnp.zeros((HALO, SSM_INNER), F32)
            tbc[BLOCK:BLOCK + HALO, :] = jnp.zeros((HALO, 2 * LANE), F32)
            dlane[...] = jnp.zeros_like(dlane)
            for r in (dcwx_ref, dcwbc_ref, dcbx_ref, dcbbc_ref, ddtb_ref, dalog_ref, dd_ref, dnw_ref):
                r[...] = jnp.zeros_like(r)

        e = e_ref[...]
        et = et_ref[...]
        pre_x = prex_ref[...]
        pre_bc = prebc_ref[...]
        sig_x = _sigmoid(pre_x)
        sig_bc = _sigmoid(pre_bc)
        xc = pre_x * sig_x
        bcv = pre_bc * sig_bc
        dt = dt_ref[...]
        a, a_cs, lower = _ssd_decays(dt, alog_ref[...])
        a_cs_t = a_cs.T
        es_full = _dot_hi(jnp.exp(a_cs), e)
        ed_full = _dot_hi(jnp.exp(a_cs[BLOCK - 1:BLOCK, :] - a_cs), e)
        dt_full = _dot_hi(dt, e)
        end_col = jnp.exp(a_cs_t[:, BLOCK - 1:BLOCK])
        dec_full = _dot_hi(et, jnp.broadcast_to(end_col, (LANE, SSM_STATE)), exact="b")
        dexp = dexp_ref[...]

        z = z_ref[...]
        zs = _sigmoid(z)
        sz = z * zs
        y_pre = ypre_ref[...]
        gt = y_pre * sz
        do = jnp.concatenate([do0_ref[...], do1_ref[...]], axis=1)
        nw = nw_ref[...]
        dgt = []
        dnw = []
        for g in range(SSM_GROUPS):
            gs = slice(GROUP_W * g, GROUP_W * (g + 1))
            gg = gt[:, gs]
            r = lax.rsqrt(jnp.mean(gg * gg, -1, keepdims=True) + EPS)
            gn = do[:, gs] * nw[:, gs]
            dgt.append(r * gn - gg * ((r * r * r) * jnp.mean(gg * gn, -1, keepdims=True)))
            dnw.append(jnp.sum(do[:, gs] * (gg * r), axis=0, keepdims=True))
        dgt = jnp.concatenate(dgt, axis=1)
        dnw_ref[...] += jnp.concatenate(dnw, axis=1)
        dy = dgt * sz
        put(COL_Z, dgt * y_pre * (zs * (1.0 + z * (1.0 - zs))))
        dlane[...] += jnp.sum(dy * xc, axis=0, keepdims=True)
        xd = xc * dt_full

        lane_id = lax.broadcasted_iota(jnp.int32, (BLOCK, LANE), 1)
        sub_id = lax.broadcasted_iota(jnp.int32, (LANE, BLOCK), 0)
        ds_to = jnp.zeros((BLOCK, LANE), F32)
        ds_from_t = jnp.zeros((LANE, BLOCK), F32)
        dxd_parts, inter_parts = [], []
        dbs, dcs = [], []
        for g in range(SSM_GROUPS):
            gs = slice(GROUP_W * g, GROUP_W * (g + 1))
            b_g = bcv[:, SSM_STATE * g:SSM_STATE * (g + 1)]
            c_g = bcv[:, LANE + SSM_STATE * g:LANE + SSM_STATE * (g + 1)]
            cb = _dot_nt(c_g, b_g)
            dcb = jnp.zeros((BLOCK, BLOCK), F32)
            dxd_h = []
            for hh in range(HEADS_PER_GROUP):
                h = g * HEADS_PER_GROUP + hh
                hs = slice(SSM_HEAD_DIM * h, SSM_HEAD_DIM * (h + 1))
                lm = _decay_matrix(a_cs, a_cs_t, h, lower)
                dy_h = dy[:, hs]
                gl = _dot_nt(dy_h, xd[:, hs]) * lm
                dcb = dcb + gl
                e_h = gl * cb
                ds_to = ds_to + jnp.where(lane_id == h, jnp.sum(e_h, axis=-1, keepdims=True), 0.0)
                ds_from_t = ds_from_t + jnp.where(sub_id == h, jnp.sum(e_h, axis=0, keepdims=True), 0.0)
                dxd_h.append(_dot_tn(cb * lm, dy_h))
            h_g = st_ref[0, gs, :]
            dh_g = dstate[gs, :]
            dys_g = dy[:, gs] * es_full[:, gs]
            xde_g = xd[:, gs] * ed_full[:, gs]
            dcs.append(_dot(dcb, b_g) + _dot(dys_g, h_g))
            dbs.append(_dot_tn(dcb, c_g) + _dot(xde_g, dh_g))
            y_off = _dot_nt(c_g, h_g) * es_full[:, gs]
            dxd_state = _dot_nt(b_g, dh_g) * ed_full[:, gs]
            inter_parts.append(dy[:, gs] * y_off - xd[:, gs] * dxd_state)
            dxd_parts.append(jnp.concatenate(dxd_h, axis=1) + dxd_state)
            dstate[gs, :] = dh_g * dec_full[gs, :] + _dot_tn(dys_g, c_g)
            if g == 0:
                end_dot = hnext[gs, :] * dh_g
            else:
                end_dot = jnp.concatenate([end_dot, hnext[gs, :] * dh_g], axis=0)
        dxd = jnp.concatenate(dxd_parts, axis=1)
        hnext[...] = st_ref[0]

        ds = ds_to - ds_from_t.T + _dot_hi(jnp.concatenate(inter_parts, axis=1), et)
        ds_end = jnp.sum(_dot_tn_hi(end_dot, et), axis=0, keepdims=True)
        rows_l = lax.broadcasted_iota(jnp.int32, (BLOCK, LANE), 0)
        ds = ds + jnp.where(rows_l == BLOCK - 1, ds_end, 0.0)
        row = lax.broadcasted_iota(jnp.int32, (BLOCK, BLOCK), 0)
        col = lax.broadcasted_iota(jnp.int32, (BLOCK, BLOCK), 1)
        dadt = _dot_hi(col >= row, ds, exact="b")
        ddt = dadt * a + _dot_hi(dxd * xc, et)
        dalog_ref[...] += jnp.sum(dadt * dt, axis=0, keepdims=True) * a
        rows = _row_ids(c * BLOCK, BLOCK, LANE)
        lanes = lax.broadcasted_iota(jnp.int32, (BLOCK, LANE), 1)
        live = jnp.logical_and(rows >= FRONT_PAD, lanes < SSM_HEADS)
        ddt_raw = jnp.where(live, ddt * _sigmoid(dtr_ref[...] + dtb_ref[...]), 0.0)
        put(COL_DT, ddt_raw)
        ddtb_ref[...] += jnp.sum(ddt_raw, axis=0, keepdims=True)

        dxc = dxd * dt_full + dy * dexp
        dpre_x = dxc * (sig_x * (1.0 + pre_x * (1.0 - sig_x)))
        dpre_bc = jnp.concatenate(dbs + dcs, axis=1) * (sig_bc * (1.0 + pre_bc * (1.0 - sig_bc)))
        dcbx_ref[...] += jnp.sum(dpre_x, axis=0, keepdims=True)
        dcbbc_ref[...] += jnp.sum(dpre_bc, axis=0, keepdims=True)
        keep_x = _row_ids(c * BLOCK, BLOCK, SSM_INNER) >= FRONT_PAD
        keep_bc = _row_ids(c * BLOCK, BLOCK, 2 * LANE) >= FRONT_PAD
        prev_live = (c > 0).astype(F32)
        for (dpre, t_ref, s_ref, cur_ref, prv_ref, w_ref, dw_ref, col, keep) in (
                (dpre_x, tx, sx, xs_ref, xsp_ref, cwx_ref, dcwx_ref, COL_XS, keep_x),
                (dpre_bc, tbc, sbc, bc_ref, bcp_ref, cwbc_ref, dcwbc_ref, COL_BC, keep_bc)):
            t_ref[0:BLOCK, :] = dpre
            acc = w_ref[0:1, :] * t_ref[pl.ds(SSM_CONV - 1, BLOCK), :]
            for j in range(1, SSM_CONV):
                acc = acc + w_ref[j:j + 1, :] * t_ref[pl.ds(SSM_CONV - 1 - j, BLOCK), :]
            put(col, jnp.where(keep, acc, 0.0))
            t_ref[BLOCK:BLOCK + HALO, :] = dpre[0:HALO, :]
            s_ref[0:HALO, :] = prv_ref[BLOCK - HALO:BLOCK, :] * prev_live
            s_ref[HALO:HALO + BLOCK, :] = cur_ref[...]
            first = HALO - (SSM_CONV - 1)
            for j in range(SSM_CONV):
                dw_ref[j:j + 1, :] += jnp.sum(dpre * s_ref[pl.ds(first + j, BLOCK), :], axis=0, keepdims=True)

        @pl.when(i == nb - 1)
        def _():
            dd_ref[...] = _dot_hi(jnp.broadcast_to(dlane[...], (HALO, SSM_INNER)), et)[0:1, :]

    blk = lambda i: nb - 1 - i
    prv = lambda i: jnp.maximum(nb - 2 - i, 0)
    full = lambda s: pl.BlockSpec(s, lambda i: (0,) * len(s))
    rowblk = lambda w, cidx: pl.BlockSpec((BLOCK, w), lambda i: (blk(i), cidx))
    prvblk = lambda w, cidx: pl.BlockSpec((BLOCK, w), lambda i: (prv(i), cidx))
    body, ex_in, ex_out, ex_shape, ex_scratch = _with_exchange(body, 24, 9, carry, nb)
    outs = pl.pallas_call(
        body, grid=(nb,),
        in_specs=[rowblk(GROUP_W, Q_W // GROUP_W), rowblk(GROUP_W, Q_W // GROUP_W + 1),
                  rowblk(SSM_INNER, COL_Z // SSM_INNER),
                  rowblk(SSM_INNER, COL_XS // SSM_INNER), prvblk(SSM_INNER, COL_XS // SSM_INNER),
                  rowblk(2 * LANE, COL_BC // (2 * LANE)), prvblk(2 * LANE, COL_BC // (2 * LANE)),
                  rowblk(LANE, COL_DT // LANE),
                  rowblk(SSM_INNER, 0), rowblk(2 * LANE, 0), rowblk(LANE, 0), rowblk(SSM_INNER, 0),
                  pl.BlockSpec((1, SSM_INNER, SSM_STATE), lambda i: (blk(i), 0, 0)),
                  rowblk(Q_W, 0), rowblk(KV_W, 0), rowblk(KV_W, 0),
                  full((SSM_CONV, SSM_INNER)), full((SSM_CONV, 2 * LANE)), full((1, LANE)), full((1, LANE)),
                  full((1, SSM_INNER)), full((1, SSM_INNER)), full((LANE, SSM_INNER)), full((SSM_INNER, LANE))] + ex_in,
        out_specs=[rowblk(PROJ_W, 0),
                   full((SSM_CONV, SSM_INNER)), full((SSM_CONV, 2 * LANE)), full((1, SSM_INNER)), full((1, 2 * LANE)),
                   full((1, LANE)), full((1, LANE)), full((1, LANE)), full((1, SSM_INNER))] + ex_out,
        out_shape=[jax.ShapeDtypeStruct((m, PROJ_W), MXU_DTYPE),
                   jax.ShapeDtypeStruct((SSM_CONV, SSM_INNER), F32), jax.ShapeDtypeStruct((SSM_CONV, 2 * LANE), F32),
                   jax.ShapeDtypeStruct((1, SSM_INNER), F32), jax.ShapeDtypeStruct((1, 2 * LANE), F32),
                   jax.ShapeDtypeStruct((1, LANE), F32), jax.ShapeDtypeStruct((1, LANE), F32),
                   jax.ShapeDtypeStruct((1, LANE), F32), jax.ShapeDtypeStruct((1, SSM_INNER), F32)] + ex_shape,
        scratch_shapes=[pltpu.VMEM((SSM_INNER, SSM_STATE), F32), pltpu.VMEM((SSM_INNER, SSM_STATE), F32),
                        pltpu.VMEM((BLOCK + HALO, SSM_INNER), F32), pltpu.VMEM((BLOCK + HALO, 2 * LANE), F32),
                        pltpu.VMEM((HALO + BLOCK, SSM_INNER), F32), pltpu.VMEM((HALO + BLOCK, 2 * LANE), F32),
                        pltpu.VMEM((1, SSM_INNER), F32)] + ex_scratch,
        name=name, compiler_params=_params("arbitrary"),
    )(dmix, dmix, proj, proj, proj, proj, proj, proj, pre_x, pre_bc, dt, y_pre, states, dq, dk, dv,
      cw_x, cw_bc, dt_bias, a_log, d_exp, norm_w, expand, expand_t, *carry)
    return outs[:9], outs[9:]


CONF_HALO = 32
SUBLANES = 8


def _for_each_window(s, offsets, rows, fn):
    total = s.shape[0]
    assert max(offsets) + rows <= total
    for b in range(SUBLANES):
        offs = [o for o in offsets if o % SUBLANES == b]
        if not offs:
            continue
        rot = s if b == 0 else pltpu.roll(s, total - b, 0)
        for o in offs:
            fn(o, rot[o - b:o - b + rows])


def _glu_masked(v, first_row):
    a = v[:, :D_MODEL]
    s = _sigmoid(v[:, D_MODEL:])
    rows = _row_ids(first_row, v.shape[0], D_MODEL)
    return jnp.where(rows >= FRONT_PAD, a * s, 0.0), a, s


def _layer_norm_stats(c):
    mu = jnp.mean(c, -1, keepdims=True)
    xc = c - mu
    rstd = lax.rsqrt(jnp.mean(xc * xc, -1, keepdims=True) + LN_EPS)
    return xc * rstd, rstd


def conformer_mid_fwd(v, dw_w, dw_b, ln_g, ln_b, gather=(), name="conf_mid_fwd"):
    m = v.shape[0]
    nb = m // BLOCK
    kpad = dw_w.shape[0]

    def body(vc_ref, vp_ref, w_ref, b_ref, g_ref, beta_ref, c_ref, s_ref):
        i = pl.program_id(0)
        g_prev, _, _ = _glu_masked(vp_ref[BLOCK - CONF_HALO:BLOCK, :], (i - 1) * BLOCK + BLOCK - CONF_HALO)
        g_cur, _, _ = _glu_masked(vc_ref[...], i * BLOCK)
        sg = jnp.concatenate([g_prev * (i > 0).astype(F32), g_cur], axis=0)
        first = CONF_HALO - (CONF_KERNEL - 1)
        acc = [jnp.broadcast_to(b_ref[...], (BLOCK, D_MODEL))]

        def tap(off, win):
            j = off - first
            acc[0] = acc[0] + w_ref[j:j + 1, :] * win

        _for_each_window(sg, [first + j for j in range(CONF_KERNEL)], BLOCK, tap)
        acc = acc[0]
        c_ref[...] = acc
        xhat, _ = _layer_norm_stats(acc)
        nrm = xhat * g_ref[...] + beta_ref[...]
        s_ref[...] = (nrm * _sigmoid(nrm)).astype(s_ref.dtype)

    full = lambda s: pl.BlockSpec(s, lambda i: (0,) * len(s))
    body, g_in, g_out, g_shape, g_scratch = _with_gather(body, 6, 2, gather, nb)
    outs = pl.pallas_call(
        body, grid=(nb,),
        in_specs=[pl.BlockSpec((BLOCK, 2 * D_MODEL), lambda i: (i, 0)),
                  pl.BlockSpec((BLOCK, 2 * D_MODEL), lambda i: (jnp.maximum(i - 1, 0), 0)),
                  full((kpad, D_MODEL)), full((1, D_MODEL)), full((1, D_MODEL)), full((1, D_MODEL))] + g_in,
        out_specs=[pl.BlockSpec((BLOCK, D_MODEL), lambda i: (i, 0)),
                   pl.BlockSpec((BLOCK, D_MODEL), lambda i: (i, 0))] + g_out,
        out_shape=[jax.ShapeDtypeStruct((m, D_MODEL), F32), jax.ShapeDtypeStruct((m, D_MODEL), MXU_DTYPE)] + g_shape,
        scratch_shapes=g_scratch, name=name, compiler_params=_params("arbitrary"),
    )(v, v, dw_w, dw_b, ln_g, ln_b, *gather)
    return outs[:2], outs[2:]


def conformer_ln_bwd(ds, c, ln_g, ln_b, name="conf_ln_bwd"):
    m, d = c.shape
    tm = ROW_TILE

    def body(ds_ref, c_ref, g_ref, beta_ref, dc_ref, dg_ref, db_ref):
        @pl.when(pl.program_id(0) == 0)
        def _():
            dg_ref[...] = jnp.zeros_like(dg_ref)
            db_ref[...] = jnp.zeros_like(db_ref)

        xhat, rstd = _layer_norm_stats(c_ref[...])
        g = g_ref[...]
        nrm = xhat * g + beta_ref[...]
        sg = _sigmoid(nrm)
        dn = ds_ref[...] * (sg * (1.0 + nrm * (1.0 - sg)))
        db_ref[...] += jnp.sum(dn, axis=0, keepdims=True)
        dg_ref[...] += jnp.sum(dn * xhat, axis=0, keepdims=True)
        dx = dn * g
        dc_ref[...] = rstd * (dx - jnp.mean(dx, -1, keepdims=True) - xhat * jnp.mean(dx * xhat, -1, keepdims=True))

    row = pl.BlockSpec((tm, d), lambda i: (i, 0))
    vec = pl.BlockSpec((1, d), lambda i: (0, 0))
    return pl.pallas_call(
        body, grid=(m // tm,), in_specs=[row, row, vec, vec], out_specs=[row, vec, vec],
        out_shape=[jax.ShapeDtypeStruct((m, d), F32), jax.ShapeDtypeStruct((1, d), F32), jax.ShapeDtypeStruct((1, d), F32)],
        name=name, compiler_params=_params("arbitrary"),
    )(ds, c, ln_g, ln_b)


def conformer_conv_bwd(dc, v, dw_w, carry=(), name="conf_conv_bwd"):
    m = v.shape[0]
    nb = m // BLOCK
    kpad = dw_w.shape[0]

    def body(dcc_ref, dcn_ref, vc_ref, vp_ref, w_ref, dv_ref, dw_ref, db_ref, dvb_ref):
        i = pl.program_id(0)

        @pl.when(i == 0)
        def _():
            dw_ref[...] = jnp.zeros_like(dw_ref)
            db_ref[...] = jnp.zeros_like(db_ref)
            dvb_ref[...] = jnp.zeros_like(dvb_ref)

        dc_cur = dcc_ref[...]
        tg = jnp.concatenate([dc_cur, dcn_ref[0:CONF_HALO, :] * (i < nb - 1).astype(F32)], axis=0)
        g_prev, _, _ = _glu_masked(vp_ref[BLOCK - CONF_HALO:BLOCK, :], (i - 1) * BLOCK + BLOCK - CONF_HALO)
        g_cur, a, s = _glu_masked(vc_ref[...], i * BLOCK)
        sg = jnp.concatenate([g_prev * (i > 0).astype(F32), g_cur], axis=0)
        db_ref[...] += jnp.sum(dc_cur, axis=0, keepdims=True)
        first = CONF_HALO - (CONF_KERNEL - 1)
        dg_acc = [jnp.zeros((BLOCK, D_MODEL), F32)]

        def tap_dg(off, win):
            j = CONF_KERNEL - 1 - off
            dg_acc[0] = dg_acc[0] + w_ref[j:j + 1, :] * win

        def tap_dw(off, win):
            j = off - first
            dw_ref[j:j + 1, :] += jnp.sum(dc_cur * win, axis=0, keepdims=True)

        _for_each_window(tg, list(range(CONF_KERNEL)), BLOCK, tap_dg)
        _for_each_window(sg, [first + j for j in range(CONF_KERNEL)], BLOCK, tap_dw)
        dg = dg_acc[0]
        rows = _row_ids(i * BLOCK, BLOCK, D_MODEL)
        dg = jnp.where(rows >= FRONT_PAD, dg, 0.0)
        da = dg * s
        dbv = dg * a * (s * (1.0 - s))
        dv = jnp.concatenate([da, dbv], axis=1)
        dv_ref[...] = dv.astype(dv_ref.dtype)
        dvb_ref[...] += jnp.sum(dv, axis=0, keepdims=True)

    full = lambda s: pl.BlockSpec(s, lambda i: (0,) * len(s))
    body, ex_in, ex_out, ex_shape, ex_scratch = _with_exchange(body, 5, 4, carry, nb)
    outs = pl.pallas_call(
        body, grid=(nb,),
        in_specs=[pl.BlockSpec((BLOCK, D_MODEL), lambda i: (i, 0)),
                  pl.BlockSpec((BLOCK, D_MODEL), lambda i: (jnp.minimum(i + 1, nb - 1), 0)),
                  pl.BlockSpec((BLOCK, 2 * D_MODEL), lambda i: (i, 0)),
                  pl.BlockSpec((BLOCK, 2 * D_MODEL), lambda i: (jnp.maximum(i - 1, 0), 0)),
                  full((kpad, D_MODEL))] + ex_in,
        out_specs=[pl.BlockSpec((BLOCK, 2 * D_MODEL), lambda i: (i, 0)), full((kpad, D_MODEL)),
                   full((1, D_MODEL)), full((1, 2 * D_MODEL))] + ex_out,
        out_shape=[jax.ShapeDtypeStruct((m, 2 * D_MODEL), MXU_DTYPE), jax.ShapeDtypeStruct((kpad, D_MODEL), F32),
                   jax.ShapeDtypeStruct((1, D_MODEL), F32), jax.ShapeDtypeStruct((1, 2 * D_MODEL), F32)] + ex_shape,
        scratch_shapes=ex_scratch, name=name, compiler_params=_params("arbitrary"),
    )(dc, dc, v, v, dw_w, *carry)
    return outs[:4], outs[4:]


def _row(v, width=None):
    v = v.reshape(1, -1).astype(F32)
    if width is not None and v.shape[1] < width:
        v = jnp.pad(v, ((0, 0), (0, width - v.shape[1])))
    return v


def _w_in_to_kernel(w):
    pad = jnp.zeros((w.shape[0], PROJ_W - COL_DT - SSM_HEADS), w.dtype)
    return jnp.concatenate([w[:, 768:1792], w[:, 1792:2816], w[:, 0:512], w[:, 2816:3072], w[:, 512:640],
                            w[:, 640:768], w[:, 3072:3088], pad], axis=1)


def _w_in_from_kernel(g):
    return jnp.concatenate([g[:, COL_Q:COL_Q + Q_W], g[:, COL_K:COL_K + KV_W], g[:, COL_V:COL_V + KV_W],
                            g[:, COL_Z:COL_Z + SSM_INNER], g[:, COL_XS:COL_XS + SSM_INNER],
                            g[:, COL_BC:COL_BC + 2 * LANE], g[:, COL_DT:COL_DT + SSM_HEADS]], axis=1)


def even_fwd(h, p, gather_att=(), gather_ssd=()):
    u = rms_fwd(h, p["norm"])
    proj = matmul(u, p["w_in"], name="mm_proj")
    att, got_att = attention_fwd(proj, p["q_norm"], p["k_norm"], p["sinks"], gather=list(gather_att))
    if p["w_out"] is None:
        p["w_out"] = got_att[0]
    (mix, pre_x, pre_bc, dt, y_pre, states), got_ssd = ssd_fwd(
        proj, att, p["cw_x"], p["cw_bc"], p["cb_x"], p["cb_bc"], p["dt_bias"], p["a_log"], p["d_exp"], p["ssm_norm"],
        gather=list(gather_ssd))
    out = matmul(mix, p["w_out"], b_kind="rowshard", layer=p["layer"], epilogue="resid", extra=h, name="mm_mix_out")
    return out, (h, u, proj, mix, pre_x, pre_bc, dt, y_pre, states), got_att, got_ssd


def even_bwd(dh, p, saved, carry_att=(), carry_ssd=(), send_w_in=False):
    h, u, proj, mix, pre_x, pre_bc, dt, y_pre, states = saved
    dmix = matmul(dh, p["w_out"], b_kind="rowshard", layer=p["layer"], trans_b=True, name="mm_dmix")
    dw_out = matmul_tn(mix, dh, ti=512, tn=D_MODEL, out_dtype=GRAD_WIRE_DTYPE, name="mm_dw_out")
    dw_out = dw_out.reshape(N_DEV, MIX_W // N_DEV, D_MODEL)
    (dq, dk, dv, dqw, dkw, dsk), got_att = attention_bwd(proj, dmix, p["q_norm"], p["k_norm"], p["sinks"],
                                                         carry=list(carry_att))
    (dproj, dcwx, dcwbc, dcbx, dcbbc, ddtb, dalog, dd, dnw), got_ssd = ssd_bwd(
        proj, dmix, dq, dk, dv, pre_x, pre_bc, dt, y_pre, states, p["cw_x"], p["cw_bc"], p["dt_bias"], p["a_log"],
        p["d_exp"], p["ssm_norm"], carry=[dw_out] + list(carry_ssd))
    dw_in = matmul_tn(u, dproj, ti=512, tn=PROJ_W, name="mm_dw_in")
    dw_in = _to_shards(_w_in_from_kernel(dw_in), 1).astype(GRAD_WIRE_DTYPE)
    if send_w_in:
        du, (dw_in,) = matmul(dproj, p["w_in"], trans_b=True, carry=[dw_in], name="mm_du_in")
    else:
        du = matmul(dproj, p["w_in"], trans_b=True, name="mm_du_in")
    dh_in, dg = rms_bwd(h, p["norm"], du, dh)
    grads = dict(norm=dg, w_in=dw_in, cw_x=dcwx, cw_bc=dcwbc, cb_x=dcbx, cb_bc=dcbbc, dt_bias=ddtb,
                 a_log=dalog, d_skip=dd, ssm_norm=dnw, q_norm=dqw, k_norm=dkw, sinks=dsk)
    return dh_in, grads, got_att, got_ssd


def conf_fwd(h, p, gather=()):
    v, u = mlp_up(h, p["norm"], p["pw1_w"], p["layer"], bias=p["pw1_b"], relu2=False, out_dtype=F32, name="mm_pw1")
    (c, s), got = conformer_mid_fwd(v, p["dw_w"], p["dw_b"], p["ln_g"], p["ln_b"], gather=list(gather))
    out = matmul(s, p["pw2_w"], b_kind="rowshard", layer=p["layer"], bias=p["pw2_b"], epilogue="resid", extra=h,
                 name="mm_pw2")
    return out, (h, u, v, c, s), got


def conf_bwd(dh, p, saved, carry=()):
    h, u, v, c, s = saved
    dpw2_b = col_sum(dh)
    ds = matmul(dh, p["pw2_w"], b_kind="rowshard", layer=p["layer"], trans_b=True, name="mm_ds")
    dpw2_w = matmul_tn(s, dh, ti=D_MODEL, tn=D_MODEL, out_dtype=GRAD_WIRE_DTYPE, name="mm_dpw2")
    dpw2_w = dpw2_w.reshape(N_DEV, D_MODEL // N_DEV, D_MODEL)
    dc, dln_g, dln_b = conformer_ln_bwd(ds, c, p["ln_g"], p["ln_b"])
    (dv, ddw_w, ddw_b, dpw1_b), got = conformer_conv_bwd(dc, v, p["dw_w"], carry=[dpw2_w] + list(carry))
    dpw1_w = mlp_dw_up(u, dv, name="mm_dpw1")
    dh_in, dg = mlp_du_rms_bwd(dv, p["pw1_w"], p["layer"], h, p["norm"], dh, name="mm_du_pw1")
    grads = dict(norm=dg, pw1_w=dpw1_w, pw1_b=dpw1_b, dw_w=ddw_w, dw_b=ddw_b, ln_g=dln_g, ln_b=dln_b, pw2_b=dpw2_b)
    return dh_in, grads, got


def mlp_fwd(h, p):
    act, u = mlp_up(h, p["norm"], p["w_up"], p["layer"])
    out = matmul(act, p["w_down"], b_kind="rowshard", layer=p["layer"], epilogue="resid", extra=h, name="mm_down")
    return out, (h, u, act)


def mlp_bwd(dh, p, saved):
    h, u, act = saved
    da = mlp_dact(dh, p["w_down"], act, p["layer"])
    dw_down = mlp_dw_down(act, dh).reshape(N_DEV, FF_BLOCK, D_MODEL)
    dw_up = mlp_dw_up(u, da)
    dh_in, dg = mlp_du_rms_bwd(da, p["w_up"], p["layer"], h, p["norm"], dh)
    return dh_in, dict(norm=dg, w_up=dw_up, w_down=dw_down)


def local_step(x, target, w, shards, first):
    n_even, n_odd = (DEPTH + 1) // 2, DEPTH // 2
    h = jnp.concatenate([jnp.zeros((FRONT_PAD, D_MODEL), F32), w["meta_tokens"].astype(F32), x], axis=0)
    even_p, odd_p, mlp_p = [None] * n_even, [None] * n_odd, [None] * DEPTH

    def even_params(i, g):
        cw = w["ssm_conv_w"][i]
        return dict(
            layer=0, norm=_row(w["mix_norm_even"][i]), w_in=_w_in_to_kernel(_from_shards(g[0][:, 0], 1)), w_out=g[1],
            cw_x=cw[:, :SSM_INNER], cw_bc=cw[:, SSM_INNER:], cb_x=_row(w["ssm_conv_b"][i][:SSM_INNER]),
            cb_bc=_row(w["ssm_conv_b"][i][SSM_INNER:]), dt_bias=_row(w["dt_bias"][i], LANE),
            a_log=_row(w["a_log"][i], LANE), d_exp=_row(jnp.repeat(w["d_skip"][i], SSM_HEAD_DIM)),
            ssm_norm=_row(w["ssm_norm_w"][i]), q_norm=_row(jnp.tile(w["q_norm"][i], ATT_HEADS)),
            k_norm=_row(jnp.tile(w["k_norm"][i], ATT_KV_HEADS)), sinks=w["sinks"][i].astype(F32))

    def odd_params(i, g):
        return dict(
            layer=0, norm=_row(w["mix_norm_odd"][i]), pw1_w=g[0], pw1_b=_row(w["pw1_b"][i]),
            dw_w=jnp.pad(w["dw_w"][i], ((0, CONF_HALO - CONF_KERNEL), (0, 0))), dw_b=_row(w["dw_b"][i]),
            ln_g=_row(w["ln_g"][i]), ln_b=_row(w["ln_b"][i]), pw2_w=g[1], pw2_b=_row(w["pw2_b"][i]))

    gathered = {}
    early = {}
    tape = []
    for layer in range(DEPTH):
        i = layer // 2
        nxt = ()
        if layer + 1 < DEPTH and layer + 1 not in gathered:
            nxt = shards[layer + 1][len(early.get(layer + 1, ())):]
        if layer == 0:
            even_p[0] = even_params(0, [first[0], None])
            ahead = shards[2][:2] if DEPTH > 2 else []
            h, saved, got_att, gathered[1] = even_fwd(h, even_p[0], gather_att=shards[0][1:] + ahead, gather_ssd=nxt)
            mlp_w = got_att[1:3]
            if ahead:
                early[2] = got_att[3:]
        else:
            g = gathered.pop(layer)
            mlp_w = g[2:]
            if layer % 2 == 0:
                even_p[i] = even_params(i, g[:2])
                h, saved, got, _ = even_fwd(h, even_p[i], gather_att=nxt)
            else:
                odd_p[i] = odd_params(i, g[:2])
                h, saved, got = conf_fwd(h, odd_p[i], gather=nxt)
            if nxt:
                gathered[layer + 1] = list(early.pop(layer + 1, ())) + list(got)
        tape.append(saved)
        mlp_p[layer] = dict(layer=0, norm=_row(w["mlp_norm"][layer]), w_up=mlp_w[0], w_down=mlp_w[1])
        h, saved = mlp_fwd(h, mlp_p[layer])
        tape.append(saved)
    dh, loss_row = loss_fwd_bwd(h, target)

    ge = [None] * n_even
    go = [None] * n_odd
    gm = [None] * DEPTH
    received = {n: [None] * shape[0] for n, shape, _ in PARAMS if n in MATMUL_WEIGHTS}
    pending = []

    def store(tags, arrays):
        for (n, l), a in zip(tags, arrays):
            received[n][l] = a

    for layer in reversed(range(DEPTH)):
        i = layer // 2
        dh, gm[layer] = mlp_bwd(dh, mlp_p[layer], tape.pop())
        mlp_tags = [("w_up", layer), ("w_down", layer)]
        mlp_parts = [gm[layer]["w_up"], gm[layer]["w_down"]]
        if layer % 2 == 0:
            riders, pending = pending, []
            dh, ge[i], got_att, got_ssd = even_bwd(dh, even_p[i], tape.pop(), carry_att=mlp_parts,
                                                   carry_ssd=[a for _, _, a in riders], send_w_in=layer == 0)
            store(mlp_tags, got_att)
            store([("w_out", i)] + [(n, l) for n, l, _ in riders], got_ssd)
            if layer == 0:
                store([("w_in", i)], [ge[i]["w_in"]])
            else:
                pending.append(("w_in", i, ge[i]["w_in"]))
        else:
            dh, go[i], got = conf_bwd(dh, odd_p[i], tape.pop(), carry=mlp_parts)
            store([("pw2_w", i)] + mlp_tags, got)
            pending.append(("pw1_w", i, go[i]["pw1_w"]))

    stack = lambda gs, f: jnp.stack([f(g) for g in gs])
    grads = dict(
        meta_tokens=dh[FRONT_PAD:BLOCK],
        mix_norm_even=stack(ge, lambda g: g["norm"][0]),
        ssm_conv_w=stack(ge, lambda g: jnp.concatenate([g["cw_x"], g["cw_bc"]], axis=1)),
        ssm_conv_b=stack(ge, lambda g: jnp.concatenate([g["cb_x"][0], g["cb_bc"][0]])),
        dt_bias=stack(ge, lambda g: g["dt_bias"][0, :SSM_HEADS]),
        a_log=stack(ge, lambda g: g["a_log"][0, :SSM_HEADS]),
        d_skip=stack(ge, lambda g: g["d_skip"][0, :SSM_HEADS]),
        ssm_norm_w=stack(ge, lambda g: g["ssm_norm"][0]),
        q_norm=stack(ge, lambda g: g["q_norm"][0, :HEAD_DIM]),
        k_norm=stack(ge, lambda g: g["k_norm"][0, :HEAD_DIM]),
        sinks=stack(ge, lambda g: g["sinks"][0, :ATT_HEADS]),
        mix_norm_odd=stack(go, lambda g: g["norm"][0]),
        pw1_b=stack(go, lambda g: g["pw1_b"][0]),
        dw_w=stack(go, lambda g: g["dw_w"][:CONF_KERNEL]),
        dw_b=stack(go, lambda g: g["dw_b"][0]),
        ln_g=stack(go, lambda g: g["ln_g"][0]),
        ln_b=stack(go, lambda g: g["ln_b"][0]),
        pw2_b=stack(go, lambda g: g["pw2_b"][0]),
        mlp_norm=stack(gm, lambda g: g["norm"][0]),
    )
    return loss_row[0, 0], dh[BLOCK:], grads, received, pending


PARAMS = (
    ("meta_tokens", (16, 1024), 1), ("mix_norm_even", (2, 1024), None), ("w_in", (2, 1024, 3088), 2),
    ("ssm_conv_w", (2, 4, 1280), 2), ("ssm_conv_b", (2, 1280), None), ("dt_bias", (2, 16), None),
    ("a_log", (2, 16), None), ("d_skip", (2, 16), None), ("ssm_norm_w", (2, 1024), None), ("q_norm", (2, 64), None),
    ("k_norm", (2, 64), None), ("sinks", (2, 8), None), ("w_out", (2, 1536, 1024), 1), ("mix_norm_odd", (2, 1024), 1),
    ("pw1_w", (2, 1024, 2048), 2), ("pw1_b", (2, 2048), 1), ("dw_w", (2, 31, 1024), 2), ("dw_b", (2, 1024), 1),
    ("ln_g", (2, 1024), 1), ("ln_b", (2, 1024), 1), ("pw2_w", (2, 1024, 1024), 1), ("pw2_b", (2, 1024), 1),
    ("mlp_norm", (4, 1024), None), ("w_up", (4, 1024, 4096), 2), ("w_down", (4, 4096, 1024), 1),
)
MATMUL_WEIGHTS = ("w_in", "w_out", "pw1_w", "pw2_w", "w_up", "w_down")
PACK_ROW_ALIGN = 16 * PACK_W


def _block_shape(shape, axis):
    if axis is None:
        return tuple(shape)
    return tuple(s // N_DEV if a == axis else s for a, s in enumerate(shape))


def _numel(shape):
    return math.prod(shape)


def _pack(arrays, dtype):
    flat = jnp.concatenate([a.reshape(-1).astype(dtype) for a in arrays])
    n = flat.shape[0]
    padded = -(-n // PACK_ROW_ALIGN) * PACK_ROW_ALIGN
    return jnp.pad(flat, (0, padded - n)).reshape(-1, PACK_W)


def _pack_rows(arrays_by_dev, dtype):
    flat = jnp.concatenate([a.reshape(N_DEV, -1).astype(dtype) for a in arrays_by_dev], axis=1)
    n = flat.shape[1]
    padded = -(-n // PACK_ROW_ALIGN) * PACK_ROW_ALIGN
    return jnp.pad(flat, ((0, 0), (0, padded - n))).reshape(N_DEV, -1, PACK_W)


def _to_shards(full, axis):
    shape = full.shape
    split = full.reshape(shape[:axis] + (N_DEV, shape[axis] // N_DEV) + shape[axis + 1:])
    return jnp.moveaxis(split, axis, 0)


def _from_shards(blocks, axis):
    moved = jnp.moveaxis(blocks, 0, axis)
    shape = moved.shape
    return moved.reshape(shape[:axis] + (shape[axis] * shape[axis + 1],) + shape[axis + 2:])


_MESH = pl.DeviceIdType.MESH
_ANY = pl.BlockSpec(memory_space=pl.ANY)


def _mesh_place():
    x, y, c = lax.axis_index("x"), lax.axis_index("y"), lax.axis_index("c")
    return x, y, c


def _peer(x, y, c, rel):
    dx, dy, dc = (rel >> 2) & 1, (rel >> 1) & 1, rel & 1
    return (x ^ dx if dx else x, y ^ dy if dy else y, c ^ dc if dc else c)


def _dev_index(x, y, c):
    return 4 * x + 2 * y + c


def all_gather_weights(bigs, small):
    nt = len(bigs)

    def body(*refs):
        big_refs, small_ref = refs[:nt], refs[nt]
        big_outs, small_out = refs[nt + 1:2 * nt + 1], refs[2 * nt + 1]
        send_sems, recv_sems, small_send, small_recv, local_sems = refs[2 * nt + 2:]
        x, y, c = _mesh_place()
        me = (x, y, c)
        sibling = (x, y, 1 - c)
        chips = [(1 - x, y), (x, 1 - y), (1 - x, 1 - y)]

        def big_copy(t, k, block, to, from_input=False):
            dst = big_outs[t].at[_dev_index(*block)]
            return pltpu.make_async_remote_copy(src_ref=big_refs[t] if from_input else dst, dst_ref=dst,
                                                send_sem=send_sems.at[t, k], recv_sem=recv_sems.at[t, k],
                                                device_id=to, device_id_type=_MESH)

        def small_copy(rel, block, to):
            return pltpu.make_async_remote_copy(src_ref=small_ref, dst_ref=small_out.at[_dev_index(*block)],
                                                send_sem=small_send.at[rel - 1], recv_sem=small_recv.at[rel - 1],
                                                device_id=to, device_id_type=_MESH)

        mine = [pltpu.make_async_copy(big_refs[t], big_outs[t].at[_dev_index(*me)], local_sems.at[t]) for t in range(nt)]
        mine.append(pltpu.make_async_copy(small_ref, small_out.at[_dev_index(*me)], local_sems.at[nt]))
        for cp in mine:
            cp.start()
        first = []
        for t in range(nt):
            first.append(big_copy(t, 0, me, sibling, from_input=True))
            first += [big_copy(t, 1 + j, me, (*chip, c), from_input=True) for j, chip in enumerate(chips)]
        for cp in first:
            cp.start()
        smalls = [small_copy(rel, me, _peer(x, y, c, rel)) for rel in range(1, N_DEV)]
        for cp in smalls:
            cp.start()
        passed = []
        for j, chip in enumerate(chips):
            for t in range(nt):
                big_copy(t, 1 + j, (*chip, c), me).wait_recv()
                fwd = big_copy(t, 4 + j, (*chip, c), sibling)
                fwd.start()
                passed.append(fwd)
        for t in range(nt):
            big_copy(t, 0, sibling, me).wait_recv()
            for j, chip in enumerate(chips):
                big_copy(t, 4 + j, (*chip, 1 - c), me).wait_recv()
        for rel in range(1, N_DEV):
            small_copy(rel, _peer(x, y, c, rel), me).wait_recv()
        for cp in first + passed + smalls:
            cp.wait_send()
        for cp in mine:
            cp.wait()

    return pl.pallas_call(
        body, in_specs=[_ANY] * (nt + 1), out_specs=[_ANY] * (nt + 1),
        out_shape=[jax.ShapeDtypeStruct((N_DEV,) + b.shape, b.dtype) for b in bigs]
        + [jax.ShapeDtypeStruct((N_DEV,) + small.shape, small.dtype)],
        scratch_shapes=[pltpu.SemaphoreType.DMA((nt, N_DEV - 1)), pltpu.SemaphoreType.DMA((nt, N_DEV - 1)),
                        pltpu.SemaphoreType.DMA((N_DEV - 1,)), pltpu.SemaphoreType.DMA((N_DEV - 1,)),
                        pltpu.SemaphoreType.DMA((nt + 1,))],
        name="all_gather_weights",
    )(*bigs, small)


def _gather_copies(in_refs, out_refs, send_sems, recv_sems, local_sems):
    x, y, c = _mesh_place()
    me = (x, y, c)
    sibling = (x, y, 1 - c)
    chips = [(1 - x, y), (x, 1 - y), (1 - x, 1 - y)]
    nt = len(in_refs)

    def copy(t, k, block, to, from_input=False):
        dst = out_refs[t].at[_dev_index(*block)]
        return pltpu.make_async_remote_copy(src_ref=in_refs[t] if from_input else dst, dst_ref=dst,
                                            send_sem=send_sems.at[t, k], recv_sem=recv_sems.at[t, k],
                                            device_id=to, device_id_type=_MESH)

    mine = [pltpu.make_async_copy(in_refs[t], out_refs[t].at[_dev_index(*me)], local_sems.at[t]) for t in range(nt)]
    first, landed, forward, last = [], [], [], []
    for t in range(nt):
        first.append(copy(t, 0, me, sibling, from_input=True))
        last.append(copy(t, 0, sibling, me))
        for j, chip in enumerate(chips):
            first.append(copy(t, 1 + j, me, (*chip, c), from_input=True))
            landed.append(copy(t, 1 + j, (*chip, c), me))
            forward.append(copy(t, 4 + j, (*chip, c), sibling))
            last.append(copy(t, 4 + j, (*chip, 1 - c), me))
    return mine, first, landed, forward, last


GATHER_FORWARD_LEAD = 8


def _with_gather(body, n_in, n_out, shards, steps):
    n = len(shards)
    if n == 0:
        return body, [], [], [], []
    fwd_step = max(steps - 1 - GATHER_FORWARD_LEAD, 0)

    def wrapped(*refs):
        ins, g_in = refs[:n_in], refs[n_in:n_in + n]
        outs, g_out = refs[n_in + n:n_in + n + n_out], refs[n_in + n + n_out:n_in + 2 * n + n_out]
        scratch = refs[n_in + 2 * n + n_out:len(refs) - 3]
        sems = refs[len(refs) - 3:]
        i = pl.program_id(0)

        @pl.when(i == 0)
        def _():
            mine, first, _, _, _ = _gather_copies(g_in, g_out, *sems)
            for cp in mine + first:
                cp.start()

        @pl.when(i == fwd_step)
        def _():
            _, _, landed, forward, _ = _gather_copies(g_in, g_out, *sems)
            for arrived, onward in zip(landed, forward):
                arrived.wait_recv()
                onward.start()

        body(*ins, *outs, *scratch)

        @pl.when(i == steps - 1)
        def _():
            mine, first, _, forward, last = _gather_copies(g_in, g_out, *sems)
            for cp in last:
                cp.wait_recv()
            for cp in first + forward:
                cp.wait_send()
            for cp in mine:
                cp.wait()

    return (wrapped, [_ANY] * n, [_ANY] * n, [jax.ShapeDtypeStruct((N_DEV,) + a.shape, a.dtype) for a in shards],
            [pltpu.SemaphoreType.DMA((n, N_DEV - 1)), pltpu.SemaphoreType.DMA((n, N_DEV - 1)),
             pltpu.SemaphoreType.DMA((n,))])


def _exchange_copies(in_refs, out_refs, send_sems, recv_sems, local_sems):
    x, y, c = _mesh_place()
    me = _dev_index(x, y, c)
    mine, sends, arrivals = [], [], []
    for p, (src, dst) in enumerate(zip(in_refs, out_refs)):
        mine.append(pltpu.make_async_copy(src.at[me], dst.at[me], local_sems.at[p]))
        for rel in range(1, N_DEV):
            peer = _peer(x, y, c, rel)
            there = _dev_index(*peer)
            sems = dict(send_sem=send_sems.at[rel - 1, p], recv_sem=recv_sems.at[rel - 1, p], device_id=peer,
                        device_id_type=_MESH)
            sends.append(pltpu.make_async_remote_copy(src_ref=src.at[there], dst_ref=dst.at[me], **sems))
            arrivals.append(pltpu.make_async_remote_copy(src_ref=src.at[me], dst_ref=dst.at[there], **sems))
    return mine, sends, arrivals


def _with_exchange(body, n_in, n_out, carry, grid):
    n = len(carry)
    if n == 0:
        return body, [], [], [], []
    grid = (grid,) if isinstance(grid, int) else tuple(grid)

    def at_step(corner):
        hit = pl.program_id(0) == corner[0]
        for axis in range(1, len(grid)):
            hit = jnp.logical_and(hit, pl.program_id(axis) == corner[axis])
        return hit

    def wrapped(*refs):
        ins, ex_in = refs[:n_in], refs[n_in:n_in + n]
        outs, ex_out = refs[n_in + n:n_in + n + n_out], refs[n_in + n + n_out:n_in + 2 * n + n_out]
        scratch = refs[n_in + 2 * n + n_out:len(refs) - 3]
        send_sems, recv_sems, local_sems = refs[len(refs) - 3:]

        @pl.when(at_step([0] * len(grid)))
        def _():
            mine, sends, _ = _exchange_copies(ex_in, ex_out, send_sems, recv_sems, local_sems)
            for cp in mine + sends:
                cp.start()

        body(*ins, *outs, *scratch)

        @pl.when(at_step([g - 1 for g in grid]))
        def _():
            mine, sends, arrivals = _exchange_copies(ex_in, ex_out, send_sems, recv_sems, local_sems)
            for cp in arrivals:
                cp.wait_recv()
            for cp in sends:
                cp.wait_send()
            for cp in mine:
                cp.wait()

    return (wrapped, [_ANY] * n, [_ANY] * n, [jax.ShapeDtypeStruct(a.shape, a.dtype) for a in carry],
            [pltpu.SemaphoreType.DMA((N_DEV - 1, n)), pltpu.SemaphoreType.DMA((N_DEV - 1, n)),
             pltpu.SemaphoreType.DMA((n,))])


def reduce_adamw(parts, w, m, v, tr, carry=()):
    nl, r, cols = w.shape
    assert len(parts) == nl

    def body(*refs):
        p_refs = refs[:nl]
        w_ref, m_ref, v_ref, g_ref, d_ref, nm_ref, nv_ref, g_acc = refs[nl:]
        layer = pl.program_id(0)
        for l in range(nl):
            @pl.when(layer == l)
            def _(l=l):
                g = p_refs[l][0].astype(F32)
                for d in range(1, N_DEV):
                    g = g + p_refs[l][d].astype(F32)
                g_acc[...] = g

        g = g_acc[...]
        g_ref[...] = g
        nm = ADAM_B1 * m_ref[...] + (1.0 - ADAM_B1) * g
        nv = ADAM_B2 * v_ref[...] + (1.0 - ADAM_B2) * (g * g)
        nm_ref[...] = nm
        nv_ref[...] = nv
        m_hat = nm / (1.0 - ADAM_B1 ** ADAM_STEP)
        v_hat = nv / (1.0 - ADAM_B2 ** ADAM_STEP)
        d_ref[...] = -ADAM_LR * (m_hat / (jnp.sqrt(v_hat) + ADAM_EPS) + ADAM_WD * w_ref[...])

    row = pl.BlockSpec((None, tr, cols), lambda l, i: (l, i, 0))

    def part_spec(own):
        def index(l, i):
            return (0, jnp.where(l == own, i, jnp.where(l < own, 0, r // tr - 1)), 0)
        return pl.BlockSpec((N_DEV, tr, cols), index)

    grid = (nl, r // tr)
    body, ex_in, ex_out, ex_shape, ex_scratch = _with_exchange(body, nl + 3, 4, carry, grid)
    outs = pl.pallas_call(
        body, grid=grid,
        in_specs=[part_spec(l) for l in range(nl)] + [row, row, row] + ex_in,
        out_specs=[row, row, row, row] + ex_out,
        out_shape=[jax.ShapeDtypeStruct((nl, r, cols), F32)] * 4 + ex_shape,
        scratch_shapes=[pltpu.VMEM((tr, cols), F32)] + ex_scratch,
        name="reduce_adamw", compiler_params=_params("arbitrary", "arbitrary"),
    )(*parts, w, m, v, *carry)
    return outs[:4], outs[4:]


ADAMW_TILE_BYTES = 1 << 19


def _adamw_tile(rows, cols):
    lanes = -(-cols // LANE) * LANE
    best = None
    for tr in range(16, rows + 1, 16):
        if rows % tr == 0 and tr * lanes * 4 <= ADAMW_TILE_BYTES:
            best = tr
    if best is None:
        raise ValueError((rows, cols))
    return best


def kernel(x, meta_tokens, mix_norm_even, w_in, ssm_conv_w, ssm_conv_b, dt_bias, a_log, d_skip, ssm_norm_w, q_norm, k_norm, sinks, w_out, mix_norm_odd, pw1_w, pw1_b, dw_w, dw_b, ln_g, ln_b, pw2_w, pw2_b, mlp_norm, w_up, w_down, loss_target, m_meta_tokens, m_mix_norm_even, m_w_in, m_ssm_conv_w, m_ssm_conv_b, m_dt_bias, m_a_log, m_d_skip, m_ssm_norm_w, m_q_norm, m_k_norm, m_sinks, m_w_out, m_mix_norm_odd, m_pw1_w, m_pw1_b, m_dw_w, m_dw_b, m_ln_g, m_ln_b, m_pw2_w, m_pw2_b, m_mlp_norm, m_w_up, m_w_down, v_meta_tokens, v_mix_norm_even, v_w_in, v_ssm_conv_w, v_ssm_conv_b, v_dt_bias, v_a_log, v_d_skip, v_ssm_norm_w, v_q_norm, v_k_norm, v_sinks, v_w_out, v_mix_norm_odd, v_pw1_w, v_pw1_b, v_dw_w, v_dw_b, v_ln_g, v_ln_b, v_pw2_w, v_pw2_b, v_mlp_norm, v_w_up, v_w_down):
    names = [p[0] for p in PARAMS]
    w_loc = dict(zip(names, (meta_tokens, mix_norm_even, w_in, ssm_conv_w, ssm_conv_b, dt_bias, a_log, d_skip, ssm_norm_w, q_norm, k_norm, sinks, w_out, mix_norm_odd, pw1_w, pw1_b, dw_w, dw_b, ln_g, ln_b, pw2_w, pw2_b, mlp_norm, w_up, w_down)))
    m_loc = dict(zip(names, (m_meta_tokens, m_mix_norm_even, m_w_in, m_ssm_conv_w, m_ssm_conv_b, m_dt_bias, m_a_log, m_d_skip, m_ssm_norm_w, m_q_norm, m_k_norm, m_sinks, m_w_out, m_mix_norm_odd, m_pw1_w, m_pw1_b, m_dw_w, m_dw_b, m_ln_g, m_ln_b, m_pw2_w, m_pw2_b, m_mlp_norm, m_w_up, m_w_down)))
    v_loc = dict(zip(names, (v_meta_tokens, v_mix_norm_even, v_w_in, v_ssm_conv_w, v_ssm_conv_b, v_dt_bias, v_a_log, v_d_skip, v_ssm_norm_w, v_q_norm, v_k_norm, v_sinks, v_w_out, v_mix_norm_odd, v_pw1_w, v_pw1_b, v_dw_w, v_dw_b, v_ln_g, v_ln_b, v_pw2_w, v_pw2_b, v_mlp_norm, v_w_up, v_w_down)))
    small_sharded = [p for p in PARAMS if p[2] is not None and p[0] not in MATMUL_WEIGHTS]
    replicated = [p for p in PARAMS if p[2] is None]
    small_list = small_sharded + replicated

    def layer_shards(layer):
        i = layer // 2
        mixer = ("w_in", "w_out") if layer % 2 == 0 else ("pw1_w", "pw2_w")
        return [w_loc[n][i:i + 1].astype(MXU_DTYPE) for n in mixer] + [
            w_loc[n][layer:layer + 1].astype(MXU_DTYPE) for n in ("w_up", "w_down")]

    shards = [layer_shards(layer) for layer in range(DEPTH)]
    gathered = all_gather_weights(shards[0][:1], _pack([w_loc[n] for n, _, _ in small_sharded], F32))
    w_full = {n: w_loc[n] for n, _, _ in replicated}
    flat = gathered[-1].reshape(N_DEV, -1)
    off = 0
    for n, shape, axis in small_sharded:
        blk = _block_shape(shape, axis)
        w_full[n] = _from_shards(flat[:, off:off + _numel(blk)].reshape((N_DEV,) + blk), axis)
        off += _numel(blk)

    loss_local, grad_x, g_full, received, pending = local_step(x[0], loss_target[0], w_full, shards, gathered[:-1])
    loss = lax.psum(loss_local, ("x", "y", "c"))

    by_dev = [_to_shards(g_full[n], axis) for n, _, axis in small_sharded]
    by_dev += [jnp.broadcast_to(g_full[n][None], (N_DEV,) + tuple(shape)) for n, shape, _ in replicated]
    assert not pending
    order = sorted(MATMUL_WEIGHTS, key=lambda n: _numel(w_loc[n].shape))
    out = {}
    for n in order:
        nl, r, cols = w_loc[n].shape
        out[n], got = reduce_adamw(received[n], w_loc[n], m_loc[n], v_loc[n], _adamw_tile(r, cols),
                                   carry=[_pack_rows(by_dev, F32)] if n == order[0] else ())
        if n == order[0]:
            small_parts = got[0]
    pk = lambda d: _pack([d[n] for n, _, _ in small_list], F32)[None]
    rows = small_parts.shape[1]
    small_out, _ = reduce_adamw([small_parts], pk(w_loc), pk(m_loc), pk(v_loc), _adamw_tile(rows, PACK_W))
    flats = [buf.reshape(-1) for buf in small_out]
    off = 0
    for n, shape, axis in small_list:
        blk = _block_shape(shape, axis)
        out[n] = tuple(f[off:off + _numel(blk)].reshape(blk) for f in flats)
        off += _numel(blk)
    return (loss, grad_x[None], *[out[n][0] for n in names], *[out[n][1] for n in names],
            *[out[n][2] for n in names], *[out[n][3] for n in names])
```

```python
import math

import jax
import jax.numpy as jnp
from jax import lax
from jax.experimental import pallas as pl
from jax.experimental.pallas import tpu as pltpu

F32 = jnp.float32
MXU_DTYPE = jnp.bfloat16
GRAD_WIRE_DTYPE = jnp.bfloat16
HIGHEST = lax.Precision.HIGHEST

D_MODEL = 1024
N_META = 16
BLOCK = 128
FRONT_PAD = BLOCK - N_META
ATT_HEADS = 8
ATT_KV_HEADS = 2
HEAD_DIM = 64
SSM_HEADS = 16
SSM_HEAD_DIM = 64
SSM_INNER = 1024
SSM_GROUPS = 2
SSM_STATE = 64
SSM_CONV = 4
CONF_KERNEL = 31
D_FF = 4096
EPS = 1e-6
LN_EPS = 1e-5
Q_W = 512
KV_W = 128
IN_W = 3088
MIX_W = 1536
DEPTH = 4
N_DEV = 8

ADAM_LR = 0.001
ADAM_B1 = 0.9
ADAM_B2 = 0.999
ADAM_EPS = 1e-08
ADAM_WD = 0.01
ADAM_STEP = 10

PROJ_W = 3200
COL_Z, COL_XS, COL_Q, COL_BC, COL_K, COL_V, COL_DT = 0, 1024, 2048, 2560, 2816, 2944, 3072

ROW_TILE = 640
TN_ROW_TILE = 1664
ACC_BYTES = 8 * 1024 * 1024
VMEM_LIMIT = 56 * 1024 * 1024
LANE = 128
PACK_W = 1024


def _params(*sem):
    return pltpu.CompilerParams(dimension_semantics=sem, vmem_limit_bytes=VMEM_LIMIT)


def _mx(x):
    return x.astype(MXU_DTYPE)


def _dot(a, b):
    return jnp.dot(_mx(a), _mx(b), preferred_element_type=F32)


def _dot_nt(a, b):
    return lax.dot_general(_mx(a), _mx(b), (((1,), (1,)), ((), ())), preferred_element_type=F32)


def _dot_tn(a, b):
    return lax.dot_general(_mx(a), _mx(b), (((0,), (0,)), ((), ())), preferred_element_type=F32)


def _split3(x):
    hi = x.astype(jnp.bfloat16)
    r1 = x - hi.astype(F32)
    mid = r1.astype(jnp.bfloat16)
    lo = (r1 - mid.astype(F32)).astype(jnp.bfloat16)
    return hi, mid, lo


def _sel_dot(x, sel, dims):
    x_first = dims[2]
    if sel.dtype == jnp.bool_:
        sel = jnp.where(sel, 1.0, 0.0)
    one = sel.astype(jnp.bfloat16)
    acc = None
    for part in _split3(x):
        args = (part, one) if x_first else (one, part)
        t = lax.dot_general(*args, (dims[:2], ((), ())), preferred_element_type=F32)
        acc = t if acc is None else acc + t
    return acc


def _dot_hi(a, b, exact="a"):
    if exact == "a":
        return _sel_dot(a, b, ((1,), (0,), True))
    return _sel_dot(b, a, ((1,), (0,), False))


def _dot_tn_hi(a, b):
    return _sel_dot(a, b, ((0,), (0,), True))


def _sigmoid(x):
    return 1.0 / (1.0 + jnp.exp(-x))


def _row_ids(start, rows, cols):
    return start + lax.broadcasted_iota(jnp.int32, (rows, cols), 0)


def rms_fwd(h, g, name="rms_fwd"):
    m, d = h.shape
    tm = ROW_TILE

    def body(h_ref, g_ref, u_ref):
        x = h_ref[...]
        r = lax.rsqrt(jnp.mean(x * x, -1, keepdims=True) + EPS)
        u_ref[...] = ((x * r) * g_ref[...]).astype(u_ref.dtype)

    return pl.pallas_call(
        body, grid=(m // tm,),
        in_specs=[pl.BlockSpec((tm, d), lambda i: (i, 0)), pl.BlockSpec((1, d), lambda i: (0, 0))],
        out_specs=pl.BlockSpec((tm, d), lambda i: (i, 0)),
        out_shape=jax.ShapeDtypeStruct((m, d), MXU_DTYPE), name=name, compiler_params=_params("arbitrary"),
    )(h, g)


def rms_bwd(h, g, du, dh_out, name="rms_bwd"):
    m, d = h.shape
    tm = ROW_TILE

    def body(h_ref, g_ref, du_ref, dho_ref, dh_ref, dg_ref):
        @pl.when(pl.program_id(0) == 0)
        def _():
            dg_ref[...] = jnp.zeros_like(dg_ref)

        x = h_ref[...]
        du_ = du_ref[...]
        r = lax.rsqrt(jnp.mean(x * x, -1, keepdims=True) + EPS)
        gy = du_ * g_ref[...]
        dx = r * gy - x * ((r * r * r) * jnp.mean(x * gy, -1, keepdims=True))
        dh_ref[...] = dho_ref[...] + dx
        dg_ref[...] += jnp.sum(du_ * (x * r), axis=0, keepdims=True)

    row = pl.BlockSpec((tm, d), lambda i: (i, 0))
    vec = pl.BlockSpec((1, d), lambda i: (0, 0))
    return pl.pallas_call(
        body, grid=(m // tm,), in_specs=[row, vec, row, row], out_specs=[row, vec],
        out_shape=[jax.ShapeDtypeStruct((m, d), F32), jax.ShapeDtypeStruct((1, d), F32)],
        name=name, compiler_params=_params("arbitrary"),
    )(h, g, du, dh_out)


def loss_fwd_bwd(h, target, name="loss"):
    m, d = h.shape
    nb = m // BLOCK

    def body(h_ref, t_ref, dh_ref, l_ref):
        i = pl.program_id(0)

        @pl.when(i == 0)
        def _():
            l_ref[...] = jnp.zeros_like(l_ref)
            dh_ref[...] = jnp.zeros_like(dh_ref)

        @pl.when(i > 0)
        def _():
            e = h_ref[...] - t_ref[...]
            dh_ref[...] = e * (1.0 / d)
            s = jnp.sum(jnp.sum(e * e, axis=-1, keepdims=True), axis=0, keepdims=True)
            l_ref[...] += jnp.broadcast_to(s * (0.5 / d), l_ref.shape)

    return pl.pallas_call(
        body, grid=(nb,),
        in_specs=[pl.BlockSpec((BLOCK, d), lambda i: (i, 0)),
                  pl.BlockSpec((BLOCK, d), lambda i: (jnp.maximum(i - 1, 0), 0))],
        out_specs=[pl.BlockSpec((BLOCK, d), lambda i: (i, 0)), pl.BlockSpec((1, LANE), lambda i: (0, 0))],
        out_shape=[jax.ShapeDtypeStruct((m, d), F32), jax.ShapeDtypeStruct((1, LANE), F32)],
        name=name, compiler_params=_params("arbitrary"),
    )(h, target)


def col_sum(x, name="col_sum"):
    m, n = x.shape
    tm = ROW_TILE

    def body(x_ref, o_ref):
        @pl.when(pl.program_id(0) == 0)
        def _():
            o_ref[...] = jnp.zeros_like(o_ref)

        o_ref[...] += jnp.sum(x_ref[...].astype(F32), axis=0, keepdims=True)

    return pl.pallas_call(
        body, grid=(m // tm,), in_specs=[pl.BlockSpec((tm, n), lambda i: (i, 0))],
        out_specs=pl.BlockSpec((1, n), lambda i: (0, 0)), out_shape=jax.ShapeDtypeStruct((1, n), F32),
        name=name, compiler_params=_params("arbitrary"),
    )(x)


def matmul(a, b, *, b_kind="full", layer=0, trans_b=False, tn=None, epilogue=None, bias=None, extra=None,
           out_dtype=F32, carry=(), residual=None, name="matmul"):
    m, k = a.shape
    tm = ROW_TILE
    merge = False
    if b_kind == "full":
        n = b.shape[0] if trans_b else b.shape[1]
        tn = n if tn is None else tn
        b_spec = pl.BlockSpec((tn, k), lambda i, j: (j, 0)) if trans_b else pl.BlockSpec((k, tn), lambda i, j: (0, j))
    elif b_kind == "rowshard":
        ks, wn = b.shape[2], b.shape[3]
        if trans_b and tn == ks:
            assert wn == k
            n = N_DEV * ks
            b_spec = pl.BlockSpec((None, None, ks, wn), lambda i, j: (j, layer, 0, 0))
        else:
            assert tn is None
            merge = True
            n = N_DEV * ks if trans_b else wn
            assert (wn if trans_b else N_DEV * ks) == k
            tn = n
            b_spec = pl.BlockSpec((N_DEV, None, ks, wn), lambda i, j: (0, layer, 0, 0))
    else:
        raise ValueError(b_kind)
    has_bias = bias is not None
    has_extra = extra is not None

    def body(*refs):
        a_ref, b_ref = refs[0], refs[1]
        pos = 2
        bias_ref = extra_ref = None
        if has_bias:
            bias_ref = refs[pos]
            pos += 1
        if has_extra:
            extra_ref = refs[pos]
            pos += 1
        if residual is not None:
            resid_ref = refs[pos]
            pos += 1
        outs = refs[pos:]
        w = b_ref[...]
        if merge:
            w = w.reshape(N_DEV * w.shape[1], w.shape[2])
        if trans_b:
            acc = _dot_nt(a_ref[...], w)
        else:
            acc = _dot(a_ref[...], w)
        if epilogue == "rms_bwd":
            @pl.when(pl.program_id(0) == 0)
            def _():
                outs[1][...] = jnp.zeros_like(outs[1])

            x = extra_ref[...]
            r = lax.rsqrt(jnp.mean(x * x, -1, keepdims=True) + EPS)
            gy = acc * bias_ref[...]
            outs[0][...] = resid_ref[...] + r * gy - x * ((r * r * r) * jnp.mean(x * gy, -1, keepdims=True))
            outs[1][...] += jnp.sum(acc * (x * r), axis=0, keepdims=True)
            return
        if has_bias:
            acc = acc + bias_ref[...]
        if epilogue is None:
            outs[0][...] = acc.astype(outs[0].dtype)
        elif epilogue == "relu2":
            outs[0][...] = acc
            r = jnp.maximum(acc, 0.0)
            outs[1][...] = (r * r).astype(outs[1].dtype)
        elif epilogue == "drelu2":
            outs[0][...] = (acc * (2.0 * jnp.maximum(extra_ref[...], 0.0))).astype(outs[0].dtype)
        elif epilogue == "resid":
            rows = _row_ids(pl.program_id(0) * tm, tm, tn)
            outs[0][...] = extra_ref[...] + jnp.where(rows >= FRONT_PAD, acc, 0.0)
        else:
            raise ValueError(epilogue)

    in_specs = [pl.BlockSpec((tm, k), lambda i, j: (i, 0)), b_spec]
    args = [a, b]
    if has_bias:
        in_specs.append(pl.BlockSpec((1, tn), lambda i, j: (0, j)))
        args.append(bias)
    if has_extra:
        in_specs.append(pl.BlockSpec((tm, tn), lambda i, j: (i, j)))
        args.append(extra)
    tile = pl.BlockSpec((tm, tn), lambda i, j: (i, j))
    if residual is not None:
        in_specs.append(tile)
        args.append(residual)
    if epilogue == "relu2":
        out_specs = [tile, tile]
        out_shape = [jax.ShapeDtypeStruct((m, n), F32), jax.ShapeDtypeStruct((m, n), MXU_DTYPE)]
    elif epilogue == "rms_bwd":
        assert tn == n
        out_specs = [tile, pl.BlockSpec((1, n), lambda i, j: (0, 0))]
        out_shape = [jax.ShapeDtypeStruct((m, n), F32), jax.ShapeDtypeStruct((1, n), F32)]
    else:
        out_specs = [tile]
        out_shape = [jax.ShapeDtypeStruct((m, n), out_dtype)]
    n_out = len(out_specs)
    grid = (m // tm, n // tn)
    body, ex_in, ex_out, ex_shape, ex_scratch = _with_exchange(body, len(args), n_out, carry, grid)
    outs = pl.pallas_call(
        body, grid=grid, in_specs=in_specs + ex_in, out_specs=out_specs + ex_out, out_shape=out_shape + ex_shape,
        scratch_shapes=ex_scratch, name=name, compiler_params=_params("arbitrary", "arbitrary"),
    )(*args, *carry)
    result = outs[0] if n_out == 1 else tuple(outs[:n_out])
    return (result, outs[n_out:]) if carry else result


def matmul_tn(x, dy, *, ti, tn, out_dtype=F32, name="matmul_tn"):
    m, k1 = x.shape
    n = dy.shape[1]
    tm = TN_ROW_TILE
    last = m // tm - 1

    def body(x_ref, dy_ref, o_ref, acc_ref):
        r = pl.program_id(2)

        @pl.when(r == 0)
        def _():
            acc_ref[...] = jnp.zeros_like(acc_ref)

        acc_ref[...] += _dot_tn(x_ref[...], dy_ref[...])

        @pl.when(r == last)
        def _():
            o_ref[...] = acc_ref[...].astype(o_ref.dtype)

    out_specs = pl.BlockSpec((ti, tn), lambda i, j, r: (i, j))
    out_shape = jax.ShapeDtypeStruct((k1, n), out_dtype)
    return pl.pallas_call(
        body, grid=(k1 // ti, n // tn, m // tm),
        in_specs=[pl.BlockSpec((tm, ti), lambda i, j, r: (r, i)), pl.BlockSpec((tm, tn), lambda i, j, r: (r, j))],
        out_specs=out_specs, out_shape=out_shape, scratch_shapes=[pltpu.VMEM((ti, tn), F32)], name=name,
        compiler_params=_params("arbitrary", "arbitrary", "arbitrary"),
    )(x, dy)


FF_BLOCK = D_FF // N_DEV
SQRT_FLOOR = 1.1754944e-38


def _ff_cols(d):
    return slice(FF_BLOCK * d, FF_BLOCK * (d + 1))


def mlp_up(h, norm_g, w, layer, *, bias=None, relu2=True, out_dtype=None, name="mlp_up"):
    m = h.shape[0]
    ns = w.shape[3]
    n = N_DEV * ns
    tm = ROW_TILE
    out_dtype = MXU_DTYPE if relu2 else out_dtype
    has_bias = bias is not None

    def body(*refs):
        h_ref, g_ref, w_ref = refs[0], refs[1], refs[2]
        bias_ref = refs[3] if has_bias else None
        o_ref, u_ref = refs[-2], refs[-1]
        x = h_ref[...]
        u_ = _mx((x * lax.rsqrt(jnp.mean(x * x, -1, keepdims=True) + EPS)) * g_ref[...])
        u_ref[...] = u_
        for d in range(N_DEV):
            cols = slice(ns * d, ns * (d + 1))
            r = _dot(u_, w_ref[d])
            if has_bias:
                r = r + bias_ref[:, cols]
            if relu2:
                r = jnp.maximum(r, 0.0)
                r = r * r
            o_ref[:, cols] = r.astype(o_ref.dtype)

    row = pl.BlockSpec((tm, D_MODEL), lambda i: (i, 0))
    in_specs = [row, pl.BlockSpec((1, D_MODEL), lambda i: (0, 0)),
                pl.BlockSpec((N_DEV, None, D_MODEL, ns), lambda i: (0, layer, 0, 0))]
    args = [h, norm_g, w]
    if has_bias:
        in_specs.append(pl.BlockSpec((1, n), lambda i: (0, 0)))
        args.append(bias)
    return pl.pallas_call(
        body, grid=(m // tm,), in_specs=in_specs, out_specs=[pl.BlockSpec((tm, n), lambda i: (i, 0)), row],
        out_shape=[jax.ShapeDtypeStruct((m, n), out_dtype), jax.ShapeDtypeStruct((m, D_MODEL), MXU_DTYPE)],
        name=name, compiler_params=_params("arbitrary"),
    )(*args)


def mlp_dact(dh, w_down, act, layer, name="mlp_dact"):
    m = dh.shape[0]
    tm = ROW_TILE

    def body(dh_ref, w_ref, act_ref, o_ref):
        dh_ = dh_ref[...]
        for d in range(N_DEV):
            p = act_ref[:, _ff_cols(d)].astype(F32)
            r = p * lax.rsqrt(jnp.maximum(p, SQRT_FLOOR))
            o_ref[:, _ff_cols(d)] = (_dot_nt(dh_, w_ref[d]) * (2.0 * r)).astype(o_ref.dtype)

    return pl.pallas_call(
        body, grid=(m // tm,),
        in_specs=[pl.BlockSpec((tm, D_MODEL), lambda i: (i, 0)),
                  pl.BlockSpec((N_DEV, None, FF_BLOCK, D_MODEL), lambda i: (0, layer, 0, 0)),
                  pl.BlockSpec((tm, D_FF), lambda i: (i, 0))],
        out_specs=pl.BlockSpec((tm, D_FF), lambda i: (i, 0)),
        out_shape=jax.ShapeDtypeStruct((m, D_FF), MXU_DTYPE), name=name, compiler_params=_params("arbitrary"),
    )(dh, w_down, act)


def mlp_du_rms_bwd(da, w_up, layer, h, g, dh_out, name="mlp_du"):
    m, n = da.shape
    ns = w_up.shape[3]
    assert n == N_DEV * ns
    tm = ROW_TILE

    def body(da_ref, w_ref, h_ref, g_ref, dho_ref, dh_ref, dg_ref):
        @pl.when(pl.program_id(0) == 0)
        def _():
            dg_ref[...] = jnp.zeros_like(dg_ref)

        du = _dot_nt(da_ref[:, 0:ns], w_ref[0])
        for d in range(1, N_DEV):
            du = du + _dot_nt(da_ref[:, ns * d:ns * (d + 1)], w_ref[d])
        x = h_ref[...]
        r = lax.rsqrt(jnp.mean(x * x, -1, keepdims=True) + EPS)
        gy = du * g_ref[...]
        dx = r * gy - x * ((r * r * r) * jnp.mean(x * gy, -1, keepdims=True))
        dh_ref[...] = dho_ref[...] + dx
        dg_ref[...] += jnp.sum(du * (x * r), axis=0, keepdims=True)

    row = pl.BlockSpec((tm, D_MODEL), lambda i: (i, 0))
    vec = pl.BlockSpec((1, D_MODEL), lambda i: (0, 0))
    return pl.pallas_call(
        body, grid=(m // tm,),
        in_specs=[pl.BlockSpec((tm, n), lambda i: (i, 0)),
                  pl.BlockSpec((N_DEV, None, D_MODEL, ns), lambda i: (0, layer, 0, 0)), row, vec, row],
        out_specs=[row, vec],
        out_shape=[jax.ShapeDtypeStruct((m, D_MODEL), F32), jax.ShapeDtypeStruct((1, D_MODEL), F32)],
        name=name, compiler_params=_params("arbitrary"),
    )(da, w_up, h, g, dh_out)


def mlp_dw_up(u, da, name="mlp_dw_up"):
    m, n = da.shape
    ns = n // N_DEV
    tm = TN_ROW_TILE
    last = m // tm - 1
    parts = -(-D_MODEL * n * 4 // ACC_BYTES)
    per = N_DEV // parts

    def body(u_ref, da_ref, o_ref, acc_ref):
        r = pl.program_id(1)

        @pl.when(r == 0)
        def _():
            acc_ref[...] = jnp.zeros_like(acc_ref)

        acc_ref[...] += _dot_tn(u_ref[...], da_ref[...])

        @pl.when(r == last)
        def _():
            for d in range(per):
                o_ref[d] = acc_ref[:, ns * d:ns * (d + 1)].astype(o_ref.dtype)

    return pl.pallas_call(
        body, grid=(parts, m // tm),
        in_specs=[pl.BlockSpec((tm, D_MODEL), lambda h, r: (r, 0)), pl.BlockSpec((tm, n // parts), lambda h, r: (r, h))],
        out_specs=pl.BlockSpec((per, D_MODEL, ns), lambda h, r: (h, 0, 0)),
        out_shape=jax.ShapeDtypeStruct((N_DEV, D_MODEL, ns), GRAD_WIRE_DTYPE),
        scratch_shapes=[pltpu.VMEM((D_MODEL, n // parts), F32)], name=name,
        compiler_params=_params("arbitrary", "arbitrary"),
    )(u, da)


def mlp_dw_down(act, dh, name="mlp_dw_down"):
    m = act.shape[0]
    tm = TN_ROW_TILE
    last = m // tm - 1
    parts = D_FF * D_MODEL * 4 // ACC_BYTES
    rows = D_FF // parts

    def body(a_ref, dh_ref, o_ref, acc_ref):
        r = pl.program_id(1)

        @pl.when(r == 0)
        def _():
            acc_ref[...] = jnp.zeros_like(acc_ref)

        acc_ref[...] += _dot_tn(a_ref[...], dh_ref[...])

        @pl.when(r == last)
        def _():
            o_ref[...] = acc_ref[...].astype(o_ref.dtype)

    return pl.pallas_call(
        body, grid=(parts, m // tm),
        in_specs=[pl.BlockSpec((tm, rows), lambda h, r: (r, h)), pl.BlockSpec((tm, D_MODEL), lambda h, r: (r, 0))],
        out_specs=pl.BlockSpec((rows, D_MODEL), lambda h, r: (h, 0)),
        out_shape=jax.ShapeDtypeStruct((D_FF, D_MODEL), GRAD_WIRE_DTYPE),
        scratch_shapes=[pltpu.VMEM((rows, D_MODEL), F32)], name=name,
        compiler_params=_params("arbitrary", "arbitrary"),
    )(act, dh)


_ATT_SCALE = HEAD_DIM ** -0.5


def _alibi_slope(h):
    return 2.0 ** (-8.0 * (h + 1) / ATT_HEADS)


HEADS_PER_KV = ATT_HEADS // ATT_KV_HEADS
STACK = HEADS_PER_KV * BLOCK
N_KEYS = 3 * BLOCK


def _head_select(width):
    c = jnp.arange(width)[:, None]
    h = jnp.arange(LANE)[None, :]
    return (c // HEAD_DIM == h).astype(F32)


def _head_fold(width):
    c = jnp.arange(width)[:, None]
    j = jnp.arange(LANE)[None, :]
    return (c % HEAD_DIM == j).astype(F32)


def _head_rms(x, sel, sel_t):
    r = lax.rsqrt(_dot_hi(x * x, sel) * (1.0 / HEAD_DIM) + EPS)
    return r, _dot_hi(r, sel_t)


def _head_norm_bwd(x, r, r_full, w_t, dy, sel, sel_t):
    gy = dy * w_t
    coef = _dot_hi((r * r * r) * _dot_hi(x * gy, sel) * (1.0 / HEAD_DIM), sel_t)
    return r_full * gy - x * coef, jnp.sum(dy * (x * r_full), axis=0, keepdims=True)


def _low_lanes(rows):
    return lax.broadcasted_iota(jnp.int32, (rows, LANE), 1) < HEAD_DIM


def _dup_half(a, g):
    rolled = pltpu.roll(a, HEAD_DIM, 1)
    low = _low_lanes(a.shape[0])
    return jnp.where(low, a, rolled) if g == 0 else jnp.where(low, rolled, a)


def _stack_heads(x, g):
    low = _low_lanes(BLOCK)
    parts = []
    for pair in range(2):
        p = x[:, 2 * LANE * g + LANE * pair:2 * LANE * g + LANE * (pair + 1)]
        parts += [jnp.where(low, p, 0.0), jnp.where(low, 0.0, p)]
    return jnp.concatenate(parts, axis=0)


def _unstack_heads(groups):
    low = _low_lanes(BLOCK)
    cols = []
    for o in groups:
        for pair in range(2):
            cols.append(jnp.where(low, o[2 * pair * BLOCK:(2 * pair + 1) * BLOCK], o[(2 * pair + 1) * BLOCK:(2 * pair + 2) * BLOCK]))
    return jnp.concatenate(cols, axis=1)


def _fold_halves(a, g):
    s = a + pltpu.roll(a, HEAD_DIM, 1)
    low = _low_lanes(a.shape[0])
    return jnp.where(low if g == 0 else jnp.logical_not(low), s, 0.0)


def _att_bias(b):
    r = lax.broadcasted_iota(jnp.int32, (STACK, N_KEYS), 0) & (BLOCK - 1)
    col = lax.broadcasted_iota(jnp.int32, (STACK, N_KEYS), 1)
    cc = col & (BLOCK - 1)
    is_meta = col < BLOCK
    is_prev = jnp.logical_and(col >= BLOCK, col < 2 * BLOCK)
    q_pos = b * BLOCK + r - FRONT_PAD
    meta_j = cc - FRONT_PAD
    valid_m = jnp.logical_and(cc >= FRONT_PAD, q_pos >= meta_j)
    valid_p = jnp.logical_and(cc > r, b >= 2)
    valid_c = jnp.logical_and(cc <= r, b >= 1)
    is_cur = col >= 2 * BLOCK
    valid = jnp.logical_or(jnp.logical_and(is_meta, valid_m),
                           jnp.logical_or(jnp.logical_and(is_prev, valid_p), jnp.logical_and(is_cur, valid_c)))
    dist = jnp.where(is_meta, jnp.minimum(q_pos - meta_j, BLOCK), jnp.where(is_prev, r - cc + BLOCK, r - cc))
    return valid, dist.astype(F32)


def _per_head_column(values):
    hid = lax.broadcasted_iota(jnp.int32, (STACK, 1), 0) >> 7
    col = jnp.where(hid == 0, values[0], values[1])
    for j in range(2, HEADS_PER_KV):
        col = jnp.where(hid == j, values[j], col)
    return col


def _att_group_probs(qs, kd, valid, dist, g, sk_ref):
    slope = _per_head_column([_alibi_slope(HEADS_PER_KV * g + j) for j in range(HEADS_PER_KV)])
    sink = _per_head_column([sk_ref[HEADS_PER_KV * g + j] for j in range(HEADS_PER_KV)])
    s = jnp.where(valid, _dot_nt(qs, kd) * _ATT_SCALE - slope * dist, -1e30)
    mx = jnp.maximum(jnp.max(s, axis=-1, keepdims=True), sink)
    p = jnp.exp(s - mx)
    p_sink = jnp.exp(sink - mx)
    inv = 1.0 / (jnp.sum(p, axis=-1, keepdims=True) + p_sink)
    return p * inv, p_sink * inv


def attention_fwd(proj, q_w, k_w, sinks, gather=(), name="att_fwd"):
    m = proj.shape[0]
    nb = m // BLOCK
    cq, ck, cv = COL_Q // Q_W, COL_K // KV_W, COL_V // KV_W
    sel_q, sel_k = _head_select(Q_W), _head_select(KV_W)

    def body(q_ref, kc_ref, vc_ref, vp_ref, vm_ref, qw_ref, kw_ref, sk_ref, sq_ref, sqt_ref, skk_ref, skt_ref,
             o_ref, kpn_s, kmn_s):
        b = pl.program_id(0)
        q, kc = q_ref[...], kc_ref[...]
        _, rq = _head_rms(q, sq_ref[...], sqt_ref[...])
        qn = q * rq * qw_ref[...]
        _, rk = _head_rms(kc, skk_ref[...], skt_ref[...])
        kcn = kc * rk * kw_ref[...]

        @pl.when(b == 0)
        def _():
            kmn_s[...] = kcn
            kpn_s[...] = kcn

        kpn, kmn = kpn_s[...], kmn_s[...]
        kpn_s[...] = kcn
        vc, vp, vm = vc_ref[...], vp_ref[...], vm_ref[...]
        valid, dist = _att_bias(b)
        outs = []
        for g in range(ATT_KV_HEADS):
            kd = jnp.concatenate([_dup_half(kmn, g), _dup_half(kpn, g), _dup_half(kcn, g)], axis=0)
            vd = jnp.concatenate([_dup_half(vm, g), _dup_half(vp, g), _dup_half(vc, g)], axis=0)
            probs, _ = _att_group_probs(_stack_heads(qn, g), kd, valid, dist, g, sk_ref)
            outs.append(_dot(probs, vd))
        o_ref[...] = _unstack_heads(outs).astype(o_ref.dtype)

    prev = lambda i: jnp.maximum(i - 1, 0)
    full = lambda s: pl.BlockSpec(s, lambda i: (0,) * len(s))
    body, g_in, g_out, g_shape, g_scratch = _with_gather(body, 12, 1, gather, nb)
    outs = pl.pallas_call(
        body, grid=(nb,),
        in_specs=[pl.BlockSpec((BLOCK, Q_W), lambda i: (i, cq)),
                  pl.BlockSpec((BLOCK, KV_W), lambda i: (i, ck)), pl.BlockSpec((BLOCK, KV_W), lambda i: (i, cv)),
                  pl.BlockSpec((BLOCK, KV_W), lambda i: (prev(i), cv)), pl.BlockSpec((BLOCK, KV_W), lambda i: (0, cv)),
                  full((1, Q_W)), full((1, KV_W)), pl.BlockSpec(memory_space=pltpu.SMEM),
                  full((Q_W, LANE)), full((LANE, Q_W)), full((KV_W, LANE)), full((LANE, KV_W))] + g_in,
        out_specs=[pl.BlockSpec((BLOCK, Q_W), lambda i: (i, 0))] + g_out,
        out_shape=[jax.ShapeDtypeStruct((m, Q_W), MXU_DTYPE)] + g_shape,
        scratch_shapes=[pltpu.VMEM((BLOCK, KV_W), F32), pltpu.VMEM((BLOCK, KV_W), F32)] + g_scratch,
        name=name, compiler_params=_params("arbitrary"),
    )(proj, proj, proj, proj, proj, q_w, k_w, sinks, sel_q, sel_q.T, sel_k, sel_k.T, *gather)
    return outs[0], outs[1:]


def attention_bwd(proj, dmix, q_w, k_w, sinks, carry=(), name="att_bwd"):
    m = proj.shape[0]
    nb = m // BLOCK
    cq, ck, cv = COL_Q // Q_W, COL_K // KV_W, COL_V // KV_W
    c_datt = 0
    sel_q, sel_k = _head_select(Q_W), _head_select(KV_W)
    fold_q, fold_k = _head_fold(Q_W), _head_fold(KV_W)

    def body(do_ref, q_ref, kc_ref, vc_ref, kp_ref, vp_ref, km_ref, vm_ref, qw_ref, kw_ref, sk_ref,
             sq_ref, sqt_ref, skk_ref, skt_ref, fq_ref, fk_ref,
             dq_ref, dk_ref, dv_ref, dqw_ref, dkw_ref, dsk_ref, car_k, car_v, met_k, met_v, kmn_s, qw_acc, kw_acc):
        i = pl.program_id(0)
        b = nb - 1 - i
        sel_q_, sel_qt, sel_k_, sel_kt = sq_ref[...], sqt_ref[...], skk_ref[...], skt_ref[...]
        qw, kw = qw_ref[...], kw_ref[...]

        @pl.when(i == 0)
        def _():
            for r in (car_k, car_v, met_k, met_v, qw_acc, kw_acc, dsk_ref):
                r[...] = jnp.zeros_like(r)
            km = km_ref[...]
            kmn_s[...] = km * _head_rms(km, sel_k_, sel_kt)[1] * kw

        q, kc, kp = q_ref[...], kc_ref[...], kp_ref[...]
        rq, rq_full = _head_rms(q, sel_q_, sel_qt)
        qn = q * rq_full * qw
        rk, rk_full = _head_rms(kc, sel_k_, sel_kt)
        kcn = kc * rk_full * kw
        kpn = kp * _head_rms(kp, sel_k_, sel_kt)[1] * kw
        kmn = kmn_s[...]
        vc, vp, vm = vc_ref[...], vp_ref[...], vm_ref[...]
        do = do_ref[...]
        valid, dist = _att_bias(b)
        lane = lax.broadcasted_iota(jnp.int32, (1, LANE), 1)
        dsk = jnp.zeros((1, LANE), F32)
        dkd_sum = jnp.zeros((N_KEYS, KV_W), F32)
        dvd_sum = jnp.zeros((N_KEYS, KV_W), F32)
        dqd = []
        for g in range(ATT_KV_HEADS):
            kd = jnp.concatenate([_dup_half(kmn, g), _dup_half(kpn, g), _dup_half(kcn, g)], axis=0)
            vd = jnp.concatenate([_dup_half(vm, g), _dup_half(vp, g), _dup_half(vc, g)], axis=0)
            qs = _stack_heads(qn, g)
            dos = _stack_heads(do, g)
            probs, p_sink = _att_group_probs(qs, kd, valid, dist, g, sk_ref)
            o = _dot(probs, vd)
            delta = jnp.sum(dos * o, axis=-1, keepdims=True)
            ds = probs * (_dot_nt(dos, vd) - delta)
            dqd.append(_dot(ds, kd) * _ATT_SCALE)
            dkd_sum = dkd_sum + _fold_halves(_dot_tn(ds, qs) * _ATT_SCALE, g)
            dvd_sum = dvd_sum + _fold_halves(_dot_tn(probs, dos), g)
            sink_grad = p_sink * delta
            for j in range(HEADS_PER_KV):
                part = jnp.sum(sink_grad[BLOCK * j:BLOCK * (j + 1)], axis=0, keepdims=True)
                dsk = dsk - jnp.where(lane == HEADS_PER_KV * g + j, part, 0.0)
        dq, dqw = _head_norm_bwd(q, rq, rq_full, qw, _unstack_heads(dqd), sel_q_, sel_qt)
        dq_ref[...] = dq
        qw_acc[...] += dqw
        dsk_ref[...] += dsk

        met_k[...] += dkd_sum[0:BLOCK]
        met_v[...] += dvd_sum[0:BLOCK]
        first = (b == 0).astype(F32)
        dkn_tot = dkd_sum[2 * BLOCK:3 * BLOCK] + car_k[...] + first * met_k[...]
        dv_ref[...] = dvd_sum[2 * BLOCK:3 * BLOCK] + car_v[...] + first * met_v[...]
        car_k[...] = dkd_sum[BLOCK:2 * BLOCK]
        car_v[...] = dvd_sum[BLOCK:2 * BLOCK]
        dk, dkw = _head_norm_bwd(kc, rk, rk_full, kw, dkn_tot, sel_k_, sel_kt)
        dk_ref[...] = dk
        kw_acc[...] += dkw

        @pl.when(i == nb - 1)
        def _():
            dqw_ref[...] = _dot_hi(jnp.broadcast_to(qw_acc[...], (8, Q_W)), fq_ref[...])[0:1]
            dkw_ref[...] = _dot_hi(jnp.broadcast_to(kw_acc[...], (8, KV_W)), fk_ref[...])[0:1]

    blk = lambda i: nb - 1 - i
    prev = lambda i: jnp.maximum(nb - 2 - i, 0)
    full = lambda s: pl.BlockSpec(s, lambda i: (0,) * len(s))
    kv_scratch = pltpu.VMEM((BLOCK, KV_W), F32)
    body, ex_in, ex_out, ex_shape, ex_scratch = _with_exchange(body, 17, 6, carry, nb)
    outs = pl.pallas_call(
        body, grid=(nb,),
        in_specs=[pl.BlockSpec((BLOCK, Q_W), lambda i: (blk(i), c_datt)),
                  pl.BlockSpec((BLOCK, Q_W), lambda i: (blk(i), cq)),
                  pl.BlockSpec((BLOCK, KV_W), lambda i: (blk(i), ck)), pl.BlockSpec((BLOCK, KV_W), lambda i: (blk(i), cv)),
                  pl.BlockSpec((BLOCK, KV_W), lambda i: (prev(i), ck)), pl.BlockSpec((BLOCK, KV_W), lambda i: (prev(i), cv)),
                  pl.BlockSpec((BLOCK, KV_W), lambda i: (0, ck)), pl.BlockSpec((BLOCK, KV_W), lambda i: (0, cv)),
                  full((1, Q_W)), full((1, KV_W)), pl.BlockSpec(memory_space=pltpu.SMEM),
                  full((Q_W, LANE)), full((LANE, Q_W)), full((KV_W, LANE)), full((LANE, KV_W)),
                  full((Q_W, LANE)), full((KV_W, LANE))] + ex_in,
        out_specs=[pl.BlockSpec((BLOCK, Q_W), lambda i: (blk(i), 0)),
                   pl.BlockSpec((BLOCK, KV_W), lambda i: (blk(i), 0)), pl.BlockSpec((BLOCK, KV_W), lambda i: (blk(i), 0)),
                   full((1, LANE)), full((1, LANE)), full((1, LANE))] + ex_out,
        out_shape=[jax.ShapeDtypeStruct((m, Q_W), F32), jax.ShapeDtypeStruct((m, KV_W), F32),
                   jax.ShapeDtypeStruct((m, KV_W), F32), jax.ShapeDtypeStruct((1, LANE), F32),
                   jax.ShapeDtypeStruct((1, LANE), F32), jax.ShapeDtypeStruct((1, LANE), F32)] + ex_shape,
        scratch_shapes=[kv_scratch, kv_scratch, kv_scratch, kv_scratch, kv_scratch,
                        pltpu.VMEM((1, Q_W), F32), pltpu.VMEM((1, KV_W), F32)] + ex_scratch,
        name=name, compiler_params=_params("arbitrary"),
    )(dmix, proj, proj, proj, proj, proj, proj, proj, q_w, k_w, sinks, sel_q, sel_q.T, sel_k, sel_k.T, fold_q, fold_k,
      *carry)
    return outs[:6], outs[6:]


HALO = 8
GROUP_W = SSM_INNER // SSM_GROUPS
HEADS_PER_GROUP = SSM_HEADS // SSM_GROUPS


def _head_expand():
    h = jnp.arange(LANE)[:, None]
    c = jnp.arange(SSM_INNER)[None, :]
    return (c // SSM_HEAD_DIM == h).astype(F32)


def _softplus(x):
    return jnp.maximum(x, 0.0) + jnp.log1p(jnp.exp(-jnp.abs(x)))


def _ssd_decays(dt, a_log_row):
    row = lax.broadcasted_iota(jnp.int32, (BLOCK, BLOCK), 0)
    col = lax.broadcasted_iota(jnp.int32, (BLOCK, BLOCK), 1)
    lower = row >= col
    a = -jnp.exp(a_log_row)
    a_cs = _dot_hi(lower, dt * a, exact="b")
    return a, a_cs, lower


def _decay_matrix(a_cs, a_cs_t, h, lower):
    diff = a_cs[:, h:h + 1] - a_cs_t[h:h + 1, :]
    return jnp.where(lower, jnp.exp(jnp.where(lower, diff, 0.0)), 0.0)


def _conv_taps(s_ref, w_ref, first, rows):
    acc = w_ref[0:1, :] * s_ref[pl.ds(first, rows), :]
    for j in range(1, SSM_CONV):
        acc = acc + w_ref[j:j + 1, :] * s_ref[pl.ds(first + j, rows), :]
    return acc


def ssd_fwd(proj, att, cw_x, cw_bc, cb_x, cb_bc, dt_bias, a_log, d_exp, norm_w, gather=(), name="ssd_fwd"):
    m = proj.shape[0]
    nb = m // BLOCK
    expand = _head_expand()
    expand_t = expand.T

    def body(z_ref, xs_ref, bc_ref, dtr_ref, att_ref, cwx_ref, cwbc_ref, cbx_ref, cbbc_ref, dtb_ref, alog_ref, dexp_ref,
             nw_ref, e_ref, et_ref, out_ref, prex_ref, prebc_ref, dt_ref, ypre_ref, st_ref, sx, sbc, state):
        c = pl.program_id(0)

        @pl.when(c == 0)
        def _():
            sx[0:HALO, :] = jnp.zeros((HALO, SSM_INNER), F32)
            sbc[0:HALO, :] = jnp.zeros((HALO, 2 * LANE), F32)
            state[...] = jnp.zeros_like(state)

        sx[HALO:HALO + BLOCK, :] = xs_ref[...]
        sbc[HALO:HALO + BLOCK, :] = bc_ref[...]
        first = HALO - (SSM_CONV - 1)
        pre_x = _conv_taps(sx, cwx_ref, first, BLOCK) + cbx_ref[...]
        pre_bc = _conv_taps(sbc, cwbc_ref, first, BLOCK) + cbbc_ref[...]
        sx[0:HALO, :] = xs_ref[BLOCK - HALO:BLOCK, :]
        sbc[0:HALO, :] = bc_ref[BLOCK - HALO:BLOCK, :]
        prex_ref[...] = pre_x
        prebc_ref[...] = pre_bc
        xc = pre_x * _sigmoid(pre_x)
        bcv = pre_bc * _sigmoid(pre_bc)

        rows = _row_ids(c * BLOCK, BLOCK, LANE)
        lanes = lax.broadcasted_iota(jnp.int32, (BLOCK, LANE), 1)
        live = jnp.logical_and(rows >= FRONT_PAD, lanes < SSM_HEADS)
        dt = jnp.where(live, _softplus(dtr_ref[...] + dtb_ref[...]), 0.0)
        dt_ref[...] = dt
        a, a_cs, lower = _ssd_decays(dt, alog_ref[...])
        a_cs_t = a_cs.T
        dt_t = dt.T
        e = e_ref[...]
        es_full = _dot_hi(jnp.exp(a_cs), e)
        wx_full = _dot_hi(jnp.exp(a_cs[BLOCK - 1:BLOCK, :] - a_cs) * dt, e)
        end_col = jnp.exp(a_cs_t[:, BLOCK - 1:BLOCK])
        dec_full = _dot_hi(et_ref[...], jnp.broadcast_to(end_col, (LANE, SSM_STATE)), exact="b")

        st_ref[0] = state[...]
        ys = []
        for g in range(SSM_GROUPS):
            b_g = bcv[:, SSM_STATE * g:SSM_STATE * (g + 1)]
            c_g = bcv[:, LANE + SSM_STATE * g:LANE + SSM_STATE * (g + 1)]
            gs = slice(GROUP_W * g, GROUP_W * (g + 1))
            cb = _dot_nt(c_g, b_g)
            yd = []
            for hh in range(HEADS_PER_GROUP):
                h = g * HEADS_PER_GROUP + hh
                w = cb * _decay_matrix(a_cs, a_cs_t, h, lower) * dt_t[h:h + 1, :]
                yd.append(_dot(w, xc[:, SSM_HEAD_DIM * h:SSM_HEAD_DIM * (h + 1)]))
            h_g = state[gs, :]
            y_off = _dot_nt(c_g, h_g) * es_full[:, gs]
            ys.append(jnp.concatenate(yd, axis=1) + y_off)
            new_state = _dot_tn(xc[:, gs] * wx_full[:, gs], b_g)
            state[gs, :] = h_g * dec_full[gs, :] + new_state
        y_pre = jnp.concatenate(ys, axis=1) + xc * dexp_ref[...]
        ypre_ref[...] = y_pre
        z = z_ref[...]
        gt = y_pre * (z * _sigmoid(z))
        outs = []
        for g in range(SSM_GROUPS):
            gg = gt[:, GROUP_W * g:GROUP_W * (g + 1)]
            r = lax.rsqrt(jnp.mean(gg * gg, -1, keepdims=True) + EPS)
            outs.append(gg * r)
        out_ref[:, 0:Q_W] = att_ref[...]
        out_ref[:, Q_W:MIX_W] = (jnp.concatenate(outs, axis=1) * nw_ref[...]).astype(out_ref.dtype)

    full = lambda s: pl.BlockSpec(s, lambda i: (0,) * len(s))
    rowblk = lambda w, cidx: pl.BlockSpec((BLOCK, w), lambda i: (i, cidx))
    body, g_in, g_out, g_shape, g_scratch = _with_gather(body, 15, 6, gather, nb)
    outs = pl.pallas_call(
        body, grid=(nb,),
        in_specs=[rowblk(SSM_INNER, COL_Z // SSM_INNER), rowblk(SSM_INNER, COL_XS // SSM_INNER),
                  rowblk(2 * LANE, COL_BC // (2 * LANE)), rowblk(LANE, COL_DT // LANE), rowblk(Q_W, 0),
                  full((SSM_CONV, SSM_INNER)), full((SSM_CONV, 2 * LANE)), full((1, SSM_INNER)), full((1, 2 * LANE)),
                  full((1, LANE)), full((1, LANE)), full((1, SSM_INNER)), full((1, SSM_INNER)),
                  full((LANE, SSM_INNER)), full((SSM_INNER, LANE))] + g_in,
        out_specs=[rowblk(MIX_W, 0), rowblk(SSM_INNER, 0), rowblk(2 * LANE, 0), rowblk(LANE, 0),
                   rowblk(SSM_INNER, 0), pl.BlockSpec((1, SSM_INNER, SSM_STATE), lambda i: (i, 0, 0))] + g_out,
        out_shape=[jax.ShapeDtypeStruct((m, MIX_W), MXU_DTYPE), jax.ShapeDtypeStruct((m, SSM_INNER), F32),
                   jax.ShapeDtypeStruct((m, 2 * LANE), F32), jax.ShapeDtypeStruct((m, LANE), F32),
                   jax.ShapeDtypeStruct((m, SSM_INNER), F32),
                   jax.ShapeDtypeStruct((nb, SSM_INNER, SSM_STATE), F32)] + g_shape,
        scratch_shapes=[pltpu.VMEM((HALO + BLOCK, SSM_INNER), F32), pltpu.VMEM((HALO + BLOCK, 2 * LANE), F32),
                        pltpu.VMEM((SSM_INNER, SSM_STATE), F32)] + g_scratch,
        name=name, compiler_params=_params("arbitrary"),
    )(proj, proj, proj, proj, att, cw_x, cw_bc, cb_x, cb_bc, dt_bias, a_log, d_exp, norm_w, expand, expand_t, *gather)
    return outs[:6], outs[6:]


def ssd_bwd(proj, dmix, dq, dk, dv, pre_x, pre_bc, dt, y_pre, states, cw_x, cw_bc, dt_bias, a_log, d_exp, norm_w,
            carry=(), name="ssd_bwd"):
    m = proj.shape[0]
    nb = m // BLOCK
    expand = _head_expand()
    expand_t = expand.T

    def body(do0_ref, do1_ref, z_ref, xs_ref, xsp_ref, bc_ref, bcp_ref, dtr_ref, prex_ref, prebc_ref, dt_ref, ypre_ref,
             st_ref, dq_ref, dk_ref, dv_ref,
             cwx_ref, cwbc_ref, dtb_ref, alog_ref, dexp_ref, nw_ref, e_ref, et_ref,
             dproj_ref, dcwx_ref, dcwbc_ref, dcbx_ref, dcbbc_ref, ddtb_ref, dalog_ref,
             dd_ref, dnw_ref,
             dstate, hnext, tx, tbc, sx, sbc, dlane):

        def put(col, value):
            dproj_ref[:, col:col + value.shape[1]] = value.astype(dproj_ref.dtype)

        put(COL_Q, dq_ref[...])
        put(COL_K, dk_ref[...])
        put(COL_V, dv_ref[...])
        i = pl.program_id(0)
        c = nb - 1 - i

        @pl.when(i == 0)
        def _():
            dstate[...] = jnp.zeros_like(dstate)
            hnext[...] = jnp.zeros_like(hnext)
            tx[BLOCK:BLOCK + HALO, :] = jnp.zeros((HALO, SSM_INNER), F32)
            tbc[BLOCK:BLOCK + HALO, :] = jnp.zeros((HALO, 2 * LANE), F32)
            dlane[...] = jnp.zeros_like(dlane)
            for r in (dcwx_ref, dcwbc_ref, dcbx_ref, dcbbc_ref, ddtb_ref, dalog_ref, dd_ref, dnw_ref):
                r[...] = jnp.zeros_like(r)

        e = e_ref[...]
        et = et_ref[...]
        pre_x = prex_ref[...]
        pre_bc = prebc_ref[...]
        sig_x = _sigmoid(pre_x)
        sig_bc = _sigmoid(pre_bc)
        xc = pre_x * sig_x
        bcv = pre_bc * sig_bc
        dt = dt_ref[...]
        a, a_cs, lower = _ssd_decays(dt, alog_ref[...])
        a_cs_t = a_cs.T
        es_full = _dot_hi(jnp.exp(a_cs), e)
        ed_full = _dot_hi(jnp.exp(a_cs[BLOCK - 1:BLOCK, :] - a_cs), e)
        dt_full = _dot_hi(dt, e)
        end_col = jnp.exp(a_cs_t[:, BLOCK - 1:BLOCK])
        dec_full = _dot_hi(et, jnp.broadcast_to(end_col, (LANE, SSM_STATE)), exact="b")
        dexp = dexp_ref[...]

        z = z_ref[...]
        zs = _sigmoid(z)
        sz = z * zs
        y_pre = ypre_ref[...]
        gt = y_pre * sz
        do = jnp.concatenate([do0_ref[...], do1_ref[...]], axis=1)
        nw = nw_ref[...]
        dgt = []
        dnw = []
        for g in range(SSM_GROUPS):
            gs = slice(GROUP_W * g, GROUP_W * (g + 1))
            gg = gt[:, gs]
            r = lax.rsqrt(jnp.mean(gg * gg, -1, keepdims=True) + EPS)
            gn = do[:, gs] * nw[:, gs]
            dgt.append(r * gn - gg * ((r * r * r) * jnp.mean(gg * gn, -1, keepdims=True)))
            dnw.append(jnp.sum(do[:, gs] * (gg * r), axis=0, keepdims=True))
        dgt = jnp.concatenate(dgt, axis=1)
        dnw_ref[...] += jnp.concatenate(dnw, axis=1)
        dy = dgt * sz
        put(COL_Z, dgt * y_pre * (zs * (1.0 + z * (1.0 - zs))))
        dlane[...] += jnp.sum(dy * xc, axis=0, keepdims=True)
        xd = xc * dt_full

        lane_id = lax.broadcasted_iota(jnp.int32, (BLOCK, LANE), 1)
        sub_id = lax.broadcasted_iota(jnp.int32, (LANE, BLOCK), 0)
        ds_to = jnp.zeros((BLOCK, LANE), F32)
        ds_from_t = jnp.zeros((LANE, BLOCK), F32)
        dxd_parts, inter_parts = [], []
        dbs, dcs = [], []
        for g in range(SSM_GROUPS):
            gs = slice(GROUP_W * g, GROUP_W * (g + 1))
            b_g = bcv[:, SSM_STATE * g:SSM_STATE * (g + 1)]
            c_g = bcv[:, LANE + SSM_STATE * g:LANE + SSM_STATE * (g + 1)]
            cb = _dot_nt(c_g, b_g)
            dcb = jnp.zeros((BLOCK, BLOCK), F32)
            dxd_h = []
            for hh in range(HEADS_PER_GROUP):
                h = g * HEADS_PER_GROUP + hh
                hs = slice(SSM_HEAD_DIM * h, SSM_HEAD_DIM * (h + 1))
                lm = _decay_matrix(a_cs, a_cs_t, h, lower)
                dy_h = dy[:, hs]
                gl = _dot_nt(dy_h, xd[:, hs]) * lm
                dcb = dcb + gl
                e_h = gl * cb
                ds_to = ds_to + jnp.where(lane_id == h, jnp.sum(e_h, axis=-1, keepdims=True), 0.0)
                ds_from_t = ds_from_t + jnp.where(sub_id == h, jnp.sum(e_h, axis=0, keepdims=True), 0.0)
                dxd_h.append(_dot_tn(cb * lm, dy_h))
            h_g = st_ref[0, gs, :]
            dh_g = dstate[gs, :]
            dys_g = dy[:, gs] * es_full[:, gs]
            xde_g = xd[:, gs] * ed_full[:, gs]
            dcs.append(_dot(dcb, b_g) + _dot(dys_g, h_g))
            dbs.append(_dot_tn(dcb, c_g) + _dot(xde_g, dh_g))
            y_off = _dot_nt(c_g, h_g) * es_full[:, gs]
            dxd_state = _dot_nt(b_g, dh_g) * ed_full[:, gs]
            inter_parts.append(dy[:, gs] * y_off - xd[:, gs] * dxd_state)
            dxd_parts.append(jnp.concatenate(dxd_h, axis=1) + dxd_state)
            dstate[gs, :] = dh_g * dec_full[gs, :] + _dot_tn(dys_g, c_g)
            if g == 0:
                end_dot = hnext[gs, :] * dh_g
            else:
                end_dot = jnp.concatenate([end_dot, hnext[gs, :] * dh_g], axis=0)
        dxd = jnp.concatenate(dxd_parts, axis=1)
        hnext[...] = st_ref[0]

        ds = ds_to - ds_from_t.T + _dot_hi(jnp.concatenate(inter_parts, axis=1), et)
        ds_end = jnp.sum(_dot_tn_hi(end_dot, et), axis=0, keepdims=True)
        rows_l = lax.broadcasted_iota(jnp.int32, (BLOCK, LANE), 0)
        ds = ds + jnp.where(rows_l == BLOCK - 1, ds_end, 0.0)
        row = lax.broadcasted_iota(jnp.int32, (BLOCK, BLOCK), 0)
        col = lax.broadcasted_iota(jnp.int32, (BLOCK, BLOCK), 1)
        dadt = _dot_hi(col >= row, ds, exact="b")
        ddt = dadt * a + _dot_hi(dxd * xc, et)
        dalog_ref[...] += jnp.sum(dadt * dt, axis=0, keepdims=True) * a
        rows = _row_ids(c * BLOCK, BLOCK, LANE)
        lanes = lax.broadcasted_iota(jnp.int32, (BLOCK, LANE), 1)
        live = jnp.logical_and(rows >= FRONT_PAD, lanes < SSM_HEADS)
        ddt_raw = jnp.where(live, ddt * _sigmoid(dtr_ref[...] + dtb_ref[...]), 0.0)
        put(COL_DT, ddt_raw)
        ddtb_ref[...] += jnp.sum(ddt_raw, axis=0, keepdims=True)

        dxc = dxd * dt_full + dy * dexp
        dpre_x = dxc * (sig_x * (1.0 + pre_x * (1.0 - sig_x)))
        dpre_bc = jnp.concatenate(dbs + dcs, axis=1) * (sig_bc * (1.0 + pre_bc * (1.0 - sig_bc)))
        dcbx_ref[...] += jnp.sum(dpre_x, axis=0, keepdims=True)
        dcbbc_ref[...] += jnp.sum(dpre_bc, axis=0, keepdims=True)
        keep_x = _row_ids(c * BLOCK, BLOCK, SSM_INNER) >= FRONT_PAD
        keep_bc = _row_ids(c * BLOCK, BLOCK, 2 * LANE) >= FRONT_PAD
        prev_live = (c > 0).astype(F32)
        for (dpre, t_ref, s_ref, cur_ref, prv_ref, w_ref, dw_ref, col, keep) in (
                (dpre_x, tx, sx, xs_ref, xsp_ref, cwx_ref, dcwx_ref, COL_XS, keep_x),
                (dpre_bc, tbc, sbc, bc_ref, bcp_ref, cwbc_ref, dcwbc_ref, COL_BC, keep_bc)):
            t_ref[0:BLOCK, :] = dpre
            acc = w_ref[0:1, :] * t_ref[pl.ds(SSM_CONV - 1, BLOCK), :]
            for j in range(1, SSM_CONV):
                acc = acc + w_ref[j:j + 1, :] * t_ref[pl.ds(SSM_CONV - 1 - j, BLOCK), :]
            put(col, jnp.where(keep, acc, 0.0))
            t_ref[BLOCK:BLOCK + HALO, :] = dpre[0:HALO, :]
            s_ref[0:HALO, :] = prv_ref[BLOCK - HALO:BLOCK, :] * prev_live
            s_ref[HALO:HALO + BLOCK, :] = cur_ref[...]
            first = HALO - (SSM_CONV - 1)
            for j in range(SSM_CONV):
                dw_ref[j:j + 1, :] += jnp.sum(dpre * s_ref[pl.ds(first + j, BLOCK), :], axis=0, keepdims=True)

        @pl.when(i == nb - 1)
        def _():
            dd_ref[...] = _dot_hi(jnp.broadcast_to(dlane[...], (HALO, SSM_INNER)), et)[0:1, :]

    blk = lambda i: nb - 1 - i
    prv = lambda i: jnp.maximum(nb - 2 - i, 0)
    full = lambda s: pl.BlockSpec(s, lambda i: (0,) * len(s))
    rowblk = lambda w, cidx: pl.BlockSpec((BLOCK, w), lambda i: (blk(i), cidx))
    prvblk = lambda w, cidx: pl.BlockSpec((BLOCK, w), lambda i: (prv(i), cidx))
    body, ex_in, ex_out, ex_shape, ex_scratch = _with_exchange(body, 24, 9, carry, nb)
    outs = pl.pallas_call(
        body, grid=(nb,),
        in_specs=[rowblk(GROUP_W, Q_W // GROUP_W), rowblk(GROUP_W, Q_W // GROUP_W + 1),
                  rowblk(SSM_INNER, COL_Z // SSM_INNER),
                  rowblk(SSM_INNER, COL_XS // SSM_INNER), prvblk(SSM_INNER, COL_XS // SSM_INNER),
                  rowblk(2 * LANE, COL_BC // (2 * LANE)), prvblk(2 * LANE, COL_BC // (2 * LANE)),
                  rowblk(LANE, COL_DT // LANE),
                  rowblk(SSM_INNER, 0), rowblk(2 * LANE, 0), rowblk(LANE, 0), rowblk(SSM_INNER, 0),
                  pl.BlockSpec((1, SSM_INNER, SSM_STATE), lambda i: (blk(i), 0, 0)),
                  rowblk(Q_W, 0), rowblk(KV_W, 0), rowblk(KV_W, 0),
                  full((SSM_CONV, SSM_INNER)), full((SSM_CONV, 2 * LANE)), full((1, LANE)), full((1, LANE)),
                  full((1, SSM_INNER)), full((1, SSM_INNER)), full((LANE, SSM_INNER)), full((SSM_INNER, LANE))] + ex_in,
        out_specs=[rowblk(PROJ_W, 0),
                   full((SSM_CONV, SSM_INNER)), full((SSM_CONV, 2 * LANE)), full((1, SSM_INNER)), full((1, 2 * LANE)),
                   full((1, LANE)), full((1, LANE)), full((1, LANE)), full((1, SSM_INNER))] + ex_out,
        out_shape=[jax.ShapeDtypeStruct((m, PROJ_W), MXU_DTYPE),
                   jax.ShapeDtypeStruct((SSM_CONV, SSM_INNER), F32), jax.ShapeDtypeStruct((SSM_CONV, 2 * LANE), F32),
                   jax.ShapeDtypeStruct((1, SSM_INNER), F32), jax.ShapeDtypeStruct((1, 2 * LANE), F32),
                   jax.ShapeDtypeStruct((1, LANE), F32), jax.ShapeDtypeStruct((1, LANE), F32),
                   jax.ShapeDtypeStruct((1, LANE), F32), jax.ShapeDtypeStruct((1, SSM_INNER), F32)] + ex_shape,
        scratch_shapes=[pltpu.VMEM((SSM_INNER, SSM_STATE), F32), pltpu.VMEM((SSM_INNER, SSM_STATE), F32),
                        pltpu.VMEM((BLOCK + HALO, SSM_INNER), F32), pltpu.VMEM((BLOCK + HALO, 2 * LANE), F32),
                        pltpu.VMEM((HALO + BLOCK, SSM_INNER), F32), pltpu.VMEM((HALO + BLOCK, 2 * LANE), F32),
                        pltpu.VMEM((1, SSM_INNER), F32)] + ex_scratch,
        name=name, compiler_params=_params("arbitrary"),
    )(dmix, dmix, proj, proj, proj, proj, proj, proj, pre_x, pre_bc, dt, y_pre, states, dq, dk, dv,
      cw_x, cw_bc, dt_bias, a_log, d_exp, norm_w, expand, expand_t, *carry)
    return outs[:9], outs[9:]


CONF_HALO = 32
SUBLANES = 8


def _for_each_window(s, offsets, rows, fn):
    total = s.shape[0]
    assert max(offsets) + rows <= total
    for b in range(SUBLANES):
        offs = [o for o in offsets if o % SUBLANES == b]
        if not offs:
            continue
        rot = s if b == 0 else pltpu.roll(s, total - b, 0)
        for o in offs:
            fn(o, rot[o - b:o - b + rows])


def _glu_masked(v, first_row):
    a = v[:, :D_MODEL]
    s = _sigmoid(v[:, D_MODEL:])
    rows = _row_ids(first_row, v.shape[0], D_MODEL)
    return jnp.where(rows >= FRONT_PAD, a * s, 0.0), a, s


def _layer_norm_stats(c):
    mu = jnp.mean(c, -1, keepdims=True)
    xc = c - mu
    rstd = lax.rsqrt(jnp.mean(xc * xc, -1, keepdims=True) + LN_EPS)
    return xc * rstd, rstd


def conformer_mid_fwd(v, dw_w, dw_b, ln_g, ln_b, gather=(), name="conf_mid_fwd"):
    m = v.shape[0]
    nb = m // BLOCK
    kpad = dw_w.shape[0]

    def body(vc_ref, vp_ref, w_ref, b_ref, g_ref, beta_ref, c_ref, s_ref):
        i = pl.program_id(0)
        g_prev, _, _ = _glu_masked(vp_ref[BLOCK - CONF_HALO:BLOCK, :], (i - 1) * BLOCK + BLOCK - CONF_HALO)
        g_cur, _, _ = _glu_masked(vc_ref[...], i * BLOCK)
        sg = jnp.concatenate([g_prev * (i > 0).astype(F32), g_cur], axis=0)
        first = CONF_HALO - (CONF_KERNEL - 1)
        acc = [jnp.broadcast_to(b_ref[...], (BLOCK, D_MODEL))]

        def tap(off, win):
            j = off - first
            acc[0] = acc[0] + w_ref[j:j + 1, :] * win

        _for_each_window(sg, [first + j for j in range(CONF_KERNEL)], BLOCK, tap)
        acc = acc[0]
        c_ref[...] = acc
        xhat, _ = _layer_norm_stats(acc)
        nrm = xhat * g_ref[...] + beta_ref[...]
        s_ref[...] = (nrm * _sigmoid(nrm)).astype(s_ref.dtype)

    full = lambda s: pl.BlockSpec(s, lambda i: (0,) * len(s))
    body, g_in, g_out, g_shape, g_scratch = _with_gather(body, 6, 2, gather, nb)
    outs = pl.pallas_call(
        body, grid=(nb,),
        in_specs=[pl.BlockSpec((BLOCK, 2 * D_MODEL), lambda i: (i, 0)),
                  pl.BlockSpec((BLOCK, 2 * D_MODEL), lambda i: (jnp.maximum(i - 1, 0), 0)),
                  full((kpad, D_MODEL)), full((1, D_MODEL)), full((1, D_MODEL)), full((1, D_MODEL))] + g_in,
        out_specs=[pl.BlockSpec((BLOCK, D_MODEL), lambda i: (i, 0)),
                   pl.BlockSpec((BLOCK, D_MODEL), lambda i: (i, 0))] + g_out,
        out_shape=[jax.ShapeDtypeStruct((m, D_MODEL), F32), jax.ShapeDtypeStruct((m, D_MODEL), MXU_DTYPE)] + g_shape,
        scratch_shapes=g_scratch, name=name, compiler_params=_params("arbitrary"),
    )(v, v, dw_w, dw_b, ln_g, ln_b, *gather)
    return outs[:2], outs[2:]


def conformer_ln_bwd(ds, c, ln_g, ln_b, name="conf_ln_bwd"):
    m, d = c.shape
    tm = ROW_TILE

    def body(ds_ref, c_ref, g_ref, beta_ref, dc_ref, dg_ref, db_ref):
        @pl.when(pl.program_id(0) == 0)
        def _():
            dg_ref[...] = jnp.zeros_like(dg_ref)
            db_ref[...] = jnp.zeros_like(db_ref)

        xhat, rstd = _layer_norm_stats(c_ref[...])
        g = g_ref[...]
        nrm = xhat * g + beta_ref[...]
        sg = _sigmoid(nrm)
        dn = ds_ref[...] * (sg * (1.0 + nrm * (1.0 - sg)))
        db_ref[...] += jnp.sum(dn, axis=0, keepdims=True)
        dg_ref[...] += jnp.sum(dn * xhat, axis=0, keepdims=True)
        dx = dn * g
        dc_ref[...] = rstd * (dx - jnp.mean(dx, -1, keepdims=True) - xhat * jnp.mean(dx * xhat, -1, keepdims=True))

    row = pl.BlockSpec((tm, d), lambda i: (i, 0))
    vec = pl.BlockSpec((1, d), lambda i: (0, 0))
    return pl.pallas_call(
        body, grid=(m // tm,), in_specs=[row, row, vec, vec], out_specs=[row, vec, vec],
        out_shape=[jax.ShapeDtypeStruct((m, d), F32), jax.ShapeDtypeStruct((1, d), F32), jax.ShapeDtypeStruct((1, d), F32)],
        name=name, compiler_params=_params("arbitrary"),
    )(ds, c, ln_g, ln_b)


def conformer_conv_bwd(dc, v, dw_w, carry=(), name="conf_conv_bwd"):
    m = v.shape[0]
    nb = m // BLOCK
    kpad = dw_w.shape[0]

    def body(dcc_ref, dcn_ref, vc_ref, vp_ref, w_ref, dv_ref, dw_ref, db_ref, dvb_ref):
        i = pl.program_id(0)

        @pl.when(i == 0)
        def _():
            dw_ref[...] = jnp.zeros_like(dw_ref)
            db_ref[...] = jnp.zeros_like(db_ref)
            dvb_ref[...] = jnp.zeros_like(dvb_ref)

        dc_cur = dcc_ref[...]
        tg = jnp.concatenate([dc_cur, dcn_ref[0:CONF_HALO, :] * (i < nb - 1).astype(F32)], axis=0)
        g_prev, _, _ = _glu_masked(vp_ref[BLOCK - CONF_HALO:BLOCK, :], (i - 1) * BLOCK + BLOCK - CONF_HALO)
        g_cur, a, s = _glu_masked(vc_ref[...], i * BLOCK)
        sg = jnp.concatenate([g_prev * (i > 0).astype(F32), g_cur], axis=0)
        db_ref[...] += jnp.sum(dc_cur, axis=0, keepdims=True)
        first = CONF_HALO - (CONF_KERNEL - 1)
        dg_acc = [jnp.zeros((BLOCK, D_MODEL), F32)]

        def tap_dg(off, win):
            j = CONF_KERNEL - 1 - off
            dg_acc[0] = dg_acc[0] + w_ref[j:j + 1, :] * win

        def tap_dw(off, win):
            j = off - first
            dw_ref[j:j + 1, :] += jnp.sum(dc_cur * win, axis=0, keepdims=True)

        _for_each_window(tg, list(range(CONF_KERNEL)), BLOCK, tap_dg)
        _for_each_window(sg, [first + j for j in range(CONF_KERNEL)], BLOCK, tap_dw)
        dg = dg_acc[0]
        rows = _row_ids(i * BLOCK, BLOCK, D_MODEL)
        dg = jnp.where(rows >= FRONT_PAD, dg, 0.0)
        da = dg * s
        dbv = dg * a * (s * (1.0 - s))
        dv = jnp.concatenate([da, dbv], axis=1)
        dv_ref[...] = dv.astype(dv_ref.dtype)
        dvb_ref[...] += jnp.sum(dv, axis=0, keepdims=True)

    full = lambda s: pl.BlockSpec(s, lambda i: (0,) * len(s))
    body, ex_in, ex_out, ex_shape, ex_scratch = _with_exchange(body, 5, 4, carry, nb)
    outs = pl.pallas_call(
        body, grid=(nb,),
        in_specs=[pl.BlockSpec((BLOCK, D_MODEL), lambda i: (i, 0)),
                  pl.BlockSpec((BLOCK, D_MODEL), lambda i: (jnp.minimum(i + 1, nb - 1), 0)),
                  pl.BlockSpec((BLOCK, 2 * D_MODEL), lambda i: (i, 0)),
                  pl.BlockSpec((BLOCK, 2 * D_MODEL), lambda i: (jnp.maximum(i - 1, 0), 0)),
                  full((kpad, D_MODEL))] + ex_in,
        out_specs=[pl.BlockSpec((BLOCK, 2 * D_MODEL), lambda i: (i, 0)), full((kpad, D_MODEL)),
                   full((1, D_MODEL)), full((1, 2 * D_MODEL))] + ex_out,
        out_shape=[jax.ShapeDtypeStruct((m, 2 * D_MODEL), MXU_DTYPE), jax.ShapeDtypeStruct((kpad, D_MODEL), F32),
                   jax.ShapeDtypeStruct((1, D_MODEL), F32), jax.ShapeDtypeStruct((1, 2 * D_MODEL), F32)] + ex_shape,
        scratch_shapes=ex_scratch, name=name, compiler_params=_params("arbitrary"),
    )(dc, dc, v, v, dw_w, *carry)
    return outs[:4], outs[4:]


def _row(v, width=None):
    v = v.reshape(1, -1).astype(F32)
    if width is not None and v.shape[1] < width:
        v = jnp.pad(v, ((0, 0), (0, width - v.shape[1])))
    return v


def _w_in_to_kernel(w):
    pad = jnp.zeros((w.shape[0], PROJ_W - COL_DT - SSM_HEADS), w.dtype)
    return jnp.concatenate([w[:, 768:1792], w[:, 1792:2816], w[:, 0:512], w[:, 2816:3072], w[:, 512:640],
                            w[:, 640:768], w[:, 3072:3088], pad], axis=1)


def _w_in_from_kernel(g):
    return jnp.concatenate([g[:, COL_Q:COL_Q + Q_W], g[:, COL_K:COL_K + KV_W], g[:, COL_V:COL_V + KV_W],
                            g[:, COL_Z:COL_Z + SSM_INNER], g[:, COL_XS:COL_XS + SSM_INNER],
                            g[:, COL_BC:COL_BC + 2 * LANE], g[:, COL_DT:COL_DT + SSM_HEADS]], axis=1)


def even_fwd(h, p, gather_att=(), gather_ssd=()):
    u = rms_fwd(h, p["norm"])
    proj = matmul(u, p["w_in"], name="mm_proj")
    att, got_att = attention_fwd(proj, p["q_norm"], p["k_norm"], p["sinks"], gather=list(gather_att))
    if p["w_out"] is None:
        p["w_out"] = got_att[0]
    (mix, pre_x, pre_bc, dt, y_pre, states), got_ssd = ssd_fwd(
        proj, att, p["cw_x"], p["cw_bc"], p["cb_x"], p["cb_bc"], p["dt_bias"], p["a_log"], p["d_exp"], p["ssm_norm"],
        gather=list(gather_ssd))
    out = matmul(mix, p["w_out"], b_kind="rowshard", layer=p["layer"], epilogue="resid", extra=h, name="mm_mix_out")
    return out, (h, u, proj, mix, pre_x, pre_bc, dt, y_pre, states), got_att, got_ssd


def even_bwd(dh, p, saved, carry_att=(), carry_ssd=(), send_w_in=False):
    h, u, proj, mix, pre_x, pre_bc, dt, y_pre, states = saved
    dmix = matmul(dh, p["w_out"], b_kind="rowshard", layer=p["layer"], trans_b=True, name="mm_dmix")
    dw_out = matmul_tn(mix, dh, ti=512, tn=D_MODEL, out_dtype=GRAD_WIRE_DTYPE, name="mm_dw_out")
    dw_out = dw_out.reshape(N_DEV, MIX_W // N_DEV, D_MODEL)
    (dq, dk, dv, dqw, dkw, dsk), got_att = attention_bwd(proj, dmix, p["q_norm"], p["k_norm"], p["sinks"],
                                                         carry=list(carry_att))
    (dproj, dcwx, dcwbc, dcbx, dcbbc, ddtb, dalog, dd, dnw), got_ssd = ssd_bwd(
        proj, dmix, dq, dk, dv, pre_x, pre_bc, dt, y_pre, states, p["cw_x"], p["cw_bc"], p["dt_bias"], p["a_log"],
        p["d_exp"], p["ssm_norm"], carry=[dw_out] + list(carry_ssd))
    dw_in = matmul_tn(u, dproj, ti=512, tn=PROJ_W, name="mm_dw_in")
    dw_in = _to_shards(_w_in_from_kernel(dw_in), 1).astype(GRAD_WIRE_DTYPE)
    fused = dict(trans_b=True, epilogue="rms_bwd", bias=p["norm"], extra=h, residual=dh, name="mm_du_in")
    if send_w_in:
        (dh_in, dg), (dw_in,) = matmul(dproj, p["w_in"], carry=[dw_in], **fused)
    else:
        dh_in, dg = matmul(dproj, p["w_in"], **fused)
    grads = dict(norm=dg, w_in=dw_in, cw_x=dcwx, cw_bc=dcwbc, cb_x=dcbx, cb_bc=dcbbc, dt_bias=ddtb,
                 a_log=dalog, d_skip=dd, ssm_norm=dnw, q_norm=dqw, k_norm=dkw, sinks=dsk)
    return dh_in, grads, got_att, got_ssd


def conf_fwd(h, p, gather=()):
    v, u = mlp_up(h, p["norm"], p["pw1_w"], p["layer"], bias=p["pw1_b"], relu2=False, out_dtype=F32, name="mm_pw1")
    (c, s), got = conformer_mid_fwd(v, p["dw_w"], p["dw_b"], p["ln_g"], p["ln_b"], gather=list(gather))
    out = matmul(s, p["pw2_w"], b_kind="rowshard", layer=p["layer"], bias=p["pw2_b"], epilogue="resid", extra=h,
                 name="mm_pw2")
    return out, (h, u, v, c, s), got


def conf_bwd(dh, p, saved, carry=()):
    h, u, v, c, s = saved
    dpw2_b = col_sum(dh)
    ds = matmul(dh, p["pw2_w"], b_kind="rowshard", layer=p["layer"], trans_b=True, name="mm_ds")
    dpw2_w = matmul_tn(s, dh, ti=D_MODEL, tn=D_MODEL, out_dtype=GRAD_WIRE_DTYPE, name="mm_dpw2")
    dpw2_w = dpw2_w.reshape(N_DEV, D_MODEL // N_DEV, D_MODEL)
    dc, dln_g, dln_b = conformer_ln_bwd(ds, c, p["ln_g"], p["ln_b"])
    (dv, ddw_w, ddw_b, dpw1_b), got = conformer_conv_bwd(dc, v, p["dw_w"], carry=[dpw2_w] + list(carry))
    dpw1_w = mlp_dw_up(u, dv, name="mm_dpw1")
    dh_in, dg = mlp_du_rms_bwd(dv, p["pw1_w"], p["layer"], h, p["norm"], dh, name="mm_du_pw1")
    grads = dict(norm=dg, pw1_w=dpw1_w, pw1_b=dpw1_b, dw_w=ddw_w, dw_b=ddw_b, ln_g=dln_g, ln_b=dln_b, pw2_b=dpw2_b)
    return dh_in, grads, got


def mlp_fwd(h, p):
    act, u = mlp_up(h, p["norm"], p["w_up"], p["layer"])
    out = matmul(act, p["w_down"], b_kind="rowshard", layer=p["layer"], epilogue="resid", extra=h, name="mm_down")
    return out, (h, u, act)


def mlp_bwd(dh, p, saved):
    h, u, act = saved
    da = mlp_dact(dh, p["w_down"], act, p["layer"])
    dw_down = mlp_dw_down(act, dh).reshape(N_DEV, FF_BLOCK, D_MODEL)
    dw_up = mlp_dw_up(u, da)
    dh_in, dg = mlp_du_rms_bwd(da, p["w_up"], p["layer"], h, p["norm"], dh)
    return dh_in, dict(norm=dg, w_up=dw_up, w_down=dw_down)


def local_step(x, target, w, shards, first):
    n_even, n_odd = (DEPTH + 1) // 2, DEPTH // 2
    h = jnp.concatenate([jnp.zeros((FRONT_PAD, D_MODEL), F32), w["meta_tokens"].astype(F32), x], axis=0)
    even_p, odd_p, mlp_p = [None] * n_even, [None] * n_odd, [None] * DEPTH

    def even_params(i, g):
        cw = w["ssm_conv_w"][i]
        return dict(
            layer=0, norm=_row(w["mix_norm_even"][i]), w_in=_w_in_to_kernel(_from_shards(g[0][:, 0], 1)), w_out=g[1],
            cw_x=cw[:, :SSM_INNER], cw_bc=cw[:, SSM_INNER:], cb_x=_row(w["ssm_conv_b"][i][:SSM_INNER]),
            cb_bc=_row(w["ssm_conv_b"][i][SSM_INNER:]), dt_bias=_row(w["dt_bias"][i], LANE),
            a_log=_row(w["a_log"][i], LANE), d_exp=_row(jnp.repeat(w["d_skip"][i], SSM_HEAD_DIM)),
            ssm_norm=_row(w["ssm_norm_w"][i]), q_norm=_row(jnp.tile(w["q_norm"][i], ATT_HEADS)),
            k_norm=_row(jnp.tile(w["k_norm"][i], ATT_KV_HEADS)), sinks=w["sinks"][i].astype(F32))

    def odd_params(i, g):
        return dict(
            layer=0, norm=_row(w["mix_norm_odd"][i]), pw1_w=g[0], pw1_b=_row(w["pw1_b"][i]),
            dw_w=jnp.pad(w["dw_w"][i], ((0, CONF_HALO - CONF_KERNEL), (0, 0))), dw_b=_row(w["dw_b"][i]),
            ln_g=_row(w["ln_g"][i]), ln_b=_row(w["ln_b"][i]), pw2_w=g[1], pw2_b=_row(w["pw2_b"][i]))

    gathered = {}
    early = {}
    tape = []
    for layer in range(DEPTH):
        i = layer // 2
        nxt = ()
        if layer + 1 < DEPTH and layer + 1 not in gathered:
            nxt = shards[layer + 1][len(early.get(layer + 1, ())):]
        if layer == 0:
            even_p[0] = even_params(0, [first[0], None])
            ahead = shards[2][:2] if DEPTH > 2 else []
            h, saved, got_att, gathered[1] = even_fwd(h, even_p[0], gather_att=shards[0][1:] + ahead, gather_ssd=nxt)
            mlp_w = got_att[1:3]
            if ahead:
                early[2] = got_att[3:]
        else:
            g = gathered.pop(layer)
            mlp_w = g[2:]
            if layer % 2 == 0:
                even_p[i] = even_params(i, g[:2])
                h, saved, got, _ = even_fwd(h, even_p[i], gather_att=nxt)
            else:
                odd_p[i] = odd_params(i, g[:2])
                h, saved, got = conf_fwd(h, odd_p[i], gather=nxt)
            if nxt:
                gathered[layer + 1] = list(early.pop(layer + 1, ())) + list(got)
        tape.append(saved)
        mlp_p[layer] = dict(layer=0, norm=_row(w["mlp_norm"][layer]), w_up=mlp_w[0], w_down=mlp_w[1])
        h, saved = mlp_fwd(h, mlp_p[layer])
        tape.append(saved)
    dh, loss_row = loss_fwd_bwd(h, target)

    ge = [None] * n_even
    go = [None] * n_odd
    gm = [None] * DEPTH
    received = {n: [None] * shape[0] for n, shape, _ in PARAMS if n in MATMUL_WEIGHTS}
    pending = []

    def store(tags, arrays):
        for (n, l), a in zip(tags, arrays):
            received[n][l] = a

    for layer in reversed(range(DEPTH)):
        i = layer // 2
        dh, gm[layer] = mlp_bwd(dh, mlp_p[layer], tape.pop())
        mlp_tags = [("w_up", layer), ("w_down", layer)]
        mlp_parts = [gm[layer]["w_up"], gm[layer]["w_down"]]
        if layer % 2 == 0:
            riders, pending = pending, []
            dh, ge[i], got_att, got_ssd = even_bwd(dh, even_p[i], tape.pop(), carry_att=mlp_parts,
                                                   carry_ssd=[a for _, _, a in riders], send_w_in=layer == 0)
            store(mlp_tags, got_att)
            store([("w_out", i)] + [(n, l) for n, l, _ in riders], got_ssd)
            if layer == 0:
                store([("w_in", i)], [ge[i]["w_in"]])
            else:
                pending.append(("w_in", i, ge[i]["w_in"]))
        else:
            dh, go[i], got = conf_bwd(dh, odd_p[i], tape.pop(), carry=mlp_parts)
            store([("pw2_w", i)] + mlp_tags, got)
            pending.append(("pw1_w", i, go[i]["pw1_w"]))

    stack = lambda gs, f: jnp.stack([f(g) for g in gs])
    grads = dict(
        meta_tokens=dh[FRONT_PAD:BLOCK],
        mix_norm_even=stack(ge, lambda g: g["norm"][0]),
        ssm_conv_w=stack(ge, lambda g: jnp.concatenate([g["cw_x"], g["cw_bc"]], axis=1)),
        ssm_conv_b=stack(ge, lambda g: jnp.concatenate([g["cb_x"][0], g["cb_bc"][0]])),
        dt_bias=stack(ge, lambda g: g["dt_bias"][0, :SSM_HEADS]),
        a_log=stack(ge, lambda g: g["a_log"][0, :SSM_HEADS]),
        d_skip=stack(ge, lambda g: g["d_skip"][0, :SSM_HEADS]),
        ssm_norm_w=stack(ge, lambda g: g["ssm_norm"][0]),
        q_norm=stack(ge, lambda g: g["q_norm"][0, :HEAD_DIM]),
        k_norm=stack(ge, lambda g: g["k_norm"][0, :HEAD_DIM]),
        sinks=stack(ge, lambda g: g["sinks"][0, :ATT_HEADS]),
        mix_norm_odd=stack(go, lambda g: g["norm"][0]),
        pw1_b=stack(go, lambda g: g["pw1_b"][0]),
        dw_w=stack(go, lambda g: g["dw_w"][:CONF_KERNEL]),
        dw_b=stack(go, lambda g: g["dw_b"][0]),
        ln_g=stack(go, lambda g: g["ln_g"][0]),
        ln_b=stack(go, lambda g: g["ln_b"][0]),
        pw2_b=stack(go, lambda g: g["pw2_b"][0]),
        mlp_norm=stack(gm, lambda g: g["norm"][0]),
    )
    return loss_row[0, 0], dh[BLOCK:], grads, received, pending


PARAMS = (
    ("meta_tokens", (16, 1024), 1), ("mix_norm_even", (2, 1024), None), ("w_in", (2, 1024, 3088), 2),
    ("ssm_conv_w", (2, 4, 1280), 2), ("ssm_conv_b", (2, 1280), None), ("dt_bias", (2, 16), None),
    ("a_log", (2, 16), None), ("d_skip", (2, 16), None), ("ssm_norm_w", (2, 1024), None), ("q_norm", (2, 64), None),
    ("k_norm", (2, 64), None), ("sinks", (2, 8), None), ("w_out", (2, 1536, 1024), 1), ("mix_norm_odd", (2, 1024), 1),
    ("pw1_w", (2, 1024, 2048), 2), ("pw1_b", (2, 2048), 1), ("dw_w", (2, 31, 1024), 2), ("dw_b", (2, 1024), 1),
    ("ln_g", (2, 1024), 1), ("ln_b", (2, 1024), 1), ("pw2_w", (2, 1024, 1024), 1), ("pw2_b", (2, 1024), 1),
    ("mlp_norm", (4, 1024), None), ("w_up", (4, 1024, 4096), 2), ("w_down", (4, 4096, 1024), 1),
)
MATMUL_WEIGHTS = ("w_in", "w_out", "pw1_w", "pw2_w", "w_up", "w_down")
PACK_ROW_ALIGN = 16 * PACK_W


def _block_shape(shape, axis):
    if axis is None:
        return tuple(shape)
    return tuple(s // N_DEV if a == axis else s for a, s in enumerate(shape))


def _numel(shape):
    return math.prod(shape)


def _pack(arrays, dtype):
    flat = jnp.concatenate([a.reshape(-1).astype(dtype) for a in arrays])
    n = flat.shape[0]
    padded = -(-n // PACK_ROW_ALIGN) * PACK_ROW_ALIGN
    return jnp.pad(flat, (0, padded - n)).reshape(-1, PACK_W)


def _pack_rows(arrays_by_dev, dtype):
    flat = jnp.concatenate([a.reshape(N_DEV, -1).astype(dtype) for a in arrays_by_dev], axis=1)
    n = flat.shape[1]
    padded = -(-n // PACK_ROW_ALIGN) * PACK_ROW_ALIGN
    return jnp.pad(flat, ((0, 0), (0, padded - n))).reshape(N_DEV, -1, PACK_W)


def _to_shards(full, axis):
    shape = full.shape
    split = full.reshape(shape[:axis] + (N_DEV, shape[axis] // N_DEV) + shape[axis + 1:])
    return jnp.moveaxis(split, axis, 0)


def _from_shards(blocks, axis):
    moved = jnp.moveaxis(blocks, 0, axis)
    shape = moved.shape
    return moved.reshape(shape[:axis] + (shape[axis] * shape[axis + 1],) + shape[axis + 2:])


_MESH = pl.DeviceIdType.MESH
_ANY = pl.BlockSpec(memory_space=pl.ANY)


def _mesh_place():
    x, y, c = lax.axis_index("x"), lax.axis_index("y"), lax.axis_index("c")
    return x, y, c


def _peer(x, y, c, rel):
    dx, dy, dc = (rel >> 2) & 1, (rel >> 1) & 1, rel & 1
    return (x ^ dx if dx else x, y ^ dy if dy else y, c ^ dc if dc else c)


def _dev_index(x, y, c):
    return 4 * x + 2 * y + c


def all_gather_weights(bigs, small):
    nt = len(bigs)

    def body(*refs):
        big_refs, small_ref = refs[:nt], refs[nt]
        big_outs, small_out = refs[nt + 1:2 * nt + 1], refs[2 * nt + 1]
        send_sems, recv_sems, small_send, small_recv, local_sems = refs[2 * nt + 2:]
        x, y, c = _mesh_place()
        me = (x, y, c)
        sibling = (x, y, 1 - c)
        chips = [(1 - x, y), (x, 1 - y), (1 - x, 1 - y)]

        def big_copy(t, k, block, to, from_input=False):
            dst = big_outs[t].at[_dev_index(*block)]
            return pltpu.make_async_remote_copy(src_ref=big_refs[t] if from_input else dst, dst_ref=dst,
                                                send_sem=send_sems.at[t, k], recv_sem=recv_sems.at[t, k],
                                                device_id=to, device_id_type=_MESH)

        def small_copy(rel, block, to):
            return pltpu.make_async_remote_copy(src_ref=small_ref, dst_ref=small_out.at[_dev_index(*block)],
                                                send_sem=small_send.at[rel - 1], recv_sem=small_recv.at[rel - 1],
                                                device_id=to, device_id_type=_MESH)

        mine = [pltpu.make_async_copy(big_refs[t], big_outs[t].at[_dev_index(*me)], local_sems.at[t]) for t in range(nt)]
        mine.append(pltpu.make_async_copy(small_ref, small_out.at[_dev_index(*me)], local_sems.at[nt]))
        for cp in mine:
            cp.start()
        first = []
        for t in range(nt):
            first.append(big_copy(t, 0, me, sibling, from_input=True))
            first += [big_copy(t, 1 + j, me, (*chip, c), from_input=True) for j, chip in enumerate(chips)]
        for cp in first:
            cp.start()
        smalls = [small_copy(rel, me, _peer(x, y, c, rel)) for rel in range(1, N_DEV)]
        for cp in smalls:
            cp.start()
        passed = []
        for j, chip in enumerate(chips):
            for t in range(nt):
                big_copy(t, 1 + j, (*chip, c), me).wait_recv()
                fwd = big_copy(t, 4 + j, (*chip, c), sibling)
                fwd.start()
                passed.append(fwd)
        for t in range(nt):
            big_copy(t, 0, sibling, me).wait_recv()
            for j, chip in enumerate(chips):
                big_copy(t, 4 + j, (*chip, 1 - c), me).wait_recv()
        for rel in range(1, N_DEV):
            small_copy(rel, _peer(x, y, c, rel), me).wait_recv()
        for cp in first + passed + smalls:
            cp.wait_send()
        for cp in mine:
            cp.wait()

    return pl.pallas_call(
        body, in_specs=[_ANY] * (nt + 1), out_specs=[_ANY] * (nt + 1),
        out_shape=[jax.ShapeDtypeStruct((N_DEV,) + b.shape, b.dtype) for b in bigs]
        + [jax.ShapeDtypeStruct((N_DEV,) + small.shape, small.dtype)],
        scratch_shapes=[pltpu.SemaphoreType.DMA((nt, N_DEV - 1)), pltpu.SemaphoreType.DMA((nt, N_DEV - 1)),
                        pltpu.SemaphoreType.DMA((N_DEV - 1,)), pltpu.SemaphoreType.DMA((N_DEV - 1,)),
                        pltpu.SemaphoreType.DMA((nt + 1,))],
        name="all_gather_weights",
    )(*bigs, small)


def _gather_copies(in_refs, out_refs, send_sems, recv_sems, local_sems):
    x, y, c = _mesh_place()
    me = (x, y, c)
    sibling = (x, y, 1 - c)
    chips = [(1 - x, y), (x, 1 - y), (1 - x, 1 - y)]
    nt = len(in_refs)

    def copy(t, k, block, to, from_input=False):
        dst = out_refs[t].at[_dev_index(*block)]
        return pltpu.make_async_remote_copy(src_ref=in_refs[t] if from_input else dst, dst_ref=dst,
                                            send_sem=send_sems.at[t, k], recv_sem=recv_sems.at[t, k],
                                            device_id=to, device_id_type=_MESH)

    mine = [pltpu.make_async_copy(in_refs[t], out_refs[t].at[_dev_index(*me)], local_sems.at[t]) for t in range(nt)]
    first, landed, forward, last = [], [], [], []
    for t in range(nt):
        first.append(copy(t, 0, me, sibling, from_input=True))
        last.append(copy(t, 0, sibling, me))
        for j, chip in enumerate(chips):
            first.append(copy(t, 1 + j, me, (*chip, c), from_input=True))
            landed.append(copy(t, 1 + j, (*chip, c), me))
            forward.append(copy(t, 4 + j, (*chip, c), sibling))
            last.append(copy(t, 4 + j, (*chip, 1 - c), me))
    return mine, first, landed, forward, last


GATHER_FORWARD_LEAD = 8


def _with_gather(body, n_in, n_out, shards, steps):
    n = len(shards)
    if n == 0:
        return body, [], [], [], []
    fwd_step = max(steps - 1 - GATHER_FORWARD_LEAD, 0)

    def wrapped(*refs):
        ins, g_in = refs[:n_in], refs[n_in:n_in + n]
        outs, g_out = refs[n_in + n:n_in + n + n_out], refs[n_in + n + n_out:n_in + 2 * n + n_out]
        scratch = refs[n_in + 2 * n + n_out:len(refs) - 3]
        sems = refs[len(refs) - 3:]
        i = pl.program_id(0)

        @pl.when(i == 0)
        def _():
            mine, first, _, _, _ = _gather_copies(g_in, g_out, *sems)
            for cp in mine + first:
                cp.start()

        @pl.when(i == fwd_step)
        def _():
            _, _, landed, forward, _ = _gather_copies(g_in, g_out, *sems)
            for arrived, onward in zip(landed, forward):
                arrived.wait_recv()
                onward.start()

        body(*ins, *outs, *scratch)

        @pl.when(i == steps - 1)
        def _():
            mine, first, _, forward, last = _gather_copies(g_in, g_out, *sems)
            for cp in last:
                cp.wait_recv()
            for cp in first + forward:
                cp.wait_send()
            for cp in mine:
                cp.wait()

    return (wrapped, [_ANY] * n, [_ANY] * n, [jax.ShapeDtypeStruct((N_DEV,) + a.shape, a.dtype) for a in shards],
            [pltpu.SemaphoreType.DMA((n, N_DEV - 1)), pltpu.SemaphoreType.DMA((n, N_DEV - 1)),
             pltpu.SemaphoreType.DMA((n,))])


def _exchange_copies(in_refs, out_refs, send_sems, recv_sems, local_sems):
    x, y, c = _mesh_place()
    me = _dev_index(x, y, c)
    mine, sends, arrivals = [], [], []
    for p, (src, dst) in enumerate(zip(in_refs, out_refs)):
        mine.append(pltpu.make_async_copy(src.at[me], dst.at[me], local_sems.at[p]))
        for rel in range(1, N_DEV):
            peer = _peer(x, y, c, rel)
            there = _dev_index(*peer)
            sems = dict(send_sem=send_sems.at[rel - 1, p], recv_sem=recv_sems.at[rel - 1, p], device_id=peer,
                        device_id_type=_MESH)
            sends.append(pltpu.make_async_remote_copy(src_ref=src.at[there], dst_ref=dst.at[me], **sems))
            arrivals.append(pltpu.make_async_remote_copy(src_ref=src.at[me], dst_ref=dst.at[there], **sems))
    return mine, sends, arrivals


def _with_exchange(body, n_in, n_out, carry, grid):
    n = len(carry)
    if n == 0:
        return body, [], [], [], []
    grid = (grid,) if isinstance(grid, int) else tuple(grid)

    def at_step(corner):
        hit = pl.program_id(0) == corner[0]
        for axis in range(1, len(grid)):
            hit = jnp.logical_and(hit, pl.program_id(axis) == corner[axis])
        return hit

    def wrapped(*refs):
        ins, ex_in = refs[:n_in], refs[n_in:n_in + n]
        outs, ex_out = refs[n_in + n:n_in + n + n_out], refs[n_in + n + n_out:n_in + 2 * n + n_out]
        scratch = refs[n_in + 2 * n + n_out:len(refs) - 3]
        send_sems, recv_sems, local_sems = refs[len(refs) - 3:]

        @pl.when(at_step([0] * len(grid)))
        def _():
            mine, sends, _ = _exchange_copies(ex_in, ex_out, send_sems, recv_sems, local_sems)
            for cp in mine + sends:
                cp.start()

        body(*ins, *outs, *scratch)

        @pl.when(at_step([g - 1 for g in grid]))
        def _():
            mine, sends, arrivals = _exchange_copies(ex_in, ex_out, send_sems, recv_sems, local_sems)
            for cp in arrivals:
                cp.wait_recv()
            for cp in sends:
                cp.wait_send()
            for cp in mine:
                cp.wait()

    return (wrapped, [_ANY] * n, [_ANY] * n, [jax.ShapeDtypeStruct(a.shape, a.dtype) for a in carry],
            [pltpu.SemaphoreType.DMA((N_DEV - 1, n)), pltpu.SemaphoreType.DMA((N_DEV - 1, n)),
             pltpu.SemaphoreType.DMA((n,))])


def reduce_adamw(parts, w, m, v, tr, carry=()):
    nl, r, cols = w.shape
    assert len(parts) == nl

    def body(*refs):
        p_refs = refs[:nl]
        w_ref, m_ref, v_ref, g_ref, d_ref, nm_ref, nv_ref, g_acc = refs[nl:]
        layer = pl.program_id(0)
        for l in range(nl):
            @pl.when(layer == l)
            def _(l=l):
                g = p_refs[l][0].astype(F32)
                for d in range(1, N_DEV):
                    g = g + p_refs[l][d].astype(F32)
                g_acc[...] = g

        g = g_acc[...]
        g_ref[...] = g
        nm = ADAM_B1 * m_ref[...] + (1.0 - ADAM_B1) * g
        nv = ADAM_B2 * v_ref[...] + (1.0 - ADAM_B2) * (g * g)
        nm_ref[...] = nm
        nv_ref[...] = nv
        m_hat = nm / (1.0 - ADAM_B1 ** ADAM_STEP)
        v_hat = nv / (1.0 - ADAM_B2 ** ADAM_STEP)
        d_ref[...] = -ADAM_LR * (m_hat / (jnp.sqrt(v_hat) + ADAM_EPS) + ADAM_WD * w_ref[...])

    row = pl.BlockSpec((None, tr, cols), lambda l, i: (l, i, 0))

    def part_spec(own):
        def index(l, i):
            return (0, jnp.where(l == own, i, jnp.where(l < own, 0, r // tr - 1)), 0)
        return pl.BlockSpec((N_DEV, tr, cols), index)

    grid = (nl, r // tr)
    body, ex_in, ex_out, ex_shape, ex_scratch = _with_exchange(body, nl + 3, 4, carry, grid)
    outs = pl.pallas_call(
        body, grid=grid,
        in_specs=[part_spec(l) for l in range(nl)] + [row, row, row] + ex_in,
        out_specs=[row, row, row, row] + ex_out,
        out_shape=[jax.ShapeDtypeStruct((nl, r, cols), F32)] * 4 + ex_shape,
        scratch_shapes=[pltpu.VMEM((tr, cols), F32)] + ex_scratch,
        name="reduce_adamw", compiler_params=_params("arbitrary", "arbitrary"),
    )(*parts, w, m, v, *carry)
    return outs[:4], outs[4:]


ADAMW_TILE_BYTES = 1 << 19


def _adamw_tile(rows, cols):
    lanes = -(-cols // LANE) * LANE
    best = None
    for tr in range(16, rows + 1, 16):
        if rows % tr == 0 and tr * lanes * 4 <= ADAMW_TILE_BYTES:
            best = tr
    if best is None:
        raise ValueError((rows, cols))
    return best


def kernel(x, meta_tokens, mix_norm_even, w_in, ssm_conv_w, ssm_conv_b, dt_bias, a_log, d_skip, ssm_norm_w, q_norm, k_norm, sinks, w_out, mix_norm_odd, pw1_w, pw1_b, dw_w, dw_b, ln_g, ln_b, pw2_w, pw2_b, mlp_norm, w_up, w_down, loss_target, m_meta_tokens, m_mix_norm_even, m_w_in, m_ssm_conv_w, m_ssm_conv_b, m_dt_bias, m_a_log, m_d_skip, m_ssm_norm_w, m_q_norm, m_k_norm, m_sinks, m_w_out, m_mix_norm_odd, m_pw1_w, m_pw1_b, m_dw_w, m_dw_b, m_ln_g, m_ln_b, m_pw2_w, m_pw2_b, m_mlp_norm, m_w_up, m_w_down, v_meta_tokens, v_mix_norm_even, v_w_in, v_ssm_conv_w, v_ssm_conv_b, v_dt_bias, v_a_log, v_d_skip, v_ssm_norm_w, v_q_norm, v_k_norm, v_sinks, v_w_out, v_mix_norm_odd, v_pw1_w, v_pw1_b, v_dw_w, v_dw_b, v_ln_g, v_ln_b, v_pw2_w, v_pw2_b, v_mlp_norm, v_w_up, v_w_down):
    names = [p[0] for p in PARAMS]
    w_loc = dict(zip(names, (meta_tokens, mix_norm_even, w_in, ssm_conv_w, ssm_conv_b, dt_bias, a_log, d_skip, ssm_norm_w, q_norm, k_norm, sinks, w_out, mix_norm_odd, pw1_w, pw1_b, dw_w, dw_b, ln_g, ln_b, pw2_w, pw2_b, mlp_norm, w_up, w_down)))
    m_loc = dict(zip(names, (m_meta_tokens, m_mix_norm_even, m_w_in, m_ssm_conv_w, m_ssm_conv_b, m_dt_bias, m_a_log, m_d_skip, m_ssm_norm_w, m_q_norm, m_k_norm, m_sinks, m_w_out, m_mix_norm_odd, m_pw1_w, m_pw1_b, m_dw_w, m_dw_b, m_ln_g, m_ln_b, m_pw2_w, m_pw2_b, m_mlp_norm, m_w_up, m_w_down)))
    v_loc = dict(zip(names, (v_meta_tokens, v_mix_norm_even, v_w_in, v_ssm_conv_w, v_ssm_conv_b, v_dt_bias, v_a_log, v_d_skip, v_ssm_norm_w, v_q_norm, v_k_norm, v_sinks, v_w_out, v_mix_norm_odd, v_pw1_w, v_pw1_b, v_dw_w, v_dw_b, v_ln_g, v_ln_b, v_pw2_w, v_pw2_b, v_mlp_norm, v_w_up, v_w_down)))
    small_sharded = [p for p in PARAMS if p[2] is not None and p[0] not in MATMUL_WEIGHTS]
    replicated = [p for p in PARAMS if p[2] is None]
    small_list = small_sharded + replicated

    def layer_shards(layer):
        i = layer // 2
        mixer = ("w_in", "w_out") if layer % 2 == 0 else ("pw1_w", "pw2_w")
        return [w_loc[n][i:i + 1].astype(MXU_DTYPE) for n in mixer] + [
            w_loc[n][layer:layer + 1].astype(MXU_DTYPE) for n in ("w_up", "w_down")]

    shards = [layer_shards(layer) for layer in range(DEPTH)]
    gathered = all_gather_weights(shards[0][:1], _pack([w_loc[n] for n, _, _ in small_sharded], F32))
    w_full = {n: w_loc[n] for n, _, _ in replicated}
    flat = gathered[-1].reshape(N_DEV, -1)
    off = 0
    for n, shape, axis in small_sharded:
        blk = _block_shape(shape, axis)
        w_full[n] = _from_shards(flat[:, off:off + _numel(blk)].reshape((N_DEV,) + blk), axis)
        off += _numel(blk)

    loss_local, grad_x, g_full, received, pending = local_step(x[0], loss_target[0], w_full, shards, gathered[:-1])
    loss = lax.psum(loss_local, ("x", "y", "c"))

    by_dev = [_to_shards(g_full[n], axis) for n, _, axis in small_sharded]
    by_dev += [jnp.broadcast_to(g_full[n][None], (N_DEV,) + tuple(shape)) for n, shape, _ in replicated]
    assert not pending
    order = sorted(MATMUL_WEIGHTS, key=lambda n: _numel(w_loc[n].shape))
    out = {}
    for n in order:
        nl, r, cols = w_loc[n].shape
        out[n], got = reduce_adamw(received[n], w_loc[n], m_loc[n], v_loc[n], _adamw_tile(r, cols),
                                   carry=[_pack_rows(by_dev, F32)] if n == order[0] else ())
        if n == order[0]:
            small_parts = got[0]
    pk = lambda d: _pack([d[n] for n, _, _ in small_list], F32)[None]
    rows = small_parts.shape[1]
    small_out, _ = reduce_adamw([small_parts], pk(w_loc), pk(m_loc), pk(v_loc), _adamw_tile(rows, PACK_W))
    flats = [buf.reshape(-1) for buf in small_out]
    off = 0
    for n, shape, axis in small_list:
        blk = _block_shape(shape, axis)
        out[n] = tuple(f[off:off + _numel(blk)].reshape(blk) for f in flats)
        off += _numel(blk)
    return (loss, grad_x[None], *[out[n][0] for n in names], *[out[n][1] for n in names],
            *[out[n][2] for n in names], *[out[n][3] for n in names])
```
